```python
import math
import jax, jax.numpy as jnp
from jax import lax
import numpy as np

D_MODEL = 1024
BATCH = 8
SEQ = 2048
DEPTH = 2

N_MIXERS = 2
BRANCH_WIDTH = 2 * D_MODEL
XQ_WIDTH = BRANCH_WIDTH // 4
PRIMARY_WIDTH = BRANCH_WIDTH - XQ_WIDTH
MEM_LEN = 256
X_HEADS = 4
X_HEAD_DIM = XQ_WIDTH // X_HEADS
S5_GROUP_CH = 16
S5_GROUPS = PRIMARY_WIDTH // S5_GROUP_CH
S5_STATE = 64
S5_STEP_MIN = 1e-3
S5_STEP_MAX = 1e-1
MLA_NOPE = 128
MLA_ROPE = 64
MLA_V = 128
MLA_HEADS = PRIMARY_WIDTH // MLA_V
MLA_Q_LORA = D_MODEL // 2
MLA_KV_LORA = D_MODEL // 4
ROPE_THETA = 10000.0
Q_BLOCK = 128
EPS = 1e-6
N_S5 = (DEPTH + 1) // 2
N_MLA = DEPTH // 2
S5_IN_WIDTH = PRIMARY_WIDTH + XQ_WIDTH + BRANCH_WIDTH
MLA_IN_WIDTH = MLA_Q_LORA + MLA_KV_LORA + MLA_ROPE + XQ_WIDTH + BRANCH_WIDTH

kernel_name = "hybrid_s5_mla_memory_block"


def rms_norm(x, g):
    xf = x.astype(jnp.float32)
    y = xf * lax.rsqrt(jnp.mean(xf * xf, axis=-1, keepdims=True) + EPS)
    return (y * g.astype(jnp.float32)).astype(x.dtype)


def rotary_tables(positions):
    half = MLA_ROPE // 2
    inv_freq = ROPE_THETA ** (-jnp.arange(half, dtype=jnp.float32) / half)
    ang = positions.astype(jnp.float32)[:, :, None, None] * inv_freq
    return jnp.cos(ang), jnp.sin(ang)


def rotary(x, cos, sin):
    x1, x2 = jnp.split(x.astype(jnp.float32), 2, axis=-1)
    return jnp.concatenate([x1 * cos - x2 * sin, x1 * sin + x2 * cos], axis=-1).astype(x.dtype)


def _ssm_combine(left, right):
    a_l, b_l = left
    a_r, b_r = right
    return a_l * a_r, a_r * b_l + b_r


def s5_mix(u, lam_re, lam_im, log_step, b_re, b_im, c_re, c_im, d):
    bsz, seq, _ = u.shape
    f32 = jnp.float32
    uf = u.astype(f32).reshape(bsz, seq, S5_GROUPS, S5_GROUP_CH)
    lam = lax.complex(lam_re.astype(f32), lam_im.astype(f32))
    step = jnp.exp(log_step.astype(f32))[:, None]
    a_bar = jnp.exp(lam * step)
    b_mat = lax.complex(b_re.astype(f32), b_im.astype(f32))
    c_mat = lax.complex(c_re.astype(f32), c_im.astype(f32))
    b_bar = ((a_bar - 1.0) / lam)[..., None] * b_mat
    bu = jnp.einsum('blgc,gpc->blgp', uf.astype(jnp.complex64), b_bar)
    a_seq = jnp.broadcast_to(a_bar, (1, seq) + a_bar.shape)
    _, state = lax.associative_scan(_ssm_combine, (a_seq, bu), axis=1)
    y = jnp.einsum('blgp,gcp->blgc', state, c_mat).real + d.astype(f32).reshape(S5_GROUPS, S5_GROUP_CH) * uf
    return y.reshape(bsz, seq, PRIMARY_WIDTH).astype(u.dtype)


def causal_block_attention(q, k, v, scale):
    bsz, seq, heads, dk = q.shape
    dv = v.shape[-1]
    n_blocks = seq // Q_BLOCK
    q_blocks = q.reshape(bsz, n_blocks, Q_BLOCK, heads, dk).transpose(1, 0, 2, 3, 4)
    k_pos = jnp.arange(seq)

    def one_block(args):
        q_blk, blk = args
        s = jnp.einsum('bqhd,bkhd->bhqk', q_blk, k).astype(jnp.float32) * scale
        q_pos = blk * Q_BLOCK + jnp.arange(Q_BLOCK)
        s = jnp.where(k_pos[None, :] <= q_pos[:, None], s, jnp.finfo(jnp.float32).min)
        p = jax.nn.softmax(s, axis=-1).astype(v.dtype)
        return jnp.einsum('bhqk,bkhd->bqhd', p, v)

    out = lax.map(one_block, (q_blocks, jnp.arange(n_blocks)))
    return out.transpose(1, 0, 2, 3, 4).reshape(bsz, seq, heads, dv)


def memory_attention(xq, mem, mem_norm, w_mem_kv, xq_norm, xk_norm):
    bsz, seq, _ = xq.shape
    kv = rms_norm(mem, mem_norm) @ w_mem_kv
    k, v = jnp.split(kv, 2, axis=-1)
    k = rms_norm(k.reshape(bsz, -1, X_HEADS, X_HEAD_DIM), xk_norm)
    v = v.reshape(bsz, -1, X_HEADS, X_HEAD_DIM)
    q = rms_norm(xq.reshape(bsz, seq, X_HEADS, X_HEAD_DIM), xq_norm)
    s = jnp.einsum('blhd,bmhd->bhlm', q, k).astype(jnp.float32) * (X_HEAD_DIM ** -0.5)
    p = jax.nn.softmax(s, axis=-1).astype(v.dtype)
    return jnp.einsum('bhlm,bmhd->blhd', p, v).reshape(bsz, seq, XQ_WIDTH)


def merge_branches(x, mixer_out, xq, gate, mem, w_out, mem_norm, w_mem_kv, xq_norm, xk_norm):
    mem_out = memory_attention(xq, mem, mem_norm, w_mem_kv, xq_norm, xk_norm)
    o = jnp.concatenate([mixer_out, mem_out], axis=-1) * jax.nn.silu(gate)
    return x + o @ w_out


def s5_layer(x, mem, ln, w_in, lam_re, lam_im, log_step, b_re, b_im, c_re, c_im, d, w_glu,
             w_out, mem_norm, w_mem_kv, xq_norm, xk_norm):
    proj = rms_norm(x, ln) @ w_in
    u, xq, gate = jnp.split(proj, [PRIMARY_WIDTH, PRIMARY_WIDTH + XQ_WIDTH], axis=-1)
    y = s5_mix(u, lam_re, lam_im, log_step, b_re, b_im, c_re, c_im, d)
    y_a, y_b = jnp.split(jax.nn.gelu(y) @ w_glu, 2, axis=-1)
    y = y_a * jax.nn.sigmoid(y_b)
    return merge_branches(x, y, xq, gate, mem, w_out, mem_norm, w_mem_kv, xq_norm, xk_norm)


def mla_layer(x, mem, cos, sin, ln, w_in, q_lora_norm, kv_lora_norm, w_uq, w_ukv,
              q_nope_norm, k_nope_norm, q_rope_norm, k_rope_norm,
              w_out, mem_norm, w_mem_kv, xq_norm, xk_norm):
    bsz, seq, _ = x.shape
    proj = rms_norm(x, ln) @ w_in
    o1 = MLA_Q_LORA
    o2 = o1 + MLA_KV_LORA
    o3 = o2 + MLA_ROPE
    o4 = o3 + XQ_WIDTH
    c_q, c_kv, k_rope, xq, gate = jnp.split(proj, [o1, o2, o3, o4], axis=-1)
    q = (rms_norm(c_q, q_lora_norm) @ w_uq).reshape(bsz, seq, MLA_HEADS, MLA_NOPE + MLA_ROPE)
    kv = (rms_norm(c_kv, kv_lora_norm) @ w_ukv).reshape(bsz, seq, MLA_HEADS, MLA_NOPE + MLA_V)
    q_nope, q_rope = q[..., :MLA_NOPE], q[..., MLA_NOPE:]
    k_nope, v = kv[..., :MLA_NOPE], kv[..., MLA_NOPE:]
    q_rope = rotary(rms_norm(q_rope, q_rope_norm), cos, sin)
    k_rope = rotary(rms_norm(k_rope.reshape(bsz, seq, 1, MLA_ROPE), k_rope_norm), cos, sin)
    q_full = jnp.concatenate([rms_norm(q_nope, q_nope_norm), q_rope], axis=-1)
    k_full = jnp.concatenate([rms_norm(k_nope, k_nope_norm),
                              jnp.broadcast_to(k_rope, (bsz, seq, MLA_HEADS, MLA_ROPE))], axis=-1)
    attn = causal_block_attention(q_full, k_full, v, (MLA_NOPE + MLA_ROPE) ** -0.5)
    attn = attn.reshape(bsz, seq, PRIMARY_WIDTH)
    return merge_branches(x, attn, xq, gate, mem, w_out, mem_norm, w_mem_kv, xq_norm, xk_norm)


def setup_inputs(seed: int = 0) -> dict:
    key = jax.random.key(seed)
    k = jax.random.split(key, 32)
    f32 = jnp.float32

    def w(kk, shape, fan_in):
        return jax.random.normal(kk, shape, f32) * (fan_in ** -0.5)

    def gain(kk, shape):
        return 1.0 + 0.02 * jax.random.normal(kk, shape, f32)

    x = jax.random.normal(k[0], (BATCH, SEQ, D_MODEL), f32)
    mem = jax.random.normal(k[1], (BATCH, MEM_LEN, D_MODEL), f32)
    offsets = jax.random.randint(k[2], (BATCH, 1), 0, 4096, dtype=jnp.int32)
    positions = offsets + jnp.arange(SEQ, dtype=jnp.int32)[None, :]

    lam_im_base = math.pi * jnp.arange(S5_STATE, dtype=f32)
    return {
        "x": x,
        "mem": mem,
        "positions": positions,
        "ln_gain": gain(k[3], (DEPTH, D_MODEL)),
        "w_out": w(k[4], (DEPTH, BRANCH_WIDTH, D_MODEL), BRANCH_WIDTH),
        "mem_norm": gain(k[5], (DEPTH, D_MODEL)),
        "w_mem_kv": w(k[6], (DEPTH, D_MODEL, 2 * XQ_WIDTH), D_MODEL),
        "xq_norm": gain(k[7], (DEPTH, X_HEAD_DIM)),
        "xk_norm": gain(k[8], (DEPTH, X_HEAD_DIM)),
        "s5_w_in": w(k[9], (N_S5, D_MODEL, S5_IN_WIDTH), D_MODEL),
        "s5_lambda_re": -0.5 + 0.01 * jax.random.normal(k[10], (N_S5, S5_GROUPS, S5_STATE), f32),
        "s5_lambda_im": lam_im_base + 0.01 * jax.random.normal(k[11], (N_S5, S5_GROUPS, S5_STATE), f32),
        "s5_log_step": jax.random.uniform(k[12], (N_S5, S5_GROUPS), f32,
                                          math.log(S5_STEP_MIN), math.log(S5_STEP_MAX)),
        "s5_b_re": w(k[13], (N_S5, S5_GROUPS, S5_STATE, S5_GROUP_CH), 2 * S5_GROUP_CH),
        "s5_b_im": w(k[14], (N_S5, S5_GROUPS, S5_STATE, S5_GROUP_CH), 2 * S5_GROUP_CH),
        "s5_c_re": w(k[15], (N_S5, S5_GROUPS, S5_GROUP_CH, S5_STATE), S5_STATE),
        "s5_c_im": w(k[16], (N_S5, S5_GROUPS, S5_GROUP_CH, S5_STATE), S5_STATE),
        "s5_d": jax.random.normal(k[17], (N_S5, PRIMARY_WIDTH), f32),
        "s5_w_glu": w(k[18], (N_S5, PRIMARY_WIDTH, 2 * PRIMARY_WIDTH), PRIMARY_WIDTH),
        "mla_w_in": w(k[19], (N_MLA, D_MODEL, MLA_IN_WIDTH), D_MODEL),
        "mla_q_lora_norm": gain(k[20], (N_MLA, MLA_Q_LORA)),
        "mla_kv_lora_norm": gain(k[21], (N_MLA, MLA_KV_LORA)),
        "mla_w_uq": w(k[22], (N_MLA, MLA_Q_LORA, MLA_HEADS * (MLA_NOPE + MLA_ROPE)), MLA_Q_LORA),
        "mla_w_ukv": w(k[23], (N_MLA, MLA_KV_LORA, MLA_HEADS * (MLA_NOPE + MLA_V)), MLA_KV_LORA),
        "mla_q_nope_norm": gain(k[24], (N_MLA, MLA_NOPE)),
        "mla_k_nope_norm": gain(k[25], (N_MLA, MLA_NOPE)),
        "mla_q_rope_norm": gain(k[26], (N_MLA, MLA_ROPE)),
        "mla_k_rope_norm": gain(k[27], (N_MLA, MLA_ROPE)),
    }


def reference(x, mem, positions, ln_gain, w_out, mem_norm, w_mem_kv, xq_norm, xk_norm,
              s5_w_in, s5_lambda_re, s5_lambda_im, s5_log_step, s5_b_re, s5_b_im, s5_c_re, s5_c_im,
              s5_d, s5_w_glu, mla_w_in, mla_q_lora_norm, mla_kv_lora_norm, mla_w_uq, mla_w_ukv,
              mla_q_nope_norm, mla_k_nope_norm, mla_q_rope_norm, mla_k_rope_norm):
    cos, sin = rotary_tables(positions)
    for i in range(DEPTH):
        j = i // N_MIXERS
        if i % N_MIXERS == 0:
            x = s5_layer(x, mem, ln_gain[i], s5_w_in[j], s5_lambda_re[j], s5_lambda_im[j], s5_log_step[j],
                         s5_b_re[j], s5_b_im[j], s5_c_re[j], s5_c_im[j], s5_d[j], s5_w_glu[j],
                         w_out[i], mem_norm[i], w_mem_kv[i], xq_norm[i], xk_norm[i])
        else:
            x = mla_layer(x, mem, cos, sin, ln_gain[i], mla_w_in[j], mla_q_lora_norm[j], mla_kv_lora_norm[j],
                          mla_w_uq[j], mla_w_ukv[j], mla_q_nope_norm[j], mla_k_nope_norm[j],
                          mla_q_rope_norm[j], mla_k_rope_norm[j],
                          w_out[i], mem_norm[i], w_mem_kv[i], xq_norm[i], xk_norm[i])
    return x
```

```python
import functools
import math

import jax
import jax.numpy as jnp
from jax import lax
from jax.experimental import pallas as pl
from jax.experimental.pallas import tpu as pltpu

D_MODEL = 1024
BRANCH_WIDTH = 2 * D_MODEL
XQ_WIDTH = BRANCH_WIDTH // 4
PRIMARY_WIDTH = BRANCH_WIDTH - XQ_WIDTH
X_HEADS = 4
X_HEAD_DIM = XQ_WIDTH // X_HEADS
S5_GROUP_CH = 16
S5_GROUPS = PRIMARY_WIDTH // S5_GROUP_CH
S5_STATE = 64
MLA_NOPE = 128
MLA_ROPE = 64
MLA_V = 128
MLA_HEADS = PRIMARY_WIDTH // MLA_V
MLA_Q_LORA = D_MODEL // 2
MLA_KV_LORA = D_MODEL // 4
ROPE_THETA = 10000.0
EPS = 1e-6

LANES = 128
MLA_QK_PAD = 2 * LANES
S5_SLAB_GROUPS = LANES // S5_GROUP_CH
S5_SLABS = S5_GROUPS // S5_SLAB_GROUPS
S5_SLAB_STATE = S5_SLAB_GROUPS * S5_STATE
VMEM_LIMIT = 56 * 1024 * 1024

F32 = jnp.float32
BF16 = jnp.bfloat16


def _cparams(sem):
    return pltpu.CompilerParams(dimension_semantics=sem, vmem_limit_bytes=VMEM_LIMIT)


def _rms(x, g):
    return x * lax.rsqrt(jnp.mean(x * x, axis=-1, keepdims=True) + EPS) * g


def _norm_matmul_kernel(x_ref, g_ref, w_ref, o_ref, *, col_chunk):
    xn = _rms(x_ref[...].astype(F32), g_ref[...]).astype(BF16)
    for c in range(o_ref.shape[1] // col_chunk):
        sl = slice(c * col_chunk, (c + 1) * col_chunk)
        o_ref[:, sl] = jnp.dot(xn, w_ref[:, sl], preferred_element_type=F32).astype(o_ref.dtype)


def _norm_matmul(x, g, w, *, tm, col_chunk, name):
    n, d = x.shape
    wout = w.shape[1]
    return pl.pallas_call(
        functools.partial(_norm_matmul_kernel, col_chunk=col_chunk),
        grid=(n // tm,),
        in_specs=[pl.BlockSpec((tm, d), lambda i: (i, 0)),
                  pl.BlockSpec((1, d), lambda i: (0, 0)),
                  pl.BlockSpec((d, wout), lambda i: (0, 0))],
        out_specs=pl.BlockSpec((tm, wout), lambda i: (i, 0)),
        out_shape=jax.ShapeDtypeStruct((n, wout), BF16),
        compiler_params=_cparams(("parallel",)),
        name=name,
    )(x, g.reshape(1, d), w)


def _mem_kv_kernel(m_ref, g_ref, w_ref, kg_ref, k_ref, v_ref):
    mn = _rms(m_ref[0], g_ref[...]).astype(BF16)
    kv = jnp.dot(mn, w_ref[...], preferred_element_type=F32)
    for h in range(X_HEADS):
        sl = slice(h * X_HEAD_DIM, (h + 1) * X_HEAD_DIM)
        k_ref[0, :, sl] = _rms(kv[:, sl], kg_ref[...]).astype(BF16)
    v_ref[0] = kv[:, XQ_WIDTH:].astype(BF16)


def _mem_kv(mem, mem_norm, w_mem_kv, xk_norm):
    b, m, d = mem.shape
    out = jax.ShapeDtypeStruct((b, m, XQ_WIDTH), BF16)
    return pl.pallas_call(
        _mem_kv_kernel,
        grid=(b,),
        in_specs=[pl.BlockSpec((1, m, d), lambda i: (i, 0, 0)),
                  pl.BlockSpec((1, d), lambda i: (0, 0)),
                  pl.BlockSpec((d, 2 * XQ_WIDTH), lambda i: (0, 0)),
                  pl.BlockSpec((1, X_HEAD_DIM), lambda i: (0, 0))],
        out_specs=[pl.BlockSpec((1, m, XQ_WIDTH), lambda i: (i, 0, 0)),
                   pl.BlockSpec((1, m, XQ_WIDTH), lambda i: (i, 0, 0))],
        out_shape=[out, out],
        compiler_params=_cparams(("parallel",)),
        name="mem_kv",
    )(mem, mem_norm.reshape(1, d), w_mem_kv.astype(BF16), xk_norm.reshape(1, X_HEAD_DIM))


def _s5_params_kernel(lr_ref, li_ref, ls_ref, lrc_ref, lic_ref, lsc_ref, br_ref, bi_ref,
                      ar_ref, ai_ref, bbr_ref, bbi_ref):
    def abar(lr, li, ls):
        step = jnp.exp(ls)
        mag = jnp.exp(lr * step)
        return mag * jnp.cos(li * step), mag * jnp.sin(li * step)

    ar, ai = abar(lr_ref[...], li_ref[...], ls_ref[...])
    ar_ref[...] = ar
    ai_ref[...] = ai
    lr, li = lrc_ref[...], lic_ref[...]
    ar, ai = abar(lr, li, lsc_ref[...])
    den = lr * lr + li * li
    mr = ((ar - 1.0) * lr + ai * li) / den
    mi = (ai * lr - (ar - 1.0) * li) / den
    br, bi = br_ref[...], bi_ref[...]
    bbr_ref[...] = mr * br - mi * bi
    bbi_ref[...] = mr * bi + mi * br


def _s5_params(lam_re, lam_im, log_step, b_re, b_im):
    g, p = lam_re.shape
    c = b_re.shape[-1]
    col = lambda a: a.reshape(g * p, 1)
    ls_col = jnp.broadcast_to(log_step[:, None], (g, p))
    return pl.pallas_call(
        _s5_params_kernel,
        out_shape=[jax.ShapeDtypeStruct((g, p), F32), jax.ShapeDtypeStruct((g, p), F32),
                   jax.ShapeDtypeStruct((g * p, c), F32), jax.ShapeDtypeStruct((g * p, c), F32)],
        name="s5_params",
    )(lam_re, lam_im, log_step.reshape(g, 1), col(lam_re), col(lam_im), col(ls_col),
      b_re.reshape(g * p, c), b_im.reshape(g * p, c))


def _slab_block_diag(w):
    _, r, c = w.shape
    w = w.reshape(S5_SLABS, S5_SLAB_GROUPS, r, c)
    eye = jnp.eye(S5_SLAB_GROUPS, dtype=w.dtype)
    return jnp.einsum('sgrc,gh->sgrhc', w, eye).reshape(S5_SLABS, S5_SLAB_GROUPS * r, S5_SLAB_GROUPS * c)


def _s5_mix_kernel(u_ref, bw_ref, cre_ref, cim_ref, ar_ref, ai_ref, d_ref, o_ref, bu_ref, h_ref, *, tt):
    nb = u_ref.shape[0]
    nblk = S5_SLAB_STATE // LANES

    @pl.when(pl.program_id(0) == 0)
    def _():
        h_ref[...] = jnp.zeros_like(h_ref)

    def slab(j, carry):
        col = pl.multiple_of(j * LANES, LANES)
        u = u_ref[:, :, pl.ds(col, LANES)].reshape(nb * tt, LANES)
        bu = jnp.dot(u, bw_ref[j], preferred_element_type=F32)
        for c in range(2 * nblk):
            bu_ref[c] = bu[:, c * LANES:(c + 1) * LANES]
        a_re = [jnp.broadcast_to(ar_ref[j, :, c * LANES:(c + 1) * LANES], (nb, LANES)) for c in range(nblk)]
        a_im = [jnp.broadcast_to(ai_ref[j, :, c * LANES:(c + 1) * LANES], (nb, LANES)) for c in range(nblk)]

        def step(t, hc):
            rows = pl.ds(t, nb, stride=tt)
            out = []
            for c in range(nblk):
                h_re, h_im = hc[c], hc[nblk + c]
                n_re = a_re[c] * h_re - a_im[c] * h_im + bu_ref[c, rows, :]
                n_im = a_re[c] * h_im + a_im[c] * h_re + bu_ref[nblk + c, rows, :]
                bu_ref[c, rows, :] = n_re
                bu_ref[nblk + c, rows, :] = n_im
                out.append((n_re, n_im))
            return tuple(o[0] for o in out) + tuple(o[1] for o in out)

        h0 = tuple(h_ref[j, c] for c in range(2 * nblk))
        hn = lax.fori_loop(0, tt, step, h0, unroll=4)
        for c in range(2 * nblk):
            h_ref[j, c] = hn[c]
        s_re = jnp.concatenate([bu_ref[c] for c in range(nblk)], axis=-1).astype(BF16)
        s_im = jnp.concatenate([bu_ref[nblk + c] for c in range(nblk)], axis=-1).astype(BF16)
        y = (jnp.dot(s_re, cre_ref[j], preferred_element_type=F32)
             - jnp.dot(s_im, cim_ref[j], preferred_element_type=F32))
        y = y + d_ref[j] * u.astype(F32)
        o_ref[:, :, pl.ds(col, LANES)] = jax.nn.gelu(y).astype(o_ref.dtype).reshape(nb, tt, LANES)
        return carry

    lax.fori_loop(0, S5_SLABS, slab, 0)


def _s5_mix(proj, bw, cre, cim, a_re, a_im, d, *, tt):
    b, l, _ = proj.shape
    ns = S5_SLAB_STATE
    full3 = lambda a: pl.BlockSpec(a.shape, lambda i: (0, 0, 0))
    return pl.pallas_call(
        functools.partial(_s5_mix_kernel, tt=tt),
        grid=(l // tt,),
        in_specs=[pl.BlockSpec((b, tt, PRIMARY_WIDTH), lambda i: (0, i, 0)),
                  full3(bw), full3(cre), full3(cim), full3(a_re), full3(a_im), full3(d)],
        out_specs=pl.BlockSpec((b, tt, PRIMARY_WIDTH), lambda i: (0, i, 0)),
        out_shape=jax.ShapeDtypeStruct((b, l, PRIMARY_WIDTH), BF16),
        scratch_shapes=[pltpu.VMEM((2 * ns // LANES, b * tt, LANES), F32),
                        pltpu.VMEM((S5_SLABS, 2 * ns // LANES, b, LANES), F32)],
        compiler_params=_cparams(("arbitrary",)),
        name="s5_mix",
    )(proj, bw, cre, cim, a_re, a_im, d)


def _glu_kernel(y_ref, w_ref, o_ref, *, col_chunk):
    y = y_ref[...]
    half = o_ref.shape[1]
    for c in range(half // col_chunk):
        a = jnp.dot(y, w_ref[:, c * col_chunk:(c + 1) * col_chunk], preferred_element_type=F32)
        g = jnp.dot(y, w_ref[:, half + c * col_chunk:half + (c + 1) * col_chunk], preferred_element_type=F32)
        o_ref[:, c * col_chunk:(c + 1) * col_chunk] = (a * jax.nn.sigmoid(g)).astype(o_ref.dtype)


def _glu(y, w, *, tm, col_chunk):
    n, k = y.shape
    half = w.shape[1] // 2
    return pl.pallas_call(
        functools.partial(_glu_kernel, col_chunk=col_chunk),
        grid=(n // tm,),
        in_specs=[pl.BlockSpec((tm, k), lambda i: (i, 0)),
                  pl.BlockSpec(w.shape, lambda i: (0, 0))],
        out_specs=pl.BlockSpec((tm, half), lambda i: (i, 0)),
        out_shape=jax.ShapeDtypeStruct((n, half), BF16),
        compiler_params=_cparams(("parallel",)),
        name="glu",
    )(y, w)


def _merge_kernel(x_ref, mix_ref, xq_ref, gate_ref, k_ref, v_ref, qg_ref, w_ref, o_ref, cat_ref):
    gate = gate_ref[0].astype(F32)
    sg = gate * jax.nn.sigmoid(gate)
    cat_ref[:, :PRIMARY_WIDTH] = (mix_ref[0].astype(F32) * sg[:, :PRIMARY_WIDTH]).astype(BF16)
    scale = X_HEAD_DIM ** -0.5
    for h in range(X_HEADS):
        sl = slice(h * X_HEAD_DIM, (h + 1) * X_HEAD_DIM)
        q = _rms(xq_ref[0, :, sl].astype(F32), qg_ref[...]).astype(BF16)
        s = lax.dot_general(q, k_ref[0, :, sl], (((1,), (1,)), ((), ())), preferred_element_type=F32) * scale
        p = jnp.exp(s - jnp.max(s, axis=-1, keepdims=True))
        p = (p / jnp.sum(p, axis=-1, keepdims=True)).astype(BF16)
        mo = jnp.dot(p, v_ref[0, :, sl], preferred_element_type=F32)
        osl = slice(PRIMARY_WIDTH + h * X_HEAD_DIM, PRIMARY_WIDTH + (h + 1) * X_HEAD_DIM)
        cat_ref[:, osl] = (mo * sg[:, osl]).astype(BF16)
    o_ref[0] = x_ref[0] + jnp.dot(cat_ref[...], w_ref[...], preferred_element_type=F32)


def _merge(x, mix, proj, xq_blk, gate_blk, mk, mv, xq_norm, w_out, *, tm):
    b, l, d = x.shape
    m = mk.shape[1]
    return pl.pallas_call(
        _merge_kernel,
        grid=(b, l // tm),
        in_specs=[pl.BlockSpec((1, tm, d), lambda i, j: (i, j, 0)),
                  pl.BlockSpec((1, tm, PRIMARY_WIDTH), lambda i, j: (i, j, 0)),
                  pl.BlockSpec((1, tm, XQ_WIDTH), lambda i, j: (i, j, xq_blk)),
                  pl.BlockSpec((1, tm, BRANCH_WIDTH), lambda i, j: (i, j, gate_blk)),
                  pl.BlockSpec((1, m, XQ_WIDTH), lambda i, j: (i, 0, 0)),
                  pl.BlockSpec((1, m, XQ_WIDTH), lambda i, j: (i, 0, 0)),
                  pl.BlockSpec((1, X_HEAD_DIM), lambda i, j: (0, 0)),
                  pl.BlockSpec((BRANCH_WIDTH, d), lambda i, j: (0, 0))],
        out_specs=pl.BlockSpec((1, tm, d), lambda i, j: (i, j, 0)),
        out_shape=jax.ShapeDtypeStruct((b, l, d), F32),
        scratch_shapes=[pltpu.VMEM((tm, BRANCH_WIDTH), BF16)],
        compiler_params=_cparams(("parallel", "parallel")),
        name="merge",
    )(x, mix, proj, proj, mk, mv, xq_norm.reshape(1, X_HEAD_DIM), w_out)


def _mla_qkv_kernel(cq_ref, ckv_ref, kr_ref, pos_ref, invf_ref, gq_ref, gkv_ref, gqn_ref, gkn_ref, gqr_ref,
                    gkr_ref, wq_ref, wkv_ref, q_ref, kn_ref, krope_ref, v_ref):
    half = MLA_ROPE // 2
    ang = pos_ref[0].astype(F32) * invf_ref[...]
    lane = lax.broadcasted_iota(jnp.int32, ang.shape, 1)
    cos = jnp.where(lane < MLA_ROPE, jnp.cos(ang), 0.0)
    sin = jnp.sin(ang)
    sin_lo = jnp.where(lane < half, -sin, 0.0)
    sin_hi = jnp.where((lane >= half) & (lane < MLA_ROPE), sin, 0.0)

    def rope(x, g):
        xn = x * lax.rsqrt(jnp.sum(x * x, axis=-1, keepdims=True) * (1.0 / MLA_ROPE) + EPS) * g
        return xn * cos + pltpu.roll(xn, LANES - half, 1) * sin_lo + pltpu.roll(xn, half, 1) * sin_hi

    scale = (MLA_NOPE + MLA_ROPE) ** -0.5
    cq = _rms(cq_ref[0].astype(F32), gq_ref[...]).astype(BF16)
    ckv = _rms(ckv_ref[0].astype(F32), gkv_ref[...]).astype(BF16)
    for h in range(MLA_HEADS):
        sl = slice(h * MLA_QK_PAD, (h + 1) * MLA_QK_PAD)
        q = jnp.dot(cq, wq_ref[:, sl], preferred_element_type=F32)
        q_ref[0, h, :, :MLA_NOPE] = (_rms(q[:, :MLA_NOPE], gqn_ref[...]) * scale).astype(BF16)
        q_ref[0, h, :, MLA_NOPE:] = (rope(q[:, MLA_NOPE:], gqr_ref[...]) * scale).astype(BF16)
        kv = jnp.dot(ckv, wkv_ref[:, sl], preferred_element_type=F32)
        kn_ref[0, h] = _rms(kv[:, :MLA_NOPE], gkn_ref[...]).astype(BF16)
        v_ref[0, h] = kv[:, MLA_NOPE:].astype(BF16)
    krope_ref[0] = rope(kr_ref[0].astype(F32), gkr_ref[...]).astype(BF16)


def _mla_qkv(proj, cq_blk, ckv_blk, kr_blk, positions, gains, wq, wkv, *, tm):
    b, l, _ = proj.shape
    hh = MLA_HEADS
    half = MLA_ROPE // 2
    inv_freq = ROPE_THETA ** (-jnp.arange(half, dtype=F32) / half)
    invf = jnp.concatenate([inv_freq, inv_freq, jnp.zeros((LANES - MLA_ROPE,), F32)]).reshape(1, LANES)
    vec = lambda n: pl.BlockSpec((1, n), lambda i, j: (0, 0))
    pad_gain = lambda g: jnp.concatenate([g, jnp.zeros((LANES - MLA_ROPE,), F32)]).reshape(1, LANES)
    gq, gkv, gqn, gkn, gqr, gkr = gains
    head_out = lambda w: pl.BlockSpec((1, hh, tm, w), lambda i, j: (i, 0, j, 0))
    return pl.pallas_call(
        _mla_qkv_kernel,
        grid=(b, l // tm),
        in_specs=[pl.BlockSpec((1, tm, MLA_Q_LORA), lambda i, j: (i, j, cq_blk)),
                  pl.BlockSpec((1, tm, MLA_KV_LORA), lambda i, j: (i, j, ckv_blk)),
                  pl.BlockSpec((1, tm, LANES), lambda i, j: (i, j, kr_blk)),
                  pl.BlockSpec((1, tm, 1), lambda i, j: (i, j, 0)),
                  vec(LANES), vec(MLA_Q_LORA), vec(MLA_KV_LORA), vec(MLA_NOPE), vec(MLA_NOPE), vec(LANES),
                  vec(LANES),
                  pl.BlockSpec(wq.shape, lambda i, j: (0, 0)),
                  pl.BlockSpec(wkv.shape, lambda i, j: (0, 0))],
        out_specs=[head_out(MLA_QK_PAD), head_out(MLA_NOPE),
                   pl.BlockSpec((1, tm, LANES), lambda i, j: (i, j, 0)), head_out(MLA_V)],
        out_shape=[jax.ShapeDtypeStruct((b, hh, l, MLA_QK_PAD), BF16),
                   jax.ShapeDtypeStruct((b, hh, l, MLA_NOPE), BF16),
                   jax.ShapeDtypeStruct((b, l, LANES), BF16),
                   jax.ShapeDtypeStruct((b, hh, l, MLA_V), BF16)],
        compiler_params=_cparams(("parallel", "parallel")),
        name="mla_qkv",
    )(proj, proj, proj, positions.reshape(b, l, 1), invf, gq.reshape(1, -1), gkv.reshape(1, -1),
      gqn.reshape(1, -1), gkn.reshape(1, -1), pad_gain(gqr), pad_gain(gkr), wq, wkv)


def _flash_kernel(q_ref, kn_ref, kr_ref, v_ref, o_ref, *, tq):
    qi = pl.program_id(2)
    q = q_ref[0, 0]

    def scores(j):
        rows = pl.ds(pl.multiple_of(j * tq, tq), tq)
        k = jnp.concatenate([kn_ref[0, 0, rows, :], kr_ref[0, rows, :]], axis=-1)
        s = lax.dot_general(q, k, (((1,), (1,)), ((), ())), preferred_element_type=F32)
        return s, v_ref[0, 0, rows, :]

    def update(s, v, carry):
        m, l, acc = carry
        m_new = jnp.maximum(m, jnp.max(s, axis=-1, keepdims=True))
        alpha = jnp.exp(m - m_new)
        p = jnp.exp(s - m_new)
        l = alpha * l + jnp.sum(p, axis=-1, keepdims=True)
        acc = alpha * acc + jnp.dot(p.astype(BF16), v, preferred_element_type=F32)
        return m_new, l, acc

    def body(j, carry):
        s, v = scores(j)
        return update(s, v, carry)

    init = (jnp.full((tq, 1), -jnp.inf, F32), jnp.zeros((tq, 1), F32), jnp.zeros((tq, MLA_V), F32))
    carry = lax.fori_loop(0, qi, body, init)
    s, v = scores(qi)
    row = lax.broadcasted_iota(jnp.int32, s.shape, 0)
    colm = lax.broadcasted_iota(jnp.int32, s.shape, 1)
    s = jnp.where(colm <= row, s, jnp.finfo(F32).min)
    _, l, acc = update(s, v, carry)
    o_ref[0] = (acc / l).astype(o_ref.dtype)


def _flash(q, kn, kr, v, *, tq):
    b, hh, l, _ = q.shape
    return pl.pallas_call(
        functools.partial(_flash_kernel, tq=tq),
        grid=(b, hh, l // tq),
        in_specs=[pl.BlockSpec((1, 1, tq, MLA_QK_PAD), lambda i, h, j: (i, h, j, 0)),
                  pl.BlockSpec((1, 1, l, MLA_NOPE), lambda i, h, j: (i, h, 0, 0)),
                  pl.BlockSpec((1, l, LANES), lambda i, h, j: (i, 0, 0)),
                  pl.BlockSpec((1, 1, l, MLA_V), lambda i, h, j: (i, h, 0, 0))],
        out_specs=pl.BlockSpec((1, tq, MLA_V), lambda i, h, j: (i, j, h)),
        out_shape=jax.ShapeDtypeStruct((b, l, hh * MLA_V), BF16),
        compiler_params=_cparams(("parallel", "parallel", "parallel")),
        name="flash",
    )(q, kn, kr, v)


def _s5_layer(x, mem, ln, w_in, lam_re, lam_im, log_step, b_re, b_im, c_re, c_im, d, w_glu,
              w_out, mem_norm, w_mem_kv, xq_norm, xk_norm):
    b, l, dm = x.shape
    proj = _norm_matmul(x.reshape(b * l, dm), ln, w_in.astype(BF16), tm=512, col_chunk=512, name="s5_in_proj")
    proj = proj.reshape(b, l, -1)
    a_re, a_im, bb_re, bb_im = _s5_params(lam_re, lam_im, log_step, b_re, b_im)
    to_in = lambda w: _slab_block_diag(w.reshape(S5_GROUPS, S5_STATE, S5_GROUP_CH).transpose(0, 2, 1))
    bw = jnp.concatenate([to_in(bb_re), to_in(bb_im)], axis=-1).astype(BF16)
    to_out = lambda w: _slab_block_diag(w.transpose(0, 2, 1)).astype(BF16)
    row = lambda a: a.reshape(S5_SLABS, 1, S5_SLAB_STATE)
    yg = _s5_mix(proj, bw, to_out(c_re), to_out(c_im), row(a_re), row(a_im),
                 d.reshape(S5_SLABS, 1, LANES), tt=64)
    y = _glu(yg.reshape(b * l, PRIMARY_WIDTH), w_glu.astype(BF16), tm=512, col_chunk=256)
    mk, mv = _mem_kv(mem, mem_norm, w_mem_kv, xk_norm)
    return _merge(x, y.reshape(b, l, PRIMARY_WIDTH), proj, PRIMARY_WIDTH // XQ_WIDTH, 1, mk, mv, xq_norm,
                  w_out.astype(BF16), tm=512)


def _mla_layer(x, mem, positions, ln, w_in, q_lora_norm, kv_lora_norm, w_uq, w_ukv, q_nope_norm, k_nope_norm,
               q_rope_norm, k_rope_norm, w_out, mem_norm, w_mem_kv, xq_norm, xk_norm):
    b, l, dm = x.shape
    o1 = MLA_Q_LORA
    o2 = o1 + MLA_KV_LORA
    o3 = o2 + MLA_ROPE
    o4 = o3 + XQ_WIDTH
    w_perm = jnp.concatenate([w_in[:, o4:], w_in[:, :o1], w_in[:, o3:o4], w_in[:, o1:o2], w_in[:, o2:o3],
                              jnp.zeros((dm, LANES - MLA_ROPE), w_in.dtype)], axis=1).astype(BF16)
    proj = _norm_matmul(x.reshape(b * l, dm), ln, w_perm, tm=512, col_chunk=384, name="mla_in_proj")
    proj = proj.reshape(b, l, -1)
    gate_blk = 0
    cq_blk = BRANCH_WIDTH // MLA_Q_LORA
    xq_blk = (BRANCH_WIDTH + MLA_Q_LORA) // XQ_WIDTH
    ckv_blk = (BRANCH_WIDTH + MLA_Q_LORA + XQ_WIDTH) // MLA_KV_LORA
    kr_blk = (BRANCH_WIDTH + MLA_Q_LORA + XQ_WIDTH + MLA_KV_LORA) // LANES
    wq = w_uq.reshape(MLA_Q_LORA, MLA_HEADS, MLA_NOPE + MLA_ROPE)
    wq = jnp.pad(wq, ((0, 0), (0, 0), (0, MLA_QK_PAD - MLA_NOPE - MLA_ROPE)))
    wq = wq.reshape(MLA_Q_LORA, MLA_HEADS * MLA_QK_PAD).astype(BF16)
    q, kn, kr, v = _mla_qkv(proj, cq_blk, ckv_blk, kr_blk, positions,
                            (q_lora_norm, kv_lora_norm, q_nope_norm, k_nope_norm, q_rope_norm, k_rope_norm),
                            wq, w_ukv.astype(BF16), tm=256)
    attn = _flash(q, kn, kr, v, tq=256)
    mk, mv = _mem_kv(mem, mem_norm, w_mem_kv, xk_norm)
    return _merge(x, attn, proj, xq_blk, gate_blk, mk, mv, xq_norm, w_out.astype(BF16), tm=512)


def kernel(x, mem, positions, ln_gain, w_out, mem_norm, w_mem_kv, xq_norm, xk_norm,
           s5_w_in, s5_lambda_re, s5_lambda_im, s5_log_step, s5_b_re, s5_b_im, s5_c_re, s5_c_im,
           s5_d, s5_w_glu, mla_w_in, mla_q_lora_norm, mla_kv_lora_norm, mla_w_uq, mla_w_ukv,
           mla_q_nope_norm, mla_k_nope_norm, mla_q_rope_norm, mla_k_rope_norm):
    depth = ln_gain.shape[0]
    for i in range(depth):
        j = i // 2
        if i % 2 == 0:
            x = _s5_layer(x, mem, ln_gain[i], s5_w_in[j], s5_lambda_re[j], s5_lambda_im[j], s5_log_step[j],
                          s5_b_re[j], s5_b_im[j], s5_c_re[j], s5_c_im[j], s5_d[j], s5_w_glu[j],
                          w_out[i], mem_norm[i], w_mem_kv[i], xq_norm[i], xk_norm[i])
        else:
            x = _mla_layer(x, mem, positions, ln_gain[i], mla_w_in[j], mla_q_lora_norm[j], mla_kv_lora_norm[j],
                           mla_w_uq[j], mla_w_ukv[j], mla_q_nope_norm[j], mla_k_nope_norm[j],
                           mla_q_rope_norm[j], mla_k_rope_norm[j],
                           w_out[i], mem_norm[i], w_mem_kv[i], xq_norm[i], xk_norm[i])
    return x
```

```python
import functools
import math

import jax
import jax.numpy as jnp
from jax import lax
from jax.experimental import pallas as pl
from jax.experimental.pallas import tpu as pltpu

D_MODEL = 1024
BRANCH_WIDTH = 2 * D_MODEL
XQ_WIDTH = BRANCH_WIDTH // 4
PRIMARY_WIDTH = BRANCH_WIDTH - XQ_WIDTH
X_HEADS = 4
X_HEAD_DIM = XQ_WIDTH // X_HEADS
S5_GROUP_CH = 16
S5_GROUPS = PRIMARY_WIDTH // S5_GROUP_CH
S5_STATE = 64
MLA_NOPE = 128
MLA_ROPE = 64
MLA_V = 128
MLA_HEADS = PRIMARY_WIDTH // MLA_V
MLA_Q_LORA = D_MODEL // 2
MLA_KV_LORA = D_MODEL // 4
ROPE_THETA = 10000.0
EPS = 1e-6

LANES = 128
MLA_QK_PAD = 2 * LANES
S5_SLAB_GROUPS = LANES // S5_GROUP_CH
S5_SLABS = S5_GROUPS // S5_SLAB_GROUPS
S5_SLAB_STATE = S5_SLAB_GROUPS * S5_STATE
VMEM_LIMIT = 56 * 1024 * 1024

F32 = jnp.float32
BF16 = jnp.bfloat16


def _cparams(sem):
    return pltpu.CompilerParams(dimension_semantics=sem, vmem_limit_bytes=VMEM_LIMIT)


def _rms(x, g):
    return x * lax.rsqrt(jnp.mean(x * x, axis=-1, keepdims=True) + EPS) * g


def _norm_matmul_kernel(x_ref, g_ref, w_ref, o_ref, *, col_chunk):
    xn = _rms(x_ref[...].astype(F32), g_ref[...]).astype(BF16)
    for c in range(o_ref.shape[1] // col_chunk):
        sl = slice(c * col_chunk, (c + 1) * col_chunk)
        o_ref[:, sl] = jnp.dot(xn, w_ref[:, sl], preferred_element_type=F32).astype(o_ref.dtype)


def _norm_matmul(x, g, w, *, tm, col_chunk, name):
    n, d = x.shape
    wout = w.shape[1]
    return pl.pallas_call(
        functools.partial(_norm_matmul_kernel, col_chunk=col_chunk),
        grid=(n // tm,),
        in_specs=[pl.BlockSpec((tm, d), lambda i: (i, 0)),
                  pl.BlockSpec((1, d), lambda i: (0, 0)),
                  pl.BlockSpec((d, wout), lambda i: (0, 0))],
        out_specs=pl.BlockSpec((tm, wout), lambda i: (i, 0)),
        out_shape=jax.ShapeDtypeStruct((n, wout), BF16),
        compiler_params=_cparams(("parallel",)),
        name=name,
    )(x, g.reshape(1, d), w)


def _mem_kv_kernel(m_ref, g_ref, w_ref, kg_ref, k_ref, v_ref):
    mn = _rms(m_ref[0], g_ref[...]).astype(BF16)
    kv = jnp.dot(mn, w_ref[...], preferred_element_type=F32)
    for h in range(X_HEADS):
        sl = slice(h * X_HEAD_DIM, (h + 1) * X_HEAD_DIM)
        k_ref[0, :, sl] = _rms(kv[:, sl], kg_ref[...]).astype(BF16)
    v_ref[0] = kv[:, XQ_WIDTH:].astype(BF16)


def _mem_kv(mem, mem_norm, w_mem_kv, xk_norm):
    b, m, d = mem.shape
    out = jax.ShapeDtypeStruct((b, m, XQ_WIDTH), BF16)
    return pl.pallas_call(
        _mem_kv_kernel,
        grid=(b,),
        in_specs=[pl.BlockSpec((1, m, d), lambda i: (i, 0, 0)),
                  pl.BlockSpec((1, d), lambda i: (0, 0)),
                  pl.BlockSpec((d, 2 * XQ_WIDTH), lambda i: (0, 0)),
                  pl.BlockSpec((1, X_HEAD_DIM), lambda i: (0, 0))],
        out_specs=[pl.BlockSpec((1, m, XQ_WIDTH), lambda i: (i, 0, 0)),
                   pl.BlockSpec((1, m, XQ_WIDTH), lambda i: (i, 0, 0))],
        out_shape=[out, out],
        compiler_params=_cparams(("parallel",)),
        name="mem_kv",
    )(mem, mem_norm.reshape(1, d), w_mem_kv.astype(BF16), xk_norm.reshape(1, X_HEAD_DIM))


def _s5_params_kernel(lr_ref, li_ref, ls_ref, lrc_ref, lic_ref, lsc_ref, br_ref, bi_ref,
                      ar_ref, ai_ref, bbr_ref, bbi_ref):
    def abar(lr, li, ls):
        step = jnp.exp(ls)
        mag = jnp.exp(lr * step)
        return mag * jnp.cos(li * step), mag * jnp.sin(li * step)

    ar, ai = abar(lr_ref[...], li_ref[...], ls_ref[...])
    ar_ref[...] = ar
    ai_ref[...] = ai
    lr, li = lrc_ref[...], lic_ref[...]
    ar, ai = abar(lr, li, lsc_ref[...])
    den = lr * lr + li * li
    mr = ((ar - 1.0) * lr + ai * li) / den
    mi = (ai * lr - (ar - 1.0) * li) / den
    br, bi = br_ref[...], bi_ref[...]
    bbr_ref[...] = mr * br - mi * bi
    bbi_ref[...] = mr * bi + mi * br


def _s5_params(lam_re, lam_im, log_step, b_re, b_im):
    g, p = lam_re.shape
    c = b_re.shape[-1]
    col = lambda a: a.reshape(g * p, 1)
    ls_col = jnp.broadcast_to(log_step[:, None], (g, p))
    return pl.pallas_call(
        _s5_params_kernel,
        out_shape=[jax.ShapeDtypeStruct((g, p), F32), jax.ShapeDtypeStruct((g, p), F32),
                   jax.ShapeDtypeStruct((g * p, c), F32), jax.ShapeDtypeStruct((g * p, c), F32)],
        name="s5_params",
    )(lam_re, lam_im, log_step.reshape(g, 1), col(lam_re), col(lam_im), col(ls_col),
      b_re.reshape(g * p, c), b_im.reshape(g * p, c))


def _slab_block_diag(w):
    _, r, c = w.shape
    w = w.reshape(S5_SLABS, S5_SLAB_GROUPS, r, c)
    eye = jnp.eye(S5_SLAB_GROUPS, dtype=w.dtype)
    return jnp.einsum('sgrc,gh->sgrhc', w, eye).reshape(S5_SLABS, S5_SLAB_GROUPS * r, S5_SLAB_GROUPS * c)


def _s5_mix_kernel(u_ref, bw_ref, cre_ref, cim_ref, ar_ref, ai_ref, d_ref, o_ref, bu_ref, h_ref, *, tt):
    nb = u_ref.shape[0]
    ns = S5_SLAB_STATE

    @pl.when(pl.program_id(0) == 0)
    def _():
        h_ref[...] = jnp.zeros_like(h_ref)

    def slab(j, carry):
        col = pl.multiple_of(j * LANES, LANES)
        u = jnp.swapaxes(u_ref[:, :, pl.ds(col, LANES)].astype(F32), 0, 1).reshape(tt * nb, LANES)
        bu_ref[...] = jnp.dot(u.astype(BF16), bw_ref[j], preferred_element_type=F32)
        a_re = jnp.broadcast_to(ar_ref[j], (nb, ns))
        a_im = jnp.broadcast_to(ai_ref[j], (nb, ns))

        def step(t, hc):
            h_re, h_im = hc
            rows = pl.ds(pl.multiple_of(t * nb, nb), nb)
            n_re = a_re * h_re - a_im * h_im + bu_ref[rows, :ns]
            n_im = a_re * h_im + a_im * h_re + bu_ref[rows, ns:]
            bu_ref[rows, :ns] = n_re
            bu_ref[rows, ns:] = n_im
            return n_re, n_im

        h_re, h_im = lax.fori_loop(0, tt, step, (h_ref[j, :, :ns], h_ref[j, :, ns:]), unroll=8)
        h_ref[j, :, :ns] = h_re
        h_ref[j, :, ns:] = h_im
        y = (jnp.dot(bu_ref[:, :ns].astype(BF16), cre_ref[j], preferred_element_type=F32)
             - jnp.dot(bu_ref[:, ns:].astype(BF16), cim_ref[j], preferred_element_type=F32))
        g = jax.nn.gelu(y + d_ref[j] * u)
        o_ref[:, :, pl.ds(col, LANES)] = jnp.swapaxes(g.reshape(tt, nb, LANES), 0, 1).astype(o_ref.dtype)
        return carry

    lax.fori_loop(0, S5_SLABS, slab, 0)


def _s5_mix(proj, bw, cre, cim, a_re, a_im, d, *, tt):
    b, l, _ = proj.shape
    ns = S5_SLAB_STATE
    full3 = lambda a: pl.BlockSpec(a.shape, lambda i: (0, 0, 0))
    return pl.pallas_call(
        functools.partial(_s5_mix_kernel, tt=tt),
        grid=(l // tt,),
        in_specs=[pl.BlockSpec((b, tt, PRIMARY_WIDTH), lambda i: (0, i, 0)),
                  full3(bw), full3(cre), full3(cim), full3(a_re), full3(a_im), full3(d)],
        out_specs=pl.BlockSpec((b, tt, PRIMARY_WIDTH), lambda i: (0, i, 0)),
        out_shape=jax.ShapeDtypeStruct((b, l, PRIMARY_WIDTH), BF16),
        scratch_shapes=[pltpu.VMEM((b * tt, 2 * ns), F32),
                        pltpu.VMEM((S5_SLABS, b, 2 * ns), F32)],
        compiler_params=_cparams(("arbitrary",)),
        name="s5_mix",
    )(proj, bw, cre, cim, a_re, a_im, d)


def _glu_kernel(y_ref, w_ref, o_ref, *, col_chunk):
    y = y_ref[...]
    half = o_ref.shape[1]
    for c in range(half // col_chunk):
        a = jnp.dot(y, w_ref[:, c * col_chunk:(c + 1) * col_chunk], preferred_element_type=F32)
        g = jnp.dot(y, w_ref[:, half + c * col_chunk:half + (c + 1) * col_chunk], preferred_element_type=F32)
        o_ref[:, c * col_chunk:(c + 1) * col_chunk] = (a * jax.nn.sigmoid(g)).astype(o_ref.dtype)


def _glu(y, w, *, tm, col_chunk):
    n, k = y.shape
    half = w.shape[1] // 2
    return pl.pallas_call(
        functools.partial(_glu_kernel, col_chunk=col_chunk),
        grid=(n // tm,),
        in_specs=[pl.BlockSpec((tm, k), lambda i: (i, 0)),
                  pl.BlockSpec(w.shape, lambda i: (0, 0))],
        out_specs=pl.BlockSpec((tm, half), lambda i: (i, 0)),
        out_shape=jax.ShapeDtypeStruct((n, half), BF16),
        compiler_params=_cparams(("parallel",)),
        name="glu",
    )(y, w)


def _merge_kernel(x_ref, mix_ref, xq_ref, gate_ref, k_ref, v_ref, qg_ref, w_ref, o_ref, cat_ref):
    gate = gate_ref[0].astype(F32)
    sg = gate * jax.nn.sigmoid(gate)
    cat_ref[:, :PRIMARY_WIDTH] = (mix_ref[0].astype(F32) * sg[:, :PRIMARY_WIDTH]).astype(BF16)
    scale = X_HEAD_DIM ** -0.5
    for h in range(X_HEADS):
        sl = slice(h * X_HEAD_DIM, (h + 1) * X_HEAD_DIM)
        q = _rms(xq_ref[0, :, sl].astype(F32), qg_ref[...]).astype(BF16)
        s = lax.dot_general(q, k_ref[0, :, sl], (((1,), (1,)), ((), ())), preferred_element_type=F32) * scale
        p = jnp.exp(s - jnp.max(s, axis=-1, keepdims=True))
        p = (p / jnp.sum(p, axis=-1, keepdims=True)).astype(BF16)
        mo = jnp.dot(p, v_ref[0, :, sl], preferred_element_type=F32)
        osl = slice(PRIMARY_WIDTH + h * X_HEAD_DIM, PRIMARY_WIDTH + (h + 1) * X_HEAD_DIM)
        cat_ref[:, osl] = (mo * sg[:, osl]).astype(BF16)
    o_ref[0] = x_ref[0] + jnp.dot(cat_ref[...], w_ref[...], preferred_element_type=F32)


def _merge(x, mix, proj, xq_blk, gate_blk, mk, mv, xq_norm, w_out, *, tm):
    b, l, d = x.shape
    m = mk.shape[1]
    return pl.pallas_call(
        _merge_kernel,
        grid=(b, l // tm),
        in_specs=[pl.BlockSpec((1, tm, d), lambda i, j: (i, j, 0)),
                  pl.BlockSpec((1, tm, PRIMARY_WIDTH), lambda i, j: (i, j, 0)),
                  pl.BlockSpec((1, tm, XQ_WIDTH), lambda i, j: (i, j, xq_blk)),
                  pl.BlockSpec((1, tm, BRANCH_WIDTH), lambda i, j: (i, j, gate_blk)),
                  pl.BlockSpec((1, m, XQ_WIDTH), lambda i, j: (i, 0, 0)),
                  pl.BlockSpec((1, m, XQ_WIDTH), lambda i, j: (i, 0, 0)),
                  pl.BlockSpec((1, X_HEAD_DIM), lambda i, j: (0, 0)),
                  pl.BlockSpec((BRANCH_WIDTH, d), lambda i, j: (0, 0))],
        out_specs=pl.BlockSpec((1, tm, d), lambda i, j: (i, j, 0)),
        out_shape=jax.ShapeDtypeStruct((b, l, d), F32),
        scratch_shapes=[pltpu.VMEM((tm, BRANCH_WIDTH), BF16)],
        compiler_params=_cparams(("parallel", "parallel")),
        name="merge",
    )(x, mix, proj, proj, mk, mv, xq_norm.reshape(1, X_HEAD_DIM), w_out)


def _mla_qkv_kernel(cq_ref, ckv_ref, kr_ref, pos_ref, invf_ref, gq_ref, gkv_ref, gqn_ref, gkn_ref, gqr_ref,
                    gkr_ref, wq_ref, wkv_ref, q_ref, kn_ref, krope_ref, v_ref):
    half = MLA_ROPE // 2
    ang = pos_ref[0].astype(F32) * invf_ref[...]
    lane = lax.broadcasted_iota(jnp.int32, ang.shape, 1)
    cos = jnp.where(lane < MLA_ROPE, jnp.cos(ang), 0.0)
    sin = jnp.sin(ang)
    sin_lo = jnp.where(lane < half, -sin, 0.0)
    sin_hi = jnp.where((lane >= half) & (lane < MLA_ROPE), sin, 0.0)

    def rope(x, g):
        xn = x * lax.rsqrt(jnp.sum(x * x, axis=-1, keepdims=True) * (1.0 / MLA_ROPE) + EPS) * g
        return xn * cos + pltpu.roll(xn, LANES - half, 1) * sin_lo + pltpu.roll(xn, half, 1) * sin_hi

    scale = (MLA_NOPE + MLA_ROPE) ** -0.5
    cq = _rms(cq_ref[0].astype(F32), gq_ref[...]).astype(BF16)
    ckv = _rms(ckv_ref[0].astype(F32), gkv_ref[...]).astype(BF16)
    for h in range(MLA_HEADS):
        sl = slice(h * MLA_QK_PAD, (h + 1) * MLA_QK_PAD)
        q = jnp.dot(cq, wq_ref[:, sl], preferred_element_type=F32)
        q_ref[0, h, :, :MLA_NOPE] = (_rms(q[:, :MLA_NOPE], gqn_ref[...]) * scale).astype(BF16)
        q_ref[0, h, :, MLA_NOPE:] = (rope(q[:, MLA_NOPE:], gqr_ref[...]) * scale).astype(BF16)
        kv = jnp.dot(ckv, wkv_ref[:, sl], preferred_element_type=F32)
        kn_ref[0, h] = _rms(kv[:, :MLA_NOPE], gkn_ref[...]).astype(BF16)
        v_ref[0, h] = kv[:, MLA_NOPE:].astype(BF16)
    krope_ref[0] = rope(kr_ref[0].astype(F32), gkr_ref[...]).astype(BF16)


def _mla_qkv(proj, cq_blk, ckv_blk, kr_blk, positions, gains, wq, wkv, *, tm):
    b, l, _ = proj.shape
    hh = MLA_HEADS
    half = MLA_ROPE // 2
    inv_freq = ROPE_THETA ** (-jnp.arange(half, dtype=F32) / half)
    invf = jnp.concatenate([inv_freq, inv_freq, jnp.zeros((LANES - MLA_ROPE,), F32)]).reshape(1, LANES)
    vec = lambda n: pl.BlockSpec((1, n), lambda i, j: (0, 0))
    pad_gain = lambda g: jnp.concatenate([g, jnp.zeros((LANES - MLA_ROPE,), F32)]).reshape(1, LANES)
    gq, gkv, gqn, gkn, gqr, gkr = gains
    head_out = lambda w: pl.BlockSpec((1, hh, tm, w), lambda i, j: (i, 0, j, 0))
    return pl.pallas_call(
        _mla_qkv_kernel,
        grid=(b, l // tm),
        in_specs=[pl.BlockSpec((1, tm, MLA_Q_LORA), lambda i, j: (i, j, cq_blk)),
                  pl.BlockSpec((1, tm, MLA_KV_LORA), lambda i, j: (i, j, ckv_blk)),
                  pl.BlockSpec((1, tm, LANES), lambda i, j: (i, j, kr_blk)),
                  pl.BlockSpec((1, tm, 1), lambda i, j: (i, j, 0)),
                  vec(LANES), vec(MLA_Q_LORA), vec(MLA_KV_LORA), vec(MLA_NOPE), vec(MLA_NOPE), vec(LANES),
                  vec(LANES),
                  pl.BlockSpec(wq.shape, lambda i, j: (0, 0)),
                  pl.BlockSpec(wkv.shape, lambda i, j: (0, 0))],
        out_specs=[head_out(MLA_QK_PAD), head_out(MLA_NOPE),
                   pl.BlockSpec((1, tm, LANES), lambda i, j: (i, j, 0)), head_out(MLA_V)],
        out_shape=[jax.ShapeDtypeStruct((b, hh, l, MLA_QK_PAD), BF16),
                   jax.ShapeDtypeStruct((b, hh, l, MLA_NOPE), BF16),
                   jax.ShapeDtypeStruct((b, l, LANES), BF16),
                   jax.ShapeDtypeStruct((b, hh, l, MLA_V), BF16)],
        compiler_params=_cparams(("parallel", "parallel")),
        name="mla_qkv",
    )(proj, proj, proj, positions.reshape(b, l, 1), invf, gq.reshape(1, -1), gkv.reshape(1, -1),
      gqn.reshape(1, -1), gkn.reshape(1, -1), pad_gain(gqr), pad_gain(gkr), wq, wkv)


def _flash_kernel(q_ref, kn_ref, kr_ref, v_ref, o_ref, m_ref, l_ref, acc_ref, *, tq, hp):
    qi = pl.program_id(2)
    m_ref[...] = jnp.full(m_ref.shape, -jnp.inf, F32)
    l_ref[...] = jnp.zeros(l_ref.shape, F32)
    acc_ref[...] = jnp.zeros(acc_ref.shape, F32)

    def block(j, diagonal):
        rows = pl.ds(pl.multiple_of(j * tq, tq), tq)
        k_rope = kr_ref[0, rows, :]
        for h in range(hp):
            k = jnp.concatenate([kn_ref[0, h, rows, :], k_rope], axis=-1)
            s = lax.dot_general(q_ref[0, h], k, (((1,), (1,)), ((), ())), preferred_element_type=F32)
            if diagonal:
                row = lax.broadcasted_iota(jnp.int32, s.shape, 0)
                col = lax.broadcasted_iota(jnp.int32, s.shape, 1)
                s = jnp.where(col <= row, s, jnp.finfo(F32).min)
            m = m_ref[h]
            m_new = jnp.maximum(m, jnp.max(s, axis=-1, keepdims=True))
            alpha = jnp.exp(m - m_new)
            p = jnp.exp(s - m_new)
            l_ref[h] = alpha * l_ref[h] + jnp.sum(p, axis=-1, keepdims=True)
            acc_ref[h] = alpha * acc_ref[h] + jnp.dot(p.astype(BF16), v_ref[0, h, rows, :],
                                                      preferred_element_type=F32)
            m_ref[h] = m_new

    def body(j, carry):
        block(j, False)
        return carry

    lax.fori_loop(0, qi, body, 0)
    block(qi, True)
    for h in range(hp):
        o_ref[0, :, h * MLA_V:(h + 1) * MLA_V] = (acc_ref[h] / l_ref[h]).astype(o_ref.dtype)


def _flash(q, kn, kr, v, *, tq, hp):
    b, hh, l, _ = q.shape
    return pl.pallas_call(
        functools.partial(_flash_kernel, tq=tq, hp=hp),
        grid=(b, hh // hp, l // tq),
        in_specs=[pl.BlockSpec((1, hp, tq, MLA_QK_PAD), lambda i, h, j: (i, h, j, 0)),
                  pl.BlockSpec((1, hp, l, MLA_NOPE), lambda i, h, j: (i, h, 0, 0)),
                  pl.BlockSpec((1, l, LANES), lambda i, h, j: (i, 0, 0)),
                  pl.BlockSpec((1, hp, l, MLA_V), lambda i, h, j: (i, h, 0, 0))],
        out_specs=pl.BlockSpec((1, tq, hp * MLA_V), lambda i, h, j: (i, j, h)),
        out_shape=jax.ShapeDtypeStruct((b, l, hh * MLA_V), BF16),
        scratch_shapes=[pltpu.VMEM((hp, tq, 1), F32), pltpu.VMEM((hp, tq, 1), F32),
                        pltpu.VMEM((hp, tq, MLA_V), F32)],
        compiler_params=_cparams(("parallel", "parallel", "parallel")),
        name="flash",
    )(q, kn, kr, v)


def _s5_layer(x, mem, ln, w_in, lam_re, lam_im, log_step, b_re, b_im, c_re, c_im, d, w_glu,
              w_out, mem_norm, w_mem_kv, xq_norm, xk_norm):
    b, l, dm = x.shape
    proj = _norm_matmul(x.reshape(b * l, dm), ln, w_in.astype(BF16), tm=512, col_chunk=512, name="s5_in_proj")
    proj = proj.reshape(b, l, -1)
    a_re, a_im, bb_re, bb_im = _s5_params(lam_re, lam_im, log_step, b_re, b_im)
    to_in = lambda w: _slab_block_diag(w.reshape(S5_GROUPS, S5_STATE, S5_GROUP_CH).transpose(0, 2, 1))
    bw = jnp.concatenate([to_in(bb_re), to_in(bb_im)], axis=-1).astype(BF16)
    to_out = lambda w: _slab_block_diag(w.transpose(0, 2, 1)).astype(BF16)
    row = lambda a: a.reshape(S5_SLABS, 1, S5_SLAB_STATE)
    yg = _s5_mix(proj, bw, to_out(c_re), to_out(c_im), row(a_re), row(a_im),
                 d.reshape(S5_SLABS, 1, LANES), tt=64)
    y = _glu(yg.reshape(b * l, PRIMARY_WIDTH), w_glu.astype(BF16), tm=512, col_chunk=256)
    mk, mv = _mem_kv(mem, mem_norm, w_mem_kv, xk_norm)
    return _merge(x, y.reshape(b, l, PRIMARY_WIDTH), proj, PRIMARY_WIDTH // XQ_WIDTH, 1, mk, mv, xq_norm,
                  w_out.astype(BF16), tm=512)


def _mla_layer(x, mem, positions, ln, w_in, q_lora_norm, kv_lora_norm, w_uq, w_ukv, q_nope_norm, k_nope_norm,
               q_rope_norm, k_rope_norm, w_out, mem_norm, w_mem_kv, xq_norm, xk_norm):
    b, l, dm = x.shape
    o1 = MLA_Q_LORA
    o2 = o1 + MLA_KV_LORA
    o3 = o2 + MLA_ROPE
    o4 = o3 + XQ_WIDTH
    w_perm = jnp.concatenate([w_in[:, o4:], w_in[:, :o1], w_in[:, o3:o4], w_in[:, o1:o2], w_in[:, o2:o3],
                              jnp.zeros((dm, LANES - MLA_ROPE), w_in.dtype)], axis=1).astype(BF16)
    proj = _norm_matmul(x.reshape(b * l, dm), ln, w_perm, tm=512, col_chunk=384, name="mla_in_proj")
    proj = proj.reshape(b, l, -1)
    gate_blk = 0
    cq_blk = BRANCH_WIDTH // MLA_Q_LORA
    xq_blk = (BRANCH_WIDTH + MLA_Q_LORA) // XQ_WIDTH
    ckv_blk = (BRANCH_WIDTH + MLA_Q_LORA + XQ_WIDTH) // MLA_KV_LORA
    kr_blk = (BRANCH_WIDTH + MLA_Q_LORA + XQ_WIDTH + MLA_KV_LORA) // LANES
    wq = w_uq.reshape(MLA_Q_LORA, MLA_HEADS, MLA_NOPE + MLA_ROPE)
    wq = jnp.pad(wq, ((0, 0), (0, 0), (0, MLA_QK_PAD - MLA_NOPE - MLA_ROPE)))
    wq = wq.reshape(MLA_Q_LORA, MLA_HEADS * MLA_QK_PAD).astype(BF16)
    q, kn, kr, v = _mla_qkv(proj, cq_blk, ckv_blk, kr_blk, positions,
                            (q_lora_norm, kv_lora_norm, q_nope_norm, k_nope_norm, q_rope_norm, k_rope_norm),
                            wq, w_ukv.astype(BF16), tm=256)
    attn = _flash(q, kn, kr, v, tq=512, hp=2)
    mk, mv = _mem_kv(mem, mem_norm, w_mem_kv, xk_norm)
    return _merge(x, attn, proj, xq_blk, gate_blk, mk, mv, xq_norm, w_out.astype(BF16), tm=512)


def kernel(x, mem, positions, ln_gain, w_out, mem_norm, w_mem_kv, xq_norm, xk_norm,
           s5_w_in, s5_lambda_re, s5_lambda_im, s5_log_step, s5_b_re, s5_b_im, s5_c_re, s5_c_im,
           s5_d, s5_w_glu, mla_w_in, mla_q_lora_norm, mla_kv_lora_norm, mla_w_uq, mla_w_ukv,
           mla_q_nope_norm, mla_k_nope_norm, mla_q_rope_norm, mla_k_rope_norm):
    depth = ln_gain.shape[0]
    for i in range(depth):
        j = i // 2
        if i % 2 == 0:
            x = _s5_layer(x, mem, ln_gain[i], s5_w_in[j], s5_lambda_re[j], s5_lambda_im[j], s5_log_step[j],
                          s5_b_re[j], s5_b_im[j], s5_c_re[j], s5_c_im[j], s5_d[j], s5_w_glu[j],
                          w_out[i], mem_norm[i], w_mem_kv[i], xq_norm[i], xk_norm[i])
        else:
            x = _mla_layer(x, mem, positions, ln_gain[i], mla_w_in[j], mla_q_lora_norm[j], mla_kv_lora_norm[j],
                           mla_w_uq[j], mla_w_ukv[j], mla_q_nope_norm[j], mla_k_nope_norm[j],
                           mla_q_rope_norm[j], mla_k_rope_norm[j],
                           w_out[i], mem_norm[i], w_mem_kv[i], xq_norm[i], xk_norm[i])
    return x
```

```python
import functools
import math

import jax
import jax.numpy as jnp
from jax import lax
from jax.experimental import pallas as pl
from jax.experimental.pallas import tpu as pltpu

D_MODEL = 1024
BRANCH_WIDTH = 2 * D_MODEL
XQ_WIDTH = BRANCH_WIDTH // 4
PRIMARY_WIDTH = BRANCH_WIDTH - XQ_WIDTH
X_HEADS = 4
X_HEAD_DIM = XQ_WIDTH // X_HEADS
S5_GROUP_CH = 16
S5_GROUPS = PRIMARY_WIDTH // S5_GROUP_CH
S5_STATE = 64
MLA_NOPE = 128
MLA_ROPE = 64
MLA_V = 128
MLA_HEADS = PRIMARY_WIDTH // MLA_V
MLA_Q_LORA = D_MODEL // 2
MLA_KV_LORA = D_MODEL // 4
ROPE_THETA = 10000.0
EPS = 1e-6

LANES = 128
MLA_QK_PAD = 2 * LANES
S5_SLAB_GROUPS = LANES // S5_GROUP_CH
S5_SLABS = S5_GROUPS // S5_SLAB_GROUPS
S5_SLAB_STATE = S5_SLAB_GROUPS * S5_STATE
VMEM_LIMIT = 56 * 1024 * 1024

F32 = jnp.float32
BF16 = jnp.bfloat16


def _cparams(sem):
    return pltpu.CompilerParams(dimension_semantics=sem, vmem_limit_bytes=VMEM_LIMIT)


def _rms(x, g):
    return x * lax.rsqrt(jnp.mean(x * x, axis=-1, keepdims=True) + EPS) * g


def _norm_matmul_kernel(x_ref, g_ref, w_ref, o_ref, *, col_chunk):
    xn = _rms(x_ref[...].astype(F32), g_ref[...]).astype(BF16)
    for c in range(o_ref.shape[1] // col_chunk):
        sl = slice(c * col_chunk, (c + 1) * col_chunk)
        o_ref[:, sl] = jnp.dot(xn, w_ref[:, sl], preferred_element_type=F32).astype(o_ref.dtype)


def _norm_matmul(x, g, w, *, tm, col_chunk, name):
    n, d = x.shape
    wout = w.shape[1]
    return pl.pallas_call(
        functools.partial(_norm_matmul_kernel, col_chunk=col_chunk),
        grid=(n // tm,),
        in_specs=[pl.BlockSpec((tm, d), lambda i: (i, 0)),
                  pl.BlockSpec((1, d), lambda i: (0, 0)),
                  pl.BlockSpec((d, wout), lambda i: (0, 0))],
        out_specs=pl.BlockSpec((tm, wout), lambda i: (i, 0)),
        out_shape=jax.ShapeDtypeStruct((n, wout), BF16),
        compiler_params=_cparams(("parallel",)),
        name=name,
    )(x, g.reshape(1, d), w)


def _mem_kv_kernel(m_ref, g_ref, w_ref, kg_ref, k_ref, v_ref):
    mn = _rms(m_ref[0], g_ref[...]).astype(BF16)
    kv = jnp.dot(mn, w_ref[...], preferred_element_type=F32)
    for h in range(X_HEADS):
        sl = slice(h * X_HEAD_DIM, (h + 1) * X_HEAD_DIM)
        k_ref[0, :, sl] = _rms(kv[:, sl], kg_ref[...]).astype(BF16)
    v_ref[0] = kv[:, XQ_WIDTH:].astype(BF16)


def _mem_kv(mem, mem_norm, w_mem_kv, xk_norm):
    b, m, d = mem.shape
    out = jax.ShapeDtypeStruct((b, m, XQ_WIDTH), BF16)
    return pl.pallas_call(
        _mem_kv_kernel,
        grid=(b,),
        in_specs=[pl.BlockSpec((1, m, d), lambda i: (i, 0, 0)),
                  pl.BlockSpec((1, d), lambda i: (0, 0)),
                  pl.BlockSpec((d, 2 * XQ_WIDTH), lambda i: (0, 0)),
                  pl.BlockSpec((1, X_HEAD_DIM), lambda i: (0, 0))],
        out_specs=[pl.BlockSpec((1, m, XQ_WIDTH), lambda i: (i, 0, 0)),
                   pl.BlockSpec((1, m, XQ_WIDTH), lambda i: (i, 0, 0))],
        out_shape=[out, out],
        compiler_params=_cparams(("parallel",)),
        name="mem_kv",
    )(mem, mem_norm.reshape(1, d), w_mem_kv.astype(BF16), xk_norm.reshape(1, X_HEAD_DIM))


def _s5_params_kernel(lr_ref, li_ref, ls_ref, lrc_ref, lic_ref, lsc_ref, br_ref, bi_ref,
                      ar_ref, ai_ref, bbr_ref, bbi_ref):
    def abar(lr, li, ls):
        step = jnp.exp(ls)
        mag = jnp.exp(lr * step)
        return mag * jnp.cos(li * step), mag * jnp.sin(li * step)

    ar, ai = abar(lr_ref[...], li_ref[...], ls_ref[...])
    ar_ref[...] = ar
    ai_ref[...] = ai
    lr, li = lrc_ref[...], lic_ref[...]
    ar, ai = abar(lr, li, lsc_ref[...])
    den = lr * lr + li * li
    mr = ((ar - 1.0) * lr + ai * li) / den
    mi = (ai * lr - (ar - 1.0) * li) / den
    br, bi = br_ref[...], bi_ref[...]
    bbr_ref[...] = mr * br - mi * bi
    bbi_ref[...] = mr * bi + mi * br


def _s5_params(lam_re, lam_im, log_step, b_re, b_im):
    g, p = lam_re.shape
    c = b_re.shape[-1]
    col = lambda a: a.reshape(g * p, 1)
    ls_col = jnp.broadcast_to(log_step[:, None], (g, p))
    return pl.pallas_call(
        _s5_params_kernel,
        out_shape=[jax.ShapeDtypeStruct((g, p), F32), jax.ShapeDtypeStruct((g, p), F32),
                   jax.ShapeDtypeStruct((g * p, c), F32), jax.ShapeDtypeStruct((g * p, c), F32)],
        name="s5_params",
    )(lam_re, lam_im, log_step.reshape(g, 1), col(lam_re), col(lam_im), col(ls_col),
      b_re.reshape(g * p, c), b_im.reshape(g * p, c))


def _slab_block_diag(w):
    _, r, c = w.shape
    w = w.reshape(S5_SLABS, S5_SLAB_GROUPS, r, c)
    eye = jnp.eye(S5_SLAB_GROUPS, dtype=w.dtype)
    return jnp.einsum('sgrc,gh->sgrhc', w, eye).reshape(S5_SLABS, S5_SLAB_GROUPS * r, S5_SLAB_GROUPS * c)


def _s5_mix_kernel(u_ref, bw_ref, cre_ref, cim_ref, ar_ref, ai_ref, d_ref, o_ref, bu_ref, h_ref, *, tt):
    nb = u_ref.shape[0]
    ns = S5_SLAB_STATE

    @pl.when(pl.program_id(0) == 0)
    def _():
        h_ref[...] = jnp.zeros_like(h_ref)

    def slab(j, carry):
        col = pl.multiple_of(j * LANES, LANES)
        u = jnp.swapaxes(u_ref[:, :, pl.ds(col, LANES)].astype(F32), 0, 1).reshape(tt * nb, LANES)
        bu_ref[...] = jnp.dot(u.astype(BF16), bw_ref[j], preferred_element_type=F32)
        a_re = jnp.broadcast_to(ar_ref[j], (nb, ns))
        a_im = jnp.broadcast_to(ai_ref[j], (nb, ns))

        def step(t, hc):
            h_re, h_im = hc
            rows = pl.ds(pl.multiple_of(t * nb, nb), nb)
            n_re = a_re * h_re - a_im * h_im + bu_ref[rows, :ns]
            n_im = a_re * h_im + a_im * h_re + bu_ref[rows, ns:]
            bu_ref[rows, :ns] = n_re
            bu_ref[rows, ns:] = n_im
            return n_re, n_im

        h_re, h_im = lax.fori_loop(0, tt, step, (h_ref[j, :, :ns], h_ref[j, :, ns:]), unroll=8)
        h_ref[j, :, :ns] = h_re
        h_ref[j, :, ns:] = h_im
        y = (jnp.dot(bu_ref[:, :ns].astype(BF16), cre_ref[j], preferred_element_type=F32)
             - jnp.dot(bu_ref[:, ns:].astype(BF16), cim_ref[j], preferred_element_type=F32))
        g = jax.nn.gelu(y + d_ref[j] * u)
        o_ref[:, :, pl.ds(col, LANES)] = jnp.swapaxes(g.reshape(tt, nb, LANES), 0, 1).astype(o_ref.dtype)
        return carry

    lax.fori_loop(0, S5_SLABS, slab, 0)


def _s5_mix(proj, bw, cre, cim, a_re, a_im, d, *, tt):
    b, l, _ = proj.shape
    ns = S5_SLAB_STATE
    full3 = lambda a: pl.BlockSpec(a.shape, lambda i: (0, 0, 0))
    return pl.pallas_call(
        functools.partial(_s5_mix_kernel, tt=tt),
        grid=(l // tt,),
        in_specs=[pl.BlockSpec((b, tt, PRIMARY_WIDTH), lambda i: (0, i, 0)),
                  full3(bw), full3(cre), full3(cim), full3(a_re), full3(a_im), full3(d)],
        out_specs=pl.BlockSpec((b, tt, PRIMARY_WIDTH), lambda i: (0, i, 0)),
        out_shape=jax.ShapeDtypeStruct((b, l, PRIMARY_WIDTH), BF16),
        scratch_shapes=[pltpu.VMEM((b * tt, 2 * ns), F32),
                        pltpu.VMEM((S5_SLABS, b, 2 * ns), F32)],
        compiler_params=_cparams(("arbitrary",)),
        name="s5_mix",
    )(proj, bw, cre, cim, a_re, a_im, d)


def _glu_kernel(y_ref, w_ref, o_ref, *, col_chunk):
    y = y_ref[...]
    half = o_ref.shape[1]
    for c in range(half // col_chunk):
        a = jnp.dot(y, w_ref[:, c * col_chunk:(c + 1) * col_chunk], preferred_element_type=F32)
        g = jnp.dot(y, w_ref[:, half + c * col_chunk:half + (c + 1) * col_chunk], preferred_element_type=F32)
        o_ref[:, c * col_chunk:(c + 1) * col_chunk] = (a * jax.nn.sigmoid(g)).astype(o_ref.dtype)


def _glu(y, w, *, tm, col_chunk):
    n, k = y.shape
    half = w.shape[1] // 2
    return pl.pallas_call(
        functools.partial(_glu_kernel, col_chunk=col_chunk),
        grid=(n // tm,),
        in_specs=[pl.BlockSpec((tm, k), lambda i: (i, 0)),
                  pl.BlockSpec(w.shape, lambda i: (0, 0))],
        out_specs=pl.BlockSpec((tm, half), lambda i: (i, 0)),
        out_shape=jax.ShapeDtypeStruct((n, half), BF16),
        compiler_params=_cparams(("parallel",)),
        name="glu",
    )(y, w)


def _merge_kernel(x_ref, mix_ref, xq_ref, gate_ref, k_ref, v_ref, qg_ref, w_ref, o_ref, cat_ref):
    gate = gate_ref[0].astype(F32)
    sg = gate * jax.nn.sigmoid(gate)
    cat_ref[:, :PRIMARY_WIDTH] = (mix_ref[0].astype(F32) * sg[:, :PRIMARY_WIDTH]).astype(BF16)
    scale = X_HEAD_DIM ** -0.5
    for h in range(X_HEADS):
        sl = slice(h * X_HEAD_DIM, (h + 1) * X_HEAD_DIM)
        q = _rms(xq_ref[0, :, sl].astype(F32), qg_ref[...]).astype(BF16)
        s = lax.dot_general(q, k_ref[0, :, sl], (((1,), (1,)), ((), ())), preferred_element_type=F32) * scale
        p = jnp.exp(s - jnp.max(s, axis=-1, keepdims=True))
        p = (p / jnp.sum(p, axis=-1, keepdims=True)).astype(BF16)
        mo = jnp.dot(p, v_ref[0, :, sl], preferred_element_type=F32)
        osl = slice(PRIMARY_WIDTH + h * X_HEAD_DIM, PRIMARY_WIDTH + (h + 1) * X_HEAD_DIM)
        cat_ref[:, osl] = (mo * sg[:, osl]).astype(BF16)
    o_ref[0] = x_ref[0] + jnp.dot(cat_ref[...], w_ref[...], preferred_element_type=F32)


def _merge(x, mix, proj, xq_blk, gate_blk, mk, mv, xq_norm, w_out, *, tm):
    b, l, d = x.shape
    m = mk.shape[1]
    return pl.pallas_call(
        _merge_kernel,
        grid=(b, l // tm),
        in_specs=[pl.BlockSpec((1, tm, d), lambda i, j: (i, j, 0)),
                  pl.BlockSpec((1, tm, PRIMARY_WIDTH), lambda i, j: (i, j, 0)),
                  pl.BlockSpec((1, tm, XQ_WIDTH), lambda i, j: (i, j, xq_blk)),
                  pl.BlockSpec((1, tm, BRANCH_WIDTH), lambda i, j: (i, j, gate_blk)),
                  pl.BlockSpec((1, m, XQ_WIDTH), lambda i, j: (i, 0, 0)),
                  pl.BlockSpec((1, m, XQ_WIDTH), lambda i, j: (i, 0, 0)),
                  pl.BlockSpec((1, X_HEAD_DIM), lambda i, j: (0, 0)),
                  pl.BlockSpec((BRANCH_WIDTH, d), lambda i, j: (0, 0))],
        out_specs=pl.BlockSpec((1, tm, d), lambda i, j: (i, j, 0)),
        out_shape=jax.ShapeDtypeStruct((b, l, d), F32),
        scratch_shapes=[pltpu.VMEM((tm, BRANCH_WIDTH), BF16)],
        compiler_params=_cparams(("parallel", "parallel")),
        name="merge",
    )(x, mix, proj, proj, mk, mv, xq_norm.reshape(1, X_HEAD_DIM), w_out)


def _mla_qkv_kernel(cq_ref, ckv_ref, kr_ref, posc_ref, posr_ref, invfr_ref, invfc_ref, gq_ref, gkv_ref, gqn_ref,
                    gkn_ref, gqr_ref, gkr_ref, wqt_ref, wk_ref, wvt_ref, qt_ref, kn_ref, krope_ref, vt_ref):
    half = MLA_ROPE // 2
    tm = cq_ref.shape[1]
    qscale = (MLA_NOPE + MLA_ROPE) ** -0.5 * math.log2(math.e)

    cq = _rms(cq_ref[0].astype(F32), gq_ref[...])
    ckv = _rms(ckv_ref[0].astype(F32), gkv_ref[...])
    cq_t = cq.T.astype(BF16)
    ckv_t = ckv.T.astype(BF16)
    q_t = jnp.dot(wqt_ref[...], cq_t, preferred_element_type=F32)
    v_t = jnp.dot(wvt_ref[...], ckv_t, preferred_element_type=F32)
    k_n = jnp.dot(ckv.astype(BF16), wk_ref[...], preferred_element_type=F32)

    ang_t = invfc_ref[...] * posr_ref[0].astype(F32)
    cos_t, sin_t = jnp.cos(ang_t), jnp.sin(ang_t)
    g_nope = jnp.broadcast_to(gqn_ref[...], (MLA_NOPE, tm)) * qscale
    g_r1 = jnp.broadcast_to(gqr_ref[:half, :], (half, tm)) * qscale
    g_r2 = jnp.broadcast_to(gqr_ref[half:, :], (half, tm)) * qscale
    for h in range(MLA_HEADS):
        q = q_t[h * MLA_QK_PAD:(h + 1) * MLA_QK_PAD]
        nope = q[:MLA_NOPE]
        r = lax.rsqrt(jnp.mean(nope * nope, axis=0, keepdims=True) + EPS)
        qt_ref[0, h, :MLA_NOPE, :] = (nope * r * g_nope).astype(BF16)
        x1, x2 = q[MLA_NOPE:MLA_NOPE + half], q[MLA_NOPE + half:MLA_NOPE + MLA_ROPE]
        ss = jnp.sum(x1 * x1, axis=0, keepdims=True) + jnp.sum(x2 * x2, axis=0, keepdims=True)
        r = lax.rsqrt(ss * (1.0 / MLA_ROPE) + EPS)
        x1, x2 = x1 * r * g_r1, x2 * r * g_r2
        qt_ref[0, h, MLA_NOPE:MLA_NOPE + half, :] = (x1 * cos_t - x2 * sin_t).astype(BF16)
        qt_ref[0, h, MLA_NOPE + half:MLA_NOPE + MLA_ROPE, :] = (x1 * sin_t + x2 * cos_t).astype(BF16)
        qt_ref[0, h, MLA_NOPE + MLA_ROPE:, :] = jnp.zeros((MLA_QK_PAD - MLA_NOPE - MLA_ROPE, tm), BF16)
        kn_ref[0, h] = _rms(k_n[:, h * MLA_NOPE:(h + 1) * MLA_NOPE], gkn_ref[...]).astype(BF16)
        vt_ref[0, h] = v_t[h * MLA_V:(h + 1) * MLA_V].astype(BF16)

    ang = posc_ref[0].astype(F32) * invfr_ref[...]
    lane = lax.broadcasted_iota(jnp.int32, ang.shape, 1)
    cos = jnp.where(lane < MLA_ROPE, jnp.cos(ang), 0.0)
    sin = jnp.sin(ang)
    sin_lo = jnp.where(lane < half, -sin, 0.0)
    sin_hi = jnp.where((lane >= half) & (lane < MLA_ROPE), sin, 0.0)
    x = kr_ref[0].astype(F32)
    xn = x * lax.rsqrt(jnp.sum(x * x, axis=-1, keepdims=True) * (1.0 / MLA_ROPE) + EPS) * gkr_ref[...]
    krope_ref[0] = (xn * cos + pltpu.roll(xn, LANES - half, 1) * sin_lo
                    + pltpu.roll(xn, half, 1) * sin_hi).astype(BF16)


def _mla_qkv(proj, cq_blk, ckv_blk, kr_blk, positions, gains, wqt, wk, wvt, *, tm):
    b, l, _ = proj.shape
    hh = MLA_HEADS
    half = MLA_ROPE // 2
    inv_freq = ROPE_THETA ** (-jnp.arange(half, dtype=F32) / half)
    invf_row = jnp.concatenate([inv_freq, inv_freq, jnp.zeros((LANES - MLA_ROPE,), F32)]).reshape(1, LANES)
    const = lambda a: pl.BlockSpec(a.shape, lambda i, j: (0,) * a.ndim)
    gq, gkv, gqn, gkn, gqr, gkr = gains
    consts = [invf_row, inv_freq.reshape(half, 1), gq.reshape(1, -1), gkv.reshape(1, -1), gqn.reshape(-1, 1),
              gkn.reshape(1, -1), gqr.reshape(-1, 1),
              jnp.concatenate([gkr, jnp.zeros((LANES - MLA_ROPE,), F32)]).reshape(1, LANES), wqt, wk, wvt]
    return pl.pallas_call(
        _mla_qkv_kernel,
        grid=(b, l // tm),
        in_specs=[pl.BlockSpec((1, tm, MLA_Q_LORA), lambda i, j: (i, j, cq_blk)),
                  pl.BlockSpec((1, tm, MLA_KV_LORA), lambda i, j: (i, j, ckv_blk)),
                  pl.BlockSpec((1, tm, LANES), lambda i, j: (i, j, kr_blk)),
                  pl.BlockSpec((1, tm, 1), lambda i, j: (i, j, 0)),
                  pl.BlockSpec((1, 1, tm), lambda i, j: (i, 0, j))] + [const(a) for a in consts],
        out_specs=[pl.BlockSpec((1, hh, MLA_QK_PAD, tm), lambda i, j: (i, 0, 0, j)),
                   pl.BlockSpec((1, hh, tm, MLA_NOPE), lambda i, j: (i, 0, j, 0)),
                   pl.BlockSpec((1, tm, LANES), lambda i, j: (i, j, 0)),
                   pl.BlockSpec((1, hh, MLA_V, tm), lambda i, j: (i, 0, 0, j))],
        out_shape=[jax.ShapeDtypeStruct((b, hh, MLA_QK_PAD, l), BF16),
                   jax.ShapeDtypeStruct((b, hh, l, MLA_NOPE), BF16),
                   jax.ShapeDtypeStruct((b, l, LANES), BF16),
                   jax.ShapeDtypeStruct((b, hh, MLA_V, l), BF16)],
        compiler_params=_cparams(("parallel", "parallel")),
        name="mla_qkv",
    )(proj, proj, proj, positions.reshape(b, l, 1), positions.reshape(b, 1, l), *consts)


def _flash_kernel(qt_ref, kn_ref, kr_ref, vt_ref, o_ref, m_ref, l_ref, acc_ref, *, tq, hp):
    qi = pl.program_id(2)
    m_ref[...] = jnp.full(m_ref.shape, -jnp.inf, F32)
    l_ref[...] = jnp.zeros(l_ref.shape, F32)
    acc_ref[...] = jnp.zeros(acc_ref.shape, F32)

    def block(j, diagonal):
        rows = pl.ds(pl.multiple_of(j * tq, tq), tq)
        k_rope = kr_ref[0, rows, :]
        for h in range(hp):
            k = jnp.concatenate([kn_ref[0, h, rows, :], k_rope], axis=-1)
            s = jnp.dot(k, qt_ref[0, h], preferred_element_type=F32)
            if diagonal:
                key = lax.broadcasted_iota(jnp.int32, s.shape, 0)
                qry = lax.broadcasted_iota(jnp.int32, s.shape, 1)
                s = jnp.where(key <= qry, s, jnp.finfo(F32).min)
            m = m_ref[h]
            m_new = jnp.maximum(m, jnp.max(s, axis=0, keepdims=True))
            alpha = jnp.exp2(m - m_new)
            p = jnp.exp2(s - m_new)
            l_ref[h] = alpha * l_ref[h] + jnp.sum(p, axis=0, keepdims=True)
            acc_ref[h] = alpha * acc_ref[h] + jnp.dot(vt_ref[0, h, :, rows], p.astype(BF16),
                                                      preferred_element_type=F32)
            m_ref[h] = m_new

    def body(j, carry):
        block(j, False)
        return carry

    lax.fori_loop(0, qi, body, 0)
    block(qi, True)
    for h in range(hp):
        o_ref[0, :, h * MLA_V:(h + 1) * MLA_V] = (acc_ref[h] / l_ref[h]).T.astype(o_ref.dtype)


def _flash(qt, kn, kr, vt, *, tq, hp):
    b, hh, _, l = qt.shape
    return pl.pallas_call(
        functools.partial(_flash_kernel, tq=tq, hp=hp),
        grid=(b, hh // hp, l // tq),
        in_specs=[pl.BlockSpec((1, hp, MLA_QK_PAD, tq), lambda i, h, j: (i, h, 0, j)),
                  pl.BlockSpec((1, hp, l, MLA_NOPE), lambda i, h, j: (i, h, 0, 0)),
                  pl.BlockSpec((1, l, LANES), lambda i, h, j: (i, 0, 0)),
                  pl.BlockSpec((1, hp, MLA_V, l), lambda i, h, j: (i, h, 0, 0))],
        out_specs=pl.BlockSpec((1, tq, hp * MLA_V), lambda i, h, j: (i, j, h)),
        out_shape=jax.ShapeDtypeStruct((b, l, hh * MLA_V), BF16),
        scratch_shapes=[pltpu.VMEM((hp, 1, tq), F32), pltpu.VMEM((hp, 1, tq), F32),
                        pltpu.VMEM((hp, MLA_V, tq), F32)],
        compiler_params=_cparams(("parallel", "parallel", "parallel")),
        name="flash",
    )(qt, kn, kr, vt)


def _s5_layer(x, mem, ln, w_in, lam_re, lam_im, log_step, b_re, b_im, c_re, c_im, d, w_glu,
              w_out, mem_norm, w_mem_kv, xq_norm, xk_norm):
    b, l, dm = x.shape
    proj = _norm_matmul(x.reshape(b * l, dm), ln, w_in.astype(BF16), tm=512, col_chunk=512, name="s5_in_proj")
    proj = proj.reshape(b, l, -1)
    a_re, a_im, bb_re, bb_im = _s5_params(lam_re, lam_im, log_step, b_re, b_im)
    to_in = lambda w: _slab_block_diag(w.reshape(S5_GROUPS, S5_STATE, S5_GROUP_CH).transpose(0, 2, 1))
    bw = jnp.concatenate([to_in(bb_re), to_in(bb_im)], axis=-1).astype(BF16)
    to_out = lambda w: _slab_block_diag(w.transpose(0, 2, 1)).astype(BF16)
    row = lambda a: a.reshape(S5_SLABS, 1, S5_SLAB_STATE)
    yg = _s5_mix(proj, bw, to_out(c_re), to_out(c_im), row(a_re), row(a_im),
                 d.reshape(S5_SLABS, 1, LANES), tt=64)
    y = _glu(yg.reshape(b * l, PRIMARY_WIDTH), w_glu.astype(BF16), tm=512, col_chunk=256)
    mk, mv = _mem_kv(mem, mem_norm, w_mem_kv, xk_norm)
    return _merge(x, y.reshape(b, l, PRIMARY_WIDTH), proj, PRIMARY_WIDTH // XQ_WIDTH, 1, mk, mv, xq_norm,
                  w_out.astype(BF16), tm=512)


def _mla_layer(x, mem, positions, ln, w_in, q_lora_norm, kv_lora_norm, w_uq, w_ukv, q_nope_norm, k_nope_norm,
               q_rope_norm, k_rope_norm, w_out, mem_norm, w_mem_kv, xq_norm, xk_norm):
    b, l, dm = x.shape
    o1 = MLA_Q_LORA
    o2 = o1 + MLA_KV_LORA
    o3 = o2 + MLA_ROPE
    o4 = o3 + XQ_WIDTH
    w_perm = jnp.concatenate([w_in[:, o4:], w_in[:, :o1], w_in[:, o3:o4], w_in[:, o1:o2], w_in[:, o2:o3],
                              jnp.zeros((dm, LANES - MLA_ROPE), w_in.dtype)], axis=1).astype(BF16)
    proj = _norm_matmul(x.reshape(b * l, dm), ln, w_perm, tm=512, col_chunk=384, name="mla_in_proj")
    proj = proj.reshape(b, l, -1)
    gate_blk = 0
    cq_blk = BRANCH_WIDTH // MLA_Q_LORA
    xq_blk = (BRANCH_WIDTH + MLA_Q_LORA) // XQ_WIDTH
    ckv_blk = (BRANCH_WIDTH + MLA_Q_LORA + XQ_WIDTH) // MLA_KV_LORA
    kr_blk = (BRANCH_WIDTH + MLA_Q_LORA + XQ_WIDTH + MLA_KV_LORA) // LANES
    wq = w_uq.reshape(MLA_Q_LORA, MLA_HEADS, MLA_NOPE + MLA_ROPE)
    wq = jnp.pad(wq, ((0, 0), (0, 0), (0, MLA_QK_PAD - MLA_NOPE - MLA_ROPE)))
    wqt = wq.reshape(MLA_Q_LORA, MLA_HEADS * MLA_QK_PAD).T.astype(BF16)
    wkv = w_ukv.reshape(MLA_KV_LORA, MLA_HEADS, MLA_NOPE + MLA_V)
    wk = wkv[:, :, :MLA_NOPE].reshape(MLA_KV_LORA, MLA_HEADS * MLA_NOPE).astype(BF16)
    wvt = wkv[:, :, MLA_NOPE:].reshape(MLA_KV_LORA, MLA_HEADS * MLA_V).T.astype(BF16)
    qt, kn, kr, vt = _mla_qkv(proj, cq_blk, ckv_blk, kr_blk, positions,
                              (q_lora_norm, kv_lora_norm, q_nope_norm, k_nope_norm, q_rope_norm, k_rope_norm),
                              wqt, wk, wvt, tm=256)
    attn = _flash(qt, kn, kr, vt, tq=512, hp=2)
    mk, mv = _mem_kv(mem, mem_norm, w_mem_kv, xk_norm)
    return _merge(x, attn, proj, xq_blk, gate_blk, mk, mv, xq_norm, w_out.astype(BF16), tm=512)


def kernel(x, mem, positions, ln_gain, w_out, mem_norm, w_mem_kv, xq_norm, xk_norm,
           s5_w_in, s5_lambda_re, s5_lambda_im, s5_log_step, s5_b_re, s5_b_im, s5_c_re, s5_c_im,
           s5_d, s5_w_glu, mla_w_in, mla_q_lora_norm, mla_kv_lora_norm, mla_w_uq, mla_w_ukv,
           mla_q_nope_norm, mla_k_nope_norm, mla_q_rope_norm, mla_k_rope_norm):
    depth = ln_gain.shape[0]
    for i in range(depth):
        j = i // 2
        if i % 2 == 0:
            x = _s5_layer(x, mem, ln_gain[i], s5_w_in[j], s5_lambda_re[j], s5_lambda_im[j], s5_log_step[j],
                          s5_b_re[j], s5_b_im[j], s5_c_re[j], s5_c_im[j], s5_d[j], s5_w_glu[j],
                          w_out[i], mem_norm[i], w_mem_kv[i], xq_norm[i], xk_norm[i])
        else:
            x = _mla_layer(x, mem, positions, ln_gain[i], mla_w_in[j], mla_q_lora_norm[j], mla_kv_lora_norm[j],
                           mla_w_uq[j], mla_w_ukv[j], mla_q_nope_norm[j], mla_k_nope_norm[j],
                           mla_q_rope_norm[j], mla_k_rope_norm[j],
                           w_out[i], mem_norm[i], w_mem_kv[i], xq_norm[i], xk_norm[i])
    return x
```

```python
import functools
import math

import jax
import jax.numpy as jnp
from jax import lax
from jax.experimental import pallas as pl
from jax.experimental.pallas import tpu as pltpu

D_MODEL = 1024
BRANCH_WIDTH = 2 * D_MODEL
XQ_WIDTH = BRANCH_WIDTH // 4
PRIMARY_WIDTH = BRANCH_WIDTH - XQ_WIDTH
X_HEADS = 4
X_HEAD_DIM = XQ_WIDTH // X_HEADS
S5_GROUP_CH = 16
S5_GROUPS = PRIMARY_WIDTH // S5_GROUP_CH
S5_STATE = 64
MLA_NOPE = 128
MLA_ROPE = 64
MLA_V = 128
MLA_HEADS = PRIMARY_WIDTH // MLA_V
MLA_Q_LORA = D_MODEL // 2
MLA_KV_LORA = D_MODEL // 4
ROPE_THETA = 10000.0
EPS = 1e-6

LANES = 128
MLA_QK_PAD = 2 * LANES
S5_SLAB_GROUPS = LANES // S5_GROUP_CH
S5_SLABS = S5_GROUPS // S5_SLAB_GROUPS
S5_SLAB_STATE = S5_SLAB_GROUPS * S5_STATE
VMEM_LIMIT = 56 * 1024 * 1024

F32 = jnp.float32
BF16 = jnp.bfloat16


def _cparams(sem):
    return pltpu.CompilerParams(dimension_semantics=sem, vmem_limit_bytes=VMEM_LIMIT)


def _rms(x, g):
    return x * lax.rsqrt(jnp.mean(x * x, axis=-1, keepdims=True) + EPS) * g


def _norm_matmul_kernel(x_ref, g_ref, w_ref, o_ref, *, col_chunk):
    xn = _rms(x_ref[...].astype(F32), g_ref[...]).astype(BF16)
    for c in range(o_ref.shape[1] // col_chunk):
        sl = slice(c * col_chunk, (c + 1) * col_chunk)
        o_ref[:, sl] = jnp.dot(xn, w_ref[:, sl], preferred_element_type=F32).astype(o_ref.dtype)


def _norm_matmul(x, g, w, *, tm, col_chunk, name):
    n, d = x.shape
    wout = w.shape[1]
    return pl.pallas_call(
        functools.partial(_norm_matmul_kernel, col_chunk=col_chunk),
        grid=(n // tm,),
        in_specs=[pl.BlockSpec((tm, d), lambda i: (i, 0)),
                  pl.BlockSpec((1, d), lambda i: (0, 0)),
                  pl.BlockSpec((d, wout), lambda i: (0, 0))],
        out_specs=pl.BlockSpec((tm, wout), lambda i: (i, 0)),
        out_shape=jax.ShapeDtypeStruct((n, wout), BF16),
        compiler_params=_cparams(("parallel",)),
        name=name,
    )(x, g.reshape(1, d), w)


def _mem_kv_kernel(m_ref, g_ref, w_ref, kg_ref, k_ref, v_ref):
    mn = _rms(m_ref[0], g_ref[...]).astype(BF16)
    kv = jnp.dot(mn, w_ref[...], preferred_element_type=F32)
    for h in range(X_HEADS):
        sl = slice(h * X_HEAD_DIM, (h + 1) * X_HEAD_DIM)
        k_ref[0, :, sl] = _rms(kv[:, sl], kg_ref[...]).astype(BF16)
    v_ref[0] = kv[:, XQ_WIDTH:].astype(BF16)


def _mem_kv(mem, mem_norm, w_mem_kv, xk_norm):
    b, m, d = mem.shape
    out = jax.ShapeDtypeStruct((b, m, XQ_WIDTH), BF16)
    return pl.pallas_call(
        _mem_kv_kernel,
        grid=(b,),
        in_specs=[pl.BlockSpec((1, m, d), lambda i: (i, 0, 0)),
                  pl.BlockSpec((1, d), lambda i: (0, 0)),
                  pl.BlockSpec((d, 2 * XQ_WIDTH), lambda i: (0, 0)),
                  pl.BlockSpec((1, X_HEAD_DIM), lambda i: (0, 0))],
        out_specs=[pl.BlockSpec((1, m, XQ_WIDTH), lambda i: (i, 0, 0)),
                   pl.BlockSpec((1, m, XQ_WIDTH), lambda i: (i, 0, 0))],
        out_shape=[out, out],
        compiler_params=_cparams(("parallel",)),
        name="mem_kv",
    )(mem, mem_norm.reshape(1, d), w_mem_kv.astype(BF16), xk_norm.reshape(1, X_HEAD_DIM))


def _s5_params_kernel(lr_ref, li_ref, ls_ref, lrc_ref, lic_ref, lsc_ref, br_ref, bi_ref,
                      ar_ref, ai_ref, bbr_ref, bbi_ref):
    def abar(lr, li, ls):
        step = jnp.exp(ls)
        mag = jnp.exp(lr * step)
        return mag * jnp.cos(li * step), mag * jnp.sin(li * step)

    ar, ai = abar(lr_ref[...], li_ref[...], ls_ref[...])
    ar_ref[...] = ar
    ai_ref[...] = ai
    lr, li = lrc_ref[...], lic_ref[...]
    ar, ai = abar(lr, li, lsc_ref[...])
    den = lr * lr + li * li
    mr = ((ar - 1.0) * lr + ai * li) / den
    mi = (ai * lr - (ar - 1.0) * li) / den
    br, bi = br_ref[...], bi_ref[...]
    bbr_ref[...] = mr * br - mi * bi
    bbi_ref[...] = mr * bi + mi * br


def _s5_params(lam_re, lam_im, log_step, b_re, b_im):
    g, p = lam_re.shape
    c = b_re.shape[-1]
    col = lambda a: a.reshape(g * p, 1)
    ls_col = jnp.broadcast_to(log_step[:, None], (g, p))
    return pl.pallas_call(
        _s5_params_kernel,
        out_shape=[jax.ShapeDtypeStruct((g, p), F32), jax.ShapeDtypeStruct((g, p), F32),
                   jax.ShapeDtypeStruct((g * p, c), F32), jax.ShapeDtypeStruct((g * p, c), F32)],
        name="s5_params",
    )(lam_re, lam_im, log_step.reshape(g, 1), col(lam_re), col(lam_im), col(ls_col),
      b_re.reshape(g * p, c), b_im.reshape(g * p, c))


def _slab_block_diag(w):
    _, r, c = w.shape
    w = w.reshape(S5_SLABS, S5_SLAB_GROUPS, r, c)
    eye = jnp.eye(S5_SLAB_GROUPS, dtype=w.dtype)
    return jnp.einsum('sgrc,gh->sgrhc', w, eye).reshape(S5_SLABS, S5_SLAB_GROUPS * r, S5_SLAB_GROUPS * c)


def _s5_mix_kernel(u_ref, bw_ref, cre_ref, cim_ref, ar_ref, ai_ref, d_ref, o_ref, bu_ref, h_ref, *, tt):
    nb = u_ref.shape[0]
    ns = S5_SLAB_STATE

    @pl.when(pl.program_id(0) == 0)
    def _():
        h_ref[...] = jnp.zeros_like(h_ref)

    def slab(j, carry):
        col = pl.multiple_of(j * LANES, LANES)
        u = jnp.swapaxes(u_ref[:, :, pl.ds(col, LANES)].astype(F32), 0, 1).reshape(tt * nb, LANES)
        bu_ref[...] = jnp.dot(u.astype(BF16), bw_ref[j], preferred_element_type=F32)
        a_re = jnp.broadcast_to(ar_ref[j], (nb, ns))
        a_im = jnp.broadcast_to(ai_ref[j], (nb, ns))

        def step(t, hc):
            h_re, h_im = hc
            rows = pl.ds(pl.multiple_of(t * nb, nb), nb)
            n_re = a_re * h_re - a_im * h_im + bu_ref[rows, :ns]
            n_im = a_re * h_im + a_im * h_re + bu_ref[rows, ns:]
            bu_ref[rows, :ns] = n_re
            bu_ref[rows, ns:] = n_im
            return n_re, n_im

        h_re, h_im = lax.fori_loop(0, tt, step, (h_ref[j, :, :ns], h_ref[j, :, ns:]), unroll=8)
        h_ref[j, :, :ns] = h_re
        h_ref[j, :, ns:] = h_im
        y = (jnp.dot(bu_ref[:, :ns].astype(BF16), cre_ref[j], preferred_element_type=F32)
             - jnp.dot(bu_ref[:, ns:].astype(BF16), cim_ref[j], preferred_element_type=F32))
        g = jax.nn.gelu(y + d_ref[j] * u)
        o_ref[:, :, pl.ds(col, LANES)] = jnp.swapaxes(g.reshape(tt, nb, LANES), 0, 1).astype(o_ref.dtype)
        return carry

    lax.fori_loop(0, S5_SLABS, slab, 0)


def _s5_mix(proj, bw, cre, cim, a_re, a_im, d, *, tt):
    b, l, _ = proj.shape
    ns = S5_SLAB_STATE
    full3 = lambda a: pl.BlockSpec(a.shape, lambda i: (0, 0, 0))
    return pl.pallas_call(
        functools.partial(_s5_mix_kernel, tt=tt),
        grid=(l // tt,),
        in_specs=[pl.BlockSpec((b, tt, PRIMARY_WIDTH), lambda i: (0, i, 0)),
                  full3(bw), full3(cre), full3(cim), full3(a_re), full3(a_im), full3(d)],
        out_specs=pl.BlockSpec((b, tt, PRIMARY_WIDTH), lambda i: (0, i, 0)),
        out_shape=jax.ShapeDtypeStruct((b, l, PRIMARY_WIDTH), BF16),
        scratch_shapes=[pltpu.VMEM((b * tt, 2 * ns), F32),
                        pltpu.VMEM((S5_SLABS, b, 2 * ns), F32)],
        compiler_params=_cparams(("arbitrary",)),
        name="s5_mix",
    )(proj, bw, cre, cim, a_re, a_im, d)


def _glu_kernel(y_ref, w_ref, o_ref, *, col_chunk):
    y = y_ref[...]
    half = o_ref.shape[1]
    for c in range(half // col_chunk):
        a = jnp.dot(y, w_ref[:, c * col_chunk:(c + 1) * col_chunk], preferred_element_type=F32)
        g = jnp.dot(y, w_ref[:, half + c * col_chunk:half + (c + 1) * col_chunk], preferred_element_type=F32)
        o_ref[:, c * col_chunk:(c + 1) * col_chunk] = (a * jax.nn.sigmoid(g)).astype(o_ref.dtype)


def _glu(y, w, *, tm, col_chunk):
    n, k = y.shape
    half = w.shape[1] // 2
    return pl.pallas_call(
        functools.partial(_glu_kernel, col_chunk=col_chunk),
        grid=(n // tm,),
        in_specs=[pl.BlockSpec((tm, k), lambda i: (i, 0)),
                  pl.BlockSpec(w.shape, lambda i: (0, 0))],
        out_specs=pl.BlockSpec((tm, half), lambda i: (i, 0)),
        out_shape=jax.ShapeDtypeStruct((n, half), BF16),
        compiler_params=_cparams(("parallel",)),
        name="glu",
    )(y, w)


def _merge_kernel(x_ref, mix_ref, xq_ref, gate_ref, k_ref, v_ref, qg_ref, w_ref, o_ref, cat_ref):
    gate = gate_ref[0].astype(F32)
    sg = gate * jax.nn.sigmoid(gate)
    cat_ref[:, :PRIMARY_WIDTH] = (mix_ref[0].astype(F32) * sg[:, :PRIMARY_WIDTH]).astype(BF16)
    scale = X_HEAD_DIM ** -0.5
    for h in range(X_HEADS):
        sl = slice(h * X_HEAD_DIM, (h + 1) * X_HEAD_DIM)
        q = _rms(xq_ref[0, :, sl].astype(F32), qg_ref[...]).astype(BF16)
        s = lax.dot_general(q, k_ref[0, :, sl], (((1,), (1,)), ((), ())), preferred_element_type=F32) * scale
        p = jnp.exp(s - jnp.max(s, axis=-1, keepdims=True))
        p = (p / jnp.sum(p, axis=-1, keepdims=True)).astype(BF16)
        mo = jnp.dot(p, v_ref[0, :, sl], preferred_element_type=F32)
        osl = slice(PRIMARY_WIDTH + h * X_HEAD_DIM, PRIMARY_WIDTH + (h + 1) * X_HEAD_DIM)
        cat_ref[:, osl] = (mo * sg[:, osl]).astype(BF16)
    o_ref[0] = x_ref[0] + jnp.dot(cat_ref[...], w_ref[...], preferred_element_type=F32)


def _merge(x, mix, proj, xq_blk, gate_blk, mk, mv, xq_norm, w_out, *, tm):
    b, l, d = x.shape
    m = mk.shape[1]
    return pl.pallas_call(
        _merge_kernel,
        grid=(b, l // tm),
        in_specs=[pl.BlockSpec((1, tm, d), lambda i, j: (i, j, 0)),
                  pl.BlockSpec((1, tm, PRIMARY_WIDTH), lambda i, j: (i, j, 0)),
                  pl.BlockSpec((1, tm, XQ_WIDTH), lambda i, j: (i, j, xq_blk)),
                  pl.BlockSpec((1, tm, BRANCH_WIDTH), lambda i, j: (i, j, gate_blk)),
                  pl.BlockSpec((1, m, XQ_WIDTH), lambda i, j: (i, 0, 0)),
                  pl.BlockSpec((1, m, XQ_WIDTH), lambda i, j: (i, 0, 0)),
                  pl.BlockSpec((1, X_HEAD_DIM), lambda i, j: (0, 0)),
                  pl.BlockSpec((BRANCH_WIDTH, d), lambda i, j: (0, 0))],
        out_specs=pl.BlockSpec((1, tm, d), lambda i, j: (i, j, 0)),
        out_shape=jax.ShapeDtypeStruct((b, l, d), F32),
        scratch_shapes=[pltpu.VMEM((tm, BRANCH_WIDTH), BF16)],
        compiler_params=_cparams(("parallel", "parallel")),
        name="merge",
    )(x, mix, proj, proj, mk, mv, xq_norm.reshape(1, X_HEAD_DIM), w_out)


def _mla_qkv_kernel(cq_ref, ckv_ref, kr_ref, posc_ref, posr_ref, invfr_ref, invfc_ref, gq_ref, gkv_ref, gqn_ref,
                    gkn_ref, gqr_ref, gkr_ref, wqt_ref, wk_ref, wvt_ref, qt_ref, kn_ref, krope_ref, vt_ref):
    half = MLA_ROPE // 2
    tm = cq_ref.shape[1]
    qscale = (MLA_NOPE + MLA_ROPE) ** -0.5 * math.log2(math.e)

    cq = _rms(cq_ref[0].astype(F32), gq_ref[...])
    ckv = _rms(ckv_ref[0].astype(F32), gkv_ref[...])
    cq_t = cq.T.astype(BF16)
    ckv_t = ckv.T.astype(BF16)
    q_t = jnp.dot(wqt_ref[...], cq_t, preferred_element_type=F32)
    v_t = jnp.dot(wvt_ref[...], ckv_t, preferred_element_type=F32)
    k_n = jnp.dot(ckv.astype(BF16), wk_ref[...], preferred_element_type=F32)

    ang_t = invfc_ref[...] * posr_ref[0].astype(F32)
    cos_t, sin_t = jnp.cos(ang_t), jnp.sin(ang_t)
    g_nope = jnp.broadcast_to(gqn_ref[...], (MLA_NOPE, tm)) * qscale
    g_r1 = jnp.broadcast_to(gqr_ref[:half, :], (half, tm)) * qscale
    g_r2 = jnp.broadcast_to(gqr_ref[half:, :], (half, tm)) * qscale
    for h in range(MLA_HEADS):
        q = q_t[h * MLA_QK_PAD:(h + 1) * MLA_QK_PAD]
        nope = q[:MLA_NOPE]
        r = lax.rsqrt(jnp.mean(nope * nope, axis=0, keepdims=True) + EPS)
        qt_ref[0, h, :MLA_NOPE, :] = (nope * r * g_nope).astype(BF16)
        x1, x2 = q[MLA_NOPE:MLA_NOPE + half], q[MLA_NOPE + half:MLA_NOPE + MLA_ROPE]
        ss = jnp.sum(x1 * x1, axis=0, keepdims=True) + jnp.sum(x2 * x2, axis=0, keepdims=True)
        r = lax.rsqrt(ss * (1.0 / MLA_ROPE) + EPS)
        x1, x2 = x1 * r * g_r1, x2 * r * g_r2
        qt_ref[0, h, MLA_NOPE:MLA_NOPE + half, :] = (x1 * cos_t - x2 * sin_t).astype(BF16)
        qt_ref[0, h, MLA_NOPE + half:MLA_NOPE + MLA_ROPE, :] = (x1 * sin_t + x2 * cos_t).astype(BF16)
        qt_ref[0, h, MLA_NOPE + MLA_ROPE:, :] = jnp.zeros((MLA_QK_PAD - MLA_NOPE - MLA_ROPE, tm), BF16)
        kn_ref[0, h] = _rms(k_n[:, h * MLA_NOPE:(h + 1) * MLA_NOPE], gkn_ref[...]).astype(BF16)
        vt_ref[0, h] = v_t[h * MLA_V:(h + 1) * MLA_V].astype(BF16)

    ang = posc_ref[0].astype(F32) * invfr_ref[...]
    lane = lax.broadcasted_iota(jnp.int32, ang.shape, 1)
    cos = jnp.where(lane < MLA_ROPE, jnp.cos(ang), 0.0)
    sin = jnp.sin(ang)
    sin_lo = jnp.where(lane < half, -sin, 0.0)
    sin_hi = jnp.where((lane >= half) & (lane < MLA_ROPE), sin, 0.0)
    x = kr_ref[0].astype(F32)
    xn = x * lax.rsqrt(jnp.sum(x * x, axis=-1, keepdims=True) * (1.0 / MLA_ROPE) + EPS) * gkr_ref[...]
    krope_ref[0] = (xn * cos + pltpu.roll(xn, LANES - half, 1) * sin_lo
                    + pltpu.roll(xn, half, 1) * sin_hi).astype(BF16)


def _mla_qkv(proj, cq_blk, ckv_blk, kr_blk, positions, gains, wqt, wk, wvt, *, tm):
    b, l, _ = proj.shape
    hh = MLA_HEADS
    half = MLA_ROPE // 2
    inv_freq = ROPE_THETA ** (-jnp.arange(half, dtype=F32) / half)
    invf_row = jnp.concatenate([inv_freq, inv_freq, jnp.zeros((LANES - MLA_ROPE,), F32)]).reshape(1, LANES)
    const = lambda a: pl.BlockSpec(a.shape, lambda i, j: (0,) * a.ndim)
    gq, gkv, gqn, gkn, gqr, gkr = gains
    consts = [invf_row, inv_freq.reshape(half, 1), gq.reshape(1, -1), gkv.reshape(1, -1), gqn.reshape(-1, 1),
              gkn.reshape(1, -1), gqr.reshape(-1, 1),
              jnp.concatenate([gkr, jnp.zeros((LANES - MLA_ROPE,), F32)]).reshape(1, LANES), wqt, wk, wvt]
    return pl.pallas_call(
        _mla_qkv_kernel,
        grid=(b, l // tm),
        in_specs=[pl.BlockSpec((1, tm, MLA_Q_LORA), lambda i, j: (i, j, cq_blk)),
                  pl.BlockSpec((1, tm, MLA_KV_LORA), lambda i, j: (i, j, ckv_blk)),
                  pl.BlockSpec((1, tm, LANES), lambda i, j: (i, j, kr_blk)),
                  pl.BlockSpec((1, tm, 1), lambda i, j: (i, j, 0)),
                  pl.BlockSpec((1, 1, tm), lambda i, j: (i, 0, j))] + [const(a) for a in consts],
        out_specs=[pl.BlockSpec((1, hh, MLA_QK_PAD, tm), lambda i, j: (i, 0, 0, j)),
                   pl.BlockSpec((1, hh, tm, MLA_NOPE), lambda i, j: (i, 0, j, 0)),
                   pl.BlockSpec((1, tm, LANES), lambda i, j: (i, j, 0)),
                   pl.BlockSpec((1, hh, MLA_V, tm), lambda i, j: (i, 0, 0, j))],
        out_shape=[jax.ShapeDtypeStruct((b, hh, MLA_QK_PAD, l), BF16),
                   jax.ShapeDtypeStruct((b, hh, l, MLA_NOPE), BF16),
                   jax.ShapeDtypeStruct((b, l, LANES), BF16),
                   jax.ShapeDtypeStruct((b, hh, MLA_V, l), BF16)],
        compiler_params=_cparams(("parallel", "parallel")),
        name="mla_qkv",
    )(proj, proj, proj, positions.reshape(b, l, 1), positions.reshape(b, 1, l), *consts)


def _flash_kernel(qt_ref, kn_ref, kr_ref, vt_ref, o_ref, m_ref, l_ref, acc_ref, *, tq, hp):
    qi = pl.program_id(2)
    m_ref[...] = jnp.full(m_ref.shape, -jnp.inf, F32)
    l_ref[...] = jnp.zeros(l_ref.shape, F32)
    acc_ref[...] = jnp.zeros(acc_ref.shape, F32)

    def block(j, diagonal):
        rows = pl.ds(pl.multiple_of(j * tq, tq), tq)
        k_rope = kr_ref[0, rows, :]
        scores = []
        for h in range(hp):
            k = jnp.concatenate([kn_ref[0, h, rows, :], k_rope], axis=-1)
            scores.append(jnp.dot(k, qt_ref[0, h], preferred_element_type=F32))
        for h in range(hp):
            s = scores[h]
            if diagonal:
                key = lax.broadcasted_iota(jnp.int32, s.shape, 0)
                qry = lax.broadcasted_iota(jnp.int32, s.shape, 1)
                s = jnp.where(key <= qry, s, jnp.finfo(F32).min)
            m = m_ref[h]
            m_new = jnp.maximum(m, jnp.max(s, axis=0, keepdims=True))
            alpha = jnp.exp2(m - m_new)
            p = jnp.exp2(s - m_new)
            l_ref[h] = alpha * l_ref[h] + jnp.sum(p, axis=0, keepdims=True)
            acc_ref[h] = alpha * acc_ref[h] + jnp.dot(vt_ref[0, h, :, rows], p.astype(BF16),
                                                      preferred_element_type=F32)
            m_ref[h] = m_new

    def body(j, carry):
        block(j, False)
        return carry

    lax.fori_loop(0, qi, body, 0)
    block(qi, True)
    for h in range(hp):
        o_ref[0, :, h * MLA_V:(h + 1) * MLA_V] = (acc_ref[h] / l_ref[h]).T.astype(o_ref.dtype)


def _flash(qt, kn, kr, vt, *, tq, hp):
    b, hh, _, l = qt.shape
    return pl.pallas_call(
        functools.partial(_flash_kernel, tq=tq, hp=hp),
        grid=(b, hh // hp, l // tq),
        in_specs=[pl.BlockSpec((1, hp, MLA_QK_PAD, tq), lambda i, h, j: (i, h, 0, j)),
                  pl.BlockSpec((1, hp, l, MLA_NOPE), lambda i, h, j: (i, h, 0, 0)),
                  pl.BlockSpec((1, l, LANES), lambda i, h, j: (i, 0, 0)),
                  pl.BlockSpec((1, hp, MLA_V, l), lambda i, h, j: (i, h, 0, 0))],
        out_specs=pl.BlockSpec((1, tq, hp * MLA_V), lambda i, h, j: (i, j, h)),
        out_shape=jax.ShapeDtypeStruct((b, l, hh * MLA_V), BF16),
        scratch_shapes=[pltpu.VMEM((hp, 1, tq), F32), pltpu.VMEM((hp, 1, tq), F32),
                        pltpu.VMEM((hp, MLA_V, tq), F32)],
        compiler_params=_cparams(("parallel", "parallel", "parallel")),
        name="flash",
    )(qt, kn, kr, vt)


def _s5_layer(x, mem, ln, w_in, lam_re, lam_im, log_step, b_re, b_im, c_re, c_im, d, w_glu,
              w_out, mem_norm, w_mem_kv, xq_norm, xk_norm):
    b, l, dm = x.shape
    proj = _norm_matmul(x.reshape(b * l, dm), ln, w_in.astype(BF16), tm=512, col_chunk=512, name="s5_in_proj")
    proj = proj.reshape(b, l, -1)
    a_re, a_im, bb_re, bb_im = _s5_params(lam_re, lam_im, log_step, b_re, b_im)
    to_in = lambda w: _slab_block_diag(w.reshape(S5_GROUPS, S5_STATE, S5_GROUP_CH).transpose(0, 2, 1))
    bw = jnp.concatenate([to_in(bb_re), to_in(bb_im)], axis=-1).astype(BF16)
    to_out = lambda w: _slab_block_diag(w.transpose(0, 2, 1)).astype(BF16)
    row = lambda a: a.reshape(S5_SLABS, 1, S5_SLAB_STATE)
    yg = _s5_mix(proj, bw, to_out(c_re), to_out(c_im), row(a_re), row(a_im),
                 d.reshape(S5_SLABS, 1, LANES), tt=64)
    y = _glu(yg.reshape(b * l, PRIMARY_WIDTH), w_glu.astype(BF16), tm=512, col_chunk=256)
    mk, mv = _mem_kv(mem, mem_norm, w_mem_kv, xk_norm)
    return _merge(x, y.reshape(b, l, PRIMARY_WIDTH), proj, PRIMARY_WIDTH // XQ_WIDTH, 1, mk, mv, xq_norm,
                  w_out.astype(BF16), tm=512)


def _mla_layer(x, mem, positions, ln, w_in, q_lora_norm, kv_lora_norm, w_uq, w_ukv, q_nope_norm, k_nope_norm,
               q_rope_norm, k_rope_norm, w_out, mem_norm, w_mem_kv, xq_norm, xk_norm):
    b, l, dm = x.shape
    o1 = MLA_Q_LORA
    o2 = o1 + MLA_KV_LORA
    o3 = o2 + MLA_ROPE
    o4 = o3 + XQ_WIDTH
    w_perm = jnp.concatenate([w_in[:, o4:], w_in[:, :o1], w_in[:, o3:o4], w_in[:, o1:o2], w_in[:, o2:o3],
                              jnp.zeros((dm, LANES - MLA_ROPE), w_in.dtype)], axis=1).astype(BF16)
    proj = _norm_matmul(x.reshape(b * l, dm), ln, w_perm, tm=512, col_chunk=384, name="mla_in_proj")
    proj = proj.reshape(b, l, -1)
    gate_blk = 0
    cq_blk = BRANCH_WIDTH // MLA_Q_LORA
    xq_blk = (BRANCH_WIDTH + MLA_Q_LORA) // XQ_WIDTH
    ckv_blk = (BRANCH_WIDTH + MLA_Q_LORA + XQ_WIDTH) // MLA_KV_LORA
    kr_blk = (BRANCH_WIDTH + MLA_Q_LORA + XQ_WIDTH + MLA_KV_LORA) // LANES
    wq = w_uq.reshape(MLA_Q_LORA, MLA_HEADS, MLA_NOPE + MLA_ROPE)
    wq = jnp.pad(wq, ((0, 0), (0, 0), (0, MLA_QK_PAD - MLA_NOPE - MLA_ROPE)))
    wqt = wq.reshape(MLA_Q_LORA, MLA_HEADS * MLA_QK_PAD).T.astype(BF16)
    wkv = w_ukv.reshape(MLA_KV_LORA, MLA_HEADS, MLA_NOPE + MLA_V)
    wk = wkv[:, :, :MLA_NOPE].reshape(MLA_KV_LORA, MLA_HEADS * MLA_NOPE).astype(BF16)
    wvt = wkv[:, :, MLA_NOPE:].reshape(MLA_KV_LORA, MLA_HEADS * MLA_V).T.astype(BF16)
    qt, kn, kr, vt = _mla_qkv(proj, cq_blk, ckv_blk, kr_blk, positions,
                              (q_lora_norm, kv_lora_norm, q_nope_norm, k_nope_norm, q_rope_norm, k_rope_norm),
                              wqt, wk, wvt, tm=256)
    attn = _flash(qt, kn, kr, vt, tq=512, hp=4)
    mk, mv = _mem_kv(mem, mem_norm, w_mem_kv, xk_norm)
    return _merge(x, attn, proj, xq_blk, gate_blk, mk, mv, xq_norm, w_out.astype(BF16), tm=512)


def kernel(x, mem, positions, ln_gain, w_out, mem_norm, w_mem_kv, xq_norm, xk_norm,
           s5_w_in, s5_lambda_re, s5_lambda_im, s5_log_step, s5_b_re, s5_b_im, s5_c_re, s5_c_im,
           s5_d, s5_w_glu, mla_w_in, mla_q_lora_norm, mla_kv_lora_norm, mla_w_uq, mla_w_ukv,
           mla_q_nope_norm, mla_k_nope_norm, mla_q_rope_norm, mla_k_rope_norm):
    depth = ln_gain.shape[0]
    for i in range(depth):
        j = i // 2
        if i % 2 == 0:
            x = _s5_layer(x, mem, ln_gain[i], s5_w_in[j], s5_lambda_re[j], s5_lambda_im[j], s5_log_step[j],
                          s5_b_re[j], s5_b_im[j], s5_c_re[j], s5_c_im[j], s5_d[j], s5_w_glu[j],
                          w_out[i], mem_norm[i], w_mem_kv[i], xq_norm[i], xk_norm[i])
        else:
            x = _mla_layer(x, mem, positions, ln_gain[i], mla_w_in[j], mla_q_lora_norm[j], mla_kv_lora_norm[j],
                           mla_w_uq[j], mla_w_ukv[j], mla_q_nope_norm[j], mla_k_nope_norm[j],
                           mla_q_rope_norm[j], mla_k_rope_norm[j],
                           w_out[i], mem_norm[i], w_mem_kv[i], xq_norm[i], xk_norm[i])
    return x
```

```python
import functools
import math

import jax
import jax.numpy as jnp
from jax import lax
from jax.experimental import pallas as pl
from jax.experimental.pallas import tpu as pltpu

D_MODEL = 1024
BRANCH_WIDTH = 2 * D_MODEL
XQ_WIDTH = BRANCH_WIDTH // 4
PRIMARY_WIDTH = BRANCH_WIDTH - XQ_WIDTH
X_HEADS = 4
X_HEAD_DIM = XQ_WIDTH // X_HEADS
S5_GROUP_CH = 16
S5_GROUPS = PRIMARY_WIDTH // S5_GROUP_CH
S5_STATE = 64
MLA_NOPE = 128
MLA_ROPE = 64
MLA_V = 128
MLA_HEADS = PRIMARY_WIDTH // MLA_V
MLA_Q_LORA = D_MODEL // 2
MLA_KV_LORA = D_MODEL // 4
ROPE_THETA = 10000.0
EPS = 1e-6

LANES = 128
MLA_QK_PAD = 2 * LANES
S5_CHUNK = 2 * LANES // S5_GROUP_CH
S5_EXPONENTS = list(range(S5_CHUNK + 1)) + [S5_CHUNK * 2 ** i for i in range(1, int(math.log2(LANES)))]
VMEM_LIMIT = 56 * 1024 * 1024

F32 = jnp.float32
BF16 = jnp.bfloat16


def _cparams(sem):
    return pltpu.CompilerParams(dimension_semantics=sem, vmem_limit_bytes=VMEM_LIMIT)


def _rms(x, g):
    return x * lax.rsqrt(jnp.mean(x * x, axis=-1, keepdims=True) + EPS) * g


def _norm_matmul_kernel(x_ref, g_ref, w_ref, o_ref, *, col_chunk):
    xn = _rms(x_ref[...].astype(F32), g_ref[...]).astype(BF16)
    for c in range(o_ref.shape[1] // col_chunk):
        sl = slice(c * col_chunk, (c + 1) * col_chunk)
        o_ref[:, sl] = jnp.dot(xn, w_ref[:, sl], preferred_element_type=F32).astype(o_ref.dtype)


def _norm_matmul(x, g, w, *, tm, col_chunk, name):
    n, d = x.shape
    wout = w.shape[1]
    return pl.pallas_call(
        functools.partial(_norm_matmul_kernel, col_chunk=col_chunk),
        grid=(n // tm,),
        in_specs=[pl.BlockSpec((tm, d), lambda i: (i, 0)),
                  pl.BlockSpec((1, d), lambda i: (0, 0)),
                  pl.BlockSpec((d, wout), lambda i: (0, 0))],
        out_specs=pl.BlockSpec((tm, wout), lambda i: (i, 0)),
        out_shape=jax.ShapeDtypeStruct((n, wout), BF16),
        compiler_params=_cparams(("parallel",)),
        name=name,
    )(x, g.reshape(1, d), w)


def _mem_kv_kernel(m_ref, g_ref, w_ref, kg_ref, k_ref, v_ref):
    mn = _rms(m_ref[0], g_ref[...]).astype(BF16)
    kv = jnp.dot(mn, w_ref[...], preferred_element_type=F32)
    for h in range(X_HEADS):
        sl = slice(h * X_HEAD_DIM, (h + 1) * X_HEAD_DIM)
        k_ref[0, :, sl] = _rms(kv[:, sl], kg_ref[...]).astype(BF16)
    v_ref[0] = kv[:, XQ_WIDTH:].astype(BF16)


def _mem_kv(mem, mem_norm, w_mem_kv, xk_norm):
    b, m, d = mem.shape
    out = jax.ShapeDtypeStruct((b, m, XQ_WIDTH), BF16)
    return pl.pallas_call(
        _mem_kv_kernel,
        grid=(b,),
        in_specs=[pl.BlockSpec((1, m, d), lambda i: (i, 0, 0)),
                  pl.BlockSpec((1, d), lambda i: (0, 0)),
                  pl.BlockSpec((d, 2 * XQ_WIDTH), lambda i: (0, 0)),
                  pl.BlockSpec((1, X_HEAD_DIM), lambda i: (0, 0))],
        out_specs=[pl.BlockSpec((1, m, XQ_WIDTH), lambda i: (i, 0, 0)),
                   pl.BlockSpec((1, m, XQ_WIDTH), lambda i: (i, 0, 0))],
        out_shape=[out, out],
        compiler_params=_cparams(("parallel",)),
        name="mem_kv",
    )(mem, mem_norm.reshape(1, d), w_mem_kv.astype(BF16), xk_norm.reshape(1, X_HEAD_DIM))


def _s5_pow_kernel(lr_ref, li_ref, ls_ref, pr_ref, pi_ref):
    lr, li = lr_ref[...], li_ref[...]
    step = jnp.exp(ls_ref[...])
    zr, zi = lr * step, li * step
    for n, e in enumerate(S5_EXPONENTS):
        mag = jnp.exp(zr * e)
        pr_ref[n] = mag * jnp.cos(zi * e)
        pi_ref[n] = mag * jnp.sin(zi * e)
    ar, ai = pr_ref[1], pi_ref[1]
    den = lr * lr + li * li
    pr_ref[len(S5_EXPONENTS)] = ((ar - 1.0) * lr + ai * li) / den
    pi_ref[len(S5_EXPONENTS)] = (ai * lr - (ar - 1.0) * li) / den


def _s5_pow(lam_re, lam_im, log_step):
    g, p = lam_re.shape
    out = jax.ShapeDtypeStruct((len(S5_EXPONENTS) + 1, g, p), F32)
    return pl.pallas_call(_s5_pow_kernel, out_shape=[out, out], name="s5_pow")(
        lam_re, lam_im, log_step.reshape(g, 1))


def _s5_asm_kernel(pr_ref, pi_ref, cr_ref, ci_ref, btr_ref, bti_ref, br_ref, bi_ref,
                   toep_ref, wout_ref, wst_ref):
    t = S5_CHUNK
    pr, pi = pr_ref[0], pi_ref[0]
    cr, ci = cr_ref[0], ci_ref[0]
    btr, bti = btr_ref[0], bti_ref[0]
    mr, mi = pr[len(S5_EXPONENTS):], pi[len(S5_EXPONENTS):]
    amr = pr[:t] * mr - pi[:t] * mi
    ami = pr[:t] * mi + pi[:t] * mr
    l_re, l_im, w_re, w_im, o_re, o_im = [], [], [], [], [], []
    for k in range(t):
        ar, ai = amr[k:k + 1], ami[k:k + 1]
        l_re.append(cr * ar - ci * ai)
        l_im.append(cr * ai + ci * ar)
        ar, ai = amr[t - 1 - k:t - k], ami[t - 1 - k:t - k]
        w_re.append(btr * ar - bti * ai)
        w_im.append(btr * ai + bti * ar)
        ar, ai = pr[k + 1:k + 2], pi[k + 1:k + 2]
        o_re.append(cr * ar - ci * ai)
        o_im.append(-(cr * ai + ci * ar))
    cat = lambda parts: jnp.concatenate(parts, axis=0)
    hi = lax.Precision.HIGHEST
    kt = (jnp.dot(cat(l_re), br_ref[0], precision=hi, preferred_element_type=F32)
          - jnp.dot(cat(l_im), bi_ref[0], precision=hi, preferred_element_type=F32))
    n = kt.shape[0]
    blk = lax.shift_right_logical(lax.broadcasted_iota(jnp.int32, kt.shape, 1), int(math.log2(S5_GROUP_CH)))
    toep = jnp.where(blk == 0, kt, 0.0)
    for s in range(1, t):
        shifted = jnp.concatenate([jnp.zeros((s * S5_GROUP_CH, n), F32), kt[:n - s * S5_GROUP_CH]], axis=0)
        toep = jnp.where(blk == s, shifted, toep)
    toep_ref[0] = toep.astype(BF16)
    wout_ref[0] = jnp.concatenate([cat(o_re), cat(o_im)], axis=1).astype(BF16)
    wst_ref[0] = jnp.concatenate([cat(w_re), cat(w_im)], axis=1).T.astype(BF16)


def _s5_asm(pw_re, pw_im, c_re, c_im, b_re, b_im):
    g, c, p = c_re.shape
    tc = S5_CHUNK * c
    blk = lambda a: pl.BlockSpec((1,) + a.shape[1:], lambda i: (i, 0, 0))
    bt_re, bt_im = b_re.transpose(0, 2, 1), b_im.transpose(0, 2, 1)
    tile = lambda a: jnp.tile(a, (1, 1, S5_CHUNK))
    args = (pw_re, pw_im, c_re, c_im, bt_re, bt_im, tile(b_re), tile(b_im))
    return pl.pallas_call(
        _s5_asm_kernel,
        grid=(g,),
        in_specs=[blk(a) for a in args],
        out_specs=[pl.BlockSpec((1, tc, tc), lambda i: (i, 0, 0)),
                   pl.BlockSpec((1, tc, 2 * p), lambda i: (i, 0, 0)),
                   pl.BlockSpec((1, 2 * p, tc), lambda i: (i, 0, 0))],
        out_shape=[jax.ShapeDtypeStruct((g, tc, tc), BF16), jax.ShapeDtypeStruct((g, tc, 2 * p), BF16),
                   jax.ShapeDtypeStruct((g, 2 * p, tc), BF16)],
        compiler_params=_cparams(("parallel",)),
        name="s5_asm",
    )(*args)


def _s5_ut_kernel(x_ref, g_ref, w_ref, o_ref):
    nb, nc, d = x_ref.shape
    xn = _rms(x_ref[...].reshape(nb * nc, d), g_ref[...]).astype(BF16)
    ut = lax.dot_general(w_ref[...], xn, (((1,), (1,)), ((), ())), preferred_element_type=F32)
    o_ref[...] = ut.astype(BF16).reshape(o_ref.shape)


def _s5_ut(x, g, wut):
    b, l, d = x.shape
    nc = l // S5_CHUNK
    x4 = x.reshape(b, nc, S5_CHUNK * d)
    return pl.pallas_call(
        _s5_ut_kernel,
        grid=(S5_CHUNK,),
        in_specs=[pl.BlockSpec((b, nc, d), lambda s: (0, 0, s)),
                  pl.BlockSpec((1, d), lambda s: (0, 0)),
                  pl.BlockSpec(wut.shape, lambda s: (0, 0))],
        out_specs=pl.BlockSpec((S5_GROUPS, S5_GROUP_CH, b * nc), lambda s: (0, s, 0)),
        out_shape=jax.ShapeDtypeStruct((S5_GROUPS, S5_CHUNK * S5_GROUP_CH, b * nc), BF16),
        compiler_params=_cparams(("parallel",)),
        name="s5_ut",
    )(x4, g.reshape(1, d), wut)


def _s5_mix_kernel(x_ref, toep_ref, wst_ref, wout_ref, sr_ref, si_ref, d_ref, o_ref, *, nb):
    p = S5_STATE
    x = x_ref[0]
    hloc = jnp.dot(wst_ref[0], x, preferred_element_type=F32)
    lane = lax.broadcasted_iota(jnp.int32, (p, LANES), 1)
    n_steps = int(math.log2(LANES))
    pw = []
    for i in range(n_steps):
        keep = lane >= (1 << i)
        pw.append((jnp.where(keep, jnp.broadcast_to(sr_ref[0, :, i:i + 1], (p, LANES)), 0.0),
                   jnp.where(keep, jnp.broadcast_to(si_ref[0, :, i:i + 1], (p, LANES)), 0.0)))
    h_re, h_im = [], []
    for b in range(nb):
        hr = hloc[:p, b * LANES:(b + 1) * LANES]
        hi = hloc[p:, b * LANES:(b + 1) * LANES]
        for i in range(n_steps):
            rs, js = pltpu.roll(hr, 1 << i, 1), pltpu.roll(hi, 1 << i, 1)
            ar, ai = pw[i]
            hr, hi = hr + ar * rs - ai * js, hi + ar * js + ai * rs
        h_re.append(jnp.where(lane >= 1, pltpu.roll(hr, 1, 1), 0.0))
        h_im.append(jnp.where(lane >= 1, pltpu.roll(hi, 1, 1), 0.0))
    h = jnp.concatenate([jnp.concatenate(h_re, axis=1), jnp.concatenate(h_im, axis=1)], axis=0).astype(BF16)
    y = (jnp.dot(toep_ref[0], x, preferred_element_type=F32)
         + jnp.dot(wout_ref[0], h, preferred_element_type=F32)
         + d_ref[0] * x.astype(F32))
    o_ref[...] = jax.nn.gelu(y).astype(o_ref.dtype).reshape(o_ref.shape)


def _s5_mix(xg, toep, wst, wout, sc_re, sc_im, dcol, *, nb):
    g, tc, cols = xg.shape
    assert cols == nb * LANES, "one batch's chunks must fill exactly one 128-lane block"
    blk = lambda a: pl.BlockSpec((1,) + a.shape[1:], lambda i: (i, 0, 0))
    return pl.pallas_call(
        functools.partial(_s5_mix_kernel, nb=nb),
        grid=(g,),
        in_specs=[blk(a) for a in (xg, toep, wst, wout, sc_re, sc_im, dcol)],
        out_specs=pl.BlockSpec((S5_CHUNK, S5_GROUP_CH, cols), lambda i: (0, i, 0)),
        out_shape=jax.ShapeDtypeStruct((S5_CHUNK, g * S5_GROUP_CH, cols), BF16),
        compiler_params=_cparams(("parallel",)),
        name="s5_mix",
    )(xg, toep, wst, wout, sc_re, sc_im, dcol)


def _glu_kernel(y_ref, w_ref, o_ref, *, col_chunk):
    y = y_ref[0].T
    half = o_ref.shape[-1]
    for c in range(half // col_chunk):
        a = jnp.dot(y, w_ref[:, c * col_chunk:(c + 1) * col_chunk], preferred_element_type=F32)
        g = jnp.dot(y, w_ref[:, half + c * col_chunk:half + (c + 1) * col_chunk], preferred_element_type=F32)
        o_ref[:, :, c * col_chunk:(c + 1) * col_chunk] = (
            (a * jax.nn.sigmoid(g)).astype(o_ref.dtype).reshape(o_ref.shape[:2] + (col_chunk,)))


def _glu(yt, w, *, nb, col_chunk):
    t, k, cols = yt.shape
    nc = cols // nb
    half = w.shape[1] // 2
    out = pl.pallas_call(
        functools.partial(_glu_kernel, col_chunk=col_chunk),
        grid=(t,),
        in_specs=[pl.BlockSpec((1, k, cols), lambda j: (j, 0, 0)),
                  pl.BlockSpec(w.shape, lambda j: (0, 0))],
        out_specs=pl.BlockSpec((nb, nc, half), lambda j: (0, 0, j)),
        out_shape=jax.ShapeDtypeStruct((nb, nc, t * half), BF16),
        compiler_params=_cparams(("parallel",)),
        name="glu",
    )(yt, w)
    return out.reshape(nb, nc * t, half)


def _merge_kernel(x_ref, mix_ref, xq_ref, gate_ref, k_ref, v_ref, qg_ref, w_ref, o_ref, cat_ref):
    gate = gate_ref[0].astype(F32)
    sg = gate * jax.nn.sigmoid(gate)
    cat_ref[:, :PRIMARY_WIDTH] = (mix_ref[0].astype(F32) * sg[:, :PRIMARY_WIDTH]).astype(BF16)
    scale = X_HEAD_DIM ** -0.5
    for h in range(X_HEADS):
        sl = slice(h * X_HEAD_DIM, (h + 1) * X_HEAD_DIM)
        q = _rms(xq_ref[0, :, sl].astype(F32), qg_ref[...]).astype(BF16)
        s = lax.dot_general(q, k_ref[0, :, sl], (((1,), (1,)), ((), ())), preferred_element_type=F32) * scale
        p = jnp.exp(s - jnp.max(s, axis=-1, keepdims=True))
        p = (p / jnp.sum(p, axis=-1, keepdims=True)).astype(BF16)
        mo = jnp.dot(p, v_ref[0, :, sl], preferred_element_type=F32)
        osl = slice(PRIMARY_WIDTH + h * X_HEAD_DIM, PRIMARY_WIDTH + (h + 1) * X_HEAD_DIM)
        cat_ref[:, osl] = (mo * sg[:, osl]).astype(BF16)
    o_ref[0] = x_ref[0] + jnp.dot(cat_ref[...], w_ref[...], preferred_element_type=F32)


def _merge(x, mix, proj, xq_blk, gate_blk, mk, mv, xq_norm, w_out, *, tm):
    b, l, d = x.shape
    m = mk.shape[1]
    return pl.pallas_call(
        _merge_kernel,
        grid=(b, l // tm),
        in_specs=[pl.BlockSpec((1, tm, d), lambda i, j: (i, j, 0)),
                  pl.BlockSpec((1, tm, PRIMARY_WIDTH), lambda i, j: (i, j, 0)),
                  pl.BlockSpec((1, tm, XQ_WIDTH), lambda i, j: (i, j, xq_blk)),
                  pl.BlockSpec((1, tm, BRANCH_WIDTH), lambda i, j: (i, j, gate_blk)),
                  pl.BlockSpec((1, m, XQ_WIDTH), lambda i, j: (i, 0, 0)),
                  pl.BlockSpec((1, m, XQ_WIDTH), lambda i, j: (i, 0, 0)),
                  pl.BlockSpec((1, X_HEAD_DIM), lambda i, j: (0, 0)),
                  pl.BlockSpec((BRANCH_WIDTH, d), lambda i, j: (0, 0))],
        out_specs=pl.BlockSpec((1, tm, d), lambda i, j: (i, j, 0)),
        out_shape=jax.ShapeDtypeStruct((b, l, d), F32),
        scratch_shapes=[pltpu.VMEM((tm, BRANCH_WIDTH), BF16)],
        compiler_params=_cparams(("parallel", "parallel")),
        name="merge",
    )(x, mix, proj, proj, mk, mv, xq_norm.reshape(1, X_HEAD_DIM), w_out)


def _mla_qkv_kernel(cq_ref, ckv_ref, kr_ref, posc_ref, posr_ref, invfr_ref, invfc_ref, gq_ref, gkv_ref, gqn_ref,
                    gkn_ref, gqr_ref, gkr_ref, wqt_ref, wk_ref, wvt_ref, qt_ref, kn_ref, krope_ref, vt_ref):
    half = MLA_ROPE // 2
    tm = cq_ref.shape[1]
    qscale = (MLA_NOPE + MLA_ROPE) ** -0.5 * math.log2(math.e)

    cq = _rms(cq_ref[0].astype(F32), gq_ref[...])
    ckv = _rms(ckv_ref[0].astype(F32), gkv_ref[...])
    cq_t = cq.T.astype(BF16)
    ckv_t = ckv.T.astype(BF16)
    q_t = jnp.dot(wqt_ref[...], cq_t, preferred_element_type=F32)
    v_t = jnp.dot(wvt_ref[...], ckv_t, preferred_element_type=F32)
    k_n = jnp.dot(ckv.astype(BF16), wk_ref[...], preferred_element_type=F32)

    ang_t = invfc_ref[...] * posr_ref[0].astype(F32)
    cos_t, sin_t = jnp.cos(ang_t), jnp.sin(ang_t)
    g_nope = jnp.broadcast_to(gqn_ref[...], (MLA_NOPE, tm)) * qscale
    g_r1 = jnp.broadcast_to(gqr_ref[:half, :], (half, tm)) * qscale
    g_r2 = jnp.broadcast_to(gqr_ref[half:, :], (half, tm)) * qscale
    for h in range(MLA_HEADS):
        q = q_t[h * MLA_QK_PAD:(h + 1) * MLA_QK_PAD]
        nope = q[:MLA_NOPE]
        r = lax.rsqrt(jnp.mean(nope * nope, axis=0, keepdims=True) + EPS)
        qt_ref[0, h, :MLA_NOPE, :] = (nope * r * g_nope).astype(BF16)
        x1, x2 = q[MLA_NOPE:MLA_NOPE + half], q[MLA_NOPE + half:MLA_NOPE + MLA_ROPE]
        ss = jnp.sum(x1 * x1, axis=0, keepdims=True) + jnp.sum(x2 * x2, axis=0, keepdims=True)
        r = lax.rsqrt(ss * (1.0 / MLA_ROPE) + EPS)
        x1, x2 = x1 * r * g_r1, x2 * r * g_r2
        qt_ref[0, h, MLA_NOPE:MLA_NOPE + half, :] = (x1 * cos_t - x2 * sin_t).astype(BF16)
        qt_ref[0, h, MLA_NOPE + half:MLA_NOPE + MLA_ROPE, :] = (x1 * sin_t + x2 * cos_t).astype(BF16)
        qt_ref[0, h, MLA_NOPE + MLA_ROPE:, :] = jnp.zeros((MLA_QK_PAD - MLA_NOPE - MLA_ROPE, tm), BF16)
        kn_ref[0, h] = _rms(k_n[:, h * MLA_NOPE:(h + 1) * MLA_NOPE], gkn_ref[...]).astype(BF16)
        vt_ref[0, h] = v_t[h * MLA_V:(h + 1) * MLA_V].astype(BF16)

    ang = posc_ref[0].astype(F32) * invfr_ref[...]
    lane = lax.broadcasted_iota(jnp.int32, ang.shape, 1)
    cos = jnp.where(lane < MLA_ROPE, jnp.cos(ang), 0.0)
    sin = jnp.sin(ang)
    sin_lo = jnp.where(lane < half, -sin, 0.0)
    sin_hi = jnp.where((lane >= half) & (lane < MLA_ROPE), sin, 0.0)
    x = kr_ref[0].astype(F32)
    xn = x * lax.rsqrt(jnp.sum(x * x, axis=-1, keepdims=True) * (1.0 / MLA_ROPE) + EPS) * gkr_ref[...]
    krope_ref[0] = (xn * cos + pltpu.roll(xn, LANES - half, 1) * sin_lo
                    + pltpu.roll(xn, half, 1) * sin_hi).astype(BF16)


def _mla_qkv(proj, cq_blk, ckv_blk, kr_blk, positions, gains, wqt, wk, wvt, *, tm):
    b, l, _ = proj.shape
    hh = MLA_HEADS
    half = MLA_ROPE // 2
    inv_freq = ROPE_THETA ** (-jnp.arange(half, dtype=F32) / half)
    invf_row = jnp.concatenate([inv_freq, inv_freq, jnp.zeros((LANES - MLA_ROPE,), F32)]).reshape(1, LANES)
    const = lambda a: pl.BlockSpec(a.shape, lambda i, j: (0,) * a.ndim)
    gq, gkv, gqn, gkn, gqr, gkr = gains
    consts = [invf_row, inv_freq.reshape(half, 1), gq.reshape(1, -1), gkv.reshape(1, -1), gqn.reshape(-1, 1),
              gkn.reshape(1, -1), gqr.reshape(-1, 1),
              jnp.concatenate([gkr, jnp.zeros((LANES - MLA_ROPE,), F32)]).reshape(1, LANES), wqt, wk, wvt]
    return pl.pallas_call(
        _mla_qkv_kernel,
        grid=(b, l // tm),
        in_specs=[pl.BlockSpec((1, tm, MLA_Q_LORA), lambda i, j: (i, j, cq_blk)),
                  pl.BlockSpec((1, tm, MLA_KV_LORA), lambda i, j: (i, j, ckv_blk)),
                  pl.BlockSpec((1, tm, LANES), lambda i, j: (i, j, kr_blk)),
                  pl.BlockSpec((1, tm, 1), lambda i, j: (i, j, 0)),
                  pl.BlockSpec((1, 1, tm), lambda i, j: (i, 0, j))] + [const(a) for a in consts],
        out_specs=[pl.BlockSpec((1, hh, MLA_QK_PAD, tm), lambda i, j: (i, 0, 0, j)),
                   pl.BlockSpec((1, hh, tm, MLA_NOPE), lambda i, j: (i, 0, j, 0)),
                   pl.BlockSpec((1, tm, LANES), lambda i, j: (i, j, 0)),
                   pl.BlockSpec((1, hh, MLA_V, tm), lambda i, j: (i, 0, 0, j))],
        out_shape=[jax.ShapeDtypeStruct((b, hh, MLA_QK_PAD, l), BF16),
                   jax.ShapeDtypeStruct((b, hh, l, MLA_NOPE), BF16),
                   jax.ShapeDtypeStruct((b, l, LANES), BF16),
                   jax.ShapeDtypeStruct((b, hh, MLA_V, l), BF16)],
        compiler_params=_cparams(("parallel", "parallel")),
        name="mla_qkv",
    )(proj, proj, proj, positions.reshape(b, l, 1), positions.reshape(b, 1, l), *consts)


def _flash_kernel(qt_ref, kn_ref, kr_ref, vt_ref, o_ref, m_ref, l_ref, acc_ref, *, tq, hp):
    qi = pl.program_id(2)
    m_ref[...] = jnp.full(m_ref.shape, -jnp.inf, F32)
    l_ref[...] = jnp.zeros(l_ref.shape, F32)
    acc_ref[...] = jnp.zeros(acc_ref.shape, F32)

    def block(j, diagonal):
        rows = pl.ds(pl.multiple_of(j * tq, tq), tq)
        k_rope = kr_ref[0, rows, :]
        scores = []
        for h in range(hp):
            k = jnp.concatenate([kn_ref[0, h, rows, :], k_rope], axis=-1)
            scores.append(jnp.dot(k, qt_ref[0, h], preferred_element_type=F32))
        for h in range(hp):
            s = scores[h]
            if diagonal:
                key = lax.broadcasted_iota(jnp.int32, s.shape, 0)
                qry = lax.broadcasted_iota(jnp.int32, s.shape, 1)
                s = jnp.where(key <= qry, s, jnp.finfo(F32).min)
            m = m_ref[h]
            m_new = jnp.maximum(m, jnp.max(s, axis=0, keepdims=True))
            alpha = jnp.exp2(m - m_new)
            p = jnp.exp2(s - m_new)
            l_ref[h] = alpha * l_ref[h] + jnp.sum(p, axis=0, keepdims=True)
            acc_ref[h] = alpha * acc_ref[h] + jnp.dot(vt_ref[0, h, :, rows], p.astype(BF16),
                                                      preferred_element_type=F32)
            m_ref[h] = m_new

    def body(j, carry):
        block(j, False)
        return carry

    lax.fori_loop(0, qi, body, 0)
    block(qi, True)
    for h in range(hp):
        o_ref[0, :, h * MLA_V:(h + 1) * MLA_V] = (acc_ref[h] / l_ref[h]).T.astype(o_ref.dtype)


def _flash(qt, kn, kr, vt, *, tq, hp):
    b, hh, _, l = qt.shape
    return pl.pallas_call(
        functools.partial(_flash_kernel, tq=tq, hp=hp),
        grid=(b, hh // hp, l // tq),
        in_specs=[pl.BlockSpec((1, hp, MLA_QK_PAD, tq), lambda i, h, j: (i, h, 0, j)),
                  pl.BlockSpec((1, hp, l, MLA_NOPE), lambda i, h, j: (i, h, 0, 0)),
                  pl.BlockSpec((1, l, LANES), lambda i, h, j: (i, 0, 0)),
                  pl.BlockSpec((1, hp, MLA_V, l), lambda i, h, j: (i, h, 0, 0))],
        out_specs=pl.BlockSpec((1, tq, hp * MLA_V), lambda i, h, j: (i, j, h)),
        out_shape=jax.ShapeDtypeStruct((b, l, hh * MLA_V), BF16),
        scratch_shapes=[pltpu.VMEM((hp, 1, tq), F32), pltpu.VMEM((hp, 1, tq), F32),
                        pltpu.VMEM((hp, MLA_V, tq), F32)],
        compiler_params=_cparams(("parallel", "parallel", "parallel")),
        name="flash",
    )(qt, kn, kr, vt)


def _s5_layer(x, mem, ln, w_in, lam_re, lam_im, log_step, b_re, b_im, c_re, c_im, d, w_glu,
              w_out, mem_norm, w_mem_kv, xq_norm, xk_norm):
    b, l, dm = x.shape
    w_gx = jnp.concatenate([w_in[:, PRIMARY_WIDTH + XQ_WIDTH:], w_in[:, PRIMARY_WIDTH:PRIMARY_WIDTH + XQ_WIDTH]],
                           axis=1).astype(BF16)
    proj = _norm_matmul(x.reshape(b * l, dm), ln, w_gx, tm=512, col_chunk=512, name="s5_in_proj")
    proj = proj.reshape(b, l, -1)
    xg = _s5_ut(x, ln, w_in[:, :PRIMARY_WIDTH].T.astype(BF16))
    pw_re, pw_im = _s5_pow(lam_re, lam_im, log_step)
    toep, wout, wst = _s5_asm(pw_re.transpose(1, 0, 2), pw_im.transpose(1, 0, 2), c_re, c_im, b_re, b_im)
    n_scan = int(math.log2(LANES))
    first = S5_EXPONENTS.index(S5_CHUNK)
    col = lambda pw: jnp.pad(pw[first:first + n_scan].transpose(1, 2, 0), ((0, 0), (0, 0), (0, 8 - n_scan)))
    dcol = jnp.tile(d.reshape(S5_GROUPS, 1, S5_GROUP_CH), (1, S5_CHUNK, 1)).reshape(S5_GROUPS, -1, 1)
    yt = _s5_mix(xg, toep, wst, wout, col(pw_re), col(pw_im), dcol, nb=b)
    y = _glu(yt, w_glu.astype(BF16), nb=b, col_chunk=256)
    mk, mv = _mem_kv(mem, mem_norm, w_mem_kv, xk_norm)
    return _merge(x, y, proj, BRANCH_WIDTH // XQ_WIDTH, 0, mk, mv, xq_norm, w_out.astype(BF16), tm=512)


def _mla_layer(x, mem, positions, ln, w_in, q_lora_norm, kv_lora_norm, w_uq, w_ukv, q_nope_norm, k_nope_norm,
               q_rope_norm, k_rope_norm, w_out, mem_norm, w_mem_kv, xq_norm, xk_norm):
    b, l, dm = x.shape
    o1 = MLA_Q_LORA
    o2 = o1 + MLA_KV_LORA
    o3 = o2 + MLA_ROPE
    o4 = o3 + XQ_WIDTH
    w_perm = jnp.concatenate([w_in[:, o4:], w_in[:, :o1], w_in[:, o3:o4], w_in[:, o1:o2], w_in[:, o2:o3],
                              jnp.zeros((dm, LANES - MLA_ROPE), w_in.dtype)], axis=1).astype(BF16)
    proj = _norm_matmul(x.reshape(b * l, dm), ln, w_perm, tm=512, col_chunk=384, name="mla_in_proj")
    proj = proj.reshape(b, l, -1)
    gate_blk = 0
    cq_blk = BRANCH_WIDTH // MLA_Q_LORA
    xq_blk = (BRANCH_WIDTH + MLA_Q_LORA) // XQ_WIDTH
    ckv_blk = (BRANCH_WIDTH + MLA_Q_LORA + XQ_WIDTH) // MLA_KV_LORA
    kr_blk = (BRANCH_WIDTH + MLA_Q_LORA + XQ_WIDTH + MLA_KV_LORA) // LANES
    wq = w_uq.reshape(MLA_Q_LORA, MLA_HEADS, MLA_NOPE + MLA_ROPE)
    wq = jnp.pad(wq, ((0, 0), (0, 0), (0, MLA_QK_PAD - MLA_NOPE - MLA_ROPE)))
    wqt = wq.reshape(MLA_Q_LORA, MLA_HEADS * MLA_QK_PAD).T.astype(BF16)
    wkv = w_ukv.reshape(MLA_KV_LORA, MLA_HEADS, MLA_NOPE + MLA_V)
    wk = wkv[:, :, :MLA_NOPE].reshape(MLA_KV_LORA, MLA_HEADS * MLA_NOPE).astype(BF16)
    wvt = wkv[:, :, MLA_NOPE:].reshape(MLA_KV_LORA, MLA_HEADS * MLA_V).T.astype(BF16)
    qt, kn, kr, vt = _mla_qkv(proj, cq_blk, ckv_blk, kr_blk, positions,
                              (q_lora_norm, kv_lora_norm, q_nope_norm, k_nope_norm, q_rope_norm, k_rope_norm),
                              wqt, wk, wvt, tm=256)
    attn = _flash(qt, kn, kr, vt, tq=512, hp=4)
    mk, mv = _mem_kv(mem, mem_norm, w_mem_kv, xk_norm)
    return _merge(x, attn, proj, xq_blk, gate_blk, mk, mv, xq_norm, w_out.astype(BF16), tm=512)


def kernel(x, mem, positions, ln_gain, w_out, mem_norm, w_mem_kv, xq_norm, xk_norm,
           s5_w_in, s5_lambda_re, s5_lambda_im, s5_log_step, s5_b_re, s5_b_im, s5_c_re, s5_c_im,
           s5_d, s5_w_glu, mla_w_in, mla_q_lora_norm, mla_kv_lora_norm, mla_w_uq, mla_w_ukv,
           mla_q_nope_norm, mla_k_nope_norm, mla_q_rope_norm, mla_k_rope_norm):
    depth = ln_gain.shape[0]
    for i in range(depth):
        j = i // 2
        if i % 2 == 0:
            x = _s5_layer(x, mem, ln_gain[i], s5_w_in[j], s5_lambda_re[j], s5_lambda_im[j], s5_log_step[j],
                          s5_b_re[j], s5_b_im[j], s5_c_re[j], s5_c_im[j], s5_d[j], s5_w_glu[j],
                          w_out[i], mem_norm[i], w_mem_kv[i], xq_norm[i], xk_norm[i])
        else:
            x = _mla_layer(x, mem, positions, ln_gain[i], mla_w_in[j], mla_q_lora_norm[j], mla_kv_lora_norm[j],
                           mla_w_uq[j], mla_w_ukv[j], mla_q_nope_norm[j], mla_k_nope_norm[j],
                           mla_q_rope_norm[j], mla_k_rope_norm[j],
                           w_out[i], mem_norm[i], w_mem_kv[i], xq_norm[i], xk_norm[i])
    return x
```

```python
import functools
import math

import jax
import jax.numpy as jnp
from jax import lax
from jax.experimental import pallas as pl
from jax.experimental.pallas import tpu as pltpu

D_MODEL = 1024
BRANCH_WIDTH = 2 * D_MODEL
XQ_WIDTH = BRANCH_WIDTH // 4
PRIMARY_WIDTH = BRANCH_WIDTH - XQ_WIDTH
X_HEADS = 4
X_HEAD_DIM = XQ_WIDTH // X_HEADS
S5_GROUP_CH = 16
S5_GROUPS = PRIMARY_WIDTH // S5_GROUP_CH
S5_STATE = 64
MLA_NOPE = 128
MLA_ROPE = 64
MLA_V = 128
MLA_HEADS = PRIMARY_WIDTH // MLA_V
MLA_Q_LORA = D_MODEL // 2
MLA_KV_LORA = D_MODEL // 4
ROPE_THETA = 10000.0
EPS = 1e-6

LANES = 128
MLA_QK_PAD = 2 * LANES
S5_CHUNK = 2 * LANES // S5_GROUP_CH
S5_EXPONENTS = list(range(S5_CHUNK + 1)) + [S5_CHUNK * 2 ** i for i in range(1, int(math.log2(LANES)))]
VMEM_LIMIT = 56 * 1024 * 1024

F32 = jnp.float32
BF16 = jnp.bfloat16


def _cparams(sem):
    return pltpu.CompilerParams(dimension_semantics=sem, vmem_limit_bytes=VMEM_LIMIT)


def _rms(x, g):
    return x * lax.rsqrt(jnp.mean(x * x, axis=-1, keepdims=True) + EPS) * g


def _norm_matmul_kernel(x_ref, g_ref, w_ref, o_ref, *, col_chunk):
    xn = _rms(x_ref[...].astype(F32), g_ref[...]).astype(BF16)
    for c in range(o_ref.shape[1] // col_chunk):
        sl = slice(c * col_chunk, (c + 1) * col_chunk)
        o_ref[:, sl] = jnp.dot(xn, w_ref[:, sl], preferred_element_type=F32).astype(o_ref.dtype)


def _norm_matmul(x, g, w, *, tm, col_chunk, name):
    n, d = x.shape
    wout = w.shape[1]
    return pl.pallas_call(
        functools.partial(_norm_matmul_kernel, col_chunk=col_chunk),
        grid=(n // tm,),
        in_specs=[pl.BlockSpec((tm, d), lambda i: (i, 0)),
                  pl.BlockSpec((1, d), lambda i: (0, 0)),
                  pl.BlockSpec((d, wout), lambda i: (0, 0))],
        out_specs=pl.BlockSpec((tm, wout), lambda i: (i, 0)),
        out_shape=jax.ShapeDtypeStruct((n, wout), BF16),
        compiler_params=_cparams(("parallel",)),
        name=name,
    )(x, g.reshape(1, d), w)


def _to_phase_order(a):
    n, d = a.shape
    return jnp.swapaxes(a.reshape(n // S5_CHUNK, S5_CHUNK, d), 0, 1).reshape(n, d)


def _from_phase_order(a):
    n, d = a.shape
    return jnp.swapaxes(a.reshape(S5_CHUNK, n // S5_CHUNK, d), 0, 1).reshape(n, d)


def _s5_in_proj_kernel(x_ref, g_ref, w_ref, o_ref, xn_ref, *, col_chunk):
    tm, d = x_ref.shape[1:]
    nc = tm // S5_CHUNK
    xn = _rms(_to_phase_order(x_ref[0]), g_ref[...]).astype(BF16)
    for s in range(S5_CHUNK):
        xn_ref[0, :, s * d:(s + 1) * d] = xn[s * nc:(s + 1) * nc]
    for c in range(o_ref.shape[2] // col_chunk):
        sl = slice(c * col_chunk, (c + 1) * col_chunk)
        o_ref[0, :, sl] = jnp.dot(xn, w_ref[:, sl], preferred_element_type=F32).astype(o_ref.dtype)


def _s5_in_proj(x, g, w, *, tm, col_chunk):
    b, l, d = x.shape
    wout = w.shape[1]
    nc = tm // S5_CHUNK
    return pl.pallas_call(
        functools.partial(_s5_in_proj_kernel, col_chunk=col_chunk),
        grid=(b, l // tm),
        in_specs=[pl.BlockSpec((1, tm, d), lambda i, j: (i, j, 0)),
                  pl.BlockSpec((1, d), lambda i, j: (0, 0)),
                  pl.BlockSpec((d, wout), lambda i, j: (0, 0))],
        out_specs=[pl.BlockSpec((1, tm, wout), lambda i, j: (i, j, 0)),
                   pl.BlockSpec((1, nc, S5_CHUNK * d), lambda i, j: (i, j, 0))],
        out_shape=[jax.ShapeDtypeStruct((b, l, wout), BF16),
                   jax.ShapeDtypeStruct((b, l // S5_CHUNK, S5_CHUNK * d), BF16)],
        compiler_params=_cparams(("parallel", "parallel")),
        name="s5_in_proj",
    )(x, g.reshape(1, d), w)


def _mem_kv_kernel(m_ref, g_ref, w_ref, kg_ref, k_ref, v_ref):
    mn = _rms(m_ref[0], g_ref[...]).astype(BF16)
    kv = jnp.dot(mn, w_ref[...], preferred_element_type=F32)
    for h in range(X_HEADS):
        sl = slice(h * X_HEAD_DIM, (h + 1) * X_HEAD_DIM)
        k_ref[0, :, sl] = _rms(kv[:, sl], kg_ref[...]).astype(BF16)
    v_ref[0] = kv[:, XQ_WIDTH:].astype(BF16)


def _mem_kv(mem, mem_norm, w_mem_kv, xk_norm):
    b, m, d = mem.shape
    out = jax.ShapeDtypeStruct((b, m, XQ_WIDTH), BF16)
    return pl.pallas_call(
        _mem_kv_kernel,
        grid=(b,),
        in_specs=[pl.BlockSpec((1, m, d), lambda i: (i, 0, 0)),
                  pl.BlockSpec((1, d), lambda i: (0, 0)),
                  pl.BlockSpec((d, 2 * XQ_WIDTH), lambda i: (0, 0)),
                  pl.BlockSpec((1, X_HEAD_DIM), lambda i: (0, 0))],
        out_specs=[pl.BlockSpec((1, m, XQ_WIDTH), lambda i: (i, 0, 0)),
                   pl.BlockSpec((1, m, XQ_WIDTH), lambda i: (i, 0, 0))],
        out_shape=[out, out],
        compiler_params=_cparams(("parallel",)),
        name="mem_kv",
    )(mem, mem_norm.reshape(1, d), w_mem_kv.astype(BF16), xk_norm.reshape(1, X_HEAD_DIM))


def _s5_pow_kernel(lr_ref, li_ref, ls_ref, pr_ref, pi_ref):
    lr, li = lr_ref[...], li_ref[...]
    step = jnp.exp(ls_ref[...])
    zr, zi = lr * step, li * step
    for n, e in enumerate(S5_EXPONENTS):
        mag = jnp.exp(zr * e)
        pr_ref[n] = mag * jnp.cos(zi * e)
        pi_ref[n] = mag * jnp.sin(zi * e)
    ar, ai = pr_ref[1], pi_ref[1]
    den = lr * lr + li * li
    pr_ref[len(S5_EXPONENTS)] = ((ar - 1.0) * lr + ai * li) / den
    pi_ref[len(S5_EXPONENTS)] = (ai * lr - (ar - 1.0) * li) / den


def _s5_pow(lam_re, lam_im, log_step):
    g, p = lam_re.shape
    out = jax.ShapeDtypeStruct((len(S5_EXPONENTS) + 1, g, p), F32)
    return pl.pallas_call(_s5_pow_kernel, out_shape=[out, out], name="s5_pow")(
        lam_re, lam_im, log_step.reshape(g, 1))


def _s5_asm_kernel(pr_ref, pi_ref, cr_ref, ci_ref, btr_ref, bti_ref, br_ref, bi_ref,
                   toep_ref, wout_ref, wst_ref):
    def group(i, carry):
        _s5_asm_group(i, pr_ref, pi_ref, cr_ref, ci_ref, btr_ref, bti_ref, br_ref, bi_ref,
                      toep_ref, wout_ref, wst_ref)
        return carry

    lax.fori_loop(0, pr_ref.shape[0], group, 0)


def _dot_3pass(a, b):
    a_hi, b_hi = a.astype(BF16), b.astype(BF16)
    a_lo = (a - a_hi.astype(F32)).astype(BF16)
    b_lo = (b - b_hi.astype(F32)).astype(BF16)
    dot = functools.partial(jnp.dot, preferred_element_type=F32)
    return dot(a_hi, b_hi) + (dot(a_hi, b_lo) + dot(a_lo, b_hi))


def _s5_asm_group(i, pr_ref, pi_ref, cr_ref, ci_ref, btr_ref, bti_ref, br_ref, bi_ref,
                  toep_ref, wout_ref, wst_ref):
    t = S5_CHUNK
    pr, pi = pr_ref[i], pi_ref[i]
    cr, ci = cr_ref[i], ci_ref[i]
    btr, bti = btr_ref[i], bti_ref[i]
    mr, mi = pr[len(S5_EXPONENTS):], pi[len(S5_EXPONENTS):]
    amr = pr[:t] * mr - pi[:t] * mi
    ami = pr[:t] * mi + pi[:t] * mr
    l_re, l_im, w_re, w_im, o_re, o_im = [], [], [], [], [], []
    for k in range(t):
        ar, ai = amr[k:k + 1], ami[k:k + 1]
        l_re.append(cr * ar - ci * ai)
        l_im.append(cr * ai + ci * ar)
        ar, ai = amr[t - 1 - k:t - k], ami[t - 1 - k:t - k]
        w_re.append(btr * ar - bti * ai)
        w_im.append(btr * ai + bti * ar)
        ar, ai = pr[k + 1:k + 2], pi[k + 1:k + 2]
        o_re.append(cr * ar - ci * ai)
        o_im.append(-(cr * ai + ci * ar))
    cat = lambda parts: jnp.concatenate(parts, axis=0)
    kt = _dot_3pass(cat(l_re), br_ref[i]) - _dot_3pass(cat(l_im), bi_ref[i])
    n = kt.shape[0]
    blk = lax.shift_right_logical(lax.broadcasted_iota(jnp.int32, kt.shape, 1), int(math.log2(S5_GROUP_CH)))
    toep = jnp.where(blk == 0, kt, 0.0)
    for s in range(1, t):
        shifted = jnp.concatenate([jnp.zeros((s * S5_GROUP_CH, n), F32), kt[:n - s * S5_GROUP_CH]], axis=0)
        toep = jnp.where(blk == s, shifted, toep)
    toep_ref[i] = toep.astype(BF16)
    wout_ref[i] = jnp.concatenate([cat(o_re), cat(o_im)], axis=1).astype(BF16)
    wst_ref[i] = jnp.concatenate([cat(w_re), cat(w_im)], axis=1).T.astype(BF16)


def _s5_asm(pw_re, pw_im, c_re, c_im, b_re, b_im):
    g, c, p = c_re.shape
    tc = S5_CHUNK * c
    gs = 8
    blk = lambda a: pl.BlockSpec((gs,) + a.shape[1:], lambda i: (i, 0, 0))
    bt_re, bt_im = b_re.transpose(0, 2, 1), b_im.transpose(0, 2, 1)
    tile = lambda a: jnp.tile(a, (1, 1, S5_CHUNK))
    args = (pw_re, pw_im, c_re, c_im, bt_re, bt_im, tile(b_re), tile(b_im))
    return pl.pallas_call(
        _s5_asm_kernel,
        grid=(g // gs,),
        in_specs=[blk(a) for a in args],
        out_specs=[pl.BlockSpec((gs, tc, tc), lambda i: (i, 0, 0)),
                   pl.BlockSpec((gs, tc, 2 * p), lambda i: (i, 0, 0)),
                   pl.BlockSpec((gs, 2 * p, tc), lambda i: (i, 0, 0))],
        out_shape=[jax.ShapeDtypeStruct((g, tc, tc), BF16), jax.ShapeDtypeStruct((g, tc, 2 * p), BF16),
                   jax.ShapeDtypeStruct((g, 2 * p, tc), BF16)],
        compiler_params=_cparams(("parallel",)),
        name="s5_asm",
    )(*args)


def _s5_ut_kernel(xn_ref, w_ref, o_ref):
    nb, nc, d = xn_ref.shape
    ut = lax.dot_general(w_ref[...], xn_ref[...].reshape(nb * nc, d), (((1,), (1,)), ((), ())),
                         preferred_element_type=F32)
    o_ref[...] = ut.astype(BF16).reshape(o_ref.shape)


def _s5_ut(xn, wut):
    b, nc, td = xn.shape
    d = td // S5_CHUNK
    return pl.pallas_call(
        _s5_ut_kernel,
        grid=(S5_CHUNK,),
        in_specs=[pl.BlockSpec((b, nc, d), lambda s: (0, 0, s)),
                  pl.BlockSpec(wut.shape, lambda s: (0, 0))],
        out_specs=pl.BlockSpec((S5_GROUPS, S5_GROUP_CH, b * nc), lambda s: (0, s, 0)),
        out_shape=jax.ShapeDtypeStruct((S5_GROUPS, S5_CHUNK * S5_GROUP_CH, b * nc), BF16),
        compiler_params=_cparams(("parallel",)),
        name="s5_ut",
    )(xn, wut)


def _s5_mix_kernel(x_ref, toep_ref, wst_ref, wout_ref, sr_ref, si_ref, d_ref, o_ref, *, nb):
    p = S5_STATE
    x = x_ref[0]
    hloc = jnp.dot(wst_ref[0], x, preferred_element_type=F32)
    lane = lax.broadcasted_iota(jnp.int32, (p, LANES), 1)
    n_steps = int(math.log2(LANES))
    pw = []
    for i in range(n_steps):
        keep = lane >= (1 << i)
        pw.append((jnp.where(keep, jnp.broadcast_to(sr_ref[0, :, i:i + 1], (p, LANES)), 0.0),
                   jnp.where(keep, jnp.broadcast_to(si_ref[0, :, i:i + 1], (p, LANES)), 0.0)))
    h_re, h_im = [], []
    for b in range(nb):
        hr = hloc[:p, b * LANES:(b + 1) * LANES]
        hi = hloc[p:, b * LANES:(b + 1) * LANES]
        for i in range(n_steps):
            rs, js = pltpu.roll(hr, 1 << i, 1), pltpu.roll(hi, 1 << i, 1)
            ar, ai = pw[i]
            hr, hi = hr + ar * rs - ai * js, hi + ar * js + ai * rs
        h_re.append(jnp.where(lane >= 1, pltpu.roll(hr, 1, 1), 0.0))
        h_im.append(jnp.where(lane >= 1, pltpu.roll(hi, 1, 1), 0.0))
    h = jnp.concatenate([jnp.concatenate(h_re, axis=1), jnp.concatenate(h_im, axis=1)], axis=0).astype(BF16)
    y = (jnp.dot(toep_ref[0], x, preferred_element_type=F32)
         + jnp.dot(wout_ref[0], h, preferred_element_type=F32)
         + d_ref[0] * x.astype(F32))
    o_ref[...] = jax.nn.gelu(y).astype(o_ref.dtype).reshape(o_ref.shape)


def _s5_mix(xg, toep, wst, wout, sc_re, sc_im, dcol, *, nb):
    g, tc, cols = xg.shape
    assert cols == nb * LANES, "one batch's chunks must fill exactly one 128-lane block"
    blk = lambda a: pl.BlockSpec((1,) + a.shape[1:], lambda i: (i, 0, 0))
    return pl.pallas_call(
        functools.partial(_s5_mix_kernel, nb=nb),
        grid=(g,),
        in_specs=[blk(a) for a in (xg, toep, wst, wout, sc_re, sc_im, dcol)],
        out_specs=pl.BlockSpec((S5_CHUNK, S5_GROUP_CH, cols), lambda i: (0, i, 0)),
        out_shape=jax.ShapeDtypeStruct((S5_CHUNK, g * S5_GROUP_CH, cols), BF16),
        compiler_params=_cparams(("parallel",)),
        name="s5_mix",
    )(xg, toep, wst, wout, sc_re, sc_im, dcol)


def _glu_kernel(y_ref, w_ref, o_ref, *, col_chunk):
    y = y_ref[0].T
    half = o_ref.shape[-1]
    for c in range(half // col_chunk):
        a = jnp.dot(y, w_ref[:, c * col_chunk:(c + 1) * col_chunk], preferred_element_type=F32)
        g = jnp.dot(y, w_ref[:, half + c * col_chunk:half + (c + 1) * col_chunk], preferred_element_type=F32)
        o_ref[:, :, c * col_chunk:(c + 1) * col_chunk] = (
            (a * jax.nn.sigmoid(g)).astype(o_ref.dtype).reshape(o_ref.shape[:2] + (col_chunk,)))


def _glu(yt, w, *, nb, col_chunk):
    t, k, cols = yt.shape
    nc = cols // nb
    half = w.shape[1] // 2
    return pl.pallas_call(
        functools.partial(_glu_kernel, col_chunk=col_chunk),
        grid=(t,),
        in_specs=[pl.BlockSpec((1, k, cols), lambda j: (j, 0, 0)),
                  pl.BlockSpec(w.shape, lambda j: (0, 0))],
        out_specs=pl.BlockSpec((nb, nc, half), lambda j: (0, 0, j)),
        out_shape=jax.ShapeDtypeStruct((nb, nc, t * half), BF16),
        compiler_params=_cparams(("parallel",)),
        name="glu",
    )(yt, w)


def _merge_kernel(x_ref, mix_ref, xq_ref, gate_ref, k_ref, v_ref, qg_ref, w_ref, o_ref, cat_ref, *, phased):
    gate = gate_ref[0].astype(F32)
    sg = gate * jax.nn.sigmoid(gate)
    if phased:
        nc = gate.shape[0] // S5_CHUNK
        for s in range(S5_CHUNK):
            rows = slice(s * nc, (s + 1) * nc)
            mix = mix_ref[0, :, s * PRIMARY_WIDTH:(s + 1) * PRIMARY_WIDTH].astype(F32)
            cat_ref[rows, :PRIMARY_WIDTH] = (mix * sg[rows, :PRIMARY_WIDTH]).astype(BF16)
    else:
        cat_ref[:, :PRIMARY_WIDTH] = (mix_ref[0].astype(F32) * sg[:, :PRIMARY_WIDTH]).astype(BF16)
    scale = X_HEAD_DIM ** -0.5
    for h in range(X_HEADS):
        sl = slice(h * X_HEAD_DIM, (h + 1) * X_HEAD_DIM)
        q = _rms(xq_ref[0, :, sl].astype(F32), qg_ref[...]).astype(BF16)
        s = lax.dot_general(q, k_ref[0, :, sl], (((1,), (1,)), ((), ())), preferred_element_type=F32) * scale
        p = jnp.exp(s - jnp.max(s, axis=-1, keepdims=True))
        p = (p / jnp.sum(p, axis=-1, keepdims=True)).astype(BF16)
        mo = jnp.dot(p, v_ref[0, :, sl], preferred_element_type=F32)
        osl = slice(PRIMARY_WIDTH + h * X_HEAD_DIM, PRIMARY_WIDTH + (h + 1) * X_HEAD_DIM)
        cat_ref[:, osl] = (mo * sg[:, osl]).astype(BF16)
    delta = jnp.dot(cat_ref[...], w_ref[...], preferred_element_type=F32)
    o_ref[0] = x_ref[0] + (_from_phase_order(delta) if phased else delta)


def _merge(x, mix, proj, xq_blk, gate_blk, mk, mv, xq_norm, w_out, *, tm, phased=False):
    b, l, d = x.shape
    m = mk.shape[1]
    mix_spec = (pl.BlockSpec((1, tm // S5_CHUNK, S5_CHUNK * PRIMARY_WIDTH), lambda i, j: (i, j, 0)) if phased
                else pl.BlockSpec((1, tm, PRIMARY_WIDTH), lambda i, j: (i, j, 0)))
    return pl.pallas_call(
        functools.partial(_merge_kernel, phased=phased),
        grid=(b, l // tm),
        in_specs=[pl.BlockSpec((1, tm, d), lambda i, j: (i, j, 0)),
                  mix_spec,
                  pl.BlockSpec((1, tm, XQ_WIDTH), lambda i, j: (i, j, xq_blk)),
                  pl.BlockSpec((1, tm, BRANCH_WIDTH), lambda i, j: (i, j, gate_blk)),
                  pl.BlockSpec((1, m, XQ_WIDTH), lambda i, j: (i, 0, 0)),
                  pl.BlockSpec((1, m, XQ_WIDTH), lambda i, j: (i, 0, 0)),
                  pl.BlockSpec((1, X_HEAD_DIM), lambda i, j: (0, 0)),
                  pl.BlockSpec((BRANCH_WIDTH, d), lambda i, j: (0, 0))],
        out_specs=pl.BlockSpec((1, tm, d), lambda i, j: (i, j, 0)),
        out_shape=jax.ShapeDtypeStruct((b, l, d), F32),
        scratch_shapes=[pltpu.VMEM((tm, BRANCH_WIDTH), BF16)],
        compiler_params=_cparams(("parallel", "parallel")),
        name="merge",
    )(x, mix, proj, proj, mk, mv, xq_norm.reshape(1, X_HEAD_DIM), w_out)


def _mla_qkv_kernel(cq_ref, ckv_ref, kr_ref, posc_ref, posr_ref, invfr_ref, invfc_ref, gq_ref, gkv_ref, gqn_ref,
                    gkn_ref, gqr_ref, gkr_ref, wqt_ref, wk_ref, wvt_ref, qt_ref, kn_ref, krope_ref, vt_ref):
    half = MLA_ROPE // 2
    tm = cq_ref.shape[1]
    qscale = (MLA_NOPE + MLA_ROPE) ** -0.5 * math.log2(math.e)

    cq = _rms(cq_ref[0].astype(F32), gq_ref[...])
    ckv = _rms(ckv_ref[0].astype(F32), gkv_ref[...])
    cq_t = cq.T.astype(BF16)
    ckv_t = ckv.T.astype(BF16)
    q_t = jnp.dot(wqt_ref[...], cq_t, preferred_element_type=F32)
    v_t = jnp.dot(wvt_ref[...], ckv_t, preferred_element_type=F32)
    k_n = jnp.dot(ckv.astype(BF16), wk_ref[...], preferred_element_type=F32)

    ang_t = invfc_ref[...] * posr_ref[0].astype(F32)
    cos_t, sin_t = jnp.cos(ang_t), jnp.sin(ang_t)
    g_nope = jnp.broadcast_to(gqn_ref[...], (MLA_NOPE, tm)) * qscale
    g_r1 = jnp.broadcast_to(gqr_ref[:half, :], (half, tm)) * qscale
    g_r2 = jnp.broadcast_to(gqr_ref[half:, :], (half, tm)) * qscale
    for h in range(MLA_HEADS):
        q = q_t[h * MLA_QK_PAD:(h + 1) * MLA_QK_PAD]
        nope = q[:MLA_NOPE]
        r = lax.rsqrt(jnp.mean(nope * nope, axis=0, keepdims=True) + EPS)
        qt_ref[0, h, :MLA_NOPE, :] = (nope * r * g_nope).astype(BF16)
        x1, x2 = q[MLA_NOPE:MLA_NOPE + half], q[MLA_NOPE + half:MLA_NOPE + MLA_ROPE]
        ss = jnp.sum(x1 * x1, axis=0, keepdims=True) + jnp.sum(x2 * x2, axis=0, keepdims=True)
        r = lax.rsqrt(ss * (1.0 / MLA_ROPE) + EPS)
        x1, x2 = x1 * r * g_r1, x2 * r * g_r2
        qt_ref[0, h, MLA_NOPE:MLA_NOPE + half, :] = (x1 * cos_t - x2 * sin_t).astype(BF16)
        qt_ref[0, h, MLA_NOPE + half:MLA_NOPE + MLA_ROPE, :] = (x1 * sin_t + x2 * cos_t).astype(BF16)
        qt_ref[0, h, MLA_NOPE + MLA_ROPE:, :] = jnp.zeros((MLA_QK_PAD - MLA_NOPE - MLA_ROPE, tm), BF16)
        kn_ref[0, h] = _rms(k_n[:, h * MLA_NOPE:(h + 1) * MLA_NOPE], gkn_ref[...]).astype(BF16)
        vt_ref[0, h] = v_t[h * MLA_V:(h + 1) * MLA_V].astype(BF16)

    ang = posc_ref[0].astype(F32) * invfr_ref[...]
    lane = lax.broadcasted_iota(jnp.int32, ang.shape, 1)
    cos = jnp.where(lane < MLA_ROPE, jnp.cos(ang), 0.0)
    sin = jnp.sin(ang)
    sin_lo = jnp.where(lane < half, -sin, 0.0)
    sin_hi = jnp.where((lane >= half) & (lane < MLA_ROPE), sin, 0.0)
    x = kr_ref[0].astype(F32)
    xn = x * lax.rsqrt(jnp.sum(x * x, axis=-1, keepdims=True) * (1.0 / MLA_ROPE) + EPS) * gkr_ref[...]
    krope_ref[0] = (xn * cos + pltpu.roll(xn, LANES - half, 1) * sin_lo
                    + pltpu.roll(xn, half, 1) * sin_hi).astype(BF16)


def _mla_qkv(proj, cq_blk, ckv_blk, kr_blk, positions, gains, wqt, wk, wvt, *, tm):
    b, l, _ = proj.shape
    hh = MLA_HEADS
    half = MLA_ROPE // 2
    inv_freq = ROPE_THETA ** (-jnp.arange(half, dtype=F32) / half)
    invf_row = jnp.concatenate([inv_freq, inv_freq, jnp.zeros((LANES - MLA_ROPE,), F32)]).reshape(1, LANES)
    const = lambda a: pl.BlockSpec(a.shape, lambda i, j: (0,) * a.ndim)
    gq, gkv, gqn, gkn, gqr, gkr = gains
    consts = [invf_row, inv_freq.reshape(half, 1), gq.reshape(1, -1), gkv.reshape(1, -1), gqn.reshape(-1, 1),
              gkn.reshape(1, -1), gqr.reshape(-1, 1),
              jnp.concatenate([gkr, jnp.zeros((LANES - MLA_ROPE,), F32)]).reshape(1, LANES), wqt, wk, wvt]
    return pl.pallas_call(
        _mla_qkv_kernel,
        grid=(b, l // tm),
        in_specs=[pl.BlockSpec((1, tm, MLA_Q_LORA), lambda i, j: (i, j, cq_blk)),
                  pl.BlockSpec((1, tm, MLA_KV_LORA), lambda i, j: (i, j, ckv_blk)),
                  pl.BlockSpec((1, tm, LANES), lambda i, j: (i, j, kr_blk)),
                  pl.BlockSpec((1, tm, 1), lambda i, j: (i, j, 0)),
                  pl.BlockSpec((1, 1, tm), lambda i, j: (i, 0, j))] + [const(a) for a in consts],
        out_specs=[pl.BlockSpec((1, hh, MLA_QK_PAD, tm), lambda i, j: (i, 0, 0, j)),
                   pl.BlockSpec((1, hh, tm, MLA_NOPE), lambda i, j: (i, 0, j, 0)),
                   pl.BlockSpec((1, tm, LANES), lambda i, j: (i, j, 0)),
                   pl.BlockSpec((1, hh, MLA_V, tm), lambda i, j: (i, 0, 0, j))],
        out_shape=[jax.ShapeDtypeStruct((b, hh, MLA_QK_PAD, l), BF16),
                   jax.ShapeDtypeStruct((b, hh, l, MLA_NOPE), BF16),
                   jax.ShapeDtypeStruct((b, l, LANES), BF16),
                   jax.ShapeDtypeStruct((b, hh, MLA_V, l), BF16)],
        compiler_params=_cparams(("parallel", "parallel")),
        name="mla_qkv",
    )(proj, proj, proj, positions.reshape(b, l, 1), positions.reshape(b, 1, l), *consts)


def _flash_kernel(qt_ref, kn_ref, kr_ref, vt_ref, o_ref, m_ref, l_ref, acc_ref, *, tq, hp):
    qi = pl.program_id(2)
    m_ref[...] = jnp.full(m_ref.shape, -jnp.inf, F32)
    l_ref[...] = jnp.zeros(l_ref.shape, F32)
    acc_ref[...] = jnp.zeros(acc_ref.shape, F32)

    def block(j, diagonal):
        rows = pl.ds(pl.multiple_of(j * tq, tq), tq)
        k_rope = kr_ref[0, rows, :]
        scores = []
        for h in range(hp):
            k = jnp.concatenate([kn_ref[0, h, rows, :], k_rope], axis=-1)
            scores.append(jnp.dot(k, qt_ref[0, h], preferred_element_type=F32))
        for h in range(hp):
            s = scores[h]
            if diagonal:
                key = lax.broadcasted_iota(jnp.int32, s.shape, 0)
                qry = lax.broadcasted_iota(jnp.int32, s.shape, 1)
                s = jnp.where(key <= qry, s, jnp.finfo(F32).min)
            m = m_ref[h]
            m_new = jnp.maximum(m, jnp.max(s, axis=0, keepdims=True))
            alpha = jnp.exp2(m - m_new)
            p = jnp.exp2(s - m_new)
            l_ref[h] = alpha * l_ref[h] + jnp.sum(p, axis=0, keepdims=True)
            acc_ref[h] = alpha * acc_ref[h] + jnp.dot(vt_ref[0, h, :, rows], p.astype(BF16),
                                                      preferred_element_type=F32)
            m_ref[h] = m_new

    def body(j, carry):
        block(j, False)
        return carry

    lax.fori_loop(0, qi, body, 0)
    block(qi, True)
    for h in range(hp):
        o_ref[0, :, h * MLA_V:(h + 1) * MLA_V] = (acc_ref[h] / l_ref[h]).T.astype(o_ref.dtype)


def _flash(qt, kn, kr, vt, *, tq, hp):
    b, hh, _, l = qt.shape
    return pl.pallas_call(
        functools.partial(_flash_kernel, tq=tq, hp=hp),
        grid=(b, hh // hp, l // tq),
        in_specs=[pl.BlockSpec((1, hp, MLA_QK_PAD, tq), lambda i, h, j: (i, h, 0, j)),
                  pl.BlockSpec((1, hp, l, MLA_NOPE), lambda i, h, j: (i, h, 0, 0)),
                  pl.BlockSpec((1, l, LANES), lambda i, h, j: (i, 0, 0)),
                  pl.BlockSpec((1, hp, MLA_V, l), lambda i, h, j: (i, h, 0, 0))],
        out_specs=pl.BlockSpec((1, tq, hp * MLA_V), lambda i, h, j: (i, j, h)),
        out_shape=jax.ShapeDtypeStruct((b, l, hh * MLA_V), BF16),
        scratch_shapes=[pltpu.VMEM((hp, 1, tq), F32), pltpu.VMEM((hp, 1, tq), F32),
                        pltpu.VMEM((hp, MLA_V, tq), F32)],
        compiler_params=_cparams(("parallel", "parallel", "parallel")),
        name="flash",
    )(qt, kn, kr, vt)


def _s5_layer(x, mem, ln, w_in, lam_re, lam_im, log_step, b_re, b_im, c_re, c_im, d, w_glu,
              w_out, mem_norm, w_mem_kv, xq_norm, xk_norm):
    b, l, dm = x.shape
    w_gx = jnp.concatenate([w_in[:, PRIMARY_WIDTH + XQ_WIDTH:], w_in[:, PRIMARY_WIDTH:PRIMARY_WIDTH + XQ_WIDTH]],
                           axis=1).astype(BF16)
    tm = 512
    proj, xn = _s5_in_proj(x, ln, w_gx, tm=tm, col_chunk=512)
    xg = _s5_ut(xn, w_in[:, :PRIMARY_WIDTH].T.astype(BF16))
    pw_re, pw_im = _s5_pow(lam_re, lam_im, log_step)
    toep, wout, wst = _s5_asm(pw_re.transpose(1, 0, 2), pw_im.transpose(1, 0, 2), c_re, c_im, b_re, b_im)
    n_scan = int(math.log2(LANES))
    first = S5_EXPONENTS.index(S5_CHUNK)
    col = lambda pw: jnp.pad(pw[first:first + n_scan].transpose(1, 2, 0), ((0, 0), (0, 0), (0, 8 - n_scan)))
    dcol = jnp.tile(d.reshape(S5_GROUPS, 1, S5_GROUP_CH), (1, S5_CHUNK, 1)).reshape(S5_GROUPS, -1, 1)
    yt = _s5_mix(xg, toep, wst, wout, col(pw_re), col(pw_im), dcol, nb=b)
    y = _glu(yt, w_glu.astype(BF16), nb=b, col_chunk=256)
    mk, mv = _mem_kv(mem, mem_norm, w_mem_kv, xk_norm)
    return _merge(x, y, proj, BRANCH_WIDTH // XQ_WIDTH, 0, mk, mv, xq_norm, w_out.astype(BF16), tm=tm, phased=True)


def _mla_layer(x, mem, positions, ln, w_in, q_lora_norm, kv_lora_norm, w_uq, w_ukv, q_nope_norm, k_nope_norm,
               q_rope_norm, k_rope_norm, w_out, mem_norm, w_mem_kv, xq_norm, xk_norm):
    b, l, dm = x.shape
    o1 = MLA_Q_LORA
    o2 = o1 + MLA_KV_LORA
    o3 = o2 + MLA_ROPE
    o4 = o3 + XQ_WIDTH
    w_perm = jnp.concatenate([w_in[:, o4:], w_in[:, :o1], w_in[:, o3:o4], w_in[:, o1:o2], w_in[:, o2:o3],
                              jnp.zeros((dm, LANES - MLA_ROPE), w_in.dtype)], axis=1).astype(BF16)
    proj = _norm_matmul(x.reshape(b * l, dm), ln, w_perm, tm=512, col_chunk=384, name="mla_in_proj")
    proj = proj.reshape(b, l, -1)
    gate_blk = 0
    cq_blk = BRANCH_WIDTH // MLA_Q_LORA
    xq_blk = (BRANCH_WIDTH + MLA_Q_LORA) // XQ_WIDTH
    ckv_blk = (BRANCH_WIDTH + MLA_Q_LORA + XQ_WIDTH) // MLA_KV_LORA
    kr_blk = (BRANCH_WIDTH + MLA_Q_LORA + XQ_WIDTH + MLA_KV_LORA) // LANES
    wq = w_uq.reshape(MLA_Q_LORA, MLA_HEADS, MLA_NOPE + MLA_ROPE)
    wq = jnp.pad(wq, ((0, 0), (0, 0), (0, MLA_QK_PAD - MLA_NOPE - MLA_ROPE)))
    wqt = wq.reshape(MLA_Q_LORA, MLA_HEADS * MLA_QK_PAD).T.astype(BF16)
    wkv = w_ukv.reshape(MLA_KV_LORA, MLA_HEADS, MLA_NOPE + MLA_V)
    wk = wkv[:, :, :MLA_NOPE].reshape(MLA_KV_LORA, MLA_HEADS * MLA_NOPE).astype(BF16)
    wvt = wkv[:, :, MLA_NOPE:].reshape(MLA_KV_LORA, MLA_HEADS * MLA_V).T.astype(BF16)
    qt, kn, kr, vt = _mla_qkv(proj, cq_blk, ckv_blk, kr_blk, positions,
                              (q_lora_norm, kv_lora_norm, q_nope_norm, k_nope_norm, q_rope_norm, k_rope_norm),
                              wqt, wk, wvt, tm=256)
    attn = _flash(qt, kn, kr, vt, tq=512, hp=4)
    mk, mv = _mem_kv(mem, mem_norm, w_mem_kv, xk_norm)
    return _merge(x, attn, proj, xq_blk, gate_blk, mk, mv, xq_norm, w_out.astype(BF16), tm=512)


def kernel(x, mem, positions, ln_gain, w_out, mem_norm, w_mem_kv, xq_norm, xk_norm,
           s5_w_in, s5_lambda_re, s5_lambda_im, s5_log_step, s5_b_re, s5_b_im, s5_c_re, s5_c_im,
           s5_d, s5_w_glu, mla_w_in, mla_q_lora_norm, mla_kv_lora_norm, mla_w_uq, mla_w_ukv,
           mla_q_nope_norm, mla_k_nope_norm, mla_q_rope_norm, mla_k_rope_norm):
    depth = ln_gain.shape[0]
    for i in range(depth):
        j = i // 2
        if i % 2 == 0:
            x = _s5_layer(x, mem, ln_gain[i], s5_w_in[j], s5_lambda_re[j], s5_lambda_im[j], s5_log_step[j],
                          s5_b_re[j], s5_b_im[j], s5_c_re[j], s5_c_im[j], s5_d[j], s5_w_glu[j],
                          w_out[i], mem_norm[i], w_mem_kv[i], xq_norm[i], xk_norm[i])
        else:
            x = _mla_layer(x, mem, positions, ln_gain[i], mla_w_in[j], mla_q_lora_norm[j], mla_kv_lora_norm[j],
                           mla_w_uq[j], mla_w_ukv[j], mla_q_nope_norm[j], mla_k_nope_norm[j],
                           mla_q_rope_norm[j], mla_k_rope_norm[j],
                           w_out[i], mem_norm[i], w_mem_kv[i], xq_norm[i], xk_norm[i])
    return x
```

```python
import functools
import math

import jax
import jax.numpy as jnp
from jax import lax
from jax.experimental import pallas as pl
from jax.experimental.pallas import tpu as pltpu

D_MODEL = 1024
BRANCH_WIDTH = 2 * D_MODEL
XQ_WIDTH = BRANCH_WIDTH // 4
PRIMARY_WIDTH = BRANCH_WIDTH - XQ_WIDTH
X_HEADS = 4
X_HEAD_DIM = XQ_WIDTH // X_HEADS
S5_GROUP_CH = 16
S5_GROUPS = PRIMARY_WIDTH // S5_GROUP_CH
S5_STATE = 64
MLA_NOPE = 128
MLA_ROPE = 64
MLA_V = 128
MLA_HEADS = PRIMARY_WIDTH // MLA_V
MLA_Q_LORA = D_MODEL // 2
MLA_KV_LORA = D_MODEL // 4
ROPE_THETA = 10000.0
EPS = 1e-6

LANES = 128
MLA_QK_PAD = 2 * LANES
S5_CHUNK = 2 * LANES // S5_GROUP_CH
S5_EXPONENTS = list(range(S5_CHUNK + 1)) + [S5_CHUNK * 2 ** i for i in range(1, int(math.log2(LANES)))]
VMEM_LIMIT = 56 * 1024 * 1024

F32 = jnp.float32
BF16 = jnp.bfloat16


def _cparams(sem):
    return pltpu.CompilerParams(dimension_semantics=sem, vmem_limit_bytes=VMEM_LIMIT)


def _rms(x, g):
    return x * lax.rsqrt(jnp.mean(x * x, axis=-1, keepdims=True) + EPS) * g


def _norm_matmul_kernel(x_ref, g_ref, w_ref, o_ref, *, col_chunk):
    xn = _rms(x_ref[...].astype(F32), g_ref[...]).astype(BF16)
    for c in range(o_ref.shape[1] // col_chunk):
        sl = slice(c * col_chunk, (c + 1) * col_chunk)
        o_ref[:, sl] = jnp.dot(xn, w_ref[:, sl], preferred_element_type=F32).astype(o_ref.dtype)


def _norm_matmul(x, g, w, *, tm, col_chunk, name):
    n, d = x.shape
    wout = w.shape[1]
    return pl.pallas_call(
        functools.partial(_norm_matmul_kernel, col_chunk=col_chunk),
        grid=(n // tm,),
        in_specs=[pl.BlockSpec((tm, d), lambda i: (i, 0)),
                  pl.BlockSpec((1, d), lambda i: (0, 0)),
                  pl.BlockSpec((d, wout), lambda i: (0, 0))],
        out_specs=pl.BlockSpec((tm, wout), lambda i: (i, 0)),
        out_shape=jax.ShapeDtypeStruct((n, wout), BF16),
        compiler_params=_cparams(("parallel",)),
        name=name,
    )(x, g.reshape(1, d), w)


def _to_phase_order(a):
    n, d = a.shape
    return jnp.swapaxes(a.reshape(n // S5_CHUNK, S5_CHUNK, d), 0, 1).reshape(n, d)


def _from_phase_order(a):
    n, d = a.shape
    return jnp.swapaxes(a.reshape(S5_CHUNK, n // S5_CHUNK, d), 0, 1).reshape(n, d)


def _s5_in_proj_kernel(x_ref, g_ref, w_ref, o_ref, xn_ref, *, col_chunk):
    tm, d = x_ref.shape[1:]
    nc = tm // S5_CHUNK
    xn = _rms(_to_phase_order(x_ref[0]), g_ref[...]).astype(BF16)
    for s in range(S5_CHUNK):
        xn_ref[0, :, s * d:(s + 1) * d] = xn[s * nc:(s + 1) * nc]
    for c in range(o_ref.shape[2] // col_chunk):
        sl = slice(c * col_chunk, (c + 1) * col_chunk)
        o_ref[0, :, sl] = jnp.dot(xn, w_ref[:, sl], preferred_element_type=F32).astype(o_ref.dtype)


def _s5_in_proj(x, g, w, *, tm, col_chunk):
    b, l, d = x.shape
    wout = w.shape[1]
    nc = tm // S5_CHUNK
    return pl.pallas_call(
        functools.partial(_s5_in_proj_kernel, col_chunk=col_chunk),
        grid=(b, l // tm),
        in_specs=[pl.BlockSpec((1, tm, d), lambda i, j: (i, j, 0)),
                  pl.BlockSpec((1, d), lambda i, j: (0, 0)),
                  pl.BlockSpec((d, wout), lambda i, j: (0, 0))],
        out_specs=[pl.BlockSpec((1, tm, wout), lambda i, j: (i, j, 0)),
                   pl.BlockSpec((1, nc, S5_CHUNK * d), lambda i, j: (i, j, 0))],
        out_shape=[jax.ShapeDtypeStruct((b, l, wout), BF16),
                   jax.ShapeDtypeStruct((b, l // S5_CHUNK, S5_CHUNK * d), BF16)],
        compiler_params=_cparams(("parallel", "parallel")),
        name="s5_in_proj",
    )(x, g.reshape(1, d), w)


def _mem_kv_kernel(m_ref, g_ref, w_ref, kg_ref, k_ref, v_ref):
    mn = _rms(m_ref[0], g_ref[...]).astype(BF16)
    kv = jnp.dot(mn, w_ref[...], preferred_element_type=F32)
    for h in range(X_HEADS):
        sl = slice(h * X_HEAD_DIM, (h + 1) * X_HEAD_DIM)
        k_ref[0, :, sl] = _rms(kv[:, sl], kg_ref[...]).astype(BF16)
    v_ref[0] = kv[:, XQ_WIDTH:].astype(BF16)


def _mem_kv(mem, mem_norm, w_mem_kv, xk_norm):
    b, m, d = mem.shape
    out = jax.ShapeDtypeStruct((b, m, XQ_WIDTH), BF16)
    return pl.pallas_call(
        _mem_kv_kernel,
        grid=(b,),
        in_specs=[pl.BlockSpec((1, m, d), lambda i: (i, 0, 0)),
                  pl.BlockSpec((1, d), lambda i: (0, 0)),
                  pl.BlockSpec((d, 2 * XQ_WIDTH), lambda i: (0, 0)),
                  pl.BlockSpec((1, X_HEAD_DIM), lambda i: (0, 0))],
        out_specs=[pl.BlockSpec((1, m, XQ_WIDTH), lambda i: (i, 0, 0)),
                   pl.BlockSpec((1, m, XQ_WIDTH), lambda i: (i, 0, 0))],
        out_shape=[out, out],
        compiler_params=_cparams(("parallel",)),
        name="mem_kv",
    )(mem, mem_norm.reshape(1, d), w_mem_kv.astype(BF16), xk_norm.reshape(1, X_HEAD_DIM))


def _s5_pow_kernel(lr_ref, li_ref, ls_ref, pr_ref, pi_ref):
    lr, li = lr_ref[...], li_ref[...]
    step = jnp.exp(ls_ref[...])
    zr, zi = lr * step, li * step
    for n, e in enumerate(S5_EXPONENTS):
        mag = jnp.exp(zr * e)
        pr_ref[n] = mag * jnp.cos(zi * e)
        pi_ref[n] = mag * jnp.sin(zi * e)
    ar, ai = pr_ref[1], pi_ref[1]
    den = lr * lr + li * li
    pr_ref[len(S5_EXPONENTS)] = ((ar - 1.0) * lr + ai * li) / den
    pi_ref[len(S5_EXPONENTS)] = (ai * lr - (ar - 1.0) * li) / den


def _s5_pow(lam_re, lam_im, log_step):
    g, p = lam_re.shape
    out = jax.ShapeDtypeStruct((len(S5_EXPONENTS) + 1, g, p), F32)
    return pl.pallas_call(_s5_pow_kernel, out_shape=[out, out], name="s5_pow")(
        lam_re, lam_im, log_step.reshape(g, 1))


def _s5_asm_kernel(pr_ref, pi_ref, cr_ref, ci_ref, btr_ref, bti_ref, br_ref, bi_ref,
                   toep_ref, wout_ref, wst_ref):
    def group(i, carry):
        _s5_asm_group(i, pr_ref, pi_ref, cr_ref, ci_ref, btr_ref, bti_ref, br_ref, bi_ref,
                      toep_ref, wout_ref, wst_ref)
        return carry

    lax.fori_loop(0, pr_ref.shape[0], group, 0)


def _dot_3pass(a, b):
    a_hi, b_hi = a.astype(BF16), b.astype(BF16)
    a_lo = (a - a_hi.astype(F32)).astype(BF16)
    b_lo = (b - b_hi.astype(F32)).astype(BF16)
    dot = functools.partial(jnp.dot, preferred_element_type=F32)
    return dot(a_hi, b_hi) + (dot(a_hi, b_lo) + dot(a_lo, b_hi))


def _s5_asm_group(i, pr_ref, pi_ref, cr_ref, ci_ref, btr_ref, bti_ref, br_ref, bi_ref,
                  toep_ref, wout_ref, wst_ref):
    t = S5_CHUNK
    pr, pi = pr_ref[i], pi_ref[i]
    cr, ci = cr_ref[i], ci_ref[i]
    btr, bti = btr_ref[i], bti_ref[i]
    mr, mi = pr[len(S5_EXPONENTS):], pi[len(S5_EXPONENTS):]
    amr = pr[:t] * mr - pi[:t] * mi
    ami = pr[:t] * mi + pi[:t] * mr
    l_re, l_im, w_re, w_im, o_re, o_im = [], [], [], [], [], []
    for k in range(t):
        ar, ai = amr[k:k + 1], ami[k:k + 1]
        l_re.append(cr * ar - ci * ai)
        l_im.append(cr * ai + ci * ar)
        ar, ai = amr[t - 1 - k:t - k], ami[t - 1 - k:t - k]
        w_re.append(btr * ar - bti * ai)
        w_im.append(btr * ai + bti * ar)
        ar, ai = pr[k + 1:k + 2], pi[k + 1:k + 2]
        o_re.append(cr * ar - ci * ai)
        o_im.append(-(cr * ai + ci * ar))
    cat = lambda parts: jnp.concatenate(parts, axis=0)
    kt = _dot_3pass(cat(l_re), br_ref[i]) - _dot_3pass(cat(l_im), bi_ref[i])
    n = kt.shape[0]
    blk = lax.shift_right_logical(lax.broadcasted_iota(jnp.int32, kt.shape, 1), int(math.log2(S5_GROUP_CH)))
    toep = jnp.where(blk == 0, kt, 0.0)
    for s in range(1, t):
        shifted = jnp.concatenate([jnp.zeros((s * S5_GROUP_CH, n), F32), kt[:n - s * S5_GROUP_CH]], axis=0)
        toep = jnp.where(blk == s, shifted, toep)
    toep_ref[i] = toep.astype(BF16)
    wout_ref[i] = jnp.concatenate([cat(o_re), cat(o_im)], axis=1).astype(BF16)
    wst_ref[i] = jnp.concatenate([cat(w_re), cat(w_im)], axis=1).T.astype(BF16)


def _s5_asm(pw_re, pw_im, c_re, c_im, b_re, b_im):
    g, c, p = c_re.shape
    tc = S5_CHUNK * c
    gs = 8
    blk = lambda a: pl.BlockSpec((gs,) + a.shape[1:], lambda i: (i, 0, 0))
    bt_re, bt_im = b_re.transpose(0, 2, 1), b_im.transpose(0, 2, 1)
    tile = lambda a: jnp.tile(a, (1, 1, S5_CHUNK))
    args = (pw_re, pw_im, c_re, c_im, bt_re, bt_im, tile(b_re), tile(b_im))
    return pl.pallas_call(
        _s5_asm_kernel,
        grid=(g // gs,),
        in_specs=[blk(a) for a in args],
        out_specs=[pl.BlockSpec((gs, tc, tc), lambda i: (i, 0, 0)),
                   pl.BlockSpec((gs, tc, 2 * p), lambda i: (i, 0, 0)),
                   pl.BlockSpec((gs, 2 * p, tc), lambda i: (i, 0, 0))],
        out_shape=[jax.ShapeDtypeStruct((g, tc, tc), BF16), jax.ShapeDtypeStruct((g, tc, 2 * p), BF16),
                   jax.ShapeDtypeStruct((g, 2 * p, tc), BF16)],
        compiler_params=_cparams(("parallel",)),
        name="s5_asm",
    )(*args)


def _s5_ut_kernel(xn_ref, w_ref, o_ref):
    nb, nc, d = xn_ref.shape
    ut = lax.dot_general(w_ref[...], xn_ref[...].reshape(nb * nc, d), (((1,), (1,)), ((), ())),
                         preferred_element_type=F32)
    o_ref[...] = ut.astype(BF16).reshape(o_ref.shape)


def _s5_ut(xn, wut):
    b, nc, td = xn.shape
    d = td // S5_CHUNK
    return pl.pallas_call(
        _s5_ut_kernel,
        grid=(S5_CHUNK,),
        in_specs=[pl.BlockSpec((b, nc, d), lambda s: (0, 0, s)),
                  pl.BlockSpec(wut.shape, lambda s: (0, 0))],
        out_specs=pl.BlockSpec((S5_GROUPS, S5_GROUP_CH, b * nc), lambda s: (0, s, 0)),
        out_shape=jax.ShapeDtypeStruct((S5_GROUPS, S5_CHUNK * S5_GROUP_CH, b * nc), BF16),
        compiler_params=_cparams(("parallel",)),
        name="s5_ut",
    )(xn, wut)


def _s5_mix_kernel(x_ref, toep_ref, wst_ref, wout_ref, sr_ref, si_ref, d_ref, o_ref, *, nb):
    p = S5_STATE
    x = x_ref[0]
    hloc = jnp.dot(wst_ref[0], x, preferred_element_type=F32)
    lane = lax.broadcasted_iota(jnp.int32, (p, LANES), 1)
    n_steps = int(math.log2(LANES))
    pw = []
    for i in range(n_steps):
        keep = lane >= (1 << i)
        pw.append((jnp.where(keep, jnp.broadcast_to(sr_ref[0, :, i:i + 1], (p, LANES)), 0.0),
                   jnp.where(keep, jnp.broadcast_to(si_ref[0, :, i:i + 1], (p, LANES)), 0.0)))
    h_re, h_im = [], []
    for b in range(nb):
        hr = hloc[:p, b * LANES:(b + 1) * LANES]
        hi = hloc[p:, b * LANES:(b + 1) * LANES]
        for i in range(n_steps):
            rs, js = pltpu.roll(hr, 1 << i, 1), pltpu.roll(hi, 1 << i, 1)
            ar, ai = pw[i]
            hr, hi = hr + ar * rs - ai * js, hi + ar * js + ai * rs
        h_re.append(jnp.where(lane >= 1, pltpu.roll(hr, 1, 1), 0.0))
        h_im.append(jnp.where(lane >= 1, pltpu.roll(hi, 1, 1), 0.0))
    h = jnp.concatenate([jnp.concatenate(h_re, axis=1), jnp.concatenate(h_im, axis=1)], axis=0).astype(BF16)
    y = (jnp.dot(toep_ref[0], x, preferred_element_type=F32)
         + jnp.dot(wout_ref[0], h, preferred_element_type=F32)
         + d_ref[0] * x.astype(F32))
    o_ref[...] = jax.nn.gelu(y).astype(o_ref.dtype).reshape(o_ref.shape)


def _s5_mix(xg, toep, wst, wout, sc_re, sc_im, dcol, *, nb):
    g, tc, cols = xg.shape
    assert cols == nb * LANES, "one batch's chunks must fill exactly one 128-lane block"
    blk = lambda a: pl.BlockSpec((1,) + a.shape[1:], lambda i: (i, 0, 0))
    return pl.pallas_call(
        functools.partial(_s5_mix_kernel, nb=nb),
        grid=(g,),
        in_specs=[blk(a) for a in (xg, toep, wst, wout, sc_re, sc_im, dcol)],
        out_specs=pl.BlockSpec((S5_CHUNK, S5_GROUP_CH, cols), lambda i: (0, i, 0)),
        out_shape=jax.ShapeDtypeStruct((S5_CHUNK, g * S5_GROUP_CH, cols), BF16),
        compiler_params=_cparams(("parallel",)),
        name="s5_mix",
    )(xg, toep, wst, wout, sc_re, sc_im, dcol)


def _glu_kernel(y_ref, w_ref, o_ref, *, col_chunk):
    y = y_ref[0].T
    half = o_ref.shape[-1]
    for c in range(half // col_chunk):
        a = jnp.dot(y, w_ref[:, c * col_chunk:(c + 1) * col_chunk], preferred_element_type=F32)
        g = jnp.dot(y, w_ref[:, half + c * col_chunk:half + (c + 1) * col_chunk], preferred_element_type=F32)
        o_ref[:, :, c * col_chunk:(c + 1) * col_chunk] = (
            (a * jax.nn.sigmoid(g)).astype(o_ref.dtype).reshape(o_ref.shape[:2] + (col_chunk,)))


def _glu(yt, w, *, nb, col_chunk):
    t, k, cols = yt.shape
    nc = cols // nb
    half = w.shape[1] // 2
    return pl.pallas_call(
        functools.partial(_glu_kernel, col_chunk=col_chunk),
        grid=(t,),
        in_specs=[pl.BlockSpec((1, k, cols), lambda j: (j, 0, 0)),
                  pl.BlockSpec(w.shape, lambda j: (0, 0))],
        out_specs=pl.BlockSpec((nb, nc, half), lambda j: (0, 0, j)),
        out_shape=jax.ShapeDtypeStruct((nb, nc, t * half), BF16),
        compiler_params=_cparams(("parallel",)),
        name="glu",
    )(yt, w)


def _merge_kernel(x_ref, mix_ref, xq_ref, gate_ref, k_ref, v_ref, qg_ref, w_ref, o_ref, cat_ref, *, phased):
    gate = gate_ref[0].astype(F32)
    sg = gate * jax.nn.sigmoid(gate)
    if phased:
        nc = gate.shape[0] // S5_CHUNK
        for s in range(S5_CHUNK):
            rows = slice(s * nc, (s + 1) * nc)
            mix = mix_ref[0, :, s * PRIMARY_WIDTH:(s + 1) * PRIMARY_WIDTH].astype(F32)
            cat_ref[rows, :PRIMARY_WIDTH] = (mix * sg[rows, :PRIMARY_WIDTH]).astype(BF16)
    else:
        cat_ref[:, :PRIMARY_WIDTH] = (mix_ref[0].astype(F32) * sg[:, :PRIMARY_WIDTH]).astype(BF16)
    scale = X_HEAD_DIM ** -0.5
    for h in range(X_HEADS):
        sl = slice(h * X_HEAD_DIM, (h + 1) * X_HEAD_DIM)
        q = _rms(xq_ref[0, :, sl].astype(F32), qg_ref[...]).astype(BF16)
        s = lax.dot_general(q, k_ref[0, :, sl], (((1,), (1,)), ((), ())), preferred_element_type=F32) * scale
        p = jnp.exp(s - jnp.max(s, axis=-1, keepdims=True))
        p = (p / jnp.sum(p, axis=-1, keepdims=True)).astype(BF16)
        mo = jnp.dot(p, v_ref[0, :, sl], preferred_element_type=F32)
        osl = slice(PRIMARY_WIDTH + h * X_HEAD_DIM, PRIMARY_WIDTH + (h + 1) * X_HEAD_DIM)
        cat_ref[:, osl] = (mo * sg[:, osl]).astype(BF16)
    delta = jnp.dot(cat_ref[...], w_ref[...], preferred_element_type=F32)
    o_ref[0] = x_ref[0] + (_from_phase_order(delta) if phased else delta)


def _merge(x, mix, proj, xq_blk, gate_blk, mk, mv, xq_norm, w_out, *, tm, phased=False):
    b, l, d = x.shape
    m = mk.shape[1]
    mix_spec = (pl.BlockSpec((1, tm // S5_CHUNK, S5_CHUNK * PRIMARY_WIDTH), lambda i, j: (i, j, 0)) if phased
                else pl.BlockSpec((1, tm, PRIMARY_WIDTH), lambda i, j: (i, j, 0)))
    return pl.pallas_call(
        functools.partial(_merge_kernel, phased=phased),
        grid=(b, l // tm),
        in_specs=[pl.BlockSpec((1, tm, d), lambda i, j: (i, j, 0)),
                  mix_spec,
                  pl.BlockSpec((1, tm, XQ_WIDTH), lambda i, j: (i, j, xq_blk)),
                  pl.BlockSpec((1, tm, BRANCH_WIDTH), lambda i, j: (i, j, gate_blk)),
                  pl.BlockSpec((1, m, XQ_WIDTH), lambda i, j: (i, 0, 0)),
                  pl.BlockSpec((1, m, XQ_WIDTH), lambda i, j: (i, 0, 0)),
                  pl.BlockSpec((1, X_HEAD_DIM), lambda i, j: (0, 0)),
                  pl.BlockSpec((BRANCH_WIDTH, d), lambda i, j: (0, 0))],
        out_specs=pl.BlockSpec((1, tm, d), lambda i, j: (i, j, 0)),
        out_shape=jax.ShapeDtypeStruct((b, l, d), F32),
        scratch_shapes=[pltpu.VMEM((tm, BRANCH_WIDTH), BF16)],
        compiler_params=_cparams(("parallel", "parallel")),
        name="merge",
    )(x, mix, proj, proj, mk, mv, xq_norm.reshape(1, X_HEAD_DIM), w_out)


def _mla_qkv_kernel(cq_ref, ckv_ref, kr_ref, posr_ref, invfc_ref, gq_ref, gkv_ref, gqn_ref,
                    gkn_ref, gqr_ref, gkr_ref, wqt_ref, wk_ref, wvt_ref, qt_ref, kn_ref, krope_ref, vt_ref):
    half = MLA_ROPE // 2
    tm = cq_ref.shape[1]
    qscale = (MLA_NOPE + MLA_ROPE) ** -0.5 * math.log2(math.e)

    cq = _rms(cq_ref[0].astype(F32), gq_ref[...])
    ckv = _rms(ckv_ref[0].astype(F32), gkv_ref[...])
    cq_t = cq.T.astype(BF16)
    ckv_t = ckv.T.astype(BF16)
    q_t = jnp.dot(wqt_ref[...], cq_t, preferred_element_type=F32)
    v_t = jnp.dot(wvt_ref[...], ckv_t, preferred_element_type=F32)
    k_n = jnp.dot(ckv.astype(BF16), wk_ref[...], preferred_element_type=F32)

    ang_t = invfc_ref[...] * posr_ref[0].astype(F32)
    cos_t, sin_t = jnp.cos(ang_t), jnp.sin(ang_t)
    g_nope = jnp.broadcast_to(gqn_ref[...], (MLA_NOPE, tm)) * qscale
    g_r1 = jnp.broadcast_to(gqr_ref[:half, :], (half, tm)) * qscale
    g_r2 = jnp.broadcast_to(gqr_ref[half:, :], (half, tm)) * qscale
    for h in range(MLA_HEADS):
        q = q_t[h * MLA_QK_PAD:(h + 1) * MLA_QK_PAD]
        nope = q[:MLA_NOPE]
        r = lax.rsqrt(jnp.mean(nope * nope, axis=0, keepdims=True) + EPS)
        qt_ref[0, h, :MLA_NOPE, :] = (nope * r * g_nope).astype(BF16)
        x1, x2 = q[MLA_NOPE:MLA_NOPE + half], q[MLA_NOPE + half:MLA_NOPE + MLA_ROPE]
        ss = jnp.sum(x1 * x1, axis=0, keepdims=True) + jnp.sum(x2 * x2, axis=0, keepdims=True)
        r = lax.rsqrt(ss * (1.0 / MLA_ROPE) + EPS)
        x1, x2 = x1 * r * g_r1, x2 * r * g_r2
        qt_ref[0, h, MLA_NOPE:MLA_NOPE + half, :] = (x1 * cos_t - x2 * sin_t).astype(BF16)
        qt_ref[0, h, MLA_NOPE + half:MLA_NOPE + MLA_ROPE, :] = (x1 * sin_t + x2 * cos_t).astype(BF16)
        qt_ref[0, h, MLA_NOPE + MLA_ROPE:, :] = jnp.zeros((MLA_QK_PAD - MLA_NOPE - MLA_ROPE, tm), BF16)
        kn_ref[0, h] = _rms(k_n[:, h * MLA_NOPE:(h + 1) * MLA_NOPE], gkn_ref[...]).astype(BF16)
        vt_ref[0, h] = v_t[h * MLA_V:(h + 1) * MLA_V].astype(BF16)

    kr_t = kr_ref[0].astype(F32).T
    x1, x2 = kr_t[:half], kr_t[half:MLA_ROPE]
    ss = jnp.sum(x1 * x1, axis=0, keepdims=True) + jnp.sum(x2 * x2, axis=0, keepdims=True)
    r = lax.rsqrt(ss * (1.0 / MLA_ROPE) + EPS)
    x1, x2 = x1 * r * gkr_ref[:half, :], x2 * r * gkr_ref[half:, :]
    rot = jnp.concatenate([x1 * cos_t - x2 * sin_t, x1 * sin_t + x2 * cos_t,
                           jnp.zeros((LANES - MLA_ROPE, tm), F32)], axis=0)
    krope_ref[0] = rot.T.astype(BF16)


def _mla_qkv(proj, cq_blk, ckv_blk, kr_blk, positions, gains, wqt, wk, wvt, *, tm):
    b, l, _ = proj.shape
    hh = MLA_HEADS
    half = MLA_ROPE // 2
    inv_freq = ROPE_THETA ** (-jnp.arange(half, dtype=F32) / half)
    const = lambda a: pl.BlockSpec(a.shape, lambda i, j: (0,) * a.ndim)
    gq, gkv, gqn, gkn, gqr, gkr = gains
    consts = [inv_freq.reshape(half, 1), gq.reshape(1, -1), gkv.reshape(1, -1), gqn.reshape(-1, 1),
              gkn.reshape(1, -1), gqr.reshape(-1, 1), gkr.reshape(-1, 1), wqt, wk, wvt]
    return pl.pallas_call(
        _mla_qkv_kernel,
        grid=(b, l // tm),
        in_specs=[pl.BlockSpec((1, tm, MLA_Q_LORA), lambda i, j: (i, j, cq_blk)),
                  pl.BlockSpec((1, tm, MLA_KV_LORA), lambda i, j: (i, j, ckv_blk)),
                  pl.BlockSpec((1, tm, LANES), lambda i, j: (i, j, kr_blk)),
                  pl.BlockSpec((1, 1, tm), lambda i, j: (i, 0, j))] + [const(a) for a in consts],
        out_specs=[pl.BlockSpec((1, hh, MLA_QK_PAD, tm), lambda i, j: (i, 0, 0, j)),
                   pl.BlockSpec((1, hh, tm, MLA_NOPE), lambda i, j: (i, 0, j, 0)),
                   pl.BlockSpec((1, tm, LANES), lambda i, j: (i, j, 0)),
                   pl.BlockSpec((1, hh, MLA_V, tm), lambda i, j: (i, 0, 0, j))],
        out_shape=[jax.ShapeDtypeStruct((b, hh, MLA_QK_PAD, l), BF16),
                   jax.ShapeDtypeStruct((b, hh, l, MLA_NOPE), BF16),
                   jax.ShapeDtypeStruct((b, l, LANES), BF16),
                   jax.ShapeDtypeStruct((b, hh, MLA_V, l), BF16)],
        compiler_params=_cparams(("parallel", "parallel")),
        name="mla_qkv",
    )(proj, proj, proj, positions.reshape(b, 1, l), *consts)


def _flash_kernel(qt_ref, kn_ref, kr_ref, vt_ref, o_ref, m_ref, l_ref, acc_ref, *, tq, hp):
    qi = pl.program_id(2)
    m_ref[...] = jnp.full(m_ref.shape, -jnp.inf, F32)
    l_ref[...] = jnp.zeros(l_ref.shape, F32)
    acc_ref[...] = jnp.zeros(acc_ref.shape, F32)

    half = tq // 2
    lower = (lax.broadcasted_iota(jnp.int32, (half, half), 0)
             <= lax.broadcasted_iota(jnp.int32, (half, half), 1))

    def blocks(j, parts, diagonal):
        base = pl.multiple_of(j * tq, tq)
        work = []
        for k0, nk, q0, nq in parts:
            rows = pl.ds(base + k0, nk)
            k_rope = kr_ref[0, rows, :]
            for h in range(hp):
                k = jnp.concatenate([kn_ref[0, h, rows, :], k_rope], axis=-1)
                s = jnp.dot(k, qt_ref[0, h, :, q0:q0 + nq], preferred_element_type=F32)
                work.append((h, rows, slice(q0, q0 + nq), s))
        for h, rows, cols, s in work:
            if diagonal:
                square = jnp.where(lower, s[:, :half], jnp.finfo(F32).min)
                s = square if s.shape[1] == half else jnp.concatenate([square, s[:, half:]], axis=1)
            m = m_ref[h, :, cols]
            m_new = jnp.maximum(m, jnp.max(s, axis=0, keepdims=True))
            alpha = jnp.exp2(m - m_new)
            p = jnp.exp2(s - m_new)
            l_ref[h, :, cols] = alpha * l_ref[h, :, cols] + jnp.sum(p, axis=0, keepdims=True)
            acc_ref[h, :, cols] = alpha * acc_ref[h, :, cols] + jnp.dot(
                vt_ref[0, h, :, rows], p.astype(BF16), preferred_element_type=F32)
            m_ref[h, :, cols] = m_new

    def body(j, carry):
        blocks(j, [(0, tq, 0, tq)], False)
        return carry

    lax.fori_loop(0, qi, body, 0)
    blocks(qi, [(0, half, 0, tq), (half, half, half, half)], True)
    for h in range(hp):
        o_ref[0, :, h * MLA_V:(h + 1) * MLA_V] = (acc_ref[h] / l_ref[h]).T.astype(o_ref.dtype)


def _flash(qt, kn, kr, vt, *, tq, hp):
    b, hh, _, l = qt.shape
    return pl.pallas_call(
        functools.partial(_flash_kernel, tq=tq, hp=hp),
        grid=(b, hh // hp, l // tq),
        in_specs=[pl.BlockSpec((1, hp, MLA_QK_PAD, tq), lambda i, h, j: (i, h, 0, j)),
                  pl.BlockSpec((1, hp, l, MLA_NOPE), lambda i, h, j: (i, h, 0, 0)),
                  pl.BlockSpec((1, l, LANES), lambda i, h, j: (i, 0, 0)),
                  pl.BlockSpec((1, hp, MLA_V, l), lambda i, h, j: (i, h, 0, 0))],
        out_specs=pl.BlockSpec((1, tq, hp * MLA_V), lambda i, h, j: (i, j, h)),
        out_shape=jax.ShapeDtypeStruct((b, l, hh * MLA_V), BF16),
        scratch_shapes=[pltpu.VMEM((hp, 1, tq), F32), pltpu.VMEM((hp, 1, tq), F32),
                        pltpu.VMEM((hp, MLA_V, tq), F32)],
        compiler_params=_cparams(("parallel", "parallel", "parallel")),
        name="flash",
    )(qt, kn, kr, vt)


def _s5_layer(x, mem, ln, w_in, lam_re, lam_im, log_step, b_re, b_im, c_re, c_im, d, w_glu,
              w_out, mem_norm, w_mem_kv, xq_norm, xk_norm):
    b, l, dm = x.shape
    w_gx = jnp.concatenate([w_in[:, PRIMARY_WIDTH + XQ_WIDTH:], w_in[:, PRIMARY_WIDTH:PRIMARY_WIDTH + XQ_WIDTH]],
                           axis=1).astype(BF16)
    tm = 512
    proj, xn = _s5_in_proj(x, ln, w_gx, tm=tm, col_chunk=512)
    xg = _s5_ut(xn, w_in[:, :PRIMARY_WIDTH].T.astype(BF16))
    pw_re, pw_im = _s5_pow(lam_re, lam_im, log_step)
    toep, wout, wst = _s5_asm(pw_re.transpose(1, 0, 2), pw_im.transpose(1, 0, 2), c_re, c_im, b_re, b_im)
    n_scan = int(math.log2(LANES))
    first = S5_EXPONENTS.index(S5_CHUNK)
    col = lambda pw: jnp.pad(pw[first:first + n_scan].transpose(1, 2, 0), ((0, 0), (0, 0), (0, 8 - n_scan)))
    dcol = jnp.tile(d.reshape(S5_GROUPS, 1, S5_GROUP_CH), (1, S5_CHUNK, 1)).reshape(S5_GROUPS, -1, 1)
    yt = _s5_mix(xg, toep, wst, wout, col(pw_re), col(pw_im), dcol, nb=b)
    y = _glu(yt, w_glu.astype(BF16), nb=b, col_chunk=256)
    mk, mv = _mem_kv(mem, mem_norm, w_mem_kv, xk_norm)
    return _merge(x, y, proj, BRANCH_WIDTH // XQ_WIDTH, 0, mk, mv, xq_norm, w_out.astype(BF16), tm=tm, phased=True)


def _mla_layer(x, mem, positions, ln, w_in, q_lora_norm, kv_lora_norm, w_uq, w_ukv, q_nope_norm, k_nope_norm,
               q_rope_norm, k_rope_norm, w_out, mem_norm, w_mem_kv, xq_norm, xk_norm):
    b, l, dm = x.shape
    o1 = MLA_Q_LORA
    o2 = o1 + MLA_KV_LORA
    o3 = o2 + MLA_ROPE
    o4 = o3 + XQ_WIDTH
    w_perm = jnp.concatenate([w_in[:, o4:], w_in[:, :o1], w_in[:, o3:o4], w_in[:, o1:o2], w_in[:, o2:o3],
                              jnp.zeros((dm, 2 * LANES - MLA_ROPE), w_in.dtype)], axis=1).astype(BF16)
    proj = _norm_matmul(x.reshape(b * l, dm), ln, w_perm, tm=512, col_chunk=512, name="mla_in_proj")
    proj = proj.reshape(b, l, -1)
    gate_blk = 0
    cq_blk = BRANCH_WIDTH // MLA_Q_LORA
    xq_blk = (BRANCH_WIDTH + MLA_Q_LORA) // XQ_WIDTH
    ckv_blk = (BRANCH_WIDTH + MLA_Q_LORA + XQ_WIDTH) // MLA_KV_LORA
    kr_blk = (BRANCH_WIDTH + MLA_Q_LORA + XQ_WIDTH + MLA_KV_LORA) // LANES
    wq = w_uq.reshape(MLA_Q_LORA, MLA_HEADS, MLA_NOPE + MLA_ROPE)
    wq = jnp.pad(wq, ((0, 0), (0, 0), (0, MLA_QK_PAD - MLA_NOPE - MLA_ROPE)))
    wqt = wq.reshape(MLA_Q_LORA, MLA_HEADS * MLA_QK_PAD).T.astype(BF16)
    wkv = w_ukv.reshape(MLA_KV_LORA, MLA_HEADS, MLA_NOPE + MLA_V)
    wk = wkv[:, :, :MLA_NOPE].reshape(MLA_KV_LORA, MLA_HEADS * MLA_NOPE).astype(BF16)
    wvt = wkv[:, :, MLA_NOPE:].reshape(MLA_KV_LORA, MLA_HEADS * MLA_V).T.astype(BF16)
    qt, kn, kr, vt = _mla_qkv(proj, cq_blk, ckv_blk, kr_blk, positions,
                              (q_lora_norm, kv_lora_norm, q_nope_norm, k_nope_norm, q_rope_norm, k_rope_norm),
                              wqt, wk, wvt, tm=256)
    attn = _flash(qt, kn, kr, vt, tq=512, hp=4)
    mk, mv = _mem_kv(mem, mem_norm, w_mem_kv, xk_norm)
    return _merge(x, attn, proj, xq_blk, gate_blk, mk, mv, xq_norm, w_out.astype(BF16), tm=512)


def kernel(x, mem, positions, ln_gain, w_out, mem_norm, w_mem_kv, xq_norm, xk_norm,
           s5_w_in, s5_lambda_re, s5_lambda_im, s5_log_step, s5_b_re, s5_b_im, s5_c_re, s5_c_im,
           s5_d, s5_w_glu, mla_w_in, mla_q_lora_norm, mla_kv_lora_norm, mla_w_uq, mla_w_ukv,
           mla_q_nope_norm, mla_k_nope_norm, mla_q_rope_norm, mla_k_rope_norm):
    depth = ln_gain.shape[0]
    for i in range(depth):
        j = i // 2
        if i % 2 == 0:
            x = _s5_layer(x, mem, ln_gain[i], s5_w_in[j], s5_lambda_re[j], s5_lambda_im[j], s5_log_step[j],
                          s5_b_re[j], s5_b_im[j], s5_c_re[j], s5_c_im[j], s5_d[j], s5_w_glu[j],
                          w_out[i], mem_norm[i], w_mem_kv[i], xq_norm[i], xk_norm[i])
        else:
            x = _mla_layer(x, mem, positions, ln_gain[i], mla_w_in[j], mla_q_lora_norm[j], mla_kv_lora_norm[j],
                           mla_w_uq[j], mla_w_ukv[j], mla_q_nope_norm[j], mla_k_nope_norm[j],
                           mla_q_rope_norm[j], mla_k_rope_norm[j],
                           w_out[i], mem_norm[i], w_mem_kv[i], xq_norm[i], xk_norm[i])
    return x
```

```python
import functools
import math

import jax
import jax.numpy as jnp
from jax import lax
from jax.experimental import pallas as pl
from jax.experimental.pallas import tpu as pltpu

D_MODEL = 1024
BRANCH_WIDTH = 2 * D_MODEL
XQ_WIDTH = BRANCH_WIDTH // 4
PRIMARY_WIDTH = BRANCH_WIDTH - XQ_WIDTH
X_HEADS = 4
X_HEAD_DIM = XQ_WIDTH // X_HEADS
S5_GROUP_CH = 16
S5_GROUPS = PRIMARY_WIDTH // S5_GROUP_CH
S5_STATE = 64
MLA_NOPE = 128
MLA_ROPE = 64
MLA_V = 128
MLA_HEADS = PRIMARY_WIDTH // MLA_V
MLA_Q_LORA = D_MODEL // 2
MLA_KV_LORA = D_MODEL // 4
ROPE_THETA = 10000.0
EPS = 1e-6

LANES = 128
MLA_QK_PAD = 2 * LANES
S5_CHUNK = 2 * LANES // S5_GROUP_CH
S5_EXPONENTS = list(range(S5_CHUNK + 1)) + [S5_CHUNK * 2 ** i for i in range(1, int(math.log2(LANES)))]
VMEM_LIMIT = 56 * 1024 * 1024

F32 = jnp.float32
BF16 = jnp.bfloat16


def _cparams(sem):
    return pltpu.CompilerParams(dimension_semantics=sem, vmem_limit_bytes=VMEM_LIMIT)


def _rms(x, g):
    return x * lax.rsqrt(jnp.mean(x * x, axis=-1, keepdims=True) + EPS) * g


def _norm_matmul_kernel(x_ref, g_ref, w_ref, o_ref, *, col_chunk):
    xn = _rms(x_ref[...].astype(F32), g_ref[...]).astype(BF16)
    for c in range(o_ref.shape[1] // col_chunk):
        sl = slice(c * col_chunk, (c + 1) * col_chunk)
        o_ref[:, sl] = jnp.dot(xn, w_ref[:, sl], preferred_element_type=F32).astype(o_ref.dtype)


def _norm_matmul(x, g, w, *, tm, col_chunk, name):
    n, d = x.shape
    wout = w.shape[1]
    return pl.pallas_call(
        functools.partial(_norm_matmul_kernel, col_chunk=col_chunk),
        grid=(n // tm,),
        in_specs=[pl.BlockSpec((tm, d), lambda i: (i, 0)),
                  pl.BlockSpec((1, d), lambda i: (0, 0)),
                  pl.BlockSpec((d, wout), lambda i: (0, 0))],
        out_specs=pl.BlockSpec((tm, wout), lambda i: (i, 0)),
        out_shape=jax.ShapeDtypeStruct((n, wout), BF16),
        compiler_params=_cparams(("parallel",)),
        name=name,
    )(x, g.reshape(1, d), w)


def _to_phase_order(a):
    n, d = a.shape
    return jnp.swapaxes(a.reshape(n // S5_CHUNK, S5_CHUNK, d), 0, 1).reshape(n, d)


def _from_phase_order(a):
    n, d = a.shape
    return jnp.swapaxes(a.reshape(S5_CHUNK, n // S5_CHUNK, d), 0, 1).reshape(n, d)


def _s5_in_proj_kernel(x_ref, g_ref, w_ref, o_ref, xn_ref, *, col_chunk):
    tm, d = x_ref.shape[1:]
    nc = tm // S5_CHUNK
    xn = _rms(_to_phase_order(x_ref[0]), g_ref[...]).astype(BF16)
    for s in range(S5_CHUNK):
        xn_ref[0, :, s * d:(s + 1) * d] = xn[s * nc:(s + 1) * nc]
    for c in range(o_ref.shape[2] // col_chunk):
        sl = slice(c * col_chunk, (c + 1) * col_chunk)
        o_ref[0, :, sl] = jnp.dot(xn, w_ref[:, sl], preferred_element_type=F32).astype(o_ref.dtype)


def _s5_in_proj(x, g, w, *, tm, col_chunk):
    b, l, d = x.shape
    wout = w.shape[1]
    nc = tm // S5_CHUNK
    return pl.pallas_call(
        functools.partial(_s5_in_proj_kernel, col_chunk=col_chunk),
        grid=(b, l // tm),
        in_specs=[pl.BlockSpec((1, tm, d), lambda i, j: (i, j, 0)),
                  pl.BlockSpec((1, d), lambda i, j: (0, 0)),
                  pl.BlockSpec((d, wout), lambda i, j: (0, 0))],
        out_specs=[pl.BlockSpec((1, tm, wout), lambda i, j: (i, j, 0)),
                   pl.BlockSpec((1, nc, S5_CHUNK * d), lambda i, j: (i, j, 0))],
        out_shape=[jax.ShapeDtypeStruct((b, l, wout), BF16),
                   jax.ShapeDtypeStruct((b, l // S5_CHUNK, S5_CHUNK * d), BF16)],
        compiler_params=_cparams(("parallel", "parallel")),
        name="s5_in_proj",
    )(x, g.reshape(1, d), w)


def _mem_kv_kernel(m_ref, g_ref, w_ref, kg_ref, k_ref, v_ref):
    mn = _rms(m_ref[0], g_ref[...]).astype(BF16)
    kv = jnp.dot(mn, w_ref[...], preferred_element_type=F32)
    for h in range(X_HEADS):
        sl = slice(h * X_HEAD_DIM, (h + 1) * X_HEAD_DIM)
        k_ref[0, :, sl] = _rms(kv[:, sl], kg_ref[...]).astype(BF16)
    v_ref[0] = kv[:, XQ_WIDTH:].astype(BF16)


def _mem_kv(mem, mem_norm, w_mem_kv, xk_norm):
    b, m, d = mem.shape
    out = jax.ShapeDtypeStruct((b, m, XQ_WIDTH), BF16)
    return pl.pallas_call(
        _mem_kv_kernel,
        grid=(b,),
        in_specs=[pl.BlockSpec((1, m, d), lambda i: (i, 0, 0)),
                  pl.BlockSpec((1, d), lambda i: (0, 0)),
                  pl.BlockSpec((d, 2 * XQ_WIDTH), lambda i: (0, 0)),
                  pl.BlockSpec((1, X_HEAD_DIM), lambda i: (0, 0))],
        out_specs=[pl.BlockSpec((1, m, XQ_WIDTH), lambda i: (i, 0, 0)),
                   pl.BlockSpec((1, m, XQ_WIDTH), lambda i: (i, 0, 0))],
        out_shape=[out, out],
        compiler_params=_cparams(("parallel",)),
        name="mem_kv",
    )(mem, mem_norm.reshape(1, d), w_mem_kv.astype(BF16), xk_norm.reshape(1, X_HEAD_DIM))


def _s5_pow_kernel(lr_ref, li_ref, ls_ref, pr_ref, pi_ref):
    lr, li = lr_ref[...], li_ref[...]
    step = jnp.exp(ls_ref[...])
    zr, zi = lr * step, li * step
    for n, e in enumerate(S5_EXPONENTS):
        mag = jnp.exp(zr * e)
        pr_ref[n] = mag * jnp.cos(zi * e)
        pi_ref[n] = mag * jnp.sin(zi * e)
    ar, ai = pr_ref[1], pi_ref[1]
    den = lr * lr + li * li
    pr_ref[len(S5_EXPONENTS)] = ((ar - 1.0) * lr + ai * li) / den
    pi_ref[len(S5_EXPONENTS)] = (ai * lr - (ar - 1.0) * li) / den


def _s5_pow(lam_re, lam_im, log_step):
    g, p = lam_re.shape
    out = jax.ShapeDtypeStruct((len(S5_EXPONENTS) + 1, g, p), F32)
    return pl.pallas_call(_s5_pow_kernel, out_shape=[out, out], name="s5_pow")(
        lam_re, lam_im, log_step.reshape(g, 1))


def _s5_asm_kernel(pr_ref, pi_ref, cr_ref, ci_ref, btr_ref, bti_ref, br_ref, bi_ref,
                   toep_ref, wout_ref, wst_ref):
    def group(i, carry):
        _s5_asm_group(i, pr_ref, pi_ref, cr_ref, ci_ref, btr_ref, bti_ref, br_ref, bi_ref,
                      toep_ref, wout_ref, wst_ref)
        return carry

    lax.fori_loop(0, pr_ref.shape[0], group, 0)


def _dot_3pass(a, b):
    a_hi, b_hi = a.astype(BF16), b.astype(BF16)
    a_lo = (a - a_hi.astype(F32)).astype(BF16)
    b_lo = (b - b_hi.astype(F32)).astype(BF16)
    dot = functools.partial(jnp.dot, preferred_element_type=F32)
    return dot(a_hi, b_hi) + (dot(a_hi, b_lo) + dot(a_lo, b_hi))


def _s5_asm_group(i, pr_ref, pi_ref, cr_ref, ci_ref, btr_ref, bti_ref, br_ref, bi_ref,
                  toep_ref, wout_ref, wst_ref):
    t = S5_CHUNK
    pr, pi = pr_ref[i], pi_ref[i]
    cr, ci = cr_ref[i], ci_ref[i]
    btr, bti = btr_ref[i], bti_ref[i]
    mr, mi = pr[len(S5_EXPONENTS):], pi[len(S5_EXPONENTS):]
    amr = pr[:t] * mr - pi[:t] * mi
    ami = pr[:t] * mi + pi[:t] * mr
    l_re, l_im, w_re, w_im, o_re, o_im = [], [], [], [], [], []
    for k in range(t):
        ar, ai = amr[k:k + 1], ami[k:k + 1]
        l_re.append(cr * ar - ci * ai)
        l_im.append(cr * ai + ci * ar)
        ar, ai = amr[t - 1 - k:t - k], ami[t - 1 - k:t - k]
        w_re.append(btr * ar - bti * ai)
        w_im.append(btr * ai + bti * ar)
        ar, ai = pr[k + 1:k + 2], pi[k + 1:k + 2]
        o_re.append(cr * ar - ci * ai)
        o_im.append(-(cr * ai + ci * ar))
    cat = lambda parts: jnp.concatenate(parts, axis=0)
    kt = _dot_3pass(cat(l_re), br_ref[i]) - _dot_3pass(cat(l_im), bi_ref[i])
    n = kt.shape[0]
    blk = lax.shift_right_logical(lax.broadcasted_iota(jnp.int32, kt.shape, 1), int(math.log2(S5_GROUP_CH)))
    toep = jnp.where(blk == 0, kt, 0.0)
    for s in range(1, t):
        shifted = jnp.concatenate([jnp.zeros((s * S5_GROUP_CH, n), F32), kt[:n - s * S5_GROUP_CH]], axis=0)
        toep = jnp.where(blk == s, shifted, toep)
    toep_ref[i] = toep.astype(BF16)
    wout_ref[i] = jnp.concatenate([cat(o_re), cat(o_im)], axis=1).astype(BF16)
    wst_ref[i] = jnp.concatenate([cat(w_re), cat(w_im)], axis=1).T.astype(BF16)


def _s5_asm(pw_re, pw_im, c_re, c_im, b_re, b_im):
    g, c, p = c_re.shape
    tc = S5_CHUNK * c
    gs = 8
    blk = lambda a: pl.BlockSpec((gs,) + a.shape[1:], lambda i: (i, 0, 0))
    bt_re, bt_im = b_re.transpose(0, 2, 1), b_im.transpose(0, 2, 1)
    tile = lambda a: jnp.tile(a, (1, 1, S5_CHUNK))
    args = (pw_re, pw_im, c_re, c_im, bt_re, bt_im, tile(b_re), tile(b_im))
    return pl.pallas_call(
        _s5_asm_kernel,
        grid=(g // gs,),
        in_specs=[blk(a) for a in args],
        out_specs=[pl.BlockSpec((gs, tc, tc), lambda i: (i, 0, 0)),
                   pl.BlockSpec((gs, tc, 2 * p), lambda i: (i, 0, 0)),
                   pl.BlockSpec((gs, 2 * p, tc), lambda i: (i, 0, 0))],
        out_shape=[jax.ShapeDtypeStruct((g, tc, tc), BF16), jax.ShapeDtypeStruct((g, tc, 2 * p), BF16),
                   jax.ShapeDtypeStruct((g, 2 * p, tc), BF16)],
        compiler_params=_cparams(("parallel",)),
        name="s5_asm",
    )(*args)


def _s5_ut_kernel(xn_ref, w_ref, o_ref):
    nb, nc, d = xn_ref.shape
    ut = lax.dot_general(w_ref[...], xn_ref[...].reshape(nb * nc, d), (((1,), (1,)), ((), ())),
                         preferred_element_type=F32)
    o_ref[...] = ut.astype(BF16).reshape(o_ref.shape)


def _s5_ut(xn, wut):
    b, nc, td = xn.shape
    d = td // S5_CHUNK
    return pl.pallas_call(
        _s5_ut_kernel,
        grid=(S5_CHUNK,),
        in_specs=[pl.BlockSpec((b, nc, d), lambda s: (0, 0, s)),
                  pl.BlockSpec(wut.shape, lambda s: (0, 0))],
        out_specs=pl.BlockSpec((S5_GROUPS, S5_GROUP_CH, b * nc), lambda s: (0, s, 0)),
        out_shape=jax.ShapeDtypeStruct((S5_GROUPS, S5_CHUNK * S5_GROUP_CH, b * nc), BF16),
        compiler_params=_cparams(("parallel",)),
        name="s5_ut",
    )(xn, wut)


def _s5_mix_kernel(x_ref, toep_ref, wst_ref, wout_ref, sr_ref, si_ref, d_ref, o_ref, *, nb):
    p = S5_STATE
    x = x_ref[0]
    hloc = jnp.dot(wst_ref[0], x, preferred_element_type=F32)
    lane = lax.broadcasted_iota(jnp.int32, (p, LANES), 1)
    n_steps = int(math.log2(LANES))
    pw = []
    for i in range(n_steps):
        keep = lane >= (1 << i)
        pw.append((jnp.where(keep, jnp.broadcast_to(sr_ref[0, :, i:i + 1], (p, LANES)), 0.0),
                   jnp.where(keep, jnp.broadcast_to(si_ref[0, :, i:i + 1], (p, LANES)), 0.0)))
    h_re, h_im = [], []
    for b in range(nb):
        hr = hloc[:p, b * LANES:(b + 1) * LANES]
        hi = hloc[p:, b * LANES:(b + 1) * LANES]
        for i in range(n_steps):
            rs, js = pltpu.roll(hr, 1 << i, 1), pltpu.roll(hi, 1 << i, 1)
            ar, ai = pw[i]
            hr, hi = hr + ar * rs - ai * js, hi + ar * js + ai * rs
        h_re.append(jnp.where(lane >= 1, pltpu.roll(hr, 1, 1), 0.0))
        h_im.append(jnp.where(lane >= 1, pltpu.roll(hi, 1, 1), 0.0))
    h = jnp.concatenate([jnp.concatenate(h_re, axis=1), jnp.concatenate(h_im, axis=1)], axis=0).astype(BF16)
    y = (jnp.dot(toep_ref[0], x, preferred_element_type=F32)
         + jnp.dot(wout_ref[0], h, preferred_element_type=F32)
         + d_ref[0] * x.astype(F32))
    o_ref[...] = jax.nn.gelu(y).astype(o_ref.dtype).reshape(o_ref.shape)


def _s5_mix(xg, toep, wst, wout, sc_re, sc_im, dcol, *, nb):
    g, tc, cols = xg.shape
    assert cols == nb * LANES, "one batch's chunks must fill exactly one 128-lane block"
    blk = lambda a: pl.BlockSpec((1,) + a.shape[1:], lambda i: (i, 0, 0))
    return pl.pallas_call(
        functools.partial(_s5_mix_kernel, nb=nb),
        grid=(g,),
        in_specs=[blk(a) for a in (xg, toep, wst, wout, sc_re, sc_im, dcol)],
        out_specs=pl.BlockSpec((S5_CHUNK, S5_GROUP_CH, cols), lambda i: (0, i, 0)),
        out_shape=jax.ShapeDtypeStruct((S5_CHUNK, g * S5_GROUP_CH, cols), BF16),
        compiler_params=_cparams(("parallel",)),
        name="s5_mix",
    )(xg, toep, wst, wout, sc_re, sc_im, dcol)


def _glu_kernel(y_ref, w_ref, o_ref, *, col_chunk):
    y = y_ref[0].T
    half = o_ref.shape[-1]
    for c in range(half // col_chunk):
        a = jnp.dot(y, w_ref[:, c * col_chunk:(c + 1) * col_chunk], preferred_element_type=F32)
        g = jnp.dot(y, w_ref[:, half + c * col_chunk:half + (c + 1) * col_chunk], preferred_element_type=F32)
        o_ref[:, :, c * col_chunk:(c + 1) * col_chunk] = (
            (a * jax.nn.sigmoid(g)).astype(o_ref.dtype).reshape(o_ref.shape[:2] + (col_chunk,)))


def _glu(yt, w, *, nb, col_chunk):
    t, k, cols = yt.shape
    nc = cols // nb
    half = w.shape[1] // 2
    return pl.pallas_call(
        functools.partial(_glu_kernel, col_chunk=col_chunk),
        grid=(t,),
        in_specs=[pl.BlockSpec((1, k, cols), lambda j: (j, 0, 0)),
                  pl.BlockSpec(w.shape, lambda j: (0, 0))],
        out_specs=pl.BlockSpec((nb, nc, half), lambda j: (0, 0, j)),
        out_shape=jax.ShapeDtypeStruct((nb, nc, t * half), BF16),
        compiler_params=_cparams(("parallel",)),
        name="glu",
    )(yt, w)


def _merge_kernel(x_ref, mix_ref, xq_ref, gate_ref, k_ref, v_ref, qg_ref, w_ref, o_ref, cat_ref, *, phased):
    gate = gate_ref[0]
    sg = gate * jax.nn.sigmoid(gate)
    if phased:
        nc = gate.shape[0] // S5_CHUNK
        for s in range(S5_CHUNK):
            rows = slice(s * nc, (s + 1) * nc)
            mix = mix_ref[0, :, s * PRIMARY_WIDTH:(s + 1) * PRIMARY_WIDTH]
            cat_ref[rows, :PRIMARY_WIDTH] = mix * sg[rows, :PRIMARY_WIDTH]
    else:
        cat_ref[:, :PRIMARY_WIDTH] = mix_ref[0] * sg[:, :PRIMARY_WIDTH]
    scale = X_HEAD_DIM ** -0.5
    for h in range(X_HEADS):
        sl = slice(h * X_HEAD_DIM, (h + 1) * X_HEAD_DIM)
        q = _rms(xq_ref[0, :, sl].astype(F32), qg_ref[...]).astype(BF16)
        s = lax.dot_general(q, k_ref[0, :, sl], (((1,), (1,)), ((), ())), preferred_element_type=F32) * scale
        p = jnp.exp(s - jnp.max(s, axis=-1, keepdims=True))
        p = (p / jnp.sum(p, axis=-1, keepdims=True)).astype(BF16)
        mo = jnp.dot(p, v_ref[0, :, sl], preferred_element_type=F32)
        osl = slice(PRIMARY_WIDTH + h * X_HEAD_DIM, PRIMARY_WIDTH + (h + 1) * X_HEAD_DIM)
        cat_ref[:, osl] = mo.astype(BF16) * sg[:, osl]
    delta = jnp.dot(cat_ref[...], w_ref[...], preferred_element_type=F32)
    o_ref[0] = x_ref[0] + (_from_phase_order(delta) if phased else delta)


def _merge(x, mix, proj, xq_blk, gate_blk, mk, mv, xq_norm, w_out, *, tm, phased=False):
    b, l, d = x.shape
    m = mk.shape[1]
    mix_spec = (pl.BlockSpec((1, tm // S5_CHUNK, S5_CHUNK * PRIMARY_WIDTH), lambda i, j: (i, j, 0)) if phased
                else pl.BlockSpec((1, tm, PRIMARY_WIDTH), lambda i, j: (i, j, 0)))
    return pl.pallas_call(
        functools.partial(_merge_kernel, phased=phased),
        grid=(b, l // tm),
        in_specs=[pl.BlockSpec((1, tm, d), lambda i, j: (i, j, 0)),
                  mix_spec,
                  pl.BlockSpec((1, tm, XQ_WIDTH), lambda i, j: (i, j, xq_blk)),
                  pl.BlockSpec((1, tm, BRANCH_WIDTH), lambda i, j: (i, j, gate_blk)),
                  pl.BlockSpec((1, m, XQ_WIDTH), lambda i, j: (i, 0, 0)),
                  pl.BlockSpec((1, m, XQ_WIDTH), lambda i, j: (i, 0, 0)),
                  pl.BlockSpec((1, X_HEAD_DIM), lambda i, j: (0, 0)),
                  pl.BlockSpec((BRANCH_WIDTH, d), lambda i, j: (0, 0))],
        out_specs=pl.BlockSpec((1, tm, d), lambda i, j: (i, j, 0)),
        out_shape=jax.ShapeDtypeStruct((b, l, d), F32),
        scratch_shapes=[pltpu.VMEM((tm, BRANCH_WIDTH), BF16)],
        compiler_params=_cparams(("parallel", "parallel")),
        name="merge",
    )(x, mix, proj, proj, mk, mv, xq_norm.reshape(1, X_HEAD_DIM), w_out)


def _mla_qkv_kernel(cq_ref, ckv_ref, kr_ref, posr_ref, invfc_ref, gq_ref, gkv_ref, gqn_ref,
                    gkn_ref, gqr_ref, gkr_ref, wqt_ref, wk_ref, wvt_ref, qt_ref, kn_ref, krope_ref, vt_ref):
    half = MLA_ROPE // 2
    tm = cq_ref.shape[1]
    qscale = (MLA_NOPE + MLA_ROPE) ** -0.5 * math.log2(math.e)

    cq = _rms(cq_ref[0].astype(F32), gq_ref[...])
    ckv = _rms(ckv_ref[0].astype(F32), gkv_ref[...])
    cq_t = cq.T.astype(BF16)
    ckv_t = ckv.T.astype(BF16)
    q_t = jnp.dot(wqt_ref[...], cq_t, preferred_element_type=F32)
    v_t = jnp.dot(wvt_ref[...], ckv_t, preferred_element_type=F32)
    k_n = jnp.dot(ckv.astype(BF16), wk_ref[...], preferred_element_type=F32)

    ang_t = invfc_ref[...] * posr_ref[0].astype(F32)
    cos_t, sin_t = jnp.cos(ang_t), jnp.sin(ang_t)
    g_nope = jnp.broadcast_to(gqn_ref[...], (MLA_NOPE, tm)) * qscale
    g_r1 = jnp.broadcast_to(gqr_ref[:half, :], (half, tm)) * qscale
    g_r2 = jnp.broadcast_to(gqr_ref[half:, :], (half, tm)) * qscale
    for h in range(MLA_HEADS):
        q = q_t[h * MLA_QK_PAD:(h + 1) * MLA_QK_PAD]
        nope = q[:MLA_NOPE]
        r = lax.rsqrt(jnp.mean(nope * nope, axis=0, keepdims=True) + EPS)
        qt_ref[0, h, :MLA_NOPE, :] = (nope * r * g_nope).astype(BF16)
        x1, x2 = q[MLA_NOPE:MLA_NOPE + half], q[MLA_NOPE + half:MLA_NOPE + MLA_ROPE]
        ss = jnp.sum(x1 * x1, axis=0, keepdims=True) + jnp.sum(x2 * x2, axis=0, keepdims=True)
        r = lax.rsqrt(ss * (1.0 / MLA_ROPE) + EPS)
        x1, x2 = x1 * r * g_r1, x2 * r * g_r2
        qt_ref[0, h, MLA_NOPE:MLA_NOPE + half, :] = (x1 * cos_t - x2 * sin_t).astype(BF16)
        qt_ref[0, h, MLA_NOPE + half:MLA_NOPE + MLA_ROPE, :] = (x1 * sin_t + x2 * cos_t).astype(BF16)
        qt_ref[0, h, MLA_NOPE + MLA_ROPE:, :] = jnp.zeros((MLA_QK_PAD - MLA_NOPE - MLA_ROPE, tm), BF16)
        kn_ref[0, h] = _rms(k_n[:, h * MLA_NOPE:(h + 1) * MLA_NOPE], gkn_ref[...]).astype(BF16)
        vt_ref[0, h] = v_t[h * MLA_V:(h + 1) * MLA_V].astype(BF16)

    kr_t = kr_ref[0].astype(F32).T
    x1, x2 = kr_t[:half], kr_t[half:MLA_ROPE]
    ss = jnp.sum(x1 * x1, axis=0, keepdims=True) + jnp.sum(x2 * x2, axis=0, keepdims=True)
    r = lax.rsqrt(ss * (1.0 / MLA_ROPE) + EPS)
    x1, x2 = x1 * r * gkr_ref[:half, :], x2 * r * gkr_ref[half:, :]
    rot = jnp.concatenate([x1 * cos_t - x2 * sin_t, x1 * sin_t + x2 * cos_t,
                           jnp.zeros((LANES - MLA_ROPE, tm), F32)], axis=0)
    krope_ref[0] = rot.T.astype(BF16)


def _mla_qkv(proj, cq_blk, ckv_blk, kr_blk, positions, gains, wqt, wk, wvt, *, tm):
    b, l, _ = proj.shape
    hh = MLA_HEADS
    half = MLA_ROPE // 2
    inv_freq = ROPE_THETA ** (-jnp.arange(half, dtype=F32) / half)
    const = lambda a: pl.BlockSpec(a.shape, lambda i, j: (0,) * a.ndim)
    gq, gkv, gqn, gkn, gqr, gkr = gains
    consts = [inv_freq.reshape(half, 1), gq.reshape(1, -1), gkv.reshape(1, -1), gqn.reshape(-1, 1),
              gkn.reshape(1, -1), gqr.reshape(-1, 1), gkr.reshape(-1, 1), wqt, wk, wvt]
    return pl.pallas_call(
        _mla_qkv_kernel,
        grid=(b, l // tm),
        in_specs=[pl.BlockSpec((1, tm, MLA_Q_LORA), lambda i, j: (i, j, cq_blk)),
                  pl.BlockSpec((1, tm, MLA_KV_LORA), lambda i, j: (i, j, ckv_blk)),
                  pl.BlockSpec((1, tm, LANES), lambda i, j: (i, j, kr_blk)),
                  pl.BlockSpec((1, 1, tm), lambda i, j: (i, 0, j))] + [const(a) for a in consts],
        out_specs=[pl.BlockSpec((1, hh, MLA_QK_PAD, tm), lambda i, j: (i, 0, 0, j)),
                   pl.BlockSpec((1, hh, tm, MLA_NOPE), lambda i, j: (i, 0, j, 0)),
                   pl.BlockSpec((1, tm, LANES), lambda i, j: (i, j, 0)),
                   pl.BlockSpec((1, hh, MLA_V, tm), lambda i, j: (i, 0, 0, j))],
        out_shape=[jax.ShapeDtypeStruct((b, hh, MLA_QK_PAD, l), BF16),
                   jax.ShapeDtypeStruct((b, hh, l, MLA_NOPE), BF16),
                   jax.ShapeDtypeStruct((b, l, LANES), BF16),
                   jax.ShapeDtypeStruct((b, hh, MLA_V, l), BF16)],
        compiler_params=_cparams(("parallel", "parallel")),
        name="mla_qkv",
    )(proj, proj, proj, positions.reshape(b, 1, l), *consts)


def _flash_kernel(qt_ref, kn_ref, kr_ref, vt_ref, o_ref, m_ref, l_ref, acc_ref, *, tq, hp):
    qi = pl.program_id(2)
    m_ref[...] = jnp.full(m_ref.shape, -jnp.inf, F32)
    l_ref[...] = jnp.zeros(l_ref.shape, F32)
    acc_ref[...] = jnp.zeros(acc_ref.shape, F32)

    half = tq // 2
    lower = (lax.broadcasted_iota(jnp.int32, (half, half), 0)
             <= lax.broadcasted_iota(jnp.int32, (half, half), 1))

    def blocks(j, parts, diagonal):
        base = pl.multiple_of(j * tq, tq)
        work = []
        for k0, nk, q0, nq in parts:
            rows = pl.ds(base + k0, nk)
            k_rope = kr_ref[0, rows, :]
            for h in range(hp):
                k = jnp.concatenate([kn_ref[0, h, rows, :], k_rope], axis=-1)
                s = jnp.dot(k, qt_ref[0, h, :, q0:q0 + nq], preferred_element_type=F32)
                work.append((h, rows, slice(q0, q0 + nq), s))
        for h, rows, cols, s in work:
            if diagonal:
                square = jnp.where(lower, s[:, :half], jnp.finfo(F32).min)
                s = square if s.shape[1] == half else jnp.concatenate([square, s[:, half:]], axis=1)
            m = m_ref[h, :, cols]
            m_new = jnp.maximum(m, jnp.max(s, axis=0, keepdims=True))
            alpha = jnp.exp2(m - m_new)
            p = jnp.exp2(s - m_new)
            l_ref[h, :, cols] = alpha * l_ref[h, :, cols] + jnp.sum(p, axis=0, keepdims=True)
            acc_ref[h, :, cols] = alpha * acc_ref[h, :, cols] + jnp.dot(
                vt_ref[0, h, :, rows], p.astype(BF16), preferred_element_type=F32)
            m_ref[h, :, cols] = m_new

    def body(j, carry):
        blocks(j, [(0, tq, 0, tq)], False)
        return carry

    lax.fori_loop(0, qi, body, 0)
    blocks(qi, [(0, half, 0, tq), (half, half, half, half)], True)
    for h in range(hp):
        o_ref[0, :, h * MLA_V:(h + 1) * MLA_V] = (acc_ref[h] / l_ref[h]).T.astype(o_ref.dtype)


def _flash(qt, kn, kr, vt, *, tq, hp):
    b, hh, _, l = qt.shape
    return pl.pallas_call(
        functools.partial(_flash_kernel, tq=tq, hp=hp),
        grid=(b, hh // hp, l // tq),
        in_specs=[pl.BlockSpec((1, hp, MLA_QK_PAD, tq), lambda i, h, j: (i, h, 0, j)),
                  pl.BlockSpec((1, hp, l, MLA_NOPE), lambda i, h, j: (i, h, 0, 0)),
                  pl.BlockSpec((1, l, LANES), lambda i, h, j: (i, 0, 0)),
                  pl.BlockSpec((1, hp, MLA_V, l), lambda i, h, j: (i, h, 0, 0))],
        out_specs=pl.BlockSpec((1, tq, hp * MLA_V), lambda i, h, j: (i, j, h)),
        out_shape=jax.ShapeDtypeStruct((b, l, hh * MLA_V), BF16),
        scratch_shapes=[pltpu.VMEM((hp, 1, tq), F32), pltpu.VMEM((hp, 1, tq), F32),
                        pltpu.VMEM((hp, MLA_V, tq), F32)],
        compiler_params=_cparams(("parallel", "parallel", "parallel")),
        name="flash",
    )(qt, kn, kr, vt)


def _s5_layer(x, mem, ln, w_in, lam_re, lam_im, log_step, b_re, b_im, c_re, c_im, d, w_glu,
              w_out, mem_norm, w_mem_kv, xq_norm, xk_norm):
    b, l, dm = x.shape
    w_gx = jnp.concatenate([w_in[:, PRIMARY_WIDTH + XQ_WIDTH:], w_in[:, PRIMARY_WIDTH:PRIMARY_WIDTH + XQ_WIDTH]],
                           axis=1).astype(BF16)
    tm = 512
    proj, xn = _s5_in_proj(x, ln, w_gx, tm=tm, col_chunk=512)
    xg = _s5_ut(xn, w_in[:, :PRIMARY_WIDTH].T.astype(BF16))
    pw_re, pw_im = _s5_pow(lam_re, lam_im, log_step)
    toep, wout, wst = _s5_asm(pw_re.transpose(1, 0, 2), pw_im.transpose(1, 0, 2), c_re, c_im, b_re, b_im)
    n_scan = int(math.log2(LANES))
    first = S5_EXPONENTS.index(S5_CHUNK)
    col = lambda pw: jnp.pad(pw[first:first + n_scan].transpose(1, 2, 0), ((0, 0), (0, 0), (0, 8 - n_scan)))
    dcol = jnp.tile(d.reshape(S5_GROUPS, 1, S5_GROUP_CH), (1, S5_CHUNK, 1)).reshape(S5_GROUPS, -1, 1)
    yt = _s5_mix(xg, toep, wst, wout, col(pw_re), col(pw_im), dcol, nb=b)
    y = _glu(yt, w_glu.astype(BF16), nb=b, col_chunk=256)
    mk, mv = _mem_kv(mem, mem_norm, w_mem_kv, xk_norm)
    return _merge(x, y, proj, BRANCH_WIDTH // XQ_WIDTH, 0, mk, mv, xq_norm, w_out.astype(BF16), tm=tm, phased=True)


def _mla_layer(x, mem, positions, ln, w_in, q_lora_norm, kv_lora_norm, w_uq, w_ukv, q_nope_norm, k_nope_norm,
               q_rope_norm, k_rope_norm, w_out, mem_norm, w_mem_kv, xq_norm, xk_norm):
    b, l, dm = x.shape
    o1 = MLA_Q_LORA
    o2 = o1 + MLA_KV_LORA
    o3 = o2 + MLA_ROPE
    o4 = o3 + XQ_WIDTH
    w_perm = jnp.concatenate([w_in[:, o4:], w_in[:, :o1], w_in[:, o3:o4], w_in[:, o1:o2], w_in[:, o2:o3],
                              jnp.zeros((dm, 2 * LANES - MLA_ROPE), w_in.dtype)], axis=1).astype(BF16)
    proj = _norm_matmul(x.reshape(b * l, dm), ln, w_perm, tm=512, col_chunk=512, name="mla_in_proj")
    proj = proj.reshape(b, l, -1)
    gate_blk = 0
    cq_blk = BRANCH_WIDTH // MLA_Q_LORA
    xq_blk = (BRANCH_WIDTH + MLA_Q_LORA) // XQ_WIDTH
    ckv_blk = (BRANCH_WIDTH + MLA_Q_LORA + XQ_WIDTH) // MLA_KV_LORA
    kr_blk = (BRANCH_WIDTH + MLA_Q_LORA + XQ_WIDTH + MLA_KV_LORA) // LANES
    wq = w_uq.reshape(MLA_Q_LORA, MLA_HEADS, MLA_NOPE + MLA_ROPE)
    wq = jnp.pad(wq, ((0, 0), (0, 0), (0, MLA_QK_PAD - MLA_NOPE - MLA_ROPE)))
    wqt = wq.reshape(MLA_Q_LORA, MLA_HEADS * MLA_QK_PAD).T.astype(BF16)
    wkv = w_ukv.reshape(MLA_KV_LORA, MLA_HEADS, MLA_NOPE + MLA_V)
    wk = wkv[:, :, :MLA_NOPE].reshape(MLA_KV_LORA, MLA_HEADS * MLA_NOPE).astype(BF16)
    wvt = wkv[:, :, MLA_NOPE:].reshape(MLA_KV_LORA, MLA_HEADS * MLA_V).T.astype(BF16)
    qt, kn, kr, vt = _mla_qkv(proj, cq_blk, ckv_blk, kr_blk, positions,
                              (q_lora_norm, kv_lora_norm, q_nope_norm, k_nope_norm, q_rope_norm, k_rope_norm),
                              wqt, wk, wvt, tm=256)
    attn = _flash(qt, kn, kr, vt, tq=512, hp=4)
    mk, mv = _mem_kv(mem, mem_norm, w_mem_kv, xk_norm)
    return _merge(x, attn, proj, xq_blk, gate_blk, mk, mv, xq_norm, w_out.astype(BF16), tm=512)


def kernel(x, mem, positions, ln_gain, w_out, mem_norm, w_mem_kv, xq_norm, xk_norm,
           s5_w_in, s5_lambda_re, s5_lambda_im, s5_log_step, s5_b_re, s5_b_im, s5_c_re, s5_c_im,
           s5_d, s5_w_glu, mla_w_in, mla_q_lora_norm, mla_kv_lora_norm, mla_w_uq, mla_w_ukv,
           mla_q_nope_norm, mla_k_nope_norm, mla_q_rope_norm, mla_k_rope_norm):
    depth = ln_gain.shape[0]
    for i in range(depth):
        j = i // 2
        if i % 2 == 0:
            x = _s5_layer(x, mem, ln_gain[i], s5_w_in[j], s5_lambda_re[j], s5_lambda_im[j], s5_log_step[j],
                          s5_b_re[j], s5_b_im[j], s5_c_re[j], s5_c_im[j], s5_d[j], s5_w_glu[j],
                          w_out[i], mem_norm[i], w_mem_kv[i], xq_norm[i], xk_norm[i])
        else:
            x = _mla_layer(x, mem, positions, ln_gain[i], mla_w_in[j], mla_q_lora_norm[j], mla_kv_lora_norm[j],
                           mla_w_uq[j], mla_w_ukv[j], mla_q_nope_norm[j], mla_k_nope_norm[j],
                           mla_q_rope_norm[j], mla_k_rope_norm[j],
                           w_out[i], mem_norm[i], w_mem_kv[i], xq_norm[i], xk_norm[i])
    return x
```

```python
import functools
import math

import jax
import jax.numpy as jnp
from jax import lax
from jax.experimental import pallas as pl
from jax.experimental.pallas import tpu as pltpu

D_MODEL = 1024
BRANCH_WIDTH = 2 * D_MODEL
XQ_WIDTH = BRANCH_WIDTH // 4
PRIMARY_WIDTH = BRANCH_WIDTH - XQ_WIDTH
X_HEADS = 4
X_HEAD_DIM = XQ_WIDTH // X_HEADS
S5_GROUP_CH = 16
S5_GROUPS = PRIMARY_WIDTH // S5_GROUP_CH
S5_STATE = 64
MLA_NOPE = 128
MLA_ROPE = 64
MLA_V = 128
MLA_HEADS = PRIMARY_WIDTH // MLA_V
MLA_Q_LORA = D_MODEL // 2
MLA_KV_LORA = D_MODEL // 4
ROPE_THETA = 10000.0
EPS = 1e-6

LANES = 128
MLA_QK_PAD = 2 * LANES
S5_CHUNK = 2 * LANES // S5_GROUP_CH
S5_EXPONENTS = list(range(S5_CHUNK + 1)) + [S5_CHUNK * 2 ** i for i in range(1, int(math.log2(LANES)))]
VMEM_LIMIT = 56 * 1024 * 1024

F32 = jnp.float32
BF16 = jnp.bfloat16


def _cparams(sem):
    return pltpu.CompilerParams(dimension_semantics=sem, vmem_limit_bytes=VMEM_LIMIT)


def _rms(x, g):
    return x * lax.rsqrt(jnp.mean(x * x, axis=-1, keepdims=True) + EPS) * g


def _norm_matmul_kernel(x_ref, g_ref, w_ref, o_ref, *, col_chunk):
    xn = _rms(x_ref[...].astype(F32), g_ref[...]).astype(BF16)
    for c in range(o_ref.shape[1] // col_chunk):
        sl = slice(c * col_chunk, (c + 1) * col_chunk)
        o_ref[:, sl] = jnp.dot(xn, w_ref[:, sl], preferred_element_type=F32).astype(o_ref.dtype)


def _norm_matmul(x, g, w, *, tm, col_chunk, name):
    n, d = x.shape
    wout = w.shape[1]
    return pl.pallas_call(
        functools.partial(_norm_matmul_kernel, col_chunk=col_chunk),
        grid=(n // tm,),
        in_specs=[pl.BlockSpec((tm, d), lambda i: (i, 0)),
                  pl.BlockSpec((1, d), lambda i: (0, 0)),
                  pl.BlockSpec((d, wout), lambda i: (0, 0))],
        out_specs=pl.BlockSpec((tm, wout), lambda i: (i, 0)),
        out_shape=jax.ShapeDtypeStruct((n, wout), BF16),
        compiler_params=_cparams(("parallel",)),
        name=name,
    )(x, g.reshape(1, d), w)


def _to_phase_order(a):
    n, d = a.shape
    return jnp.swapaxes(a.reshape(n // S5_CHUNK, S5_CHUNK, d), 0, 1).reshape(n, d)


def _from_phase_order(a):
    n, d = a.shape
    return jnp.swapaxes(a.reshape(S5_CHUNK, n // S5_CHUNK, d), 0, 1).reshape(n, d)


def _s5_in_proj_kernel(x_ref, g_ref, w_ref, o_ref, xn_ref, *, col_chunk):
    tm, d = x_ref.shape[1:]
    nc = tm // S5_CHUNK
    xn = _rms(_to_phase_order(x_ref[0]), g_ref[...]).astype(BF16)
    for s in range(S5_CHUNK):
        xn_ref[0, :, s * d:(s + 1) * d] = xn[s * nc:(s + 1) * nc]
    for c in range(o_ref.shape[2] // col_chunk):
        sl = slice(c * col_chunk, (c + 1) * col_chunk)
        o_ref[0, :, sl] = jnp.dot(xn, w_ref[:, sl], preferred_element_type=F32).astype(o_ref.dtype)


def _s5_in_proj(x, g, w, *, tm, col_chunk):
    b, l, d = x.shape
    wout = w.shape[1]
    nc = tm // S5_CHUNK
    return pl.pallas_call(
        functools.partial(_s5_in_proj_kernel, col_chunk=col_chunk),
        grid=(b, l // tm),
        in_specs=[pl.BlockSpec((1, tm, d), lambda i, j: (i, j, 0)),
                  pl.BlockSpec((1, d), lambda i, j: (0, 0)),
                  pl.BlockSpec((d, wout), lambda i, j: (0, 0))],
        out_specs=[pl.BlockSpec((1, tm, wout), lambda i, j: (i, j, 0)),
                   pl.BlockSpec((1, nc, S5_CHUNK * d), lambda i, j: (i, j, 0))],
        out_shape=[jax.ShapeDtypeStruct((b, l, wout), BF16),
                   jax.ShapeDtypeStruct((b, l // S5_CHUNK, S5_CHUNK * d), BF16)],
        compiler_params=_cparams(("parallel", "parallel")),
        name="s5_in_proj",
    )(x, g.reshape(1, d), w)


def _mem_kv_kernel(m_ref, g_ref, w_ref, kg_ref, k_ref, v_ref):
    mn = _rms(m_ref[0], g_ref[...]).astype(BF16)
    kv = jnp.dot(mn, w_ref[...], preferred_element_type=F32)
    for h in range(X_HEADS):
        sl = slice(h * X_HEAD_DIM, (h + 1) * X_HEAD_DIM)
        k_ref[0, :, sl] = _rms(kv[:, sl], kg_ref[...]).astype(BF16)
    v_ref[0] = kv[:, XQ_WIDTH:].astype(BF16)


def _mem_kv(mem, mem_norm, w_mem_kv, xk_norm):
    b, m, d = mem.shape
    out = jax.ShapeDtypeStruct((b, m, XQ_WIDTH), BF16)
    return pl.pallas_call(
        _mem_kv_kernel,
        grid=(b,),
        in_specs=[pl.BlockSpec((1, m, d), lambda i: (i, 0, 0)),
                  pl.BlockSpec((1, d), lambda i: (0, 0)),
                  pl.BlockSpec((d, 2 * XQ_WIDTH), lambda i: (0, 0)),
                  pl.BlockSpec((1, X_HEAD_DIM), lambda i: (0, 0))],
        out_specs=[pl.BlockSpec((1, m, XQ_WIDTH), lambda i: (i, 0, 0)),
                   pl.BlockSpec((1, m, XQ_WIDTH), lambda i: (i, 0, 0))],
        out_shape=[out, out],
        compiler_params=_cparams(("parallel",)),
        name="mem_kv",
    )(mem, mem_norm.reshape(1, d), w_mem_kv.astype(BF16), xk_norm.reshape(1, X_HEAD_DIM))


def _s5_pow_kernel(lr_ref, li_ref, ls_ref, pr_ref, pi_ref):
    lr, li = lr_ref[...], li_ref[...]
    step = jnp.exp(ls_ref[...])
    zr, zi = lr * step, li * step
    for n, e in enumerate(S5_EXPONENTS):
        mag = jnp.exp(zr * e)
        pr_ref[n] = mag * jnp.cos(zi * e)
        pi_ref[n] = mag * jnp.sin(zi * e)
    ar, ai = pr_ref[1], pi_ref[1]
    den = lr * lr + li * li
    pr_ref[len(S5_EXPONENTS)] = ((ar - 1.0) * lr + ai * li) / den
    pi_ref[len(S5_EXPONENTS)] = (ai * lr - (ar - 1.0) * li) / den


def _s5_pow(lam_re, lam_im, log_step):
    g, p = lam_re.shape
    out = jax.ShapeDtypeStruct((len(S5_EXPONENTS) + 1, g, p), F32)
    return pl.pallas_call(_s5_pow_kernel, out_shape=[out, out], name="s5_pow")(
        lam_re, lam_im, log_step.reshape(g, 1))


def _s5_asm_kernel(pr_ref, pi_ref, cr_ref, ci_ref, btr_ref, bti_ref, br_ref, bi_ref,
                   toep_ref, wout_ref, wst_ref):
    def group(i, carry):
        _s5_asm_group(i, pr_ref, pi_ref, cr_ref, ci_ref, btr_ref, bti_ref, br_ref, bi_ref,
                      toep_ref, wout_ref, wst_ref)
        return carry

    lax.fori_loop(0, pr_ref.shape[0], group, 0)


def _dot_3pass(a, b):
    a_hi, b_hi = a.astype(BF16), b.astype(BF16)
    a_lo = (a - a_hi.astype(F32)).astype(BF16)
    b_lo = (b - b_hi.astype(F32)).astype(BF16)
    dot = functools.partial(jnp.dot, preferred_element_type=F32)
    return dot(a_hi, b_hi) + (dot(a_hi, b_lo) + dot(a_lo, b_hi))


def _s5_asm_group(i, pr_ref, pi_ref, cr_ref, ci_ref, btr_ref, bti_ref, br_ref, bi_ref,
                  toep_ref, wout_ref, wst_ref):
    t = S5_CHUNK
    pr, pi = pr_ref[i], pi_ref[i]
    cr, ci = cr_ref[i], ci_ref[i]
    btr, bti = btr_ref[i], bti_ref[i]
    mr, mi = pr[len(S5_EXPONENTS):], pi[len(S5_EXPONENTS):]
    amr = pr[:t] * mr - pi[:t] * mi
    ami = pr[:t] * mi + pi[:t] * mr
    l_re, l_im, w_re, w_im, o_re, o_im = [], [], [], [], [], []
    for k in range(t):
        ar, ai = amr[k:k + 1], ami[k:k + 1]
        l_re.append(cr * ar - ci * ai)
        l_im.append(cr * ai + ci * ar)
        ar, ai = amr[t - 1 - k:t - k], ami[t - 1 - k:t - k]
        w_re.append(btr * ar - bti * ai)
        w_im.append(btr * ai + bti * ar)
        ar, ai = pr[k + 1:k + 2], pi[k + 1:k + 2]
        o_re.append(cr * ar - ci * ai)
        o_im.append(-(cr * ai + ci * ar))
    cat = lambda parts: jnp.concatenate(parts, axis=0)
    kt = _dot_3pass(cat(l_re), br_ref[i]) - _dot_3pass(cat(l_im), bi_ref[i])
    n = kt.shape[0]
    blk = lax.shift_right_logical(lax.broadcasted_iota(jnp.int32, kt.shape, 1), int(math.log2(S5_GROUP_CH)))
    toep = jnp.where(blk == 0, kt, 0.0)
    for s in range(1, t):
        shifted = jnp.concatenate([jnp.zeros((s * S5_GROUP_CH, n), F32), kt[:n - s * S5_GROUP_CH]], axis=0)
        toep = jnp.where(blk == s, shifted, toep)
    toep_ref[i] = toep.astype(BF16)
    wout_ref[i] = jnp.concatenate([cat(o_re), cat(o_im)], axis=1).astype(BF16)
    wst_ref[i] = jnp.concatenate([cat(w_re), cat(w_im)], axis=1).T.astype(BF16)


def _s5_asm(pw_re, pw_im, c_re, c_im, b_re, b_im):
    g, c, p = c_re.shape
    tc = S5_CHUNK * c
    gs = 8
    blk = lambda a: pl.BlockSpec((gs,) + a.shape[1:], lambda i: (i, 0, 0))
    bt_re, bt_im = b_re.transpose(0, 2, 1), b_im.transpose(0, 2, 1)
    tile = lambda a: jnp.tile(a, (1, 1, S5_CHUNK))
    args = (pw_re, pw_im, c_re, c_im, bt_re, bt_im, tile(b_re), tile(b_im))
    return pl.pallas_call(
        _s5_asm_kernel,
        grid=(g // gs,),
        in_specs=[blk(a) for a in args],
        out_specs=[pl.BlockSpec((gs, tc, tc), lambda i: (i, 0, 0)),
                   pl.BlockSpec((gs, tc, 2 * p), lambda i: (i, 0, 0)),
                   pl.BlockSpec((gs, 2 * p, tc), lambda i: (i, 0, 0))],
        out_shape=[jax.ShapeDtypeStruct((g, tc, tc), BF16), jax.ShapeDtypeStruct((g, tc, 2 * p), BF16),
                   jax.ShapeDtypeStruct((g, 2 * p, tc), BF16)],
        compiler_params=_cparams(("parallel",)),
        name="s5_asm",
    )(*args)


def _s5_ut_kernel(xn_ref, w_ref, o_ref):
    nb, nc, d = xn_ref.shape
    ut = lax.dot_general(w_ref[...], xn_ref[...].reshape(nb * nc, d), (((1,), (1,)), ((), ())),
                         preferred_element_type=F32)
    o_ref[...] = ut.astype(BF16).reshape(o_ref.shape)


def _s5_ut(xn, wut):
    b, nc, td = xn.shape
    d = td // S5_CHUNK
    return pl.pallas_call(
        _s5_ut_kernel,
        grid=(S5_CHUNK,),
        in_specs=[pl.BlockSpec((b, nc, d), lambda s: (0, 0, s)),
                  pl.BlockSpec(wut.shape, lambda s: (0, 0))],
        out_specs=pl.BlockSpec((S5_GROUPS, S5_GROUP_CH, b * nc), lambda s: (0, s, 0)),
        out_shape=jax.ShapeDtypeStruct((S5_GROUPS, S5_CHUNK * S5_GROUP_CH, b * nc), BF16),
        compiler_params=_cparams(("parallel",)),
        name="s5_ut",
    )(xn, wut)


def _s5_mix_kernel(x_ref, toep_ref, wst_ref, wout_ref, sr_ref, si_ref, d_ref, o_ref, *, nb):
    p = S5_STATE
    x = x_ref[0]
    hloc = jnp.dot(wst_ref[0], x, preferred_element_type=F32)
    lane = lax.broadcasted_iota(jnp.int32, (p, LANES), 1)
    n_steps = int(math.log2(LANES))
    pw = []
    for i in range(n_steps):
        keep = lane >= (1 << i)
        pw.append((jnp.where(keep, jnp.broadcast_to(sr_ref[0, :, i:i + 1], (p, LANES)), 0.0),
                   jnp.where(keep, jnp.broadcast_to(si_ref[0, :, i:i + 1], (p, LANES)), 0.0)))
    h_re, h_im = [], []
    for b in range(nb):
        hr = hloc[:p, b * LANES:(b + 1) * LANES]
        hi = hloc[p:, b * LANES:(b + 1) * LANES]
        for i in range(n_steps):
            rs, js = pltpu.roll(hr, 1 << i, 1), pltpu.roll(hi, 1 << i, 1)
            ar, ai = pw[i]
            hr, hi = hr + ar * rs - ai * js, hi + ar * js + ai * rs
        h_re.append(jnp.where(lane >= 1, pltpu.roll(hr, 1, 1), 0.0))
        h_im.append(jnp.where(lane >= 1, pltpu.roll(hi, 1, 1), 0.0))
    h = jnp.concatenate([jnp.concatenate(h_re, axis=1), jnp.concatenate(h_im, axis=1)], axis=0).astype(BF16)
    y = (jnp.dot(toep_ref[0], x, preferred_element_type=F32)
         + jnp.dot(wout_ref[0], h, preferred_element_type=F32)
         + d_ref[0] * x.astype(F32))
    o_ref[...] = jax.nn.gelu(y).astype(o_ref.dtype).reshape(o_ref.shape)


def _s5_mix(xg, toep, wst, wout, sc_re, sc_im, dcol, *, nb):
    g, tc, cols = xg.shape
    assert cols == nb * LANES, "one batch's chunks must fill exactly one 128-lane block"
    blk = lambda a: pl.BlockSpec((1,) + a.shape[1:], lambda i: (i, 0, 0))
    return pl.pallas_call(
        functools.partial(_s5_mix_kernel, nb=nb),
        grid=(g,),
        in_specs=[blk(a) for a in (xg, toep, wst, wout, sc_re, sc_im, dcol)],
        out_specs=pl.BlockSpec((S5_CHUNK, S5_GROUP_CH, cols), lambda i: (0, i, 0)),
        out_shape=jax.ShapeDtypeStruct((S5_CHUNK, g * S5_GROUP_CH, cols), BF16),
        compiler_params=_cparams(("parallel",)),
        name="s5_mix",
    )(xg, toep, wst, wout, sc_re, sc_im, dcol)


def _glu_kernel(y_ref, w_ref, o_ref, *, col_chunk):
    y = y_ref[0].T
    half = o_ref.shape[-1]
    for c in range(half // col_chunk):
        a = jnp.dot(y, w_ref[:, c * col_chunk:(c + 1) * col_chunk], preferred_element_type=F32)
        g = jnp.dot(y, w_ref[:, half + c * col_chunk:half + (c + 1) * col_chunk], preferred_element_type=F32)
        o_ref[:, :, c * col_chunk:(c + 1) * col_chunk] = (
            (a * jax.nn.sigmoid(g)).astype(o_ref.dtype).reshape(o_ref.shape[:2] + (col_chunk,)))


def _glu(yt, w, *, nb, col_chunk):
    t, k, cols = yt.shape
    nc = cols // nb
    half = w.shape[1] // 2
    return pl.pallas_call(
        functools.partial(_glu_kernel, col_chunk=col_chunk),
        grid=(t,),
        in_specs=[pl.BlockSpec((1, k, cols), lambda j: (j, 0, 0)),
                  pl.BlockSpec(w.shape, lambda j: (0, 0))],
        out_specs=pl.BlockSpec((nb, nc, half), lambda j: (0, 0, j)),
        out_shape=jax.ShapeDtypeStruct((nb, nc, t * half), BF16),
        compiler_params=_cparams(("parallel",)),
        name="glu",
    )(yt, w)


def _merge_kernel(x_ref, mix_ref, xq_ref, gate_ref, k_ref, v_ref, qg_ref, w_ref, o_ref, cat_ref, *, phased):
    gate = gate_ref[0]
    sg = gate * jax.nn.sigmoid(gate)
    if phased:
        nc = gate.shape[0] // S5_CHUNK
        for s in range(S5_CHUNK):
            rows = slice(s * nc, (s + 1) * nc)
            mix = mix_ref[0, :, s * PRIMARY_WIDTH:(s + 1) * PRIMARY_WIDTH]
            cat_ref[rows, :PRIMARY_WIDTH] = mix * sg[rows, :PRIMARY_WIDTH]
    else:
        cat_ref[:, :PRIMARY_WIDTH] = mix_ref[0] * sg[:, :PRIMARY_WIDTH]
    scale = X_HEAD_DIM ** -0.5
    for h in range(X_HEADS):
        sl = slice(h * X_HEAD_DIM, (h + 1) * X_HEAD_DIM)
        q = _rms(xq_ref[0, :, sl].astype(F32), qg_ref[...]).astype(BF16)
        s = lax.dot_general(q, k_ref[0, :, sl], (((1,), (1,)), ((), ())), preferred_element_type=F32) * scale
        p = jnp.exp(s - jnp.max(s, axis=-1, keepdims=True))
        p = (p / jnp.sum(p, axis=-1, keepdims=True)).astype(BF16)
        mo = jnp.dot(p, v_ref[0, :, sl], preferred_element_type=F32)
        osl = slice(PRIMARY_WIDTH + h * X_HEAD_DIM, PRIMARY_WIDTH + (h + 1) * X_HEAD_DIM)
        cat_ref[:, osl] = mo.astype(BF16) * sg[:, osl]
    delta = jnp.dot(cat_ref[...], w_ref[...], preferred_element_type=F32)
    o_ref[0] = x_ref[0] + (_from_phase_order(delta) if phased else delta)


def _merge(x, mix, proj, xq_blk, gate_blk, mk, mv, xq_norm, w_out, *, tm, phased=False):
    b, l, d = x.shape
    m = mk.shape[1]
    mix_spec = (pl.BlockSpec((1, tm // S5_CHUNK, S5_CHUNK * PRIMARY_WIDTH), lambda i, j: (i, j, 0)) if phased
                else pl.BlockSpec((1, tm, PRIMARY_WIDTH), lambda i, j: (i, j, 0)))
    return pl.pallas_call(
        functools.partial(_merge_kernel, phased=phased),
        grid=(b, l // tm),
        in_specs=[pl.BlockSpec((1, tm, d), lambda i, j: (i, j, 0)),
                  mix_spec,
                  pl.BlockSpec((1, tm, XQ_WIDTH), lambda i, j: (i, j, xq_blk)),
                  pl.BlockSpec((1, tm, BRANCH_WIDTH), lambda i, j: (i, j, gate_blk)),
                  pl.BlockSpec((1, m, XQ_WIDTH), lambda i, j: (i, 0, 0)),
                  pl.BlockSpec((1, m, XQ_WIDTH), lambda i, j: (i, 0, 0)),
                  pl.BlockSpec((1, X_HEAD_DIM), lambda i, j: (0, 0)),
                  pl.BlockSpec((BRANCH_WIDTH, d), lambda i, j: (0, 0))],
        out_specs=pl.BlockSpec((1, tm, d), lambda i, j: (i, j, 0)),
        out_shape=jax.ShapeDtypeStruct((b, l, d), F32),
        scratch_shapes=[pltpu.VMEM((tm, BRANCH_WIDTH), BF16)],
        compiler_params=_cparams(("parallel", "parallel")),
        name="merge",
    )(x, mix, proj, proj, mk, mv, xq_norm.reshape(1, X_HEAD_DIM), w_out)


def _mla_qkv_kernel(cq_ref, ckv_ref, kr_ref, posr_ref, invfc_ref, gq_ref, gkv_ref, gqn_ref,
                    gkn_ref, gqr_ref, gkr_ref, wqt_ref, wk_ref, wvt_ref, qt_ref, kn_ref, krope_ref, vt_ref):
    half = MLA_ROPE // 2
    tm = cq_ref.shape[1]
    qscale = (MLA_NOPE + MLA_ROPE) ** -0.5 * math.log2(math.e)

    cq = _rms(cq_ref[0].astype(F32), gq_ref[...])
    ckv = _rms(ckv_ref[0].astype(F32), gkv_ref[...])
    cq_t = cq.T.astype(BF16)
    ckv_t = ckv.T.astype(BF16)
    q_t = jnp.dot(wqt_ref[...], cq_t, preferred_element_type=F32)
    v_t = jnp.dot(wvt_ref[...], ckv_t, preferred_element_type=F32)
    k_n = jnp.dot(ckv.astype(BF16), wk_ref[...], preferred_element_type=F32)

    ang_t = invfc_ref[...] * posr_ref[0].astype(F32)
    cos_t, sin_t = jnp.cos(ang_t), jnp.sin(ang_t)
    g_nope = jnp.broadcast_to(gqn_ref[...], (MLA_NOPE, tm)) * qscale
    g_r1 = jnp.broadcast_to(gqr_ref[:half, :], (half, tm)) * qscale
    g_r2 = jnp.broadcast_to(gqr_ref[half:, :], (half, tm)) * qscale
    for h in range(MLA_HEADS):
        q = q_t[h * MLA_QK_PAD:(h + 1) * MLA_QK_PAD]
        nope = q[:MLA_NOPE]
        r = lax.rsqrt(jnp.mean(nope * nope, axis=0, keepdims=True) + EPS)
        qt_ref[0, h, :MLA_NOPE, :] = (nope * r * g_nope).astype(BF16)
        x1, x2 = q[MLA_NOPE:MLA_NOPE + half], q[MLA_NOPE + half:MLA_NOPE + MLA_ROPE]
        ss = jnp.sum(x1 * x1, axis=0, keepdims=True) + jnp.sum(x2 * x2, axis=0, keepdims=True)
        r = lax.rsqrt(ss * (1.0 / MLA_ROPE) + EPS)
        x1, x2 = x1 * r * g_r1, x2 * r * g_r2
        qt_ref[0, h, MLA_NOPE:MLA_NOPE + half, :] = (x1 * cos_t - x2 * sin_t).astype(BF16)
        qt_ref[0, h, MLA_NOPE + half:MLA_NOPE + MLA_ROPE, :] = (x1 * sin_t + x2 * cos_t).astype(BF16)
        qt_ref[0, h, MLA_NOPE + MLA_ROPE:, :] = jnp.zeros((MLA_QK_PAD - MLA_NOPE - MLA_ROPE, tm), BF16)
        kn_ref[0, h] = _rms(k_n[:, h * MLA_NOPE:(h + 1) * MLA_NOPE], gkn_ref[...]).astype(BF16)
        vt_ref[0, h] = v_t[h * MLA_V:(h + 1) * MLA_V].astype(BF16)

    kr_t = kr_ref[0].astype(F32).T
    x1, x2 = kr_t[:half], kr_t[half:MLA_ROPE]
    ss = jnp.sum(x1 * x1, axis=0, keepdims=True) + jnp.sum(x2 * x2, axis=0, keepdims=True)
    r = lax.rsqrt(ss * (1.0 / MLA_ROPE) + EPS)
    x1, x2 = x1 * r * gkr_ref[:half, :], x2 * r * gkr_ref[half:, :]
    rot = jnp.concatenate([x1 * cos_t - x2 * sin_t, x1 * sin_t + x2 * cos_t,
                           jnp.zeros((LANES - MLA_ROPE, tm), F32)], axis=0)
    krope_ref[0] = rot.T.astype(BF16)


def _mla_qkv(proj, cq_blk, ckv_blk, kr_blk, positions, gains, wqt, wk, wvt, *, tm):
    b, l, _ = proj.shape
    hh = MLA_HEADS
    half = MLA_ROPE // 2
    inv_freq = ROPE_THETA ** (-jnp.arange(half, dtype=F32) / half)
    const = lambda a: pl.BlockSpec(a.shape, lambda i, j: (0,) * a.ndim)
    gq, gkv, gqn, gkn, gqr, gkr = gains
    consts = [inv_freq.reshape(half, 1), gq.reshape(1, -1), gkv.reshape(1, -1), gqn.reshape(-1, 1),
              gkn.reshape(1, -1), gqr.reshape(-1, 1), gkr.reshape(-1, 1), wqt, wk, wvt]
    return pl.pallas_call(
        _mla_qkv_kernel,
        grid=(b, l // tm),
        in_specs=[pl.BlockSpec((1, tm, MLA_Q_LORA), lambda i, j: (i, j, cq_blk)),
                  pl.BlockSpec((1, tm, MLA_KV_LORA), lambda i, j: (i, j, ckv_blk)),
                  pl.BlockSpec((1, tm, LANES), lambda i, j: (i, j, kr_blk)),
                  pl.BlockSpec((1, 1, tm), lambda i, j: (i, 0, j))] + [const(a) for a in consts],
        out_specs=[pl.BlockSpec((1, hh, MLA_QK_PAD, tm), lambda i, j: (i, 0, 0, j)),
                   pl.BlockSpec((1, hh, tm, MLA_NOPE), lambda i, j: (i, 0, j, 0)),
                   pl.BlockSpec((1, tm, LANES), lambda i, j: (i, j, 0)),
                   pl.BlockSpec((1, hh, MLA_V, tm), lambda i, j: (i, 0, 0, j))],
        out_shape=[jax.ShapeDtypeStruct((b, hh, MLA_QK_PAD, l), BF16),
                   jax.ShapeDtypeStruct((b, hh, l, MLA_NOPE), BF16),
                   jax.ShapeDtypeStruct((b, l, LANES), BF16),
                   jax.ShapeDtypeStruct((b, hh, MLA_V, l), BF16)],
        compiler_params=_cparams(("parallel", "parallel")),
        name="mla_qkv",
    )(proj, proj, proj, positions.reshape(b, 1, l), *consts)


def _flash_kernel(qt_ref, kn_ref, kr_ref, vt_ref, o_ref, m_ref, l_ref, acc_ref, *, tq, hp, ahead):
    qi = pl.program_id(2)
    m_ref[...] = jnp.full(m_ref.shape, -jnp.inf, F32)
    l_ref[...] = jnp.zeros(l_ref.shape, F32)
    acc_ref[...] = jnp.zeros(acc_ref.shape, F32)

    half = tq // 2
    lower = (lax.broadcasted_iota(jnp.int32, (half, half), 0)
             <= lax.broadcasted_iota(jnp.int32, (half, half), 1))

    def blocks(j, parts, diagonal):
        base = pl.multiple_of(j * tq, tq)
        items = [(h, pl.ds(base + k0, nk), slice(q0, q0 + nq)) for k0, nk, q0, nq in parts for h in range(hp)]

        def scores(h, rows, cols):
            k = jnp.concatenate([kn_ref[0, h, rows, :], kr_ref[0, rows, :]], axis=-1)
            return jnp.dot(k, qt_ref[0, h, :, cols], preferred_element_type=F32)

        pending = [scores(*it) for it in items[:ahead]]
        for n, (h, rows, cols) in enumerate(items):
            if n + ahead < len(items):
                pending.append(scores(*items[n + ahead]))
            s = pending[n]
            if diagonal:
                square = jnp.where(lower, s[:, :half], jnp.finfo(F32).min)
                s = square if s.shape[1] == half else jnp.concatenate([square, s[:, half:]], axis=1)
            m = m_ref[h, :, cols]
            m_new = jnp.maximum(m, jnp.max(s, axis=0, keepdims=True))
            alpha = jnp.exp2(m - m_new)
            p = jnp.exp2(s - m_new)
            l_ref[h, :, cols] = alpha * l_ref[h, :, cols] + jnp.sum(p, axis=0, keepdims=True)
            acc_ref[h, :, cols] = alpha * acc_ref[h, :, cols] + jnp.dot(
                vt_ref[0, h, :, rows], p.astype(BF16), preferred_element_type=F32)
            m_ref[h, :, cols] = m_new

    def body(j, carry):
        blocks(j, [(0, tq, 0, tq)], False)
        return carry

    lax.fori_loop(0, qi, body, 0)
    blocks(qi, [(0, half, 0, tq), (half, half, half, half)], True)
    for h in range(hp):
        o_ref[0, :, h * MLA_V:(h + 1) * MLA_V] = (acc_ref[h] / l_ref[h]).T.astype(o_ref.dtype)


def _flash(qt, kn, kr, vt, *, tq, hp, ahead):
    b, hh, _, l = qt.shape
    return pl.pallas_call(
        functools.partial(_flash_kernel, tq=tq, hp=hp, ahead=ahead),
        grid=(b, hh // hp, l // tq),
        in_specs=[pl.BlockSpec((1, hp, MLA_QK_PAD, tq), lambda i, h, j: (i, h, 0, j)),
                  pl.BlockSpec((1, hp, l, MLA_NOPE), lambda i, h, j: (i, h, 0, 0)),
                  pl.BlockSpec((1, l, LANES), lambda i, h, j: (i, 0, 0)),
                  pl.BlockSpec((1, hp, MLA_V, l), lambda i, h, j: (i, h, 0, 0))],
        out_specs=pl.BlockSpec((1, tq, hp * MLA_V), lambda i, h, j: (i, j, h)),
        out_shape=jax.ShapeDtypeStruct((b, l, hh * MLA_V), BF16),
        scratch_shapes=[pltpu.VMEM((hp, 1, tq), F32), pltpu.VMEM((hp, 1, tq), F32),
                        pltpu.VMEM((hp, MLA_V, tq), F32)],
        compiler_params=_cparams(("parallel", "parallel", "parallel")),
        name="flash",
    )(qt, kn, kr, vt)


def _s5_layer(x, mem, ln, w_in, lam_re, lam_im, log_step, b_re, b_im, c_re, c_im, d, w_glu,
              w_out, mem_norm, w_mem_kv, xq_norm, xk_norm):
    b, l, dm = x.shape
    w_gx = jnp.concatenate([w_in[:, PRIMARY_WIDTH + XQ_WIDTH:], w_in[:, PRIMARY_WIDTH:PRIMARY_WIDTH + XQ_WIDTH]],
                           axis=1).astype(BF16)
    tm = 512
    proj, xn = _s5_in_proj(x, ln, w_gx, tm=tm, col_chunk=512)
    xg = _s5_ut(xn, w_in[:, :PRIMARY_WIDTH].T.astype(BF16))
    pw_re, pw_im = _s5_pow(lam_re, lam_im, log_step)
    toep, wout, wst = _s5_asm(pw_re.transpose(1, 0, 2), pw_im.transpose(1, 0, 2), c_re, c_im, b_re, b_im)
    n_scan = int(math.log2(LANES))
    first = S5_EXPONENTS.index(S5_CHUNK)
    col = lambda pw: jnp.pad(pw[first:first + n_scan].transpose(1, 2, 0), ((0, 0), (0, 0), (0, 8 - n_scan)))
    dcol = jnp.tile(d.reshape(S5_GROUPS, 1, S5_GROUP_CH), (1, S5_CHUNK, 1)).reshape(S5_GROUPS, -1, 1)
    yt = _s5_mix(xg, toep, wst, wout, col(pw_re), col(pw_im), dcol, nb=b)
    y = _glu(yt, w_glu.astype(BF16), nb=b, col_chunk=256)
    mk, mv = _mem_kv(mem, mem_norm, w_mem_kv, xk_norm)
    return _merge(x, y, proj, BRANCH_WIDTH // XQ_WIDTH, 0, mk, mv, xq_norm, w_out.astype(BF16), tm=tm, phased=True)


def _mla_layer(x, mem, positions, ln, w_in, q_lora_norm, kv_lora_norm, w_uq, w_ukv, q_nope_norm, k_nope_norm,
               q_rope_norm, k_rope_norm, w_out, mem_norm, w_mem_kv, xq_norm, xk_norm):
    b, l, dm = x.shape
    o1 = MLA_Q_LORA
    o2 = o1 + MLA_KV_LORA
    o3 = o2 + MLA_ROPE
    o4 = o3 + XQ_WIDTH
    w_perm = jnp.concatenate([w_in[:, o4:], w_in[:, :o1], w_in[:, o3:o4], w_in[:, o1:o2], w_in[:, o2:o3],
                              jnp.zeros((dm, 2 * LANES - MLA_ROPE), w_in.dtype)], axis=1).astype(BF16)
    proj = _norm_matmul(x.reshape(b * l, dm), ln, w_perm, tm=512, col_chunk=512, name="mla_in_proj")
    proj = proj.reshape(b, l, -1)
    gate_blk = 0
    cq_blk = BRANCH_WIDTH // MLA_Q_LORA
    xq_blk = (BRANCH_WIDTH + MLA_Q_LORA) // XQ_WIDTH
    ckv_blk = (BRANCH_WIDTH + MLA_Q_LORA + XQ_WIDTH) // MLA_KV_LORA
    kr_blk = (BRANCH_WIDTH + MLA_Q_LORA + XQ_WIDTH + MLA_KV_LORA) // LANES
    wq = w_uq.reshape(MLA_Q_LORA, MLA_HEADS, MLA_NOPE + MLA_ROPE)
    wq = jnp.pad(wq, ((0, 0), (0, 0), (0, MLA_QK_PAD - MLA_NOPE - MLA_ROPE)))
    wqt = wq.reshape(MLA_Q_LORA, MLA_HEADS * MLA_QK_PAD).T.astype(BF16)
    wkv = w_ukv.reshape(MLA_KV_LORA, MLA_HEADS, MLA_NOPE + MLA_V)
    wk = wkv[:, :, :MLA_NOPE].reshape(MLA_KV_LORA, MLA_HEADS * MLA_NOPE).astype(BF16)
    wvt = wkv[:, :, MLA_NOPE:].reshape(MLA_KV_LORA, MLA_HEADS * MLA_V).T.astype(BF16)
    qt, kn, kr, vt = _mla_qkv(proj, cq_blk, ckv_blk, kr_blk, positions,
                              (q_lora_norm, kv_lora_norm, q_nope_norm, k_nope_norm, q_rope_norm, k_rope_norm),
                              wqt, wk, wvt, tm=256)
    attn = _flash(qt, kn, kr, vt, tq=512, hp=12, ahead=2)
    mk, mv = _mem_kv(mem, mem_norm, w_mem_kv, xk_norm)
    return _merge(x, attn, proj, xq_blk, gate_blk, mk, mv, xq_norm, w_out.astype(BF16), tm=512)


def kernel(x, mem, positions, ln_gain, w_out, mem_norm, w_mem_kv, xq_norm, xk_norm,
           s5_w_in, s5_lambda_re, s5_lambda_im, s5_log_step, s5_b_re, s5_b_im, s5_c_re, s5_c_im,
           s5_d, s5_w_glu, mla_w_in, mla_q_lora_norm, mla_kv_lora_norm, mla_w_uq, mla_w_ukv,
           mla_q_nope_norm, mla_k_nope_norm, mla_q_rope_norm, mla_k_rope_norm):
    depth = ln_gain.shape[0]
    for i in range(depth):
        j = i // 2
        if i % 2 == 0:
            x = _s5_layer(x, mem, ln_gain[i], s5_w_in[j], s5_lambda_re[j], s5_lambda_im[j], s5_log_step[j],
                          s5_b_re[j], s5_b_im[j], s5_c_re[j], s5_c_im[j], s5_d[j], s5_w_glu[j],
                          w_out[i], mem_norm[i], w_mem_kv[i], xq_norm[i], xk_norm[i])
        else:
            x = _mla_layer(x, mem, positions, ln_gain[i], mla_w_in[j], mla_q_lora_norm[j], mla_kv_lora_norm[j],
                           mla_w_uq[j], mla_w_ukv[j], mla_q_nope_norm[j], mla_k_nope_norm[j],
                           mla_q_rope_norm[j], mla_k_rope_norm[j],
                           w_out[i], mem_norm[i], w_mem_kv[i], xq_norm[i], xk_norm[i])
    return x
```

```python
import functools
import math

import jax
import jax.numpy as jnp
from jax import lax
from jax.experimental import pallas as pl
from jax.experimental.pallas import tpu as pltpu

D_MODEL = 1024
BRANCH_WIDTH = 2 * D_MODEL
XQ_WIDTH = BRANCH_WIDTH // 4
PRIMARY_WIDTH = BRANCH_WIDTH - XQ_WIDTH
X_HEADS = 4
X_HEAD_DIM = XQ_WIDTH // X_HEADS
S5_GROUP_CH = 16
S5_GROUPS = PRIMARY_WIDTH // S5_GROUP_CH
S5_STATE = 64
MLA_NOPE = 128
MLA_ROPE = 64
MLA_V = 128
MLA_HEADS = PRIMARY_WIDTH // MLA_V
MLA_Q_LORA = D_MODEL // 2
MLA_KV_LORA = D_MODEL // 4
ROPE_THETA = 10000.0
EPS = 1e-6

LANES = 128
MLA_QK_PAD = 2 * LANES
S5_CHUNK = 2 * LANES // S5_GROUP_CH
S5_EXPONENTS = list(range(S5_CHUNK + 1)) + [S5_CHUNK * 2 ** i for i in range(1, int(math.log2(LANES)))]
VMEM_LIMIT = 56 * 1024 * 1024

F32 = jnp.float32
BF16 = jnp.bfloat16


def _cparams(sem):
    return pltpu.CompilerParams(dimension_semantics=sem, vmem_limit_bytes=VMEM_LIMIT)


def _rms(x, g):
    return x * lax.rsqrt(jnp.mean(x * x, axis=-1, keepdims=True) + EPS) * g


def _norm_matmul_kernel(x_ref, g_ref, w_ref, o_ref, *, col_chunk):
    xn = _rms(x_ref[...].astype(F32), g_ref[...]).astype(BF16)
    for c in range(o_ref.shape[1] // col_chunk):
        sl = slice(c * col_chunk, (c + 1) * col_chunk)
        o_ref[:, sl] = jnp.dot(xn, w_ref[:, sl], preferred_element_type=F32).astype(o_ref.dtype)


def _norm_matmul(x, g, w, *, tm, col_chunk, name):
    n, d = x.shape
    wout = w.shape[1]
    return pl.pallas_call(
        functools.partial(_norm_matmul_kernel, col_chunk=col_chunk),
        grid=(n // tm,),
        in_specs=[pl.BlockSpec((tm, d), lambda i: (i, 0)),
                  pl.BlockSpec((1, d), lambda i: (0, 0)),
                  pl.BlockSpec((d, wout), lambda i: (0, 0))],
        out_specs=pl.BlockSpec((tm, wout), lambda i: (i, 0)),
        out_shape=jax.ShapeDtypeStruct((n, wout), BF16),
        compiler_params=_cparams(("parallel",)),
        name=name,
    )(x, g.reshape(1, d), w)


def _to_phase_order(a):
    n, d = a.shape
    return jnp.swapaxes(a.reshape(n // S5_CHUNK, S5_CHUNK, d), 0, 1).reshape(n, d)


def _from_phase_order(a):
    n, d = a.shape
    return jnp.swapaxes(a.reshape(S5_CHUNK, n // S5_CHUNK, d), 0, 1).reshape(n, d)


def _s5_in_proj_kernel(x_ref, g_ref, w_ref, o_ref, xn_ref, *, col_chunk):
    tm, d = x_ref.shape[1:]
    nc = tm // S5_CHUNK
    xn = _rms(_to_phase_order(x_ref[0]), g_ref[...]).astype(BF16)
    for s in range(S5_CHUNK):
        xn_ref[0, :, s * d:(s + 1) * d] = xn[s * nc:(s + 1) * nc]
    for c in range(o_ref.shape[2] // col_chunk):
        sl = slice(c * col_chunk, (c + 1) * col_chunk)
        o_ref[0, :, sl] = jnp.dot(xn, w_ref[:, sl], preferred_element_type=F32).astype(o_ref.dtype)


def _s5_in_proj(x, g, w, *, tm, col_chunk):
    b, l, d = x.shape
    wout = w.shape[1]
    nc = tm // S5_CHUNK
    return pl.pallas_call(
        functools.partial(_s5_in_proj_kernel, col_chunk=col_chunk),
        grid=(b, l // tm),
        in_specs=[pl.BlockSpec((1, tm, d), lambda i, j: (i, j, 0)),
                  pl.BlockSpec((1, d), lambda i, j: (0, 0)),
                  pl.BlockSpec((d, wout), lambda i, j: (0, 0))],
        out_specs=[pl.BlockSpec((1, tm, wout), lambda i, j: (i, j, 0)),
                   pl.BlockSpec((1, nc, S5_CHUNK * d), lambda i, j: (i, j, 0))],
        out_shape=[jax.ShapeDtypeStruct((b, l, wout), BF16),
                   jax.ShapeDtypeStruct((b, l // S5_CHUNK, S5_CHUNK * d), BF16)],
        compiler_params=_cparams(("parallel", "parallel")),
        name="s5_in_proj",
    )(x, g.reshape(1, d), w)


def _mem_kv_kernel(m_ref, g_ref, w_ref, kg_ref, k_ref, v_ref):
    mn = _rms(m_ref[0], g_ref[...]).astype(BF16)
    kv = jnp.dot(mn, w_ref[...], preferred_element_type=F32)
    for h in range(X_HEADS):
        sl = slice(h * X_HEAD_DIM, (h + 1) * X_HEAD_DIM)
        k_ref[0, :, sl] = _rms(kv[:, sl], kg_ref[...]).astype(BF16)
    v_ref[0] = kv[:, XQ_WIDTH:].astype(BF16)


def _mem_kv(mem, mem_norm, w_mem_kv, xk_norm):
    b, m, d = mem.shape
    out = jax.ShapeDtypeStruct((b, m, XQ_WIDTH), BF16)
    return pl.pallas_call(
        _mem_kv_kernel,
        grid=(b,),
        in_specs=[pl.BlockSpec((1, m, d), lambda i: (i, 0, 0)),
                  pl.BlockSpec((1, d), lambda i: (0, 0)),
                  pl.BlockSpec((d, 2 * XQ_WIDTH), lambda i: (0, 0)),
                  pl.BlockSpec((1, X_HEAD_DIM), lambda i: (0, 0))],
        out_specs=[pl.BlockSpec((1, m, XQ_WIDTH), lambda i: (i, 0, 0)),
                   pl.BlockSpec((1, m, XQ_WIDTH), lambda i: (i, 0, 0))],
        out_shape=[out, out],
        compiler_params=_cparams(("parallel",)),
        name="mem_kv",
    )(mem, mem_norm.reshape(1, d), w_mem_kv.astype(BF16), xk_norm.reshape(1, X_HEAD_DIM))


def _s5_pow_kernel(lr_ref, li_ref, ls_ref, pr_ref, pi_ref):
    lr, li = lr_ref[...], li_ref[...]
    step = jnp.exp(ls_ref[...])
    zr, zi = lr * step, li * step
    for n, e in enumerate(S5_EXPONENTS):
        mag = jnp.exp(zr * e)
        pr_ref[n] = mag * jnp.cos(zi * e)
        pi_ref[n] = mag * jnp.sin(zi * e)
    ar, ai = pr_ref[1], pi_ref[1]
    den = lr * lr + li * li
    pr_ref[len(S5_EXPONENTS)] = ((ar - 1.0) * lr + ai * li) / den
    pi_ref[len(S5_EXPONENTS)] = (ai * lr - (ar - 1.0) * li) / den


def _s5_pow(lam_re, lam_im, log_step):
    g, p = lam_re.shape
    out = jax.ShapeDtypeStruct((len(S5_EXPONENTS) + 1, g, p), F32)
    return pl.pallas_call(_s5_pow_kernel, out_shape=[out, out], name="s5_pow")(
        lam_re, lam_im, log_step.reshape(g, 1))


def _s5_asm_kernel(pr_ref, pi_ref, cr_ref, ci_ref, btr_ref, bti_ref, br_ref, bi_ref,
                   toep_ref, wout_ref, wst_ref):
    def group(i, carry):
        _s5_asm_group(i, pr_ref, pi_ref, cr_ref, ci_ref, btr_ref, bti_ref, br_ref, bi_ref,
                      toep_ref, wout_ref, wst_ref)
        return carry

    lax.fori_loop(0, pr_ref.shape[0], group, 0)


def _dot_3pass(a, b):
    a_hi, b_hi = a.astype(BF16), b.astype(BF16)
    a_lo = (a - a_hi.astype(F32)).astype(BF16)
    b_lo = (b - b_hi.astype(F32)).astype(BF16)
    dot = functools.partial(jnp.dot, preferred_element_type=F32)
    return dot(a_hi, b_hi) + (dot(a_hi, b_lo) + dot(a_lo, b_hi))


def _s5_asm_group(i, pr_ref, pi_ref, cr_ref, ci_ref, btr_ref, bti_ref, br_ref, bi_ref,
                  toep_ref, wout_ref, wst_ref):
    t = S5_CHUNK
    pr, pi = pr_ref[i], pi_ref[i]
    cr, ci = cr_ref[i], ci_ref[i]
    btr, bti = btr_ref[i], bti_ref[i]
    mr, mi = pr[len(S5_EXPONENTS):], pi[len(S5_EXPONENTS):]
    amr = pr[:t] * mr - pi[:t] * mi
    ami = pr[:t] * mi + pi[:t] * mr
    l_re, l_im, w_re, w_im, o_re, o_im = [], [], [], [], [], []
    for k in range(t):
        ar, ai = amr[k:k + 1], ami[k:k + 1]
        l_re.append(cr * ar - ci * ai)
        l_im.append(cr * ai + ci * ar)
        ar, ai = amr[t - 1 - k:t - k], ami[t - 1 - k:t - k]
        w_re.append(btr * ar - bti * ai)
        w_im.append(btr * ai + bti * ar)
        ar, ai = pr[k + 1:k + 2], pi[k + 1:k + 2]
        o_re.append(cr * ar - ci * ai)
        o_im.append(-(cr * ai + ci * ar))
    cat = lambda parts: jnp.concatenate(parts, axis=0)
    kt = _dot_3pass(cat(l_re), br_ref[i]) - _dot_3pass(cat(l_im), bi_ref[i])
    n = kt.shape[0]
    blk = lax.shift_right_logical(lax.broadcasted_iota(jnp.int32, kt.shape, 1), int(math.log2(S5_GROUP_CH)))
    toep = jnp.where(blk == 0, kt, 0.0)
    for s in range(1, t):
        shifted = jnp.concatenate([jnp.zeros((s * S5_GROUP_CH, n), F32), kt[:n - s * S5_GROUP_CH]], axis=0)
        toep = jnp.where(blk == s, shifted, toep)
    toep_ref[i] = toep.astype(BF16)
    wout_ref[i] = jnp.concatenate([cat(o_re), cat(o_im)], axis=1).astype(BF16)
    wst_ref[i] = jnp.concatenate([cat(w_re), cat(w_im)], axis=1).T.astype(BF16)


def _s5_asm(pw_re, pw_im, c_re, c_im, b_re, b_im):
    g, c, p = c_re.shape
    tc = S5_CHUNK * c
    gs = 8
    blk = lambda a: pl.BlockSpec((gs,) + a.shape[1:], lambda i: (i, 0, 0))
    bt_re, bt_im = b_re.transpose(0, 2, 1), b_im.transpose(0, 2, 1)
    tile = lambda a: jnp.tile(a, (1, 1, S5_CHUNK))
    args = (pw_re, pw_im, c_re, c_im, bt_re, bt_im, tile(b_re), tile(b_im))
    return pl.pallas_call(
        _s5_asm_kernel,
        grid=(g // gs,),
        in_specs=[blk(a) for a in args],
        out_specs=[pl.BlockSpec((gs, tc, tc), lambda i: (i, 0, 0)),
                   pl.BlockSpec((gs, tc, 2 * p), lambda i: (i, 0, 0)),
                   pl.BlockSpec((gs, 2 * p, tc), lambda i: (i, 0, 0))],
        out_shape=[jax.ShapeDtypeStruct((g, tc, tc), BF16), jax.ShapeDtypeStruct((g, tc, 2 * p), BF16),
                   jax.ShapeDtypeStruct((g, 2 * p, tc), BF16)],
        compiler_params=_cparams(("parallel",)),
        name="s5_asm",
    )(*args)


def _s5_ut_kernel(xn_ref, w_ref, o_ref):
    nb, nc, d = xn_ref.shape
    ut = lax.dot_general(w_ref[...], xn_ref[...].reshape(nb * nc, d), (((1,), (1,)), ((), ())),
                         preferred_element_type=F32)
    o_ref[...] = ut.astype(BF16).reshape(o_ref.shape)


def _s5_ut(xn, wut):
    b, nc, td = xn.shape
    d = td // S5_CHUNK
    return pl.pallas_call(
        _s5_ut_kernel,
        grid=(S5_CHUNK,),
        in_specs=[pl.BlockSpec((b, nc, d), lambda s: (0, 0, s)),
                  pl.BlockSpec(wut.shape, lambda s: (0, 0))],
        out_specs=pl.BlockSpec((S5_GROUPS, S5_GROUP_CH, b * nc), lambda s: (0, s, 0)),
        out_shape=jax.ShapeDtypeStruct((S5_GROUPS, S5_CHUNK * S5_GROUP_CH, b * nc), BF16),
        compiler_params=_cparams(("parallel",)),
        name="s5_ut",
    )(xn, wut)


def _s5_mix_kernel(x_ref, toep_ref, wst_ref, wout_ref, sr_ref, si_ref, d_ref, o_ref, *, nb):
    p = S5_STATE
    gs = x_ref.shape[0]
    lane = lax.broadcasted_iota(jnp.int32, (p, LANES), 1)
    n_steps = int(math.log2(LANES))

    def scan(g, hloc):
        pw = []
        for i in range(n_steps):
            keep = lane >= (1 << i)
            pw.append((jnp.where(keep, jnp.broadcast_to(sr_ref[g, :, i:i + 1], (p, LANES)), 0.0),
                       jnp.where(keep, jnp.broadcast_to(si_ref[g, :, i:i + 1], (p, LANES)), 0.0)))
        h_re = [hloc[:p, b * LANES:(b + 1) * LANES] for b in range(nb)]
        h_im = [hloc[p:, b * LANES:(b + 1) * LANES] for b in range(nb)]
        for i in range(n_steps):
            ar, ai = pw[i]
            r_sh = [pltpu.roll(v, 1 << i, 1) for v in h_re]
            i_sh = [pltpu.roll(v, 1 << i, 1) for v in h_im]
            h_re = [h_re[b] + ar * r_sh[b] - ai * i_sh[b] for b in range(nb)]
            h_im = [h_im[b] + ar * i_sh[b] + ai * r_sh[b] for b in range(nb)]
        h_re = [jnp.where(lane >= 1, pltpu.roll(v, 1, 1), 0.0) for v in h_re]
        h_im = [jnp.where(lane >= 1, pltpu.roll(v, 1, 1), 0.0) for v in h_im]
        return jnp.concatenate([jnp.concatenate(h_re, axis=1), jnp.concatenate(h_im, axis=1)],
                               axis=0).astype(BF16)

    def outputs(g, h):
        x = x_ref[g]
        y = (jnp.dot(toep_ref[g], x, preferred_element_type=F32)
             + jnp.dot(wout_ref[g], h, preferred_element_type=F32)
             + d_ref[g] * x.astype(F32))
        o_ref[:, g * S5_GROUP_CH:(g + 1) * S5_GROUP_CH, :] = (
            jax.nn.gelu(y).astype(o_ref.dtype).reshape(S5_CHUNK, S5_GROUP_CH, y.shape[1]))

    h_prev = None
    for g in range(gs):
        hloc = jnp.dot(wst_ref[g], x_ref[g], preferred_element_type=F32)
        h = scan(g, hloc)
        if h_prev is not None:
            outputs(g - 1, h_prev)
        h_prev = h
    outputs(gs - 1, h_prev)


def _s5_mix(xg, toep, wst, wout, sc_re, sc_im, dcol, *, nb, gs):
    g, tc, cols = xg.shape
    assert cols == nb * LANES, "one batch's chunks must fill exactly one 128-lane block"
    blk = lambda a: pl.BlockSpec((gs,) + a.shape[1:], lambda i: (i, 0, 0))
    return pl.pallas_call(
        functools.partial(_s5_mix_kernel, nb=nb),
        grid=(g // gs,),
        in_specs=[blk(a) for a in (xg, toep, wst, wout, sc_re, sc_im, dcol)],
        out_specs=pl.BlockSpec((S5_CHUNK, gs * S5_GROUP_CH, cols), lambda i: (0, i, 0)),
        out_shape=jax.ShapeDtypeStruct((S5_CHUNK, g * S5_GROUP_CH, cols), BF16),
        compiler_params=_cparams(("parallel",)),
        name="s5_mix",
    )(xg, toep, wst, wout, sc_re, sc_im, dcol)


def _glu_kernel(y_ref, w_ref, o_ref, *, col_chunk):
    y = y_ref[0].T
    half = o_ref.shape[-1]
    for c in range(half // col_chunk):
        a = jnp.dot(y, w_ref[:, c * col_chunk:(c + 1) * col_chunk], preferred_element_type=F32)
        g = jnp.dot(y, w_ref[:, half + c * col_chunk:half + (c + 1) * col_chunk], preferred_element_type=F32)
        o_ref[:, :, c * col_chunk:(c + 1) * col_chunk] = (
            (a * jax.nn.sigmoid(g)).astype(o_ref.dtype).reshape(o_ref.shape[:2] + (col_chunk,)))


def _glu(yt, w, *, nb, col_chunk):
    t, k, cols = yt.shape
    nc = cols // nb
    half = w.shape[1] // 2
    return pl.pallas_call(
        functools.partial(_glu_kernel, col_chunk=col_chunk),
        grid=(t,),
        in_specs=[pl.BlockSpec((1, k, cols), lambda j: (j, 0, 0)),
                  pl.BlockSpec(w.shape, lambda j: (0, 0))],
        out_specs=pl.BlockSpec((nb, nc, half), lambda j: (0, 0, j)),
        out_shape=jax.ShapeDtypeStruct((nb, nc, t * half), BF16),
        compiler_params=_cparams(("parallel",)),
        name="glu",
    )(yt, w)


def _merge_kernel(x_ref, mix_ref, xq_ref, gate_ref, k_ref, v_ref, qg_ref, w_ref, o_ref, cat_ref, *, phased):
    gate = gate_ref[0]
    sg = gate * jax.nn.sigmoid(gate)
    if phased:
        nc = gate.shape[0] // S5_CHUNK
        for s in range(S5_CHUNK):
            rows = slice(s * nc, (s + 1) * nc)
            mix = mix_ref[0, :, s * PRIMARY_WIDTH:(s + 1) * PRIMARY_WIDTH]
            cat_ref[rows, :PRIMARY_WIDTH] = mix * sg[rows, :PRIMARY_WIDTH]
    else:
        cat_ref[:, :PRIMARY_WIDTH] = mix_ref[0] * sg[:, :PRIMARY_WIDTH]
    scale = X_HEAD_DIM ** -0.5
    for h in range(X_HEADS):
        sl = slice(h * X_HEAD_DIM, (h + 1) * X_HEAD_DIM)
        q = _rms(xq_ref[0, :, sl].astype(F32), qg_ref[...]).astype(BF16)
        s = lax.dot_general(q, k_ref[0, :, sl], (((1,), (1,)), ((), ())), preferred_element_type=F32) * scale
        p = jnp.exp(s - jnp.max(s, axis=-1, keepdims=True))
        p = (p / jnp.sum(p, axis=-1, keepdims=True)).astype(BF16)
        mo = jnp.dot(p, v_ref[0, :, sl], preferred_element_type=F32)
        osl = slice(PRIMARY_WIDTH + h * X_HEAD_DIM, PRIMARY_WIDTH + (h + 1) * X_HEAD_DIM)
        cat_ref[:, osl] = mo.astype(BF16) * sg[:, osl]
    delta = jnp.dot(cat_ref[...], w_ref[...], preferred_element_type=F32)
    o_ref[0] = x_ref[0] + (_from_phase_order(delta) if phased else delta)


def _merge(x, mix, proj, xq_blk, gate_blk, mk, mv, xq_norm, w_out, *, tm, phased=False):
    b, l, d = x.shape
    m = mk.shape[1]
    mix_spec = (pl.BlockSpec((1, tm // S5_CHUNK, S5_CHUNK * PRIMARY_WIDTH), lambda i, j: (i, j, 0)) if phased
                else pl.BlockSpec((1, tm, PRIMARY_WIDTH), lambda i, j: (i, j, 0)))
    return pl.pallas_call(
        functools.partial(_merge_kernel, phased=phased),
        grid=(b, l // tm),
        in_specs=[pl.BlockSpec((1, tm, d), lambda i, j: (i, j, 0)),
                  mix_spec,
                  pl.BlockSpec((1, tm, XQ_WIDTH), lambda i, j: (i, j, xq_blk)),
                  pl.BlockSpec((1, tm, BRANCH_WIDTH), lambda i, j: (i, j, gate_blk)),
                  pl.BlockSpec((1, m, XQ_WIDTH), lambda i, j: (i, 0, 0)),
                  pl.BlockSpec((1, m, XQ_WIDTH), lambda i, j: (i, 0, 0)),
                  pl.BlockSpec((1, X_HEAD_DIM), lambda i, j: (0, 0)),
                  pl.BlockSpec((BRANCH_WIDTH, d), lambda i, j: (0, 0))],
        out_specs=pl.BlockSpec((1, tm, d), lambda i, j: (i, j, 0)),
        out_shape=jax.ShapeDtypeStruct((b, l, d), F32),
        scratch_shapes=[pltpu.VMEM((tm, BRANCH_WIDTH), BF16)],
        compiler_params=_cparams(("parallel", "parallel")),
        name="merge",
    )(x, mix, proj, proj, mk, mv, xq_norm.reshape(1, X_HEAD_DIM), w_out)


def _mla_qkv_kernel(cq_ref, ckv_ref, kr_ref, posr_ref, invfc_ref, gq_ref, gkv_ref, gqn_ref,
                    gkn_ref, gqr_ref, gkr_ref, wqt_ref, wk_ref, wvt_ref, qt_ref, kn_ref, krope_ref, vt_ref):
    half = MLA_ROPE // 2
    tm = cq_ref.shape[1]
    qscale = (MLA_NOPE + MLA_ROPE) ** -0.5 * math.log2(math.e)

    cq = _rms(cq_ref[0].astype(F32), gq_ref[...])
    ckv = _rms(ckv_ref[0].astype(F32), gkv_ref[...])
    cq_t = cq.T.astype(BF16)
    ckv_t = ckv.T.astype(BF16)
    q_t = jnp.dot(wqt_ref[...], cq_t, preferred_element_type=F32)
    v_t = jnp.dot(wvt_ref[...], ckv_t, preferred_element_type=F32)
    k_n = jnp.dot(ckv.astype(BF16), wk_ref[...], preferred_element_type=F32)

    ang_t = invfc_ref[...] * posr_ref[0].astype(F32)
    cos_t, sin_t = jnp.cos(ang_t), jnp.sin(ang_t)
    g_nope = jnp.broadcast_to(gqn_ref[...], (MLA_NOPE, tm)) * qscale
    g_r1 = jnp.broadcast_to(gqr_ref[:half, :], (half, tm)) * qscale
    g_r2 = jnp.broadcast_to(gqr_ref[half:, :], (half, tm)) * qscale
    for h in range(MLA_HEADS):
        q = q_t[h * MLA_QK_PAD:(h + 1) * MLA_QK_PAD]
        nope = q[:MLA_NOPE]
        r = lax.rsqrt(jnp.mean(nope * nope, axis=0, keepdims=True) + EPS)
        qt_ref[0, h, :MLA_NOPE, :] = (nope * r * g_nope).astype(BF16)
        x1, x2 = q[MLA_NOPE:MLA_NOPE + half], q[MLA_NOPE + half:MLA_NOPE + MLA_ROPE]
        ss = jnp.sum(x1 * x1, axis=0, keepdims=True) + jnp.sum(x2 * x2, axis=0, keepdims=True)
        r = lax.rsqrt(ss * (1.0 / MLA_ROPE) + EPS)
        x1, x2 = x1 * r * g_r1, x2 * r * g_r2
        qt_ref[0, h, MLA_NOPE:MLA_NOPE + half, :] = (x1 * cos_t - x2 * sin_t).astype(BF16)
        qt_ref[0, h, MLA_NOPE + half:MLA_NOPE + MLA_ROPE, :] = (x1 * sin_t + x2 * cos_t).astype(BF16)
        qt_ref[0, h, MLA_NOPE + MLA_ROPE:, :] = jnp.zeros((MLA_QK_PAD - MLA_NOPE - MLA_ROPE, tm), BF16)
        kn_ref[0, h] = _rms(k_n[:, h * MLA_NOPE:(h + 1) * MLA_NOPE], gkn_ref[...]).astype(BF16)
        vt_ref[0, h] = v_t[h * MLA_V:(h + 1) * MLA_V].astype(BF16)

    kr_t = kr_ref[0].astype(F32).T
    x1, x2 = kr_t[:half], kr_t[half:MLA_ROPE]
    ss = jnp.sum(x1 * x1, axis=0, keepdims=True) + jnp.sum(x2 * x2, axis=0, keepdims=True)
    r = lax.rsqrt(ss * (1.0 / MLA_ROPE) + EPS)
    x1, x2 = x1 * r * gkr_ref[:half, :], x2 * r * gkr_ref[half:, :]
    rot = jnp.concatenate([x1 * cos_t - x2 * sin_t, x1 * sin_t + x2 * cos_t,
                           jnp.zeros((LANES - MLA_ROPE, tm), F32)], axis=0)
    krope_ref[0] = rot.T.astype(BF16)


def _mla_qkv(proj, cq_blk, ckv_blk, kr_blk, positions, gains, wqt, wk, wvt, *, tm):
    b, l, _ = proj.shape
    hh = MLA_HEADS
    half = MLA_ROPE // 2
    inv_freq = ROPE_THETA ** (-jnp.arange(half, dtype=F32) / half)
    const = lambda a: pl.BlockSpec(a.shape, lambda i, j: (0,) * a.ndim)
    gq, gkv, gqn, gkn, gqr, gkr = gains
    consts = [inv_freq.reshape(half, 1), gq.reshape(1, -1), gkv.reshape(1, -1), gqn.reshape(-1, 1),
              gkn.reshape(1, -1), gqr.reshape(-1, 1), gkr.reshape(-1, 1), wqt, wk, wvt]
    return pl.pallas_call(
        _mla_qkv_kernel,
        grid=(b, l // tm),
        in_specs=[pl.BlockSpec((1, tm, MLA_Q_LORA), lambda i, j: (i, j, cq_blk)),
                  pl.BlockSpec((1, tm, MLA_KV_LORA), lambda i, j: (i, j, ckv_blk)),
                  pl.BlockSpec((1, tm, LANES), lambda i, j: (i, j, kr_blk)),
                  pl.BlockSpec((1, 1, tm), lambda i, j: (i, 0, j))] + [const(a) for a in consts],
        out_specs=[pl.BlockSpec((1, hh, MLA_QK_PAD, tm), lambda i, j: (i, 0, 0, j)),
                   pl.BlockSpec((1, hh, tm, MLA_NOPE), lambda i, j: (i, 0, j, 0)),
                   pl.BlockSpec((1, tm, LANES), lambda i, j: (i, j, 0)),
                   pl.BlockSpec((1, hh, MLA_V, tm), lambda i, j: (i, 0, 0, j))],
        out_shape=[jax.ShapeDtypeStruct((b, hh, MLA_QK_PAD, l), BF16),
                   jax.ShapeDtypeStruct((b, hh, l, MLA_NOPE), BF16),
                   jax.ShapeDtypeStruct((b, l, LANES), BF16),
                   jax.ShapeDtypeStruct((b, hh, MLA_V, l), BF16)],
        compiler_params=_cparams(("parallel", "parallel")),
        name="mla_qkv",
    )(proj, proj, proj, positions.reshape(b, 1, l), *consts)


def _flash_kernel(qt_ref, kn_ref, kr_ref, vt_ref, o_ref, m_ref, l_ref, acc_ref, *, tq, hp, ahead):
    qi = pl.program_id(2)
    m_ref[...] = jnp.full(m_ref.shape, -jnp.inf, F32)
    l_ref[...] = jnp.zeros(l_ref.shape, F32)
    acc_ref[...] = jnp.zeros(acc_ref.shape, F32)

    half = tq // 2
    lower = (lax.broadcasted_iota(jnp.int32, (half, half), 0)
             <= lax.broadcasted_iota(jnp.int32, (half, half), 1))

    def blocks(j, parts, diagonal):
        base = pl.multiple_of(j * tq, tq)
        items = [(h, pl.ds(base + k0, nk), slice(q0, q0 + nq)) for k0, nk, q0, nq in parts for h in range(hp)]

        def scores(h, rows, cols):
            k = jnp.concatenate([kn_ref[0, h, rows, :], kr_ref[0, rows, :]], axis=-1)
            return jnp.dot(k, qt_ref[0, h, :, cols], preferred_element_type=F32)

        pending = [scores(*it) for it in items[:ahead]]
        for n, (h, rows, cols) in enumerate(items):
            if n + ahead < len(items):
                pending.append(scores(*items[n + ahead]))
            s = pending[n]
            if diagonal:
                square = jnp.where(lower, s[:, :half], jnp.finfo(F32).min)
                s = square if s.shape[1] == half else jnp.concatenate([square, s[:, half:]], axis=1)
            m = m_ref[h, :, cols]
            m_new = jnp.maximum(m, jnp.max(s, axis=0, keepdims=True))
            alpha = jnp.exp2(m - m_new)
            p = jnp.exp2(s - m_new)
            l_ref[h, :, cols] = alpha * l_ref[h, :, cols] + jnp.sum(p, axis=0, keepdims=True)
            acc_ref[h, :, cols] = alpha * acc_ref[h, :, cols] + jnp.dot(
                vt_ref[0, h, :, rows], p.astype(BF16), preferred_element_type=F32)
            m_ref[h, :, cols] = m_new

    def body(j, carry):
        blocks(j, [(0, tq, 0, tq)], False)
        return carry

    lax.fori_loop(0, qi, body, 0)
    blocks(qi, [(0, half, 0, tq), (half, half, half, half)], True)
    for h in range(hp):
        o_ref[0, :, h * MLA_V:(h + 1) * MLA_V] = (acc_ref[h] / l_ref[h]).T.astype(o_ref.dtype)


def _flash(qt, kn, kr, vt, *, tq, hp, ahead):
    b, hh, _, l = qt.shape
    return pl.pallas_call(
        functools.partial(_flash_kernel, tq=tq, hp=hp, ahead=ahead),
        grid=(b, hh // hp, l // tq),
        in_specs=[pl.BlockSpec((1, hp, MLA_QK_PAD, tq), lambda i, h, j: (i, h, 0, j)),
                  pl.BlockSpec((1, hp, l, MLA_NOPE), lambda i, h, j: (i, h, 0, 0)),
                  pl.BlockSpec((1, l, LANES), lambda i, h, j: (i, 0, 0)),
                  pl.BlockSpec((1, hp, MLA_V, l), lambda i, h, j: (i, h, 0, 0))],
        out_specs=pl.BlockSpec((1, tq, hp * MLA_V), lambda i, h, j: (i, j, h)),
        out_shape=jax.ShapeDtypeStruct((b, l, hh * MLA_V), BF16),
        scratch_shapes=[pltpu.VMEM((hp, 1, tq), F32), pltpu.VMEM((hp, 1, tq), F32),
                        pltpu.VMEM((hp, MLA_V, tq), F32)],
        compiler_params=_cparams(("parallel", "parallel", "parallel")),
        name="flash",
    )(qt, kn, kr, vt)


def _s5_layer(x, mem, ln, w_in, lam_re, lam_im, log_step, b_re, b_im, c_re, c_im, d, w_glu,
              w_out, mem_norm, w_mem_kv, xq_norm, xk_norm):
    b, l, dm = x.shape
    w_gx = jnp.concatenate([w_in[:, PRIMARY_WIDTH + XQ_WIDTH:], w_in[:, PRIMARY_WIDTH:PRIMARY_WIDTH + XQ_WIDTH]],
                           axis=1).astype(BF16)
    tm = 512
    proj, xn = _s5_in_proj(x, ln, w_gx, tm=tm, col_chunk=512)
    xg = _s5_ut(xn, w_in[:, :PRIMARY_WIDTH].T.astype(BF16))
    pw_re, pw_im = _s5_pow(lam_re, lam_im, log_step)
    toep, wout, wst = _s5_asm(pw_re.transpose(1, 0, 2), pw_im.transpose(1, 0, 2), c_re, c_im, b_re, b_im)
    n_scan = int(math.log2(LANES))
    first = S5_EXPONENTS.index(S5_CHUNK)
    col = lambda pw: jnp.pad(pw[first:first + n_scan].transpose(1, 2, 0), ((0, 0), (0, 0), (0, 8 - n_scan)))
    dcol = jnp.tile(d.reshape(S5_GROUPS, 1, S5_GROUP_CH), (1, S5_CHUNK, 1)).reshape(S5_GROUPS, -1, 1)
    yt = _s5_mix(xg, toep, wst, wout, col(pw_re), col(pw_im), dcol, nb=b, gs=4)
    y = _glu(yt, w_glu.astype(BF16), nb=b, col_chunk=256)
    mk, mv = _mem_kv(mem, mem_norm, w_mem_kv, xk_norm)
    return _merge(x, y, proj, BRANCH_WIDTH // XQ_WIDTH, 0, mk, mv, xq_norm, w_out.astype(BF16), tm=tm, phased=True)


def _mla_layer(x, mem, positions, ln, w_in, q_lora_norm, kv_lora_norm, w_uq, w_ukv, q_nope_norm, k_nope_norm,
               q_rope_norm, k_rope_norm, w_out, mem_norm, w_mem_kv, xq_norm, xk_norm):
    b, l, dm = x.shape
    o1 = MLA_Q_LORA
    o2 = o1 + MLA_KV_LORA
    o3 = o2 + MLA_ROPE
    o4 = o3 + XQ_WIDTH
    w_perm = jnp.concatenate([w_in[:, o4:], w_in[:, :o1], w_in[:, o3:o4], w_in[:, o1:o2], w_in[:, o2:o3],
                              jnp.zeros((dm, 2 * LANES - MLA_ROPE), w_in.dtype)], axis=1).astype(BF16)
    proj = _norm_matmul(x.reshape(b * l, dm), ln, w_perm, tm=512, col_chunk=512, name="mla_in_proj")
    proj = proj.reshape(b, l, -1)
    gate_blk = 0
    cq_blk = BRANCH_WIDTH // MLA_Q_LORA
    xq_blk = (BRANCH_WIDTH + MLA_Q_LORA) // XQ_WIDTH
    ckv_blk = (BRANCH_WIDTH + MLA_Q_LORA + XQ_WIDTH) // MLA_KV_LORA
    kr_blk = (BRANCH_WIDTH + MLA_Q_LORA + XQ_WIDTH + MLA_KV_LORA) // LANES
    wq = w_uq.reshape(MLA_Q_LORA, MLA_HEADS, MLA_NOPE + MLA_ROPE)
    wq = jnp.pad(wq, ((0, 0), (0, 0), (0, MLA_QK_PAD - MLA_NOPE - MLA_ROPE)))
    wqt = wq.reshape(MLA_Q_LORA, MLA_HEADS * MLA_QK_PAD).T.astype(BF16)
    wkv = w_ukv.reshape(MLA_KV_LORA, MLA_HEADS, MLA_NOPE + MLA_V)
    wk = wkv[:, :, :MLA_NOPE].reshape(MLA_KV_LORA, MLA_HEADS * MLA_NOPE).astype(BF16)
    wvt = wkv[:, :, MLA_NOPE:].reshape(MLA_KV_LORA, MLA_HEADS * MLA_V).T.astype(BF16)
    qt, kn, kr, vt = _mla_qkv(proj, cq_blk, ckv_blk, kr_blk, positions,
                              (q_lora_norm, kv_lora_norm, q_nope_norm, k_nope_norm, q_rope_norm, k_rope_norm),
                              wqt, wk, wvt, tm=256)
    attn = _flash(qt, kn, kr, vt, tq=512, hp=12, ahead=2)
    mk, mv = _mem_kv(mem, mem_norm, w_mem_kv, xk_norm)
    return _merge(x, attn, proj, xq_blk, gate_blk, mk, mv, xq_norm, w_out.astype(BF16), tm=512)


def kernel(x, mem, positions, ln_gain, w_out, mem_norm, w_mem_kv, xq_norm, xk_norm,
           s5_w_in, s5_lambda_re, s5_lambda_im, s5_log_step, s5_b_re, s5_b_im, s5_c_re, s5_c_im,
           s5_d, s5_w_glu, mla_w_in, mla_q_lora_norm, mla_kv_lora_norm, mla_w_uq, mla_w_ukv,
           mla_q_nope_norm, mla_k_nope_norm, mla_q_rope_norm, mla_k_rope_norm):
    depth = ln_gain.shape[0]
    for i in range(depth):
        j = i // 2
        if i % 2 == 0:
            x = _s5_layer(x, mem, ln_gain[i], s5_w_in[j], s5_lambda_re[j], s5_lambda_im[j], s5_log_step[j],
                          s5_b_re[j], s5_b_im[j], s5_c_re[j], s5_c_im[j], s5_d[j], s5_w_glu[j],
                          w_out[i], mem_norm[i], w_mem_kv[i], xq_norm[i], xk_norm[i])
        else:
            x = _mla_layer(x, mem, positions, ln_gain[i], mla_w_in[j], mla_q_lora_norm[j], mla_kv_lora_norm[j],
                           mla_w_uq[j], mla_w_ukv[j], mla_q_nope_norm[j], mla_k_nope_norm[j],
                           mla_q_rope_norm[j], mla_k_rope_norm[j],
                           w_out[i], mem_norm[i], w_mem_kv[i], xq_norm[i], xk_norm[i])
    return x
```

```python
import functools
import math

import jax
import jax.numpy as jnp
from jax import lax
from jax.experimental import pallas as pl
from jax.experimental.pallas import tpu as pltpu

D_MODEL = 1024
BRANCH_WIDTH = 2 * D_MODEL
XQ_WIDTH = BRANCH_WIDTH // 4
PRIMARY_WIDTH = BRANCH_WIDTH - XQ_WIDTH
X_HEADS = 4
X_HEAD_DIM = XQ_WIDTH // X_HEADS
S5_GROUP_CH = 16
S5_GROUPS = PRIMARY_WIDTH // S5_GROUP_CH
S5_STATE = 64
MLA_NOPE = 128
MLA_ROPE = 64
MLA_V = 128
MLA_HEADS = PRIMARY_WIDTH // MLA_V
MLA_Q_LORA = D_MODEL // 2
MLA_KV_LORA = D_MODEL // 4
ROPE_THETA = 10000.0
EPS = 1e-6

LANES = 128
MLA_QK_PAD = 2 * LANES
S5_CHUNK = 2 * LANES // S5_GROUP_CH
S5_EXPONENTS = list(range(S5_CHUNK + 1)) + [S5_CHUNK * 2 ** i for i in range(1, int(math.log2(LANES)))]
VMEM_LIMIT = 56 * 1024 * 1024

F32 = jnp.float32
BF16 = jnp.bfloat16


def _cparams(sem):
    return pltpu.CompilerParams(dimension_semantics=sem, vmem_limit_bytes=VMEM_LIMIT)


def _rms(x, g):
    return x * lax.rsqrt(jnp.mean(x * x, axis=-1, keepdims=True) + EPS) * g


def _norm_matmul_kernel(x_ref, g_ref, w_ref, o_ref, *, col_chunk):
    xn = _rms(x_ref[...].astype(F32), g_ref[...]).astype(BF16)
    for c in range(o_ref.shape[1] // col_chunk):
        sl = slice(c * col_chunk, (c + 1) * col_chunk)
        o_ref[:, sl] = jnp.dot(xn, w_ref[:, sl], preferred_element_type=F32).astype(o_ref.dtype)


def _norm_matmul(x, g, w, *, tm, col_chunk, name):
    n, d = x.shape
    wout = w.shape[1]
    return pl.pallas_call(
        functools.partial(_norm_matmul_kernel, col_chunk=col_chunk),
        grid=(n // tm,),
        in_specs=[pl.BlockSpec((tm, d), lambda i: (i, 0)),
                  pl.BlockSpec((1, d), lambda i: (0, 0)),
                  pl.BlockSpec((d, wout), lambda i: (0, 0))],
        out_specs=pl.BlockSpec((tm, wout), lambda i: (i, 0)),
        out_shape=jax.ShapeDtypeStruct((n, wout), BF16),
        compiler_params=_cparams(("parallel",)),
        name=name,
    )(x, g.reshape(1, d), w)


def _to_phase_order(a):
    n, d = a.shape
    return jnp.swapaxes(a.reshape(n // S5_CHUNK, S5_CHUNK, d), 0, 1).reshape(n, d)


def _from_phase_order(a):
    n, d = a.shape
    return jnp.swapaxes(a.reshape(S5_CHUNK, n // S5_CHUNK, d), 0, 1).reshape(n, d)


def _s5_in_proj_kernel(x_ref, g_ref, wg_ref, wx_ref, o_ref, xn_ref, *, col_chunk):
    tm, d = x_ref.shape[1:]
    nc = tm // S5_CHUNK
    xn = _rms(_to_phase_order(x_ref[0]), g_ref[...]).astype(BF16)
    for s in range(S5_CHUNK):
        xn_ref[0, :, s * d:(s + 1) * d] = xn[s * nc:(s + 1) * nc]
    for c in range(BRANCH_WIDTH // col_chunk):
        sl = slice(c * col_chunk, (c + 1) * col_chunk)
        o_ref[0, :, sl] = jnp.dot(xn, wg_ref[:, sl].astype(BF16), preferred_element_type=F32).astype(o_ref.dtype)
    o_ref[0, :, BRANCH_WIDTH:] = jnp.dot(xn, wx_ref[...].astype(BF16),
                                         preferred_element_type=F32).astype(o_ref.dtype)


def _s5_in_proj(x, g, w_in, *, tm, col_chunk):
    b, l, d = x.shape
    wout = BRANCH_WIDTH + XQ_WIDTH
    nc = tm // S5_CHUNK
    return pl.pallas_call(
        functools.partial(_s5_in_proj_kernel, col_chunk=col_chunk),
        grid=(b, l // tm),
        in_specs=[pl.BlockSpec((1, tm, d), lambda i, j: (i, j, 0)),
                  pl.BlockSpec((1, d), lambda i, j: (0, 0)),
                  pl.BlockSpec((d, BRANCH_WIDTH), lambda i, j: (0, (PRIMARY_WIDTH + XQ_WIDTH) // BRANCH_WIDTH)),
                  pl.BlockSpec((d, XQ_WIDTH), lambda i, j: (0, PRIMARY_WIDTH // XQ_WIDTH))],
        out_specs=[pl.BlockSpec((1, tm, wout), lambda i, j: (i, j, 0)),
                   pl.BlockSpec((1, nc, S5_CHUNK * d), lambda i, j: (i, j, 0))],
        out_shape=[jax.ShapeDtypeStruct((b, l, wout), BF16),
                   jax.ShapeDtypeStruct((b, l // S5_CHUNK, S5_CHUNK * d), BF16)],
        compiler_params=_cparams(("parallel", "parallel")),
        name="s5_in_proj",
    )(x, g.reshape(1, d), w_in, w_in)


def _mem_kv_kernel(m_ref, g_ref, w_ref, kg_ref, k_ref, v_ref):
    mn = _rms(m_ref[0], g_ref[0]).astype(BF16)
    kv = jnp.dot(mn, w_ref[0].astype(BF16), preferred_element_type=F32)
    for h in range(X_HEADS):
        sl = slice(h * X_HEAD_DIM, (h + 1) * X_HEAD_DIM)
        k_ref[0, 0, :, sl] = _rms(kv[:, sl], kg_ref[0]).astype(BF16)
    v_ref[0, 0] = kv[:, XQ_WIDTH:].astype(BF16)


def _mem_kv(mem, mem_norm, w_mem_kv, xk_norm):
    b, m, d = mem.shape
    depth = w_mem_kv.shape[0]
    out = jax.ShapeDtypeStruct((depth, b, m, XQ_WIDTH), BF16)
    return pl.pallas_call(
        _mem_kv_kernel,
        grid=(depth, b),
        in_specs=[pl.BlockSpec((1, m, d), lambda n, i: (i, 0, 0)),
                  pl.BlockSpec((1, 1, d), lambda n, i: (n, 0, 0)),
                  pl.BlockSpec((1, d, 2 * XQ_WIDTH), lambda n, i: (n, 0, 0)),
                  pl.BlockSpec((1, 1, X_HEAD_DIM), lambda n, i: (n, 0, 0))],
        out_specs=[pl.BlockSpec((1, 1, m, XQ_WIDTH), lambda n, i: (n, i, 0, 0)),
                   pl.BlockSpec((1, 1, m, XQ_WIDTH), lambda n, i: (n, i, 0, 0))],
        out_shape=[out, out],
        compiler_params=_cparams(("parallel", "parallel")),
        name="mem_kv",
    )(mem, mem_norm.reshape(depth, 1, d), w_mem_kv, xk_norm.reshape(depth, 1, X_HEAD_DIM))


def _s5_pow_kernel(lr_ref, li_ref, ls_ref, pr_ref, pi_ref):
    lr, li = lr_ref[...], li_ref[...]
    step = jnp.exp(ls_ref[...])
    zr, zi = lr * step, li * step
    for n, e in enumerate(S5_EXPONENTS):
        mag = jnp.exp(zr * e)
        pr_ref[n] = mag * jnp.cos(zi * e)
        pi_ref[n] = mag * jnp.sin(zi * e)
    ar, ai = pr_ref[1], pi_ref[1]
    den = lr * lr + li * li
    pr_ref[len(S5_EXPONENTS)] = ((ar - 1.0) * lr + ai * li) / den
    pi_ref[len(S5_EXPONENTS)] = (ai * lr - (ar - 1.0) * li) / den


def _s5_pow(lam_re, lam_im, log_step):
    g, p = lam_re.shape
    out = jax.ShapeDtypeStruct((len(S5_EXPONENTS) + 1, g, p), F32)
    return pl.pallas_call(_s5_pow_kernel, out_shape=[out, out], name="s5_pow")(
        lam_re, lam_im, log_step.reshape(g, 1))


def _s5_asm_kernel(pr_ref, pi_ref, cr_ref, ci_ref, btr_ref, bti_ref, br_ref, bi_ref,
                   toep_ref, wout_ref, wst_ref):
    def group(i, carry):
        _s5_asm_group(i, pr_ref, pi_ref, cr_ref, ci_ref, btr_ref, bti_ref, br_ref, bi_ref,
                      toep_ref, wout_ref, wst_ref)
        return carry

    lax.fori_loop(0, pr_ref.shape[0], group, 0)


def _dot_3pass(a, b):
    a_hi, b_hi = a.astype(BF16), b.astype(BF16)
    a_lo = (a - a_hi.astype(F32)).astype(BF16)
    b_lo = (b - b_hi.astype(F32)).astype(BF16)
    dot = functools.partial(jnp.dot, preferred_element_type=F32)
    return dot(a_hi, b_hi) + (dot(a_hi, b_lo) + dot(a_lo, b_hi))


def _s5_asm_group(i, pr_ref, pi_ref, cr_ref, ci_ref, btr_ref, bti_ref, br_ref, bi_ref,
                  toep_ref, wout_ref, wst_ref):
    t = S5_CHUNK
    pr, pi = pr_ref[i], pi_ref[i]
    cr, ci = cr_ref[i], ci_ref[i]
    btr, bti = btr_ref[i], bti_ref[i]
    mr, mi = pr[len(S5_EXPONENTS):], pi[len(S5_EXPONENTS):]
    amr = pr[:t] * mr - pi[:t] * mi
    ami = pr[:t] * mi + pi[:t] * mr
    l_re, l_im, w_re, w_im, o_re, o_im = [], [], [], [], [], []
    for k in range(t):
        ar, ai = amr[k:k + 1], ami[k:k + 1]
        l_re.append(cr * ar - ci * ai)
        l_im.append(cr * ai + ci * ar)
        ar, ai = amr[t - 1 - k:t - k], ami[t - 1 - k:t - k]
        w_re.append(btr * ar - bti * ai)
        w_im.append(btr * ai + bti * ar)
        ar, ai = pr[k + 1:k + 2], pi[k + 1:k + 2]
        o_re.append(cr * ar - ci * ai)
        o_im.append(-(cr * ai + ci * ar))
    cat = lambda parts: jnp.concatenate(parts, axis=0)
    kt = _dot_3pass(cat(l_re), br_ref[i]) - _dot_3pass(cat(l_im), bi_ref[i])
    n = kt.shape[0]
    blk = lax.shift_right_logical(lax.broadcasted_iota(jnp.int32, kt.shape, 1), int(math.log2(S5_GROUP_CH)))
    toep = jnp.where(blk == 0, kt, 0.0)
    for s in range(1, t):
        shifted = jnp.concatenate([jnp.zeros((s * S5_GROUP_CH, n), F32), kt[:n - s * S5_GROUP_CH]], axis=0)
        toep = jnp.where(blk == s, shifted, toep)
    toep_ref[i] = toep.astype(BF16)
    wout_ref[i] = jnp.concatenate([cat(o_re), cat(o_im)], axis=1).astype(BF16)
    wst_ref[i] = jnp.concatenate([cat(w_re), cat(w_im)], axis=1).T.astype(BF16)


def _s5_asm(pw_re, pw_im, c_re, c_im, b_re, b_im):
    g, c, p = c_re.shape
    tc = S5_CHUNK * c
    gs = 8
    blk = lambda a: pl.BlockSpec((gs,) + a.shape[1:], lambda i: (i, 0, 0))
    bt_re, bt_im = b_re.transpose(0, 2, 1), b_im.transpose(0, 2, 1)
    tile = lambda a: jnp.tile(a, (1, 1, S5_CHUNK))
    args = (pw_re, pw_im, c_re, c_im, bt_re, bt_im, tile(b_re), tile(b_im))
    return pl.pallas_call(
        _s5_asm_kernel,
        grid=(g // gs,),
        in_specs=[blk(a) for a in args],
        out_specs=[pl.BlockSpec((gs, tc, tc), lambda i: (i, 0, 0)),
                   pl.BlockSpec((gs, tc, 2 * p), lambda i: (i, 0, 0)),
                   pl.BlockSpec((gs, 2 * p, tc), lambda i: (i, 0, 0))],
        out_shape=[jax.ShapeDtypeStruct((g, tc, tc), BF16), jax.ShapeDtypeStruct((g, tc, 2 * p), BF16),
                   jax.ShapeDtypeStruct((g, 2 * p, tc), BF16)],
        compiler_params=_cparams(("parallel",)),
        name="s5_asm",
    )(*args)


def _s5_ut_kernel(xn_ref, w_ref, o_ref):
    nb, nc, d = xn_ref.shape
    ut = lax.dot_general(w_ref[...], xn_ref[...].reshape(nb * nc, d), (((1,), (1,)), ((), ())),
                         preferred_element_type=F32)
    o_ref[...] = ut.astype(BF16).reshape(o_ref.shape)


def _s5_ut(xn, wut):
    b, nc, td = xn.shape
    d = td // S5_CHUNK
    return pl.pallas_call(
        _s5_ut_kernel,
        grid=(S5_CHUNK,),
        in_specs=[pl.BlockSpec((b, nc, d), lambda s: (0, 0, s)),
                  pl.BlockSpec(wut.shape, lambda s: (0, 0))],
        out_specs=pl.BlockSpec((S5_GROUPS, S5_GROUP_CH, b * nc), lambda s: (0, s, 0)),
        out_shape=jax.ShapeDtypeStruct((S5_GROUPS, S5_CHUNK * S5_GROUP_CH, b * nc), BF16),
        compiler_params=_cparams(("parallel",)),
        name="s5_ut",
    )(xn, wut)


def _s5_mix_kernel(x_ref, toep_ref, wst_ref, wout_ref, sr_ref, si_ref, d_ref, o_ref, *, nb):
    p = S5_STATE
    gs = x_ref.shape[0]
    lane = lax.broadcasted_iota(jnp.int32, (p, LANES), 1)
    n_steps = int(math.log2(LANES))

    def scan(g, hloc):
        pw = []
        for i in range(n_steps):
            keep = lane >= (1 << i)
            pw.append((jnp.where(keep, jnp.broadcast_to(sr_ref[g, :, i:i + 1], (p, LANES)), 0.0),
                       jnp.where(keep, jnp.broadcast_to(si_ref[g, :, i:i + 1], (p, LANES)), 0.0)))
        h_re = [hloc[:p, b * LANES:(b + 1) * LANES] for b in range(nb)]
        h_im = [hloc[p:, b * LANES:(b + 1) * LANES] for b in range(nb)]
        for i in range(n_steps):
            ar, ai = pw[i]
            r_sh = [pltpu.roll(v, 1 << i, 1) for v in h_re]
            i_sh = [pltpu.roll(v, 1 << i, 1) for v in h_im]
            h_re = [h_re[b] + ar * r_sh[b] - ai * i_sh[b] for b in range(nb)]
            h_im = [h_im[b] + ar * i_sh[b] + ai * r_sh[b] for b in range(nb)]
        h_re = [jnp.where(lane >= 1, pltpu.roll(v, 1, 1), 0.0) for v in h_re]
        h_im = [jnp.where(lane >= 1, pltpu.roll(v, 1, 1), 0.0) for v in h_im]
        return jnp.concatenate([jnp.concatenate(h_re, axis=1), jnp.concatenate(h_im, axis=1)],
                               axis=0).astype(BF16)

    def outputs(g, h):
        x = x_ref[g]
        y = (jnp.dot(toep_ref[g], x, preferred_element_type=F32)
             + jnp.dot(wout_ref[g], h, preferred_element_type=F32)
             + d_ref[g] * x.astype(F32))
        o_ref[:, g * S5_GROUP_CH:(g + 1) * S5_GROUP_CH, :] = (
            jax.nn.gelu(y).astype(o_ref.dtype).reshape(S5_CHUNK, S5_GROUP_CH, y.shape[1]))

    h_prev = None
    for g in range(gs):
        hloc = jnp.dot(wst_ref[g], x_ref[g], preferred_element_type=F32)
        h = scan(g, hloc)
        if h_prev is not None:
            outputs(g - 1, h_prev)
        h_prev = h
    outputs(gs - 1, h_prev)


def _s5_mix(xg, toep, wst, wout, sc_re, sc_im, dcol, *, nb, gs):
    g, tc, cols = xg.shape
    assert cols == nb * LANES, "one batch's chunks must fill exactly one 128-lane block"
    blk = lambda a: pl.BlockSpec((gs,) + a.shape[1:], lambda i: (i, 0, 0))
    return pl.pallas_call(
        functools.partial(_s5_mix_kernel, nb=nb),
        grid=(g // gs,),
        in_specs=[blk(a) for a in (xg, toep, wst, wout, sc_re, sc_im, dcol)],
        out_specs=pl.BlockSpec((S5_CHUNK, gs * S5_GROUP_CH, cols), lambda i: (0, i, 0)),
        out_shape=jax.ShapeDtypeStruct((S5_CHUNK, g * S5_GROUP_CH, cols), BF16),
        compiler_params=_cparams(("parallel",)),
        name="s5_mix",
    )(xg, toep, wst, wout, sc_re, sc_im, dcol)


def _glu_kernel(y_ref, w_ref, o_ref, *, col_chunk):
    y = y_ref[0].T
    half = o_ref.shape[-1]
    for c in range(half // col_chunk):
        a = jnp.dot(y, w_ref[:, c * col_chunk:(c + 1) * col_chunk], preferred_element_type=F32)
        g = jnp.dot(y, w_ref[:, half + c * col_chunk:half + (c + 1) * col_chunk], preferred_element_type=F32)
        o_ref[:, :, c * col_chunk:(c + 1) * col_chunk] = (
            (a * jax.nn.sigmoid(g)).astype(o_ref.dtype).reshape(o_ref.shape[:2] + (col_chunk,)))


def _glu(yt, w, *, nb, col_chunk):
    t, k, cols = yt.shape
    nc = cols // nb
    half = w.shape[1] // 2
    return pl.pallas_call(
        functools.partial(_glu_kernel, col_chunk=col_chunk),
        grid=(t,),
        in_specs=[pl.BlockSpec((1, k, cols), lambda j: (j, 0, 0)),
                  pl.BlockSpec(w.shape, lambda j: (0, 0))],
        out_specs=pl.BlockSpec((nb, nc, half), lambda j: (0, 0, j)),
        out_shape=jax.ShapeDtypeStruct((nb, nc, t * half), BF16),
        compiler_params=_cparams(("parallel",)),
        name="glu",
    )(yt, w)


def _merge_kernel(x_ref, mix_ref, xq_ref, gate_ref, k_ref, v_ref, qg_ref, w_ref, o_ref, cat_ref, *, phased):
    gate = gate_ref[0]
    sg = gate * jax.nn.sigmoid(gate)
    if phased:
        nc = gate.shape[0] // S5_CHUNK
        for s in range(S5_CHUNK):
            rows = slice(s * nc, (s + 1) * nc)
            mix = mix_ref[0, :, s * PRIMARY_WIDTH:(s + 1) * PRIMARY_WIDTH]
            cat_ref[rows, :PRIMARY_WIDTH] = mix * sg[rows, :PRIMARY_WIDTH]
    else:
        cat_ref[:, :PRIMARY_WIDTH] = mix_ref[0] * sg[:, :PRIMARY_WIDTH]
    scale = X_HEAD_DIM ** -0.5
    for h in range(X_HEADS):
        sl = slice(h * X_HEAD_DIM, (h + 1) * X_HEAD_DIM)
        q = _rms(xq_ref[0, :, sl].astype(F32), qg_ref[...]).astype(BF16)
        s = lax.dot_general(q, k_ref[0, :, sl], (((1,), (1,)), ((), ())), preferred_element_type=F32) * scale
        p = jnp.exp(s - jnp.max(s, axis=-1, keepdims=True))
        p = (p / jnp.sum(p, axis=-1, keepdims=True)).astype(BF16)
        mo = jnp.dot(p, v_ref[0, :, sl], preferred_element_type=F32)
        osl = slice(PRIMARY_WIDTH + h * X_HEAD_DIM, PRIMARY_WIDTH + (h + 1) * X_HEAD_DIM)
        cat_ref[:, osl] = mo.astype(BF16) * sg[:, osl]
    delta = jnp.dot(cat_ref[...], w_ref[...].astype(BF16), preferred_element_type=F32)
    o_ref[0] = x_ref[0] + (_from_phase_order(delta) if phased else delta)


def _merge(x, mix, proj, xq_blk, gate_blk, mk, mv, xq_norm, w_out, layer, *, tm, phased=False):
    b, l, d = x.shape
    m = mk.shape[2]
    mix_spec = (pl.BlockSpec((1, tm // S5_CHUNK, S5_CHUNK * PRIMARY_WIDTH), lambda i, j: (i, j, 0)) if phased
                else pl.BlockSpec((1, tm, PRIMARY_WIDTH), lambda i, j: (i, j, 0)))
    return pl.pallas_call(
        functools.partial(_merge_kernel, phased=phased),
        grid=(b, l // tm),
        in_specs=[pl.BlockSpec((1, tm, d), lambda i, j: (i, j, 0)),
                  mix_spec,
                  pl.BlockSpec((1, tm, XQ_WIDTH), lambda i, j: (i, j, xq_blk)),
                  pl.BlockSpec((1, tm, BRANCH_WIDTH), lambda i, j: (i, j, gate_blk)),
                  pl.BlockSpec((None, 1, m, XQ_WIDTH), lambda i, j: (layer, i, 0, 0)),
                  pl.BlockSpec((None, 1, m, XQ_WIDTH), lambda i, j: (layer, i, 0, 0)),
                  pl.BlockSpec((None, 1, X_HEAD_DIM), lambda i, j: (layer, 0, 0)),
                  pl.BlockSpec((None, BRANCH_WIDTH, d), lambda i, j: (layer, 0, 0))],
        out_specs=pl.BlockSpec((1, tm, d), lambda i, j: (i, j, 0)),
        out_shape=jax.ShapeDtypeStruct((b, l, d), F32),
        scratch_shapes=[pltpu.VMEM((tm, BRANCH_WIDTH), BF16)],
        compiler_params=_cparams(("parallel", "parallel")),
        name="merge",
    )(x, mix, proj, proj, mk, mv, xq_norm.reshape(-1, 1, X_HEAD_DIM), w_out)


def _mla_qkv_kernel(cq_ref, ckv_ref, kr_ref, posr_ref, invfc_ref, gq_ref, gkv_ref, gqn_ref,
                    gkn_ref, gqr_ref, gkr_ref, wqt_ref, wk_ref, wvt_ref, qt_ref, kn_ref, krope_ref, vt_ref):
    half = MLA_ROPE // 2
    tm = cq_ref.shape[1]
    qscale = (MLA_NOPE + MLA_ROPE) ** -0.5 * math.log2(math.e)

    cq = _rms(cq_ref[0].astype(F32), gq_ref[...])
    ckv = _rms(ckv_ref[0].astype(F32), gkv_ref[...])
    cq_t = cq.T.astype(BF16)
    ckv_t = ckv.T.astype(BF16)
    ckv_b = ckv.astype(BF16)

    def project(h):
        dot = functools.partial(jnp.dot, preferred_element_type=F32)
        k_pair = dot(ckv_b, wk_ref[:, h * MLA_NOPE:(h + 2) * MLA_NOPE]) if h % 2 == 0 else None
        return (dot(wqt_ref[h * MLA_QK_PAD:h * MLA_QK_PAD + MLA_NOPE + MLA_ROPE, :], cq_t),
                dot(wvt_ref[h * MLA_V:(h + 1) * MLA_V, :], ckv_t), k_pair)

    ang_t = invfc_ref[...] * posr_ref[0].astype(F32)
    cos_t, sin_t = jnp.cos(ang_t), jnp.sin(ang_t)
    g_nope = jnp.broadcast_to(gqn_ref[...], (MLA_NOPE, tm)) * qscale
    g_r1 = jnp.broadcast_to(gqr_ref[:half, :], (half, tm)) * qscale
    g_r2 = jnp.broadcast_to(gqr_ref[half:, :], (half, tm)) * qscale
    ahead = 2
    pending = [project(h) for h in range(ahead)]
    for h in range(MLA_HEADS):
        if h + ahead < MLA_HEADS:
            pending.append(project(h + ahead))
        q, v_t, _ = pending[h]
        k_n = pending[h - h % 2][2][:, (h % 2) * MLA_NOPE:(h % 2 + 1) * MLA_NOPE]
        nope = q[:MLA_NOPE]
        r = lax.rsqrt(jnp.mean(nope * nope, axis=0, keepdims=True) + EPS)
        qt_ref[0, h, :MLA_NOPE, :] = (nope * r * g_nope).astype(BF16)
        x1, x2 = q[MLA_NOPE:MLA_NOPE + half], q[MLA_NOPE + half:MLA_NOPE + MLA_ROPE]
        ss = jnp.sum(x1 * x1, axis=0, keepdims=True) + jnp.sum(x2 * x2, axis=0, keepdims=True)
        r = lax.rsqrt(ss * (1.0 / MLA_ROPE) + EPS)
        x1, x2 = x1 * r * g_r1, x2 * r * g_r2
        qt_ref[0, h, MLA_NOPE:MLA_NOPE + half, :] = (x1 * cos_t - x2 * sin_t).astype(BF16)
        qt_ref[0, h, MLA_NOPE + half:MLA_NOPE + MLA_ROPE, :] = (x1 * sin_t + x2 * cos_t).astype(BF16)
        qt_ref[0, h, MLA_NOPE + MLA_ROPE:, :] = jnp.zeros((MLA_QK_PAD - MLA_NOPE - MLA_ROPE, tm), BF16)
        kn_ref[0, h] = _rms(k_n, gkn_ref[...]).astype(BF16)
        vt_ref[0, h] = v_t.astype(BF16)

    kr_t = kr_ref[0].astype(F32).T
    x1, x2 = kr_t[:half], kr_t[half:MLA_ROPE]
    ss = jnp.sum(x1 * x1, axis=0, keepdims=True) + jnp.sum(x2 * x2, axis=0, keepdims=True)
    r = lax.rsqrt(ss * (1.0 / MLA_ROPE) + EPS)
    x1, x2 = x1 * r * gkr_ref[:half, :], x2 * r * gkr_ref[half:, :]
    rot = jnp.concatenate([x1 * cos_t - x2 * sin_t, x1 * sin_t + x2 * cos_t,
                           jnp.zeros((LANES - MLA_ROPE, tm), F32)], axis=0)
    krope_ref[0] = rot.T.astype(BF16)


def _mla_qkv(proj, cq_blk, ckv_blk, kr_blk, positions, gains, wqt, wk, wvt, *, tm):
    b, l, _ = proj.shape
    hh = MLA_HEADS
    half = MLA_ROPE // 2
    inv_freq = ROPE_THETA ** (-jnp.arange(half, dtype=F32) / half)
    const = lambda a: pl.BlockSpec(a.shape, lambda i, j: (0,) * a.ndim)
    gq, gkv, gqn, gkn, gqr, gkr = gains
    consts = [inv_freq.reshape(half, 1), gq.reshape(1, -1), gkv.reshape(1, -1), gqn.reshape(-1, 1),
              gkn.reshape(1, -1), gqr.reshape(-1, 1), gkr.reshape(-1, 1), wqt, wk, wvt]
    return pl.pallas_call(
        _mla_qkv_kernel,
        grid=(b, l // tm),
        in_specs=[pl.BlockSpec((1, tm, MLA_Q_LORA), lambda i, j: (i, j, cq_blk)),
                  pl.BlockSpec((1, tm, MLA_KV_LORA), lambda i, j: (i, j, ckv_blk)),
                  pl.BlockSpec((1, tm, LANES), lambda i, j: (i, j, kr_blk)),
                  pl.BlockSpec((1, 1, tm), lambda i, j: (i, 0, j))] + [const(a) for a in consts],
        out_specs=[pl.BlockSpec((1, hh, MLA_QK_PAD, tm), lambda i, j: (i, 0, 0, j)),
                   pl.BlockSpec((1, hh, tm, MLA_NOPE), lambda i, j: (i, 0, j, 0)),
                   pl.BlockSpec((1, tm, LANES), lambda i, j: (i, j, 0)),
                   pl.BlockSpec((1, hh, MLA_V, tm), lambda i, j: (i, 0, 0, j))],
        out_shape=[jax.ShapeDtypeStruct((b, hh, MLA_QK_PAD, l), BF16),
                   jax.ShapeDtypeStruct((b, hh, l, MLA_NOPE), BF16),
                   jax.ShapeDtypeStruct((b, l, LANES), BF16),
                   jax.ShapeDtypeStruct((b, hh, MLA_V, l), BF16)],
        compiler_params=_cparams(("parallel", "parallel")),
        name="mla_qkv",
    )(proj, proj, proj, positions.reshape(b, 1, l), *consts)


def _flash_kernel(qt_ref, kn_ref, kr_ref, vt_ref, o_ref, m_ref, l_ref, acc_ref, *, tq, hp, ahead):
    qi = pl.program_id(2)
    m_ref[...] = jnp.full(m_ref.shape, -jnp.inf, F32)
    l_ref[...] = jnp.zeros(l_ref.shape, F32)
    acc_ref[...] = jnp.zeros(acc_ref.shape, F32)

    half = tq // 2
    lower = (lax.broadcasted_iota(jnp.int32, (half, half), 0)
             <= lax.broadcasted_iota(jnp.int32, (half, half), 1))

    def blocks(j, parts, diagonal):
        base = pl.multiple_of(j * tq, tq)
        items = [(h, pl.ds(base + k0, nk), slice(q0, q0 + nq)) for k0, nk, q0, nq in parts for h in range(hp)]

        def scores(h, rows, cols):
            k = jnp.concatenate([kn_ref[0, h, rows, :], kr_ref[0, rows, :]], axis=-1)
            return jnp.dot(k, qt_ref[0, h, :, cols], preferred_element_type=F32)

        pending = [scores(*it) for it in items[:ahead]]
        for n, (h, rows, cols) in enumerate(items):
            if n + ahead < len(items):
                pending.append(scores(*items[n + ahead]))
            s = pending[n]
            if diagonal:
                square = jnp.where(lower, s[:, :half], jnp.finfo(F32).min)
                s = square if s.shape[1] == half else jnp.concatenate([square, s[:, half:]], axis=1)
            m = m_ref[h, :, cols]
            m_new = jnp.maximum(m, jnp.max(s, axis=0, keepdims=True))
            alpha = jnp.exp2(m - m_new)
            p = jnp.exp2(s - m_new)
            l_ref[h, :, cols] = alpha * l_ref[h, :, cols] + jnp.sum(p, axis=0, keepdims=True)
            acc_ref[h, :, cols] = alpha * acc_ref[h, :, cols] + jnp.dot(
                vt_ref[0, h, :, rows], p.astype(BF16), preferred_element_type=F32)
            m_ref[h, :, cols] = m_new

    def body(j, carry):
        blocks(j, [(0, tq, 0, tq)], False)
        return carry

    lax.fori_loop(0, qi, body, 0)
    blocks(qi, [(0, half, 0, tq), (half, half, half, half)], True)
    for h in range(hp):
        o_ref[0, :, h * MLA_V:(h + 1) * MLA_V] = (acc_ref[h] / l_ref[h]).T.astype(o_ref.dtype)


def _flash(qt, kn, kr, vt, *, tq, hp, ahead):
    b, hh, _, l = qt.shape
    return pl.pallas_call(
        functools.partial(_flash_kernel, tq=tq, hp=hp, ahead=ahead),
        grid=(b, hh // hp, l // tq),
        in_specs=[pl.BlockSpec((1, hp, MLA_QK_PAD, tq), lambda i, h, j: (i, h, 0, j)),
                  pl.BlockSpec((1, hp, l, MLA_NOPE), lambda i, h, j: (i, h, 0, 0)),
                  pl.BlockSpec((1, l, LANES), lambda i, h, j: (i, 0, 0)),
                  pl.BlockSpec((1, hp, MLA_V, l), lambda i, h, j: (i, h, 0, 0))],
        out_specs=pl.BlockSpec((1, tq, hp * MLA_V), lambda i, h, j: (i, j, h)),
        out_shape=jax.ShapeDtypeStruct((b, l, hh * MLA_V), BF16),
        scratch_shapes=[pltpu.VMEM((hp, 1, tq), F32), pltpu.VMEM((hp, 1, tq), F32),
                        pltpu.VMEM((hp, MLA_V, tq), F32)],
        compiler_params=_cparams(("parallel", "parallel", "parallel")),
        name="flash",
    )(qt, kn, kr, vt)


def _s5_layer(x, ln, w_in, lam_re, lam_im, log_step, b_re, b_im, c_re, c_im, d, w_glu,
              w_out, mem_kv, xq_norm, layer):
    b, l, dm = x.shape
    tm = 512
    proj, xn = _s5_in_proj(x, ln, w_in, tm=tm, col_chunk=512)
    xg = _s5_ut(xn, w_in[:, :PRIMARY_WIDTH].T.astype(BF16))
    pw_re, pw_im = _s5_pow(lam_re, lam_im, log_step)
    toep, wout, wst = _s5_asm(pw_re.transpose(1, 0, 2), pw_im.transpose(1, 0, 2), c_re, c_im, b_re, b_im)
    n_scan = int(math.log2(LANES))
    first = S5_EXPONENTS.index(S5_CHUNK)
    col = lambda pw: jnp.pad(pw[first:first + n_scan].transpose(1, 2, 0), ((0, 0), (0, 0), (0, 8 - n_scan)))
    dcol = jnp.tile(d.reshape(S5_GROUPS, 1, S5_GROUP_CH), (1, S5_CHUNK, 1)).reshape(S5_GROUPS, -1, 1)
    yt = _s5_mix(xg, toep, wst, wout, col(pw_re), col(pw_im), dcol, nb=b, gs=4)
    y = _glu(yt, w_glu.astype(BF16), nb=b, col_chunk=256)
    return _merge(x, y, proj, BRANCH_WIDTH // XQ_WIDTH, 0, *mem_kv, xq_norm, w_out, layer, tm=tm, phased=True)


def _mla_layer(x, positions, ln, w_in, q_lora_norm, kv_lora_norm, w_uq, w_ukv, q_nope_norm, k_nope_norm,
               q_rope_norm, k_rope_norm, w_out, mem_kv, xq_norm, layer):
    b, l, dm = x.shape
    o1 = MLA_Q_LORA
    o2 = o1 + MLA_KV_LORA
    o3 = o2 + MLA_ROPE
    o4 = o3 + XQ_WIDTH
    w_perm = jnp.concatenate([w_in[:, o4:], w_in[:, :o1], w_in[:, o3:o4], w_in[:, o1:o2], w_in[:, o2:o3],
                              jnp.zeros((dm, 2 * LANES - MLA_ROPE), w_in.dtype)], axis=1).astype(BF16)
    proj = _norm_matmul(x.reshape(b * l, dm), ln, w_perm, tm=512, col_chunk=512, name="mla_in_proj")
    proj = proj.reshape(b, l, -1)
    gate_blk = 0
    cq_blk = BRANCH_WIDTH // MLA_Q_LORA
    xq_blk = (BRANCH_WIDTH + MLA_Q_LORA) // XQ_WIDTH
    ckv_blk = (BRANCH_WIDTH + MLA_Q_LORA + XQ_WIDTH) // MLA_KV_LORA
    kr_blk = (BRANCH_WIDTH + MLA_Q_LORA + XQ_WIDTH + MLA_KV_LORA) // LANES
    wq = w_uq.reshape(MLA_Q_LORA, MLA_HEADS, MLA_NOPE + MLA_ROPE)
    wq = jnp.pad(wq, ((0, 0), (0, 0), (0, MLA_QK_PAD - MLA_NOPE - MLA_ROPE)))
    wqt = wq.reshape(MLA_Q_LORA, MLA_HEADS * MLA_QK_PAD).T.astype(BF16)
    wkv = w_ukv.reshape(MLA_KV_LORA, MLA_HEADS, MLA_NOPE + MLA_V)
    wk = wkv[:, :, :MLA_NOPE].reshape(MLA_KV_LORA, MLA_HEADS * MLA_NOPE).astype(BF16)
    wvt = wkv[:, :, MLA_NOPE:].reshape(MLA_KV_LORA, MLA_HEADS * MLA_V).T.astype(BF16)
    qt, kn, kr, vt = _mla_qkv(proj, cq_blk, ckv_blk, kr_blk, positions,
                              (q_lora_norm, kv_lora_norm, q_nope_norm, k_nope_norm, q_rope_norm, k_rope_norm),
                              wqt, wk, wvt, tm=256)
    attn = _flash(qt, kn, kr, vt, tq=512, hp=12, ahead=2)
    return _merge(x, attn, proj, xq_blk, gate_blk, *mem_kv, xq_norm, w_out, layer, tm=512)


def kernel(x, mem, positions, ln_gain, w_out, mem_norm, w_mem_kv, xq_norm, xk_norm,
           s5_w_in, s5_lambda_re, s5_lambda_im, s5_log_step, s5_b_re, s5_b_im, s5_c_re, s5_c_im,
           s5_d, s5_w_glu, mla_w_in, mla_q_lora_norm, mla_kv_lora_norm, mla_w_uq, mla_w_ukv,
           mla_q_nope_norm, mla_k_nope_norm, mla_q_rope_norm, mla_k_rope_norm):
    depth = ln_gain.shape[0]
    mem_kv = _mem_kv(mem, mem_norm, w_mem_kv, xk_norm)
    for i in range(depth):
        j = i // 2
        if i % 2 == 0:
            x = _s5_layer(x, ln_gain[i], s5_w_in[j], s5_lambda_re[j], s5_lambda_im[j], s5_log_step[j],
                          s5_b_re[j], s5_b_im[j], s5_c_re[j], s5_c_im[j], s5_d[j], s5_w_glu[j],
                          w_out, mem_kv, xq_norm, i)
        else:
            x = _mla_layer(x, positions, ln_gain[i], mla_w_in[j], mla_q_lora_norm[j], mla_kv_lora_norm[j],
                           mla_w_uq[j], mla_w_ukv[j], mla_q_nope_norm[j], mla_k_nope_norm[j],
                           mla_q_rope_norm[j], mla_k_rope_norm[j],
                           w_out, mem_kv, xq_norm, i)
    return x
```

```python
import functools
import math

import jax
import jax.numpy as jnp
from jax import lax
from jax.experimental import pallas as pl
from jax.experimental.pallas import tpu as pltpu

D_MODEL = 1024
BRANCH_WIDTH = 2 * D_MODEL
XQ_WIDTH = BRANCH_WIDTH // 4
PRIMARY_WIDTH = BRANCH_WIDTH - XQ_WIDTH
X_HEADS = 4
X_HEAD_DIM = XQ_WIDTH // X_HEADS
S5_GROUP_CH = 16
S5_GROUPS = PRIMARY_WIDTH // S5_GROUP_CH
S5_STATE = 64
MLA_NOPE = 128
MLA_ROPE = 64
MLA_V = 128
MLA_HEADS = PRIMARY_WIDTH // MLA_V
MLA_Q_LORA = D_MODEL // 2
MLA_KV_LORA = D_MODEL // 4
ROPE_THETA = 10000.0
EPS = 1e-6

LANES = 128
MLA_QK_PAD = 2 * LANES
S5_CHUNK = 2 * LANES // S5_GROUP_CH
S5_EXPONENTS = list(range(S5_CHUNK + 1)) + [S5_CHUNK * 2 ** i for i in range(1, int(math.log2(LANES)))]
VMEM_LIMIT = 56 * 1024 * 1024

F32 = jnp.float32
BF16 = jnp.bfloat16


def _cparams(sem):
    return pltpu.CompilerParams(dimension_semantics=sem, vmem_limit_bytes=VMEM_LIMIT)


def _rms(x, g):
    return x * lax.rsqrt(jnp.mean(x * x, axis=-1, keepdims=True) + EPS) * g


def _norm_matmul_kernel(x_ref, g_ref, w_ref, o_ref, *, col_chunk):
    xn = _rms(x_ref[...].astype(F32), g_ref[...]).astype(BF16)
    for c in range(o_ref.shape[1] // col_chunk):
        sl = slice(c * col_chunk, (c + 1) * col_chunk)
        o_ref[:, sl] = jnp.dot(xn, w_ref[:, sl], preferred_element_type=F32).astype(o_ref.dtype)


def _norm_matmul(x, g, w, *, tm, col_chunk, name):
    n, d = x.shape
    wout = w.shape[1]
    return pl.pallas_call(
        functools.partial(_norm_matmul_kernel, col_chunk=col_chunk),
        grid=(n // tm,),
        in_specs=[pl.BlockSpec((tm, d), lambda i: (i, 0)),
                  pl.BlockSpec((1, d), lambda i: (0, 0)),
                  pl.BlockSpec((d, wout), lambda i: (0, 0))],
        out_specs=pl.BlockSpec((tm, wout), lambda i: (i, 0)),
        out_shape=jax.ShapeDtypeStruct((n, wout), BF16),
        compiler_params=_cparams(("parallel",)),
        name=name,
    )(x, g.reshape(1, d), w)


def _to_phase_order(a):
    n, d = a.shape
    return jnp.swapaxes(a.reshape(n // S5_CHUNK, S5_CHUNK, d), 0, 1).reshape(n, d)


def _from_phase_order(a):
    n, d = a.shape
    return jnp.swapaxes(a.reshape(S5_CHUNK, n // S5_CHUNK, d), 0, 1).reshape(n, d)


def _s5_in_proj_kernel(x_ref, g_ref, wg_ref, wx_ref, o_ref, xn_ref, *, col_chunk):
    tm, d = x_ref.shape[1:]
    nc = tm // S5_CHUNK
    xn = _rms(_to_phase_order(x_ref[0]), g_ref[...]).astype(BF16)
    for s in range(S5_CHUNK):
        xn_ref[0, :, s * d:(s + 1) * d] = xn[s * nc:(s + 1) * nc]
    for c in range(BRANCH_WIDTH // col_chunk):
        sl = slice(c * col_chunk, (c + 1) * col_chunk)
        o_ref[0, :, sl] = jnp.dot(xn, wg_ref[:, sl].astype(BF16), preferred_element_type=F32).astype(o_ref.dtype)
    o_ref[0, :, BRANCH_WIDTH:] = jnp.dot(xn, wx_ref[...].astype(BF16),
                                         preferred_element_type=F32).astype(o_ref.dtype)


def _s5_in_proj(x, g, w_in, *, tm, col_chunk):
    b, l, d = x.shape
    wout = BRANCH_WIDTH + XQ_WIDTH
    nc = tm // S5_CHUNK
    return pl.pallas_call(
        functools.partial(_s5_in_proj_kernel, col_chunk=col_chunk),
        grid=(b, l // tm),
        in_specs=[pl.BlockSpec((1, tm, d), lambda i, j: (i, j, 0)),
                  pl.BlockSpec((1, d), lambda i, j: (0, 0)),
                  pl.BlockSpec((d, BRANCH_WIDTH), lambda i, j: (0, (PRIMARY_WIDTH + XQ_WIDTH) // BRANCH_WIDTH)),
                  pl.BlockSpec((d, XQ_WIDTH), lambda i, j: (0, PRIMARY_WIDTH // XQ_WIDTH))],
        out_specs=[pl.BlockSpec((1, tm, wout), lambda i, j: (i, j, 0)),
                   pl.BlockSpec((1, nc, S5_CHUNK * d), lambda i, j: (i, j, 0))],
        out_shape=[jax.ShapeDtypeStruct((b, l, wout), BF16),
                   jax.ShapeDtypeStruct((b, l // S5_CHUNK, S5_CHUNK * d), BF16)],
        compiler_params=_cparams(("parallel", "parallel")),
        name="s5_in_proj",
    )(x, g.reshape(1, d), w_in, w_in)


def _mem_kv_kernel(m_ref, g_ref, w_ref, kg_ref, k_ref, v_ref):
    mn = _rms(m_ref[0], g_ref[0]).astype(BF16)
    kv = jnp.dot(mn, w_ref[0].astype(BF16), preferred_element_type=F32)
    for h in range(X_HEADS):
        sl = slice(h * X_HEAD_DIM, (h + 1) * X_HEAD_DIM)
        k_ref[0, 0, :, sl] = _rms(kv[:, sl], kg_ref[0]).astype(BF16)
    v_ref[0, 0] = kv[:, XQ_WIDTH:].astype(BF16)


def _mem_kv(mem, mem_norm, w_mem_kv, xk_norm):
    b, m, d = mem.shape
    depth = w_mem_kv.shape[0]
    out = jax.ShapeDtypeStruct((depth, b, m, XQ_WIDTH), BF16)
    return pl.pallas_call(
        _mem_kv_kernel,
        grid=(depth, b),
        in_specs=[pl.BlockSpec((1, m, d), lambda n, i: (i, 0, 0)),
                  pl.BlockSpec((1, 1, d), lambda n, i: (n, 0, 0)),
                  pl.BlockSpec((1, d, 2 * XQ_WIDTH), lambda n, i: (n, 0, 0)),
                  pl.BlockSpec((1, 1, X_HEAD_DIM), lambda n, i: (n, 0, 0))],
        out_specs=[pl.BlockSpec((1, 1, m, XQ_WIDTH), lambda n, i: (n, i, 0, 0)),
                   pl.BlockSpec((1, 1, m, XQ_WIDTH), lambda n, i: (n, i, 0, 0))],
        out_shape=[out, out],
        compiler_params=_cparams(("parallel", "parallel")),
        name="mem_kv",
    )(mem, mem_norm.reshape(depth, 1, d), w_mem_kv, xk_norm.reshape(depth, 1, X_HEAD_DIM))


def _s5_pow_kernel(lr_ref, li_ref, ls_ref, pr_ref, pi_ref):
    lr, li = lr_ref[...], li_ref[...]
    step = jnp.exp(ls_ref[...])
    zr, zi = lr * step, li * step
    for n, e in enumerate(S5_EXPONENTS):
        mag = jnp.exp(zr * e)
        pr_ref[n] = mag * jnp.cos(zi * e)
        pi_ref[n] = mag * jnp.sin(zi * e)
    ar, ai = pr_ref[1], pi_ref[1]
    den = lr * lr + li * li
    pr_ref[len(S5_EXPONENTS)] = ((ar - 1.0) * lr + ai * li) / den
    pi_ref[len(S5_EXPONENTS)] = (ai * lr - (ar - 1.0) * li) / den


def _s5_pow(lam_re, lam_im, log_step):
    g, p = lam_re.shape
    out = jax.ShapeDtypeStruct((len(S5_EXPONENTS) + 1, g, p), F32)
    return pl.pallas_call(_s5_pow_kernel, out_shape=[out, out], name="s5_pow")(
        lam_re, lam_im, log_step.reshape(g, 1))


def _s5_asm_kernel(pr_ref, pi_ref, cr_ref, ci_ref, btr_ref, bti_ref, br_ref, bi_ref,
                   toep_ref, wout_ref, wst_ref):
    def group(i, carry):
        _s5_asm_group(i, pr_ref, pi_ref, cr_ref, ci_ref, btr_ref, bti_ref, br_ref, bi_ref,
                      toep_ref, wout_ref, wst_ref)
        return carry

    lax.fori_loop(0, pr_ref.shape[0], group, 0)


def _dot_3pass_tiled(a, b):
    c = b.shape[1]
    rep = ((lax.broadcasted_iota(jnp.int32, (c, S5_CHUNK * c), 1) & (c - 1))
           == lax.broadcasted_iota(jnp.int32, (c, S5_CHUNK * c), 0)).astype(BF16)
    dot = functools.partial(jnp.dot, preferred_element_type=F32)
    a_hi, b_hi = a.astype(BF16), b.astype(BF16)
    a_lo = (a - a_hi.astype(F32)).astype(BF16)
    b_lo = (b - b_hi.astype(F32)).astype(BF16)
    b_hi, b_lo = dot(b_hi, rep).astype(BF16), dot(b_lo, rep).astype(BF16)
    return dot(a_hi, b_hi) + (dot(a_hi, b_lo) + dot(a_lo, b_hi))


def _s5_asm_group(i, pr_ref, pi_ref, cr_ref, ci_ref, btr_ref, bti_ref, br_ref, bi_ref,
                  toep_ref, wout_ref, wst_ref):
    t = S5_CHUNK
    pr, pi = pr_ref[i], pi_ref[i]
    cr, ci = cr_ref[i], ci_ref[i]
    btr, bti = btr_ref[i], bti_ref[i]
    mr, mi = pr[len(S5_EXPONENTS):], pi[len(S5_EXPONENTS):]
    amr = pr[:t] * mr - pi[:t] * mi
    ami = pr[:t] * mi + pi[:t] * mr
    l_re, l_im, w_re, w_im, o_re, o_im = [], [], [], [], [], []
    for k in range(t):
        ar, ai = amr[k:k + 1], ami[k:k + 1]
        l_re.append(cr * ar - ci * ai)
        l_im.append(cr * ai + ci * ar)
        ar, ai = amr[t - 1 - k:t - k], ami[t - 1 - k:t - k]
        w_re.append(btr * ar - bti * ai)
        w_im.append(btr * ai + bti * ar)
        ar, ai = pr[k + 1:k + 2], pi[k + 1:k + 2]
        o_re.append(cr * ar - ci * ai)
        o_im.append(-(cr * ai + ci * ar))
    cat = lambda parts: jnp.concatenate(parts, axis=0)
    kt = _dot_3pass_tiled(cat(l_re), br_ref[i]) - _dot_3pass_tiled(cat(l_im), bi_ref[i])
    n = kt.shape[0]
    blk = lax.shift_right_logical(lax.broadcasted_iota(jnp.int32, kt.shape, 1), int(math.log2(S5_GROUP_CH)))
    toep = jnp.where(blk == 0, kt, 0.0)
    for s in range(1, t):
        shifted = jnp.concatenate([jnp.zeros((s * S5_GROUP_CH, n), F32), kt[:n - s * S5_GROUP_CH]], axis=0)
        toep = jnp.where(blk == s, shifted, toep)
    toep_ref[i] = toep.astype(BF16)
    wout_ref[i] = jnp.concatenate([cat(o_re), cat(o_im)], axis=1).astype(BF16)
    wst_ref[i] = jnp.concatenate([cat(w_re), cat(w_im)], axis=1).T.astype(BF16)


def _s5_asm(pw_re, pw_im, c_re, c_im, b_re, b_im):
    g, c, p = c_re.shape
    tc = S5_CHUNK * c
    gs = 8
    blk = lambda a: pl.BlockSpec((gs,) + a.shape[1:], lambda i: (i, 0, 0))
    bt_re, bt_im = b_re.transpose(0, 2, 1), b_im.transpose(0, 2, 1)
    args = (pw_re, pw_im, c_re, c_im, bt_re, bt_im, b_re, b_im)
    return pl.pallas_call(
        _s5_asm_kernel,
        grid=(g // gs,),
        in_specs=[blk(a) for a in args],
        out_specs=[pl.BlockSpec((gs, tc, tc), lambda i: (i, 0, 0)),
                   pl.BlockSpec((gs, tc, 2 * p), lambda i: (i, 0, 0)),
                   pl.BlockSpec((gs, 2 * p, tc), lambda i: (i, 0, 0))],
        out_shape=[jax.ShapeDtypeStruct((g, tc, tc), BF16), jax.ShapeDtypeStruct((g, tc, 2 * p), BF16),
                   jax.ShapeDtypeStruct((g, 2 * p, tc), BF16)],
        compiler_params=_cparams(("parallel",)),
        name="s5_asm",
    )(*args)


def _s5_ut_kernel(xn_ref, w_ref, o_ref, wt_ref):
    @pl.when(pl.program_id(0) == 0)
    def _():
        wt_ref[...] = w_ref[...].T.astype(BF16)

    nb, nc, d = xn_ref.shape
    ut = lax.dot_general(wt_ref[...], xn_ref[...].reshape(nb * nc, d), (((1,), (1,)), ((), ())),
                         preferred_element_type=F32)
    o_ref[...] = ut.astype(BF16).reshape(o_ref.shape)


def _s5_ut(xn, w_in):
    b, nc, td = xn.shape
    d = td // S5_CHUNK
    return pl.pallas_call(
        _s5_ut_kernel,
        grid=(S5_CHUNK,),
        in_specs=[pl.BlockSpec((b, nc, d), lambda s: (0, 0, s)),
                  pl.BlockSpec((d, PRIMARY_WIDTH), lambda s: (0, 0))],
        out_specs=pl.BlockSpec((S5_GROUPS, S5_GROUP_CH, b * nc), lambda s: (0, s, 0)),
        out_shape=jax.ShapeDtypeStruct((S5_GROUPS, S5_CHUNK * S5_GROUP_CH, b * nc), BF16),
        scratch_shapes=[pltpu.VMEM((PRIMARY_WIDTH, d), BF16)],
        compiler_params=_cparams(("arbitrary",)),
        name="s5_ut",
    )(xn, w_in)


def _s5_mix_kernel(x_ref, toep_ref, wst_ref, wout_ref, sr_ref, si_ref, d_ref, o_ref, *, nb):
    p = S5_STATE
    gs = x_ref.shape[0]
    lane = lax.broadcasted_iota(jnp.int32, (p, LANES), 1)
    n_steps = int(math.log2(LANES))

    def scan(g, hloc):
        pw = []
        for i in range(n_steps):
            keep = lane >= (1 << i)
            pw.append((jnp.where(keep, jnp.broadcast_to(sr_ref[g, :, i:i + 1], (p, LANES)), 0.0),
                       jnp.where(keep, jnp.broadcast_to(si_ref[g, :, i:i + 1], (p, LANES)), 0.0)))
        h_re = [hloc[:p, b * LANES:(b + 1) * LANES] for b in range(nb)]
        h_im = [hloc[p:, b * LANES:(b + 1) * LANES] for b in range(nb)]
        for i in range(n_steps):
            ar, ai = pw[i]
            r_sh = [pltpu.roll(v, 1 << i, 1) for v in h_re]
            i_sh = [pltpu.roll(v, 1 << i, 1) for v in h_im]
            h_re = [h_re[b] + ar * r_sh[b] - ai * i_sh[b] for b in range(nb)]
            h_im = [h_im[b] + ar * i_sh[b] + ai * r_sh[b] for b in range(nb)]
        h_re = [jnp.where(lane >= 1, pltpu.roll(v, 1, 1), 0.0) for v in h_re]
        h_im = [jnp.where(lane >= 1, pltpu.roll(v, 1, 1), 0.0) for v in h_im]
        return jnp.concatenate([jnp.concatenate(h_re, axis=1), jnp.concatenate(h_im, axis=1)],
                               axis=0).astype(BF16)

    def outputs(g, h):
        x = x_ref[g]
        y = (jnp.dot(toep_ref[g], x, preferred_element_type=F32)
             + jnp.dot(wout_ref[g], h, preferred_element_type=F32)
             + d_ref[g] * x.astype(F32))
        o_ref[:, g * S5_GROUP_CH:(g + 1) * S5_GROUP_CH, :] = (
            jax.nn.gelu(y).astype(o_ref.dtype).reshape(S5_CHUNK, S5_GROUP_CH, y.shape[1]))

    h_prev = None
    for g in range(gs):
        hloc = jnp.dot(wst_ref[g], x_ref[g], preferred_element_type=F32)
        h = scan(g, hloc)
        if h_prev is not None:
            outputs(g - 1, h_prev)
        h_prev = h
    outputs(gs - 1, h_prev)


def _s5_mix(xg, toep, wst, wout, sc_re, sc_im, dcol, *, nb, gs):
    g, tc, cols = xg.shape
    assert cols == nb * LANES, "one batch's chunks must fill exactly one 128-lane block"
    blk = lambda a: pl.BlockSpec((gs,) + a.shape[1:], lambda i: (i, 0, 0))
    return pl.pallas_call(
        functools.partial(_s5_mix_kernel, nb=nb),
        grid=(g // gs,),
        in_specs=[blk(a) for a in (xg, toep, wst, wout, sc_re, sc_im, dcol)],
        out_specs=pl.BlockSpec((S5_CHUNK, gs * S5_GROUP_CH, cols), lambda i: (0, i, 0)),
        out_shape=jax.ShapeDtypeStruct((S5_CHUNK, g * S5_GROUP_CH, cols), BF16),
        compiler_params=_cparams(("parallel",)),
        name="s5_mix",
    )(xg, toep, wst, wout, sc_re, sc_im, dcol)


def _glu_kernel(y_ref, w_ref, o_ref, *, col_chunk):
    y = y_ref[0].T
    half = o_ref.shape[-1]
    for c in range(half // col_chunk):
        wa = w_ref[:, c * col_chunk:(c + 1) * col_chunk].astype(BF16)
        wg = w_ref[:, half + c * col_chunk:half + (c + 1) * col_chunk].astype(BF16)
        a = jnp.dot(y, wa, preferred_element_type=F32)
        g = jnp.dot(y, wg, preferred_element_type=F32)
        o_ref[:, :, c * col_chunk:(c + 1) * col_chunk] = (
            (a * jax.nn.sigmoid(g)).astype(o_ref.dtype).reshape(o_ref.shape[:2] + (col_chunk,)))


def _glu(yt, w, *, nb, col_chunk):
    t, k, cols = yt.shape
    nc = cols // nb
    half = w.shape[1] // 2
    return pl.pallas_call(
        functools.partial(_glu_kernel, col_chunk=col_chunk),
        grid=(t,),
        in_specs=[pl.BlockSpec((1, k, cols), lambda j: (j, 0, 0)),
                  pl.BlockSpec(w.shape, lambda j: (0, 0), pipeline_mode=pl.Buffered(1))],
        out_specs=pl.BlockSpec((nb, nc, half), lambda j: (0, 0, j)),
        out_shape=jax.ShapeDtypeStruct((nb, nc, t * half), BF16),
        compiler_params=_cparams(("parallel",)),
        name="glu",
    )(yt, w)


def _merge_kernel(x_ref, mix_ref, xq_ref, gate_ref, k_ref, v_ref, qg_ref, w_ref, o_ref, cat_ref, *, phased):
    gate = gate_ref[0]
    sg = gate * jax.nn.sigmoid(gate)
    if phased:
        nc = gate.shape[0] // S5_CHUNK
        for s in range(S5_CHUNK):
            rows = slice(s * nc, (s + 1) * nc)
            mix = mix_ref[0, :, s * PRIMARY_WIDTH:(s + 1) * PRIMARY_WIDTH]
            cat_ref[rows, :PRIMARY_WIDTH] = mix * sg[rows, :PRIMARY_WIDTH]
    else:
        cat_ref[:, :PRIMARY_WIDTH] = mix_ref[0] * sg[:, :PRIMARY_WIDTH]
    scale = X_HEAD_DIM ** -0.5
    for h in range(X_HEADS):
        sl = slice(h * X_HEAD_DIM, (h + 1) * X_HEAD_DIM)
        q = _rms(xq_ref[0, :, sl].astype(F32), qg_ref[...]).astype(BF16)
        s = lax.dot_general(q, k_ref[0, :, sl], (((1,), (1,)), ((), ())), preferred_element_type=F32) * scale
        p = jnp.exp(s - jnp.max(s, axis=-1, keepdims=True))
        p = (p / jnp.sum(p, axis=-1, keepdims=True)).astype(BF16)
        mo = jnp.dot(p, v_ref[0, :, sl], preferred_element_type=F32)
        osl = slice(PRIMARY_WIDTH + h * X_HEAD_DIM, PRIMARY_WIDTH + (h + 1) * X_HEAD_DIM)
        cat_ref[:, osl] = mo.astype(BF16) * sg[:, osl]
    delta = jnp.dot(cat_ref[...], w_ref[...].astype(BF16), preferred_element_type=F32)
    o_ref[0] = x_ref[0] + (_from_phase_order(delta) if phased else delta)


def _merge(x, mix, proj, xq_blk, gate_blk, mk, mv, xq_norm, w_out, layer, *, tm, phased=False):
    b, l, d = x.shape
    m = mk.shape[2]
    mix_spec = (pl.BlockSpec((1, tm // S5_CHUNK, S5_CHUNK * PRIMARY_WIDTH), lambda i, j: (i, j, 0)) if phased
                else pl.BlockSpec((1, tm, PRIMARY_WIDTH), lambda i, j: (i, j, 0)))
    return pl.pallas_call(
        functools.partial(_merge_kernel, phased=phased),
        grid=(b, l // tm),
        in_specs=[pl.BlockSpec((1, tm, d), lambda i, j: (i, j, 0)),
                  mix_spec,
                  pl.BlockSpec((1, tm, XQ_WIDTH), lambda i, j: (i, j, xq_blk)),
                  pl.BlockSpec((1, tm, BRANCH_WIDTH), lambda i, j: (i, j, gate_blk)),
                  pl.BlockSpec((None, 1, m, XQ_WIDTH), lambda i, j: (layer, i, 0, 0)),
                  pl.BlockSpec((None, 1, m, XQ_WIDTH), lambda i, j: (layer, i, 0, 0)),
                  pl.BlockSpec((None, 1, X_HEAD_DIM), lambda i, j: (layer, 0, 0)),
                  pl.BlockSpec((None, BRANCH_WIDTH, d), lambda i, j: (layer, 0, 0))],
        out_specs=pl.BlockSpec((1, tm, d), lambda i, j: (i, j, 0)),
        out_shape=jax.ShapeDtypeStruct((b, l, d), F32),
        scratch_shapes=[pltpu.VMEM((tm, BRANCH_WIDTH), BF16)],
        compiler_params=_cparams(("parallel", "parallel")),
        name="merge",
    )(x, mix, proj, proj, mk, mv, xq_norm.reshape(-1, 1, X_HEAD_DIM), w_out)


def _mla_qkv_kernel(cq_ref, ckv_ref, kr_ref, posr_ref, invfc_ref, gq_ref, gkv_ref, gqn_ref,
                    gkn_ref, gqr_ref, gkr_ref, wqt_ref, wk_ref, wvt_ref, qt_ref, kn_ref, krope_ref, vt_ref):
    half = MLA_ROPE // 2
    tm = cq_ref.shape[1]
    qscale = (MLA_NOPE + MLA_ROPE) ** -0.5 * math.log2(math.e)

    cq = _rms(cq_ref[0].astype(F32), gq_ref[...])
    ckv = _rms(ckv_ref[0].astype(F32), gkv_ref[...])
    cq_t = cq.T.astype(BF16)
    ckv_t = ckv.T.astype(BF16)
    ckv_b = ckv.astype(BF16)

    def project(h):
        dot = functools.partial(jnp.dot, preferred_element_type=F32)
        k_pair = dot(ckv_b, wk_ref[:, h * MLA_NOPE:(h + 2) * MLA_NOPE]) if h % 2 == 0 else None
        return (dot(wqt_ref[h * MLA_QK_PAD:h * MLA_QK_PAD + MLA_NOPE + MLA_ROPE, :], cq_t),
                dot(wvt_ref[h * MLA_V:(h + 1) * MLA_V, :], ckv_t), k_pair)

    ang_t = invfc_ref[...] * posr_ref[0].astype(F32)
    cos_t, sin_t = jnp.cos(ang_t), jnp.sin(ang_t)
    g_nope = jnp.broadcast_to(gqn_ref[...], (MLA_NOPE, tm)) * qscale
    g_r1 = jnp.broadcast_to(gqr_ref[:half, :], (half, tm)) * qscale
    g_r2 = jnp.broadcast_to(gqr_ref[half:, :], (half, tm)) * qscale
    ahead = 2
    pending = [project(h) for h in range(ahead)]
    for h in range(MLA_HEADS):
        if h + ahead < MLA_HEADS:
            pending.append(project(h + ahead))
        q, v_t, _ = pending[h]
        k_n = pending[h - h % 2][2][:, (h % 2) * MLA_NOPE:(h % 2 + 1) * MLA_NOPE]
        nope = q[:MLA_NOPE]
        r = lax.rsqrt(jnp.mean(nope * nope, axis=0, keepdims=True) + EPS)
        qt_ref[0, h, :MLA_NOPE, :] = (nope * r * g_nope).astype(BF16)
        x1, x2 = q[MLA_NOPE:MLA_NOPE + half], q[MLA_NOPE + half:MLA_NOPE + MLA_ROPE]
        ss = jnp.sum(x1 * x1, axis=0, keepdims=True) + jnp.sum(x2 * x2, axis=0, keepdims=True)
        r = lax.rsqrt(ss * (1.0 / MLA_ROPE) + EPS)
        x1, x2 = x1 * r * g_r1, x2 * r * g_r2
        qt_ref[0, h, MLA_NOPE:MLA_NOPE + half, :] = (x1 * cos_t - x2 * sin_t).astype(BF16)
        qt_ref[0, h, MLA_NOPE + half:MLA_NOPE + MLA_ROPE, :] = (x1 * sin_t + x2 * cos_t).astype(BF16)
        qt_ref[0, h, MLA_NOPE + MLA_ROPE:, :] = jnp.zeros((MLA_QK_PAD - MLA_NOPE - MLA_ROPE, tm), BF16)
        kn_ref[0, h] = _rms(k_n, gkn_ref[...]).astype(BF16)
        vt_ref[0, h] = v_t.astype(BF16)

    kr_t = kr_ref[0].astype(F32).T
    x1, x2 = kr_t[:half], kr_t[half:MLA_ROPE]
    ss = jnp.sum(x1 * x1, axis=0, keepdims=True) + jnp.sum(x2 * x2, axis=0, keepdims=True)
    r = lax.rsqrt(ss * (1.0 / MLA_ROPE) + EPS)
    x1, x2 = x1 * r * gkr_ref[:half, :], x2 * r * gkr_ref[half:, :]
    rot = jnp.concatenate([x1 * cos_t - x2 * sin_t, x1 * sin_t + x2 * cos_t,
                           jnp.zeros((LANES - MLA_ROPE, tm), F32)], axis=0)
    krope_ref[0] = rot.T.astype(BF16)


def _mla_qkv(proj, cq_blk, ckv_blk, kr_blk, positions, gains, wqt, wk, wvt, *, tm):
    b, l, _ = proj.shape
    hh = MLA_HEADS
    half = MLA_ROPE // 2
    inv_freq = ROPE_THETA ** (-jnp.arange(half, dtype=F32) / half)
    const = lambda a: pl.BlockSpec(a.shape, lambda i, j: (0,) * a.ndim)
    gq, gkv, gqn, gkn, gqr, gkr = gains
    consts = [inv_freq.reshape(half, 1), gq.reshape(1, -1), gkv.reshape(1, -1), gqn.reshape(-1, 1),
              gkn.reshape(1, -1), gqr.reshape(-1, 1), gkr.reshape(-1, 1), wqt, wk, wvt]
    return pl.pallas_call(
        _mla_qkv_kernel,
        grid=(b, l // tm),
        in_specs=[pl.BlockSpec((1, tm, MLA_Q_LORA), lambda i, j: (i, j, cq_blk)),
                  pl.BlockSpec((1, tm, MLA_KV_LORA), lambda i, j: (i, j, ckv_blk)),
                  pl.BlockSpec((1, tm, LANES), lambda i, j: (i, j, kr_blk)),
                  pl.BlockSpec((1, 1, tm), lambda i, j: (i, 0, j))] + [const(a) for a in consts],
        out_specs=[pl.BlockSpec((1, hh, MLA_QK_PAD, tm), lambda i, j: (i, 0, 0, j)),
                   pl.BlockSpec((1, hh, tm, MLA_NOPE), lambda i, j: (i, 0, j, 0)),
                   pl.BlockSpec((1, tm, LANES), lambda i, j: (i, j, 0)),
                   pl.BlockSpec((1, hh, MLA_V, tm), lambda i, j: (i, 0, 0, j))],
        out_shape=[jax.ShapeDtypeStruct((b, hh, MLA_QK_PAD, l), BF16),
                   jax.ShapeDtypeStruct((b, hh, l, MLA_NOPE), BF16),
                   jax.ShapeDtypeStruct((b, l, LANES), BF16),
                   jax.ShapeDtypeStruct((b, hh, MLA_V, l), BF16)],
        compiler_params=_cparams(("parallel", "parallel")),
        name="mla_qkv",
    )(proj, proj, proj, positions.reshape(b, 1, l), *consts)


def _flash_kernel(qt_ref, kn_ref, kr_ref, vt_ref, o_ref, m_ref, l_ref, acc_ref, *, tq, hp, ahead):
    qi = pl.program_id(2)
    m_ref[...] = jnp.full(m_ref.shape, -jnp.inf, F32)
    l_ref[...] = jnp.zeros(l_ref.shape, F32)
    acc_ref[...] = jnp.zeros(acc_ref.shape, F32)

    half = tq // 2
    lower = (lax.broadcasted_iota(jnp.int32, (half, half), 0)
             <= lax.broadcasted_iota(jnp.int32, (half, half), 1))

    def blocks(j, parts, diagonal):
        base = pl.multiple_of(j * tq, tq)
        items = [(h, pl.ds(base + k0, nk), slice(q0, q0 + nq)) for k0, nk, q0, nq in parts for h in range(hp)]

        def scores(h, rows, cols):
            k = jnp.concatenate([kn_ref[0, h, rows, :], kr_ref[0, rows, :]], axis=-1)
            return jnp.dot(k, qt_ref[0, h, :, cols], preferred_element_type=F32)

        pending = [scores(*it) for it in items[:ahead]]
        for n, (h, rows, cols) in enumerate(items):
            if n + ahead < len(items):
                pending.append(scores(*items[n + ahead]))
            s = pending[n]
            if diagonal:
                square = jnp.where(lower, s[:, :half], jnp.finfo(F32).min)
                s = square if s.shape[1] == half else jnp.concatenate([square, s[:, half:]], axis=1)
            m = m_ref[h, :, cols]
            m_new = jnp.maximum(m, jnp.max(s, axis=0, keepdims=True))
            alpha = jnp.exp2(m - m_new)
            p = jnp.exp2(s - m_new)
            l_ref[h, :, cols] = alpha * l_ref[h, :, cols] + jnp.sum(p, axis=0, keepdims=True)
            acc_ref[h, :, cols] = alpha * acc_ref[h, :, cols] + jnp.dot(
                vt_ref[0, h, :, rows], p.astype(BF16), preferred_element_type=F32)
            m_ref[h, :, cols] = m_new

    def body(j, carry):
        blocks(j, [(0, tq, 0, tq)], False)
        return carry

    lax.fori_loop(0, qi, body, 0)
    blocks(qi, [(0, half, 0, tq), (half, half, half, half)], True)
    for h in range(hp):
        o_ref[0, :, h * MLA_V:(h + 1) * MLA_V] = (acc_ref[h] / l_ref[h]).T.astype(o_ref.dtype)


def _flash(qt, kn, kr, vt, *, tq, hp, ahead):
    b, hh, _, l = qt.shape
    return pl.pallas_call(
        functools.partial(_flash_kernel, tq=tq, hp=hp, ahead=ahead),
        grid=(b, hh // hp, l // tq),
        in_specs=[pl.BlockSpec((1, hp, MLA_QK_PAD, tq), lambda i, h, j: (i, h, 0, j)),
                  pl.BlockSpec((1, hp, l, MLA_NOPE), lambda i, h, j: (i, h, 0, 0)),
                  pl.BlockSpec((1, l, LANES), lambda i, h, j: (i, 0, 0)),
                  pl.BlockSpec((1, hp, MLA_V, l), lambda i, h, j: (i, h, 0, 0))],
        out_specs=pl.BlockSpec((1, tq, hp * MLA_V), lambda i, h, j: (i, j, h)),
        out_shape=jax.ShapeDtypeStruct((b, l, hh * MLA_V), BF16),
        scratch_shapes=[pltpu.VMEM((hp, 1, tq), F32), pltpu.VMEM((hp, 1, tq), F32),
                        pltpu.VMEM((hp, MLA_V, tq), F32)],
        compiler_params=_cparams(("parallel", "parallel", "parallel")),
        name="flash",
    )(qt, kn, kr, vt)


def _s5_layer(x, ln, w_in, lam_re, lam_im, log_step, b_re, b_im, c_re, c_im, d, w_glu,
              w_out, mem_kv, xq_norm, layer):
    b, l, dm = x.shape
    tm = 512
    proj, xn = _s5_in_proj(x, ln, w_in, tm=tm, col_chunk=512)
    xg = _s5_ut(xn, w_in)
    pw_re, pw_im = _s5_pow(lam_re, lam_im, log_step)
    toep, wout, wst = _s5_asm(pw_re.transpose(1, 0, 2), pw_im.transpose(1, 0, 2), c_re, c_im, b_re, b_im)
    n_scan = int(math.log2(LANES))
    first = S5_EXPONENTS.index(S5_CHUNK)
    col = lambda pw: jnp.pad(pw[first:first + n_scan].transpose(1, 2, 0), ((0, 0), (0, 0), (0, 8 - n_scan)))
    dcol = jnp.tile(d.reshape(S5_GROUPS, 1, S5_GROUP_CH), (1, S5_CHUNK, 1)).reshape(S5_GROUPS, -1, 1)
    yt = _s5_mix(xg, toep, wst, wout, col(pw_re), col(pw_im), dcol, nb=b, gs=4)
    y = _glu(yt, w_glu, nb=b, col_chunk=256)
    return _merge(x, y, proj, BRANCH_WIDTH // XQ_WIDTH, 0, *mem_kv, xq_norm, w_out, layer, tm=tm, phased=True)


def _mla_layer(x, positions, ln, w_in, q_lora_norm, kv_lora_norm, w_uq, w_ukv, q_nope_norm, k_nope_norm,
               q_rope_norm, k_rope_norm, w_out, mem_kv, xq_norm, layer):
    b, l, dm = x.shape
    o1 = MLA_Q_LORA
    o2 = o1 + MLA_KV_LORA
    o3 = o2 + MLA_ROPE
    o4 = o3 + XQ_WIDTH
    w_perm = jnp.concatenate([w_in[:, o4:], w_in[:, :o1], w_in[:, o3:o4], w_in[:, o1:o2], w_in[:, o2:o3],
                              jnp.zeros((dm, 2 * LANES - MLA_ROPE), w_in.dtype)], axis=1).astype(BF16)
    proj = _norm_matmul(x.reshape(b * l, dm), ln, w_perm, tm=512, col_chunk=512, name="mla_in_proj")
    proj = proj.reshape(b, l, -1)
    gate_blk = 0
    cq_blk = BRANCH_WIDTH // MLA_Q_LORA
    xq_blk = (BRANCH_WIDTH + MLA_Q_LORA) // XQ_WIDTH
    ckv_blk = (BRANCH_WIDTH + MLA_Q_LORA + XQ_WIDTH) // MLA_KV_LORA
    kr_blk = (BRANCH_WIDTH + MLA_Q_LORA + XQ_WIDTH + MLA_KV_LORA) // LANES
    wq = w_uq.reshape(MLA_Q_LORA, MLA_HEADS, MLA_NOPE + MLA_ROPE)
    wq = jnp.pad(wq, ((0, 0), (0, 0), (0, MLA_QK_PAD - MLA_NOPE - MLA_ROPE)))
    wqt = wq.reshape(MLA_Q_LORA, MLA_HEADS * MLA_QK_PAD).T.astype(BF16)
    wkv = w_ukv.reshape(MLA_KV_LORA, MLA_HEADS, MLA_NOPE + MLA_V)
    wk = wkv[:, :, :MLA_NOPE].reshape(MLA_KV_LORA, MLA_HEADS * MLA_NOPE).astype(BF16)
    wvt = wkv[:, :, MLA_NOPE:].reshape(MLA_KV_LORA, MLA_HEADS * MLA_V).T.astype(BF16)
    qt, kn, kr, vt = _mla_qkv(proj, cq_blk, ckv_blk, kr_blk, positions,
                              (q_lora_norm, kv_lora_norm, q_nope_norm, k_nope_norm, q_rope_norm, k_rope_norm),
                              wqt, wk, wvt, tm=256)
    attn = _flash(qt, kn, kr, vt, tq=512, hp=12, ahead=2)
    return _merge(x, attn, proj, xq_blk, gate_blk, *mem_kv, xq_norm, w_out, layer, tm=512)


def kernel(x, mem, positions, ln_gain, w_out, mem_norm, w_mem_kv, xq_norm, xk_norm,
           s5_w_in, s5_lambda_re, s5_lambda_im, s5_log_step, s5_b_re, s5_b_im, s5_c_re, s5_c_im,
           s5_d, s5_w_glu, mla_w_in, mla_q_lora_norm, mla_kv_lora_norm, mla_w_uq, mla_w_ukv,
           mla_q_nope_norm, mla_k_nope_norm, mla_q_rope_norm, mla_k_rope_norm):
    depth = ln_gain.shape[0]
    mem_kv = _mem_kv(mem, mem_norm, w_mem_kv, xk_norm)
    for i in range(depth):
        j = i // 2
        if i % 2 == 0:
            x = _s5_layer(x, ln_gain[i], s5_w_in[j], s5_lambda_re[j], s5_lambda_im[j], s5_log_step[j],
                          s5_b_re[j], s5_b_im[j], s5_c_re[j], s5_c_im[j], s5_d[j], s5_w_glu[j],
                          w_out, mem_kv, xq_norm, i)
        else:
            x = _mla_layer(x, positions, ln_gain[i], mla_w_in[j], mla_q_lora_norm[j], mla_kv_lora_norm[j],
                           mla_w_uq[j], mla_w_ukv[j], mla_q_nope_norm[j], mla_k_nope_norm[j],
                           mla_q_rope_norm[j], mla_k_rope_norm[j],
                           w_out, mem_kv, xq_norm, i)
    return x
```

```python
import functools
import math

import jax
import jax.numpy as jnp
from jax import lax
from jax.experimental import pallas as pl
from jax.experimental.pallas import tpu as pltpu

D_MODEL = 1024
BRANCH_WIDTH = 2 * D_MODEL
XQ_WIDTH = BRANCH_WIDTH // 4
PRIMARY_WIDTH = BRANCH_WIDTH - XQ_WIDTH
X_HEADS = 4
X_HEAD_DIM = XQ_WIDTH // X_HEADS
S5_GROUP_CH = 16
S5_GROUPS = PRIMARY_WIDTH // S5_GROUP_CH
S5_STATE = 64
MLA_NOPE = 128
MLA_ROPE = 64
MLA_V = 128
MLA_HEADS = PRIMARY_WIDTH // MLA_V
MLA_Q_LORA = D_MODEL // 2
MLA_KV_LORA = D_MODEL // 4
ROPE_THETA = 10000.0
EPS = 1e-6

LANES = 128
MLA_QK_PAD = 2 * LANES
S5_CHUNK = 2 * LANES // S5_GROUP_CH
S5_EXPONENTS = list(range(S5_CHUNK + 1)) + [S5_CHUNK * 2 ** i for i in range(1, int(math.log2(LANES)))]
VMEM_LIMIT = 56 * 1024 * 1024

F32 = jnp.float32
BF16 = jnp.bfloat16


def _cparams(sem):
    return pltpu.CompilerParams(dimension_semantics=sem, vmem_limit_bytes=VMEM_LIMIT)


def _rms(x, g):
    return x * lax.rsqrt(jnp.mean(x * x, axis=-1, keepdims=True) + EPS) * g


def _norm_matmul_kernel(x_ref, g_ref, w_ref, o_ref, *, col_chunk):
    xn = _rms(x_ref[...].astype(F32), g_ref[...]).astype(BF16)
    for c in range(o_ref.shape[1] // col_chunk):
        sl = slice(c * col_chunk, (c + 1) * col_chunk)
        o_ref[:, sl] = jnp.dot(xn, w_ref[:, sl], preferred_element_type=F32).astype(o_ref.dtype)


def _norm_matmul(x, g, w, *, tm, col_chunk, name):
    n, d = x.shape
    wout = w.shape[1]
    return pl.pallas_call(
        functools.partial(_norm_matmul_kernel, col_chunk=col_chunk),
        grid=(n // tm,),
        in_specs=[pl.BlockSpec((tm, d), lambda i: (i, 0)),
                  pl.BlockSpec((1, d), lambda i: (0, 0)),
                  pl.BlockSpec((d, wout), lambda i: (0, 0))],
        out_specs=pl.BlockSpec((tm, wout), lambda i: (i, 0)),
        out_shape=jax.ShapeDtypeStruct((n, wout), BF16),
        compiler_params=_cparams(("parallel",)),
        name=name,
    )(x, g.reshape(1, d), w)


def _mla_in_proj_kernel(x_ref, g_ref, w_ref, o_ref, wp_ref, *, segments, col_chunk):
    @pl.when(pl.program_id(0) == 0)
    def _():
        at = 0
        for start, width in segments:
            wp_ref[:, at:at + width] = w_ref[:, start:start + width].astype(BF16)
            at += width
        wp_ref[:, at:] = jnp.zeros((wp_ref.shape[0], wp_ref.shape[1] - at), BF16)

    xn = _rms(x_ref[...], g_ref[...]).astype(BF16)
    for c in range(o_ref.shape[1] // col_chunk):
        sl = slice(c * col_chunk, (c + 1) * col_chunk)
        o_ref[:, sl] = jnp.dot(xn, wp_ref[:, sl], preferred_element_type=F32).astype(o_ref.dtype)


def _mla_in_proj(x, g, w, segments, wout, *, tm, col_chunk):
    n, d = x.shape
    return pl.pallas_call(
        functools.partial(_mla_in_proj_kernel, segments=segments, col_chunk=col_chunk),
        grid=(n // tm,),
        in_specs=[pl.BlockSpec((tm, d), lambda i: (i, 0)),
                  pl.BlockSpec((1, d), lambda i: (0, 0)),
                  pl.BlockSpec(w.shape, lambda i: (0, 0), pipeline_mode=pl.Buffered(1))],
        out_specs=pl.BlockSpec((tm, wout), lambda i: (i, 0)),
        out_shape=jax.ShapeDtypeStruct((n, wout), BF16),
        scratch_shapes=[pltpu.VMEM((d, wout), BF16)],
        compiler_params=_cparams(("arbitrary",)),
        name="mla_in_proj",
    )(x, g.reshape(1, d), w)


def _to_phase_order(a):
    n, d = a.shape
    return jnp.swapaxes(a.reshape(n // S5_CHUNK, S5_CHUNK, d), 0, 1).reshape(n, d)


def _from_phase_order(a):
    n, d = a.shape
    return jnp.swapaxes(a.reshape(S5_CHUNK, n // S5_CHUNK, d), 0, 1).reshape(n, d)


def _s5_in_proj_kernel(x_ref, g_ref, wg_ref, wx_ref, o_ref, xn_ref, *, col_chunk):
    tm, d = x_ref.shape[1:]
    nc = tm // S5_CHUNK
    xn = _rms(_to_phase_order(x_ref[0]), g_ref[...]).astype(BF16)
    for s in range(S5_CHUNK):
        xn_ref[0, :, s * d:(s + 1) * d] = xn[s * nc:(s + 1) * nc]
    for c in range(BRANCH_WIDTH // col_chunk):
        sl = slice(c * col_chunk, (c + 1) * col_chunk)
        o_ref[0, :, sl] = jnp.dot(xn, wg_ref[:, sl].astype(BF16), preferred_element_type=F32).astype(o_ref.dtype)
    o_ref[0, :, BRANCH_WIDTH:] = jnp.dot(xn, wx_ref[...].astype(BF16),
                                         preferred_element_type=F32).astype(o_ref.dtype)


def _s5_in_proj(x, g, w_in, *, tm, col_chunk):
    b, l, d = x.shape
    wout = BRANCH_WIDTH + XQ_WIDTH
    nc = tm // S5_CHUNK
    return pl.pallas_call(
        functools.partial(_s5_in_proj_kernel, col_chunk=col_chunk),
        grid=(b, l // tm),
        in_specs=[pl.BlockSpec((1, tm, d), lambda i, j: (i, j, 0)),
                  pl.BlockSpec((1, d), lambda i, j: (0, 0)),
                  pl.BlockSpec((d, BRANCH_WIDTH), lambda i, j: (0, (PRIMARY_WIDTH + XQ_WIDTH) // BRANCH_WIDTH)),
                  pl.BlockSpec((d, XQ_WIDTH), lambda i, j: (0, PRIMARY_WIDTH // XQ_WIDTH))],
        out_specs=[pl.BlockSpec((1, tm, wout), lambda i, j: (i, j, 0)),
                   pl.BlockSpec((1, nc, S5_CHUNK * d), lambda i, j: (i, j, 0))],
        out_shape=[jax.ShapeDtypeStruct((b, l, wout), BF16),
                   jax.ShapeDtypeStruct((b, l // S5_CHUNK, S5_CHUNK * d), BF16)],
        compiler_params=_cparams(("parallel", "parallel")),
        name="s5_in_proj",
    )(x, g.reshape(1, d), w_in, w_in)


def _mem_kv_kernel(m_ref, g_ref, w_ref, kg_ref, k_ref, v_ref):
    mn = _rms(m_ref[0], g_ref[0]).astype(BF16)
    kv = jnp.dot(mn, w_ref[0].astype(BF16), preferred_element_type=F32)
    for h in range(X_HEADS):
        sl = slice(h * X_HEAD_DIM, (h + 1) * X_HEAD_DIM)
        k_ref[0, 0, :, sl] = _rms(kv[:, sl], kg_ref[0]).astype(BF16)
    v_ref[0, 0] = kv[:, XQ_WIDTH:].astype(BF16)


def _mem_kv(mem, mem_norm, w_mem_kv, xk_norm):
    b, m, d = mem.shape
    depth = w_mem_kv.shape[0]
    out = jax.ShapeDtypeStruct((depth, b, m, XQ_WIDTH), BF16)
    return pl.pallas_call(
        _mem_kv_kernel,
        grid=(depth, b),
        in_specs=[pl.BlockSpec((1, m, d), lambda n, i: (i, 0, 0)),
                  pl.BlockSpec((1, 1, d), lambda n, i: (n, 0, 0)),
                  pl.BlockSpec((1, d, 2 * XQ_WIDTH), lambda n, i: (n, 0, 0)),
                  pl.BlockSpec((1, 1, X_HEAD_DIM), lambda n, i: (n, 0, 0))],
        out_specs=[pl.BlockSpec((1, 1, m, XQ_WIDTH), lambda n, i: (n, i, 0, 0)),
                   pl.BlockSpec((1, 1, m, XQ_WIDTH), lambda n, i: (n, i, 0, 0))],
        out_shape=[out, out],
        compiler_params=_cparams(("parallel", "parallel")),
        name="mem_kv",
    )(mem, mem_norm.reshape(depth, 1, d), w_mem_kv, xk_norm.reshape(depth, 1, X_HEAD_DIM))


def _s5_pow_kernel(lr_ref, li_ref, ls_ref, pr_ref, pi_ref):
    lr, li = lr_ref[...], li_ref[...]
    step = jnp.exp(ls_ref[...])
    zr, zi = lr * step, li * step
    for n, e in enumerate(S5_EXPONENTS):
        mag = jnp.exp(zr * e)
        pr_ref[n] = mag * jnp.cos(zi * e)
        pi_ref[n] = mag * jnp.sin(zi * e)
    ar, ai = pr_ref[1], pi_ref[1]
    den = lr * lr + li * li
    pr_ref[len(S5_EXPONENTS)] = ((ar - 1.0) * lr + ai * li) / den
    pi_ref[len(S5_EXPONENTS)] = (ai * lr - (ar - 1.0) * li) / den


def _s5_pow(lam_re, lam_im, log_step):
    g, p = lam_re.shape
    out = jax.ShapeDtypeStruct((len(S5_EXPONENTS) + 1, g, p), F32)
    return pl.pallas_call(_s5_pow_kernel, out_shape=[out, out], name="s5_pow")(
        lam_re, lam_im, log_step.reshape(g, 1))


def _s5_asm_kernel(pr_ref, pi_ref, cr_ref, ci_ref, btr_ref, bti_ref, br_ref, bi_ref,
                   toep_ref, wout_ref, wst_ref):
    def group(i, carry):
        _s5_asm_group(i, pr_ref, pi_ref, cr_ref, ci_ref, btr_ref, bti_ref, br_ref, bi_ref,
                      toep_ref, wout_ref, wst_ref)
        return carry

    lax.fori_loop(0, pr_ref.shape[0], group, 0)


def _dot_3pass_tiled(a, b):
    c = b.shape[1]
    rep = ((lax.broadcasted_iota(jnp.int32, (c, S5_CHUNK * c), 1) & (c - 1))
           == lax.broadcasted_iota(jnp.int32, (c, S5_CHUNK * c), 0)).astype(BF16)
    dot = functools.partial(jnp.dot, preferred_element_type=F32)
    a_hi, b_hi = a.astype(BF16), b.astype(BF16)
    a_lo = (a - a_hi.astype(F32)).astype(BF16)
    b_lo = (b - b_hi.astype(F32)).astype(BF16)
    b_hi, b_lo = dot(b_hi, rep).astype(BF16), dot(b_lo, rep).astype(BF16)
    return dot(a_hi, b_hi) + (dot(a_hi, b_lo) + dot(a_lo, b_hi))


def _s5_asm_group(i, pr_ref, pi_ref, cr_ref, ci_ref, btr_ref, bti_ref, br_ref, bi_ref,
                  toep_ref, wout_ref, wst_ref):
    t = S5_CHUNK
    pr, pi = pr_ref[i], pi_ref[i]
    cr, ci = cr_ref[i], ci_ref[i]
    btr, bti = btr_ref[i], bti_ref[i]
    mr, mi = pr[len(S5_EXPONENTS):], pi[len(S5_EXPONENTS):]
    amr = pr[:t] * mr - pi[:t] * mi
    ami = pr[:t] * mi + pi[:t] * mr
    l_re, l_im, w_re, w_im, o_re, o_im = [], [], [], [], [], []
    for k in range(t):
        ar, ai = amr[k:k + 1], ami[k:k + 1]
        l_re.append(cr * ar - ci * ai)
        l_im.append(cr * ai + ci * ar)
        ar, ai = amr[t - 1 - k:t - k], ami[t - 1 - k:t - k]
        w_re.append(btr * ar - bti * ai)
        w_im.append(btr * ai + bti * ar)
        ar, ai = pr[k + 1:k + 2], pi[k + 1:k + 2]
        o_re.append(cr * ar - ci * ai)
        o_im.append(-(cr * ai + ci * ar))
    cat = lambda parts: jnp.concatenate(parts, axis=0)
    kt = _dot_3pass_tiled(cat(l_re), br_ref[i]) - _dot_3pass_tiled(cat(l_im), bi_ref[i])
    n = kt.shape[0]
    blk = lax.shift_right_logical(lax.broadcasted_iota(jnp.int32, kt.shape, 1), int(math.log2(S5_GROUP_CH)))
    toep = jnp.where(blk == 0, kt, 0.0)
    for s in range(1, t):
        shifted = jnp.concatenate([jnp.zeros((s * S5_GROUP_CH, n), F32), kt[:n - s * S5_GROUP_CH]], axis=0)
        toep = jnp.where(blk == s, shifted, toep)
    toep_ref[i] = toep.astype(BF16)
    wout_ref[i] = jnp.concatenate([cat(o_re), cat(o_im)], axis=1).astype(BF16)
    wst_ref[i] = jnp.concatenate([cat(w_re), cat(w_im)], axis=1).T.astype(BF16)


def _s5_asm(pw_re, pw_im, c_re, c_im, b_re, b_im):
    g, c, p = c_re.shape
    tc = S5_CHUNK * c
    gs = 8
    blk = lambda a: pl.BlockSpec((gs,) + a.shape[1:], lambda i: (i, 0, 0))
    bt_re, bt_im = b_re.transpose(0, 2, 1), b_im.transpose(0, 2, 1)
    args = (pw_re, pw_im, c_re, c_im, bt_re, bt_im, b_re, b_im)
    return pl.pallas_call(
        _s5_asm_kernel,
        grid=(g // gs,),
        in_specs=[blk(a) for a in args],
        out_specs=[pl.BlockSpec((gs, tc, tc), lambda i: (i, 0, 0)),
                   pl.BlockSpec((gs, tc, 2 * p), lambda i: (i, 0, 0)),
                   pl.BlockSpec((gs, 2 * p, tc), lambda i: (i, 0, 0))],
        out_shape=[jax.ShapeDtypeStruct((g, tc, tc), BF16), jax.ShapeDtypeStruct((g, tc, 2 * p), BF16),
                   jax.ShapeDtypeStruct((g, 2 * p, tc), BF16)],
        compiler_params=_cparams(("parallel",)),
        name="s5_asm",
    )(*args)


def _s5_ut_kernel(xn_ref, w_ref, o_ref, wt_ref):
    @pl.when(pl.program_id(0) == 0)
    def _():
        wt_ref[...] = w_ref[...].T.astype(BF16)

    nb, nc, d = xn_ref.shape
    ut = lax.dot_general(wt_ref[...], xn_ref[...].reshape(nb * nc, d), (((1,), (1,)), ((), ())),
                         preferred_element_type=F32)
    o_ref[...] = ut.astype(BF16).reshape(o_ref.shape)


def _s5_ut(xn, w_in):
    b, nc, td = xn.shape
    d = td // S5_CHUNK
    return pl.pallas_call(
        _s5_ut_kernel,
        grid=(S5_CHUNK,),
        in_specs=[pl.BlockSpec((b, nc, d), lambda s: (0, 0, s)),
                  pl.BlockSpec((d, PRIMARY_WIDTH), lambda s: (0, 0))],
        out_specs=pl.BlockSpec((S5_GROUPS, S5_GROUP_CH, b * nc), lambda s: (0, s, 0)),
        out_shape=jax.ShapeDtypeStruct((S5_GROUPS, S5_CHUNK * S5_GROUP_CH, b * nc), BF16),
        scratch_shapes=[pltpu.VMEM((PRIMARY_WIDTH, d), BF16)],
        compiler_params=_cparams(("arbitrary",)),
        name="s5_ut",
    )(xn, w_in)


def _s5_mix_kernel(x_ref, toep_ref, wst_ref, wout_ref, sr_ref, si_ref, d_ref, o_ref, *, nb):
    p = S5_STATE
    gs = x_ref.shape[0]
    lane = lax.broadcasted_iota(jnp.int32, (p, LANES), 1)
    n_steps = int(math.log2(LANES))

    def scan(g, hloc):
        pw = []
        for i in range(n_steps):
            keep = lane >= (1 << i)
            pw.append((jnp.where(keep, jnp.broadcast_to(sr_ref[g, :, i:i + 1], (p, LANES)), 0.0),
                       jnp.where(keep, jnp.broadcast_to(si_ref[g, :, i:i + 1], (p, LANES)), 0.0)))
        h_re = [hloc[:p, b * LANES:(b + 1) * LANES] for b in range(nb)]
        h_im = [hloc[p:, b * LANES:(b + 1) * LANES] for b in range(nb)]
        for i in range(n_steps):
            ar, ai = pw[i]
            r_sh = [pltpu.roll(v, 1 << i, 1) for v in h_re]
            i_sh = [pltpu.roll(v, 1 << i, 1) for v in h_im]
            h_re = [h_re[b] + ar * r_sh[b] - ai * i_sh[b] for b in range(nb)]
            h_im = [h_im[b] + ar * i_sh[b] + ai * r_sh[b] for b in range(nb)]
        h_re = [jnp.where(lane >= 1, pltpu.roll(v, 1, 1), 0.0) for v in h_re]
        h_im = [jnp.where(lane >= 1, pltpu.roll(v, 1, 1), 0.0) for v in h_im]
        return jnp.concatenate([jnp.concatenate(h_re, axis=1), jnp.concatenate(h_im, axis=1)],
                               axis=0).astype(BF16)

    def outputs(g, h):
        x = x_ref[g]
        y = (jnp.dot(toep_ref[g], x, preferred_element_type=F32)
             + jnp.dot(wout_ref[g], h, preferred_element_type=F32)
             + d_ref[g] * x.astype(F32))
        o_ref[:, g * S5_GROUP_CH:(g + 1) * S5_GROUP_CH, :] = (
            jax.nn.gelu(y).astype(o_ref.dtype).reshape(S5_CHUNK, S5_GROUP_CH, y.shape[1]))

    h_prev = None
    for g in range(gs):
        hloc = jnp.dot(wst_ref[g], x_ref[g], preferred_element_type=F32)
        h = scan(g, hloc)
        if h_prev is not None:
            outputs(g - 1, h_prev)
        h_prev = h
    outputs(gs - 1, h_prev)


def _s5_mix(xg, toep, wst, wout, sc_re, sc_im, dcol, *, nb, gs):
    g, tc, cols = xg.shape
    assert cols == nb * LANES, "one batch's chunks must fill exactly one 128-lane block"
    blk = lambda a: pl.BlockSpec((gs,) + a.shape[1:], lambda i: (i, 0, 0))
    return pl.pallas_call(
        functools.partial(_s5_mix_kernel, nb=nb),
        grid=(g // gs,),
        in_specs=[blk(a) for a in (xg, toep, wst, wout, sc_re, sc_im, dcol)],
        out_specs=pl.BlockSpec((S5_CHUNK, gs * S5_GROUP_CH, cols), lambda i: (0, i, 0)),
        out_shape=jax.ShapeDtypeStruct((S5_CHUNK, g * S5_GROUP_CH, cols), BF16),
        compiler_params=_cparams(("parallel",)),
        name="s5_mix",
    )(xg, toep, wst, wout, sc_re, sc_im, dcol)


def _glu_kernel(y_ref, w_ref, o_ref, *, col_chunk):
    y = y_ref[0].T
    half = o_ref.shape[-1]
    for c in range(half // col_chunk):
        wa = w_ref[:, c * col_chunk:(c + 1) * col_chunk].astype(BF16)
        wg = w_ref[:, half + c * col_chunk:half + (c + 1) * col_chunk].astype(BF16)
        a = jnp.dot(y, wa, preferred_element_type=F32)
        g = jnp.dot(y, wg, preferred_element_type=F32)
        o_ref[:, :, c * col_chunk:(c + 1) * col_chunk] = (
            (a * jax.nn.sigmoid(g)).astype(o_ref.dtype).reshape(o_ref.shape[:2] + (col_chunk,)))


def _glu(yt, w, *, nb, col_chunk):
    t, k, cols = yt.shape
    nc = cols // nb
    half = w.shape[1] // 2
    return pl.pallas_call(
        functools.partial(_glu_kernel, col_chunk=col_chunk),
        grid=(t,),
        in_specs=[pl.BlockSpec((1, k, cols), lambda j: (j, 0, 0)),
                  pl.BlockSpec(w.shape, lambda j: (0, 0), pipeline_mode=pl.Buffered(1))],
        out_specs=pl.BlockSpec((nb, nc, half), lambda j: (0, 0, j)),
        out_shape=jax.ShapeDtypeStruct((nb, nc, t * half), BF16),
        compiler_params=_cparams(("parallel",)),
        name="glu",
    )(yt, w)


def _merge_kernel(x_ref, mix_ref, xq_ref, gate_ref, k_ref, v_ref, qg_ref, w_ref, o_ref, cat_ref, *, phased):
    gate = gate_ref[0]
    sg = gate * jax.nn.sigmoid(gate)
    if phased:
        nc = gate.shape[0] // S5_CHUNK
        for s in range(S5_CHUNK):
            rows = slice(s * nc, (s + 1) * nc)
            mix = mix_ref[0, :, s * PRIMARY_WIDTH:(s + 1) * PRIMARY_WIDTH]
            cat_ref[rows, :PRIMARY_WIDTH] = mix * sg[rows, :PRIMARY_WIDTH]
    else:
        cat_ref[:, :PRIMARY_WIDTH] = mix_ref[0] * sg[:, :PRIMARY_WIDTH]
    scale = X_HEAD_DIM ** -0.5
    for h in range(X_HEADS):
        sl = slice(h * X_HEAD_DIM, (h + 1) * X_HEAD_DIM)
        q = _rms(xq_ref[0, :, sl].astype(F32), qg_ref[...]).astype(BF16)
        s = lax.dot_general(q, k_ref[0, :, sl], (((1,), (1,)), ((), ())), preferred_element_type=F32) * scale
        p = jnp.exp(s - jnp.max(s, axis=-1, keepdims=True))
        p = (p / jnp.sum(p, axis=-1, keepdims=True)).astype(BF16)
        mo = jnp.dot(p, v_ref[0, :, sl], preferred_element_type=F32)
        osl = slice(PRIMARY_WIDTH + h * X_HEAD_DIM, PRIMARY_WIDTH + (h + 1) * X_HEAD_DIM)
        cat_ref[:, osl] = mo.astype(BF16) * sg[:, osl]
    delta = jnp.dot(cat_ref[...], w_ref[...].astype(BF16), preferred_element_type=F32)
    o_ref[0] = x_ref[0] + (_from_phase_order(delta) if phased else delta)


def _merge(x, mix, proj, xq_blk, gate_blk, mk, mv, xq_norm, w_out, layer, *, tm, phased=False):
    b, l, d = x.shape
    m = mk.shape[2]
    mix_spec = (pl.BlockSpec((1, tm // S5_CHUNK, S5_CHUNK * PRIMARY_WIDTH), lambda i, j: (i, j, 0)) if phased
                else pl.BlockSpec((1, tm, PRIMARY_WIDTH), lambda i, j: (i, j, 0)))
    return pl.pallas_call(
        functools.partial(_merge_kernel, phased=phased),
        grid=(b, l // tm),
        in_specs=[pl.BlockSpec((1, tm, d), lambda i, j: (i, j, 0)),
                  mix_spec,
                  pl.BlockSpec((1, tm, XQ_WIDTH), lambda i, j: (i, j, xq_blk)),
                  pl.BlockSpec((1, tm, BRANCH_WIDTH), lambda i, j: (i, j, gate_blk)),
                  pl.BlockSpec((None, 1, m, XQ_WIDTH), lambda i, j: (layer, i, 0, 0)),
                  pl.BlockSpec((None, 1, m, XQ_WIDTH), lambda i, j: (layer, i, 0, 0)),
                  pl.BlockSpec((None, 1, X_HEAD_DIM), lambda i, j: (layer, 0, 0)),
                  pl.BlockSpec((None, BRANCH_WIDTH, d), lambda i, j: (layer, 0, 0))],
        out_specs=pl.BlockSpec((1, tm, d), lambda i, j: (i, j, 0)),
        out_shape=jax.ShapeDtypeStruct((b, l, d), F32),
        scratch_shapes=[pltpu.VMEM((tm, BRANCH_WIDTH), BF16)],
        compiler_params=_cparams(("parallel", "parallel")),
        name="merge",
    )(x, mix, proj, proj, mk, mv, xq_norm.reshape(-1, 1, X_HEAD_DIM), w_out)


def _mla_qkv_kernel(cq_ref, ckv_ref, kr_ref, posr_ref, invfc_ref, gq_ref, gkv_ref, gqn_ref,
                    gkn_ref, gqr_ref, gkr_ref, wuq_ref, wukv_ref, qt_ref, kn_ref, krope_ref, vt_ref,
                    wqt_ref, wk_ref, wvt_ref):
    half = MLA_ROPE // 2
    tm = cq_ref.shape[1]
    qk = MLA_NOPE + MLA_ROPE

    @pl.when((pl.program_id(0) == 0) & (pl.program_id(1) == 0))
    def _():
        wqt_ref[...] = wuq_ref[...].T.astype(BF16)
        for h in range(MLA_HEADS):
            c0 = h * (MLA_NOPE + MLA_V)
            wk_ref[:, h * MLA_NOPE:(h + 1) * MLA_NOPE] = wukv_ref[:, c0:c0 + MLA_NOPE].astype(BF16)
            wvt_ref[h * MLA_V:(h + 1) * MLA_V, :] = wukv_ref[:, c0 + MLA_NOPE:c0 + MLA_NOPE + MLA_V].T.astype(BF16)

    qscale = (MLA_NOPE + MLA_ROPE) ** -0.5 * math.log2(math.e)

    cq = _rms(cq_ref[0].astype(F32), gq_ref[...])
    ckv = _rms(ckv_ref[0].astype(F32), gkv_ref[...])
    cq_t = cq.T.astype(BF16)
    ckv_t = ckv.T.astype(BF16)
    ckv_b = ckv.astype(BF16)

    def project(h):
        dot = functools.partial(jnp.dot, preferred_element_type=F32)
        k_pair = dot(ckv_b, wk_ref[:, h * MLA_NOPE:(h + 2) * MLA_NOPE]) if h % 2 == 0 else None
        return (dot(wqt_ref[h * qk:(h + 1) * qk, :], cq_t),
                dot(wvt_ref[h * MLA_V:(h + 1) * MLA_V, :], ckv_t), k_pair)

    ang_t = invfc_ref[...] * posr_ref[0].astype(F32)
    cos_t, sin_t = jnp.cos(ang_t), jnp.sin(ang_t)
    g_nope = jnp.broadcast_to(gqn_ref[...], (MLA_NOPE, tm)) * qscale
    g_r1 = jnp.broadcast_to(gqr_ref[:half, :], (half, tm)) * qscale
    g_r2 = jnp.broadcast_to(gqr_ref[half:, :], (half, tm)) * qscale
    ahead = 2
    pending = [project(h) for h in range(ahead)]
    for h in range(MLA_HEADS):
        if h + ahead < MLA_HEADS:
            pending.append(project(h + ahead))
        q, v_t, _ = pending[h]
        k_n = pending[h - h % 2][2][:, (h % 2) * MLA_NOPE:(h % 2 + 1) * MLA_NOPE]
        nope = q[:MLA_NOPE]
        r = lax.rsqrt(jnp.mean(nope * nope, axis=0, keepdims=True) + EPS)
        qt_ref[0, h, :MLA_NOPE, :] = (nope * r * g_nope).astype(BF16)
        x1, x2 = q[MLA_NOPE:MLA_NOPE + half], q[MLA_NOPE + half:MLA_NOPE + MLA_ROPE]
        ss = jnp.sum(x1 * x1, axis=0, keepdims=True) + jnp.sum(x2 * x2, axis=0, keepdims=True)
        r = lax.rsqrt(ss * (1.0 / MLA_ROPE) + EPS)
        x1, x2 = x1 * r * g_r1, x2 * r * g_r2
        qt_ref[0, h, MLA_NOPE:MLA_NOPE + half, :] = (x1 * cos_t - x2 * sin_t).astype(BF16)
        qt_ref[0, h, MLA_NOPE + half:MLA_NOPE + MLA_ROPE, :] = (x1 * sin_t + x2 * cos_t).astype(BF16)
        qt_ref[0, h, MLA_NOPE + MLA_ROPE:, :] = jnp.zeros((MLA_QK_PAD - MLA_NOPE - MLA_ROPE, tm), BF16)
        kn_ref[0, h] = _rms(k_n, gkn_ref[...]).astype(BF16)
        vt_ref[0, h] = v_t.astype(BF16)

    kr_t = kr_ref[0].astype(F32).T
    x1, x2 = kr_t[:half], kr_t[half:MLA_ROPE]
    ss = jnp.sum(x1 * x1, axis=0, keepdims=True) + jnp.sum(x2 * x2, axis=0, keepdims=True)
    r = lax.rsqrt(ss * (1.0 / MLA_ROPE) + EPS)
    x1, x2 = x1 * r * gkr_ref[:half, :], x2 * r * gkr_ref[half:, :]
    rot = jnp.concatenate([x1 * cos_t - x2 * sin_t, x1 * sin_t + x2 * cos_t,
                           jnp.zeros((LANES - MLA_ROPE, tm), F32)], axis=0)
    krope_ref[0] = rot.T.astype(BF16)


def _mla_qkv(proj, cq_blk, ckv_blk, kr_blk, positions, gains, w_uq, w_ukv, *, tm):
    b, l, _ = proj.shape
    hh = MLA_HEADS
    half = MLA_ROPE // 2
    inv_freq = ROPE_THETA ** (-jnp.arange(half, dtype=F32) / half)
    const = lambda a: pl.BlockSpec(a.shape, lambda i, j: (0,) * a.ndim)
    gq, gkv, gqn, gkn, gqr, gkr = gains
    consts = [inv_freq.reshape(half, 1), gq.reshape(1, -1), gkv.reshape(1, -1), gqn.reshape(-1, 1),
              gkn.reshape(1, -1), gqr.reshape(-1, 1), gkr.reshape(-1, 1), w_uq, w_ukv]
    return pl.pallas_call(
        _mla_qkv_kernel,
        grid=(b, l // tm),
        in_specs=[pl.BlockSpec((1, tm, MLA_Q_LORA), lambda i, j: (i, j, cq_blk)),
                  pl.BlockSpec((1, tm, MLA_KV_LORA), lambda i, j: (i, j, ckv_blk)),
                  pl.BlockSpec((1, tm, LANES), lambda i, j: (i, j, kr_blk)),
                  pl.BlockSpec((1, 1, tm), lambda i, j: (i, 0, j))] + [const(a) for a in consts],
        out_specs=[pl.BlockSpec((1, hh, MLA_QK_PAD, tm), lambda i, j: (i, 0, 0, j)),
                   pl.BlockSpec((1, hh, tm, MLA_NOPE), lambda i, j: (i, 0, j, 0)),
                   pl.BlockSpec((1, tm, LANES), lambda i, j: (i, j, 0)),
                   pl.BlockSpec((1, hh, MLA_V, tm), lambda i, j: (i, 0, 0, j))],
        out_shape=[jax.ShapeDtypeStruct((b, hh, MLA_QK_PAD, l), BF16),
                   jax.ShapeDtypeStruct((b, hh, l, MLA_NOPE), BF16),
                   jax.ShapeDtypeStruct((b, l, LANES), BF16),
                   jax.ShapeDtypeStruct((b, hh, MLA_V, l), BF16)],
        scratch_shapes=[pltpu.VMEM((hh * (MLA_NOPE + MLA_ROPE), MLA_Q_LORA), BF16),
                        pltpu.VMEM((MLA_KV_LORA, hh * MLA_NOPE), BF16),
                        pltpu.VMEM((hh * MLA_V, MLA_KV_LORA), BF16)],
        compiler_params=_cparams(("arbitrary", "arbitrary")),
        name="mla_qkv",
    )(proj, proj, proj, positions.reshape(b, 1, l), *consts)


def _flash_kernel(qt_ref, kn_ref, kr_ref, vt_ref, o_ref, m_ref, l_ref, acc_ref, *, tq, hp, ahead):
    qi = pl.program_id(2)
    m_ref[...] = jnp.full(m_ref.shape, -jnp.inf, F32)
    l_ref[...] = jnp.zeros(l_ref.shape, F32)
    acc_ref[...] = jnp.zeros(acc_ref.shape, F32)

    half = tq // 2
    lower = (lax.broadcasted_iota(jnp.int32, (half, half), 0)
             <= lax.broadcasted_iota(jnp.int32, (half, half), 1))

    def blocks(j, parts, diagonal):
        base = pl.multiple_of(j * tq, tq)
        items = [(h, pl.ds(base + k0, nk), slice(q0, q0 + nq)) for k0, nk, q0, nq in parts for h in range(hp)]

        def scores(h, rows, cols):
            k = jnp.concatenate([kn_ref[0, h, rows, :], kr_ref[0, rows, :]], axis=-1)
            return jnp.dot(k, qt_ref[0, h, :, cols], preferred_element_type=F32)

        pending = [scores(*it) for it in items[:ahead]]
        for n, (h, rows, cols) in enumerate(items):
            if n + ahead < len(items):
                pending.append(scores(*items[n + ahead]))
            s = pending[n]
            if diagonal:
                square = jnp.where(lower, s[:, :half], jnp.finfo(F32).min)
                s = square if s.shape[1] == half else jnp.concatenate([square, s[:, half:]], axis=1)
            m = m_ref[h, :, cols]
            m_new = jnp.maximum(m, jnp.max(s, axis=0, keepdims=True))
            alpha = jnp.exp2(m - m_new)
            p = jnp.exp2(s - m_new)
            l_ref[h, :, cols] = alpha * l_ref[h, :, cols] + jnp.sum(p, axis=0, keepdims=True)
            acc_ref[h, :, cols] = alpha * acc_ref[h, :, cols] + jnp.dot(
                vt_ref[0, h, :, rows], p.astype(BF16), preferred_element_type=F32)
            m_ref[h, :, cols] = m_new

    def body(j, carry):
        blocks(j, [(0, tq, 0, tq)], False)
        return carry

    lax.fori_loop(0, qi, body, 0)
    blocks(qi, [(0, half, 0, tq), (half, half, half, half)], True)
    for h in range(hp):
        o_ref[0, :, h * MLA_V:(h + 1) * MLA_V] = (acc_ref[h] / l_ref[h]).T.astype(o_ref.dtype)


def _flash(qt, kn, kr, vt, *, tq, hp, ahead):
    b, hh, _, l = qt.shape
    return pl.pallas_call(
        functools.partial(_flash_kernel, tq=tq, hp=hp, ahead=ahead),
        grid=(b, hh // hp, l // tq),
        in_specs=[pl.BlockSpec((1, hp, MLA_QK_PAD, tq), lambda i, h, j: (i, h, 0, j)),
                  pl.BlockSpec((1, hp, l, MLA_NOPE), lambda i, h, j: (i, h, 0, 0)),
                  pl.BlockSpec((1, l, LANES), lambda i, h, j: (i, 0, 0)),
                  pl.BlockSpec((1, hp, MLA_V, l), lambda i, h, j: (i, h, 0, 0))],
        out_specs=pl.BlockSpec((1, tq, hp * MLA_V), lambda i, h, j: (i, j, h)),
        out_shape=jax.ShapeDtypeStruct((b, l, hh * MLA_V), BF16),
        scratch_shapes=[pltpu.VMEM((hp, 1, tq), F32), pltpu.VMEM((hp, 1, tq), F32),
                        pltpu.VMEM((hp, MLA_V, tq), F32)],
        compiler_params=_cparams(("parallel", "parallel", "parallel")),
        name="flash",
    )(qt, kn, kr, vt)


def _s5_layer(x, ln, w_in, lam_re, lam_im, log_step, b_re, b_im, c_re, c_im, d, w_glu,
              w_out, mem_kv, xq_norm, layer):
    b, l, dm = x.shape
    tm = 512
    proj, xn = _s5_in_proj(x, ln, w_in, tm=tm, col_chunk=512)
    xg = _s5_ut(xn, w_in)
    pw_re, pw_im = _s5_pow(lam_re, lam_im, log_step)
    toep, wout, wst = _s5_asm(pw_re.transpose(1, 0, 2), pw_im.transpose(1, 0, 2), c_re, c_im, b_re, b_im)
    n_scan = int(math.log2(LANES))
    first = S5_EXPONENTS.index(S5_CHUNK)
    col = lambda pw: jnp.pad(pw[first:first + n_scan].transpose(1, 2, 0), ((0, 0), (0, 0), (0, 8 - n_scan)))
    dcol = jnp.tile(d.reshape(S5_GROUPS, 1, S5_GROUP_CH), (1, S5_CHUNK, 1)).reshape(S5_GROUPS, -1, 1)
    yt = _s5_mix(xg, toep, wst, wout, col(pw_re), col(pw_im), dcol, nb=b, gs=4)
    y = _glu(yt, w_glu, nb=b, col_chunk=256)
    return _merge(x, y, proj, BRANCH_WIDTH // XQ_WIDTH, 0, *mem_kv, xq_norm, w_out, layer, tm=tm, phased=True)


def _mla_layer(x, positions, ln, w_in, q_lora_norm, kv_lora_norm, w_uq, w_ukv, q_nope_norm, k_nope_norm,
               q_rope_norm, k_rope_norm, w_out, mem_kv, xq_norm, layer):
    b, l, dm = x.shape
    o1 = MLA_Q_LORA
    o2 = o1 + MLA_KV_LORA
    o3 = o2 + MLA_ROPE
    o4 = o3 + XQ_WIDTH
    segments = ((o4, BRANCH_WIDTH), (0, o1), (o3, XQ_WIDTH), (o1, MLA_KV_LORA), (o2, MLA_ROPE))
    wout = -(-(o4 + BRANCH_WIDTH) // 512) * 512
    proj = _mla_in_proj(x.reshape(b * l, dm), ln, w_in, segments, wout, tm=512, col_chunk=512)
    proj = proj.reshape(b, l, -1)
    gate_blk = 0
    cq_blk = BRANCH_WIDTH // MLA_Q_LORA
    xq_blk = (BRANCH_WIDTH + MLA_Q_LORA) // XQ_WIDTH
    ckv_blk = (BRANCH_WIDTH + MLA_Q_LORA + XQ_WIDTH) // MLA_KV_LORA
    kr_blk = (BRANCH_WIDTH + MLA_Q_LORA + XQ_WIDTH + MLA_KV_LORA) // LANES
    qt, kn, kr, vt = _mla_qkv(proj, cq_blk, ckv_blk, kr_blk, positions,
                              (q_lora_norm, kv_lora_norm, q_nope_norm, k_nope_norm, q_rope_norm, k_rope_norm),
                              w_uq, w_ukv, tm=256)
    attn = _flash(qt, kn, kr, vt, tq=512, hp=12, ahead=2)
    return _merge(x, attn, proj, xq_blk, gate_blk, *mem_kv, xq_norm, w_out, layer, tm=512)


def kernel(x, mem, positions, ln_gain, w_out, mem_norm, w_mem_kv, xq_norm, xk_norm,
           s5_w_in, s5_lambda_re, s5_lambda_im, s5_log_step, s5_b_re, s5_b_im, s5_c_re, s5_c_im,
           s5_d, s5_w_glu, mla_w_in, mla_q_lora_norm, mla_kv_lora_norm, mla_w_uq, mla_w_ukv,
           mla_q_nope_norm, mla_k_nope_norm, mla_q_rope_norm, mla_k_rope_norm):
    depth = ln_gain.shape[0]
    mem_kv = _mem_kv(mem, mem_norm, w_mem_kv, xk_norm)
    for i in range(depth):
        j = i // 2
        if i % 2 == 0:
            x = _s5_layer(x, ln_gain[i], s5_w_in[j], s5_lambda_re[j], s5_lambda_im[j], s5_log_step[j],
                          s5_b_re[j], s5_b_im[j], s5_c_re[j], s5_c_im[j], s5_d[j], s5_w_glu[j],
                          w_out, mem_kv, xq_norm, i)
        else:
            x = _mla_layer(x, positions, ln_gain[i], mla_w_in[j], mla_q_lora_norm[j], mla_kv_lora_norm[j],
                           mla_w_uq[j], mla_w_ukv[j], mla_q_nope_norm[j], mla_k_nope_norm[j],
                           mla_q_rope_norm[j], mla_k_rope_norm[j],
                           w_out, mem_kv, xq_norm, i)
    return x
```

```python
import functools
import math

import jax
import jax.numpy as jnp
from jax import lax
from jax.experimental import pallas as pl
from jax.experimental.pallas import tpu as pltpu

D_MODEL = 1024
BRANCH_WIDTH = 2 * D_MODEL
XQ_WIDTH = BRANCH_WIDTH // 4
PRIMARY_WIDTH = BRANCH_WIDTH - XQ_WIDTH
X_HEADS = 4
X_HEAD_DIM = XQ_WIDTH // X_HEADS
S5_GROUP_CH = 16
S5_GROUPS = PRIMARY_WIDTH // S5_GROUP_CH
S5_STATE = 64
MLA_NOPE = 128
MLA_ROPE = 64
MLA_V = 128
MLA_HEADS = PRIMARY_WIDTH // MLA_V
MLA_Q_LORA = D_MODEL // 2
MLA_KV_LORA = D_MODEL // 4
ROPE_THETA = 10000.0
EPS = 1e-6

LANES = 128
MLA_QK_PAD = 2 * LANES
S5_CHUNK = 2 * LANES // S5_GROUP_CH
S5_EXPONENTS = list(range(S5_CHUNK + 1)) + [S5_CHUNK * 2 ** i for i in range(1, int(math.log2(LANES)))]
VMEM_LIMIT = 56 * 1024 * 1024

F32 = jnp.float32
BF16 = jnp.bfloat16


def _cparams(sem):
    return pltpu.CompilerParams(dimension_semantics=sem, vmem_limit_bytes=VMEM_LIMIT)


def _rms(x, g):
    return x * lax.rsqrt(jnp.mean(x * x, axis=-1, keepdims=True) + EPS) * g


def _layer_block(w, j, block=None, index=None):
    block = tuple(w.shape[1:]) if block is None else block
    index = (0,) * len(block) if index is None else index
    return (None,) + block, lambda *_: (j,) + index


def _mla_in_proj_kernel(x_ref, g_ref, w_ref, o_ref, wp_ref, *, segments, col_chunk):
    @pl.when(pl.program_id(0) == 0)
    def _():
        at = 0
        for start, width in segments:
            wp_ref[:, at:at + width] = w_ref[:, start:start + width].astype(BF16)
            at += width
        wp_ref[:, at:] = jnp.zeros((wp_ref.shape[0], wp_ref.shape[1] - at), BF16)

    xn = _rms(x_ref[...], g_ref[...]).astype(BF16)
    for c in range(o_ref.shape[1] // col_chunk):
        sl = slice(c * col_chunk, (c + 1) * col_chunk)
        o_ref[:, sl] = jnp.dot(xn, wp_ref[:, sl], preferred_element_type=F32).astype(o_ref.dtype)


def _mla_in_proj(x, g, w, j, segments, wout, *, tm, col_chunk):
    n, d = x.shape
    return pl.pallas_call(
        functools.partial(_mla_in_proj_kernel, segments=segments, col_chunk=col_chunk),
        grid=(n // tm,),
        in_specs=[pl.BlockSpec((tm, d), lambda i: (i, 0)),
                  pl.BlockSpec((1, d), lambda i: (0, 0)),
                  pl.BlockSpec(*_layer_block(w, j), pipeline_mode=pl.Buffered(1))],
        out_specs=pl.BlockSpec((tm, wout), lambda i: (i, 0)),
        out_shape=jax.ShapeDtypeStruct((n, wout), BF16),
        scratch_shapes=[pltpu.VMEM((d, wout), BF16)],
        compiler_params=_cparams(("arbitrary",)),
        name="mla_in_proj",
    )(x, g.reshape(1, d), w)


def _to_phase_order(a):
    n, d = a.shape
    return jnp.swapaxes(a.reshape(n // S5_CHUNK, S5_CHUNK, d), 0, 1).reshape(n, d)


def _from_phase_order(a):
    n, d = a.shape
    return jnp.swapaxes(a.reshape(S5_CHUNK, n // S5_CHUNK, d), 0, 1).reshape(n, d)


def _s5_in_proj_kernel(x_ref, g_ref, wg_ref, wx_ref, o_ref, xn_ref, *, col_chunk):
    tm, d = x_ref.shape[1:]
    nc = tm // S5_CHUNK
    xn = _rms(_to_phase_order(x_ref[0]), g_ref[...]).astype(BF16)
    for s in range(S5_CHUNK):
        xn_ref[0, :, s * d:(s + 1) * d] = xn[s * nc:(s + 1) * nc]
    for c in range(BRANCH_WIDTH // col_chunk):
        sl = slice(c * col_chunk, (c + 1) * col_chunk)
        o_ref[0, :, sl] = jnp.dot(xn, wg_ref[:, sl].astype(BF16), preferred_element_type=F32).astype(o_ref.dtype)
    o_ref[0, :, BRANCH_WIDTH:] = jnp.dot(xn, wx_ref[...].astype(BF16),
                                         preferred_element_type=F32).astype(o_ref.dtype)


def _s5_in_proj(x, g, w_in, j, *, tm, col_chunk):
    b, l, d = x.shape
    wout = BRANCH_WIDTH + XQ_WIDTH
    nc = tm // S5_CHUNK
    return pl.pallas_call(
        functools.partial(_s5_in_proj_kernel, col_chunk=col_chunk),
        grid=(b, l // tm),
        in_specs=[pl.BlockSpec((1, tm, d), lambda i, j: (i, j, 0)),
                  pl.BlockSpec((1, d), lambda i, j: (0, 0)),
                  pl.BlockSpec(*_layer_block(w_in, j, (d, BRANCH_WIDTH),
                                             (0, (PRIMARY_WIDTH + XQ_WIDTH) // BRANCH_WIDTH))),
                  pl.BlockSpec(*_layer_block(w_in, j, (d, XQ_WIDTH), (0, PRIMARY_WIDTH // XQ_WIDTH)))],
        out_specs=[pl.BlockSpec((1, tm, wout), lambda i, j: (i, j, 0)),
                   pl.BlockSpec((1, nc, S5_CHUNK * d), lambda i, j: (i, j, 0))],
        out_shape=[jax.ShapeDtypeStruct((b, l, wout), BF16),
                   jax.ShapeDtypeStruct((b, l // S5_CHUNK, S5_CHUNK * d), BF16)],
        compiler_params=_cparams(("parallel", "parallel")),
        name="s5_in_proj",
    )(x, g.reshape(1, d), w_in, w_in)


def _mem_kv_kernel(m_ref, g_ref, w_ref, kg_ref, k_ref, v_ref):
    mn = _rms(m_ref[0], g_ref[0]).astype(BF16)
    kv = jnp.dot(mn, w_ref[0].astype(BF16), preferred_element_type=F32)
    for h in range(X_HEADS):
        sl = slice(h * X_HEAD_DIM, (h + 1) * X_HEAD_DIM)
        k_ref[0, 0, :, sl] = _rms(kv[:, sl], kg_ref[0]).astype(BF16)
    v_ref[0, 0] = kv[:, XQ_WIDTH:].astype(BF16)


def _mem_kv(mem, mem_norm, w_mem_kv, xk_norm):
    b, m, d = mem.shape
    depth = w_mem_kv.shape[0]
    out = jax.ShapeDtypeStruct((depth, b, m, XQ_WIDTH), BF16)
    return pl.pallas_call(
        _mem_kv_kernel,
        grid=(depth, b),
        in_specs=[pl.BlockSpec((1, m, d), lambda n, i: (i, 0, 0)),
                  pl.BlockSpec((1, 1, d), lambda n, i: (n, 0, 0)),
                  pl.BlockSpec((1, d, 2 * XQ_WIDTH), lambda n, i: (n, 0, 0)),
                  pl.BlockSpec((1, 1, X_HEAD_DIM), lambda n, i: (n, 0, 0))],
        out_specs=[pl.BlockSpec((1, 1, m, XQ_WIDTH), lambda n, i: (n, i, 0, 0)),
                   pl.BlockSpec((1, 1, m, XQ_WIDTH), lambda n, i: (n, i, 0, 0))],
        out_shape=[out, out],
        compiler_params=_cparams(("parallel", "parallel")),
        name="mem_kv",
    )(mem, mem_norm.reshape(depth, 1, d), w_mem_kv, xk_norm.reshape(depth, 1, X_HEAD_DIM))


def _s5_pow_kernel(lr_ref, li_ref, ls_ref, pr_ref, pi_ref):
    lr, li = lr_ref[...], li_ref[...]
    step = jnp.exp(ls_ref[...])
    zr, zi = lr * step, li * step
    for n, e in enumerate(S5_EXPONENTS):
        mag = jnp.exp(zr * e)
        pr_ref[n] = mag * jnp.cos(zi * e)
        pi_ref[n] = mag * jnp.sin(zi * e)
    ar, ai = pr_ref[1], pi_ref[1]
    den = lr * lr + li * li
    pr_ref[len(S5_EXPONENTS)] = ((ar - 1.0) * lr + ai * li) / den
    pi_ref[len(S5_EXPONENTS)] = (ai * lr - (ar - 1.0) * li) / den


def _s5_pow(lam_re, lam_im, log_step):
    g, p = lam_re.shape
    out = jax.ShapeDtypeStruct((len(S5_EXPONENTS) + 1, g, p), F32)
    return pl.pallas_call(_s5_pow_kernel, out_shape=[out, out], name="s5_pow")(
        lam_re, lam_im, log_step.reshape(g, 1))


def _s5_asm_kernel(pr_ref, pi_ref, cr_ref, ci_ref, btr_ref, bti_ref, br_ref, bi_ref,
                   toep_ref, wout_ref, wst_ref):
    def group(i, carry):
        _s5_asm_group(i, pr_ref, pi_ref, cr_ref, ci_ref, btr_ref, bti_ref, br_ref, bi_ref,
                      toep_ref, wout_ref, wst_ref)
        return carry

    lax.fori_loop(0, pr_ref.shape[0], group, 0)


def _dot_3pass_tiled(a, b):
    c = b.shape[1]
    rep = ((lax.broadcasted_iota(jnp.int32, (c, S5_CHUNK * c), 1) & (c - 1))
           == lax.broadcasted_iota(jnp.int32, (c, S5_CHUNK * c), 0)).astype(BF16)
    dot = functools.partial(jnp.dot, preferred_element_type=F32)
    a_hi, b_hi = a.astype(BF16), b.astype(BF16)
    a_lo = (a - a_hi.astype(F32)).astype(BF16)
    b_lo = (b - b_hi.astype(F32)).astype(BF16)
    b_hi, b_lo = dot(b_hi, rep).astype(BF16), dot(b_lo, rep).astype(BF16)
    return dot(a_hi, b_hi) + (dot(a_hi, b_lo) + dot(a_lo, b_hi))


def _s5_asm_group(i, pr_ref, pi_ref, cr_ref, ci_ref, btr_ref, bti_ref, br_ref, bi_ref,
                  toep_ref, wout_ref, wst_ref):
    t = S5_CHUNK
    pr, pi = pr_ref[i], pi_ref[i]
    cr, ci = cr_ref[i], ci_ref[i]
    btr, bti = btr_ref[i], bti_ref[i]
    mr, mi = pr[len(S5_EXPONENTS):], pi[len(S5_EXPONENTS):]
    amr = pr[:t] * mr - pi[:t] * mi
    ami = pr[:t] * mi + pi[:t] * mr
    l_re, l_im, w_re, w_im, o_re, o_im = [], [], [], [], [], []
    for k in range(t):
        ar, ai = amr[k:k + 1], ami[k:k + 1]
        l_re.append(cr * ar - ci * ai)
        l_im.append(-(cr * ai + ci * ar))
        ar, ai = amr[t - 1 - k:t - k], ami[t - 1 - k:t - k]
        w_re.append(btr * ar - bti * ai)
        w_im.append(btr * ai + bti * ar)
        ar, ai = pr[k + 1:k + 2], pi[k + 1:k + 2]
        o_re.append(cr * ar - ci * ai)
        o_im.append(-(cr * ai + ci * ar))
    cat = lambda parts: jnp.concatenate(parts, axis=0)
    kt = _dot_3pass_tiled(jnp.concatenate([cat(l_re), cat(l_im)], axis=1),
                          jnp.concatenate([br_ref[i], bi_ref[i]], axis=0))
    n = kt.shape[0]
    blk = lax.shift_right_logical(lax.broadcasted_iota(jnp.int32, kt.shape, 1), int(math.log2(S5_GROUP_CH)))
    toep = jnp.where(blk == 0, kt, 0.0)
    for s in range(1, t):
        shifted = jnp.concatenate([jnp.zeros((s * S5_GROUP_CH, n), F32), kt[:n - s * S5_GROUP_CH]], axis=0)
        toep = jnp.where(blk == s, shifted, toep)
    toep_ref[i] = toep.astype(BF16)
    wout_ref[i] = jnp.concatenate([cat(o_re), cat(o_im)], axis=1).astype(BF16)
    wst_ref[i] = jnp.concatenate([cat(w_re), cat(w_im)], axis=1).T.astype(BF16)


def _s5_asm(pw_re, pw_im, c_re, c_im, b_re, b_im):
    g, c, p = c_re.shape
    tc = S5_CHUNK * c
    gs = 8
    blk = lambda a: pl.BlockSpec((gs,) + a.shape[1:], lambda i: (i, 0, 0))
    bt_re, bt_im = b_re.transpose(0, 2, 1), b_im.transpose(0, 2, 1)
    args = (pw_re, pw_im, c_re, c_im, bt_re, bt_im, b_re, b_im)
    return pl.pallas_call(
        _s5_asm_kernel,
        grid=(g // gs,),
        in_specs=[blk(a) for a in args],
        out_specs=[pl.BlockSpec((gs, tc, tc), lambda i: (i, 0, 0)),
                   pl.BlockSpec((gs, tc, 2 * p), lambda i: (i, 0, 0)),
                   pl.BlockSpec((gs, 2 * p, tc), lambda i: (i, 0, 0))],
        out_shape=[jax.ShapeDtypeStruct((g, tc, tc), BF16), jax.ShapeDtypeStruct((g, tc, 2 * p), BF16),
                   jax.ShapeDtypeStruct((g, 2 * p, tc), BF16)],
        compiler_params=_cparams(("parallel",)),
        name="s5_asm",
    )(*args)


def _s5_ut_kernel(xn_ref, w_ref, o_ref, wt_ref):
    @pl.when(pl.program_id(0) == 0)
    def _():
        wt_ref[...] = w_ref[...].T.astype(BF16)

    nb, nc, d = xn_ref.shape
    ut = lax.dot_general(wt_ref[...], xn_ref[...].reshape(nb * nc, d), (((1,), (1,)), ((), ())),
                         preferred_element_type=F32)
    o_ref[...] = ut.astype(BF16).reshape(o_ref.shape)


def _s5_ut(xn, w_in, j):
    b, nc, td = xn.shape
    d = td // S5_CHUNK
    return pl.pallas_call(
        _s5_ut_kernel,
        grid=(S5_CHUNK,),
        in_specs=[pl.BlockSpec((b, nc, d), lambda s: (0, 0, s)),
                  pl.BlockSpec(*_layer_block(w_in, j, (d, PRIMARY_WIDTH)))],
        out_specs=pl.BlockSpec((S5_GROUPS, S5_GROUP_CH, b * nc), lambda s: (0, s, 0)),
        out_shape=jax.ShapeDtypeStruct((S5_GROUPS, S5_CHUNK * S5_GROUP_CH, b * nc), BF16),
        scratch_shapes=[pltpu.VMEM((PRIMARY_WIDTH, d), BF16)],
        compiler_params=_cparams(("arbitrary",)),
        name="s5_ut",
    )(xn, w_in)


def _s5_mix_kernel(x_ref, toep_ref, wst_ref, wout_ref, sr_ref, si_ref, d_ref, o_ref, *, nb):
    p = S5_STATE
    gs = x_ref.shape[0]
    lane = lax.broadcasted_iota(jnp.int32, (p, LANES), 1)
    n_steps = int(math.log2(LANES))

    def scan(g, hloc):
        pw = []
        for i in range(n_steps):
            keep = lane >= (1 << i)
            pw.append((jnp.where(keep, jnp.broadcast_to(sr_ref[g, :, i:i + 1], (p, LANES)), 0.0),
                       jnp.where(keep, jnp.broadcast_to(si_ref[g, :, i:i + 1], (p, LANES)), 0.0)))
        h_re = [hloc[:p, b * LANES:(b + 1) * LANES] for b in range(nb)]
        h_im = [hloc[p:, b * LANES:(b + 1) * LANES] for b in range(nb)]
        for i in range(n_steps):
            ar, ai = pw[i]
            r_sh = [pltpu.roll(v, 1 << i, 1) for v in h_re]
            i_sh = [pltpu.roll(v, 1 << i, 1) for v in h_im]
            h_re = [h_re[b] + ar * r_sh[b] - ai * i_sh[b] for b in range(nb)]
            h_im = [h_im[b] + ar * i_sh[b] + ai * r_sh[b] for b in range(nb)]
        h_re = [jnp.where(lane >= 1, pltpu.roll(v, 1, 1), 0.0) for v in h_re]
        h_im = [jnp.where(lane >= 1, pltpu.roll(v, 1, 1), 0.0) for v in h_im]
        return jnp.concatenate([jnp.concatenate(h_re, axis=1), jnp.concatenate(h_im, axis=1)],
                               axis=0).astype(BF16)

    def outputs(g, h):
        x = x_ref[g]
        y = (jnp.dot(toep_ref[g], x, preferred_element_type=F32)
             + jnp.dot(wout_ref[g], h, preferred_element_type=F32)
             + d_ref[g] * x.astype(F32))
        o_ref[:, g * S5_GROUP_CH:(g + 1) * S5_GROUP_CH, :] = (
            jax.nn.gelu(y).astype(o_ref.dtype).reshape(S5_CHUNK, S5_GROUP_CH, y.shape[1]))

    h_prev = None
    for g in range(gs):
        hloc = jnp.dot(wst_ref[g], x_ref[g], preferred_element_type=F32)
        h = scan(g, hloc)
        if h_prev is not None:
            outputs(g - 1, h_prev)
        h_prev = h
    outputs(gs - 1, h_prev)


def _s5_mix(xg, toep, wst, wout, sc_re, sc_im, dcol, *, nb, gs):
    g, tc, cols = xg.shape
    assert cols == nb * LANES, "one batch's chunks must fill exactly one 128-lane block"
    blk = lambda a: pl.BlockSpec((gs,) + a.shape[1:], lambda i: (i, 0, 0))
    return pl.pallas_call(
        functools.partial(_s5_mix_kernel, nb=nb),
        grid=(g // gs,),
        in_specs=[blk(a) for a in (xg, toep, wst, wout, sc_re, sc_im, dcol)],
        out_specs=pl.BlockSpec((S5_CHUNK, gs * S5_GROUP_CH, cols), lambda i: (0, i, 0)),
        out_shape=jax.ShapeDtypeStruct((S5_CHUNK, g * S5_GROUP_CH, cols), BF16),
        compiler_params=_cparams(("parallel",)),
        name="s5_mix",
    )(xg, toep, wst, wout, sc_re, sc_im, dcol)


def _glu_kernel(y_ref, w_ref, o_ref, *, col_chunk):
    y = y_ref[0].T
    half = o_ref.shape[-1]
    for c in range(half // col_chunk):
        wa = w_ref[:, c * col_chunk:(c + 1) * col_chunk].astype(BF16)
        wg = w_ref[:, half + c * col_chunk:half + (c + 1) * col_chunk].astype(BF16)
        a = jnp.dot(y, wa, preferred_element_type=F32)
        g = jnp.dot(y, wg, preferred_element_type=F32)
        o_ref[:, :, c * col_chunk:(c + 1) * col_chunk] = (
            (a * jax.nn.sigmoid(g)).astype(o_ref.dtype).reshape(o_ref.shape[:2] + (col_chunk,)))


def _glu(yt, w, layer, *, nb, col_chunk):
    t, k, cols = yt.shape
    nc = cols // nb
    half = w.shape[2] // 2
    return pl.pallas_call(
        functools.partial(_glu_kernel, col_chunk=col_chunk),
        grid=(t,),
        in_specs=[pl.BlockSpec((1, k, cols), lambda j: (j, 0, 0)),
                  pl.BlockSpec(*_layer_block(w, layer), pipeline_mode=pl.Buffered(1))],
        out_specs=pl.BlockSpec((nb, nc, half), lambda j: (0, 0, j)),
        out_shape=jax.ShapeDtypeStruct((nb, nc, t * half), BF16),
        compiler_params=_cparams(("parallel",)),
        name="glu",
    )(yt, w)


def _merge_kernel(x_ref, mix_ref, xq_ref, gate_ref, k_ref, v_ref, qg_ref, w_ref, o_ref, cat_ref, *, phased):
    gate = gate_ref[0]
    sg = gate * jax.nn.sigmoid(gate)
    if phased:
        nc = gate.shape[0] // S5_CHUNK
        for s in range(S5_CHUNK):
            rows = slice(s * nc, (s + 1) * nc)
            mix = mix_ref[0, :, s * PRIMARY_WIDTH:(s + 1) * PRIMARY_WIDTH]
            cat_ref[rows, :PRIMARY_WIDTH] = mix * sg[rows, :PRIMARY_WIDTH]
    else:
        cat_ref[:, :PRIMARY_WIDTH] = mix_ref[0] * sg[:, :PRIMARY_WIDTH]
    scale = X_HEAD_DIM ** -0.5
    for h in range(X_HEADS):
        sl = slice(h * X_HEAD_DIM, (h + 1) * X_HEAD_DIM)
        q = _rms(xq_ref[0, :, sl].astype(F32), qg_ref[...]).astype(BF16)
        s = lax.dot_general(q, k_ref[0, :, sl], (((1,), (1,)), ((), ())), preferred_element_type=F32) * scale
        p = jnp.exp(s - jnp.max(s, axis=-1, keepdims=True))
        p = (p / jnp.sum(p, axis=-1, keepdims=True)).astype(BF16)
        mo = jnp.dot(p, v_ref[0, :, sl], preferred_element_type=F32)
        osl = slice(PRIMARY_WIDTH + h * X_HEAD_DIM, PRIMARY_WIDTH + (h + 1) * X_HEAD_DIM)
        cat_ref[:, osl] = mo.astype(BF16) * sg[:, osl]
    delta = jnp.dot(cat_ref[...], w_ref[...].astype(BF16), preferred_element_type=F32)
    o_ref[0] = x_ref[0] + (_from_phase_order(delta) if phased else delta)


def _merge(x, mix, proj, xq_blk, gate_blk, mk, mv, xq_norm, w_out, layer, *, tm, phased=False):
    b, l, d = x.shape
    m = mk.shape[2]
    mix_spec = (pl.BlockSpec((1, tm // S5_CHUNK, S5_CHUNK * PRIMARY_WIDTH), lambda i, j: (i, j, 0)) if phased
                else pl.BlockSpec((1, tm, PRIMARY_WIDTH), lambda i, j: (i, j, 0)))
    return pl.pallas_call(
        functools.partial(_merge_kernel, phased=phased),
        grid=(b, l // tm),
        in_specs=[pl.BlockSpec((1, tm, d), lambda i, j: (i, j, 0)),
                  mix_spec,
                  pl.BlockSpec((1, tm, XQ_WIDTH), lambda i, j: (i, j, xq_blk)),
                  pl.BlockSpec((1, tm, BRANCH_WIDTH), lambda i, j: (i, j, gate_blk)),
                  pl.BlockSpec((None, 1, m, XQ_WIDTH), lambda i, j: (layer, i, 0, 0)),
                  pl.BlockSpec((None, 1, m, XQ_WIDTH), lambda i, j: (layer, i, 0, 0)),
                  pl.BlockSpec((None, 1, X_HEAD_DIM), lambda i, j: (layer, 0, 0)),
                  pl.BlockSpec((None, BRANCH_WIDTH, d), lambda i, j: (layer, 0, 0))],
        out_specs=pl.BlockSpec((1, tm, d), lambda i, j: (i, j, 0)),
        out_shape=jax.ShapeDtypeStruct((b, l, d), F32),
        scratch_shapes=[pltpu.VMEM((tm, BRANCH_WIDTH), BF16)],
        compiler_params=_cparams(("parallel", "parallel")),
        name="merge",
    )(x, mix, proj, proj, mk, mv, xq_norm.reshape(-1, 1, X_HEAD_DIM), w_out)


def _mla_qkv_kernel(cq_ref, ckv_ref, kr_ref, posr_ref, invfc_ref, gq_ref, gkv_ref, gqn_ref,
                    gkn_ref, gqr_ref, gkr_ref, wuq_ref, wukv_ref, qt_ref, kn_ref, krope_ref, vt_ref,
                    wqt_ref, wk_ref, wvt_ref):
    half = MLA_ROPE // 2
    tm = cq_ref.shape[1]
    qk = MLA_NOPE + MLA_ROPE

    @pl.when((pl.program_id(0) == 0) & (pl.program_id(1) == 0))
    def _():
        wqt_ref[...] = wuq_ref[...].T.astype(BF16)
        for h in range(MLA_HEADS):
            c0 = h * (MLA_NOPE + MLA_V)
            wk_ref[:, h * MLA_NOPE:(h + 1) * MLA_NOPE] = wukv_ref[:, c0:c0 + MLA_NOPE].astype(BF16)
            wvt_ref[h * MLA_V:(h + 1) * MLA_V, :] = wukv_ref[:, c0 + MLA_NOPE:c0 + MLA_NOPE + MLA_V].T.astype(BF16)

    qscale = (MLA_NOPE + MLA_ROPE) ** -0.5 * math.log2(math.e)

    cq = _rms(cq_ref[0].astype(F32), gq_ref[...])
    ckv = _rms(ckv_ref[0].astype(F32), gkv_ref[...])
    cq_t = cq.T.astype(BF16)
    ckv_t = ckv.T.astype(BF16)
    ckv_b = ckv.astype(BF16)

    def project(h):
        dot = functools.partial(jnp.dot, preferred_element_type=F32)
        k_pair = dot(ckv_b, wk_ref[:, h * MLA_NOPE:(h + 2) * MLA_NOPE]) if h % 2 == 0 else None
        return (dot(wqt_ref[h * qk:(h + 1) * qk, :], cq_t),
                dot(wvt_ref[h * MLA_V:(h + 1) * MLA_V, :], ckv_t), k_pair)

    ang_t = invfc_ref[...] * posr_ref[0].astype(F32)
    cos_t, sin_t = jnp.cos(ang_t), jnp.sin(ang_t)
    g_nope = jnp.broadcast_to(gqn_ref[...], (MLA_NOPE, tm)) * qscale
    g_r1 = jnp.broadcast_to(gqr_ref[:half, :], (half, tm)) * qscale
    g_r2 = jnp.broadcast_to(gqr_ref[half:, :], (half, tm)) * qscale
    ahead = 2
    pending = [project(h) for h in range(ahead)]
    for h in range(MLA_HEADS):
        if h + ahead < MLA_HEADS:
            pending.append(project(h + ahead))
        q, v_t, _ = pending[h]
        k_n = pending[h - h % 2][2][:, (h % 2) * MLA_NOPE:(h % 2 + 1) * MLA_NOPE]
        nope = q[:MLA_NOPE]
        r = lax.rsqrt(jnp.mean(nope * nope, axis=0, keepdims=True) + EPS)
        qt_ref[0, h, :MLA_NOPE, :] = (nope * r * g_nope).astype(BF16)
        x1, x2 = q[MLA_NOPE:MLA_NOPE + half], q[MLA_NOPE + half:MLA_NOPE + MLA_ROPE]
        ss = jnp.sum(x1 * x1, axis=0, keepdims=True) + jnp.sum(x2 * x2, axis=0, keepdims=True)
        r = lax.rsqrt(ss * (1.0 / MLA_ROPE) + EPS)
        x1, x2 = x1 * r * g_r1, x2 * r * g_r2
        qt_ref[0, h, MLA_NOPE:MLA_NOPE + half, :] = (x1 * cos_t - x2 * sin_t).astype(BF16)
        qt_ref[0, h, MLA_NOPE + half:MLA_NOPE + MLA_ROPE, :] = (x1 * sin_t + x2 * cos_t).astype(BF16)
        qt_ref[0, h, MLA_NOPE + MLA_ROPE:, :] = jnp.zeros((MLA_QK_PAD - MLA_NOPE - MLA_ROPE, tm), BF16)
        kn_ref[0, h] = _rms(k_n, gkn_ref[...]).astype(BF16)
        vt_ref[0, h] = v_t.astype(BF16)

    kr_t = kr_ref[0].astype(F32).T
    x1, x2 = kr_t[:half], kr_t[half:MLA_ROPE]
    ss = jnp.sum(x1 * x1, axis=0, keepdims=True) + jnp.sum(x2 * x2, axis=0, keepdims=True)
    r = lax.rsqrt(ss * (1.0 / MLA_ROPE) + EPS)
    x1, x2 = x1 * r * gkr_ref[:half, :], x2 * r * gkr_ref[half:, :]
    rot = jnp.concatenate([x1 * cos_t - x2 * sin_t, x1 * sin_t + x2 * cos_t,
                           jnp.zeros((LANES - MLA_ROPE, tm), F32)], axis=0)
    krope_ref[0] = rot.T.astype(BF16)


def _mla_qkv(proj, cq_blk, ckv_blk, kr_blk, positions, gains, w_uq, w_ukv, layer, *, tm):
    b, l, _ = proj.shape
    hh = MLA_HEADS
    half = MLA_ROPE // 2
    inv_freq = ROPE_THETA ** (-jnp.arange(half, dtype=F32) / half)
    const = lambda a: pl.BlockSpec(a.shape, lambda i, j: (0,) * a.ndim)
    gq, gkv, gqn, gkn, gqr, gkr = gains
    consts = [inv_freq.reshape(half, 1), gq.reshape(1, -1), gkv.reshape(1, -1), gqn.reshape(-1, 1),
              gkn.reshape(1, -1), gqr.reshape(-1, 1), gkr.reshape(-1, 1)]
    weights = [w_uq, w_ukv]
    return pl.pallas_call(
        _mla_qkv_kernel,
        grid=(b, l // tm),
        in_specs=[pl.BlockSpec((1, tm, MLA_Q_LORA), lambda i, j: (i, j, cq_blk)),
                  pl.BlockSpec((1, tm, MLA_KV_LORA), lambda i, j: (i, j, ckv_blk)),
                  pl.BlockSpec((1, tm, LANES), lambda i, j: (i, j, kr_blk)),
                  pl.BlockSpec((1, 1, tm), lambda i, j: (i, 0, j))] + [const(a) for a in consts]
                 + [pl.BlockSpec(*_layer_block(w, layer)) for w in weights],
        out_specs=[pl.BlockSpec((1, hh, MLA_QK_PAD, tm), lambda i, j: (i, 0, 0, j)),
                   pl.BlockSpec((1, hh, tm, MLA_NOPE), lambda i, j: (i, 0, j, 0)),
                   pl.BlockSpec((1, tm, LANES), lambda i, j: (i, j, 0)),
                   pl.BlockSpec((1, hh, MLA_V, tm), lambda i, j: (i, 0, 0, j))],
        out_shape=[jax.ShapeDtypeStruct((b, hh, MLA_QK_PAD, l), BF16),
                   jax.ShapeDtypeStruct((b, hh, l, MLA_NOPE), BF16),
                   jax.ShapeDtypeStruct((b, l, LANES), BF16),
                   jax.ShapeDtypeStruct((b, hh, MLA_V, l), BF16)],
        scratch_shapes=[pltpu.VMEM((hh * (MLA_NOPE + MLA_ROPE), MLA_Q_LORA), BF16),
                        pltpu.VMEM((MLA_KV_LORA, hh * MLA_NOPE), BF16),
                        pltpu.VMEM((hh * MLA_V, MLA_KV_LORA), BF16)],
        compiler_params=_cparams(("arbitrary", "arbitrary")),
        name="mla_qkv",
    )(proj, proj, proj, positions.reshape(b, 1, l), *consts, *weights)


def _flash_kernel(qt_ref, kn_ref, kr_ref, vt_ref, o_ref, m_ref, l_ref, acc_ref, *, tq, hp, ahead):
    qi = pl.program_id(2)
    m_ref[...] = jnp.full(m_ref.shape, -jnp.inf, F32)
    l_ref[...] = jnp.zeros(l_ref.shape, F32)
    acc_ref[...] = jnp.zeros(acc_ref.shape, F32)

    half = tq // 2
    lower = (lax.broadcasted_iota(jnp.int32, (half, half), 0)
             <= lax.broadcasted_iota(jnp.int32, (half, half), 1))

    def blocks(j, parts, diagonal):
        base = pl.multiple_of(j * tq, tq)
        items = [(h, pl.ds(base + k0, nk), slice(q0, q0 + nq)) for k0, nk, q0, nq in parts for h in range(hp)]

        def scores(h, rows, cols):
            k = jnp.concatenate([kn_ref[0, h, rows, :], kr_ref[0, rows, :]], axis=-1)
            return jnp.dot(k, qt_ref[0, h, :, cols], preferred_element_type=F32)

        pending = [scores(*it) for it in items[:ahead]]
        for n, (h, rows, cols) in enumerate(items):
            if n + ahead < len(items):
                pending.append(scores(*items[n + ahead]))
            s = pending[n]
            if diagonal:
                square = jnp.where(lower, s[:, :half], jnp.finfo(F32).min)
                s = square if s.shape[1] == half else jnp.concatenate([square, s[:, half:]], axis=1)
            m = m_ref[h, :, cols]
            m_new = jnp.maximum(m, jnp.max(s, axis=0, keepdims=True))
            alpha = jnp.exp2(m - m_new)
            p = jnp.exp2(s - m_new)
            l_ref[h, :, cols] = alpha * l_ref[h, :, cols] + jnp.sum(p, axis=0, keepdims=True)
            acc_ref[h, :, cols] = alpha * acc_ref[h, :, cols] + jnp.dot(
                vt_ref[0, h, :, rows], p.astype(BF16), preferred_element_type=F32)
            m_ref[h, :, cols] = m_new

    def body(j, carry):
        blocks(j, [(0, tq, 0, tq)], False)
        return carry

    lax.fori_loop(0, qi, body, 0)
    blocks(qi, [(0, half, 0, tq), (half, half, half, half)], True)
    for h in range(hp):
        o_ref[0, :, h * MLA_V:(h + 1) * MLA_V] = (acc_ref[h] / l_ref[h]).T.astype(o_ref.dtype)


def _flash(qt, kn, kr, vt, *, tq, hp, ahead):
    b, hh, _, l = qt.shape
    return pl.pallas_call(
        functools.partial(_flash_kernel, tq=tq, hp=hp, ahead=ahead),
        grid=(b, hh // hp, l // tq),
        in_specs=[pl.BlockSpec((1, hp, MLA_QK_PAD, tq), lambda i, h, j: (i, h, 0, j)),
                  pl.BlockSpec((1, hp, l, MLA_NOPE), lambda i, h, j: (i, h, 0, 0)),
                  pl.BlockSpec((1, l, LANES), lambda i, h, j: (i, 0, 0)),
                  pl.BlockSpec((1, hp, MLA_V, l), lambda i, h, j: (i, h, 0, 0))],
        out_specs=pl.BlockSpec((1, tq, hp * MLA_V), lambda i, h, j: (i, j, h)),
        out_shape=jax.ShapeDtypeStruct((b, l, hh * MLA_V), BF16),
        scratch_shapes=[pltpu.VMEM((hp, 1, tq), F32), pltpu.VMEM((hp, 1, tq), F32),
                        pltpu.VMEM((hp, MLA_V, tq), F32)],
        compiler_params=_cparams(("parallel", "parallel", "parallel")),
        name="flash",
    )(qt, kn, kr, vt)


def _s5_layer(x, ln, w_in, lam_re, lam_im, log_step, b_re, b_im, c_re, c_im, d, w_glu,
              w_out, mem_kv, xq_norm, layer, j):
    b, l, dm = x.shape
    tm = 512
    proj, xn = _s5_in_proj(x, ln, w_in, j, tm=tm, col_chunk=512)
    xg = _s5_ut(xn, w_in, j)
    pw_re, pw_im = _s5_pow(lam_re, lam_im, log_step)
    toep, wout, wst = _s5_asm(pw_re.transpose(1, 0, 2), pw_im.transpose(1, 0, 2), c_re, c_im, b_re, b_im)
    n_scan = int(math.log2(LANES))
    first = S5_EXPONENTS.index(S5_CHUNK)
    col = lambda pw: jnp.pad(pw[first:first + n_scan].transpose(1, 2, 0), ((0, 0), (0, 0), (0, 8 - n_scan)))
    dcol = jnp.tile(d.reshape(S5_GROUPS, 1, S5_GROUP_CH), (1, S5_CHUNK, 1)).reshape(S5_GROUPS, -1, 1)
    yt = _s5_mix(xg, toep, wst, wout, col(pw_re), col(pw_im), dcol, nb=b, gs=4)
    y = _glu(yt, w_glu, j, nb=b, col_chunk=256)
    return _merge(x, y, proj, BRANCH_WIDTH // XQ_WIDTH, 0, *mem_kv, xq_norm, w_out, layer, tm=tm, phased=True)


def _mla_layer(x, positions, ln, w_in, q_lora_norm, kv_lora_norm, w_uq, w_ukv, q_nope_norm, k_nope_norm,
               q_rope_norm, k_rope_norm, w_out, mem_kv, xq_norm, layer, j):
    b, l, dm = x.shape
    o1 = MLA_Q_LORA
    o2 = o1 + MLA_KV_LORA
    o3 = o2 + MLA_ROPE
    o4 = o3 + XQ_WIDTH
    segments = ((o4, BRANCH_WIDTH), (0, o1), (o3, XQ_WIDTH), (o1, MLA_KV_LORA), (o2, MLA_ROPE))
    wout = -(-(o4 + BRANCH_WIDTH) // 512) * 512
    proj = _mla_in_proj(x.reshape(b * l, dm), ln, w_in, j, segments, wout, tm=512, col_chunk=512)
    proj = proj.reshape(b, l, -1)
    gate_blk = 0
    cq_blk = BRANCH_WIDTH // MLA_Q_LORA
    xq_blk = (BRANCH_WIDTH + MLA_Q_LORA) // XQ_WIDTH
    ckv_blk = (BRANCH_WIDTH + MLA_Q_LORA + XQ_WIDTH) // MLA_KV_LORA
    kr_blk = (BRANCH_WIDTH + MLA_Q_LORA + XQ_WIDTH + MLA_KV_LORA) // LANES
    qt, kn, kr, vt = _mla_qkv(proj, cq_blk, ckv_blk, kr_blk, positions,
                              (q_lora_norm, kv_lora_norm, q_nope_norm, k_nope_norm, q_rope_norm, k_rope_norm),
                              w_uq, w_ukv, j, tm=256)
    attn = _flash(qt, kn, kr, vt, tq=512, hp=12, ahead=2)
    return _merge(x, attn, proj, xq_blk, gate_blk, *mem_kv, xq_norm, w_out, layer, tm=512)


def kernel(x, mem, positions, ln_gain, w_out, mem_norm, w_mem_kv, xq_norm, xk_norm,
           s5_w_in, s5_lambda_re, s5_lambda_im, s5_log_step, s5_b_re, s5_b_im, s5_c_re, s5_c_im,
           s5_d, s5_w_glu, mla_w_in, mla_q_lora_norm, mla_kv_lora_norm, mla_w_uq, mla_w_ukv,
           mla_q_nope_norm, mla_k_nope_norm, mla_q_rope_norm, mla_k_rope_norm):
    depth = ln_gain.shape[0]
    mem_kv = _mem_kv(mem, mem_norm, w_mem_kv, xk_norm)
    for i in range(depth):
        j = i // 2
        if i % 2 == 0:
            x = _s5_layer(x, ln_gain[i], s5_w_in, s5_lambda_re[j], s5_lambda_im[j], s5_log_step[j],
                          s5_b_re[j], s5_b_im[j], s5_c_re[j], s5_c_im[j], s5_d[j], s5_w_glu,
                          w_out, mem_kv, xq_norm, i, j)
        else:
            x = _mla_layer(x, positions, ln_gain[i], mla_w_in, mla_q_lora_norm[j], mla_kv_lora_norm[j],
                           mla_w_uq, mla_w_ukv, mla_q_nope_norm[j], mla_k_nope_norm[j],
                           mla_q_rope_norm[j], mla_k_rope_norm[j],
                           w_out, mem_kv, xq_norm, i, j)
    return x
```

```python
import functools
import math

import jax
import jax.numpy as jnp
from jax import lax
from jax.experimental import pallas as pl
from jax.experimental.pallas import tpu as pltpu

D_MODEL = 1024
BRANCH_WIDTH = 2 * D_MODEL
XQ_WIDTH = BRANCH_WIDTH // 4
PRIMARY_WIDTH = BRANCH_WIDTH - XQ_WIDTH
X_HEADS = 4
X_HEAD_DIM = XQ_WIDTH // X_HEADS
S5_GROUP_CH = 16
S5_GROUPS = PRIMARY_WIDTH // S5_GROUP_CH
S5_STATE = 64
MLA_NOPE = 128
MLA_ROPE = 64
MLA_V = 128
MLA_HEADS = PRIMARY_WIDTH // MLA_V
MLA_Q_LORA = D_MODEL // 2
MLA_KV_LORA = D_MODEL // 4
ROPE_THETA = 10000.0
EPS = 1e-6

LANES = 128
MLA_QK_PAD = 2 * LANES
S5_CHUNK = 2 * LANES // S5_GROUP_CH
S5_EXPONENTS = list(range(S5_CHUNK + 1)) + [S5_CHUNK * 2 ** i for i in range(1, int(math.log2(LANES)))]
VMEM_LIMIT = 56 * 1024 * 1024

F32 = jnp.float32
BF16 = jnp.bfloat16


def _cparams(sem):
    return pltpu.CompilerParams(dimension_semantics=sem, vmem_limit_bytes=VMEM_LIMIT)


def _rms(x, g):
    return x * lax.rsqrt(jnp.mean(x * x, axis=-1, keepdims=True) + EPS) * g


def _layer_block(w, j, block=None, index=None):
    block = tuple(w.shape[1:]) if block is None else block
    index = (0,) * len(block) if index is None else index
    return (None,) + block, lambda *_: (j,) + index


def _mla_in_proj_kernel(x_ref, g_ref, w_ref, o_ref, wp_ref, *, segments, col_chunk):
    @pl.when(pl.program_id(0) == 0)
    def _():
        at = 0
        for start, width in segments:
            wp_ref[:, at:at + width] = w_ref[:, start:start + width].astype(BF16)
            at += width
        wp_ref[:, at:] = jnp.zeros((wp_ref.shape[0], wp_ref.shape[1] - at), BF16)

    xn = _rms(x_ref[...], g_ref[...]).astype(BF16)
    for c in range(o_ref.shape[1] // col_chunk):
        sl = slice(c * col_chunk, (c + 1) * col_chunk)
        o_ref[:, sl] = jnp.dot(xn, wp_ref[:, sl], preferred_element_type=F32).astype(o_ref.dtype)


def _mla_in_proj(x, g, w, j, segments, wout, *, tm, col_chunk):
    n, d = x.shape
    return pl.pallas_call(
        functools.partial(_mla_in_proj_kernel, segments=segments, col_chunk=col_chunk),
        grid=(n // tm,),
        in_specs=[pl.BlockSpec((tm, d), lambda i: (i, 0)),
                  pl.BlockSpec((1, d), lambda i: (0, 0)),
                  pl.BlockSpec(*_layer_block(w, j), pipeline_mode=pl.Buffered(1))],
        out_specs=pl.BlockSpec((tm, wout), lambda i: (i, 0)),
        out_shape=jax.ShapeDtypeStruct((n, wout), BF16),
        scratch_shapes=[pltpu.VMEM((d, wout), BF16)],
        compiler_params=_cparams(("arbitrary",)),
        name="mla_in_proj",
    )(x, g.reshape(1, d), w)


def _to_phase_order(a):
    n, d = a.shape
    return jnp.swapaxes(a.reshape(n // S5_CHUNK, S5_CHUNK, d), 0, 1).reshape(n, d)


def _from_phase_order(a):
    n, d = a.shape
    return jnp.swapaxes(a.reshape(S5_CHUNK, n // S5_CHUNK, d), 0, 1).reshape(n, d)


def _s5_in_proj_kernel(x_ref, g_ref, wg_ref, wx_ref, o_ref, xn_ref, *, col_chunk):
    tm, d = x_ref.shape[1:]
    nc = tm // S5_CHUNK
    xn = _rms(_to_phase_order(x_ref[0]), g_ref[...]).astype(BF16)
    for s in range(S5_CHUNK):
        xn_ref[0, :, s * d:(s + 1) * d] = xn[s * nc:(s + 1) * nc]
    for c in range(BRANCH_WIDTH // col_chunk):
        sl = slice(c * col_chunk, (c + 1) * col_chunk)
        o_ref[0, :, sl] = jnp.dot(xn, wg_ref[:, sl].astype(BF16), preferred_element_type=F32).astype(o_ref.dtype)
    o_ref[0, :, BRANCH_WIDTH:] = jnp.dot(xn, wx_ref[...].astype(BF16),
                                         preferred_element_type=F32).astype(o_ref.dtype)


def _s5_in_proj(x, g, w_in, j, *, tm, col_chunk):
    b, l, d = x.shape
    wout = BRANCH_WIDTH + XQ_WIDTH
    nc = tm // S5_CHUNK
    return pl.pallas_call(
        functools.partial(_s5_in_proj_kernel, col_chunk=col_chunk),
        grid=(b, l // tm),
        in_specs=[pl.BlockSpec((1, tm, d), lambda i, j: (i, j, 0)),
                  pl.BlockSpec((1, d), lambda i, j: (0, 0)),
                  pl.BlockSpec(*_layer_block(w_in, j, (d, BRANCH_WIDTH),
                                             (0, (PRIMARY_WIDTH + XQ_WIDTH) // BRANCH_WIDTH))),
                  pl.BlockSpec(*_layer_block(w_in, j, (d, XQ_WIDTH), (0, PRIMARY_WIDTH // XQ_WIDTH)))],
        out_specs=[pl.BlockSpec((1, tm, wout), lambda i, j: (i, j, 0)),
                   pl.BlockSpec((1, nc, S5_CHUNK * d), lambda i, j: (i, j, 0))],
        out_shape=[jax.ShapeDtypeStruct((b, l, wout), BF16),
                   jax.ShapeDtypeStruct((b, l // S5_CHUNK, S5_CHUNK * d), BF16)],
        compiler_params=_cparams(("parallel", "parallel")),
        name="s5_in_proj",
    )(x, g.reshape(1, d), w_in, w_in)


def _mem_kv_kernel(m_ref, g_ref, w_ref, kg_ref, k_ref, v_ref):
    mn = _rms(m_ref[0], g_ref[0]).astype(BF16)
    kv = jnp.dot(mn, w_ref[0].astype(BF16), preferred_element_type=F32)
    for h in range(X_HEADS):
        sl = slice(h * X_HEAD_DIM, (h + 1) * X_HEAD_DIM)
        k_ref[0, 0, :, sl] = _rms(kv[:, sl], kg_ref[0]).astype(BF16)
    v_ref[0, 0] = kv[:, XQ_WIDTH:].astype(BF16)


def _mem_kv(mem, mem_norm, w_mem_kv, xk_norm):
    b, m, d = mem.shape
    depth = w_mem_kv.shape[0]
    out = jax.ShapeDtypeStruct((depth, b, m, XQ_WIDTH), BF16)
    return pl.pallas_call(
        _mem_kv_kernel,
        grid=(depth, b),
        in_specs=[pl.BlockSpec((1, m, d), lambda n, i: (i, 0, 0)),
                  pl.BlockSpec((1, 1, d), lambda n, i: (n, 0, 0)),
                  pl.BlockSpec((1, d, 2 * XQ_WIDTH), lambda n, i: (n, 0, 0)),
                  pl.BlockSpec((1, 1, X_HEAD_DIM), lambda n, i: (n, 0, 0))],
        out_specs=[pl.BlockSpec((1, 1, m, XQ_WIDTH), lambda n, i: (n, i, 0, 0)),
                   pl.BlockSpec((1, 1, m, XQ_WIDTH), lambda n, i: (n, i, 0, 0))],
        out_shape=[out, out],
        compiler_params=_cparams(("parallel", "parallel")),
        name="mem_kv",
    )(mem, mem_norm.reshape(depth, 1, d), w_mem_kv, xk_norm.reshape(depth, 1, X_HEAD_DIM))


def _s5_pow_kernel(lr_ref, li_ref, ls_ref, pr_ref, pi_ref):
    lr, li = lr_ref[...], li_ref[...]
    step = jnp.exp(ls_ref[...])
    zr, zi = lr * step, li * step
    for n, e in enumerate(S5_EXPONENTS):
        mag = jnp.exp(zr * e)
        pr_ref[n] = mag * jnp.cos(zi * e)
        pi_ref[n] = mag * jnp.sin(zi * e)
    ar, ai = pr_ref[1], pi_ref[1]
    den = lr * lr + li * li
    pr_ref[len(S5_EXPONENTS)] = ((ar - 1.0) * lr + ai * li) / den
    pi_ref[len(S5_EXPONENTS)] = (ai * lr - (ar - 1.0) * li) / den


def _s5_pow(lam_re, lam_im, log_step):
    g, p = lam_re.shape
    out = jax.ShapeDtypeStruct((len(S5_EXPONENTS) + 1, g, p), F32)
    return pl.pallas_call(_s5_pow_kernel, out_shape=[out, out], name="s5_pow")(
        lam_re, lam_im, log_step.reshape(g, 1))


def _s5_asm_kernel(pr_ref, pi_ref, cr_ref, ci_ref, btr_ref, bti_ref, br_ref, bi_ref,
                   toep_ref, wout_ref, wst_ref):
    def group(i, carry):
        _s5_asm_group(i, pr_ref, pi_ref, cr_ref, ci_ref, btr_ref, bti_ref, br_ref, bi_ref,
                      toep_ref, wout_ref, wst_ref)
        return carry

    lax.fori_loop(0, pr_ref.shape[0], group, 0)


def _dot_3pass_tiled(a, b):
    c = b.shape[1]
    rep = ((lax.broadcasted_iota(jnp.int32, (c, S5_CHUNK * c), 1) & (c - 1))
           == lax.broadcasted_iota(jnp.int32, (c, S5_CHUNK * c), 0)).astype(BF16)
    dot = functools.partial(jnp.dot, preferred_element_type=F32)
    a_hi, b_hi = a.astype(BF16), b.astype(BF16)
    a_lo = (a - a_hi.astype(F32)).astype(BF16)
    b_lo = (b - b_hi.astype(F32)).astype(BF16)
    b_hi, b_lo = dot(b_hi, rep).astype(BF16), dot(b_lo, rep).astype(BF16)
    return dot(a_hi, b_hi) + (dot(a_hi, b_lo) + dot(a_lo, b_hi))


def _s5_asm_group(i, pr_ref, pi_ref, cr_ref, ci_ref, btr_ref, bti_ref, br_ref, bi_ref,
                  toep_ref, wout_ref, wst_ref):
    t = S5_CHUNK
    pr, pi = pr_ref[i], pi_ref[i]
    cr, ci = cr_ref[i], ci_ref[i]
    btr, bti = btr_ref[i], bti_ref[i]
    mr, mi = pr[len(S5_EXPONENTS):], pi[len(S5_EXPONENTS):]
    amr = pr[:t] * mr - pi[:t] * mi
    ami = pr[:t] * mi + pi[:t] * mr
    l_re, l_im, w_re, w_im, o_re, o_im = [], [], [], [], [], []
    for k in range(t):
        ar, ai = amr[k:k + 1], ami[k:k + 1]
        l_re.append(cr * ar - ci * ai)
        l_im.append(-(cr * ai + ci * ar))
        ar, ai = amr[t - 1 - k:t - k], ami[t - 1 - k:t - k]
        w_re.append(btr * ar - bti * ai)
        w_im.append(btr * ai + bti * ar)
        ar, ai = pr[k + 1:k + 2], pi[k + 1:k + 2]
        o_re.append(cr * ar - ci * ai)
        o_im.append(-(cr * ai + ci * ar))
    cat = lambda parts: jnp.concatenate(parts, axis=0)
    kt = _dot_3pass_tiled(jnp.concatenate([cat(l_re), cat(l_im)], axis=1),
                          jnp.concatenate([br_ref[i], bi_ref[i]], axis=0))
    n = kt.shape[0]
    blk = lax.shift_right_logical(lax.broadcasted_iota(jnp.int32, kt.shape, 1), int(math.log2(S5_GROUP_CH)))
    toep = jnp.where(blk == 0, kt, 0.0)
    for s in range(1, t):
        shifted = jnp.concatenate([jnp.zeros((s * S5_GROUP_CH, n), F32), kt[:n - s * S5_GROUP_CH]], axis=0)
        toep = jnp.where(blk == s, shifted, toep)
    toep_ref[i] = toep.astype(BF16)
    wout_ref[i] = jnp.concatenate([cat(o_re), cat(o_im)], axis=1).astype(BF16)
    wst_ref[i] = jnp.concatenate([cat(w_re), cat(w_im)], axis=1).T.astype(BF16)


def _s5_asm(pw_re, pw_im, c_re, c_im, b_re, b_im):
    g, c, p = c_re.shape
    tc = S5_CHUNK * c
    gs = 8
    blk = lambda a: pl.BlockSpec((gs,) + a.shape[1:], lambda i: (i, 0, 0))
    bt_re, bt_im = b_re.transpose(0, 2, 1), b_im.transpose(0, 2, 1)
    args = (pw_re, pw_im, c_re, c_im, bt_re, bt_im, b_re, b_im)
    return pl.pallas_call(
        _s5_asm_kernel,
        grid=(g // gs,),
        in_specs=[blk(a) for a in args],
        out_specs=[pl.BlockSpec((gs, tc, tc), lambda i: (i, 0, 0)),
                   pl.BlockSpec((gs, tc, 2 * p), lambda i: (i, 0, 0)),
                   pl.BlockSpec((gs, 2 * p, tc), lambda i: (i, 0, 0))],
        out_shape=[jax.ShapeDtypeStruct((g, tc, tc), BF16), jax.ShapeDtypeStruct((g, tc, 2 * p), BF16),
                   jax.ShapeDtypeStruct((g, 2 * p, tc), BF16)],
        compiler_params=_cparams(("parallel",)),
        name="s5_asm",
    )(*args)


def _s5_ut_kernel(xn_ref, w_ref, o_ref, wt_ref):
    @pl.when(pl.program_id(0) == 0)
    def _():
        wt_ref[...] = w_ref[...].T.astype(BF16)

    nb, nc, d = xn_ref.shape
    ut = lax.dot_general(wt_ref[...], xn_ref[...].reshape(nb * nc, d), (((1,), (1,)), ((), ())),
                         preferred_element_type=F32)
    o_ref[...] = ut.astype(BF16).reshape(o_ref.shape)


def _s5_ut(xn, w_in, j):
    b, nc, td = xn.shape
    d = td // S5_CHUNK
    return pl.pallas_call(
        _s5_ut_kernel,
        grid=(S5_CHUNK,),
        in_specs=[pl.BlockSpec((b, nc, d), lambda s: (0, 0, s)),
                  pl.BlockSpec(*_layer_block(w_in, j, (d, PRIMARY_WIDTH)))],
        out_specs=pl.BlockSpec((S5_GROUPS, S5_GROUP_CH, b * nc), lambda s: (0, s, 0)),
        out_shape=jax.ShapeDtypeStruct((S5_GROUPS, S5_CHUNK * S5_GROUP_CH, b * nc), BF16),
        scratch_shapes=[pltpu.VMEM((PRIMARY_WIDTH, d), BF16)],
        compiler_params=_cparams(("arbitrary",)),
        name="s5_ut",
    )(xn, w_in)


def _s5_mix_kernel(x_ref, toep_ref, wst_ref, wout_ref, sr_ref, si_ref, d_ref, o_ref, *, nb):
    p = S5_STATE
    gs = x_ref.shape[0]
    lane = lax.broadcasted_iota(jnp.int32, (p, LANES), 1)
    n_steps = int(math.log2(LANES))

    def scan(g, hloc):
        pw = []
        for i in range(n_steps):
            keep = lane >= (1 << i)
            pw.append((jnp.where(keep, jnp.broadcast_to(sr_ref[g, :, i:i + 1], (p, LANES)), 0.0),
                       jnp.where(keep, jnp.broadcast_to(si_ref[g, :, i:i + 1], (p, LANES)), 0.0)))
        h_re = [hloc[:p, b * LANES:(b + 1) * LANES] for b in range(nb)]
        h_im = [hloc[p:, b * LANES:(b + 1) * LANES] for b in range(nb)]
        for i in range(n_steps):
            ar, ai = pw[i]
            r_sh = [pltpu.roll(v, 1 << i, 1) for v in h_re]
            i_sh = [pltpu.roll(v, 1 << i, 1) for v in h_im]
            h_re = [h_re[b] + ar * r_sh[b] - ai * i_sh[b] for b in range(nb)]
            h_im = [h_im[b] + ar * i_sh[b] + ai * r_sh[b] for b in range(nb)]
        h_re = [jnp.where(lane >= 1, pltpu.roll(v, 1, 1), 0.0) for v in h_re]
        h_im = [jnp.where(lane >= 1, pltpu.roll(v, 1, 1), 0.0) for v in h_im]
        return jnp.concatenate([jnp.concatenate(h_re, axis=1), jnp.concatenate(h_im, axis=1)],
                               axis=0).astype(BF16)

    def outputs(g, h):
        x = x_ref[g]
        y = (jnp.dot(toep_ref[g], x, preferred_element_type=F32)
             + jnp.dot(wout_ref[g], h, preferred_element_type=F32)
             + d_ref[g] * x.astype(F32))
        o_ref[:, g * S5_GROUP_CH:(g + 1) * S5_GROUP_CH, :] = (
            jax.nn.gelu(y).astype(o_ref.dtype).reshape(S5_CHUNK, S5_GROUP_CH, y.shape[1]))

    h_prev = None
    for g in range(gs):
        hloc = jnp.dot(wst_ref[g], x_ref[g], preferred_element_type=F32)
        h = scan(g, hloc)
        if h_prev is not None:
            outputs(g - 1, h_prev)
        h_prev = h
    outputs(gs - 1, h_prev)


def _s5_mix(xg, toep, wst, wout, sc_re, sc_im, dcol, *, nb, gs):
    g, tc, cols = xg.shape
    assert cols == nb * LANES, "one batch's chunks must fill exactly one 128-lane block"
    blk = lambda a: pl.BlockSpec((gs,) + a.shape[1:], lambda i: (i, 0, 0))
    return pl.pallas_call(
        functools.partial(_s5_mix_kernel, nb=nb),
        grid=(g // gs,),
        in_specs=[blk(a) for a in (xg, toep, wst, wout, sc_re, sc_im, dcol)],
        out_specs=pl.BlockSpec((S5_CHUNK, gs * S5_GROUP_CH, cols), lambda i: (0, i, 0)),
        out_shape=jax.ShapeDtypeStruct((S5_CHUNK, g * S5_GROUP_CH, cols), BF16),
        compiler_params=_cparams(("parallel",)),
        name="s5_mix",
    )(xg, toep, wst, wout, sc_re, sc_im, dcol)


def _glu_kernel(y_ref, w_ref, o_ref, *, col_chunk):
    y = y_ref[0].T
    half = o_ref.shape[-1]
    for c in range(half // col_chunk):
        wa = w_ref[:, c * col_chunk:(c + 1) * col_chunk].astype(BF16)
        wg = w_ref[:, half + c * col_chunk:half + (c + 1) * col_chunk].astype(BF16)
        a = jnp.dot(y, wa, preferred_element_type=F32)
        g = jnp.dot(y, wg, preferred_element_type=F32)
        o_ref[:, :, c * col_chunk:(c + 1) * col_chunk] = (
            (a * jax.nn.sigmoid(g)).astype(o_ref.dtype).reshape(o_ref.shape[:2] + (col_chunk,)))


def _glu(yt, w, layer, *, nb, col_chunk):
    t, k, cols = yt.shape
    nc = cols // nb
    half = w.shape[2] // 2
    return pl.pallas_call(
        functools.partial(_glu_kernel, col_chunk=col_chunk),
        grid=(t,),
        in_specs=[pl.BlockSpec((1, k, cols), lambda j: (j, 0, 0)),
                  pl.BlockSpec(*_layer_block(w, layer), pipeline_mode=pl.Buffered(1))],
        out_specs=pl.BlockSpec((nb, nc, half), lambda j: (0, 0, j)),
        out_shape=jax.ShapeDtypeStruct((nb, nc, t * half), BF16),
        compiler_params=_cparams(("parallel",)),
        name="glu",
    )(yt, w)


def _merge_kernel(x_ref, mix_ref, xq_ref, gate_ref, k_ref, v_ref, qg_ref, w_ref, o_ref, *cat_refs, tm, phased):
    scale = X_HEAD_DIM ** -0.5
    nc = tm // S5_CHUNK
    w = w_ref[...].astype(BF16)

    def gather(k, cat_ref):
        r0 = k * tm
        gate = gate_ref[0, r0:r0 + tm, :]
        sg = gate * jax.nn.sigmoid(gate)
        if phased:
            for s in range(S5_CHUNK):
                rows = slice(s * nc, (s + 1) * nc)
                mix = mix_ref[0, k * nc:(k + 1) * nc, s * PRIMARY_WIDTH:(s + 1) * PRIMARY_WIDTH]
                cat_ref[rows, :PRIMARY_WIDTH] = mix * sg[rows, :PRIMARY_WIDTH]
        else:
            cat_ref[:, :PRIMARY_WIDTH] = mix_ref[0, r0:r0 + tm, :] * sg[:, :PRIMARY_WIDTH]
        for h in range(X_HEADS):
            sl = slice(h * X_HEAD_DIM, (h + 1) * X_HEAD_DIM)
            q = _rms(xq_ref[0, r0:r0 + tm, sl].astype(F32), qg_ref[...]).astype(BF16)
            s = lax.dot_general(q, k_ref[0, :, sl], (((1,), (1,)), ((), ())), preferred_element_type=F32) * scale
            p = jnp.exp(s - jnp.max(s, axis=-1, keepdims=True))
            p = (p / jnp.sum(p, axis=-1, keepdims=True)).astype(BF16)
            mo = jnp.dot(p, v_ref[0, :, sl], preferred_element_type=F32)
            osl = slice(PRIMARY_WIDTH + h * X_HEAD_DIM, PRIMARY_WIDTH + (h + 1) * X_HEAD_DIM)
            cat_ref[:, osl] = mo.astype(BF16) * sg[:, osl]

    def project(k, cat_ref):
        r0 = k * tm
        delta = jnp.dot(cat_ref[...], w, preferred_element_type=F32)
        o_ref[0, r0:r0 + tm, :] = x_ref[0, r0:r0 + tm, :] + (_from_phase_order(delta) if phased else delta)

    n_sub = len(cat_refs)
    for k in range(n_sub):
        gather(k, cat_refs[k])
        if k > 0:
            project(k - 1, cat_refs[k - 1])
    project(n_sub - 1, cat_refs[n_sub - 1])


def _merge(x, mix, proj, xq_blk, gate_blk, mk, mv, xq_norm, w_out, layer, *, tm, phased=False):
    b, l, d = x.shape
    m = mk.shape[2]
    n_sub = 2
    tg = n_sub * tm
    mix_spec = (pl.BlockSpec((1, tg // S5_CHUNK, S5_CHUNK * PRIMARY_WIDTH), lambda i, j: (i, j, 0)) if phased
                else pl.BlockSpec((1, tg, PRIMARY_WIDTH), lambda i, j: (i, j, 0)))
    return pl.pallas_call(
        functools.partial(_merge_kernel, tm=tm, phased=phased),
        grid=(b, l // tg),
        in_specs=[pl.BlockSpec((1, tg, d), lambda i, j: (i, j, 0)),
                  mix_spec,
                  pl.BlockSpec((1, tg, XQ_WIDTH), lambda i, j: (i, j, xq_blk)),
                  pl.BlockSpec((1, tg, BRANCH_WIDTH), lambda i, j: (i, j, gate_blk)),
                  pl.BlockSpec((None, 1, m, XQ_WIDTH), lambda i, j: (layer, i, 0, 0)),
                  pl.BlockSpec((None, 1, m, XQ_WIDTH), lambda i, j: (layer, i, 0, 0)),
                  pl.BlockSpec((None, 1, X_HEAD_DIM), lambda i, j: (layer, 0, 0)),
                  pl.BlockSpec((None, BRANCH_WIDTH, d), lambda i, j: (layer, 0, 0),
                               pipeline_mode=pl.Buffered(1))],
        out_specs=pl.BlockSpec((1, tg, d), lambda i, j: (i, j, 0)),
        out_shape=jax.ShapeDtypeStruct((b, l, d), F32),
        scratch_shapes=[pltpu.VMEM((tm, BRANCH_WIDTH), BF16)] * n_sub,
        compiler_params=_cparams(("parallel", "parallel")),
        name="merge",
    )(x, mix, proj, proj, mk, mv, xq_norm.reshape(-1, 1, X_HEAD_DIM), w_out)


def _mla_qkv_kernel(cq_ref, ckv_ref, kr_ref, posr_ref, invfc_ref, gq_ref, gkv_ref, gqn_ref,
                    gkn_ref, gqr_ref, gkr_ref, wuq_ref, wukv_ref, qt_ref, kn_ref, krope_ref, vt_ref,
                    wqt_ref, wk_ref, wvt_ref):
    half = MLA_ROPE // 2
    tm = cq_ref.shape[1]
    qk = MLA_NOPE + MLA_ROPE

    @pl.when((pl.program_id(0) == 0) & (pl.program_id(1) == 0))
    def _():
        wqt_ref[...] = wuq_ref[...].T.astype(BF16)
        for h in range(MLA_HEADS):
            c0 = h * (MLA_NOPE + MLA_V)
            wk_ref[:, h * MLA_NOPE:(h + 1) * MLA_NOPE] = wukv_ref[:, c0:c0 + MLA_NOPE].astype(BF16)
            wvt_ref[h * MLA_V:(h + 1) * MLA_V, :] = wukv_ref[:, c0 + MLA_NOPE:c0 + MLA_NOPE + MLA_V].T.astype(BF16)

    qscale = (MLA_NOPE + MLA_ROPE) ** -0.5 * math.log2(math.e)

    cq = _rms(cq_ref[0].astype(F32), gq_ref[...])
    ckv = _rms(ckv_ref[0].astype(F32), gkv_ref[...])
    cq_t = cq.T.astype(BF16)
    ckv_t = ckv.T.astype(BF16)
    ckv_b = ckv.astype(BF16)

    def project(h):
        dot = functools.partial(jnp.dot, preferred_element_type=F32)
        k_pair = dot(ckv_b, wk_ref[:, h * MLA_NOPE:(h + 2) * MLA_NOPE]) if h % 2 == 0 else None
        return (dot(wqt_ref[h * qk:(h + 1) * qk, :], cq_t),
                dot(wvt_ref[h * MLA_V:(h + 1) * MLA_V, :], ckv_t), k_pair)

    ang_t = invfc_ref[...] * posr_ref[0].astype(F32)
    cos_t, sin_t = jnp.cos(ang_t), jnp.sin(ang_t)
    g_nope = jnp.broadcast_to(gqn_ref[...], (MLA_NOPE, tm)) * qscale
    g_r1 = jnp.broadcast_to(gqr_ref[:half, :], (half, tm)) * qscale
    g_r2 = jnp.broadcast_to(gqr_ref[half:, :], (half, tm)) * qscale
    ahead = 2
    pending = [project(h) for h in range(ahead)]
    for h in range(MLA_HEADS):
        if h + ahead < MLA_HEADS:
            pending.append(project(h + ahead))
        q, v_t, _ = pending[h]
        k_n = pending[h - h % 2][2][:, (h % 2) * MLA_NOPE:(h % 2 + 1) * MLA_NOPE]
        nope = q[:MLA_NOPE]
        r = lax.rsqrt(jnp.mean(nope * nope, axis=0, keepdims=True) + EPS)
        qt_ref[0, h, :MLA_NOPE, :] = (nope * r * g_nope).astype(BF16)
        x1, x2 = q[MLA_NOPE:MLA_NOPE + half], q[MLA_NOPE + half:MLA_NOPE + MLA_ROPE]
        ss = jnp.sum(x1 * x1, axis=0, keepdims=True) + jnp.sum(x2 * x2, axis=0, keepdims=True)
        r = lax.rsqrt(ss * (1.0 / MLA_ROPE) + EPS)
        x1, x2 = x1 * r * g_r1, x2 * r * g_r2
        qt_ref[0, h, MLA_NOPE:MLA_NOPE + half, :] = (x1 * cos_t - x2 * sin_t).astype(BF16)
        qt_ref[0, h, MLA_NOPE + half:MLA_NOPE + MLA_ROPE, :] = (x1 * sin_t + x2 * cos_t).astype(BF16)
        qt_ref[0, h, MLA_NOPE + MLA_ROPE:, :] = jnp.zeros((MLA_QK_PAD - MLA_NOPE - MLA_ROPE, tm), BF16)
        kn_ref[0, h] = _rms(k_n, gkn_ref[...]).astype(BF16)
        vt_ref[0, h] = v_t.astype(BF16)

    kr_t = kr_ref[0].astype(F32).T
    x1, x2 = kr_t[:half], kr_t[half:MLA_ROPE]
    ss = jnp.sum(x1 * x1, axis=0, keepdims=True) + jnp.sum(x2 * x2, axis=0, keepdims=True)
    r = lax.rsqrt(ss * (1.0 / MLA_ROPE) + EPS)
    x1, x2 = x1 * r * gkr_ref[:half, :], x2 * r * gkr_ref[half:, :]
    rot = jnp.concatenate([x1 * cos_t - x2 * sin_t, x1 * sin_t + x2 * cos_t,
                           jnp.zeros((LANES - MLA_ROPE, tm), F32)], axis=0)
    krope_ref[0] = rot.T.astype(BF16)


def _mla_qkv(proj, cq_blk, ckv_blk, kr_blk, positions, gains, w_uq, w_ukv, layer, *, tm):
    b, l, _ = proj.shape
    hh = MLA_HEADS
    half = MLA_ROPE // 2
    inv_freq = ROPE_THETA ** (-jnp.arange(half, dtype=F32) / half)
    const = lambda a: pl.BlockSpec(a.shape, lambda i, j: (0,) * a.ndim)
    gq, gkv, gqn, gkn, gqr, gkr = gains
    consts = [inv_freq.reshape(half, 1), gq.reshape(1, -1), gkv.reshape(1, -1), gqn.reshape(-1, 1),
              gkn.reshape(1, -1), gqr.reshape(-1, 1), gkr.reshape(-1, 1)]
    weights = [w_uq, w_ukv]
    return pl.pallas_call(
        _mla_qkv_kernel,
        grid=(b, l // tm),
        in_specs=[pl.BlockSpec((1, tm, MLA_Q_LORA), lambda i, j: (i, j, cq_blk)),
                  pl.BlockSpec((1, tm, MLA_KV_LORA), lambda i, j: (i, j, ckv_blk)),
                  pl.BlockSpec((1, tm, LANES), lambda i, j: (i, j, kr_blk)),
                  pl.BlockSpec((1, 1, tm), lambda i, j: (i, 0, j))] + [const(a) for a in consts]
                 + [pl.BlockSpec(*_layer_block(w, layer)) for w in weights],
        out_specs=[pl.BlockSpec((1, hh, MLA_QK_PAD, tm), lambda i, j: (i, 0, 0, j)),
                   pl.BlockSpec((1, hh, tm, MLA_NOPE), lambda i, j: (i, 0, j, 0)),
                   pl.BlockSpec((1, tm, LANES), lambda i, j: (i, j, 0)),
                   pl.BlockSpec((1, hh, MLA_V, tm), lambda i, j: (i, 0, 0, j))],
        out_shape=[jax.ShapeDtypeStruct((b, hh, MLA_QK_PAD, l), BF16),
                   jax.ShapeDtypeStruct((b, hh, l, MLA_NOPE), BF16),
                   jax.ShapeDtypeStruct((b, l, LANES), BF16),
                   jax.ShapeDtypeStruct((b, hh, MLA_V, l), BF16)],
        scratch_shapes=[pltpu.VMEM((hh * (MLA_NOPE + MLA_ROPE), MLA_Q_LORA), BF16),
                        pltpu.VMEM((MLA_KV_LORA, hh * MLA_NOPE), BF16),
                        pltpu.VMEM((hh * MLA_V, MLA_KV_LORA), BF16)],
        compiler_params=_cparams(("arbitrary", "arbitrary")),
        name="mla_qkv",
    )(proj, proj, proj, positions.reshape(b, 1, l), *consts, *weights)


def _flash_kernel(qt_ref, kn_ref, kr_ref, vt_ref, o_ref, m_ref, l_ref, acc_ref, *, tq, hp, ahead):
    qi = pl.program_id(2)
    m_ref[...] = jnp.full(m_ref.shape, -jnp.inf, F32)
    l_ref[...] = jnp.zeros(l_ref.shape, F32)
    acc_ref[...] = jnp.zeros(acc_ref.shape, F32)

    half = tq // 2
    lower = (lax.broadcasted_iota(jnp.int32, (half, half), 0)
             <= lax.broadcasted_iota(jnp.int32, (half, half), 1))

    def blocks(j, parts, diagonal):
        base = pl.multiple_of(j * tq, tq)
        items = [(h, pl.ds(base + k0, nk), slice(q0, q0 + nq)) for k0, nk, q0, nq in parts for h in range(hp)]

        def scores(h, rows, cols):
            k = jnp.concatenate([kn_ref[0, h, rows, :], kr_ref[0, rows, :]], axis=-1)
            return jnp.dot(k, qt_ref[0, h, :, cols], preferred_element_type=F32)

        pending = [scores(*it) for it in items[:ahead]]
        for n, (h, rows, cols) in enumerate(items):
            if n + ahead < len(items):
                pending.append(scores(*items[n + ahead]))
            s = pending[n]
            if diagonal:
                square = jnp.where(lower, s[:, :half], jnp.finfo(F32).min)
                s = square if s.shape[1] == half else jnp.concatenate([square, s[:, half:]], axis=1)
            m = m_ref[h, :, cols]
            m_new = jnp.maximum(m, jnp.max(s, axis=0, keepdims=True))
            alpha = jnp.exp2(m - m_new)
            p = jnp.exp2(s - m_new)
            l_ref[h, :, cols] = alpha * l_ref[h, :, cols] + jnp.sum(p, axis=0, keepdims=True)
            acc_ref[h, :, cols] = alpha * acc_ref[h, :, cols] + jnp.dot(
                vt_ref[0, h, :, rows], p.astype(BF16), preferred_element_type=F32)
            m_ref[h, :, cols] = m_new

    def body(j, carry):
        blocks(j, [(0, tq, 0, tq)], False)
        return carry

    lax.fori_loop(0, qi, body, 0)
    blocks(qi, [(0, half, 0, tq), (half, half, half, half)], True)
    for h in range(hp):
        o_ref[0, :, h * MLA_V:(h + 1) * MLA_V] = (acc_ref[h] / l_ref[h]).T.astype(o_ref.dtype)


def _flash(qt, kn, kr, vt, *, tq, hp, ahead):
    b, hh, _, l = qt.shape
    return pl.pallas_call(
        functools.partial(_flash_kernel, tq=tq, hp=hp, ahead=ahead),
        grid=(b, hh // hp, l // tq),
        in_specs=[pl.BlockSpec((1, hp, MLA_QK_PAD, tq), lambda i, h, j: (i, h, 0, j)),
                  pl.BlockSpec((1, hp, l, MLA_NOPE), lambda i, h, j: (i, h, 0, 0)),
                  pl.BlockSpec((1, l, LANES), lambda i, h, j: (i, 0, 0)),
                  pl.BlockSpec((1, hp, MLA_V, l), lambda i, h, j: (i, h, 0, 0))],
        out_specs=pl.BlockSpec((1, tq, hp * MLA_V), lambda i, h, j: (i, j, h)),
        out_shape=jax.ShapeDtypeStruct((b, l, hh * MLA_V), BF16),
        scratch_shapes=[pltpu.VMEM((hp, 1, tq), F32), pltpu.VMEM((hp, 1, tq), F32),
                        pltpu.VMEM((hp, MLA_V, tq), F32)],
        compiler_params=_cparams(("parallel", "parallel", "parallel")),
        name="flash",
    )(qt, kn, kr, vt)


def _s5_layer(x, ln, w_in, lam_re, lam_im, log_step, b_re, b_im, c_re, c_im, d, w_glu,
              w_out, mem_kv, xq_norm, layer, j):
    b, l, dm = x.shape
    tm = 512
    proj, xn = _s5_in_proj(x, ln, w_in, j, tm=tm, col_chunk=512)
    xg = _s5_ut(xn, w_in, j)
    pw_re, pw_im = _s5_pow(lam_re, lam_im, log_step)
    toep, wout, wst = _s5_asm(pw_re.transpose(1, 0, 2), pw_im.transpose(1, 0, 2), c_re, c_im, b_re, b_im)
    n_scan = int(math.log2(LANES))
    first = S5_EXPONENTS.index(S5_CHUNK)
    col = lambda pw: jnp.pad(pw[first:first + n_scan].transpose(1, 2, 0), ((0, 0), (0, 0), (0, 8 - n_scan)))
    dcol = jnp.tile(d.reshape(S5_GROUPS, 1, S5_GROUP_CH), (1, S5_CHUNK, 1)).reshape(S5_GROUPS, -1, 1)
    yt = _s5_mix(xg, toep, wst, wout, col(pw_re), col(pw_im), dcol, nb=b, gs=4)
    y = _glu(yt, w_glu, j, nb=b, col_chunk=256)
    return _merge(x, y, proj, BRANCH_WIDTH // XQ_WIDTH, 0, *mem_kv, xq_norm, w_out, layer, tm=tm, phased=True)


def _mla_layer(x, positions, ln, w_in, q_lora_norm, kv_lora_norm, w_uq, w_ukv, q_nope_norm, k_nope_norm,
               q_rope_norm, k_rope_norm, w_out, mem_kv, xq_norm, layer, j):
    b, l, dm = x.shape
    o1 = MLA_Q_LORA
    o2 = o1 + MLA_KV_LORA
    o3 = o2 + MLA_ROPE
    o4 = o3 + XQ_WIDTH
    segments = ((o4, BRANCH_WIDTH), (0, o1), (o3, XQ_WIDTH), (o1, MLA_KV_LORA), (o2, MLA_ROPE))
    wout = -(-(o4 + BRANCH_WIDTH) // 512) * 512
    proj = _mla_in_proj(x.reshape(b * l, dm), ln, w_in, j, segments, wout, tm=512, col_chunk=512)
    proj = proj.reshape(b, l, -1)
    gate_blk = 0
    cq_blk = BRANCH_WIDTH // MLA_Q_LORA
    xq_blk = (BRANCH_WIDTH + MLA_Q_LORA) // XQ_WIDTH
    ckv_blk = (BRANCH_WIDTH + MLA_Q_LORA + XQ_WIDTH) // MLA_KV_LORA
    kr_blk = (BRANCH_WIDTH + MLA_Q_LORA + XQ_WIDTH + MLA_KV_LORA) // LANES
    qt, kn, kr, vt = _mla_qkv(proj, cq_blk, ckv_blk, kr_blk, positions,
                              (q_lora_norm, kv_lora_norm, q_nope_norm, k_nope_norm, q_rope_norm, k_rope_norm),
                              w_uq, w_ukv, j, tm=256)
    attn = _flash(qt, kn, kr, vt, tq=512, hp=12, ahead=2)
    return _merge(x, attn, proj, xq_blk, gate_blk, *mem_kv, xq_norm, w_out, layer, tm=512)


def kernel(x, mem, positions, ln_gain, w_out, mem_norm, w_mem_kv, xq_norm, xk_norm,
           s5_w_in, s5_lambda_re, s5_lambda_im, s5_log_step, s5_b_re, s5_b_im, s5_c_re, s5_c_im,
           s5_d, s5_w_glu, mla_w_in, mla_q_lora_norm, mla_kv_lora_norm, mla_w_uq, mla_w_ukv,
           mla_q_nope_norm, mla_k_nope_norm, mla_q_rope_norm, mla_k_rope_norm):
    depth = ln_gain.shape[0]
    mem_kv = _mem_kv(mem, mem_norm, w_mem_kv, xk_norm)
    for i in range(depth):
        j = i // 2
        if i % 2 == 0:
            x = _s5_layer(x, ln_gain[i], s5_w_in, s5_lambda_re[j], s5_lambda_im[j], s5_log_step[j],
                          s5_b_re[j], s5_b_im[j], s5_c_re[j], s5_c_im[j], s5_d[j], s5_w_glu,
                          w_out, mem_kv, xq_norm, i, j)
        else:
            x = _mla_layer(x, positions, ln_gain[i], mla_w_in, mla_q_lora_norm[j], mla_kv_lora_norm[j],
                           mla_w_uq, mla_w_ukv, mla_q_nope_norm[j], mla_k_nope_norm[j],
                           mla_q_rope_norm[j], mla_k_rope_norm[j],
                           w_out, mem_kv, xq_norm, i, j)
    return x
```

```python
import functools
import math

import jax
import jax.numpy as jnp
from jax import lax
from jax.experimental import pallas as pl
from jax.experimental.pallas import tpu as pltpu

D_MODEL = 1024
BRANCH_WIDTH = 2 * D_MODEL
XQ_WIDTH = BRANCH_WIDTH // 4
PRIMARY_WIDTH = BRANCH_WIDTH - XQ_WIDTH
X_HEADS = 4
X_HEAD_DIM = XQ_WIDTH // X_HEADS
S5_GROUP_CH = 16
S5_GROUPS = PRIMARY_WIDTH // S5_GROUP_CH
S5_STATE = 64
MLA_NOPE = 128
MLA_ROPE = 64
MLA_V = 128
MLA_HEADS = PRIMARY_WIDTH // MLA_V
MLA_Q_LORA = D_MODEL // 2
MLA_KV_LORA = D_MODEL // 4
ROPE_THETA = 10000.0
EPS = 1e-6

LANES = 128
MLA_QK_PAD = 2 * LANES
BF16_SUBLANES = 16
MLA_VL = MLA_V + BF16_SUBLANES
S5_CHUNK = 2 * LANES // S5_GROUP_CH
S5_EXPONENTS = list(range(S5_CHUNK + 1)) + [S5_CHUNK * 2 ** i for i in range(1, int(math.log2(LANES)))]
VMEM_LIMIT = 56 * 1024 * 1024

F32 = jnp.float32
BF16 = jnp.bfloat16


def _cparams(sem):
    return pltpu.CompilerParams(dimension_semantics=sem, vmem_limit_bytes=VMEM_LIMIT)


def _rms(x, g):
    return x * lax.rsqrt(jnp.mean(x * x, axis=-1, keepdims=True) + EPS) * g


def _layer_block(w, j, block=None, index=None):
    block = tuple(w.shape[1:]) if block is None else block
    index = (0,) * len(block) if index is None else index
    return (None,) + block, lambda *_: (j,) + index


def _mla_in_proj_kernel(x_ref, g_ref, w_ref, o_ref, wp_ref, *, segments, col_chunk):
    @pl.when(pl.program_id(0) == 0)
    def _():
        at = 0
        for start, width in segments:
            wp_ref[:, at:at + width] = w_ref[:, start:start + width].astype(BF16)
            at += width
        wp_ref[:, at:] = jnp.zeros((wp_ref.shape[0], wp_ref.shape[1] - at), BF16)

    xn = _rms(x_ref[...], g_ref[...]).astype(BF16)
    for c in range(o_ref.shape[1] // col_chunk):
        sl = slice(c * col_chunk, (c + 1) * col_chunk)
        o_ref[:, sl] = jnp.dot(xn, wp_ref[:, sl], preferred_element_type=F32).astype(o_ref.dtype)


def _mla_in_proj(x, g, w, j, segments, wout, *, tm, col_chunk):
    n, d = x.shape
    return pl.pallas_call(
        functools.partial(_mla_in_proj_kernel, segments=segments, col_chunk=col_chunk),
        grid=(n // tm,),
        in_specs=[pl.BlockSpec((tm, d), lambda i: (i, 0)),
                  pl.BlockSpec((1, d), lambda i: (0, 0)),
                  pl.BlockSpec(*_layer_block(w, j), pipeline_mode=pl.Buffered(1))],
        out_specs=pl.BlockSpec((tm, wout), lambda i: (i, 0)),
        out_shape=jax.ShapeDtypeStruct((n, wout), BF16),
        scratch_shapes=[pltpu.VMEM((d, wout), BF16)],
        compiler_params=_cparams(("arbitrary",)),
        name="mla_in_proj",
    )(x, g.reshape(1, d), w)


def _to_phase_order(a):
    n, d = a.shape
    return jnp.swapaxes(a.reshape(n // S5_CHUNK, S5_CHUNK, d), 0, 1).reshape(n, d)


def _from_phase_order(a):
    n, d = a.shape
    return jnp.swapaxes(a.reshape(S5_CHUNK, n // S5_CHUNK, d), 0, 1).reshape(n, d)


def _s5_in_proj_kernel(x_ref, g_ref, wg_ref, wx_ref, o_ref, xn_ref, *, tm, col_chunk):
    d = x_ref.shape[2]
    nc = tm // S5_CHUNK
    chunks = [slice(c * col_chunk, (c + 1) * col_chunk) for c in range(BRANCH_WIDTH // col_chunk)]
    weights = [wg_ref[:, sl].astype(BF16) for sl in chunks] + [wx_ref[...].astype(BF16)]
    chunks.append(slice(BRANCH_WIDTH, BRANCH_WIDTH + XQ_WIDTH))

    def normalise(k):
        xn = _rms(_to_phase_order(x_ref[0, k * tm:(k + 1) * tm, :]), g_ref[...]).astype(BF16)
        for s in range(S5_CHUNK):
            xn_ref[0, k * nc:(k + 1) * nc, s * d:(s + 1) * d] = xn[s * nc:(s + 1) * nc]
        return xn

    def project(k, xn):
        for sl, w in zip(chunks, weights):
            o_ref[0, k * tm:(k + 1) * tm, sl] = jnp.dot(xn, w, preferred_element_type=F32).astype(o_ref.dtype)

    n_sub = x_ref.shape[1] // tm
    prev = None
    for k in range(n_sub):
        xn = normalise(k)
        if prev is not None:
            project(k - 1, prev)
        prev = xn
    project(n_sub - 1, prev)


def _s5_in_proj(x, g, w_in, j, *, tm, col_chunk):
    b, l, d = x.shape
    wout = BRANCH_WIDTH + XQ_WIDTH
    tg = 2 * tm
    nc = tg // S5_CHUNK
    return pl.pallas_call(
        functools.partial(_s5_in_proj_kernel, tm=tm, col_chunk=col_chunk),
        grid=(b, l // tg),
        in_specs=[pl.BlockSpec((1, tg, d), lambda i, j: (i, j, 0)),
                  pl.BlockSpec((1, d), lambda i, j: (0, 0)),
                  pl.BlockSpec(*_layer_block(w_in, j, (d, BRANCH_WIDTH),
                                             (0, (PRIMARY_WIDTH + XQ_WIDTH) // BRANCH_WIDTH)),
                               pipeline_mode=pl.Buffered(1)),
                  pl.BlockSpec(*_layer_block(w_in, j, (d, XQ_WIDTH), (0, PRIMARY_WIDTH // XQ_WIDTH)),
                               pipeline_mode=pl.Buffered(1))],
        out_specs=[pl.BlockSpec((1, tg, wout), lambda i, j: (i, j, 0)),
                   pl.BlockSpec((1, nc, S5_CHUNK * d), lambda i, j: (i, j, 0))],
        out_shape=[jax.ShapeDtypeStruct((b, l, wout), BF16),
                   jax.ShapeDtypeStruct((b, l // S5_CHUNK, S5_CHUNK * d), BF16)],
        compiler_params=_cparams(("parallel", "parallel")),
        name="s5_in_proj",
    )(x, g.reshape(1, d), w_in, w_in)


def _mem_kv_kernel(m_ref, g_ref, w_ref, kg_ref, k_ref, v_ref):
    mn = _rms(m_ref[0], g_ref[0]).astype(BF16)
    kv = jnp.dot(mn, w_ref[0].astype(BF16), preferred_element_type=F32)
    for h in range(X_HEADS):
        sl = slice(h * X_HEAD_DIM, (h + 1) * X_HEAD_DIM)
        k_ref[0, 0, :, sl] = _rms(kv[:, sl], kg_ref[0]).astype(BF16)
    v_ref[0, 0] = kv[:, XQ_WIDTH:].astype(BF16)


def _mem_kv(mem, mem_norm, w_mem_kv, xk_norm):
    b, m, d = mem.shape
    depth = w_mem_kv.shape[0]
    out = jax.ShapeDtypeStruct((depth, b, m, XQ_WIDTH), BF16)
    return pl.pallas_call(
        _mem_kv_kernel,
        grid=(depth, b),
        in_specs=[pl.BlockSpec((1, m, d), lambda n, i: (i, 0, 0)),
                  pl.BlockSpec((1, 1, d), lambda n, i: (n, 0, 0)),
                  pl.BlockSpec((1, d, 2 * XQ_WIDTH), lambda n, i: (n, 0, 0)),
                  pl.BlockSpec((1, 1, X_HEAD_DIM), lambda n, i: (n, 0, 0))],
        out_specs=[pl.BlockSpec((1, 1, m, XQ_WIDTH), lambda n, i: (n, i, 0, 0)),
                   pl.BlockSpec((1, 1, m, XQ_WIDTH), lambda n, i: (n, i, 0, 0))],
        out_shape=[out, out],
        compiler_params=_cparams(("parallel", "parallel")),
        name="mem_kv",
    )(mem, mem_norm.reshape(depth, 1, d), w_mem_kv, xk_norm.reshape(depth, 1, X_HEAD_DIM))


def _s5_pow_kernel(lr_ref, li_ref, ls_ref, pr_ref, pi_ref):
    lr, li = lr_ref[...], li_ref[...]
    step = jnp.exp(ls_ref[...])
    zr, zi = lr * step, li * step
    for n, e in enumerate(S5_EXPONENTS):
        mag = jnp.exp(zr * e)
        pr_ref[n] = mag * jnp.cos(zi * e)
        pi_ref[n] = mag * jnp.sin(zi * e)
    ar, ai = pr_ref[1], pi_ref[1]
    den = lr * lr + li * li
    pr_ref[len(S5_EXPONENTS)] = ((ar - 1.0) * lr + ai * li) / den
    pi_ref[len(S5_EXPONENTS)] = (ai * lr - (ar - 1.0) * li) / den


def _s5_pow(lam_re, lam_im, log_step):
    g, p = lam_re.shape
    out = jax.ShapeDtypeStruct((len(S5_EXPONENTS) + 1, g, p), F32)
    return pl.pallas_call(_s5_pow_kernel, out_shape=[out, out], name="s5_pow")(
        lam_re, lam_im, log_step.reshape(g, 1))


def _s5_asm_kernel(pr_ref, pi_ref, cr_ref, ci_ref, btr_ref, bti_ref, br_ref, bi_ref,
                   toep_ref, wout_ref, wst_ref):
    def group(i, carry):
        _s5_asm_group(i, pr_ref, pi_ref, cr_ref, ci_ref, btr_ref, bti_ref, br_ref, bi_ref,
                      toep_ref, wout_ref, wst_ref)
        return carry

    lax.fori_loop(0, pr_ref.shape[0], group, 0)


def _dot_3pass_tiled(a, b):
    c = b.shape[1]
    rep = ((lax.broadcasted_iota(jnp.int32, (c, S5_CHUNK * c), 1) & (c - 1))
           == lax.broadcasted_iota(jnp.int32, (c, S5_CHUNK * c), 0)).astype(BF16)
    dot = functools.partial(jnp.dot, preferred_element_type=F32)
    a_hi, b_hi = a.astype(BF16), b.astype(BF16)
    a_lo = (a - a_hi.astype(F32)).astype(BF16)
    b_lo = (b - b_hi.astype(F32)).astype(BF16)
    b_hi, b_lo = dot(b_hi, rep).astype(BF16), dot(b_lo, rep).astype(BF16)
    return dot(a_hi, b_hi) + (dot(a_hi, b_lo) + dot(a_lo, b_hi))


def _s5_asm_group(i, pr_ref, pi_ref, cr_ref, ci_ref, btr_ref, bti_ref, br_ref, bi_ref,
                  toep_ref, wout_ref, wst_ref):
    t = S5_CHUNK
    pr, pi = pr_ref[i], pi_ref[i]
    cr, ci = cr_ref[i], ci_ref[i]
    btr, bti = btr_ref[i], bti_ref[i]
    mr, mi = pr[len(S5_EXPONENTS):], pi[len(S5_EXPONENTS):]
    amr = pr[:t] * mr - pi[:t] * mi
    ami = pr[:t] * mi + pi[:t] * mr
    l_re, l_im, w_re, w_im, o_re, o_im = [], [], [], [], [], []
    for k in range(t):
        ar, ai = amr[k:k + 1], ami[k:k + 1]
        l_re.append(cr * ar - ci * ai)
        l_im.append(-(cr * ai + ci * ar))
        ar, ai = amr[t - 1 - k:t - k], ami[t - 1 - k:t - k]
        w_re.append(btr * ar - bti * ai)
        w_im.append(btr * ai + bti * ar)
        ar, ai = pr[k + 1:k + 2], pi[k + 1:k + 2]
        o_re.append(cr * ar - ci * ai)
        o_im.append(-(cr * ai + ci * ar))
    cat = lambda parts: jnp.concatenate(parts, axis=0)
    kt = _dot_3pass_tiled(jnp.concatenate([cat(l_re), cat(l_im)], axis=1),
                          jnp.concatenate([br_ref[i], bi_ref[i]], axis=0))
    n = kt.shape[0]
    blk = lax.shift_right_logical(lax.broadcasted_iota(jnp.int32, kt.shape, 1), int(math.log2(S5_GROUP_CH)))
    toep = jnp.where(blk == 0, kt, 0.0)
    for s in range(1, t):
        shifted = jnp.concatenate([jnp.zeros((s * S5_GROUP_CH, n), F32), kt[:n - s * S5_GROUP_CH]], axis=0)
        toep = jnp.where(blk == s, shifted, toep)
    toep_ref[i] = toep.astype(BF16)
    wout_ref[i] = jnp.concatenate([cat(o_re), cat(o_im)], axis=1).astype(BF16)
    wst_ref[i] = jnp.concatenate([cat(w_re), cat(w_im)], axis=1).T.astype(BF16)


def _s5_asm(pw_re, pw_im, c_re, c_im, b_re, b_im):
    g, c, p = c_re.shape
    tc = S5_CHUNK * c
    gs = 8
    blk = lambda a: pl.BlockSpec((gs,) + a.shape[1:], lambda i: (i, 0, 0))
    bt_re, bt_im = b_re.transpose(0, 2, 1), b_im.transpose(0, 2, 1)
    args = (pw_re, pw_im, c_re, c_im, bt_re, bt_im, b_re, b_im)
    return pl.pallas_call(
        _s5_asm_kernel,
        grid=(g // gs,),
        in_specs=[blk(a) for a in args],
        out_specs=[pl.BlockSpec((gs, tc, tc), lambda i: (i, 0, 0)),
                   pl.BlockSpec((gs, tc, 2 * p), lambda i: (i, 0, 0)),
                   pl.BlockSpec((gs, 2 * p, tc), lambda i: (i, 0, 0))],
        out_shape=[jax.ShapeDtypeStruct((g, tc, tc), BF16), jax.ShapeDtypeStruct((g, tc, 2 * p), BF16),
                   jax.ShapeDtypeStruct((g, 2 * p, tc), BF16)],
        compiler_params=_cparams(("parallel",)),
        name="s5_asm",
    )(*args)


def _s5_ut_kernel(xn_ref, w_ref, o_ref, wt_ref):
    @pl.when(pl.program_id(0) == 0)
    def _():
        wt_ref[...] = w_ref[...].T.astype(BF16)

    nb, nc, d = xn_ref.shape
    ut = lax.dot_general(wt_ref[...], xn_ref[...].reshape(nb * nc, d), (((1,), (1,)), ((), ())),
                         preferred_element_type=F32)
    o_ref[...] = ut.astype(BF16).reshape(o_ref.shape)


def _s5_ut(xn, w_in, j):
    b, nc, td = xn.shape
    d = td // S5_CHUNK
    return pl.pallas_call(
        _s5_ut_kernel,
        grid=(S5_CHUNK,),
        in_specs=[pl.BlockSpec((b, nc, d), lambda s: (0, 0, s)),
                  pl.BlockSpec(*_layer_block(w_in, j, (d, PRIMARY_WIDTH)))],
        out_specs=pl.BlockSpec((S5_GROUPS, S5_GROUP_CH, b * nc), lambda s: (0, s, 0)),
        out_shape=jax.ShapeDtypeStruct((S5_GROUPS, S5_CHUNK * S5_GROUP_CH, b * nc), BF16),
        scratch_shapes=[pltpu.VMEM((PRIMARY_WIDTH, d), BF16)],
        compiler_params=_cparams(("arbitrary",)),
        name="s5_ut",
    )(xn, w_in)


def _s5_mix_kernel(x_ref, toep_ref, wst_ref, wout_ref, sr_ref, si_ref, d_ref, o_ref, *, nb):
    p = S5_STATE
    gs = x_ref.shape[0]
    lane = lax.broadcasted_iota(jnp.int32, (p, LANES), 1)
    n_steps = int(math.log2(LANES))

    def scan(g, hloc):
        pw = []
        for i in range(n_steps):
            keep = lane >= (1 << i)
            pw.append((jnp.where(keep, jnp.broadcast_to(sr_ref[g, :, i:i + 1], (p, LANES)), 0.0),
                       jnp.where(keep, jnp.broadcast_to(si_ref[g, :, i:i + 1], (p, LANES)), 0.0)))
        h_re = [hloc[:p, b * LANES:(b + 1) * LANES] for b in range(nb)]
        h_im = [hloc[p:, b * LANES:(b + 1) * LANES] for b in range(nb)]
        for i in range(n_steps):
            ar, ai = pw[i]
            r_sh = [pltpu.roll(v, 1 << i, 1) for v in h_re]
            i_sh = [pltpu.roll(v, 1 << i, 1) for v in h_im]
            h_re = [h_re[b] + ar * r_sh[b] - ai * i_sh[b] for b in range(nb)]
            h_im = [h_im[b] + ar * i_sh[b] + ai * r_sh[b] for b in range(nb)]
        h_re = [jnp.where(lane >= 1, pltpu.roll(v, 1, 1), 0.0) for v in h_re]
        h_im = [jnp.where(lane >= 1, pltpu.roll(v, 1, 1), 0.0) for v in h_im]
        return jnp.concatenate([jnp.concatenate(h_re, axis=1), jnp.concatenate(h_im, axis=1)],
                               axis=0).astype(BF16)

    def outputs(g, h):
        x = x_ref[g]
        y = (jnp.dot(toep_ref[g], x, preferred_element_type=F32)
             + jnp.dot(wout_ref[g], h, preferred_element_type=F32)
             + d_ref[g] * x.astype(F32))
        o_ref[:, g * S5_GROUP_CH:(g + 1) * S5_GROUP_CH, :] = (
            jax.nn.gelu(y).astype(o_ref.dtype).reshape(S5_CHUNK, S5_GROUP_CH, y.shape[1]))

    h_prev = None
    for g in range(gs):
        hloc = jnp.dot(wst_ref[g], x_ref[g], preferred_element_type=F32)
        h = scan(g, hloc)
        if h_prev is not None:
            outputs(g - 1, h_prev)
        h_prev = h
    outputs(gs - 1, h_prev)


def _s5_mix(xg, toep, wst, wout, sc_re, sc_im, dcol, *, nb, gs):
    g, tc, cols = xg.shape
    assert cols == nb * LANES, "one batch's chunks must fill exactly one 128-lane block"
    blk = lambda a: pl.BlockSpec((gs,) + a.shape[1:], lambda i: (i, 0, 0))
    return pl.pallas_call(
        functools.partial(_s5_mix_kernel, nb=nb),
        grid=(g // gs,),
        in_specs=[blk(a) for a in (xg, toep, wst, wout, sc_re, sc_im, dcol)],
        out_specs=pl.BlockSpec((S5_CHUNK, gs * S5_GROUP_CH, cols), lambda i: (0, i, 0)),
        out_shape=jax.ShapeDtypeStruct((S5_CHUNK, g * S5_GROUP_CH, cols), BF16),
        compiler_params=_cparams(("parallel",)),
        name="s5_mix",
    )(xg, toep, wst, wout, sc_re, sc_im, dcol)


def _glu_kernel(y_ref, w_ref, o_ref, *, col_chunk):
    y = y_ref[0].T
    half = o_ref.shape[-1]
    for c in range(half // col_chunk):
        wa = w_ref[:, c * col_chunk:(c + 1) * col_chunk].astype(BF16)
        wg = w_ref[:, half + c * col_chunk:half + (c + 1) * col_chunk].astype(BF16)
        a = jnp.dot(y, wa, preferred_element_type=F32)
        g = jnp.dot(y, wg, preferred_element_type=F32)
        o_ref[:, :, c * col_chunk:(c + 1) * col_chunk] = (
            (a * jax.nn.sigmoid(g)).astype(o_ref.dtype).reshape(o_ref.shape[:2] + (col_chunk,)))


def _glu(yt, w, layer, *, nb, col_chunk):
    t, k, cols = yt.shape
    nc = cols // nb
    half = w.shape[2] // 2
    return pl.pallas_call(
        functools.partial(_glu_kernel, col_chunk=col_chunk),
        grid=(t,),
        in_specs=[pl.BlockSpec((1, k, cols), lambda j: (j, 0, 0)),
                  pl.BlockSpec(*_layer_block(w, layer), pipeline_mode=pl.Buffered(1))],
        out_specs=pl.BlockSpec((nb, nc, half), lambda j: (0, 0, j)),
        out_shape=jax.ShapeDtypeStruct((nb, nc, t * half), BF16),
        compiler_params=_cparams(("parallel",)),
        name="glu",
    )(yt, w)


def _merge_kernel(x_ref, mix_ref, xq_ref, gate_ref, k_ref, v_ref, qg_ref, w_ref, o_ref, *cat_refs, tm, phased):
    scale = X_HEAD_DIM ** -0.5
    nc = tm // S5_CHUNK
    w = w_ref[...].astype(BF16)

    def gather(k, cat_ref):
        r0 = k * tm
        gate = gate_ref[0, r0:r0 + tm, :]
        sg = gate * jax.nn.sigmoid(gate)
        if phased:
            for s in range(S5_CHUNK):
                rows = slice(s * nc, (s + 1) * nc)
                mix = mix_ref[0, k * nc:(k + 1) * nc, s * PRIMARY_WIDTH:(s + 1) * PRIMARY_WIDTH]
                cat_ref[rows, :PRIMARY_WIDTH] = mix * sg[rows, :PRIMARY_WIDTH]
        else:
            cat_ref[:, :PRIMARY_WIDTH] = mix_ref[0, r0:r0 + tm, :] * sg[:, :PRIMARY_WIDTH]
        for h in range(X_HEADS):
            sl = slice(h * X_HEAD_DIM, (h + 1) * X_HEAD_DIM)
            q = _rms(xq_ref[0, r0:r0 + tm, sl].astype(F32), qg_ref[...]).astype(BF16)
            s = lax.dot_general(q, k_ref[0, :, sl], (((1,), (1,)), ((), ())), preferred_element_type=F32) * scale
            p = jnp.exp(s - jnp.max(s, axis=-1, keepdims=True))
            p = (p / jnp.sum(p, axis=-1, keepdims=True)).astype(BF16)
            mo = jnp.dot(p, v_ref[0, :, sl], preferred_element_type=F32)
            osl = slice(PRIMARY_WIDTH + h * X_HEAD_DIM, PRIMARY_WIDTH + (h + 1) * X_HEAD_DIM)
            cat_ref[:, osl] = mo.astype(BF16) * sg[:, osl]

    def project(k, cat_ref):
        r0 = k * tm
        delta = jnp.dot(cat_ref[...], w, preferred_element_type=F32)
        o_ref[0, r0:r0 + tm, :] = x_ref[0, r0:r0 + tm, :] + (_from_phase_order(delta) if phased else delta)

    n_sub = len(cat_refs)
    for k in range(n_sub):
        gather(k, cat_refs[k])
        if k > 0:
            project(k - 1, cat_refs[k - 1])
    project(n_sub - 1, cat_refs[n_sub - 1])


def _merge(x, mix, proj, xq_blk, gate_blk, mk, mv, xq_norm, w_out, layer, *, tm, phased=False):
    b, l, d = x.shape
    m = mk.shape[2]
    n_sub = 2
    tg = n_sub * tm
    mix_spec = (pl.BlockSpec((1, tg // S5_CHUNK, S5_CHUNK * PRIMARY_WIDTH), lambda i, j: (i, j, 0)) if phased
                else pl.BlockSpec((1, tg, PRIMARY_WIDTH), lambda i, j: (i, j, 0)))
    return pl.pallas_call(
        functools.partial(_merge_kernel, tm=tm, phased=phased),
        grid=(b, l // tg),
        in_specs=[pl.BlockSpec((1, tg, d), lambda i, j: (i, j, 0)),
                  mix_spec,
                  pl.BlockSpec((1, tg, XQ_WIDTH), lambda i, j: (i, j, xq_blk)),
                  pl.BlockSpec((1, tg, BRANCH_WIDTH), lambda i, j: (i, j, gate_blk)),
                  pl.BlockSpec((None, 1, m, XQ_WIDTH), lambda i, j: (layer, i, 0, 0)),
                  pl.BlockSpec((None, 1, m, XQ_WIDTH), lambda i, j: (layer, i, 0, 0)),
                  pl.BlockSpec((None, 1, X_HEAD_DIM), lambda i, j: (layer, 0, 0)),
                  pl.BlockSpec((None, BRANCH_WIDTH, d), lambda i, j: (layer, 0, 0),
                               pipeline_mode=pl.Buffered(1))],
        out_specs=pl.BlockSpec((1, tg, d), lambda i, j: (i, j, 0)),
        out_shape=jax.ShapeDtypeStruct((b, l, d), F32),
        scratch_shapes=[pltpu.VMEM((tm, BRANCH_WIDTH), BF16)] * n_sub,
        compiler_params=_cparams(("parallel", "parallel")),
        name="merge",
    )(x, mix, proj, proj, mk, mv, xq_norm.reshape(-1, 1, X_HEAD_DIM), w_out)


def _mla_qkv_kernel(cq_ref, ckv_ref, kr_ref, posr_ref, invfc_ref, gq_ref, gkv_ref, gqn_ref,
                    gkn_ref, gqr_ref, gkr_ref, wuq_ref, wukv_ref, qt_ref, kn_ref, krope_ref, vt_ref,
                    wqt_ref, wk_ref, wvt_ref):
    half = MLA_ROPE // 2
    tm = cq_ref.shape[1]
    qk = MLA_NOPE + MLA_ROPE

    @pl.when((pl.program_id(0) == 0) & (pl.program_id(1) == 0))
    def _():
        wqt_ref[...] = wuq_ref[...].T.astype(BF16)
        for h in range(MLA_HEADS):
            c0 = h * (MLA_NOPE + MLA_V)
            wk_ref[:, h * MLA_NOPE:(h + 1) * MLA_NOPE] = wukv_ref[:, c0:c0 + MLA_NOPE].astype(BF16)
            wvt_ref[h * MLA_V:(h + 1) * MLA_V, :] = wukv_ref[:, c0 + MLA_NOPE:c0 + MLA_NOPE + MLA_V].T.astype(BF16)

    qscale = (MLA_NOPE + MLA_ROPE) ** -0.5 * math.log2(math.e)

    cq = _rms(cq_ref[0].astype(F32), gq_ref[...])
    ckv = _rms(ckv_ref[0].astype(F32), gkv_ref[...])
    cq_t = cq.T.astype(BF16)
    ckv_t = ckv.T.astype(BF16)
    ckv_b = ckv.astype(BF16)

    def project(h):
        dot = functools.partial(jnp.dot, preferred_element_type=F32)
        k_pair = dot(ckv_b, wk_ref[:, h * MLA_NOPE:(h + 2) * MLA_NOPE]) if h % 2 == 0 else None
        return (dot(wqt_ref[h * qk:(h + 1) * qk, :], cq_t),
                dot(wvt_ref[h * MLA_V:(h + 1) * MLA_V, :], ckv_t), k_pair)

    ang_t = invfc_ref[...] * posr_ref[0].astype(F32)
    cos_t, sin_t = jnp.cos(ang_t), jnp.sin(ang_t)
    g_nope = jnp.broadcast_to(gqn_ref[...], (MLA_NOPE, tm)) * qscale
    g_r1 = jnp.broadcast_to(gqr_ref[:half, :], (half, tm)) * qscale
    g_r2 = jnp.broadcast_to(gqr_ref[half:, :], (half, tm)) * qscale
    ahead = 2
    pending = [project(h) for h in range(ahead)]
    for h in range(MLA_HEADS):
        if h + ahead < MLA_HEADS:
            pending.append(project(h + ahead))
        q, v_t, _ = pending[h]
        k_n = pending[h - h % 2][2][:, (h % 2) * MLA_NOPE:(h % 2 + 1) * MLA_NOPE]
        nope = q[:MLA_NOPE]
        r = lax.rsqrt(jnp.mean(nope * nope, axis=0, keepdims=True) + EPS)
        qt_ref[0, h, :MLA_NOPE, :] = (nope * r * g_nope).astype(BF16)
        x1, x2 = q[MLA_NOPE:MLA_NOPE + half], q[MLA_NOPE + half:MLA_NOPE + MLA_ROPE]
        ss = jnp.sum(x1 * x1, axis=0, keepdims=True) + jnp.sum(x2 * x2, axis=0, keepdims=True)
        r = lax.rsqrt(ss * (1.0 / MLA_ROPE) + EPS)
        x1, x2 = x1 * r * g_r1, x2 * r * g_r2
        qt_ref[0, h, MLA_NOPE:MLA_NOPE + half, :] = (x1 * cos_t - x2 * sin_t).astype(BF16)
        qt_ref[0, h, MLA_NOPE + half:MLA_NOPE + MLA_ROPE, :] = (x1 * sin_t + x2 * cos_t).astype(BF16)
        qt_ref[0, h, MLA_NOPE + MLA_ROPE:, :] = jnp.zeros((MLA_QK_PAD - MLA_NOPE - MLA_ROPE, tm), BF16)
        kn_ref[0, h] = _rms(k_n, gkn_ref[...]).astype(BF16)
        vt_ref[0, h, :MLA_V, :] = v_t.astype(BF16)
        vt_ref[0, h, MLA_V:, :] = jnp.ones((MLA_VL - MLA_V, tm), BF16)

    kr_t = kr_ref[0].astype(F32).T
    x1, x2 = kr_t[:half], kr_t[half:MLA_ROPE]
    ss = jnp.sum(x1 * x1, axis=0, keepdims=True) + jnp.sum(x2 * x2, axis=0, keepdims=True)
    r = lax.rsqrt(ss * (1.0 / MLA_ROPE) + EPS)
    x1, x2 = x1 * r * gkr_ref[:half, :], x2 * r * gkr_ref[half:, :]
    rot = jnp.concatenate([x1 * cos_t - x2 * sin_t, x1 * sin_t + x2 * cos_t,
                           jnp.zeros((LANES - MLA_ROPE, tm), F32)], axis=0)
    krope_ref[0] = rot.T.astype(BF16)


def _mla_qkv(proj, cq_blk, ckv_blk, kr_blk, positions, gains, w_uq, w_ukv, layer, *, tm):
    b, l, _ = proj.shape
    hh = MLA_HEADS
    half = MLA_ROPE // 2
    inv_freq = ROPE_THETA ** (-jnp.arange(half, dtype=F32) / half)
    const = lambda a: pl.BlockSpec(a.shape, lambda i, j: (0,) * a.ndim)
    gq, gkv, gqn, gkn, gqr, gkr = gains
    consts = [inv_freq.reshape(half, 1), gq.reshape(1, -1), gkv.reshape(1, -1), gqn.reshape(-1, 1),
              gkn.reshape(1, -1), gqr.reshape(-1, 1), gkr.reshape(-1, 1)]
    weights = [w_uq, w_ukv]
    return pl.pallas_call(
        _mla_qkv_kernel,
        grid=(b, l // tm),
        in_specs=[pl.BlockSpec((1, tm, MLA_Q_LORA), lambda i, j: (i, j, cq_blk)),
                  pl.BlockSpec((1, tm, MLA_KV_LORA), lambda i, j: (i, j, ckv_blk)),
                  pl.BlockSpec((1, tm, LANES), lambda i, j: (i, j, kr_blk)),
                  pl.BlockSpec((1, 1, tm), lambda i, j: (i, 0, j))] + [const(a) for a in consts]
                 + [pl.BlockSpec(*_layer_block(w, layer)) for w in weights],
        out_specs=[pl.BlockSpec((1, hh, MLA_QK_PAD, tm), lambda i, j: (i, 0, 0, j)),
                   pl.BlockSpec((1, hh, tm, MLA_NOPE), lambda i, j: (i, 0, j, 0)),
                   pl.BlockSpec((1, tm, LANES), lambda i, j: (i, j, 0)),
                   pl.BlockSpec((1, hh, MLA_VL, tm), lambda i, j: (i, 0, 0, j))],
        out_shape=[jax.ShapeDtypeStruct((b, hh, MLA_QK_PAD, l), BF16),
                   jax.ShapeDtypeStruct((b, hh, l, MLA_NOPE), BF16),
                   jax.ShapeDtypeStruct((b, l, LANES), BF16),
                   jax.ShapeDtypeStruct((b, hh, MLA_VL, l), BF16)],
        scratch_shapes=[pltpu.VMEM((hh * (MLA_NOPE + MLA_ROPE), MLA_Q_LORA), BF16),
                        pltpu.VMEM((MLA_KV_LORA, hh * MLA_NOPE), BF16),
                        pltpu.VMEM((hh * MLA_V, MLA_KV_LORA), BF16)],
        compiler_params=_cparams(("arbitrary", "arbitrary")),
        name="mla_qkv",
    )(proj, proj, proj, positions.reshape(b, 1, l), *consts, *weights)


def _flash_kernel(qt_ref, kn_ref, kr_ref, vt_ref, o_ref, m_ref, acc_ref, *, tq, hp, ahead):
    qi = pl.program_id(2)
    m_ref[...] = jnp.full(m_ref.shape, -jnp.inf, F32)
    acc_ref[...] = jnp.zeros(acc_ref.shape, F32)

    half = tq // 2
    lower = (lax.broadcasted_iota(jnp.int32, (half, half), 0)
             <= lax.broadcasted_iota(jnp.int32, (half, half), 1))

    def blocks(j, parts, diagonal):
        base = pl.multiple_of(j * tq, tq)
        items = [(h, pl.ds(base + k0, nk), slice(q0, q0 + nq)) for k0, nk, q0, nq in parts for h in range(hp)]

        def scores(h, rows, cols):
            k = jnp.concatenate([kn_ref[0, h, rows, :], kr_ref[0, rows, :]], axis=-1)
            return jnp.dot(k, qt_ref[0, h, :, cols], preferred_element_type=F32)

        pending = [scores(*it) for it in items[:ahead]]
        for n, (h, rows, cols) in enumerate(items):
            if n + ahead < len(items):
                pending.append(scores(*items[n + ahead]))
            s = pending[n]
            if diagonal:
                square = jnp.where(lower, s[:, :half], jnp.finfo(F32).min)
                s = square if s.shape[1] == half else jnp.concatenate([square, s[:, half:]], axis=1)
            m = m_ref[h, :, cols]
            m_new = jnp.maximum(m, jnp.max(s, axis=0, keepdims=True))
            alpha = jnp.exp2(m - m_new)
            p = jnp.exp2(s - m_new)
            acc_ref[h, :, cols] = alpha * acc_ref[h, :, cols] + jnp.dot(
                vt_ref[0, h, :, rows], p.astype(BF16), preferred_element_type=F32)
            m_ref[h, :, cols] = m_new

    def body(j, carry):
        blocks(j, [(0, tq, 0, tq)], False)
        return carry

    lax.fori_loop(0, qi, body, 0)
    blocks(qi, [(0, half, 0, tq), (half, half, half, half)], True)
    for h in range(hp):
        o_ref[0, :, h * MLA_V:(h + 1) * MLA_V] = (
            acc_ref[h, :MLA_V, :] / acc_ref[h, MLA_V:MLA_V + 1, :]).T.astype(o_ref.dtype)


def _flash(qt, kn, kr, vt, *, tq, hp, ahead):
    b, hh, _, l = qt.shape
    return pl.pallas_call(
        functools.partial(_flash_kernel, tq=tq, hp=hp, ahead=ahead),
        grid=(b, hh // hp, l // tq),
        in_specs=[pl.BlockSpec((1, hp, MLA_QK_PAD, tq), lambda i, h, j: (i, h, 0, j)),
                  pl.BlockSpec((1, hp, l, MLA_NOPE), lambda i, h, j: (i, h, 0, 0)),
                  pl.BlockSpec((1, l, LANES), lambda i, h, j: (i, 0, 0)),
                  pl.BlockSpec((1, hp, MLA_VL, l), lambda i, h, j: (i, h, 0, 0))],
        out_specs=pl.BlockSpec((1, tq, hp * MLA_V), lambda i, h, j: (i, j, h)),
        out_shape=jax.ShapeDtypeStruct((b, l, hh * MLA_V), BF16),
        scratch_shapes=[pltpu.VMEM((hp, 1, tq), F32), pltpu.VMEM((hp, MLA_VL, tq), F32)],
        compiler_params=_cparams(("parallel", "parallel", "parallel")),
        name="flash",
    )(qt, kn, kr, vt)


def _s5_layer(x, ln, w_in, lam_re, lam_im, log_step, b_re, b_im, c_re, c_im, d, w_glu,
              w_out, mem_kv, xq_norm, layer, j):
    b, l, dm = x.shape
    tm = 512
    proj, xn = _s5_in_proj(x, ln, w_in, j, tm=tm, col_chunk=512)
    xg = _s5_ut(xn, w_in, j)
    pw_re, pw_im = _s5_pow(lam_re, lam_im, log_step)
    toep, wout, wst = _s5_asm(pw_re.transpose(1, 0, 2), pw_im.transpose(1, 0, 2), c_re, c_im, b_re, b_im)
    n_scan = int(math.log2(LANES))
    first = S5_EXPONENTS.index(S5_CHUNK)
    col = lambda pw: jnp.pad(pw[first:first + n_scan].transpose(1, 2, 0), ((0, 0), (0, 0), (0, 8 - n_scan)))
    dcol = jnp.tile(d.reshape(S5_GROUPS, 1, S5_GROUP_CH), (1, S5_CHUNK, 1)).reshape(S5_GROUPS, -1, 1)
    yt = _s5_mix(xg, toep, wst, wout, col(pw_re), col(pw_im), dcol, nb=b, gs=4)
    y = _glu(yt, w_glu, j, nb=b, col_chunk=256)
    return _merge(x, y, proj, BRANCH_WIDTH // XQ_WIDTH, 0, *mem_kv, xq_norm, w_out, layer, tm=tm, phased=True)


def _mla_layer(x, positions, ln, w_in, q_lora_norm, kv_lora_norm, w_uq, w_ukv, q_nope_norm, k_nope_norm,
               q_rope_norm, k_rope_norm, w_out, mem_kv, xq_norm, layer, j):
    b, l, dm = x.shape
    o1 = MLA_Q_LORA
    o2 = o1 + MLA_KV_LORA
    o3 = o2 + MLA_ROPE
    o4 = o3 + XQ_WIDTH
    segments = ((o4, BRANCH_WIDTH), (0, o1), (o3, XQ_WIDTH), (o1, MLA_KV_LORA), (o2, MLA_ROPE))
    wout = -(-(o4 + BRANCH_WIDTH) // 512) * 512
    proj = _mla_in_proj(x.reshape(b * l, dm), ln, w_in, j, segments, wout, tm=512, col_chunk=512)
    proj = proj.reshape(b, l, -1)
    gate_blk = 0
    cq_blk = BRANCH_WIDTH // MLA_Q_LORA
    xq_blk = (BRANCH_WIDTH + MLA_Q_LORA) // XQ_WIDTH
    ckv_blk = (BRANCH_WIDTH + MLA_Q_LORA + XQ_WIDTH) // MLA_KV_LORA
    kr_blk = (BRANCH_WIDTH + MLA_Q_LORA + XQ_WIDTH + MLA_KV_LORA) // LANES
    qt, kn, kr, vt = _mla_qkv(proj, cq_blk, ckv_blk, kr_blk, positions,
                              (q_lora_norm, kv_lora_norm, q_nope_norm, k_nope_norm, q_rope_norm, k_rope_norm),
                              w_uq, w_ukv, j, tm=256)
    attn = _flash(qt, kn, kr, vt, tq=512, hp=12, ahead=2)
    return _merge(x, attn, proj, xq_blk, gate_blk, *mem_kv, xq_norm, w_out, layer, tm=512)


def kernel(x, mem, positions, ln_gain, w_out, mem_norm, w_mem_kv, xq_norm, xk_norm,
           s5_w_in, s5_lambda_re, s5_lambda_im, s5_log_step, s5_b_re, s5_b_im, s5_c_re, s5_c_im,
           s5_d, s5_w_glu, mla_w_in, mla_q_lora_norm, mla_kv_lora_norm, mla_w_uq, mla_w_ukv,
           mla_q_nope_norm, mla_k_nope_norm, mla_q_rope_norm, mla_k_rope_norm):
    depth = ln_gain.shape[0]
    mem_kv = _mem_kv(mem, mem_norm, w_mem_kv, xk_norm)
    for i in range(depth):
        j = i // 2
        if i % 2 == 0:
            x = _s5_layer(x, ln_gain[i], s5_w_in, s5_lambda_re[j], s5_lambda_im[j], s5_log_step[j],
                          s5_b_re[j], s5_b_im[j], s5_c_re[j], s5_c_im[j], s5_d[j], s5_w_glu,
                          w_out, mem_kv, xq_norm, i, j)
        else:
            x = _mla_layer(x, positions, ln_gain[i], mla_w_in, mla_q_lora_norm[j], mla_kv_lora_norm[j],
                           mla_w_uq, mla_w_ukv, mla_q_nope_norm[j], mla_k_nope_norm[j],
                           mla_q_rope_norm[j], mla_k_rope_norm[j],
                           w_out, mem_kv, xq_norm, i, j)
    return x
```

```python
import functools
import math

import jax
import jax.numpy as jnp
from jax import lax
from jax.experimental import pallas as pl
from jax.experimental.pallas import tpu as pltpu

D_MODEL = 1024
BRANCH_WIDTH = 2 * D_MODEL
XQ_WIDTH = BRANCH_WIDTH // 4
PRIMARY_WIDTH = BRANCH_WIDTH - XQ_WIDTH
X_HEADS = 4
X_HEAD_DIM = XQ_WIDTH // X_HEADS
S5_GROUP_CH = 16
S5_GROUPS = PRIMARY_WIDTH // S5_GROUP_CH
S5_STATE = 64
MLA_NOPE = 128
MLA_ROPE = 64
MLA_V = 128
MLA_HEADS = PRIMARY_WIDTH // MLA_V
MLA_Q_LORA = D_MODEL // 2
MLA_KV_LORA = D_MODEL // 4
ROPE_THETA = 10000.0
EPS = 1e-6

LANES = 128
MLA_QK_PAD = 2 * LANES
BF16_SUBLANES = 16
MLA_VL = MLA_V + BF16_SUBLANES
S5_CHUNK = 2 * LANES // S5_GROUP_CH
S5_SCAN_EXPONENTS = [S5_CHUNK * 2 ** i for i in range(int(math.log2(LANES)))]
VMEM_LIMIT = 56 * 1024 * 1024

F32 = jnp.float32
BF16 = jnp.bfloat16


def _cparams(sem):
    return pltpu.CompilerParams(dimension_semantics=sem, vmem_limit_bytes=VMEM_LIMIT)


def _rms(x, g):
    return x * lax.rsqrt(jnp.mean(x * x, axis=-1, keepdims=True) + EPS) * g


def _layer_block(w, j, block=None, index=None):
    block = tuple(w.shape[1:]) if block is None else block
    index = (0,) * len(block) if index is None else index
    return (None,) + block, lambda *_: (j,) + index


def _mla_in_proj_kernel(x_ref, g_ref, w_ref, o_ref, wp_ref, *, segments, col_chunk):
    @pl.when(pl.program_id(0) == 0)
    def _():
        at = 0
        for start, width in segments:
            wp_ref[:, at:at + width] = w_ref[:, start:start + width].astype(BF16)
            at += width
        wp_ref[:, at:] = jnp.zeros((wp_ref.shape[0], wp_ref.shape[1] - at), BF16)

    xn = _rms(x_ref[...], g_ref[...]).astype(BF16)
    for c in range(o_ref.shape[1] // col_chunk):
        sl = slice(c * col_chunk, (c + 1) * col_chunk)
        o_ref[:, sl] = jnp.dot(xn, wp_ref[:, sl], preferred_element_type=F32).astype(o_ref.dtype)


def _mla_in_proj(x, g, w, j, segments, wout, *, tm, col_chunk):
    n, d = x.shape
    return pl.pallas_call(
        functools.partial(_mla_in_proj_kernel, segments=segments, col_chunk=col_chunk),
        grid=(n // tm,),
        in_specs=[pl.BlockSpec((tm, d), lambda i: (i, 0)),
                  pl.BlockSpec((1, d), lambda i: (0, 0)),
                  pl.BlockSpec(*_layer_block(w, j), pipeline_mode=pl.Buffered(1))],
        out_specs=pl.BlockSpec((tm, wout), lambda i: (i, 0)),
        out_shape=jax.ShapeDtypeStruct((n, wout), BF16),
        scratch_shapes=[pltpu.VMEM((d, wout), BF16)],
        compiler_params=_cparams(("arbitrary",)),
        name="mla_in_proj",
    )(x, g.reshape(1, d), w)


def _to_phase_order(a):
    n, d = a.shape
    return jnp.swapaxes(a.reshape(n // S5_CHUNK, S5_CHUNK, d), 0, 1).reshape(n, d)


def _from_phase_order(a):
    n, d = a.shape
    return jnp.swapaxes(a.reshape(S5_CHUNK, n // S5_CHUNK, d), 0, 1).reshape(n, d)


def _s5_in_proj_kernel(x_ref, g_ref, wg_ref, wx_ref, o_ref, xn_ref, *, tm, col_chunk):
    d = x_ref.shape[2]
    nc = tm // S5_CHUNK
    chunks = [slice(c * col_chunk, (c + 1) * col_chunk) for c in range(BRANCH_WIDTH // col_chunk)]
    weights = [wg_ref[:, sl].astype(BF16) for sl in chunks] + [wx_ref[...].astype(BF16)]
    chunks.append(slice(BRANCH_WIDTH, BRANCH_WIDTH + XQ_WIDTH))

    def normalise(k):
        xn = _rms(_to_phase_order(x_ref[0, k * tm:(k + 1) * tm, :]), g_ref[...]).astype(BF16)
        for s in range(S5_CHUNK):
            xn_ref[0, k * nc:(k + 1) * nc, s * d:(s + 1) * d] = xn[s * nc:(s + 1) * nc]
        return xn

    def project(k, xn):
        for sl, w in zip(chunks, weights):
            o_ref[0, k * tm:(k + 1) * tm, sl] = jnp.dot(xn, w, preferred_element_type=F32).astype(o_ref.dtype)

    n_sub = x_ref.shape[1] // tm
    prev = None
    for k in range(n_sub):
        xn = normalise(k)
        if prev is not None:
            project(k - 1, prev)
        prev = xn
    project(n_sub - 1, prev)


def _s5_in_proj(x, g, w_in, j, *, tm, col_chunk):
    b, l, d = x.shape
    wout = BRANCH_WIDTH + XQ_WIDTH
    tg = 2 * tm
    nc = tg // S5_CHUNK
    return pl.pallas_call(
        functools.partial(_s5_in_proj_kernel, tm=tm, col_chunk=col_chunk),
        grid=(b, l // tg),
        in_specs=[pl.BlockSpec((1, tg, d), lambda i, j: (i, j, 0)),
                  pl.BlockSpec((1, d), lambda i, j: (0, 0)),
                  pl.BlockSpec(*_layer_block(w_in, j, (d, BRANCH_WIDTH),
                                             (0, (PRIMARY_WIDTH + XQ_WIDTH) // BRANCH_WIDTH)),
                               pipeline_mode=pl.Buffered(1)),
                  pl.BlockSpec(*_layer_block(w_in, j, (d, XQ_WIDTH), (0, PRIMARY_WIDTH // XQ_WIDTH)),
                               pipeline_mode=pl.Buffered(1))],
        out_specs=[pl.BlockSpec((1, tg, wout), lambda i, j: (i, j, 0)),
                   pl.BlockSpec((1, nc, S5_CHUNK * d), lambda i, j: (i, j, 0))],
        out_shape=[jax.ShapeDtypeStruct((b, l, wout), BF16),
                   jax.ShapeDtypeStruct((b, l // S5_CHUNK, S5_CHUNK * d), BF16)],
        compiler_params=_cparams(("parallel", "parallel")),
        name="s5_in_proj",
    )(x, g.reshape(1, d), w_in, w_in)


def _mem_kv_kernel(m_ref, g_ref, w_ref, kg_ref, k_ref, v_ref):
    b, m, d = m_ref.shape
    w = w_ref[0].astype(BF16)
    for i in range(b):
        mn = _rms(m_ref[i], g_ref[0]).astype(BF16)
        kv = jnp.dot(mn, w, preferred_element_type=F32)
        for h in range(X_HEADS):
            sl = slice(h * X_HEAD_DIM, (h + 1) * X_HEAD_DIM)
            k_ref[0, i, :, sl] = _rms(kv[:, sl], kg_ref[0]).astype(BF16)
        v_ref[0, i] = kv[:, XQ_WIDTH:].astype(BF16)


def _mem_kv(mem, mem_norm, w_mem_kv, xk_norm):
    b, m, d = mem.shape
    depth = w_mem_kv.shape[0]
    out = jax.ShapeDtypeStruct((depth, b, m, XQ_WIDTH), BF16)
    return pl.pallas_call(
        _mem_kv_kernel,
        grid=(depth,),
        in_specs=[pl.BlockSpec((b, m, d), lambda n: (0, 0, 0)),
                  pl.BlockSpec((1, 1, d), lambda n: (n, 0, 0)),
                  pl.BlockSpec((1, d, 2 * XQ_WIDTH), lambda n: (n, 0, 0)),
                  pl.BlockSpec((1, 1, X_HEAD_DIM), lambda n: (n, 0, 0))],
        out_specs=[pl.BlockSpec((1, b, m, XQ_WIDTH), lambda n: (n, 0, 0, 0)),
                   pl.BlockSpec((1, b, m, XQ_WIDTH), lambda n: (n, 0, 0, 0))],
        out_shape=[out, out],
        compiler_params=_cparams(("parallel",)),
        name="mem_kv",
    )(mem, mem_norm.reshape(depth, 1, d), w_mem_kv, xk_norm.reshape(depth, 1, X_HEAD_DIM))


def _s5_pow_kernel(lr_ref, li_ref, ls_ref, pr_ref, pi_ref):
    step = jnp.exp(ls_ref[...])
    zr, zi = lr_ref[...] * step, li_ref[...] * step
    for n in range(pr_ref.shape[0]):
        if n < len(S5_SCAN_EXPONENTS):
            e = S5_SCAN_EXPONENTS[n]
            mag = jnp.exp(zr * e)
            pr_ref[n] = mag * jnp.cos(zi * e)
            pi_ref[n] = mag * jnp.sin(zi * e)
        else:
            pr_ref[n] = jnp.zeros_like(zr)
            pi_ref[n] = jnp.zeros_like(zr)


def _s5_pow(lam_re, lam_im, log_step):
    g, p = lam_re.shape
    out = jax.ShapeDtypeStruct((8, g, p), F32)
    return pl.pallas_call(_s5_pow_kernel, out_shape=[out, out], name="s5_pow")(
        lam_re, lam_im, log_step.reshape(g, 1))


def _s5_asm_kernel(lr_ref, li_ref, ls_ref, cr_ref, ci_ref, btr_ref, bti_ref, br_ref, bi_ref,
                   toep_ref, wout_ref, wst_ref):
    def group(i, carry):
        _s5_asm_group(i, lr_ref, li_ref, ls_ref, cr_ref, ci_ref, btr_ref, bti_ref, br_ref, bi_ref,
                      toep_ref, wout_ref, wst_ref)
        return carry

    lax.fori_loop(0, cr_ref.shape[0], group, 0)


def _dot_3pass_tiled(a, b):
    c = b.shape[1]
    rep = ((lax.broadcasted_iota(jnp.int32, (c, S5_CHUNK * c), 1) & (c - 1))
           == lax.broadcasted_iota(jnp.int32, (c, S5_CHUNK * c), 0)).astype(BF16)
    dot = functools.partial(jnp.dot, preferred_element_type=F32)
    a_hi, b_hi = a.astype(BF16), b.astype(BF16)
    a_lo = (a - a_hi.astype(F32)).astype(BF16)
    b_lo = (b - b_hi.astype(F32)).astype(BF16)
    b_hi, b_lo = dot(b_hi, rep).astype(BF16), dot(b_lo, rep).astype(BF16)
    return dot(a_hi, b_hi) + (dot(a_hi, b_lo) + dot(a_lo, b_hi))


def _s5_asm_group(i, lr_ref, li_ref, ls_ref, cr_ref, ci_ref, btr_ref, bti_ref, br_ref, bi_ref,
                  toep_ref, wout_ref, wst_ref):
    t = S5_CHUNK
    lr, li = lr_ref[i], li_ref[i]
    step = jnp.exp(ls_ref[i])
    zr, zi = lr * step, li * step
    e = lax.broadcasted_iota(jnp.int32, (3 * 8, lr.shape[1]), 0).astype(F32)
    mag = jnp.exp(e * zr)
    pr, pi = mag * jnp.cos(e * zi), mag * jnp.sin(e * zi)
    den = lr * lr + li * li
    mr = ((pr[1:2] - 1.0) * lr + pi[1:2] * li) / den
    mi = (pi[1:2] * lr - (pr[1:2] - 1.0) * li) / den
    cr, ci = cr_ref[i], ci_ref[i]
    btr, bti = btr_ref[i], bti_ref[i]
    amr = pr[:t] * mr - pi[:t] * mi
    ami = pr[:t] * mi + pi[:t] * mr
    l_re, l_im, w_re, w_im, o_re, o_im = [], [], [], [], [], []
    for k in range(t):
        ar, ai = amr[k:k + 1], ami[k:k + 1]
        l_re.append(cr * ar - ci * ai)
        l_im.append(-(cr * ai + ci * ar))
        ar, ai = amr[t - 1 - k:t - k], ami[t - 1 - k:t - k]
        w_re.append(btr * ar - bti * ai)
        w_im.append(btr * ai + bti * ar)
        ar, ai = pr[k + 1:k + 2], pi[k + 1:k + 2]
        o_re.append(cr * ar - ci * ai)
        o_im.append(-(cr * ai + ci * ar))
    cat = lambda parts: jnp.concatenate(parts, axis=0)
    kt = _dot_3pass_tiled(jnp.concatenate([cat(l_re), cat(l_im)], axis=1),
                          jnp.concatenate([br_ref[i], bi_ref[i]], axis=0))
    n = kt.shape[0]
    blk = lax.shift_right_logical(lax.broadcasted_iota(jnp.int32, kt.shape, 1), int(math.log2(S5_GROUP_CH)))
    toep = jnp.where(blk == 0, kt, 0.0)
    for s in range(1, t):
        shifted = jnp.concatenate([jnp.zeros((s * S5_GROUP_CH, n), F32), kt[:n - s * S5_GROUP_CH]], axis=0)
        toep = jnp.where(blk == s, shifted, toep)
    toep_ref[i] = toep.astype(BF16)
    wout_ref[i] = jnp.concatenate([cat(o_re), cat(o_im)], axis=1).astype(BF16)
    wst_ref[i] = jnp.concatenate([cat(w_re), cat(w_im)], axis=1).T.astype(BF16)


def _s5_asm(lam_re, lam_im, log_step, c_re, c_im, b_re, b_im):
    g, c, p = c_re.shape
    tc = S5_CHUNK * c
    assert S5_CHUNK < 3 * 8
    gs = 8
    blk = lambda a: pl.BlockSpec((gs,) + a.shape[1:], lambda i: (i, 0, 0))
    bt_re, bt_im = b_re.transpose(0, 2, 1), b_im.transpose(0, 2, 1)
    args = (lam_re.reshape(g, 1, p), lam_im.reshape(g, 1, p), log_step.reshape(g, 1, 1),
            c_re, c_im, bt_re, bt_im, b_re, b_im)
    return pl.pallas_call(
        _s5_asm_kernel,
        grid=(g // gs,),
        in_specs=[blk(a) for a in args],
        out_specs=[pl.BlockSpec((gs, tc, tc), lambda i: (i, 0, 0)),
                   pl.BlockSpec((gs, tc, 2 * p), lambda i: (i, 0, 0)),
                   pl.BlockSpec((gs, 2 * p, tc), lambda i: (i, 0, 0))],
        out_shape=[jax.ShapeDtypeStruct((g, tc, tc), BF16), jax.ShapeDtypeStruct((g, tc, 2 * p), BF16),
                   jax.ShapeDtypeStruct((g, 2 * p, tc), BF16)],
        compiler_params=_cparams(("parallel",)),
        name="s5_asm",
    )(*args)


def _s5_ut_kernel(xn_ref, w_ref, o_ref, wt_ref):
    @pl.when(pl.program_id(0) == 0)
    def _():
        wt_ref[...] = w_ref[...].T.astype(BF16)

    nb, nc, d = xn_ref.shape
    ut = lax.dot_general(wt_ref[...], xn_ref[...].reshape(nb * nc, d), (((1,), (1,)), ((), ())),
                         preferred_element_type=F32)
    o_ref[...] = ut.astype(BF16).reshape(o_ref.shape)


def _s5_ut(xn, w_in, j):
    b, nc, td = xn.shape
    d = td // S5_CHUNK
    return pl.pallas_call(
        _s5_ut_kernel,
        grid=(S5_CHUNK,),
        in_specs=[pl.BlockSpec((b, nc, d), lambda s: (0, 0, s)),
                  pl.BlockSpec(*_layer_block(w_in, j, (d, PRIMARY_WIDTH)))],
        out_specs=pl.BlockSpec((S5_GROUPS, S5_GROUP_CH, b * nc), lambda s: (0, s, 0)),
        out_shape=jax.ShapeDtypeStruct((S5_GROUPS, S5_CHUNK * S5_GROUP_CH, b * nc), BF16),
        scratch_shapes=[pltpu.VMEM((PRIMARY_WIDTH, d), BF16)],
        compiler_params=_cparams(("arbitrary",)),
        name="s5_ut",
    )(xn, w_in)


def _s5_mix_kernel(x_ref, toep_ref, wst_ref, wout_ref, sr_ref, si_ref, d_ref, o_ref, *, nb):
    p = S5_STATE
    gs = x_ref.shape[0]
    lane = lax.broadcasted_iota(jnp.int32, (p, LANES), 1)
    n_steps = int(math.log2(LANES))

    def scan(g, hloc):
        pw = []
        for i in range(n_steps):
            keep = lane >= (1 << i)
            pw.append((jnp.where(keep, jnp.broadcast_to(sr_ref[g, :, i:i + 1], (p, LANES)), 0.0),
                       jnp.where(keep, jnp.broadcast_to(si_ref[g, :, i:i + 1], (p, LANES)), 0.0)))
        h_re = [hloc[:p, b * LANES:(b + 1) * LANES] for b in range(nb)]
        h_im = [hloc[p:, b * LANES:(b + 1) * LANES] for b in range(nb)]
        for i in range(n_steps):
            ar, ai = pw[i]
            r_sh = [pltpu.roll(v, 1 << i, 1) for v in h_re]
            i_sh = [pltpu.roll(v, 1 << i, 1) for v in h_im]
            h_re = [h_re[b] + ar * r_sh[b] - ai * i_sh[b] for b in range(nb)]
            h_im = [h_im[b] + ar * i_sh[b] + ai * r_sh[b] for b in range(nb)]
        h_re = [jnp.where(lane >= 1, pltpu.roll(v, 1, 1), 0.0) for v in h_re]
        h_im = [jnp.where(lane >= 1, pltpu.roll(v, 1, 1), 0.0) for v in h_im]
        return jnp.concatenate([jnp.concatenate(h_re, axis=1), jnp.concatenate(h_im, axis=1)],
                               axis=0).astype(BF16)

    def outputs(g, h):
        x = x_ref[g]
        y = (jnp.dot(toep_ref[g], x, preferred_element_type=F32)
             + jnp.dot(wout_ref[g], h, preferred_element_type=F32)
             + d_ref[g] * x.astype(F32))
        o_ref[:, g * S5_GROUP_CH:(g + 1) * S5_GROUP_CH, :] = (
            jax.nn.gelu(y).astype(o_ref.dtype).reshape(S5_CHUNK, S5_GROUP_CH, y.shape[1]))

    h_prev = None
    for g in range(gs):
        hloc = jnp.dot(wst_ref[g], x_ref[g], preferred_element_type=F32)
        h = scan(g, hloc)
        if h_prev is not None:
            outputs(g - 1, h_prev)
        h_prev = h
    outputs(gs - 1, h_prev)


def _s5_mix(xg, toep, wst, wout, sc_re, sc_im, dcol, *, nb, gs):
    g, tc, cols = xg.shape
    assert cols == nb * LANES, "one batch's chunks must fill exactly one 128-lane block"
    blk = lambda a: pl.BlockSpec((gs,) + a.shape[1:], lambda i: (i, 0, 0))
    return pl.pallas_call(
        functools.partial(_s5_mix_kernel, nb=nb),
        grid=(g // gs,),
        in_specs=[blk(a) for a in (xg, toep, wst, wout, sc_re, sc_im, dcol)],
        out_specs=pl.BlockSpec((S5_CHUNK, gs * S5_GROUP_CH, cols), lambda i: (0, i, 0)),
        out_shape=jax.ShapeDtypeStruct((S5_CHUNK, g * S5_GROUP_CH, cols), BF16),
        compiler_params=_cparams(("parallel",)),
        name="s5_mix",
    )(xg, toep, wst, wout, sc_re, sc_im, dcol)


def _glu_kernel(y_ref, w_ref, o_ref, *, col_chunk):
    y = y_ref[0].T
    half = o_ref.shape[-1]
    for c in range(half // col_chunk):
        wa = w_ref[:, c * col_chunk:(c + 1) * col_chunk].astype(BF16)
        wg = w_ref[:, half + c * col_chunk:half + (c + 1) * col_chunk].astype(BF16)
        a = jnp.dot(y, wa, preferred_element_type=F32)
        g = jnp.dot(y, wg, preferred_element_type=F32)
        o_ref[:, :, c * col_chunk:(c + 1) * col_chunk] = (
            (a * jax.nn.sigmoid(g)).astype(o_ref.dtype).reshape(o_ref.shape[:2] + (col_chunk,)))


def _glu(yt, w, layer, *, nb, col_chunk):
    t, k, cols = yt.shape
    nc = cols // nb
    half = w.shape[2] // 2
    return pl.pallas_call(
        functools.partial(_glu_kernel, col_chunk=col_chunk),
        grid=(t,),
        in_specs=[pl.BlockSpec((1, k, cols), lambda j: (j, 0, 0)),
                  pl.BlockSpec(*_layer_block(w, layer), pipeline_mode=pl.Buffered(1))],
        out_specs=pl.BlockSpec((nb, nc, half), lambda j: (0, 0, j)),
        out_shape=jax.ShapeDtypeStruct((nb, nc, t * half), BF16),
        compiler_params=_cparams(("parallel",)),
        name="glu",
    )(yt, w)


def _merge_kernel(x_ref, mix_ref, xq_ref, gate_ref, k_ref, v_ref, qg_ref, w_ref, o_ref, *cat_refs, tm, phased):
    scale = X_HEAD_DIM ** -0.5
    nc = tm // S5_CHUNK
    w = w_ref[...].astype(BF16)

    def gather(k, cat_ref):
        r0 = k * tm
        gate = gate_ref[0, r0:r0 + tm, :]
        sg = gate * jax.nn.sigmoid(gate)
        if phased:
            for s in range(S5_CHUNK):
                rows = slice(s * nc, (s + 1) * nc)
                mix = mix_ref[0, k * nc:(k + 1) * nc, s * PRIMARY_WIDTH:(s + 1) * PRIMARY_WIDTH]
                cat_ref[rows, :PRIMARY_WIDTH] = mix * sg[rows, :PRIMARY_WIDTH]
        else:
            cat_ref[:, :PRIMARY_WIDTH] = mix_ref[0, r0:r0 + tm, :] * sg[:, :PRIMARY_WIDTH]
        for h in range(X_HEADS):
            sl = slice(h * X_HEAD_DIM, (h + 1) * X_HEAD_DIM)
            q = _rms(xq_ref[0, r0:r0 + tm, sl].astype(F32), qg_ref[...]).astype(BF16)
            s = lax.dot_general(q, k_ref[0, :, sl], (((1,), (1,)), ((), ())), preferred_element_type=F32) * scale
            p = jnp.exp(s - jnp.max(s, axis=-1, keepdims=True))
            p = (p / jnp.sum(p, axis=-1, keepdims=True)).astype(BF16)
            mo = jnp.dot(p, v_ref[0, :, sl], preferred_element_type=F32)
            osl = slice(PRIMARY_WIDTH + h * X_HEAD_DIM, PRIMARY_WIDTH + (h + 1) * X_HEAD_DIM)
            cat_ref[:, osl] = mo.astype(BF16) * sg[:, osl]

    def project(k, cat_ref):
        r0 = k * tm
        delta = jnp.dot(cat_ref[...], w, preferred_element_type=F32)
        o_ref[0, r0:r0 + tm, :] = x_ref[0, r0:r0 + tm, :] + (_from_phase_order(delta) if phased else delta)

    n_sub = len(cat_refs)
    for k in range(n_sub):
        gather(k, cat_refs[k])
        if k > 0:
            project(k - 1, cat_refs[k - 1])
    project(n_sub - 1, cat_refs[n_sub - 1])


def _merge(x, mix, proj, xq_blk, gate_blk, mk, mv, xq_norm, w_out, layer, *, tm, phased=False):
    b, l, d = x.shape
    m = mk.shape[2]
    n_sub = 2
    tg = n_sub * tm
    mix_spec = (pl.BlockSpec((1, tg // S5_CHUNK, S5_CHUNK * PRIMARY_WIDTH), lambda i, j: (i, j, 0)) if phased
                else pl.BlockSpec((1, tg, PRIMARY_WIDTH), lambda i, j: (i, j, 0)))
    return pl.pallas_call(
        functools.partial(_merge_kernel, tm=tm, phased=phased),
        grid=(b, l // tg),
        in_specs=[pl.BlockSpec((1, tg, d), lambda i, j: (i, j, 0)),
                  mix_spec,
                  pl.BlockSpec((1, tg, XQ_WIDTH), lambda i, j: (i, j, xq_blk)),
                  pl.BlockSpec((1, tg, BRANCH_WIDTH), lambda i, j: (i, j, gate_blk)),
                  pl.BlockSpec((None, 1, m, XQ_WIDTH), lambda i, j: (layer, i, 0, 0)),
                  pl.BlockSpec((None, 1, m, XQ_WIDTH), lambda i, j: (layer, i, 0, 0)),
                  pl.BlockSpec((None, 1, X_HEAD_DIM), lambda i, j: (layer, 0, 0)),
                  pl.BlockSpec((None, BRANCH_WIDTH, d), lambda i, j: (layer, 0, 0),
                               pipeline_mode=pl.Buffered(1))],
        out_specs=pl.BlockSpec((1, tg, d), lambda i, j: (i, j, 0)),
        out_shape=jax.ShapeDtypeStruct((b, l, d), F32),
        scratch_shapes=[pltpu.VMEM((tm, BRANCH_WIDTH), BF16)] * n_sub,
        compiler_params=_cparams(("parallel", "parallel")),
        name="merge",
    )(x, mix, proj, proj, mk, mv, xq_norm.reshape(-1, 1, X_HEAD_DIM), w_out)


def _mla_qkv_kernel(cq_ref, ckv_ref, kr_ref, posr_ref, invfc_ref, gq_ref, gkv_ref, gqn_ref,
                    gkn_ref, gqr_ref, gkr_ref, wuq_ref, wukv_ref, qt_ref, kn_ref, krope_ref, vt_ref,
                    wqt_ref, wk_ref, wvt_ref):
    half = MLA_ROPE // 2
    tm = cq_ref.shape[1]
    qk = MLA_NOPE + MLA_ROPE

    @pl.when((pl.program_id(0) == 0) & (pl.program_id(1) == 0))
    def _():
        wqt_ref[...] = wuq_ref[...].T.astype(BF16)
        for h in range(MLA_HEADS):
            c0 = h * (MLA_NOPE + MLA_V)
            wk_ref[:, h * MLA_NOPE:(h + 1) * MLA_NOPE] = wukv_ref[:, c0:c0 + MLA_NOPE].astype(BF16)
            wvt_ref[h * MLA_V:(h + 1) * MLA_V, :] = wukv_ref[:, c0 + MLA_NOPE:c0 + MLA_NOPE + MLA_V].T.astype(BF16)

    qscale = (MLA_NOPE + MLA_ROPE) ** -0.5 * math.log2(math.e)

    cq = _rms(cq_ref[0].astype(F32), gq_ref[...])
    ckv = _rms(ckv_ref[0].astype(F32), gkv_ref[...])
    cq_t = cq.T.astype(BF16)
    ckv_t = ckv.T.astype(BF16)
    ckv_b = ckv.astype(BF16)

    def project(h):
        dot = functools.partial(jnp.dot, preferred_element_type=F32)
        k_pair = dot(ckv_b, wk_ref[:, h * MLA_NOPE:(h + 2) * MLA_NOPE]) if h % 2 == 0 else None
        return (dot(wqt_ref[h * qk:(h + 1) * qk, :], cq_t),
                dot(wvt_ref[h * MLA_V:(h + 1) * MLA_V, :], ckv_t), k_pair)

    ang_t = invfc_ref[...] * posr_ref[0].astype(F32)
    cos_t, sin_t = jnp.cos(ang_t), jnp.sin(ang_t)
    g_nope = jnp.broadcast_to(gqn_ref[...], (MLA_NOPE, tm)) * qscale
    g_r1 = jnp.broadcast_to(gqr_ref[:half, :], (half, tm)) * qscale
    g_r2 = jnp.broadcast_to(gqr_ref[half:, :], (half, tm)) * qscale
    ahead = 2
    pending = [project(h) for h in range(ahead)]
    for h in range(MLA_HEADS):
        if h + ahead < MLA_HEADS:
            pending.append(project(h + ahead))
        q, v_t, _ = pending[h]
        k_n = pending[h - h % 2][2][:, (h % 2) * MLA_NOPE:(h % 2 + 1) * MLA_NOPE]
        nope = q[:MLA_NOPE]
        r = lax.rsqrt(jnp.mean(nope * nope, axis=0, keepdims=True) + EPS)
        qt_ref[0, h, :MLA_NOPE, :] = (nope * r * g_nope).astype(BF16)
        x1, x2 = q[MLA_NOPE:MLA_NOPE + half], q[MLA_NOPE + half:MLA_NOPE + MLA_ROPE]
        ss = jnp.sum(x1 * x1, axis=0, keepdims=True) + jnp.sum(x2 * x2, axis=0, keepdims=True)
        r = lax.rsqrt(ss * (1.0 / MLA_ROPE) + EPS)
        x1, x2 = x1 * r * g_r1, x2 * r * g_r2
        qt_ref[0, h, MLA_NOPE:MLA_NOPE + half, :] = (x1 * cos_t - x2 * sin_t).astype(BF16)
        qt_ref[0, h, MLA_NOPE + half:MLA_NOPE + MLA_ROPE, :] = (x1 * sin_t + x2 * cos_t).astype(BF16)
        qt_ref[0, h, MLA_NOPE + MLA_ROPE:, :] = jnp.zeros((MLA_QK_PAD - MLA_NOPE - MLA_ROPE, tm), BF16)
        kn_ref[0, h] = _rms(k_n, gkn_ref[...]).astype(BF16)
        vt_ref[0, h, :MLA_V, :] = v_t.astype(BF16)
        vt_ref[0, h, MLA_V:, :] = jnp.ones((MLA_VL - MLA_V, tm), BF16)

    kr_t = kr_ref[0].astype(F32).T
    x1, x2 = kr_t[:half], kr_t[half:MLA_ROPE]
    ss = jnp.sum(x1 * x1, axis=0, keepdims=True) + jnp.sum(x2 * x2, axis=0, keepdims=True)
    r = lax.rsqrt(ss * (1.0 / MLA_ROPE) + EPS)
    x1, x2 = x1 * r * gkr_ref[:half, :], x2 * r * gkr_ref[half:, :]
    rot = jnp.concatenate([x1 * cos_t - x2 * sin_t, x1 * sin_t + x2 * cos_t,
                           jnp.zeros((LANES - MLA_ROPE, tm), F32)], axis=0)
    krope_ref[0] = rot.T.astype(BF16)


def _mla_qkv(proj, cq_blk, ckv_blk, kr_blk, positions, gains, w_uq, w_ukv, layer, *, tm):
    b, l, _ = proj.shape
    hh = MLA_HEADS
    half = MLA_ROPE // 2
    inv_freq = ROPE_THETA ** (-jnp.arange(half, dtype=F32) / half)
    const = lambda a: pl.BlockSpec(a.shape, lambda i, j: (0,) * a.ndim)
    gq, gkv, gqn, gkn, gqr, gkr = gains
    consts = [inv_freq.reshape(half, 1), gq.reshape(1, -1), gkv.reshape(1, -1), gqn.reshape(-1, 1),
              gkn.reshape(1, -1), gqr.reshape(-1, 1), gkr.reshape(-1, 1)]
    weights = [w_uq, w_ukv]
    return pl.pallas_call(
        _mla_qkv_kernel,
        grid=(b, l // tm),
        in_specs=[pl.BlockSpec((1, tm, MLA_Q_LORA), lambda i, j: (i, j, cq_blk)),
                  pl.BlockSpec((1, tm, MLA_KV_LORA), lambda i, j: (i, j, ckv_blk)),
                  pl.BlockSpec((1, tm, LANES), lambda i, j: (i, j, kr_blk)),
                  pl.BlockSpec((1, 1, tm), lambda i, j: (i, 0, j))] + [const(a) for a in consts]
                 + [pl.BlockSpec(*_layer_block(w, layer)) for w in weights],
        out_specs=[pl.BlockSpec((1, hh, MLA_QK_PAD, tm), lambda i, j: (i, 0, 0, j)),
                   pl.BlockSpec((1, hh, tm, MLA_NOPE), lambda i, j: (i, 0, j, 0)),
                   pl.BlockSpec((1, tm, LANES), lambda i, j: (i, j, 0)),
                   pl.BlockSpec((1, hh, MLA_VL, tm), lambda i, j: (i, 0, 0, j))],
        out_shape=[jax.ShapeDtypeStruct((b, hh, MLA_QK_PAD, l), BF16),
                   jax.ShapeDtypeStruct((b, hh, l, MLA_NOPE), BF16),
                   jax.ShapeDtypeStruct((b, l, LANES), BF16),
                   jax.ShapeDtypeStruct((b, hh, MLA_VL, l), BF16)],
        scratch_shapes=[pltpu.VMEM((hh * (MLA_NOPE + MLA_ROPE), MLA_Q_LORA), BF16),
                        pltpu.VMEM((MLA_KV_LORA, hh * MLA_NOPE), BF16),
                        pltpu.VMEM((hh * MLA_V, MLA_KV_LORA), BF16)],
        compiler_params=_cparams(("arbitrary", "arbitrary")),
        name="mla_qkv",
    )(proj, proj, proj, positions.reshape(b, 1, l), *consts, *weights)


def _flash_kernel(qt_ref, kn_ref, kr_ref, vt_ref, o_ref, m_ref, acc_ref, *, tq, hp, ahead):
    qi = pl.program_id(2)
    m_ref[...] = jnp.full(m_ref.shape, -jnp.inf, F32)
    acc_ref[...] = jnp.zeros(acc_ref.shape, F32)

    half = tq // 2
    lower = (lax.broadcasted_iota(jnp.int32, (half, half), 0)
             <= lax.broadcasted_iota(jnp.int32, (half, half), 1))

    def blocks(j, parts, diagonal):
        base = pl.multiple_of(j * tq, tq)
        items = [(h, pl.ds(base + k0, nk), slice(q0, q0 + nq)) for k0, nk, q0, nq in parts for h in range(hp)]

        def scores(h, rows, cols):
            k = jnp.concatenate([kn_ref[0, h, rows, :], kr_ref[0, rows, :]], axis=-1)
            return jnp.dot(k, qt_ref[0, h, :, cols], preferred_element_type=F32)

        pending = [scores(*it) for it in items[:ahead]]
        for n, (h, rows, cols) in enumerate(items):
            if n + ahead < len(items):
                pending.append(scores(*items[n + ahead]))
            s = pending[n]
            if diagonal:
                square = jnp.where(lower, s[:, :half], jnp.finfo(F32).min)
                s = square if s.shape[1] == half else jnp.concatenate([square, s[:, half:]], axis=1)
            m = m_ref[h, :, cols]
            m_new = jnp.maximum(m, jnp.max(s, axis=0, keepdims=True))
            alpha = jnp.exp2(m - m_new)
            p = jnp.exp2(s - m_new)
            acc_ref[h, :, cols] = alpha * acc_ref[h, :, cols] + jnp.dot(
                vt_ref[0, h, :, rows], p.astype(BF16), preferred_element_type=F32)
            m_ref[h, :, cols] = m_new

    def body(j, carry):
        blocks(j, [(0, tq, 0, tq)], False)
        return carry

    lax.fori_loop(0, qi, body, 0)
    blocks(qi, [(0, half, 0, tq), (half, half, half, half)], True)
    for h in range(hp):
        o_ref[0, :, h * MLA_V:(h + 1) * MLA_V] = (
            acc_ref[h, :MLA_V, :] / acc_ref[h, MLA_V:MLA_V + 1, :]).T.astype(o_ref.dtype)


def _flash(qt, kn, kr, vt, *, tq, hp, ahead):
    b, hh, _, l = qt.shape
    return pl.pallas_call(
        functools.partial(_flash_kernel, tq=tq, hp=hp, ahead=ahead),
        grid=(b, hh // hp, l // tq),
        in_specs=[pl.BlockSpec((1, hp, MLA_QK_PAD, tq), lambda i, h, j: (i, h, 0, j)),
                  pl.BlockSpec((1, hp, l, MLA_NOPE), lambda i, h, j: (i, h, 0, 0)),
                  pl.BlockSpec((1, l, LANES), lambda i, h, j: (i, 0, 0)),
                  pl.BlockSpec((1, hp, MLA_VL, l), lambda i, h, j: (i, h, 0, 0))],
        out_specs=pl.BlockSpec((1, tq, hp * MLA_V), lambda i, h, j: (i, j, h)),
        out_shape=jax.ShapeDtypeStruct((b, l, hh * MLA_V), BF16),
        scratch_shapes=[pltpu.VMEM((hp, 1, tq), F32), pltpu.VMEM((hp, MLA_VL, tq), F32)],
        compiler_params=_cparams(("parallel", "parallel", "parallel")),
        name="flash",
    )(qt, kn, kr, vt)


def _s5_layer(x, ln, w_in, lam_re, lam_im, log_step, b_re, b_im, c_re, c_im, d, w_glu,
              w_out, mem_kv, xq_norm, layer, j):
    b, l, dm = x.shape
    tm = 512
    proj, xn = _s5_in_proj(x, ln, w_in, j, tm=tm, col_chunk=512)
    xg = _s5_ut(xn, w_in, j)
    toep, wout, wst = _s5_asm(lam_re, lam_im, log_step, c_re, c_im, b_re, b_im)
    pw_re, pw_im = _s5_pow(lam_re, lam_im, log_step)
    col = lambda pw: pw.transpose(1, 2, 0)
    dcol = jnp.tile(d.reshape(S5_GROUPS, 1, S5_GROUP_CH), (1, S5_CHUNK, 1)).reshape(S5_GROUPS, -1, 1)
    yt = _s5_mix(xg, toep, wst, wout, col(pw_re), col(pw_im), dcol, nb=b, gs=8)
    y = _glu(yt, w_glu, j, nb=b, col_chunk=256)
    return _merge(x, y, proj, BRANCH_WIDTH // XQ_WIDTH, 0, *mem_kv, xq_norm, w_out, layer, tm=tm, phased=True)


def _mla_layer(x, positions, ln, w_in, q_lora_norm, kv_lora_norm, w_uq, w_ukv, q_nope_norm, k_nope_norm,
               q_rope_norm, k_rope_norm, w_out, mem_kv, xq_norm, layer, j):
    b, l, dm = x.shape
    o1 = MLA_Q_LORA
    o2 = o1 + MLA_KV_LORA
    o3 = o2 + MLA_ROPE
    o4 = o3 + XQ_WIDTH
    segments = ((o4, BRANCH_WIDTH), (0, o1), (o3, XQ_WIDTH), (o1, MLA_KV_LORA), (o2, MLA_ROPE))
    wout = -(-(o4 + BRANCH_WIDTH) // 512) * 512
    proj = _mla_in_proj(x.reshape(b * l, dm), ln, w_in, j, segments, wout, tm=512, col_chunk=512)
    proj = proj.reshape(b, l, -1)
    gate_blk = 0
    cq_blk = BRANCH_WIDTH // MLA_Q_LORA
    xq_blk = (BRANCH_WIDTH + MLA_Q_LORA) // XQ_WIDTH
    ckv_blk = (BRANCH_WIDTH + MLA_Q_LORA + XQ_WIDTH) // MLA_KV_LORA
    kr_blk = (BRANCH_WIDTH + MLA_Q_LORA + XQ_WIDTH + MLA_KV_LORA) // LANES
    qt, kn, kr, vt = _mla_qkv(proj, cq_blk, ckv_blk, kr_blk, positions,
                              (q_lora_norm, kv_lora_norm, q_nope_norm, k_nope_norm, q_rope_norm, k_rope_norm),
                              w_uq, w_ukv, j, tm=256)
    attn = _flash(qt, kn, kr, vt, tq=512, hp=12, ahead=2)
    return _merge(x, attn, proj, xq_blk, gate_blk, *mem_kv, xq_norm, w_out, layer, tm=512)


def kernel(x, mem, positions, ln_gain, w_out, mem_norm, w_mem_kv, xq_norm, xk_norm,
           s5_w_in, s5_lambda_re, s5_lambda_im, s5_log_step, s5_b_re, s5_b_im, s5_c_re, s5_c_im,
           s5_d, s5_w_glu, mla_w_in, mla_q_lora_norm, mla_kv_lora_norm, mla_w_uq, mla_w_ukv,
           mla_q_nope_norm, mla_k_nope_norm, mla_q_rope_norm, mla_k_rope_norm):
    depth = ln_gain.shape[0]
    mem_kv = _mem_kv(mem, mem_norm, w_mem_kv, xk_norm)
    for i in range(depth):
        j = i // 2
        if i % 2 == 0:
            x = _s5_layer(x, ln_gain[i], s5_w_in, s5_lambda_re[j], s5_lambda_im[j], s5_log_step[j],
                          s5_b_re[j], s5_b_im[j], s5_c_re[j], s5_c_im[j], s5_d[j], s5_w_glu,
                          w_out, mem_kv, xq_norm, i, j)
        else:
            x = _mla_layer(x, positions, ln_gain[i], mla_w_in, mla_q_lora_norm[j], mla_kv_lora_norm[j],
                           mla_w_uq, mla_w_ukv, mla_q_nope_norm[j], mla_k_nope_norm[j],
                           mla_q_rope_norm[j], mla_k_rope_norm[j],
                           w_out, mem_kv, xq_norm, i, j)
    return x
```

```python
import functools
import math

import jax
import jax.numpy as jnp
from jax import lax
from jax.experimental import pallas as pl
from jax.experimental.pallas import tpu as pltpu

D_MODEL = 1024
BRANCH_WIDTH = 2 * D_MODEL
XQ_WIDTH = BRANCH_WIDTH // 4
PRIMARY_WIDTH = BRANCH_WIDTH - XQ_WIDTH
X_HEADS = 4
X_HEAD_DIM = XQ_WIDTH // X_HEADS
S5_GROUP_CH = 16
S5_GROUPS = PRIMARY_WIDTH // S5_GROUP_CH
S5_STATE = 64
MLA_NOPE = 128
MLA_ROPE = 64
MLA_V = 128
MLA_HEADS = PRIMARY_WIDTH // MLA_V
MLA_Q_LORA = D_MODEL // 2
MLA_KV_LORA = D_MODEL // 4
ROPE_THETA = 10000.0
EPS = 1e-6

LANES = 128
MLA_QK_PAD = 2 * LANES
F32_SUBLANES = 8
BF16_SUBLANES = 16
MLA_VL = MLA_V + BF16_SUBLANES
S5_CHUNK = 2 * LANES // S5_GROUP_CH
S5_SCAN_EXPONENTS = [S5_CHUNK * 2 ** i for i in range(int(math.log2(LANES)))]
VMEM_LIMIT = 56 * 1024 * 1024

F32 = jnp.float32
BF16 = jnp.bfloat16


def _cparams(sem):
    return pltpu.CompilerParams(dimension_semantics=sem, vmem_limit_bytes=VMEM_LIMIT)


def _rms(x, g):
    return x * lax.rsqrt(jnp.mean(x * x, axis=-1, keepdims=True) + EPS) * g


def _layer_block(w, j, block=None, index=None):
    block = tuple(w.shape[1:]) if block is None else block
    index = (0,) * len(block) if index is None else index
    return (None,) + block, lambda *_: (j,) + index


def _mla_in_proj_kernel(x_ref, g_ref, w_ref, o_ref, wp_ref, *, segments, col_chunk):
    @pl.when(pl.program_id(0) == 0)
    def _():
        at = 0
        for start, width in segments:
            wp_ref[:, at:at + width] = w_ref[:, start:start + width].astype(BF16)
            at += width
        wp_ref[:, at:] = jnp.zeros((wp_ref.shape[0], wp_ref.shape[1] - at), BF16)

    xn = _rms(x_ref[...], g_ref[...]).astype(BF16)
    for c in range(o_ref.shape[1] // col_chunk):
        sl = slice(c * col_chunk, (c + 1) * col_chunk)
        o_ref[:, sl] = jnp.dot(xn, wp_ref[:, sl], preferred_element_type=F32).astype(o_ref.dtype)


def _mla_in_proj(x, g, w, j, segments, wout, *, tm, col_chunk):
    n, d = x.shape
    return pl.pallas_call(
        functools.partial(_mla_in_proj_kernel, segments=segments, col_chunk=col_chunk),
        grid=(n // tm,),
        in_specs=[pl.BlockSpec((tm, d), lambda i: (i, 0)),
                  pl.BlockSpec((1, d), lambda i: (0, 0)),
                  pl.BlockSpec(*_layer_block(w, j), pipeline_mode=pl.Buffered(1))],
        out_specs=pl.BlockSpec((tm, wout), lambda i: (i, 0)),
        out_shape=jax.ShapeDtypeStruct((n, wout), BF16),
        scratch_shapes=[pltpu.VMEM((d, wout), BF16)],
        compiler_params=_cparams(("arbitrary",)),
        name="mla_in_proj",
    )(x, g.reshape(1, d), w)


def _to_phase_order(a):
    n, d = a.shape
    return jnp.swapaxes(a.reshape(n // S5_CHUNK, S5_CHUNK, d), 0, 1).reshape(n, d)


def _from_phase_order(a):
    n, d = a.shape
    return jnp.swapaxes(a.reshape(S5_CHUNK, n // S5_CHUNK, d), 0, 1).reshape(n, d)


def _s5_in_proj_kernel(x_ref, g_ref, wg_ref, wx_ref, o_ref, xn_ref, *, tm, col_chunk):
    d = x_ref.shape[2]
    nc = tm // S5_CHUNK
    chunks = [slice(c * col_chunk, (c + 1) * col_chunk) for c in range(BRANCH_WIDTH // col_chunk)]
    weights = [wg_ref[:, sl].astype(BF16) for sl in chunks] + [wx_ref[...].astype(BF16)]
    chunks.append(slice(BRANCH_WIDTH, BRANCH_WIDTH + XQ_WIDTH))

    def normalise(k):
        xn = _rms(_to_phase_order(x_ref[0, k * tm:(k + 1) * tm, :]), g_ref[...]).astype(BF16)
        for s in range(S5_CHUNK):
            xn_ref[0, k * nc:(k + 1) * nc, s * d:(s + 1) * d] = xn[s * nc:(s + 1) * nc]
        return xn

    def project(k, xn):
        for sl, w in zip(chunks, weights):
            o_ref[0, k * tm:(k + 1) * tm, sl] = jnp.dot(xn, w, preferred_element_type=F32).astype(o_ref.dtype)

    n_sub = x_ref.shape[1] // tm
    prev = None
    for k in range(n_sub):
        xn = normalise(k)
        if prev is not None:
            project(k - 1, prev)
        prev = xn
    project(n_sub - 1, prev)


def _s5_in_proj(x, g, w_in, j, *, tm, col_chunk):
    b, l, d = x.shape
    wout = BRANCH_WIDTH + XQ_WIDTH
    tg = 2 * tm
    nc = tg // S5_CHUNK
    return pl.pallas_call(
        functools.partial(_s5_in_proj_kernel, tm=tm, col_chunk=col_chunk),
        grid=(b, l // tg),
        in_specs=[pl.BlockSpec((1, tg, d), lambda i, j: (i, j, 0)),
                  pl.BlockSpec((1, d), lambda i, j: (0, 0)),
                  pl.BlockSpec(*_layer_block(w_in, j, (d, BRANCH_WIDTH),
                                             (0, (PRIMARY_WIDTH + XQ_WIDTH) // BRANCH_WIDTH)),
                               pipeline_mode=pl.Buffered(1)),
                  pl.BlockSpec(*_layer_block(w_in, j, (d, XQ_WIDTH), (0, PRIMARY_WIDTH // XQ_WIDTH)),
                               pipeline_mode=pl.Buffered(1))],
        out_specs=[pl.BlockSpec((1, tg, wout), lambda i, j: (i, j, 0)),
                   pl.BlockSpec((1, nc, S5_CHUNK * d), lambda i, j: (i, j, 0))],
        out_shape=[jax.ShapeDtypeStruct((b, l, wout), BF16),
                   jax.ShapeDtypeStruct((b, l // S5_CHUNK, S5_CHUNK * d), BF16)],
        compiler_params=_cparams(("parallel", "parallel")),
        name="s5_in_proj",
    )(x, g.reshape(1, d), w_in, w_in)


def _mem_kv_kernel(m_ref, g_ref, w_ref, kg_ref, k_ref, v_ref):
    b, m, d = m_ref.shape
    w = w_ref[0].astype(BF16)
    for i in range(b):
        mn = _rms(m_ref[i], g_ref[0]).astype(BF16)
        kv = jnp.dot(mn, w, preferred_element_type=F32)
        for h in range(X_HEADS):
            sl = slice(h * X_HEAD_DIM, (h + 1) * X_HEAD_DIM)
            k_ref[0, i, :, sl] = _rms(kv[:, sl], kg_ref[0]).astype(BF16)
        v_ref[0, i] = kv[:, XQ_WIDTH:].astype(BF16)


def _mem_kv(mem, mem_norm, w_mem_kv, xk_norm):
    b, m, d = mem.shape
    depth = w_mem_kv.shape[0]
    out = jax.ShapeDtypeStruct((depth, b, m, XQ_WIDTH), BF16)
    return pl.pallas_call(
        _mem_kv_kernel,
        grid=(depth,),
        in_specs=[pl.BlockSpec((b, m, d), lambda n: (0, 0, 0)),
                  pl.BlockSpec((1, 1, d), lambda n: (n, 0, 0)),
                  pl.BlockSpec((1, d, 2 * XQ_WIDTH), lambda n: (n, 0, 0)),
                  pl.BlockSpec((1, 1, X_HEAD_DIM), lambda n: (n, 0, 0))],
        out_specs=[pl.BlockSpec((1, b, m, XQ_WIDTH), lambda n: (n, 0, 0, 0)),
                   pl.BlockSpec((1, b, m, XQ_WIDTH), lambda n: (n, 0, 0, 0))],
        out_shape=[out, out],
        compiler_params=_cparams(("parallel",)),
        name="mem_kv",
    )(mem, mem_norm.reshape(depth, 1, d), w_mem_kv, xk_norm.reshape(depth, 1, X_HEAD_DIM))


def _s5_pow_kernel(lr_ref, li_ref, ls_ref, pr_ref, pi_ref):
    step = jnp.exp(ls_ref[...])
    zr, zi = lr_ref[...] * step, li_ref[...] * step
    for n in range(pr_ref.shape[0]):
        if n < len(S5_SCAN_EXPONENTS):
            e = S5_SCAN_EXPONENTS[n]
            mag = jnp.exp(zr * e)
            pr_ref[n] = mag * jnp.cos(zi * e)
            pi_ref[n] = mag * jnp.sin(zi * e)
        else:
            pr_ref[n] = jnp.zeros_like(zr)
            pi_ref[n] = jnp.zeros_like(zr)


def _s5_pow(lam_re, lam_im, log_step):
    g, p = lam_re.shape
    slots = -(-len(S5_SCAN_EXPONENTS) // F32_SUBLANES) * F32_SUBLANES
    out = jax.ShapeDtypeStruct((slots, g, p), F32)
    return pl.pallas_call(_s5_pow_kernel, out_shape=[out, out], name="s5_pow")(
        lam_re, lam_im, log_step.reshape(g, 1))


def _s5_asm_kernel(lr_ref, li_ref, ls_ref, cr_ref, ci_ref, btr_ref, bti_ref, br_ref, bi_ref, d_ref,
                   toep_ref, wout_ref, wst_ref, dcol_ref):
    def group(i, carry):
        _s5_asm_group(i, lr_ref, li_ref, ls_ref, cr_ref, ci_ref, btr_ref, bti_ref, br_ref, bi_ref,
                      toep_ref, wout_ref, wst_ref)
        dcol_ref[i] = jnp.concatenate([d_ref[i]] * S5_CHUNK, axis=0)
        return carry

    lax.fori_loop(0, cr_ref.shape[0], group, 0)


def _dot_3pass_tiled(a, b):
    c = b.shape[1]
    rep = ((lax.broadcasted_iota(jnp.int32, (c, S5_CHUNK * c), 1) & (c - 1))
           == lax.broadcasted_iota(jnp.int32, (c, S5_CHUNK * c), 0)).astype(BF16)
    dot = functools.partial(jnp.dot, preferred_element_type=F32)
    a_hi, b_hi = a.astype(BF16), b.astype(BF16)
    a_lo = (a - a_hi.astype(F32)).astype(BF16)
    b_lo = (b - b_hi.astype(F32)).astype(BF16)
    b_hi, b_lo = dot(b_hi, rep).astype(BF16), dot(b_lo, rep).astype(BF16)
    return dot(a_hi, b_hi) + (dot(a_hi, b_lo) + dot(a_lo, b_hi))


def _s5_asm_group(i, lr_ref, li_ref, ls_ref, cr_ref, ci_ref, btr_ref, bti_ref, br_ref, bi_ref,
                  toep_ref, wout_ref, wst_ref):
    t = S5_CHUNK
    lr, li = lr_ref[i], li_ref[i]
    step = jnp.exp(ls_ref[i])
    zr, zi = lr * step, li * step
    rows = -(-(t + 1) // F32_SUBLANES) * F32_SUBLANES
    e = lax.broadcasted_iota(jnp.int32, (rows, lr.shape[1]), 0).astype(F32)
    mag = jnp.exp(e * zr)
    pr, pi = mag * jnp.cos(e * zi), mag * jnp.sin(e * zi)
    den = lr * lr + li * li
    mr = ((pr[1:2] - 1.0) * lr + pi[1:2] * li) / den
    mi = (pi[1:2] * lr - (pr[1:2] - 1.0) * li) / den
    cr, ci = cr_ref[i], ci_ref[i]
    btr, bti = btr_ref[i], bti_ref[i]
    amr = pr[:t] * mr - pi[:t] * mi
    ami = pr[:t] * mi + pi[:t] * mr
    l_re, l_im, w_re, w_im, o_re, o_im = [], [], [], [], [], []
    for k in range(t):
        ar, ai = amr[k:k + 1], ami[k:k + 1]
        l_re.append(cr * ar - ci * ai)
        l_im.append(-(cr * ai + ci * ar))
        ar, ai = amr[t - 1 - k:t - k], ami[t - 1 - k:t - k]
        w_re.append(btr * ar - bti * ai)
        w_im.append(btr * ai + bti * ar)
        ar, ai = pr[k + 1:k + 2], pi[k + 1:k + 2]
        o_re.append(cr * ar - ci * ai)
        o_im.append(-(cr * ai + ci * ar))
    cat = lambda parts: jnp.concatenate(parts, axis=0)
    kt = _dot_3pass_tiled(jnp.concatenate([cat(l_re), cat(l_im)], axis=1),
                          jnp.concatenate([br_ref[i], bi_ref[i]], axis=0))
    n = kt.shape[0]
    blk = lax.shift_right_logical(lax.broadcasted_iota(jnp.int32, kt.shape, 1), int(math.log2(S5_GROUP_CH)))
    toep = jnp.where(blk == 0, kt, 0.0)
    for s in range(1, t):
        shifted = jnp.concatenate([jnp.zeros((s * S5_GROUP_CH, n), F32), kt[:n - s * S5_GROUP_CH]], axis=0)
        toep = jnp.where(blk == s, shifted, toep)
    toep_ref[i] = toep.astype(BF16)
    wout_ref[i] = jnp.concatenate([cat(o_re), cat(o_im)], axis=1).astype(BF16)
    wst_ref[i] = jnp.concatenate([cat(w_re), cat(w_im)], axis=1).T.astype(BF16)


def _s5_asm(lam_re, lam_im, log_step, c_re, c_im, b_re, b_im, d):
    g, c, p = c_re.shape
    tc = S5_CHUNK * c
    gs = 8
    blk = lambda a: pl.BlockSpec((gs,) + a.shape[1:], lambda i: (i, 0, 0))
    bt_re, bt_im = b_re.transpose(0, 2, 1), b_im.transpose(0, 2, 1)
    args = (lam_re.reshape(g, 1, p), lam_im.reshape(g, 1, p), log_step.reshape(g, 1, 1),
            c_re, c_im, bt_re, bt_im, b_re, b_im, d.reshape(g, c, 1))
    return pl.pallas_call(
        _s5_asm_kernel,
        grid=(g // gs,),
        in_specs=[blk(a) for a in args],
        out_specs=[pl.BlockSpec((gs, tc, tc), lambda i: (i, 0, 0)),
                   pl.BlockSpec((gs, tc, 2 * p), lambda i: (i, 0, 0)),
                   pl.BlockSpec((gs, 2 * p, tc), lambda i: (i, 0, 0)),
                   pl.BlockSpec((gs, tc, 1), lambda i: (i, 0, 0))],
        out_shape=[jax.ShapeDtypeStruct((g, tc, tc), BF16), jax.ShapeDtypeStruct((g, tc, 2 * p), BF16),
                   jax.ShapeDtypeStruct((g, 2 * p, tc), BF16), jax.ShapeDtypeStruct((g, tc, 1), F32)],
        compiler_params=_cparams(("parallel",)),
        name="s5_asm",
    )(*args)


def _s5_ut_kernel(xn_ref, w_ref, o_ref, wt_ref):
    @pl.when(pl.program_id(0) == 0)
    def _():
        wt_ref[...] = w_ref[...].T.astype(BF16)

    nb, nc, d = xn_ref.shape
    ut = lax.dot_general(wt_ref[...], xn_ref[...].reshape(nb * nc, d), (((1,), (1,)), ((), ())),
                         preferred_element_type=F32)
    o_ref[...] = ut.astype(BF16).reshape(o_ref.shape)


def _s5_ut(xn, w_in, j):
    b, nc, td = xn.shape
    d = td // S5_CHUNK
    return pl.pallas_call(
        _s5_ut_kernel,
        grid=(S5_CHUNK,),
        in_specs=[pl.BlockSpec((b, nc, d), lambda s: (0, 0, s)),
                  pl.BlockSpec(*_layer_block(w_in, j, (d, PRIMARY_WIDTH)))],
        out_specs=pl.BlockSpec((S5_GROUPS, S5_GROUP_CH, b * nc), lambda s: (0, s, 0)),
        out_shape=jax.ShapeDtypeStruct((S5_GROUPS, S5_CHUNK * S5_GROUP_CH, b * nc), BF16),
        scratch_shapes=[pltpu.VMEM((PRIMARY_WIDTH, d), BF16)],
        compiler_params=_cparams(("arbitrary",)),
        name="s5_ut",
    )(xn, w_in)


def _s5_mix_kernel(x_ref, toep_ref, wst_ref, wout_ref, sr_ref, si_ref, d_ref, o_ref, *, nb):
    p = S5_STATE
    gs = x_ref.shape[0]
    lane = lax.broadcasted_iota(jnp.int32, (p, LANES), 1)
    n_steps = int(math.log2(LANES))

    def scan(g, hloc):
        pw = []
        for i in range(n_steps):
            keep = lane >= (1 << i)
            pw.append((jnp.where(keep, jnp.broadcast_to(sr_ref[g, :, i:i + 1], (p, LANES)), 0.0),
                       jnp.where(keep, jnp.broadcast_to(si_ref[g, :, i:i + 1], (p, LANES)), 0.0)))
        h_re = [hloc[:p, b * LANES:(b + 1) * LANES] for b in range(nb)]
        h_im = [hloc[p:, b * LANES:(b + 1) * LANES] for b in range(nb)]
        for i in range(n_steps):
            ar, ai = pw[i]
            r_sh = [pltpu.roll(v, 1 << i, 1) for v in h_re]
            i_sh = [pltpu.roll(v, 1 << i, 1) for v in h_im]
            h_re = [h_re[b] + ar * r_sh[b] - ai * i_sh[b] for b in range(nb)]
            h_im = [h_im[b] + ar * i_sh[b] + ai * r_sh[b] for b in range(nb)]
        h_re = [jnp.where(lane >= 1, pltpu.roll(v, 1, 1), 0.0) for v in h_re]
        h_im = [jnp.where(lane >= 1, pltpu.roll(v, 1, 1), 0.0) for v in h_im]
        return jnp.concatenate([jnp.concatenate(h_re, axis=1), jnp.concatenate(h_im, axis=1)],
                               axis=0).astype(BF16)

    def outputs(g, h):
        x = x_ref[g]
        y = (jnp.dot(toep_ref[g], x, preferred_element_type=F32)
             + jnp.dot(wout_ref[g], h, preferred_element_type=F32)
             + d_ref[g] * x.astype(F32))
        o_ref[:, g * S5_GROUP_CH:(g + 1) * S5_GROUP_CH, :] = (
            jax.nn.gelu(y).astype(o_ref.dtype).reshape(S5_CHUNK, S5_GROUP_CH, y.shape[1]))

    h_prev = None
    for g in range(gs):
        hloc = jnp.dot(wst_ref[g], x_ref[g], preferred_element_type=F32)
        h = scan(g, hloc)
        if h_prev is not None:
            outputs(g - 1, h_prev)
        h_prev = h
    outputs(gs - 1, h_prev)


def _s5_mix(xg, toep, wst, wout, sc_re, sc_im, dcol, *, nb, gs):
    g, tc, cols = xg.shape
    assert cols == nb * LANES, "one batch's chunks must fill exactly one 128-lane block"
    blk = lambda a: pl.BlockSpec((gs,) + a.shape[1:], lambda i: (i, 0, 0))
    return pl.pallas_call(
        functools.partial(_s5_mix_kernel, nb=nb),
        grid=(g // gs,),
        in_specs=[blk(a) for a in (xg, toep, wst, wout, sc_re, sc_im, dcol)],
        out_specs=pl.BlockSpec((S5_CHUNK, gs * S5_GROUP_CH, cols), lambda i: (0, i, 0)),
        out_shape=jax.ShapeDtypeStruct((S5_CHUNK, g * S5_GROUP_CH, cols), BF16),
        compiler_params=_cparams(("parallel",)),
        name="s5_mix",
    )(xg, toep, wst, wout, sc_re, sc_im, dcol)


def _glu_kernel(y_ref, w_ref, o_ref, *, col_chunk):
    y = y_ref[0].T
    half = o_ref.shape[-1]
    for c in range(half // col_chunk):
        wa = w_ref[:, c * col_chunk:(c + 1) * col_chunk].astype(BF16)
        wg = w_ref[:, half + c * col_chunk:half + (c + 1) * col_chunk].astype(BF16)
        a = jnp.dot(y, wa, preferred_element_type=F32)
        g = jnp.dot(y, wg, preferred_element_type=F32)
        o_ref[:, :, c * col_chunk:(c + 1) * col_chunk] = (
            (a * jax.nn.sigmoid(g)).astype(o_ref.dtype).reshape(o_ref.shape[:2] + (col_chunk,)))


def _glu(yt, w, layer, *, nb, col_chunk):
    t, k, cols = yt.shape
    nc = cols // nb
    half = w.shape[2] // 2
    return pl.pallas_call(
        functools.partial(_glu_kernel, col_chunk=col_chunk),
        grid=(t,),
        in_specs=[pl.BlockSpec((1, k, cols), lambda j: (j, 0, 0)),
                  pl.BlockSpec(*_layer_block(w, layer), pipeline_mode=pl.Buffered(1))],
        out_specs=pl.BlockSpec((nb, nc, half), lambda j: (0, 0, j)),
        out_shape=jax.ShapeDtypeStruct((nb, nc, t * half), BF16),
        compiler_params=_cparams(("parallel",)),
        name="glu",
    )(yt, w)


def _merge_kernel(x_ref, mix_ref, xq_ref, gate_ref, k_ref, v_ref, qg_ref, w_ref, o_ref, *cat_refs, tm, phased):
    scale = X_HEAD_DIM ** -0.5
    nc = tm // S5_CHUNK
    w = w_ref[...].astype(BF16)

    def gather(k, cat_ref):
        r0 = k * tm
        gate = gate_ref[0, r0:r0 + tm, :]
        sg = gate * jax.nn.sigmoid(gate)
        if phased:
            for s in range(S5_CHUNK):
                rows = slice(s * nc, (s + 1) * nc)
                mix = mix_ref[0, k * nc:(k + 1) * nc, s * PRIMARY_WIDTH:(s + 1) * PRIMARY_WIDTH]
                cat_ref[rows, :PRIMARY_WIDTH] = mix * sg[rows, :PRIMARY_WIDTH]
        else:
            cat_ref[:, :PRIMARY_WIDTH] = mix_ref[0, r0:r0 + tm, :] * sg[:, :PRIMARY_WIDTH]
        for h in range(X_HEADS):
            sl = slice(h * X_HEAD_DIM, (h + 1) * X_HEAD_DIM)
            q = _rms(xq_ref[0, r0:r0 + tm, sl].astype(F32), qg_ref[...]).astype(BF16)
            s = lax.dot_general(q, k_ref[0, :, sl], (((1,), (1,)), ((), ())), preferred_element_type=F32) * scale
            p = jnp.exp(s - jnp.max(s, axis=-1, keepdims=True))
            p = (p / jnp.sum(p, axis=-1, keepdims=True)).astype(BF16)
            mo = jnp.dot(p, v_ref[0, :, sl], preferred_element_type=F32)
            osl = slice(PRIMARY_WIDTH + h * X_HEAD_DIM, PRIMARY_WIDTH + (h + 1) * X_HEAD_DIM)
            cat_ref[:, osl] = mo.astype(BF16) * sg[:, osl]

    def project(k, cat_ref):
        r0 = k * tm
        delta = jnp.dot(cat_ref[...], w, preferred_element_type=F32)
        o_ref[0, r0:r0 + tm, :] = x_ref[0, r0:r0 + tm, :] + (_from_phase_order(delta) if phased else delta)

    n_sub = len(cat_refs)
    for k in range(n_sub):
        gather(k, cat_refs[k])
        if k > 0:
            project(k - 1, cat_refs[k - 1])
    project(n_sub - 1, cat_refs[n_sub - 1])


def _merge(x, mix, proj, xq_blk, gate_blk, mk, mv, xq_norm, w_out, layer, *, tm, phased=False):
    b, l, d = x.shape
    m = mk.shape[2]
    n_sub = 2
    tg = n_sub * tm
    mix_spec = (pl.BlockSpec((1, tg // S5_CHUNK, S5_CHUNK * PRIMARY_WIDTH), lambda i, j: (i, j, 0)) if phased
                else pl.BlockSpec((1, tg, PRIMARY_WIDTH), lambda i, j: (i, j, 0)))
    return pl.pallas_call(
        functools.partial(_merge_kernel, tm=tm, phased=phased),
        grid=(b, l // tg),
        in_specs=[pl.BlockSpec((1, tg, d), lambda i, j: (i, j, 0)),
                  mix_spec,
                  pl.BlockSpec((1, tg, XQ_WIDTH), lambda i, j: (i, j, xq_blk)),
                  pl.BlockSpec((1, tg, BRANCH_WIDTH), lambda i, j: (i, j, gate_blk)),
                  pl.BlockSpec((None, 1, m, XQ_WIDTH), lambda i, j: (layer, i, 0, 0)),
                  pl.BlockSpec((None, 1, m, XQ_WIDTH), lambda i, j: (layer, i, 0, 0)),
                  pl.BlockSpec((None, 1, X_HEAD_DIM), lambda i, j: (layer, 0, 0)),
                  pl.BlockSpec((None, BRANCH_WIDTH, d), lambda i, j: (layer, 0, 0),
                               pipeline_mode=pl.Buffered(1))],
        out_specs=pl.BlockSpec((1, tg, d), lambda i, j: (i, j, 0)),
        out_shape=jax.ShapeDtypeStruct((b, l, d), F32),
        scratch_shapes=[pltpu.VMEM((tm, BRANCH_WIDTH), BF16)] * n_sub,
        compiler_params=_cparams(("parallel", "parallel")),
        name="merge",
    )(x, mix, proj, proj, mk, mv, xq_norm.reshape(-1, 1, X_HEAD_DIM), w_out)


def _mla_qkv_kernel(cq_ref, ckv_ref, kr_ref, posr_ref, invfc_ref, gq_ref, gkv_ref, gqn_ref,
                    gkn_ref, gqr_ref, gkr_ref, wuq_ref, wukv_ref, qt_ref, kn_ref, krope_ref, vt_ref,
                    wqt_ref, wk_ref, wvt_ref):
    half = MLA_ROPE // 2
    tm = cq_ref.shape[1]
    qk = MLA_NOPE + MLA_ROPE

    @pl.when((pl.program_id(0) == 0) & (pl.program_id(1) == 0))
    def _():
        wqt_ref[...] = wuq_ref[...].T.astype(BF16)
        for h in range(MLA_HEADS):
            c0 = h * (MLA_NOPE + MLA_V)
            wk_ref[:, h * MLA_NOPE:(h + 1) * MLA_NOPE] = wukv_ref[:, c0:c0 + MLA_NOPE].astype(BF16)
            wvt_ref[h * MLA_V:(h + 1) * MLA_V, :] = wukv_ref[:, c0 + MLA_NOPE:c0 + MLA_NOPE + MLA_V].T.astype(BF16)

    qscale = (MLA_NOPE + MLA_ROPE) ** -0.5 * math.log2(math.e)

    cq = _rms(cq_ref[0].astype(F32), gq_ref[...])
    ckv = _rms(ckv_ref[0].astype(F32), gkv_ref[...])
    cq_t = cq.T.astype(BF16)
    ckv_t = ckv.T.astype(BF16)
    ckv_b = ckv.astype(BF16)

    def project(h):
        dot = functools.partial(jnp.dot, preferred_element_type=F32)
        k_pair = dot(ckv_b, wk_ref[:, h * MLA_NOPE:(h + 2) * MLA_NOPE]) if h % 2 == 0 else None
        return (dot(wqt_ref[h * qk:(h + 1) * qk, :], cq_t),
                dot(wvt_ref[h * MLA_V:(h + 1) * MLA_V, :], ckv_t), k_pair)

    ang_t = invfc_ref[...] * posr_ref[0].astype(F32)
    cos_t, sin_t = jnp.cos(ang_t), jnp.sin(ang_t)
    g_nope = jnp.broadcast_to(gqn_ref[...], (MLA_NOPE, tm)) * qscale
    g_r1 = jnp.broadcast_to(gqr_ref[:half, :], (half, tm)) * qscale
    g_r2 = jnp.broadcast_to(gqr_ref[half:, :], (half, tm)) * qscale
    ahead = 2
    pending = [project(h) for h in range(ahead)]
    for h in range(MLA_HEADS):
        if h + ahead < MLA_HEADS:
            pending.append(project(h + ahead))
        q, v_t, _ = pending[h]
        k_n = pending[h - h % 2][2][:, (h % 2) * MLA_NOPE:(h % 2 + 1) * MLA_NOPE]
        nope = q[:MLA_NOPE]
        r = lax.rsqrt(jnp.mean(nope * nope, axis=0, keepdims=True) + EPS)
        qt_ref[0, h, :MLA_NOPE, :] = (nope * r * g_nope).astype(BF16)
        x1, x2 = q[MLA_NOPE:MLA_NOPE + half], q[MLA_NOPE + half:MLA_NOPE + MLA_ROPE]
        ss = jnp.sum(x1 * x1, axis=0, keepdims=True) + jnp.sum(x2 * x2, axis=0, keepdims=True)
        r = lax.rsqrt(ss * (1.0 / MLA_ROPE) + EPS)
        x1, x2 = x1 * r * g_r1, x2 * r * g_r2
        qt_ref[0, h, MLA_NOPE:MLA_NOPE + half, :] = (x1 * cos_t - x2 * sin_t).astype(BF16)
        qt_ref[0, h, MLA_NOPE + half:MLA_NOPE + MLA_ROPE, :] = (x1 * sin_t + x2 * cos_t).astype(BF16)
        qt_ref[0, h, MLA_NOPE + MLA_ROPE:, :] = jnp.zeros((MLA_QK_PAD - MLA_NOPE - MLA_ROPE, tm), BF16)
        kn_ref[0, h] = _rms(k_n, gkn_ref[...]).astype(BF16)
        vt_ref[0, h, :MLA_V, :] = v_t.astype(BF16)
        vt_ref[0, h, MLA_V:, :] = jnp.ones((MLA_VL - MLA_V, tm), BF16)

    kr_t = kr_ref[0].astype(F32).T
    x1, x2 = kr_t[:half], kr_t[half:MLA_ROPE]
    ss = jnp.sum(x1 * x1, axis=0, keepdims=True) + jnp.sum(x2 * x2, axis=0, keepdims=True)
    r = lax.rsqrt(ss * (1.0 / MLA_ROPE) + EPS)
    x1, x2 = x1 * r * gkr_ref[:half, :], x2 * r * gkr_ref[half:, :]
    rot = jnp.concatenate([x1 * cos_t - x2 * sin_t, x1 * sin_t + x2 * cos_t,
                           jnp.zeros((LANES - MLA_ROPE, tm), F32)], axis=0)
    krope_ref[0] = rot.T.astype(BF16)


def _mla_qkv(proj, cq_blk, ckv_blk, kr_blk, positions, gains, w_uq, w_ukv, layer, *, tm):
    b, l, _ = proj.shape
    hh = MLA_HEADS
    half = MLA_ROPE // 2
    inv_freq = ROPE_THETA ** (-jnp.arange(half, dtype=F32) / half)
    const = lambda a: pl.BlockSpec(a.shape, lambda i, j: (0,) * a.ndim)
    gq, gkv, gqn, gkn, gqr, gkr = gains
    consts = [inv_freq.reshape(half, 1), gq.reshape(1, -1), gkv.reshape(1, -1), gqn.reshape(-1, 1),
              gkn.reshape(1, -1), gqr.reshape(-1, 1), gkr.reshape(-1, 1)]
    weights = [w_uq, w_ukv]
    return pl.pallas_call(
        _mla_qkv_kernel,
        grid=(b, l // tm),
        in_specs=[pl.BlockSpec((1, tm, MLA_Q_LORA), lambda i, j: (i, j, cq_blk)),
                  pl.BlockSpec((1, tm, MLA_KV_LORA), lambda i, j: (i, j, ckv_blk)),
                  pl.BlockSpec((1, tm, LANES), lambda i, j: (i, j, kr_blk)),
                  pl.BlockSpec((1, 1, tm), lambda i, j: (i, 0, j))] + [const(a) for a in consts]
                 + [pl.BlockSpec(*_layer_block(w, layer)) for w in weights],
        out_specs=[pl.BlockSpec((1, hh, MLA_QK_PAD, tm), lambda i, j: (i, 0, 0, j)),
                   pl.BlockSpec((1, hh, tm, MLA_NOPE), lambda i, j: (i, 0, j, 0)),
                   pl.BlockSpec((1, tm, LANES), lambda i, j: (i, j, 0)),
                   pl.BlockSpec((1, hh, MLA_VL, tm), lambda i, j: (i, 0, 0, j))],
        out_shape=[jax.ShapeDtypeStruct((b, hh, MLA_QK_PAD, l), BF16),
                   jax.ShapeDtypeStruct((b, hh, l, MLA_NOPE), BF16),
                   jax.ShapeDtypeStruct((b, l, LANES), BF16),
                   jax.ShapeDtypeStruct((b, hh, MLA_VL, l), BF16)],
        scratch_shapes=[pltpu.VMEM((hh * (MLA_NOPE + MLA_ROPE), MLA_Q_LORA), BF16),
                        pltpu.VMEM((MLA_KV_LORA, hh * MLA_NOPE), BF16),
                        pltpu.VMEM((hh * MLA_V, MLA_KV_LORA), BF16)],
        compiler_params=_cparams(("arbitrary", "arbitrary")),
        name="mla_qkv",
    )(proj, proj, proj, positions.reshape(b, 1, l), *consts, *weights)


def _flash_kernel(qt_ref, kn_ref, kr_ref, vt_ref, o_ref, m_ref, acc_ref, *, tq, hp, ahead):
    qi = pl.program_id(2)
    m_ref[...] = jnp.full(m_ref.shape, -jnp.inf, F32)
    acc_ref[...] = jnp.zeros(acc_ref.shape, F32)

    half = tq // 2
    lower = (lax.broadcasted_iota(jnp.int32, (half, half), 0)
             <= lax.broadcasted_iota(jnp.int32, (half, half), 1))

    def blocks(j, parts, diagonal):
        base = pl.multiple_of(j * tq, tq)
        items = [(h, pl.ds(base + k0, nk), slice(q0, q0 + nq)) for k0, nk, q0, nq in parts for h in range(hp)]

        def scores(h, rows, cols):
            k = jnp.concatenate([kn_ref[0, h, rows, :], kr_ref[0, rows, :]], axis=-1)
            return jnp.dot(k, qt_ref[0, h, :, cols], preferred_element_type=F32)

        pending = [scores(*it) for it in items[:ahead]]
        for n, (h, rows, cols) in enumerate(items):
            if n + ahead < len(items):
                pending.append(scores(*items[n + ahead]))
            s = pending[n]
            if diagonal:
                square = jnp.where(lower, s[:, :half], jnp.finfo(F32).min)
                s = square if s.shape[1] == half else jnp.concatenate([square, s[:, half:]], axis=1)
            m = m_ref[h, :, cols]
            m_new = jnp.maximum(m, jnp.max(s, axis=0, keepdims=True))
            alpha = jnp.exp2(m - m_new)
            p = jnp.exp2(s - m_new)
            acc_ref[h, :, cols] = alpha * acc_ref[h, :, cols] + jnp.dot(
                vt_ref[0, h, :, rows], p.astype(BF16), preferred_element_type=F32)
            m_ref[h, :, cols] = m_new

    def body(j, carry):
        blocks(j, [(0, tq, 0, tq)], False)
        return carry

    lax.fori_loop(0, qi, body, 0)
    blocks(qi, [(0, half, 0, tq), (half, half, half, half)], True)
    for h in range(hp):
        o_ref[0, :, h * MLA_V:(h + 1) * MLA_V] = (
            acc_ref[h, :MLA_V, :] / acc_ref[h, MLA_V:MLA_V + 1, :]).T.astype(o_ref.dtype)


def _flash(qt, kn, kr, vt, *, tq, hp, ahead):
    b, hh, _, l = qt.shape
    return pl.pallas_call(
        functools.partial(_flash_kernel, tq=tq, hp=hp, ahead=ahead),
        grid=(b, hh // hp, l // tq),
        in_specs=[pl.BlockSpec((1, hp, MLA_QK_PAD, tq), lambda i, h, j: (i, h, 0, j)),
                  pl.BlockSpec((1, hp, l, MLA_NOPE), lambda i, h, j: (i, h, 0, 0)),
                  pl.BlockSpec((1, l, LANES), lambda i, h, j: (i, 0, 0)),
                  pl.BlockSpec((1, hp, MLA_VL, l), lambda i, h, j: (i, h, 0, 0))],
        out_specs=pl.BlockSpec((1, tq, hp * MLA_V), lambda i, h, j: (i, j, h)),
        out_shape=jax.ShapeDtypeStruct((b, l, hh * MLA_V), BF16),
        scratch_shapes=[pltpu.VMEM((hp, 1, tq), F32), pltpu.VMEM((hp, MLA_VL, tq), F32)],
        compiler_params=_cparams(("parallel", "parallel", "parallel")),
        name="flash",
    )(qt, kn, kr, vt)


def _s5_layer(x, ln, w_in, lam_re, lam_im, log_step, b_re, b_im, c_re, c_im, d, w_glu,
              w_out, mem_kv, xq_norm, layer, j):
    b, l, dm = x.shape
    tm = 512
    proj, xn = _s5_in_proj(x, ln, w_in, j, tm=tm, col_chunk=512)
    xg = _s5_ut(xn, w_in, j)
    toep, wout, wst, dcol = _s5_asm(lam_re, lam_im, log_step, c_re, c_im, b_re, b_im, d)
    pw_re, pw_im = _s5_pow(lam_re, lam_im, log_step)
    col = lambda pw: pw.transpose(1, 2, 0)
    yt = _s5_mix(xg, toep, wst, wout, col(pw_re), col(pw_im), dcol, nb=b, gs=8)
    y = _glu(yt, w_glu, j, nb=b, col_chunk=256)
    return _merge(x, y, proj, BRANCH_WIDTH // XQ_WIDTH, 0, *mem_kv, xq_norm, w_out, layer, tm=tm, phased=True)


def _mla_layer(x, positions, ln, w_in, q_lora_norm, kv_lora_norm, w_uq, w_ukv, q_nope_norm, k_nope_norm,
               q_rope_norm, k_rope_norm, w_out, mem_kv, xq_norm, layer, j):
    b, l, dm = x.shape
    o1 = MLA_Q_LORA
    o2 = o1 + MLA_KV_LORA
    o3 = o2 + MLA_ROPE
    o4 = o3 + XQ_WIDTH
    segments = ((o4, BRANCH_WIDTH), (0, o1), (o3, XQ_WIDTH), (o1, MLA_KV_LORA), (o2, MLA_ROPE))
    wout = -(-(o4 + BRANCH_WIDTH) // 512) * 512
    proj = _mla_in_proj(x.reshape(b * l, dm), ln, w_in, j, segments, wout, tm=512, col_chunk=512)
    proj = proj.reshape(b, l, -1)
    gate_blk = 0
    cq_blk = BRANCH_WIDTH // MLA_Q_LORA
    xq_blk = (BRANCH_WIDTH + MLA_Q_LORA) // XQ_WIDTH
    ckv_blk = (BRANCH_WIDTH + MLA_Q_LORA + XQ_WIDTH) // MLA_KV_LORA
    kr_blk = (BRANCH_WIDTH + MLA_Q_LORA + XQ_WIDTH + MLA_KV_LORA) // LANES
    qt, kn, kr, vt = _mla_qkv(proj, cq_blk, ckv_blk, kr_blk, positions,
                              (q_lora_norm, kv_lora_norm, q_nope_norm, k_nope_norm, q_rope_norm, k_rope_norm),
                              w_uq, w_ukv, j, tm=512)
    attn = _flash(qt, kn, kr, vt, tq=512, hp=12, ahead=2)
    return _merge(x, attn, proj, xq_blk, gate_blk, *mem_kv, xq_norm, w_out, layer, tm=512)


def kernel(x, mem, positions, ln_gain, w_out, mem_norm, w_mem_kv, xq_norm, xk_norm,
           s5_w_in, s5_lambda_re, s5_lambda_im, s5_log_step, s5_b_re, s5_b_im, s5_c_re, s5_c_im,
           s5_d, s5_w_glu, mla_w_in, mla_q_lora_norm, mla_kv_lora_norm, mla_w_uq, mla_w_ukv,
           mla_q_nope_norm, mla_k_nope_norm, mla_q_rope_norm, mla_k_rope_norm):
    depth = ln_gain.shape[0]
    mem_kv = _mem_kv(mem, mem_norm, w_mem_kv, xk_norm)
    for i in range(depth):
        j = i // 2
        if i % 2 == 0:
            x = _s5_layer(x, ln_gain[i], s5_w_in, s5_lambda_re[j], s5_lambda_im[j], s5_log_step[j],
                          s5_b_re[j], s5_b_im[j], s5_c_re[j], s5_c_im[j], s5_d[j], s5_w_glu,
                          w_out, mem_kv, xq_norm, i, j)
        else:
            x = _mla_layer(x, positions, ln_gain[i], mla_w_in, mla_q_lora_norm[j], mla_kv_lora_norm[j],
                           mla_w_uq, mla_w_ukv, mla_q_nope_norm[j], mla_k_nope_norm[j],
                           mla_q_rope_norm[j], mla_k_rope_norm[j],
                           w_out, mem_kv, xq_norm, i, j)
    return x
```

```python
import functools
import math

import jax
import jax.numpy as jnp
from jax import lax
from jax.experimental import pallas as pl
from jax.experimental.pallas import tpu as pltpu

D_MODEL = 1024
BRANCH_WIDTH = 2 * D_MODEL
XQ_WIDTH = BRANCH_WIDTH // 4
PRIMARY_WIDTH = BRANCH_WIDTH - XQ_WIDTH
X_HEADS = 4
X_HEAD_DIM = XQ_WIDTH // X_HEADS
S5_GROUP_CH = 16
S5_GROUPS = PRIMARY_WIDTH // S5_GROUP_CH
S5_STATE = 64
MLA_NOPE = 128
MLA_ROPE = 64
MLA_V = 128
MLA_HEADS = PRIMARY_WIDTH // MLA_V
MLA_Q_LORA = D_MODEL // 2
MLA_KV_LORA = D_MODEL // 4
ROPE_THETA = 10000.0
EPS = 1e-6

LANES = 128
MLA_QK_PAD = 2 * LANES
F32_SUBLANES = 8
BF16_SUBLANES = 16
MLA_VL = MLA_V + BF16_SUBLANES
S5_CHUNK = 2 * LANES // S5_GROUP_CH
S5_SCAN_EXPONENTS = [S5_CHUNK * 2 ** i for i in range(int(math.log2(LANES)))]
VMEM_LIMIT = 56 * 1024 * 1024

F32 = jnp.float32
BF16 = jnp.bfloat16


def _cparams(sem):
    return pltpu.CompilerParams(dimension_semantics=sem, vmem_limit_bytes=VMEM_LIMIT)


def _rms(x, g):
    return x * lax.rsqrt(jnp.mean(x * x, axis=-1, keepdims=True) + EPS) * g


def _layer_block(w, j, block=None, index=None):
    block = tuple(w.shape[1:]) if block is None else block
    index = (0,) * len(block) if index is None else index
    return (None,) + block, lambda *_: (j,) + index


def _mla_in_proj_kernel(x_ref, g_ref, w_ref, o_ref, wp_ref, *, segments, col_chunk):
    @pl.when(pl.program_id(0) == 0)
    def _():
        at = 0
        for start, width in segments:
            wp_ref[:, at:at + width] = w_ref[:, start:start + width].astype(BF16)
            at += width
        wp_ref[:, at:] = jnp.zeros((wp_ref.shape[0], wp_ref.shape[1] - at), BF16)

    xn = _rms(x_ref[...], g_ref[...]).astype(BF16)
    for c in range(o_ref.shape[1] // col_chunk):
        sl = slice(c * col_chunk, (c + 1) * col_chunk)
        o_ref[:, sl] = jnp.dot(xn, wp_ref[:, sl], preferred_element_type=F32).astype(o_ref.dtype)


def _mla_in_proj(x, g, w, j, segments, wout, *, tm, col_chunk):
    n, d = x.shape
    return pl.pallas_call(
        functools.partial(_mla_in_proj_kernel, segments=segments, col_chunk=col_chunk),
        grid=(n // tm,),
        in_specs=[pl.BlockSpec((tm, d), lambda i: (i, 0)),
                  pl.BlockSpec((1, d), lambda i: (0, 0)),
                  pl.BlockSpec(*_layer_block(w, j), pipeline_mode=pl.Buffered(1))],
        out_specs=pl.BlockSpec((tm, wout), lambda i: (i, 0)),
        out_shape=jax.ShapeDtypeStruct((n, wout), BF16),
        scratch_shapes=[pltpu.VMEM((d, wout), BF16)],
        compiler_params=_cparams(("arbitrary",)),
        name="mla_in_proj",
    )(x, g.reshape(1, d), w)


def _to_phase_order(a):
    n, d = a.shape
    return jnp.swapaxes(a.reshape(n // S5_CHUNK, S5_CHUNK, d), 0, 1).reshape(n, d)


def _from_phase_order(a):
    n, d = a.shape
    return jnp.swapaxes(a.reshape(S5_CHUNK, n // S5_CHUNK, d), 0, 1).reshape(n, d)


def _s5_in_proj_kernel(x_ref, g_ref, wg_ref, wx_ref, o_ref, xn_ref, *, tm, col_chunk):
    d = x_ref.shape[2]
    nc = tm // S5_CHUNK
    chunks = [slice(c * col_chunk, (c + 1) * col_chunk) for c in range(BRANCH_WIDTH // col_chunk)]
    weights = [wg_ref[:, sl].astype(BF16) for sl in chunks] + [wx_ref[...].astype(BF16)]
    chunks.append(slice(BRANCH_WIDTH, BRANCH_WIDTH + XQ_WIDTH))

    def normalise(k):
        xn = _rms(_to_phase_order(x_ref[0, k * tm:(k + 1) * tm, :]), g_ref[...]).astype(BF16)
        for s in range(S5_CHUNK):
            xn_ref[0, k * nc:(k + 1) * nc, s * d:(s + 1) * d] = xn[s * nc:(s + 1) * nc]
        return xn

    def project(k, xn):
        for sl, w in zip(chunks, weights):
            o_ref[0, k * tm:(k + 1) * tm, sl] = jnp.dot(xn, w, preferred_element_type=F32).astype(o_ref.dtype)

    n_sub = x_ref.shape[1] // tm
    prev = None
    for k in range(n_sub):
        xn = normalise(k)
        if prev is not None:
            project(k - 1, prev)
        prev = xn
    project(n_sub - 1, prev)


def _s5_in_proj(x, g, w_in, j, *, tm, col_chunk):
    b, l, d = x.shape
    wout = BRANCH_WIDTH + XQ_WIDTH
    tg = 2 * tm
    nc = tg // S5_CHUNK
    return pl.pallas_call(
        functools.partial(_s5_in_proj_kernel, tm=tm, col_chunk=col_chunk),
        grid=(b, l // tg),
        in_specs=[pl.BlockSpec((1, tg, d), lambda i, j: (i, j, 0)),
                  pl.BlockSpec((1, d), lambda i, j: (0, 0)),
                  pl.BlockSpec(*_layer_block(w_in, j, (d, BRANCH_WIDTH),
                                             (0, (PRIMARY_WIDTH + XQ_WIDTH) // BRANCH_WIDTH)),
                               pipeline_mode=pl.Buffered(1)),
                  pl.BlockSpec(*_layer_block(w_in, j, (d, XQ_WIDTH), (0, PRIMARY_WIDTH // XQ_WIDTH)),
                               pipeline_mode=pl.Buffered(1))],
        out_specs=[pl.BlockSpec((1, tg, wout), lambda i, j: (i, j, 0)),
                   pl.BlockSpec((1, nc, S5_CHUNK * d), lambda i, j: (i, j, 0))],
        out_shape=[jax.ShapeDtypeStruct((b, l, wout), BF16),
                   jax.ShapeDtypeStruct((b, l // S5_CHUNK, S5_CHUNK * d), BF16)],
        compiler_params=_cparams(("parallel", "parallel")),
        name="s5_in_proj",
    )(x, g.reshape(1, d), w_in, w_in)


def _mem_kv_kernel(m_ref, g_ref, w_ref, kg_ref, k_ref, v_ref):
    b, m, d = m_ref.shape
    w = w_ref[0].astype(BF16)
    for i in range(b):
        mn = _rms(m_ref[i], g_ref[0]).astype(BF16)
        kv = jnp.dot(mn, w, preferred_element_type=F32)
        for h in range(X_HEADS):
            sl = slice(h * X_HEAD_DIM, (h + 1) * X_HEAD_DIM)
            k_ref[0, i, :, sl] = _rms(kv[:, sl], kg_ref[0]).astype(BF16)
        v_ref[0, i] = kv[:, XQ_WIDTH:].astype(BF16)


def _mem_kv(mem, mem_norm, w_mem_kv, xk_norm):
    b, m, d = mem.shape
    depth = w_mem_kv.shape[0]
    out = jax.ShapeDtypeStruct((depth, b, m, XQ_WIDTH), BF16)
    return pl.pallas_call(
        _mem_kv_kernel,
        grid=(depth,),
        in_specs=[pl.BlockSpec((b, m, d), lambda n: (0, 0, 0)),
                  pl.BlockSpec((1, 1, d), lambda n: (n, 0, 0)),
                  pl.BlockSpec((1, d, 2 * XQ_WIDTH), lambda n: (n, 0, 0)),
                  pl.BlockSpec((1, 1, X_HEAD_DIM), lambda n: (n, 0, 0))],
        out_specs=[pl.BlockSpec((1, b, m, XQ_WIDTH), lambda n: (n, 0, 0, 0)),
                   pl.BlockSpec((1, b, m, XQ_WIDTH), lambda n: (n, 0, 0, 0))],
        out_shape=[out, out],
        compiler_params=_cparams(("parallel",)),
        name="mem_kv",
    )(mem, mem_norm.reshape(depth, 1, d), w_mem_kv, xk_norm.reshape(depth, 1, X_HEAD_DIM))


def _s5_pow_kernel(lr_ref, li_ref, ls_ref, pr_ref, pi_ref):
    step = jnp.exp(ls_ref[...])
    zr, zi = lr_ref[...] * step, li_ref[...] * step
    for n in range(pr_ref.shape[0]):
        if n < len(S5_SCAN_EXPONENTS):
            e = S5_SCAN_EXPONENTS[n]
            mag = jnp.exp(zr * e)
            pr_ref[n] = mag * jnp.cos(zi * e)
            pi_ref[n] = mag * jnp.sin(zi * e)
        else:
            pr_ref[n] = jnp.zeros_like(zr)
            pi_ref[n] = jnp.zeros_like(zr)


def _s5_pow(lam_re, lam_im, log_step):
    g, p = lam_re.shape
    slots = -(-len(S5_SCAN_EXPONENTS) // F32_SUBLANES) * F32_SUBLANES
    out = jax.ShapeDtypeStruct((slots, g, p), F32)
    return pl.pallas_call(_s5_pow_kernel, out_shape=[out, out], name="s5_pow")(
        lam_re, lam_im, log_step.reshape(g, 1))


def _s5_asm_kernel(lr_ref, li_ref, ls_ref, cr_ref, ci_ref, btr_ref, bti_ref, br_ref, bi_ref, d_ref,
                   toep_ref, wout_ref, wst_ref, dcol_ref):
    def group(i, carry):
        _s5_asm_group(i, lr_ref, li_ref, ls_ref, cr_ref, ci_ref, btr_ref, bti_ref, br_ref, bi_ref,
                      toep_ref, wout_ref, wst_ref)
        dcol_ref[i] = jnp.concatenate([d_ref[i]] * S5_CHUNK, axis=0)
        return carry

    lax.fori_loop(0, cr_ref.shape[0], group, 0)


def _dot_3pass_tiled(a, b):
    c = b.shape[1]
    rep = ((lax.broadcasted_iota(jnp.int32, (c, S5_CHUNK * c), 1) & (c - 1))
           == lax.broadcasted_iota(jnp.int32, (c, S5_CHUNK * c), 0)).astype(BF16)
    dot = functools.partial(jnp.dot, preferred_element_type=F32)
    a_hi, b_hi = a.astype(BF16), b.astype(BF16)
    a_lo = (a - a_hi.astype(F32)).astype(BF16)
    b_lo = (b - b_hi.astype(F32)).astype(BF16)
    b_hi, b_lo = dot(b_hi, rep).astype(BF16), dot(b_lo, rep).astype(BF16)
    return dot(a_hi, b_hi) + (dot(a_hi, b_lo) + dot(a_lo, b_hi))


def _s5_asm_group(i, lr_ref, li_ref, ls_ref, cr_ref, ci_ref, btr_ref, bti_ref, br_ref, bi_ref,
                  toep_ref, wout_ref, wst_ref):
    t = S5_CHUNK
    lr, li = lr_ref[i], li_ref[i]
    step = jnp.exp(ls_ref[i])
    zr, zi = lr * step, li * step
    rows = -(-(t + 1) // F32_SUBLANES) * F32_SUBLANES
    e = lax.broadcasted_iota(jnp.int32, (rows, lr.shape[1]), 0).astype(F32)
    mag = jnp.exp(e * zr)
    pr, pi = mag * jnp.cos(e * zi), mag * jnp.sin(e * zi)
    den = lr * lr + li * li
    mr = ((pr[1:2] - 1.0) * lr + pi[1:2] * li) / den
    mi = (pi[1:2] * lr - (pr[1:2] - 1.0) * li) / den
    cr, ci = cr_ref[i], ci_ref[i]
    btr, bti = btr_ref[i], bti_ref[i]
    amr = pr[:t] * mr - pi[:t] * mi
    ami = pr[:t] * mi + pi[:t] * mr
    l_re, l_im, w_re, w_im, o_re, o_im = [], [], [], [], [], []
    for k in range(t):
        ar, ai = amr[k:k + 1], ami[k:k + 1]
        l_re.append(cr * ar - ci * ai)
        l_im.append(-(cr * ai + ci * ar))
        ar, ai = amr[t - 1 - k:t - k], ami[t - 1 - k:t - k]
        w_re.append(btr * ar - bti * ai)
        w_im.append(btr * ai + bti * ar)
        ar, ai = pr[k + 1:k + 2], pi[k + 1:k + 2]
        o_re.append(cr * ar - ci * ai)
        o_im.append(-(cr * ai + ci * ar))
    cat = lambda parts: jnp.concatenate(parts, axis=0)
    kt = _dot_3pass_tiled(jnp.concatenate([cat(l_re), cat(l_im)], axis=1),
                          jnp.concatenate([br_ref[i], bi_ref[i]], axis=0))
    n = kt.shape[0]
    blk = lax.shift_right_logical(lax.broadcasted_iota(jnp.int32, kt.shape, 1), int(math.log2(S5_GROUP_CH)))
    toep = jnp.where(blk == 0, kt, 0.0)
    for s in range(1, t):
        shifted = jnp.concatenate([jnp.zeros((s * S5_GROUP_CH, n), F32), kt[:n - s * S5_GROUP_CH]], axis=0)
        toep = jnp.where(blk == s, shifted, toep)
    toep_ref[i] = toep.astype(BF16)
    wout_ref[i] = jnp.concatenate([cat(o_re), cat(o_im)], axis=1).astype(BF16)
    wst_ref[i] = jnp.concatenate([cat(w_re), cat(w_im)], axis=1).T.astype(BF16)


def _s5_asm(lam_re, lam_im, log_step, c_re, c_im, b_re, b_im, d):
    g, c, p = c_re.shape
    tc = S5_CHUNK * c
    gs = 8
    blk = lambda a: pl.BlockSpec((gs,) + a.shape[1:], lambda i: (i, 0, 0))
    bt_re, bt_im = b_re.transpose(0, 2, 1), b_im.transpose(0, 2, 1)
    args = (lam_re.reshape(g, 1, p), lam_im.reshape(g, 1, p), log_step.reshape(g, 1, 1),
            c_re, c_im, bt_re, bt_im, b_re, b_im, d.reshape(g, c, 1))
    return pl.pallas_call(
        _s5_asm_kernel,
        grid=(g // gs,),
        in_specs=[blk(a) for a in args],
        out_specs=[pl.BlockSpec((gs, tc, tc), lambda i: (i, 0, 0)),
                   pl.BlockSpec((gs, tc, 2 * p), lambda i: (i, 0, 0)),
                   pl.BlockSpec((gs, 2 * p, tc), lambda i: (i, 0, 0)),
                   pl.BlockSpec((gs, tc, 1), lambda i: (i, 0, 0))],
        out_shape=[jax.ShapeDtypeStruct((g, tc, tc), BF16), jax.ShapeDtypeStruct((g, tc, 2 * p), BF16),
                   jax.ShapeDtypeStruct((g, 2 * p, tc), BF16), jax.ShapeDtypeStruct((g, tc, 1), F32)],
        compiler_params=_cparams(("parallel",)),
        name="s5_asm",
    )(*args)


def _s5_ut_kernel(xn_ref, w_ref, o_ref, wt_ref):
    @pl.when(pl.program_id(0) == 0)
    def _():
        wt_ref[...] = w_ref[...].T.astype(BF16)

    nb, nc, d = xn_ref.shape
    ut = lax.dot_general(wt_ref[...], xn_ref[...].reshape(nb * nc, d), (((1,), (1,)), ((), ())),
                         preferred_element_type=F32)
    o_ref[...] = ut.astype(BF16).reshape(o_ref.shape)


def _s5_ut(xn, w_in, j):
    b, nc, td = xn.shape
    d = td // S5_CHUNK
    return pl.pallas_call(
        _s5_ut_kernel,
        grid=(S5_CHUNK,),
        in_specs=[pl.BlockSpec((b, nc, d), lambda s: (0, 0, s)),
                  pl.BlockSpec(*_layer_block(w_in, j, (d, PRIMARY_WIDTH)))],
        out_specs=pl.BlockSpec((S5_GROUPS, S5_GROUP_CH, b * nc), lambda s: (0, s, 0)),
        out_shape=jax.ShapeDtypeStruct((S5_GROUPS, S5_CHUNK * S5_GROUP_CH, b * nc), BF16),
        scratch_shapes=[pltpu.VMEM((PRIMARY_WIDTH, d), BF16)],
        compiler_params=_cparams(("arbitrary",)),
        name="s5_ut",
    )(xn, w_in)


def _s5_mix_kernel(x_ref, toep_ref, wst_ref, wout_ref, sr_ref, si_ref, d_ref, o_ref, *, nb):
    p = S5_STATE
    gs = x_ref.shape[0]
    lane = lax.broadcasted_iota(jnp.int32, (p, LANES), 1)
    n_steps = int(math.log2(LANES))

    def scan(g, hloc):
        pw = []
        for i in range(n_steps):
            keep = lane >= (1 << i)
            pw.append((jnp.where(keep, jnp.broadcast_to(sr_ref[g, :, i:i + 1], (p, LANES)), 0.0),
                       jnp.where(keep, jnp.broadcast_to(si_ref[g, :, i:i + 1], (p, LANES)), 0.0)))
        h_re = [hloc[:p, b * LANES:(b + 1) * LANES] for b in range(nb)]
        h_im = [hloc[p:, b * LANES:(b + 1) * LANES] for b in range(nb)]
        for i in range(n_steps):
            ar, ai = pw[i]
            r_sh = [pltpu.roll(v, 1 << i, 1) for v in h_re]
            i_sh = [pltpu.roll(v, 1 << i, 1) for v in h_im]
            h_re = [h_re[b] + ar * r_sh[b] - ai * i_sh[b] for b in range(nb)]
            h_im = [h_im[b] + ar * i_sh[b] + ai * r_sh[b] for b in range(nb)]
        h_re = [jnp.where(lane >= 1, pltpu.roll(v, 1, 1), 0.0) for v in h_re]
        h_im = [jnp.where(lane >= 1, pltpu.roll(v, 1, 1), 0.0) for v in h_im]
        return jnp.concatenate([jnp.concatenate(h_re, axis=1), jnp.concatenate(h_im, axis=1)],
                               axis=0).astype(BF16)

    def outputs(g, h):
        x = x_ref[g]
        y = (jnp.dot(toep_ref[g], x, preferred_element_type=F32)
             + jnp.dot(wout_ref[g], h, preferred_element_type=F32)
             + d_ref[g] * x.astype(F32))
        o_ref[:, g * S5_GROUP_CH:(g + 1) * S5_GROUP_CH, :] = (
            jax.nn.gelu(y).astype(o_ref.dtype).reshape(S5_CHUNK, S5_GROUP_CH, y.shape[1]))

    h_prev = None
    for g in range(gs):
        hloc = jnp.dot(wst_ref[g], x_ref[g], preferred_element_type=F32)
        h = scan(g, hloc)
        if h_prev is not None:
            outputs(g - 1, h_prev)
        h_prev = h
    outputs(gs - 1, h_prev)


def _s5_mix(xg, toep, wst, wout, sc_re, sc_im, dcol, *, nb, gs):
    g, tc, cols = xg.shape
    assert cols == nb * LANES, "one batch's chunks must fill exactly one 128-lane block"
    blk = lambda a: pl.BlockSpec((gs,) + a.shape[1:], lambda i: (i, 0, 0))
    return pl.pallas_call(
        functools.partial(_s5_mix_kernel, nb=nb),
        grid=(g // gs,),
        in_specs=[blk(a) for a in (xg, toep, wst, wout, sc_re, sc_im, dcol)],
        out_specs=pl.BlockSpec((S5_CHUNK, gs * S5_GROUP_CH, cols), lambda i: (0, i, 0)),
        out_shape=jax.ShapeDtypeStruct((S5_CHUNK, g * S5_GROUP_CH, cols), BF16),
        compiler_params=_cparams(("parallel",)),
        name="s5_mix",
    )(xg, toep, wst, wout, sc_re, sc_im, dcol)


def _glu_kernel(y_ref, w_ref, o_ref, *, col_chunk):
    y = y_ref[0].T
    half = o_ref.shape[-1]
    for c in range(half // col_chunk):
        wa = w_ref[:, c * col_chunk:(c + 1) * col_chunk].astype(BF16)
        wg = w_ref[:, half + c * col_chunk:half + (c + 1) * col_chunk].astype(BF16)
        a = jnp.dot(y, wa, preferred_element_type=F32)
        g = jnp.dot(y, wg, preferred_element_type=F32)
        o_ref[:, :, c * col_chunk:(c + 1) * col_chunk] = (
            (a * jax.nn.sigmoid(g)).astype(o_ref.dtype).reshape(o_ref.shape[:2] + (col_chunk,)))


def _glu(yt, w, layer, *, nb, col_chunk):
    t, k, cols = yt.shape
    nc = cols // nb
    half = w.shape[2] // 2
    return pl.pallas_call(
        functools.partial(_glu_kernel, col_chunk=col_chunk),
        grid=(t,),
        in_specs=[pl.BlockSpec((1, k, cols), lambda j: (j, 0, 0)),
                  pl.BlockSpec(*_layer_block(w, layer), pipeline_mode=pl.Buffered(1))],
        out_specs=pl.BlockSpec((nb, nc, half), lambda j: (0, 0, j)),
        out_shape=jax.ShapeDtypeStruct((nb, nc, t * half), BF16),
        compiler_params=_cparams(("parallel",)),
        name="glu",
    )(yt, w)


def _merge_kernel(x_ref, mix_ref, xq_ref, gate_ref, k_ref, v_ref, qg_ref, w_ref, o_ref, *cat_refs, tm, phased):
    scale = X_HEAD_DIM ** -0.5
    nc = tm // S5_CHUNK
    w = w_ref[...].astype(BF16)

    def gather(k, cat_ref):
        r0 = k * tm
        gate = gate_ref[0, r0:r0 + tm, :]
        sg = gate * jax.nn.sigmoid(gate)
        if phased:
            for s in range(S5_CHUNK):
                rows = slice(s * nc, (s + 1) * nc)
                mix = mix_ref[0, k * nc:(k + 1) * nc, s * PRIMARY_WIDTH:(s + 1) * PRIMARY_WIDTH]
                cat_ref[rows, :PRIMARY_WIDTH] = mix * sg[rows, :PRIMARY_WIDTH]
        else:
            cat_ref[:, :PRIMARY_WIDTH] = mix_ref[0, r0:r0 + tm, :] * sg[:, :PRIMARY_WIDTH]
        for h in range(X_HEADS):
            sl = slice(h * X_HEAD_DIM, (h + 1) * X_HEAD_DIM)
            q = _rms(xq_ref[0, r0:r0 + tm, sl].astype(F32), qg_ref[...]).astype(BF16)
            s = lax.dot_general(q, k_ref[0, :, sl], (((1,), (1,)), ((), ())), preferred_element_type=F32) * scale
            p = jnp.exp(s - jnp.max(s, axis=-1, keepdims=True))
            p = (p / jnp.sum(p, axis=-1, keepdims=True)).astype(BF16)
            mo = jnp.dot(p, v_ref[0, :, sl], preferred_element_type=F32)
            osl = slice(PRIMARY_WIDTH + h * X_HEAD_DIM, PRIMARY_WIDTH + (h + 1) * X_HEAD_DIM)
            cat_ref[:, osl] = mo.astype(BF16) * sg[:, osl]

    def project(k, cat_ref):
        r0 = k * tm
        delta = jnp.dot(cat_ref[...], w, preferred_element_type=F32)
        o_ref[0, r0:r0 + tm, :] = x_ref[0, r0:r0 + tm, :] + (_from_phase_order(delta) if phased else delta)

    n_sub = len(cat_refs)
    for k in range(n_sub):
        gather(k, cat_refs[k])
        if k > 0:
            project(k - 1, cat_refs[k - 1])
    project(n_sub - 1, cat_refs[n_sub - 1])


def _merge(x, mix, proj, xq_blk, gate_blk, mk, mv, xq_norm, w_out, layer, *, tm, phased=False):
    b, l, d = x.shape
    m = mk.shape[2]
    n_sub = 2
    tg = n_sub * tm
    mix_spec = (pl.BlockSpec((1, tg // S5_CHUNK, S5_CHUNK * PRIMARY_WIDTH), lambda i, j: (i, j, 0)) if phased
                else pl.BlockSpec((1, tg, PRIMARY_WIDTH), lambda i, j: (i, j, 0)))
    return pl.pallas_call(
        functools.partial(_merge_kernel, tm=tm, phased=phased),
        grid=(b, l // tg),
        in_specs=[pl.BlockSpec((1, tg, d), lambda i, j: (i, j, 0)),
                  mix_spec,
                  pl.BlockSpec((1, tg, XQ_WIDTH), lambda i, j: (i, j, xq_blk)),
                  pl.BlockSpec((1, tg, BRANCH_WIDTH), lambda i, j: (i, j, gate_blk)),
                  pl.BlockSpec((None, 1, m, XQ_WIDTH), lambda i, j: (layer, i, 0, 0)),
                  pl.BlockSpec((None, 1, m, XQ_WIDTH), lambda i, j: (layer, i, 0, 0)),
                  pl.BlockSpec((None, 1, X_HEAD_DIM), lambda i, j: (layer, 0, 0)),
                  pl.BlockSpec((None, BRANCH_WIDTH, d), lambda i, j: (layer, 0, 0),
                               pipeline_mode=pl.Buffered(1))],
        out_specs=pl.BlockSpec((1, tg, d), lambda i, j: (i, j, 0)),
        out_shape=jax.ShapeDtypeStruct((b, l, d), F32),
        scratch_shapes=[pltpu.VMEM((tm, BRANCH_WIDTH), BF16)] * n_sub,
        compiler_params=_cparams(("parallel", "parallel")),
        name="merge",
    )(x, mix, proj, proj, mk, mv, xq_norm.reshape(-1, 1, X_HEAD_DIM), w_out)


def _mla_qkv_kernel(cq_ref, ckv_ref, kr_ref, posr_ref, invfc_ref, gq_ref, gkv_ref, gqn_ref,
                    gkn_ref, gqr_ref, gkr_ref, wuq_ref, wukv_ref, qt_ref, kn_ref, krope_ref, vt_ref,
                    wqt_ref, wk_ref, wvt_ref):
    half = MLA_ROPE // 2
    tm = cq_ref.shape[1]
    qk = MLA_NOPE + MLA_ROPE

    @pl.when((pl.program_id(0) == 0) & (pl.program_id(1) == 0))
    def _():
        wqt_ref[...] = wuq_ref[...].T.astype(BF16)
        for h in range(MLA_HEADS):
            c0 = h * (MLA_NOPE + MLA_V)
            wk_ref[:, h * MLA_NOPE:(h + 1) * MLA_NOPE] = wukv_ref[:, c0:c0 + MLA_NOPE].astype(BF16)
            wvt_ref[h * MLA_V:(h + 1) * MLA_V, :] = wukv_ref[:, c0 + MLA_NOPE:c0 + MLA_NOPE + MLA_V].T.astype(BF16)

    qscale = (MLA_NOPE + MLA_ROPE) ** -0.5 * math.log2(math.e)

    cq = _rms(cq_ref[0].astype(F32), gq_ref[...])
    ckv = _rms(ckv_ref[0].astype(F32), gkv_ref[...])
    cq_t = cq.T.astype(BF16)
    ckv_t = ckv.T.astype(BF16)
    ckv_b = ckv.astype(BF16)

    def project(h):
        dot = functools.partial(jnp.dot, preferred_element_type=F32)
        k_pair = dot(ckv_b, wk_ref[:, h * MLA_NOPE:(h + 2) * MLA_NOPE]) if h % 2 == 0 else None
        return (dot(wqt_ref[h * qk:(h + 1) * qk, :], cq_t),
                dot(wvt_ref[h * MLA_V:(h + 1) * MLA_V, :], ckv_t), k_pair)

    ang_t = invfc_ref[...] * posr_ref[0].astype(F32)
    cos_t, sin_t = jnp.cos(ang_t), jnp.sin(ang_t)
    g_nope = jnp.broadcast_to(gqn_ref[...], (MLA_NOPE, tm)) * qscale
    g_r1 = jnp.broadcast_to(gqr_ref[:half, :], (half, tm)) * qscale
    g_r2 = jnp.broadcast_to(gqr_ref[half:, :], (half, tm)) * qscale
    ahead = 2
    pending = [project(h) for h in range(ahead)]
    for h in range(MLA_HEADS):
        if h + ahead < MLA_HEADS:
            pending.append(project(h + ahead))
        q, v_t, _ = pending[h]
        k_n = pending[h - h % 2][2][:, (h % 2) * MLA_NOPE:(h % 2 + 1) * MLA_NOPE]
        nope = q[:MLA_NOPE]
        r = lax.rsqrt(jnp.mean(nope * nope, axis=0, keepdims=True) + EPS)
        qt_ref[0, h, :MLA_NOPE, :] = (nope * r * g_nope).astype(BF16)
        x1, x2 = q[MLA_NOPE:MLA_NOPE + half], q[MLA_NOPE + half:MLA_NOPE + MLA_ROPE]
        ss = jnp.sum(x1 * x1, axis=0, keepdims=True) + jnp.sum(x2 * x2, axis=0, keepdims=True)
        r = lax.rsqrt(ss * (1.0 / MLA_ROPE) + EPS)
        x1, x2 = x1 * r * g_r1, x2 * r * g_r2
        qt_ref[0, h, MLA_NOPE:MLA_NOPE + half, :] = (x1 * cos_t - x2 * sin_t).astype(BF16)
        qt_ref[0, h, MLA_NOPE + half:MLA_NOPE + MLA_ROPE, :] = (x1 * sin_t + x2 * cos_t).astype(BF16)
        qt_ref[0, h, MLA_NOPE + MLA_ROPE:, :] = jnp.zeros((MLA_QK_PAD - MLA_NOPE - MLA_ROPE, tm), BF16)
        kn_ref[0, h] = _rms(k_n, gkn_ref[...]).astype(BF16)
        vt_ref[0, h, :MLA_V, :] = v_t.astype(BF16)
        vt_ref[0, h, MLA_V:, :] = jnp.ones((MLA_VL - MLA_V, tm), BF16)

    kr_t = kr_ref[0].astype(F32).T
    x1, x2 = kr_t[:half], kr_t[half:MLA_ROPE]
    ss = jnp.sum(x1 * x1, axis=0, keepdims=True) + jnp.sum(x2 * x2, axis=0, keepdims=True)
    r = lax.rsqrt(ss * (1.0 / MLA_ROPE) + EPS)
    x1, x2 = x1 * r * gkr_ref[:half, :], x2 * r * gkr_ref[half:, :]
    rot = jnp.concatenate([x1 * cos_t - x2 * sin_t, x1 * sin_t + x2 * cos_t,
                           jnp.zeros((LANES - MLA_ROPE, tm), F32)], axis=0)
    krope_ref[0] = rot.T.astype(BF16)


def _mla_qkv(proj, cq_blk, ckv_blk, kr_blk, positions, gains, w_uq, w_ukv, layer, *, tm):
    b, l, _ = proj.shape
    hh = MLA_HEADS
    half = MLA_ROPE // 2
    inv_freq = ROPE_THETA ** (-jnp.arange(half, dtype=F32) / half)
    const = lambda a: pl.BlockSpec(a.shape, lambda i, j: (0,) * a.ndim)
    gq, gkv, gqn, gkn, gqr, gkr = gains
    consts = [inv_freq.reshape(half, 1), gq.reshape(1, -1), gkv.reshape(1, -1), gqn.reshape(-1, 1),
              gkn.reshape(1, -1), gqr.reshape(-1, 1), gkr.reshape(-1, 1)]
    weights = [w_uq, w_ukv]
    return pl.pallas_call(
        _mla_qkv_kernel,
        grid=(b, l // tm),
        in_specs=[pl.BlockSpec((1, tm, MLA_Q_LORA), lambda i, j: (i, j, cq_blk)),
                  pl.BlockSpec((1, tm, MLA_KV_LORA), lambda i, j: (i, j, ckv_blk)),
                  pl.BlockSpec((1, tm, LANES), lambda i, j: (i, j, kr_blk)),
                  pl.BlockSpec((1, 1, tm), lambda i, j: (i, 0, j))] + [const(a) for a in consts]
                 + [pl.BlockSpec(*_layer_block(w, layer)) for w in weights],
        out_specs=[pl.BlockSpec((1, hh, MLA_QK_PAD, tm), lambda i, j: (i, 0, 0, j)),
                   pl.BlockSpec((1, hh, tm, MLA_NOPE), lambda i, j: (i, 0, j, 0)),
                   pl.BlockSpec((1, tm, LANES), lambda i, j: (i, j, 0)),
                   pl.BlockSpec((1, hh, MLA_VL, tm), lambda i, j: (i, 0, 0, j))],
        out_shape=[jax.ShapeDtypeStruct((b, hh, MLA_QK_PAD, l), BF16),
                   jax.ShapeDtypeStruct((b, hh, l, MLA_NOPE), BF16),
                   jax.ShapeDtypeStruct((b, l, LANES), BF16),
                   jax.ShapeDtypeStruct((b, hh, MLA_VL, l), BF16)],
        scratch_shapes=[pltpu.VMEM((hh * (MLA_NOPE + MLA_ROPE), MLA_Q_LORA), BF16),
                        pltpu.VMEM((MLA_KV_LORA, hh * MLA_NOPE), BF16),
                        pltpu.VMEM((hh * MLA_V, MLA_KV_LORA), BF16)],
        compiler_params=_cparams(("arbitrary", "arbitrary")),
        name="mla_qkv",
    )(proj, proj, proj, positions.reshape(b, 1, l), *consts, *weights)


def _flash_kernel(qt_ref, kn_ref, kr_ref, vt_ref, o_ref, m_ref, acc_ref, *, tq, hp, ahead):
    qi = pl.program_id(2)
    m_ref[...] = jnp.full(m_ref.shape, -jnp.inf, F32)
    acc_ref[...] = jnp.zeros(acc_ref.shape, F32)

    half = tq // 2
    lower = (lax.broadcasted_iota(jnp.int32, (half, half), 0)
             <= lax.broadcasted_iota(jnp.int32, (half, half), 1))

    def blocks(j, parts, diagonal):
        base = pl.multiple_of(j * tq, tq)
        items = [(h, pl.ds(base + k0, nk), slice(q0, q0 + nq)) for k0, nk, q0, nq in parts for h in range(hp)]

        def scores(h, rows, cols):
            k = jnp.concatenate([kn_ref[0, h, rows, :], kr_ref[0, rows, :]], axis=-1)
            return jnp.dot(k, qt_ref[0, h, :, cols], preferred_element_type=F32)

        pending = [scores(*it) for it in items[:ahead]]
        for n, (h, rows, cols) in enumerate(items):
            if n + ahead < len(items):
                pending.append(scores(*items[n + ahead]))
            s = pending[n]
            if diagonal:
                square = jnp.where(lower, s[:, :half], jnp.finfo(F32).min)
                s = square if s.shape[1] == half else jnp.concatenate([square, s[:, half:]], axis=1)
            m = m_ref[h, :, cols]
            m_new = jnp.maximum(m, jnp.max(s, axis=0, keepdims=True))
            alpha = jnp.exp2(m - m_new)
            p = jnp.exp2(s - m_new)
            acc_ref[h, :, cols] = alpha * acc_ref[h, :, cols] + jnp.dot(
                vt_ref[0, h, :, rows], p.astype(BF16), preferred_element_type=F32)
            m_ref[h, :, cols] = m_new

    def body(j, carry):
        blocks(j, [(0, tq, 0, tq)], False)
        return carry

    lax.fori_loop(0, qi, body, 0)
    blocks(qi, [(0, half, 0, tq), (half, half, half, half)], True)
    for h in range(hp):
        o_ref[0, :, h * MLA_V:(h + 1) * MLA_V] = (
            acc_ref[h, :MLA_V, :] / acc_ref[h, MLA_V:MLA_V + 1, :]).T.astype(o_ref.dtype)


def _flash(qt, kn, kr, vt, *, tq, hp, ahead):
    b, hh, _, l = qt.shape
    return pl.pallas_call(
        functools.partial(_flash_kernel, tq=tq, hp=hp, ahead=ahead),
        grid=(b, hh // hp, l // tq),
        in_specs=[pl.BlockSpec((1, hp, MLA_QK_PAD, tq), lambda i, h, j: (i, h, 0, j)),
                  pl.BlockSpec((1, hp, l, MLA_NOPE), lambda i, h, j: (i, h, 0, 0)),
                  pl.BlockSpec((1, l, LANES), lambda i, h, j: (i, 0, 0)),
                  pl.BlockSpec((1, hp, MLA_VL, l), lambda i, h, j: (i, h, 0, 0))],
        out_specs=pl.BlockSpec((1, tq, hp * MLA_V), lambda i, h, j: (i, j, h)),
        out_shape=jax.ShapeDtypeStruct((b, l, hh * MLA_V), BF16),
        scratch_shapes=[pltpu.VMEM((hp, 1, tq), F32), pltpu.VMEM((hp, MLA_VL, tq), F32)],
        compiler_params=_cparams(("parallel", "parallel", "parallel")),
        name="flash",
    )(qt, kn, kr, vt)


def _s5_layer(x, ln, w_in, lam_re, lam_im, log_step, b_re, b_im, c_re, c_im, d, w_glu,
              w_out, mem_kv, xq_norm, layer, j):
    b, l, dm = x.shape
    tm = 512
    proj, xn = _s5_in_proj(x, ln, w_in, j, tm=tm, col_chunk=512)
    xg = _s5_ut(xn, w_in, j)
    toep, wout, wst, dcol = _s5_asm(lam_re, lam_im, log_step, c_re, c_im, b_re, b_im, d)
    pw_re, pw_im = _s5_pow(lam_re, lam_im, log_step)
    col = lambda pw: pw.transpose(1, 2, 0)
    yt = _s5_mix(xg, toep, wst, wout, col(pw_re), col(pw_im), dcol, nb=b, gs=8)
    y = _glu(yt, w_glu, j, nb=b, col_chunk=256)
    return _merge(x, y, proj, BRANCH_WIDTH // XQ_WIDTH, 0, *mem_kv, xq_norm, w_out, layer, tm=tm, phased=True)


def _mla_layer(x, positions, ln, w_in, q_lora_norm, kv_lora_norm, w_uq, w_ukv, q_nope_norm, k_nope_norm,
               q_rope_norm, k_rope_norm, w_out, mem_kv, xq_norm, layer, j):
    b, l, dm = x.shape
    o1 = MLA_Q_LORA
    o2 = o1 + MLA_KV_LORA
    o3 = o2 + MLA_ROPE
    o4 = o3 + XQ_WIDTH
    segments = ((o4, BRANCH_WIDTH), (0, o1), (o3, XQ_WIDTH), (o1, MLA_KV_LORA), (o2, MLA_ROPE))
    wout = -(-(o4 + BRANCH_WIDTH) // 512) * 512
    proj = _mla_in_proj(x.reshape(b * l, dm), ln, w_in, j, segments, wout, tm=1024, col_chunk=512)
    proj = proj.reshape(b, l, -1)
    gate_blk = 0
    cq_blk = BRANCH_WIDTH // MLA_Q_LORA
    xq_blk = (BRANCH_WIDTH + MLA_Q_LORA) // XQ_WIDTH
    ckv_blk = (BRANCH_WIDTH + MLA_Q_LORA + XQ_WIDTH) // MLA_KV_LORA
    kr_blk = (BRANCH_WIDTH + MLA_Q_LORA + XQ_WIDTH + MLA_KV_LORA) // LANES
    qt, kn, kr, vt = _mla_qkv(proj, cq_blk, ckv_blk, kr_blk, positions,
                              (q_lora_norm, kv_lora_norm, q_nope_norm, k_nope_norm, q_rope_norm, k_rope_norm),
                              w_uq, w_ukv, j, tm=512)
    attn = _flash(qt, kn, kr, vt, tq=512, hp=12, ahead=2)
    return _merge(x, attn, proj, xq_blk, gate_blk, *mem_kv, xq_norm, w_out, layer, tm=512)


def kernel(x, mem, positions, ln_gain, w_out, mem_norm, w_mem_kv, xq_norm, xk_norm,
           s5_w_in, s5_lambda_re, s5_lambda_im, s5_log_step, s5_b_re, s5_b_im, s5_c_re, s5_c_im,
           s5_d, s5_w_glu, mla_w_in, mla_q_lora_norm, mla_kv_lora_norm, mla_w_uq, mla_w_ukv,
           mla_q_nope_norm, mla_k_nope_norm, mla_q_rope_norm, mla_k_rope_norm):
    depth = ln_gain.shape[0]
    mem_kv = _mem_kv(mem, mem_norm, w_mem_kv, xk_norm)
    for i in range(depth):
        j = i // 2
        if i % 2 == 0:
            x = _s5_layer(x, ln_gain[i], s5_w_in, s5_lambda_re[j], s5_lambda_im[j], s5_log_step[j],
                          s5_b_re[j], s5_b_im[j], s5_c_re[j], s5_c_im[j], s5_d[j], s5_w_glu,
                          w_out, mem_kv, xq_norm, i, j)
        else:
            x = _mla_layer(x, positions, ln_gain[i], mla_w_in, mla_q_lora_norm[j], mla_kv_lora_norm[j],
                           mla_w_uq, mla_w_ukv, mla_q_nope_norm[j], mla_k_nope_norm[j],
                           mla_q_rope_norm[j], mla_k_rope_norm[j],
                           w_out, mem_kv, xq_norm, i, j)
    return x
```

```python
import functools
import math

import jax
import jax.numpy as jnp
from jax import lax
from jax.experimental import pallas as pl
from jax.experimental.pallas import tpu as pltpu

D_MODEL = 1024
BRANCH_WIDTH = 2 * D_MODEL
XQ_WIDTH = BRANCH_WIDTH // 4
PRIMARY_WIDTH = BRANCH_WIDTH - XQ_WIDTH
X_HEADS = 4
X_HEAD_DIM = XQ_WIDTH // X_HEADS
S5_GROUP_CH = 16
S5_GROUPS = PRIMARY_WIDTH // S5_GROUP_CH
S5_STATE = 64
MLA_NOPE = 128
MLA_ROPE = 64
MLA_V = 128
MLA_HEADS = PRIMARY_WIDTH // MLA_V
MLA_Q_LORA = D_MODEL // 2
MLA_KV_LORA = D_MODEL // 4
ROPE_THETA = 10000.0
EPS = 1e-6

LANES = 128
MLA_QK_PAD = 2 * LANES
F32_SUBLANES = 8
BF16_SUBLANES = 16
MLA_VL = MLA_V + BF16_SUBLANES
S5_CHUNK = 2 * LANES // S5_GROUP_CH
S5_SCAN_EXPONENTS = [S5_CHUNK * 2 ** i for i in range(int(math.log2(LANES)))]
VMEM_LIMIT = 56 * 1024 * 1024

F32 = jnp.float32
BF16 = jnp.bfloat16


def _cparams(sem):
    return pltpu.CompilerParams(dimension_semantics=sem, vmem_limit_bytes=VMEM_LIMIT)


def _rms(x, g):
    return x * lax.rsqrt(jnp.mean(x * x, axis=-1, keepdims=True) + EPS) * g


def _layer_block(w, j, block=None, index=None):
    block = tuple(w.shape[1:]) if block is None else block
    index = (0,) * len(block) if index is None else index
    return (None,) + block, lambda *_: (j,) + index


def _mla_in_proj_kernel(x_ref, g_ref, w_ref, o_ref, wp_ref, *, segments, col_chunk):
    @pl.when(pl.program_id(0) == 0)
    def _():
        at = 0
        for start, width in segments:
            wp_ref[:, at:at + width] = w_ref[:, start:start + width].astype(BF16)
            at += width
        wp_ref[:, at:] = jnp.zeros((wp_ref.shape[0], wp_ref.shape[1] - at), BF16)

    xn = _rms(x_ref[...], g_ref[...]).astype(BF16)
    for c in range(o_ref.shape[1] // col_chunk):
        sl = slice(c * col_chunk, (c + 1) * col_chunk)
        o_ref[:, sl] = jnp.dot(xn, wp_ref[:, sl], preferred_element_type=F32).astype(o_ref.dtype)


def _mla_in_proj(x, g, w, j, segments, wout, *, tm, col_chunk):
    n, d = x.shape
    return pl.pallas_call(
        functools.partial(_mla_in_proj_kernel, segments=segments, col_chunk=col_chunk),
        grid=(n // tm,),
        in_specs=[pl.BlockSpec((tm, d), lambda i: (i, 0)),
                  pl.BlockSpec((1, d), lambda i: (0, 0)),
                  pl.BlockSpec(*_layer_block(w, j), pipeline_mode=pl.Buffered(1))],
        out_specs=pl.BlockSpec((tm, wout), lambda i: (i, 0)),
        out_shape=jax.ShapeDtypeStruct((n, wout), BF16),
        scratch_shapes=[pltpu.VMEM((d, wout), BF16)],
        compiler_params=_cparams(("arbitrary",)),
        name="mla_in_proj",
    )(x, g.reshape(1, d), w)


def _to_phase_order(a):
    n, d = a.shape
    return jnp.swapaxes(a.reshape(n // S5_CHUNK, S5_CHUNK, d), 0, 1).reshape(n, d)


def _from_phase_order(a):
    n, d = a.shape
    return jnp.swapaxes(a.reshape(S5_CHUNK, n // S5_CHUNK, d), 0, 1).reshape(n, d)


def _s5_in_proj_kernel(x_ref, g_ref, wg_ref, wx_ref, o_ref, xn_ref, *, tm, col_chunk):
    d = x_ref.shape[2]
    nc = tm // S5_CHUNK
    chunks = [slice(c * col_chunk, (c + 1) * col_chunk) for c in range(BRANCH_WIDTH // col_chunk)]
    weights = [wg_ref[:, sl].astype(BF16) for sl in chunks] + [wx_ref[...].astype(BF16)]
    chunks.append(slice(BRANCH_WIDTH, BRANCH_WIDTH + XQ_WIDTH))

    def normalise(k):
        xn = _rms(_to_phase_order(x_ref[0, k * tm:(k + 1) * tm, :]), g_ref[...]).astype(BF16)
        for s in range(S5_CHUNK):
            xn_ref[0, k * nc:(k + 1) * nc, s * d:(s + 1) * d] = xn[s * nc:(s + 1) * nc]
        return xn

    def project(k, xn):
        for sl, w in zip(chunks, weights):
            o_ref[0, k * tm:(k + 1) * tm, sl] = jnp.dot(xn, w, preferred_element_type=F32).astype(o_ref.dtype)

    n_sub = x_ref.shape[1] // tm
    prev = None
    for k in range(n_sub):
        if prev is not None:
            project(k - 1, prev)
        prev = normalise(k)
    project(n_sub - 1, prev)


def _s5_in_proj(x, g, w_in, j, *, tm, col_chunk):
    b, l, d = x.shape
    wout = BRANCH_WIDTH + XQ_WIDTH
    tg = 2 * tm
    nc = tg // S5_CHUNK
    return pl.pallas_call(
        functools.partial(_s5_in_proj_kernel, tm=tm, col_chunk=col_chunk),
        grid=(b, l // tg),
        in_specs=[pl.BlockSpec((1, tg, d), lambda i, j: (i, j, 0)),
                  pl.BlockSpec((1, d), lambda i, j: (0, 0)),
                  pl.BlockSpec(*_layer_block(w_in, j, (d, BRANCH_WIDTH),
                                             (0, (PRIMARY_WIDTH + XQ_WIDTH) // BRANCH_WIDTH)),
                               pipeline_mode=pl.Buffered(1)),
                  pl.BlockSpec(*_layer_block(w_in, j, (d, XQ_WIDTH), (0, PRIMARY_WIDTH // XQ_WIDTH)),
                               pipeline_mode=pl.Buffered(1))],
        out_specs=[pl.BlockSpec((1, tg, wout), lambda i, j: (i, j, 0)),
                   pl.BlockSpec((1, nc, S5_CHUNK * d), lambda i, j: (i, j, 0))],
        out_shape=[jax.ShapeDtypeStruct((b, l, wout), BF16),
                   jax.ShapeDtypeStruct((b, l // S5_CHUNK, S5_CHUNK * d), BF16)],
        compiler_params=_cparams(("parallel", "parallel")),
        name="s5_in_proj",
    )(x, g.reshape(1, d), w_in, w_in)


def _mem_kv_kernel(m_ref, g_ref, w_ref, kg_ref, k_ref, v_ref):
    b, m, d = m_ref.shape
    w = w_ref[0].astype(BF16)
    for i in range(b):
        mn = _rms(m_ref[i], g_ref[0]).astype(BF16)
        kv = jnp.dot(mn, w, preferred_element_type=F32)
        for h in range(X_HEADS):
            sl = slice(h * X_HEAD_DIM, (h + 1) * X_HEAD_DIM)
            k_ref[0, i, :, sl] = _rms(kv[:, sl], kg_ref[0]).astype(BF16)
        v_ref[0, i] = kv[:, XQ_WIDTH:].astype(BF16)


def _mem_kv(mem, mem_norm, w_mem_kv, xk_norm):
    b, m, d = mem.shape
    depth = w_mem_kv.shape[0]
    out = jax.ShapeDtypeStruct((depth, b, m, XQ_WIDTH), BF16)
    return pl.pallas_call(
        _mem_kv_kernel,
        grid=(depth,),
        in_specs=[pl.BlockSpec((b, m, d), lambda n: (0, 0, 0)),
                  pl.BlockSpec((1, 1, d), lambda n: (n, 0, 0)),
                  pl.BlockSpec((1, d, 2 * XQ_WIDTH), lambda n: (n, 0, 0)),
                  pl.BlockSpec((1, 1, X_HEAD_DIM), lambda n: (n, 0, 0))],
        out_specs=[pl.BlockSpec((1, b, m, XQ_WIDTH), lambda n: (n, 0, 0, 0)),
                   pl.BlockSpec((1, b, m, XQ_WIDTH), lambda n: (n, 0, 0, 0))],
        out_shape=[out, out],
        compiler_params=_cparams(("parallel",)),
        name="mem_kv",
    )(mem, mem_norm.reshape(depth, 1, d), w_mem_kv, xk_norm.reshape(depth, 1, X_HEAD_DIM))


def _s5_pow_kernel(lr_ref, li_ref, ls_ref, pr_ref, pi_ref):
    step = jnp.exp(ls_ref[...])
    zr, zi = lr_ref[...] * step, li_ref[...] * step
    for n in range(pr_ref.shape[0]):
        if n < len(S5_SCAN_EXPONENTS):
            e = S5_SCAN_EXPONENTS[n]
            mag = jnp.exp(zr * e)
            pr_ref[n] = mag * jnp.cos(zi * e)
            pi_ref[n] = mag * jnp.sin(zi * e)
        else:
            pr_ref[n] = jnp.zeros_like(zr)
            pi_ref[n] = jnp.zeros_like(zr)


def _s5_pow(lam_re, lam_im, log_step):
    g, p = lam_re.shape
    slots = -(-len(S5_SCAN_EXPONENTS) // F32_SUBLANES) * F32_SUBLANES
    out = jax.ShapeDtypeStruct((slots, g, p), F32)
    return pl.pallas_call(_s5_pow_kernel, out_shape=[out, out], name="s5_pow")(
        lam_re, lam_im, log_step.reshape(g, 1))


def _s5_asm_kernel(lr_ref, li_ref, ls_ref, cr_ref, ci_ref, btr_ref, bti_ref, br_ref, bi_ref, d_ref,
                   toep_ref, wout_ref, wst_ref, dcol_ref):
    def group(i, carry):
        _s5_asm_group(i, lr_ref, li_ref, ls_ref, cr_ref, ci_ref, btr_ref, bti_ref, br_ref, bi_ref,
                      toep_ref, wout_ref, wst_ref)
        dcol_ref[i] = jnp.concatenate([d_ref[i]] * S5_CHUNK, axis=0)
        return carry

    lax.fori_loop(0, cr_ref.shape[0], group, 0)


def _dot_3pass_tiled(a, b):
    c = b.shape[1]
    rep = ((lax.broadcasted_iota(jnp.int32, (c, S5_CHUNK * c), 1) & (c - 1))
           == lax.broadcasted_iota(jnp.int32, (c, S5_CHUNK * c), 0)).astype(BF16)
    dot = functools.partial(jnp.dot, preferred_element_type=F32)
    a_hi, b_hi = a.astype(BF16), b.astype(BF16)
    a_lo = (a - a_hi.astype(F32)).astype(BF16)
    b_lo = (b - b_hi.astype(F32)).astype(BF16)
    b_hi, b_lo = dot(b_hi, rep).astype(BF16), dot(b_lo, rep).astype(BF16)
    return dot(a_hi, b_hi) + (dot(a_hi, b_lo) + dot(a_lo, b_hi))


def _s5_asm_group(i, lr_ref, li_ref, ls_ref, cr_ref, ci_ref, btr_ref, bti_ref, br_ref, bi_ref,
                  toep_ref, wout_ref, wst_ref):
    t = S5_CHUNK
    lr, li = lr_ref[i], li_ref[i]
    step = jnp.exp(ls_ref[i])
    zr, zi = lr * step, li * step
    rows = -(-(t + 1) // F32_SUBLANES) * F32_SUBLANES
    e = lax.broadcasted_iota(jnp.int32, (rows, lr.shape[1]), 0).astype(F32)
    mag = jnp.exp(e * zr)
    pr, pi = mag * jnp.cos(e * zi), mag * jnp.sin(e * zi)
    den = lr * lr + li * li
    mr = ((pr[1:2] - 1.0) * lr + pi[1:2] * li) / den
    mi = (pi[1:2] * lr - (pr[1:2] - 1.0) * li) / den
    cr, ci = cr_ref[i], ci_ref[i]
    btr, bti = btr_ref[i], bti_ref[i]
    amr = pr[:t] * mr - pi[:t] * mi
    ami = pr[:t] * mi + pi[:t] * mr
    l_re, l_im, w_re, w_im, o_re, o_im = [], [], [], [], [], []
    for k in range(t):
        ar, ai = amr[k:k + 1], ami[k:k + 1]
        l_re.append(cr * ar - ci * ai)
        l_im.append(-(cr * ai + ci * ar))
        ar, ai = amr[t - 1 - k:t - k], ami[t - 1 - k:t - k]
        w_re.append(btr * ar - bti * ai)
        w_im.append(btr * ai + bti * ar)
        ar, ai = pr[k + 1:k + 2], pi[k + 1:k + 2]
        o_re.append(cr * ar - ci * ai)
        o_im.append(-(cr * ai + ci * ar))
    cat = lambda parts: jnp.concatenate(parts, axis=0)
    kt = _dot_3pass_tiled(jnp.concatenate([cat(l_re), cat(l_im)], axis=1),
                          jnp.concatenate([br_ref[i], bi_ref[i]], axis=0))
    n = kt.shape[0]
    blk = lax.shift_right_logical(lax.broadcasted_iota(jnp.int32, kt.shape, 1), int(math.log2(S5_GROUP_CH)))
    toep = jnp.where(blk == 0, kt, 0.0)
    for s in range(1, t):
        shifted = jnp.concatenate([jnp.zeros((s * S5_GROUP_CH, n), F32), kt[:n - s * S5_GROUP_CH]], axis=0)
        toep = jnp.where(blk == s, shifted, toep)
    toep_ref[i] = toep.astype(BF16)
    wout_ref[i] = jnp.concatenate([cat(o_re), cat(o_im)], axis=1).astype(BF16)
    wst_ref[i] = jnp.concatenate([cat(w_re), cat(w_im)], axis=1).T.astype(BF16)


def _s5_asm(lam_re, lam_im, log_step, c_re, c_im, b_re, b_im, d):
    g, c, p = c_re.shape
    tc = S5_CHUNK * c
    gs = 8
    blk = lambda a: pl.BlockSpec((gs,) + a.shape[1:], lambda i: (i, 0, 0))
    bt_re, bt_im = b_re.transpose(0, 2, 1), b_im.transpose(0, 2, 1)
    args = (lam_re.reshape(g, 1, p), lam_im.reshape(g, 1, p), log_step.reshape(g, 1, 1),
            c_re, c_im, bt_re, bt_im, b_re, b_im, d.reshape(g, c, 1))
    return pl.pallas_call(
        _s5_asm_kernel,
        grid=(g // gs,),
        in_specs=[blk(a) for a in args],
        out_specs=[pl.BlockSpec((gs, tc, tc), lambda i: (i, 0, 0)),
                   pl.BlockSpec((gs, tc, 2 * p), lambda i: (i, 0, 0)),
                   pl.BlockSpec((gs, 2 * p, tc), lambda i: (i, 0, 0)),
                   pl.BlockSpec((gs, tc, 1), lambda i: (i, 0, 0))],
        out_shape=[jax.ShapeDtypeStruct((g, tc, tc), BF16), jax.ShapeDtypeStruct((g, tc, 2 * p), BF16),
                   jax.ShapeDtypeStruct((g, 2 * p, tc), BF16), jax.ShapeDtypeStruct((g, tc, 1), F32)],
        compiler_params=_cparams(("parallel",)),
        name="s5_asm",
    )(*args)


def _s5_ut_kernel(xn_ref, w_ref, o_ref, wt_ref):
    @pl.when(pl.program_id(0) == 0)
    def _():
        wt_ref[...] = w_ref[...].T.astype(BF16)

    nb, nc, d = xn_ref.shape
    ut = lax.dot_general(wt_ref[...], xn_ref[...].reshape(nb * nc, d), (((1,), (1,)), ((), ())),
                         preferred_element_type=F32)
    o_ref[...] = ut.astype(BF16).reshape(o_ref.shape)


def _s5_ut(xn, w_in, j):
    b, nc, td = xn.shape
    d = td // S5_CHUNK
    return pl.pallas_call(
        _s5_ut_kernel,
        grid=(S5_CHUNK,),
        in_specs=[pl.BlockSpec((b, nc, d), lambda s: (0, 0, s)),
                  pl.BlockSpec(*_layer_block(w_in, j, (d, PRIMARY_WIDTH)))],
        out_specs=pl.BlockSpec((S5_GROUPS, S5_GROUP_CH, b * nc), lambda s: (0, s, 0)),
        out_shape=jax.ShapeDtypeStruct((S5_GROUPS, S5_CHUNK * S5_GROUP_CH, b * nc), BF16),
        scratch_shapes=[pltpu.VMEM((PRIMARY_WIDTH, d), BF16)],
        compiler_params=_cparams(("arbitrary",)),
        name="s5_ut",
    )(xn, w_in)


def _s5_mix_kernel(x_ref, toep_ref, wst_ref, wout_ref, sr_ref, si_ref, d_ref, o_ref, *, nb):
    p = S5_STATE
    gs = x_ref.shape[0]
    lane = lax.broadcasted_iota(jnp.int32, (p, LANES), 1)
    n_steps = int(math.log2(LANES))

    def scan(g, hloc):
        pw = []
        for i in range(n_steps):
            keep = lane >= (1 << i)
            pw.append((jnp.where(keep, jnp.broadcast_to(sr_ref[g, :, i:i + 1], (p, LANES)), 0.0),
                       jnp.where(keep, jnp.broadcast_to(si_ref[g, :, i:i + 1], (p, LANES)), 0.0)))
        h_re = [hloc[:p, b * LANES:(b + 1) * LANES] for b in range(nb)]
        h_im = [hloc[p:, b * LANES:(b + 1) * LANES] for b in range(nb)]
        for i in range(n_steps):
            ar, ai = pw[i]
            r_sh = [pltpu.roll(v, 1 << i, 1) for v in h_re]
            i_sh = [pltpu.roll(v, 1 << i, 1) for v in h_im]
            h_re = [h_re[b] + ar * r_sh[b] - ai * i_sh[b] for b in range(nb)]
            h_im = [h_im[b] + ar * i_sh[b] + ai * r_sh[b] for b in range(nb)]
        h_re = [jnp.where(lane >= 1, pltpu.roll(v, 1, 1), 0.0) for v in h_re]
        h_im = [jnp.where(lane >= 1, pltpu.roll(v, 1, 1), 0.0) for v in h_im]
        return jnp.concatenate([jnp.concatenate(h_re, axis=1), jnp.concatenate(h_im, axis=1)],
                               axis=0).astype(BF16)

    def outputs(g, h):
        x = x_ref[g]
        y = (jnp.dot(toep_ref[g], x, preferred_element_type=F32)
             + jnp.dot(wout_ref[g], h, preferred_element_type=F32)
             + d_ref[g] * x.astype(F32))
        o_ref[:, g * S5_GROUP_CH:(g + 1) * S5_GROUP_CH, :] = (
            jax.nn.gelu(y).astype(o_ref.dtype).reshape(S5_CHUNK, S5_GROUP_CH, y.shape[1]))

    h_prev = None
    for g in range(gs):
        hloc = jnp.dot(wst_ref[g], x_ref[g], preferred_element_type=F32)
        h = scan(g, hloc)
        if h_prev is not None:
            outputs(g - 1, h_prev)
        h_prev = h
    outputs(gs - 1, h_prev)


def _s5_mix(xg, toep, wst, wout, sc_re, sc_im, dcol, *, nb, gs):
    g, tc, cols = xg.shape
    assert cols == nb * LANES, "one batch's chunks must fill exactly one 128-lane block"
    blk = lambda a: pl.BlockSpec((gs,) + a.shape[1:], lambda i: (i, 0, 0))
    return pl.pallas_call(
        functools.partial(_s5_mix_kernel, nb=nb),
        grid=(g // gs,),
        in_specs=[blk(a) for a in (xg, toep, wst, wout, sc_re, sc_im, dcol)],
        out_specs=pl.BlockSpec((S5_CHUNK, gs * S5_GROUP_CH, cols), lambda i: (0, i, 0)),
        out_shape=jax.ShapeDtypeStruct((S5_CHUNK, g * S5_GROUP_CH, cols), BF16),
        compiler_params=_cparams(("parallel",)),
        name="s5_mix",
    )(xg, toep, wst, wout, sc_re, sc_im, dcol)


def _glu_kernel(y_ref, w_ref, o_ref, *, col_chunk):
    y = y_ref[0].T
    half = o_ref.shape[-1]
    for c in range(half // col_chunk):
        wa = w_ref[:, c * col_chunk:(c + 1) * col_chunk].astype(BF16)
        wg = w_ref[:, half + c * col_chunk:half + (c + 1) * col_chunk].astype(BF16)
        a = jnp.dot(y, wa, preferred_element_type=F32)
        g = jnp.dot(y, wg, preferred_element_type=F32)
        o_ref[:, :, c * col_chunk:(c + 1) * col_chunk] = (
            (a * jax.nn.sigmoid(g)).astype(o_ref.dtype).reshape(o_ref.shape[:2] + (col_chunk,)))


def _glu(yt, w, layer, *, nb, col_chunk):
    t, k, cols = yt.shape
    nc = cols // nb
    half = w.shape[2] // 2
    return pl.pallas_call(
        functools.partial(_glu_kernel, col_chunk=col_chunk),
        grid=(t,),
        in_specs=[pl.BlockSpec((1, k, cols), lambda j: (j, 0, 0)),
                  pl.BlockSpec(*_layer_block(w, layer), pipeline_mode=pl.Buffered(1))],
        out_specs=pl.BlockSpec((nb, nc, half), lambda j: (0, 0, j)),
        out_shape=jax.ShapeDtypeStruct((nb, nc, t * half), BF16),
        compiler_params=_cparams(("parallel",)),
        name="glu",
    )(yt, w)


def _merge_kernel(x_ref, mix_ref, xq_ref, gate_ref, k_ref, v_ref, qg_ref, w_ref, o_ref, *cat_refs, tm, phased):
    scale = X_HEAD_DIM ** -0.5
    nc = tm // S5_CHUNK
    w = w_ref[...].astype(BF16)

    def gather(k, cat_ref):
        r0 = k * tm
        gate = gate_ref[0, r0:r0 + tm, :]
        sg = gate * jax.nn.sigmoid(gate)
        if phased:
            for s in range(S5_CHUNK):
                rows = slice(s * nc, (s + 1) * nc)
                mix = mix_ref[0, k * nc:(k + 1) * nc, s * PRIMARY_WIDTH:(s + 1) * PRIMARY_WIDTH]
                cat_ref[rows, :PRIMARY_WIDTH] = mix * sg[rows, :PRIMARY_WIDTH]
        else:
            cat_ref[:, :PRIMARY_WIDTH] = mix_ref[0, r0:r0 + tm, :] * sg[:, :PRIMARY_WIDTH]
        for h in range(X_HEADS):
            sl = slice(h * X_HEAD_DIM, (h + 1) * X_HEAD_DIM)
            q = _rms(xq_ref[0, r0:r0 + tm, sl].astype(F32), qg_ref[...]).astype(BF16)
            s = lax.dot_general(q, k_ref[0, :, sl], (((1,), (1,)), ((), ())), preferred_element_type=F32) * scale
            p = jnp.exp(s - jnp.max(s, axis=-1, keepdims=True))
            p = (p / jnp.sum(p, axis=-1, keepdims=True)).astype(BF16)
            mo = jnp.dot(p, v_ref[0, :, sl], preferred_element_type=F32)
            osl = slice(PRIMARY_WIDTH + h * X_HEAD_DIM, PRIMARY_WIDTH + (h + 1) * X_HEAD_DIM)
            cat_ref[:, osl] = mo.astype(BF16) * sg[:, osl]

    def project(k, cat_ref):
        r0 = k * tm
        delta = jnp.dot(cat_ref[...], w, preferred_element_type=F32)
        o_ref[0, r0:r0 + tm, :] = x_ref[0, r0:r0 + tm, :] + (_from_phase_order(delta) if phased else delta)

    n_sub = len(cat_refs)
    for k in range(n_sub):
        if k > 0:
            project(k - 1, cat_refs[k - 1])
        gather(k, cat_refs[k])
    project(n_sub - 1, cat_refs[n_sub - 1])


def _merge(x, mix, proj, xq_blk, gate_blk, mk, mv, xq_norm, w_out, layer, *, tm, phased=False):
    b, l, d = x.shape
    m = mk.shape[2]
    n_sub = 2
    tg = n_sub * tm
    mix_spec = (pl.BlockSpec((1, tg // S5_CHUNK, S5_CHUNK * PRIMARY_WIDTH), lambda i, j: (i, j, 0)) if phased
                else pl.BlockSpec((1, tg, PRIMARY_WIDTH), lambda i, j: (i, j, 0)))
    return pl.pallas_call(
        functools.partial(_merge_kernel, tm=tm, phased=phased),
        grid=(b, l // tg),
        in_specs=[pl.BlockSpec((1, tg, d), lambda i, j: (i, j, 0)),
                  mix_spec,
                  pl.BlockSpec((1, tg, XQ_WIDTH), lambda i, j: (i, j, xq_blk)),
                  pl.BlockSpec((1, tg, BRANCH_WIDTH), lambda i, j: (i, j, gate_blk)),
                  pl.BlockSpec((None, 1, m, XQ_WIDTH), lambda i, j: (layer, i, 0, 0)),
                  pl.BlockSpec((None, 1, m, XQ_WIDTH), lambda i, j: (layer, i, 0, 0)),
                  pl.BlockSpec((None, 1, X_HEAD_DIM), lambda i, j: (layer, 0, 0)),
                  pl.BlockSpec((None, BRANCH_WIDTH, d), lambda i, j: (layer, 0, 0),
                               pipeline_mode=pl.Buffered(1))],
        out_specs=pl.BlockSpec((1, tg, d), lambda i, j: (i, j, 0)),
        out_shape=jax.ShapeDtypeStruct((b, l, d), F32),
        scratch_shapes=[pltpu.VMEM((tm, BRANCH_WIDTH), BF16)] * n_sub,
        compiler_params=_cparams(("parallel", "parallel")),
        name="merge",
    )(x, mix, proj, proj, mk, mv, xq_norm.reshape(-1, 1, X_HEAD_DIM), w_out)


def _mla_qkv_kernel(cq_ref, ckv_ref, kr_ref, posr_ref, invfc_ref, gq_ref, gkv_ref, gqn_ref,
                    gkn_ref, gqr_ref, gkr_ref, wuq_ref, wukv_ref, qt_ref, kn_ref, krope_ref, vt_ref,
                    wqt_ref, wk_ref, wvt_ref):
    half = MLA_ROPE // 2
    tm = cq_ref.shape[1]
    qk = MLA_NOPE + MLA_ROPE

    @pl.when((pl.program_id(0) == 0) & (pl.program_id(1) == 0))
    def _():
        wqt_ref[...] = wuq_ref[...].T.astype(BF16)
        for h in range(MLA_HEADS):
            c0 = h * (MLA_NOPE + MLA_V)
            wk_ref[:, h * MLA_NOPE:(h + 1) * MLA_NOPE] = wukv_ref[:, c0:c0 + MLA_NOPE].astype(BF16)
            wvt_ref[h * MLA_V:(h + 1) * MLA_V, :] = wukv_ref[:, c0 + MLA_NOPE:c0 + MLA_NOPE + MLA_V].T.astype(BF16)

    qscale = (MLA_NOPE + MLA_ROPE) ** -0.5 * math.log2(math.e)

    cq = _rms(cq_ref[0].astype(F32), gq_ref[...])
    ckv = _rms(ckv_ref[0].astype(F32), gkv_ref[...])
    cq_t = cq.T.astype(BF16)
    ckv_t = ckv.T.astype(BF16)
    ckv_b = ckv.astype(BF16)

    def project(h):
        dot = functools.partial(jnp.dot, preferred_element_type=F32)
        k_pair = dot(ckv_b, wk_ref[:, h * MLA_NOPE:(h + 2) * MLA_NOPE]) if h % 2 == 0 else None
        return (dot(wqt_ref[h * qk:(h + 1) * qk, :], cq_t),
                dot(wvt_ref[h * MLA_V:(h + 1) * MLA_V, :], ckv_t), k_pair)

    ang_t = invfc_ref[...] * posr_ref[0].astype(F32)
    cos_t, sin_t = jnp.cos(ang_t), jnp.sin(ang_t)
    g_nope = jnp.broadcast_to(gqn_ref[...], (MLA_NOPE, tm)) * qscale
    g_r1 = jnp.broadcast_to(gqr_ref[:half, :], (half, tm)) * qscale
    g_r2 = jnp.broadcast_to(gqr_ref[half:, :], (half, tm)) * qscale
    ahead = 2
    pending = [project(h) for h in range(ahead)]
    for h in range(MLA_HEADS):
        if h + ahead < MLA_HEADS:
            pending.append(project(h + ahead))
        q, v_t, _ = pending[h]
        k_n = pending[h - h % 2][2][:, (h % 2) * MLA_NOPE:(h % 2 + 1) * MLA_NOPE]
        nope = q[:MLA_NOPE]
        r = lax.rsqrt(jnp.mean(nope * nope, axis=0, keepdims=True) + EPS)
        qt_ref[0, h, :MLA_NOPE, :] = (nope * r * g_nope).astype(BF16)
        x1, x2 = q[MLA_NOPE:MLA_NOPE + half], q[MLA_NOPE + half:MLA_NOPE + MLA_ROPE]
        ss = jnp.sum(x1 * x1, axis=0, keepdims=True) + jnp.sum(x2 * x2, axis=0, keepdims=True)
        r = lax.rsqrt(ss * (1.0 / MLA_ROPE) + EPS)
        x1, x2 = x1 * r * g_r1, x2 * r * g_r2
        qt_ref[0, h, MLA_NOPE:MLA_NOPE + half, :] = (x1 * cos_t - x2 * sin_t).astype(BF16)
        qt_ref[0, h, MLA_NOPE + half:MLA_NOPE + MLA_ROPE, :] = (x1 * sin_t + x2 * cos_t).astype(BF16)
        qt_ref[0, h, MLA_NOPE + MLA_ROPE:, :] = jnp.zeros((MLA_QK_PAD - MLA_NOPE - MLA_ROPE, tm), BF16)
        kn_ref[0, h] = _rms(k_n, gkn_ref[...]).astype(BF16)
        vt_ref[0, h, :MLA_V, :] = v_t.astype(BF16)
        vt_ref[0, h, MLA_V:, :] = jnp.ones((MLA_VL - MLA_V, tm), BF16)

    kr_t = kr_ref[0].astype(F32).T
    x1, x2 = kr_t[:half], kr_t[half:MLA_ROPE]
    ss = jnp.sum(x1 * x1, axis=0, keepdims=True) + jnp.sum(x2 * x2, axis=0, keepdims=True)
    r = lax.rsqrt(ss * (1.0 / MLA_ROPE) + EPS)
    x1, x2 = x1 * r * gkr_ref[:half, :], x2 * r * gkr_ref[half:, :]
    rot = jnp.concatenate([x1 * cos_t - x2 * sin_t, x1 * sin_t + x2 * cos_t,
                           jnp.zeros((LANES - MLA_ROPE, tm), F32)], axis=0)
    krope_ref[0] = rot.T.astype(BF16)


def _mla_qkv(proj, cq_blk, ckv_blk, kr_blk, positions, gains, w_uq, w_ukv, layer, *, tm):
    b, l, _ = proj.shape
    hh = MLA_HEADS
    half = MLA_ROPE // 2
    inv_freq = ROPE_THETA ** (-jnp.arange(half, dtype=F32) / half)
    const = lambda a: pl.BlockSpec(a.shape, lambda i, j: (0,) * a.ndim)
    gq, gkv, gqn, gkn, gqr, gkr = gains
    consts = [inv_freq.reshape(half, 1), gq.reshape(1, -1), gkv.reshape(1, -1), gqn.reshape(-1, 1),
              gkn.reshape(1, -1), gqr.reshape(-1, 1), gkr.reshape(-1, 1)]
    weights = [w_uq, w_ukv]
    return pl.pallas_call(
        _mla_qkv_kernel,
        grid=(b, l // tm),
        in_specs=[pl.BlockSpec((1, tm, MLA_Q_LORA), lambda i, j: (i, j, cq_blk)),
                  pl.BlockSpec((1, tm, MLA_KV_LORA), lambda i, j: (i, j, ckv_blk)),
                  pl.BlockSpec((1, tm, LANES), lambda i, j: (i, j, kr_blk)),
                  pl.BlockSpec((1, 1, tm), lambda i, j: (i, 0, j))] + [const(a) for a in consts]
                 + [pl.BlockSpec(*_layer_block(w, layer)) for w in weights],
        out_specs=[pl.BlockSpec((1, hh, MLA_QK_PAD, tm), lambda i, j: (i, 0, 0, j)),
                   pl.BlockSpec((1, hh, tm, MLA_NOPE), lambda i, j: (i, 0, j, 0)),
                   pl.BlockSpec((1, tm, LANES), lambda i, j: (i, j, 0)),
                   pl.BlockSpec((1, hh, MLA_VL, tm), lambda i, j: (i, 0, 0, j))],
        out_shape=[jax.ShapeDtypeStruct((b, hh, MLA_QK_PAD, l), BF16),
                   jax.ShapeDtypeStruct((b, hh, l, MLA_NOPE), BF16),
                   jax.ShapeDtypeStruct((b, l, LANES), BF16),
                   jax.ShapeDtypeStruct((b, hh, MLA_VL, l), BF16)],
        scratch_shapes=[pltpu.VMEM((hh * (MLA_NOPE + MLA_ROPE), MLA_Q_LORA), BF16),
                        pltpu.VMEM((MLA_KV_LORA, hh * MLA_NOPE), BF16),
                        pltpu.VMEM((hh * MLA_V, MLA_KV_LORA), BF16)],
        compiler_params=_cparams(("arbitrary", "arbitrary")),
        name="mla_qkv",
    )(proj, proj, proj, positions.reshape(b, 1, l), *consts, *weights)


def _flash_kernel(qt_ref, kn_ref, kr_ref, vt_ref, o_ref, m_ref, acc_ref, *, tq, hp, ahead_full, ahead_diagonal):
    qi = pl.program_id(2)
    m_ref[...] = jnp.full(m_ref.shape, -jnp.inf, F32)
    acc_ref[...] = jnp.zeros(acc_ref.shape, F32)

    half = tq // 2
    lower = (lax.broadcasted_iota(jnp.int32, (half, half), 0)
             <= lax.broadcasted_iota(jnp.int32, (half, half), 1))

    def blocks(j, parts, diagonal):
        ahead = ahead_diagonal if diagonal else ahead_full
        base = pl.multiple_of(j * tq, tq)
        items = [(h, pl.ds(base + k0, nk), slice(q0, q0 + nq)) for k0, nk, q0, nq in parts for h in range(hp)]

        def scores(h, rows, cols):
            k = jnp.concatenate([kn_ref[0, h, rows, :], kr_ref[0, rows, :]], axis=-1)
            return jnp.dot(k, qt_ref[0, h, :, cols], preferred_element_type=F32)

        pending = [scores(*it) for it in items[:ahead]]
        for n, (h, rows, cols) in enumerate(items):
            if n + ahead < len(items):
                pending.append(scores(*items[n + ahead]))
            s = pending[n]
            if diagonal:
                square = jnp.where(lower, s[:, :half], jnp.finfo(F32).min)
                s = square if s.shape[1] == half else jnp.concatenate([square, s[:, half:]], axis=1)
            m = m_ref[h, :, cols]
            m_new = jnp.maximum(m, jnp.max(s, axis=0, keepdims=True))
            alpha = jnp.exp2(m - m_new)
            p = jnp.exp2(s - m_new)
            acc_ref[h, :, cols] = alpha * acc_ref[h, :, cols] + jnp.dot(
                vt_ref[0, h, :, rows], p.astype(BF16), preferred_element_type=F32)
            m_ref[h, :, cols] = m_new

    def body(j, carry):
        blocks(j, [(0, tq, 0, tq)], False)
        return carry

    lax.fori_loop(0, qi, body, 0)
    blocks(qi, [(0, half, 0, tq), (half, half, half, half)], True)
    for h in range(hp):
        o_ref[0, :, h * MLA_V:(h + 1) * MLA_V] = (
            acc_ref[h, :MLA_V, :] / acc_ref[h, MLA_V:MLA_V + 1, :]).T.astype(o_ref.dtype)


def _flash(qt, kn, kr, vt, *, tq, hp, ahead_full, ahead_diagonal):
    b, hh, _, l = qt.shape
    return pl.pallas_call(
        functools.partial(_flash_kernel, tq=tq, hp=hp, ahead_full=ahead_full, ahead_diagonal=ahead_diagonal),
        grid=(b, hh // hp, l // tq),
        in_specs=[pl.BlockSpec((1, hp, MLA_QK_PAD, tq), lambda i, h, j: (i, h, 0, j)),
                  pl.BlockSpec((1, hp, l, MLA_NOPE), lambda i, h, j: (i, h, 0, 0)),
                  pl.BlockSpec((1, l, LANES), lambda i, h, j: (i, 0, 0)),
                  pl.BlockSpec((1, hp, MLA_VL, l), lambda i, h, j: (i, h, 0, 0))],
        out_specs=pl.BlockSpec((1, tq, hp * MLA_V), lambda i, h, j: (i, j, h)),
        out_shape=jax.ShapeDtypeStruct((b, l, hh * MLA_V), BF16),
        scratch_shapes=[pltpu.VMEM((hp, 1, tq), F32), pltpu.VMEM((hp, MLA_VL, tq), F32)],
        compiler_params=_cparams(("parallel", "parallel", "parallel")),
        name="flash",
    )(qt, kn, kr, vt)


def _s5_layer(x, ln, w_in, lam_re, lam_im, log_step, b_re, b_im, c_re, c_im, d, w_glu,
              w_out, mem_kv, xq_norm, layer, j):
    b, l, dm = x.shape
    tm = 512
    proj, xn = _s5_in_proj(x, ln, w_in, j, tm=tm, col_chunk=512)
    xg = _s5_ut(xn, w_in, j)
    toep, wout, wst, dcol = _s5_asm(lam_re, lam_im, log_step, c_re, c_im, b_re, b_im, d)
    pw_re, pw_im = _s5_pow(lam_re, lam_im, log_step)
    col = lambda pw: pw.transpose(1, 2, 0)
    yt = _s5_mix(xg, toep, wst, wout, col(pw_re), col(pw_im), dcol, nb=b, gs=8)
    y = _glu(yt, w_glu, j, nb=b, col_chunk=256)
    return _merge(x, y, proj, BRANCH_WIDTH // XQ_WIDTH, 0, *mem_kv, xq_norm, w_out, layer, tm=tm, phased=True)


def _mla_layer(x, positions, ln, w_in, q_lora_norm, kv_lora_norm, w_uq, w_ukv, q_nope_norm, k_nope_norm,
               q_rope_norm, k_rope_norm, w_out, mem_kv, xq_norm, layer, j):
    b, l, dm = x.shape
    o1 = MLA_Q_LORA
    o2 = o1 + MLA_KV_LORA
    o3 = o2 + MLA_ROPE
    o4 = o3 + XQ_WIDTH
    segments = ((o4, BRANCH_WIDTH), (0, o1), (o3, XQ_WIDTH), (o1, MLA_KV_LORA), (o2, MLA_ROPE))
    wout = -(-(o4 + BRANCH_WIDTH) // 512) * 512
    proj = _mla_in_proj(x.reshape(b * l, dm), ln, w_in, j, segments, wout, tm=512, col_chunk=512)
    proj = proj.reshape(b, l, -1)
    gate_blk = 0
    cq_blk = BRANCH_WIDTH // MLA_Q_LORA
    xq_blk = (BRANCH_WIDTH + MLA_Q_LORA) // XQ_WIDTH
    ckv_blk = (BRANCH_WIDTH + MLA_Q_LORA + XQ_WIDTH) // MLA_KV_LORA
    kr_blk = (BRANCH_WIDTH + MLA_Q_LORA + XQ_WIDTH + MLA_KV_LORA) // LANES
    qt, kn, kr, vt = _mla_qkv(proj, cq_blk, ckv_blk, kr_blk, positions,
                              (q_lora_norm, kv_lora_norm, q_nope_norm, k_nope_norm, q_rope_norm, k_rope_norm),
                              w_uq, w_ukv, j, tm=512)
    attn = _flash(qt, kn, kr, vt, tq=512, hp=12, ahead_full=2, ahead_diagonal=4)
    return _merge(x, attn, proj, xq_blk, gate_blk, *mem_kv, xq_norm, w_out, layer, tm=512)


def kernel(x, mem, positions, ln_gain, w_out, mem_norm, w_mem_kv, xq_norm, xk_norm,
           s5_w_in, s5_lambda_re, s5_lambda_im, s5_log_step, s5_b_re, s5_b_im, s5_c_re, s5_c_im,
           s5_d, s5_w_glu, mla_w_in, mla_q_lora_norm, mla_kv_lora_norm, mla_w_uq, mla_w_ukv,
           mla_q_nope_norm, mla_k_nope_norm, mla_q_rope_norm, mla_k_rope_norm):
    depth = ln_gain.shape[0]
    mem_kv = _mem_kv(mem, mem_norm, w_mem_kv, xk_norm)
    for i in range(depth):
        j = i // 2
        if i % 2 == 0:
            x = _s5_layer(x, ln_gain[i], s5_w_in, s5_lambda_re[j], s5_lambda_im[j], s5_log_step[j],
                          s5_b_re[j], s5_b_im[j], s5_c_re[j], s5_c_im[j], s5_d[j], s5_w_glu,
                          w_out, mem_kv, xq_norm, i, j)
        else:
            x = _mla_layer(x, positions, ln_gain[i], mla_w_in, mla_q_lora_norm[j], mla_kv_lora_norm[j],
                           mla_w_uq, mla_w_ukv, mla_q_nope_norm[j], mla_k_nope_norm[j],
                           mla_q_rope_norm[j], mla_k_rope_norm[j],
                           w_out, mem_kv, xq_norm, i, j)
    return x
```

```python
import functools
import math

import jax
import jax.numpy as jnp
from jax import lax
from jax.experimental import pallas as pl
from jax.experimental.pallas import tpu as pltpu

D_MODEL = 1024
BRANCH_WIDTH = 2 * D_MODEL
XQ_WIDTH = BRANCH_WIDTH // 4
PRIMARY_WIDTH = BRANCH_WIDTH - XQ_WIDTH
X_HEADS = 4
X_HEAD_DIM = XQ_WIDTH // X_HEADS
S5_GROUP_CH = 16
S5_GROUPS = PRIMARY_WIDTH // S5_GROUP_CH
S5_STATE = 64
MLA_NOPE = 128
MLA_ROPE = 64
MLA_V = 128
MLA_HEADS = PRIMARY_WIDTH // MLA_V
MLA_Q_LORA = D_MODEL // 2
MLA_KV_LORA = D_MODEL // 4
ROPE_THETA = 10000.0
EPS = 1e-6

LANES = 128
MLA_QK_PAD = 2 * LANES
F32_SUBLANES = 8
BF16_SUBLANES = 16
MLA_VL = MLA_V + BF16_SUBLANES
S5_CHUNK = 2 * LANES // S5_GROUP_CH
S5_SCAN_EXPONENTS = [S5_CHUNK * 2 ** i for i in range(int(math.log2(LANES)))]
VMEM_LIMIT = 56 * 1024 * 1024

F32 = jnp.float32
BF16 = jnp.bfloat16


def _cparams(sem):
    return pltpu.CompilerParams(dimension_semantics=sem, vmem_limit_bytes=VMEM_LIMIT)


def _rms(x, g):
    return x * lax.rsqrt(jnp.mean(x * x, axis=-1, keepdims=True) + EPS) * g


def _layer_block(w, j, block=None, index=None):
    block = tuple(w.shape[1:]) if block is None else block
    index = (0,) * len(block) if index is None else index
    return (None,) + block, lambda *_: (j,) + index


def _mla_in_proj_kernel(x_ref, g_ref, w_ref, o_ref, wp_ref, *, segments, col_chunk):
    @pl.when(pl.program_id(0) == 0)
    def _():
        at = 0
        for start, width in segments:
            wp_ref[:, at:at + width] = w_ref[:, start:start + width].astype(BF16)
            at += width
        wp_ref[:, at:] = jnp.zeros((wp_ref.shape[0], wp_ref.shape[1] - at), BF16)

    xn = _rms(x_ref[...], g_ref[...]).astype(BF16)
    for c in range(o_ref.shape[1] // col_chunk):
        sl = slice(c * col_chunk, (c + 1) * col_chunk)
        o_ref[:, sl] = jnp.dot(xn, wp_ref[:, sl], preferred_element_type=F32).astype(o_ref.dtype)


def _mla_in_proj(x, g, w, j, segments, wout, *, tm, col_chunk):
    n, d = x.shape
    return pl.pallas_call(
        functools.partial(_mla_in_proj_kernel, segments=segments, col_chunk=col_chunk),
        grid=(n // tm,),
        in_specs=[pl.BlockSpec((tm, d), lambda i: (i, 0)),
                  pl.BlockSpec((1, d), lambda i: (0, 0)),
                  pl.BlockSpec(*_layer_block(w, j), pipeline_mode=pl.Buffered(1))],
        out_specs=pl.BlockSpec((tm, wout), lambda i: (i, 0)),
        out_shape=jax.ShapeDtypeStruct((n, wout), BF16),
        scratch_shapes=[pltpu.VMEM((d, wout), BF16)],
        compiler_params=_cparams(("arbitrary",)),
        name="mla_in_proj",
    )(x, g.reshape(1, d), w)


def _to_phase_order(a):
    n, d = a.shape
    return jnp.swapaxes(a.reshape(n // S5_CHUNK, S5_CHUNK, d), 0, 1).reshape(n, d)


def _from_phase_order(a):
    n, d = a.shape
    return jnp.swapaxes(a.reshape(S5_CHUNK, n // S5_CHUNK, d), 0, 1).reshape(n, d)


def _s5_in_proj_kernel(x_ref, g_ref, wg_ref, wx_ref, o_ref, xn_ref, *, tm, col_chunk):
    d = x_ref.shape[2]
    nc = tm // S5_CHUNK
    chunks = [slice(c * col_chunk, (c + 1) * col_chunk) for c in range(BRANCH_WIDTH // col_chunk)]
    weights = [wg_ref[:, sl].astype(BF16) for sl in chunks] + [wx_ref[...].astype(BF16)]
    chunks.append(slice(BRANCH_WIDTH, BRANCH_WIDTH + XQ_WIDTH))

    def normalise(k):
        xn = _rms(_to_phase_order(x_ref[0, k * tm:(k + 1) * tm, :]), g_ref[...]).astype(BF16)
        for s in range(S5_CHUNK):
            xn_ref[0, k * nc:(k + 1) * nc, s * d:(s + 1) * d] = xn[s * nc:(s + 1) * nc]
        return xn

    def project(k, xn):
        for sl, w in zip(chunks, weights):
            o_ref[0, k * tm:(k + 1) * tm, sl] = jnp.dot(xn, w, preferred_element_type=F32).astype(o_ref.dtype)

    n_sub = x_ref.shape[1] // tm
    prev = None
    for k in range(n_sub):
        if prev is not None:
            project(k - 1, prev)
        prev = normalise(k)
    project(n_sub - 1, prev)


def _s5_in_proj(x, g, w_in, j, *, tm, col_chunk):
    b, l, d = x.shape
    wout = BRANCH_WIDTH + XQ_WIDTH
    tg = 2 * tm
    nc = tg // S5_CHUNK
    return pl.pallas_call(
        functools.partial(_s5_in_proj_kernel, tm=tm, col_chunk=col_chunk),
        grid=(b, l // tg),
        in_specs=[pl.BlockSpec((1, tg, d), lambda i, j: (i, j, 0)),
                  pl.BlockSpec((1, d), lambda i, j: (0, 0)),
                  pl.BlockSpec(*_layer_block(w_in, j, (d, BRANCH_WIDTH),
                                             (0, (PRIMARY_WIDTH + XQ_WIDTH) // BRANCH_WIDTH)),
                               pipeline_mode=pl.Buffered(1)),
                  pl.BlockSpec(*_layer_block(w_in, j, (d, XQ_WIDTH), (0, PRIMARY_WIDTH // XQ_WIDTH)),
                               pipeline_mode=pl.Buffered(1))],
        out_specs=[pl.BlockSpec((1, tg, wout), lambda i, j: (i, j, 0)),
                   pl.BlockSpec((1, nc, S5_CHUNK * d), lambda i, j: (i, j, 0))],
        out_shape=[jax.ShapeDtypeStruct((b, l, wout), BF16),
                   jax.ShapeDtypeStruct((b, l // S5_CHUNK, S5_CHUNK * d), BF16)],
        compiler_params=_cparams(("parallel", "parallel")),
        name="s5_in_proj",
    )(x, g.reshape(1, d), w_in, w_in)


def _mem_kv_kernel(m_ref, g_ref, w_ref, kg_ref, k_ref, v_ref):
    b, m, d = m_ref.shape
    w = w_ref[0].astype(BF16)
    for i in range(b):
        mn = _rms(m_ref[i], g_ref[0]).astype(BF16)
        kv = jnp.dot(mn, w, preferred_element_type=F32)
        for h in range(X_HEADS):
            sl = slice(h * X_HEAD_DIM, (h + 1) * X_HEAD_DIM)
            k_ref[0, i, :, sl] = _rms(kv[:, sl], kg_ref[0]).astype(BF16)
        v_ref[0, i] = kv[:, XQ_WIDTH:].astype(BF16)


def _mem_kv(mem, mem_norm, w_mem_kv, xk_norm):
    b, m, d = mem.shape
    depth = w_mem_kv.shape[0]
    out = jax.ShapeDtypeStruct((depth, b, m, XQ_WIDTH), BF16)
    return pl.pallas_call(
        _mem_kv_kernel,
        grid=(depth,),
        in_specs=[pl.BlockSpec((b, m, d), lambda n: (0, 0, 0)),
                  pl.BlockSpec((1, 1, d), lambda n: (n, 0, 0)),
                  pl.BlockSpec((1, d, 2 * XQ_WIDTH), lambda n: (n, 0, 0)),
                  pl.BlockSpec((1, 1, X_HEAD_DIM), lambda n: (n, 0, 0))],
        out_specs=[pl.BlockSpec((1, b, m, XQ_WIDTH), lambda n: (n, 0, 0, 0)),
                   pl.BlockSpec((1, b, m, XQ_WIDTH), lambda n: (n, 0, 0, 0))],
        out_shape=[out, out],
        compiler_params=_cparams(("parallel",)),
        name="mem_kv",
    )(mem, mem_norm.reshape(depth, 1, d), w_mem_kv, xk_norm.reshape(depth, 1, X_HEAD_DIM))


def _s5_pow_kernel(lr_ref, li_ref, ls_ref, pr_ref, pi_ref):
    step = jnp.exp(ls_ref[...])
    zr, zi = lr_ref[...] * step, li_ref[...] * step
    for n in range(pr_ref.shape[0]):
        if n < len(S5_SCAN_EXPONENTS):
            e = S5_SCAN_EXPONENTS[n]
            mag = jnp.exp(zr * e)
            pr_ref[n] = mag * jnp.cos(zi * e)
            pi_ref[n] = mag * jnp.sin(zi * e)
        else:
            pr_ref[n] = jnp.zeros_like(zr)
            pi_ref[n] = jnp.zeros_like(zr)


def _s5_pow(lam_re, lam_im, log_step):
    g, p = lam_re.shape
    slots = -(-len(S5_SCAN_EXPONENTS) // F32_SUBLANES) * F32_SUBLANES
    out = jax.ShapeDtypeStruct((slots, g, p), F32)
    return pl.pallas_call(_s5_pow_kernel, out_shape=[out, out], name="s5_pow")(
        lam_re, lam_im, log_step.reshape(g, 1))


def _s5_asm_kernel(lr_ref, li_ref, ls_ref, cr_ref, ci_ref, btr_ref, bti_ref, br_ref, bi_ref, d_ref,
                   toep_ref, wout_ref, wst_ref, dcol_ref):
    def group(i, carry):
        _s5_asm_group(i, lr_ref, li_ref, ls_ref, cr_ref, ci_ref, btr_ref, bti_ref, br_ref, bi_ref,
                      toep_ref, wout_ref, wst_ref)
        dcol_ref[i] = jnp.concatenate([d_ref[i]] * S5_CHUNK, axis=0)
        return carry

    lax.fori_loop(0, cr_ref.shape[0], group, 0)


def _dot_3pass_tiled(a, b):
    c = b.shape[1]
    rep = ((lax.broadcasted_iota(jnp.int32, (c, S5_CHUNK * c), 1) & (c - 1))
           == lax.broadcasted_iota(jnp.int32, (c, S5_CHUNK * c), 0)).astype(BF16)
    dot = functools.partial(jnp.dot, preferred_element_type=F32)
    a_hi, b_hi = a.astype(BF16), b.astype(BF16)
    a_lo = (a - a_hi.astype(F32)).astype(BF16)
    b_lo = (b - b_hi.astype(F32)).astype(BF16)
    b_hi, b_lo = dot(b_hi, rep).astype(BF16), dot(b_lo, rep).astype(BF16)
    return dot(a_hi, b_hi) + (dot(a_hi, b_lo) + dot(a_lo, b_hi))


def _s5_asm_group(i, lr_ref, li_ref, ls_ref, cr_ref, ci_ref, btr_ref, bti_ref, br_ref, bi_ref,
                  toep_ref, wout_ref, wst_ref):
    t = S5_CHUNK
    lr, li = lr_ref[i], li_ref[i]
    step = jnp.exp(ls_ref[i])
    zr, zi = lr * step, li * step
    rows = -(-(t + 1) // F32_SUBLANES) * F32_SUBLANES
    e = lax.broadcasted_iota(jnp.int32, (rows, lr.shape[1]), 0).astype(F32)
    mag = jnp.exp(e * zr)
    pr, pi = mag * jnp.cos(e * zi), mag * jnp.sin(e * zi)
    den = lr * lr + li * li
    mr = ((pr[1:2] - 1.0) * lr + pi[1:2] * li) / den
    mi = (pi[1:2] * lr - (pr[1:2] - 1.0) * li) / den
    cr, ci = cr_ref[i], ci_ref[i]
    btr, bti = btr_ref[i], bti_ref[i]
    amr = pr[:t] * mr - pi[:t] * mi
    ami = pr[:t] * mi + pi[:t] * mr
    l_re, l_im, w_re, w_im, o_re, o_im = [], [], [], [], [], []
    for k in range(t):
        ar, ai = amr[k:k + 1], ami[k:k + 1]
        l_re.append(cr * ar - ci * ai)
        l_im.append(-(cr * ai + ci * ar))
        ar, ai = amr[t - 1 - k:t - k], ami[t - 1 - k:t - k]
        w_re.append(btr * ar - bti * ai)
        w_im.append(btr * ai + bti * ar)
        ar, ai = pr[k + 1:k + 2], pi[k + 1:k + 2]
        o_re.append(cr * ar - ci * ai)
        o_im.append(-(cr * ai + ci * ar))
    cat = lambda parts: jnp.concatenate(parts, axis=0)
    kt = _dot_3pass_tiled(jnp.concatenate([cat(l_re), cat(l_im)], axis=1),
                          jnp.concatenate([br_ref[i], bi_ref[i]], axis=0))
    n = kt.shape[0]
    blk = lax.shift_right_logical(lax.broadcasted_iota(jnp.int32, kt.shape, 1), int(math.log2(S5_GROUP_CH)))
    toep = jnp.where(blk == 0, kt, 0.0)
    for s in range(1, t):
        shifted = jnp.concatenate([jnp.zeros((s * S5_GROUP_CH, n), F32), kt[:n - s * S5_GROUP_CH]], axis=0)
        toep = jnp.where(blk == s, shifted, toep)
    toep_ref[i] = toep.astype(BF16)
    wout_ref[i] = jnp.concatenate([cat(o_re), cat(o_im)], axis=1).astype(BF16)
    wst_ref[i] = jnp.concatenate([cat(w_re), cat(w_im)], axis=1).T.astype(BF16)


def _s5_asm(lam_re, lam_im, log_step, c_re, c_im, b_re, b_im, d):
    g, c, p = c_re.shape
    tc = S5_CHUNK * c
    gs = 8
    blk = lambda a: pl.BlockSpec((gs,) + a.shape[1:], lambda i: (i, 0, 0))
    bt_re, bt_im = b_re.transpose(0, 2, 1), b_im.transpose(0, 2, 1)
    args = (lam_re.reshape(g, 1, p), lam_im.reshape(g, 1, p), log_step.reshape(g, 1, 1),
            c_re, c_im, bt_re, bt_im, b_re, b_im, d.reshape(g, c, 1))
    return pl.pallas_call(
        _s5_asm_kernel,
        grid=(g // gs,),
        in_specs=[blk(a) for a in args],
        out_specs=[pl.BlockSpec((gs, tc, tc), lambda i: (i, 0, 0)),
                   pl.BlockSpec((gs, tc, 2 * p), lambda i: (i, 0, 0)),
                   pl.BlockSpec((gs, 2 * p, tc), lambda i: (i, 0, 0)),
                   pl.BlockSpec((gs, tc, 1), lambda i: (i, 0, 0))],
        out_shape=[jax.ShapeDtypeStruct((g, tc, tc), BF16), jax.ShapeDtypeStruct((g, tc, 2 * p), BF16),
                   jax.ShapeDtypeStruct((g, 2 * p, tc), BF16), jax.ShapeDtypeStruct((g, tc, 1), F32)],
        compiler_params=_cparams(("parallel",)),
        name="s5_asm",
    )(*args)


def _s5_ut_kernel(xn_ref, w_ref, o_ref, wt_ref):
    @pl.when(pl.program_id(0) == 0)
    def _():
        wt_ref[...] = w_ref[...].T.astype(BF16)

    nb, nc, d = xn_ref.shape
    ut = lax.dot_general(wt_ref[...], xn_ref[...].reshape(nb * nc, d), (((1,), (1,)), ((), ())),
                         preferred_element_type=F32)
    o_ref[...] = ut.astype(BF16).reshape(o_ref.shape)


def _s5_ut(xn, w_in, j):
    b, nc, td = xn.shape
    d = td // S5_CHUNK
    return pl.pallas_call(
        _s5_ut_kernel,
        grid=(S5_CHUNK,),
        in_specs=[pl.BlockSpec((b, nc, d), lambda s: (0, 0, s)),
                  pl.BlockSpec(*_layer_block(w_in, j, (d, PRIMARY_WIDTH)))],
        out_specs=pl.BlockSpec((S5_GROUPS, S5_GROUP_CH, b * nc), lambda s: (0, s, 0)),
        out_shape=jax.ShapeDtypeStruct((S5_GROUPS, S5_CHUNK * S5_GROUP_CH, b * nc), BF16),
        scratch_shapes=[pltpu.VMEM((PRIMARY_WIDTH, d), BF16)],
        compiler_params=_cparams(("arbitrary",)),
        name="s5_ut",
    )(xn, w_in)


def _s5_mix_kernel(x_ref, toep_ref, wst_ref, wout_ref, sr_ref, si_ref, d_ref, o_ref, *, nb):
    p = S5_STATE
    gs = x_ref.shape[0]
    lane = lax.broadcasted_iota(jnp.int32, (p, LANES), 1)
    n_steps = int(math.log2(LANES))

    def scan(g, hloc):
        pw = []
        for i in range(n_steps):
            keep = lane >= (1 << i)
            pw.append((jnp.where(keep, jnp.broadcast_to(sr_ref[g, :, i:i + 1], (p, LANES)), 0.0),
                       jnp.where(keep, jnp.broadcast_to(si_ref[g, :, i:i + 1], (p, LANES)), 0.0)))
        h_re = [hloc[:p, b * LANES:(b + 1) * LANES] for b in range(nb)]
        h_im = [hloc[p:, b * LANES:(b + 1) * LANES] for b in range(nb)]
        for i in range(n_steps):
            ar, ai = pw[i]
            r_sh = [pltpu.roll(v, 1 << i, 1) for v in h_re]
            i_sh = [pltpu.roll(v, 1 << i, 1) for v in h_im]
            h_re = [h_re[b] + ar * r_sh[b] - ai * i_sh[b] for b in range(nb)]
            h_im = [h_im[b] + ar * i_sh[b] + ai * r_sh[b] for b in range(nb)]
        h_re = [jnp.where(lane >= 1, pltpu.roll(v, 1, 1), 0.0) for v in h_re]
        h_im = [jnp.where(lane >= 1, pltpu.roll(v, 1, 1), 0.0) for v in h_im]
        return jnp.concatenate([jnp.concatenate(h_re, axis=1), jnp.concatenate(h_im, axis=1)],
                               axis=0).astype(BF16)

    def outputs(g, h):
        x = x_ref[g]
        y = (jnp.dot(toep_ref[g], x, preferred_element_type=F32)
             + jnp.dot(wout_ref[g], h, preferred_element_type=F32)
             + d_ref[g] * x.astype(F32))
        o_ref[:, g * S5_GROUP_CH:(g + 1) * S5_GROUP_CH, :] = (
            jax.nn.gelu(y).astype(o_ref.dtype).reshape(S5_CHUNK, S5_GROUP_CH, y.shape[1]))

    h_prev = None
    for g in range(gs):
        hloc = jnp.dot(wst_ref[g], x_ref[g], preferred_element_type=F32)
        h = scan(g, hloc)
        if h_prev is not None:
            outputs(g - 1, h_prev)
        h_prev = h
    outputs(gs - 1, h_prev)


def _s5_mix(xg, toep, wst, wout, sc_re, sc_im, dcol, *, nb, gs):
    g, tc, cols = xg.shape
    assert cols == nb * LANES, "one batch's chunks must fill exactly one 128-lane block"
    blk = lambda a: pl.BlockSpec((gs,) + a.shape[1:], lambda i: (i, 0, 0))
    return pl.pallas_call(
        functools.partial(_s5_mix_kernel, nb=nb),
        grid=(g // gs,),
        in_specs=[blk(a) for a in (xg, toep, wst, wout, sc_re, sc_im, dcol)],
        out_specs=pl.BlockSpec((S5_CHUNK, gs * S5_GROUP_CH, cols), lambda i: (0, i, 0)),
        out_shape=jax.ShapeDtypeStruct((S5_CHUNK, g * S5_GROUP_CH, cols), BF16),
        compiler_params=_cparams(("parallel",)),
        name="s5_mix",
    )(xg, toep, wst, wout, sc_re, sc_im, dcol)


def _glu_kernel(y_ref, w_ref, o_ref, *, col_chunk):
    y = y_ref[0].T
    half = o_ref.shape[-1]
    for c in range(half // col_chunk):
        wa = w_ref[:, c * col_chunk:(c + 1) * col_chunk].astype(BF16)
        wg = w_ref[:, half + c * col_chunk:half + (c + 1) * col_chunk].astype(BF16)
        a = jnp.dot(y, wa, preferred_element_type=F32)
        g = jnp.dot(y, wg, preferred_element_type=F32)
        o_ref[:, :, c * col_chunk:(c + 1) * col_chunk] = (
            (a * jax.nn.sigmoid(g)).astype(o_ref.dtype).reshape(o_ref.shape[:2] + (col_chunk,)))


def _glu(yt, w, layer, *, nb, col_chunk):
    t, k, cols = yt.shape
    nc = cols // nb
    half = w.shape[2] // 2
    return pl.pallas_call(
        functools.partial(_glu_kernel, col_chunk=col_chunk),
        grid=(t,),
        in_specs=[pl.BlockSpec((1, k, cols), lambda j: (j, 0, 0)),
                  pl.BlockSpec(*_layer_block(w, layer), pipeline_mode=pl.Buffered(1))],
        out_specs=pl.BlockSpec((nb, nc, half), lambda j: (0, 0, j)),
        out_shape=jax.ShapeDtypeStruct((nb, nc, t * half), BF16),
        compiler_params=_cparams(("parallel",)),
        name="glu",
    )(yt, w)


def _merge_kernel(x_ref, mix_ref, xq_ref, gate_ref, k_ref, v_ref, qg_ref, w_ref, o_ref, *cat_refs, tm, phased):
    scale = X_HEAD_DIM ** -0.5
    nc = tm // S5_CHUNK
    w = w_ref[...].astype(BF16)

    def gather(k, cat_ref):
        r0 = k * tm
        gate = gate_ref[0, r0:r0 + tm, :]
        sg = gate * jax.nn.sigmoid(gate)
        if phased:
            for s in range(S5_CHUNK):
                rows = slice(s * nc, (s + 1) * nc)
                mix = mix_ref[0, k * nc:(k + 1) * nc, s * PRIMARY_WIDTH:(s + 1) * PRIMARY_WIDTH]
                cat_ref[rows, :PRIMARY_WIDTH] = mix * sg[rows, :PRIMARY_WIDTH]
        else:
            cat_ref[:, :PRIMARY_WIDTH] = mix_ref[0, r0:r0 + tm, :] * sg[:, :PRIMARY_WIDTH]
        for h in range(X_HEADS):
            sl = slice(h * X_HEAD_DIM, (h + 1) * X_HEAD_DIM)
            q = _rms(xq_ref[0, r0:r0 + tm, sl].astype(F32), qg_ref[...]).astype(BF16)
            s = lax.dot_general(q, k_ref[0, :, sl], (((1,), (1,)), ((), ())), preferred_element_type=F32) * scale
            p = jnp.exp(s - jnp.max(s, axis=-1, keepdims=True))
            p = (p / jnp.sum(p, axis=-1, keepdims=True)).astype(BF16)
            mo = jnp.dot(p, v_ref[0, :, sl], preferred_element_type=F32)
            osl = slice(PRIMARY_WIDTH + h * X_HEAD_DIM, PRIMARY_WIDTH + (h + 1) * X_HEAD_DIM)
            cat_ref[:, osl] = mo.astype(BF16) * sg[:, osl]

    def project(k, cat_ref):
        r0 = k * tm
        delta = jnp.dot(cat_ref[...], w, preferred_element_type=F32)
        o_ref[0, r0:r0 + tm, :] = x_ref[0, r0:r0 + tm, :] + (_from_phase_order(delta) if phased else delta)

    n_sub = len(cat_refs)
    for k in range(n_sub):
        gather(k, cat_refs[k])
        if k > 0:
            project(k - 1, cat_refs[k - 1])
    project(n_sub - 1, cat_refs[n_sub - 1])


def _merge(x, mix, proj, xq_blk, gate_blk, mk, mv, xq_norm, w_out, layer, *, tm, phased=False):
    b, l, d = x.shape
    m = mk.shape[2]
    n_sub = 2
    tg = n_sub * tm
    mix_spec = (pl.BlockSpec((1, tg // S5_CHUNK, S5_CHUNK * PRIMARY_WIDTH), lambda i, j: (i, j, 0)) if phased
                else pl.BlockSpec((1, tg, PRIMARY_WIDTH), lambda i, j: (i, j, 0)))
    return pl.pallas_call(
        functools.partial(_merge_kernel, tm=tm, phased=phased),
        grid=(b, l // tg),
        in_specs=[pl.BlockSpec((1, tg, d), lambda i, j: (i, j, 0)),
                  mix_spec,
                  pl.BlockSpec((1, tg, XQ_WIDTH), lambda i, j: (i, j, xq_blk)),
                  pl.BlockSpec((1, tg, BRANCH_WIDTH), lambda i, j: (i, j, gate_blk)),
                  pl.BlockSpec((None, 1, m, XQ_WIDTH), lambda i, j: (layer, i, 0, 0)),
                  pl.BlockSpec((None, 1, m, XQ_WIDTH), lambda i, j: (layer, i, 0, 0)),
                  pl.BlockSpec((None, 1, X_HEAD_DIM), lambda i, j: (layer, 0, 0)),
                  pl.BlockSpec((None, BRANCH_WIDTH, d), lambda i, j: (layer, 0, 0),
                               pipeline_mode=pl.Buffered(1))],
        out_specs=pl.BlockSpec((1, tg, d), lambda i, j: (i, j, 0)),
        out_shape=jax.ShapeDtypeStruct((b, l, d), F32),
        scratch_shapes=[pltpu.VMEM((tm, BRANCH_WIDTH), BF16)] * n_sub,
        compiler_params=_cparams(("parallel", "parallel")),
        name="merge",
    )(x, mix, proj, proj, mk, mv, xq_norm.reshape(-1, 1, X_HEAD_DIM), w_out)


def _mla_qkv_kernel(cq_ref, ckv_ref, kr_ref, posr_ref, invfc_ref, gq_ref, gkv_ref, gqn_ref,
                    gkn_ref, gqr_ref, gkr_ref, wuq_ref, wukv_ref, qt_ref, kn_ref, krope_ref, vt_ref,
                    wqt_ref, wk_ref, wvt_ref):
    half = MLA_ROPE // 2
    tm = cq_ref.shape[1]
    qk = MLA_NOPE + MLA_ROPE

    @pl.when((pl.program_id(0) == 0) & (pl.program_id(1) == 0))
    def _():
        wqt_ref[...] = wuq_ref[...].T.astype(BF16)
        for h in range(MLA_HEADS):
            c0 = h * (MLA_NOPE + MLA_V)
            wk_ref[:, h * MLA_NOPE:(h + 1) * MLA_NOPE] = wukv_ref[:, c0:c0 + MLA_NOPE].astype(BF16)
            wvt_ref[h * MLA_V:(h + 1) * MLA_V, :] = wukv_ref[:, c0 + MLA_NOPE:c0 + MLA_NOPE + MLA_V].T.astype(BF16)

    qscale = (MLA_NOPE + MLA_ROPE) ** -0.5 * math.log2(math.e)

    cq = _rms(cq_ref[0].astype(F32), gq_ref[...])
    ckv = _rms(ckv_ref[0].astype(F32), gkv_ref[...])
    cq_t = cq.T.astype(BF16)
    ckv_t = ckv.T.astype(BF16)
    ckv_b = ckv.astype(BF16)

    def project(h):
        dot = functools.partial(jnp.dot, preferred_element_type=F32)
        k_pair = dot(ckv_b, wk_ref[:, h * MLA_NOPE:(h + 2) * MLA_NOPE]) if h % 2 == 0 else None
        return (dot(wqt_ref[h * qk:(h + 1) * qk, :], cq_t),
                dot(wvt_ref[h * MLA_V:(h + 1) * MLA_V, :], ckv_t), k_pair)

    ang_t = invfc_ref[...] * posr_ref[0].astype(F32)
    cos_t, sin_t = jnp.cos(ang_t), jnp.sin(ang_t)
    g_nope = jnp.broadcast_to(gqn_ref[...], (MLA_NOPE, tm)) * qscale
    g_r1 = jnp.broadcast_to(gqr_ref[:half, :], (half, tm)) * qscale
    g_r2 = jnp.broadcast_to(gqr_ref[half:, :], (half, tm)) * qscale
    ahead = 2
    pending = [project(h) for h in range(ahead)]
    for h in range(MLA_HEADS):
        if h + ahead < MLA_HEADS:
            pending.append(project(h + ahead))
        q, v_t, _ = pending[h]
        k_n = pending[h - h % 2][2][:, (h % 2) * MLA_NOPE:(h % 2 + 1) * MLA_NOPE]
        nope = q[:MLA_NOPE]
        r = lax.rsqrt(jnp.mean(nope * nope, axis=0, keepdims=True) + EPS)
        qt_ref[0, h, :MLA_NOPE, :] = (nope * r * g_nope).astype(BF16)
        x1, x2 = q[MLA_NOPE:MLA_NOPE + half], q[MLA_NOPE + half:MLA_NOPE + MLA_ROPE]
        ss = jnp.sum(x1 * x1, axis=0, keepdims=True) + jnp.sum(x2 * x2, axis=0, keepdims=True)
        r = lax.rsqrt(ss * (1.0 / MLA_ROPE) + EPS)
        x1, x2 = x1 * r * g_r1, x2 * r * g_r2
        qt_ref[0, h, MLA_NOPE:MLA_NOPE + half, :] = (x1 * cos_t - x2 * sin_t).astype(BF16)
        qt_ref[0, h, MLA_NOPE + half:MLA_NOPE + MLA_ROPE, :] = (x1 * sin_t + x2 * cos_t).astype(BF16)
        qt_ref[0, h, MLA_NOPE + MLA_ROPE:, :] = jnp.zeros((MLA_QK_PAD - MLA_NOPE - MLA_ROPE, tm), BF16)
        kn_ref[0, h] = _rms(k_n, gkn_ref[...]).astype(BF16)
        vt_ref[0, h, :MLA_V, :] = v_t.astype(BF16)
        vt_ref[0, h, MLA_V:, :] = jnp.ones((MLA_VL - MLA_V, tm), BF16)

    kr_t = kr_ref[0].astype(F32).T
    x1, x2 = kr_t[:half], kr_t[half:MLA_ROPE]
    ss = jnp.sum(x1 * x1, axis=0, keepdims=True) + jnp.sum(x2 * x2, axis=0, keepdims=True)
    r = lax.rsqrt(ss * (1.0 / MLA_ROPE) + EPS)
    x1, x2 = x1 * r * gkr_ref[:half, :], x2 * r * gkr_ref[half:, :]
    rot = jnp.concatenate([x1 * cos_t - x2 * sin_t, x1 * sin_t + x2 * cos_t,
                           jnp.zeros((LANES - MLA_ROPE, tm), F32)], axis=0)
    krope_ref[0] = rot.T.astype(BF16)


def _mla_qkv(proj, cq_blk, ckv_blk, kr_blk, positions, gains, w_uq, w_ukv, layer, *, tm):
    b, l, _ = proj.shape
    hh = MLA_HEADS
    half = MLA_ROPE // 2
    inv_freq = ROPE_THETA ** (-jnp.arange(half, dtype=F32) / half)
    const = lambda a: pl.BlockSpec(a.shape, lambda i, j: (0,) * a.ndim)
    gq, gkv, gqn, gkn, gqr, gkr = gains
    consts = [inv_freq.reshape(half, 1), gq.reshape(1, -1), gkv.reshape(1, -1), gqn.reshape(-1, 1),
              gkn.reshape(1, -1), gqr.reshape(-1, 1), gkr.reshape(-1, 1)]
    weights = [w_uq, w_ukv]
    return pl.pallas_call(
        _mla_qkv_kernel,
        grid=(b, l // tm),
        in_specs=[pl.BlockSpec((1, tm, MLA_Q_LORA), lambda i, j: (i, j, cq_blk)),
                  pl.BlockSpec((1, tm, MLA_KV_LORA), lambda i, j: (i, j, ckv_blk)),
                  pl.BlockSpec((1, tm, LANES), lambda i, j: (i, j, kr_blk)),
                  pl.BlockSpec((1, 1, tm), lambda i, j: (i, 0, j))] + [const(a) for a in consts]
                 + [pl.BlockSpec(*_layer_block(w, layer)) for w in weights],
        out_specs=[pl.BlockSpec((1, hh, MLA_QK_PAD, tm), lambda i, j: (i, 0, 0, j)),
                   pl.BlockSpec((1, hh, tm, MLA_NOPE), lambda i, j: (i, 0, j, 0)),
                   pl.BlockSpec((1, tm, LANES), lambda i, j: (i, j, 0)),
                   pl.BlockSpec((1, hh, MLA_VL, tm), lambda i, j: (i, 0, 0, j))],
        out_shape=[jax.ShapeDtypeStruct((b, hh, MLA_QK_PAD, l), BF16),
                   jax.ShapeDtypeStruct((b, hh, l, MLA_NOPE), BF16),
                   jax.ShapeDtypeStruct((b, l, LANES), BF16),
                   jax.ShapeDtypeStruct((b, hh, MLA_VL, l), BF16)],
        scratch_shapes=[pltpu.VMEM((hh * (MLA_NOPE + MLA_ROPE), MLA_Q_LORA), BF16),
                        pltpu.VMEM((MLA_KV_LORA, hh * MLA_NOPE), BF16),
                        pltpu.VMEM((hh * MLA_V, MLA_KV_LORA), BF16)],
        compiler_params=_cparams(("arbitrary", "arbitrary")),
        name="mla_qkv",
    )(proj, proj, proj, positions.reshape(b, 1, l), *consts, *weights)


def _flash_kernel(qt_ref, kn_ref, kr_ref, vt_ref, o_ref, m_ref, acc_ref, *, tq, hp, ahead_full, ahead_diagonal):
    qi = pl.program_id(2)
    m_ref[...] = jnp.full(m_ref.shape, -jnp.inf, F32)
    acc_ref[...] = jnp.zeros(acc_ref.shape, F32)

    half = tq // 2
    lower = (lax.broadcasted_iota(jnp.int32, (half, half), 0)
             <= lax.broadcasted_iota(jnp.int32, (half, half), 1))

    def blocks(j, parts, diagonal):
        ahead = ahead_diagonal if diagonal else ahead_full
        base = pl.multiple_of(j * tq, tq)
        items = [(h, pl.ds(base + k0, nk), slice(q0, q0 + nq)) for k0, nk, q0, nq in parts for h in range(hp)]

        def scores(h, rows, cols):
            k = jnp.concatenate([kn_ref[0, h, rows, :], kr_ref[0, rows, :]], axis=-1)
            return jnp.dot(k, qt_ref[0, h, :, cols], preferred_element_type=F32)

        pending = [scores(*it) for it in items[:ahead]]
        for n, (h, rows, cols) in enumerate(items):
            if n + ahead < len(items):
                pending.append(scores(*items[n + ahead]))
            s = pending[n]
            if diagonal:
                square = jnp.where(lower, s[:, :half], jnp.finfo(F32).min)
                s = square if s.shape[1] == half else jnp.concatenate([square, s[:, half:]], axis=1)
            m = m_ref[h, :, cols]
            m_new = jnp.maximum(m, jnp.max(s, axis=0, keepdims=True))
            alpha = jnp.exp2(m - m_new)
            p = jnp.exp2(s - m_new)
            acc_ref[h, :, cols] = alpha * acc_ref[h, :, cols] + jnp.dot(
                vt_ref[0, h, :, rows], p.astype(BF16), preferred_element_type=F32)
            m_ref[h, :, cols] = m_new

    def body(j, carry):
        blocks(j, [(0, tq, 0, tq)], False)
        return carry

    lax.fori_loop(0, qi, body, 0)
    blocks(qi, [(0, half, 0, tq), (half, half, half, half)], True)
    for h in range(hp):
        o_ref[0, :, h * MLA_V:(h + 1) * MLA_V] = (
            acc_ref[h, :MLA_V, :] / acc_ref[h, MLA_V:MLA_V + 1, :]).T.astype(o_ref.dtype)


def _flash(qt, kn, kr, vt, *, tq, hp, ahead_full, ahead_diagonal):
    b, hh, _, l = qt.shape
    return pl.pallas_call(
        functools.partial(_flash_kernel, tq=tq, hp=hp, ahead_full=ahead_full, ahead_diagonal=ahead_diagonal),
        grid=(b, hh // hp, l // tq),
        in_specs=[pl.BlockSpec((1, hp, MLA_QK_PAD, tq), lambda i, h, j: (i, h, 0, j)),
                  pl.BlockSpec((1, hp, l, MLA_NOPE), lambda i, h, j: (i, h, 0, 0)),
                  pl.BlockSpec((1, l, LANES), lambda i, h, j: (i, 0, 0)),
                  pl.BlockSpec((1, hp, MLA_VL, l), lambda i, h, j: (i, h, 0, 0))],
        out_specs=pl.BlockSpec((1, tq, hp * MLA_V), lambda i, h, j: (i, j, h)),
        out_shape=jax.ShapeDtypeStruct((b, l, hh * MLA_V), BF16),
        scratch_shapes=[pltpu.VMEM((hp, 1, tq), F32), pltpu.VMEM((hp, MLA_VL, tq), F32)],
        compiler_params=_cparams(("parallel", "parallel", "parallel")),
        name="flash",
    )(qt, kn, kr, vt)


def _s5_layer(x, ln, w_in, lam_re, lam_im, log_step, b_re, b_im, c_re, c_im, d, w_glu,
              w_out, mem_kv, xq_norm, layer, j):
    b, l, dm = x.shape
    tm = 512
    proj, xn = _s5_in_proj(x, ln, w_in, j, tm=tm, col_chunk=512)
    xg = _s5_ut(xn, w_in, j)
    toep, wout, wst, dcol = _s5_asm(lam_re, lam_im, log_step, c_re, c_im, b_re, b_im, d)
    pw_re, pw_im = _s5_pow(lam_re, lam_im, log_step)
    col = lambda pw: pw.transpose(1, 2, 0)
    yt = _s5_mix(xg, toep, wst, wout, col(pw_re), col(pw_im), dcol, nb=b, gs=8)
    y = _glu(yt, w_glu, j, nb=b, col_chunk=256)
    return _merge(x, y, proj, BRANCH_WIDTH // XQ_WIDTH, 0, *mem_kv, xq_norm, w_out, layer, tm=tm, phased=True)


def _mla_layer(x, positions, ln, w_in, q_lora_norm, kv_lora_norm, w_uq, w_ukv, q_nope_norm, k_nope_norm,
               q_rope_norm, k_rope_norm, w_out, mem_kv, xq_norm, layer, j):
    b, l, dm = x.shape
    o1 = MLA_Q_LORA
    o2 = o1 + MLA_KV_LORA
    o3 = o2 + MLA_ROPE
    o4 = o3 + XQ_WIDTH
    segments = ((o4, BRANCH_WIDTH), (0, o1), (o3, XQ_WIDTH), (o1, MLA_KV_LORA), (o2, MLA_ROPE))
    wout = -(-(o4 + BRANCH_WIDTH) // 512) * 512
    proj = _mla_in_proj(x.reshape(b * l, dm), ln, w_in, j, segments, wout, tm=512, col_chunk=512)
    proj = proj.reshape(b, l, -1)
    gate_blk = 0
    cq_blk = BRANCH_WIDTH // MLA_Q_LORA
    xq_blk = (BRANCH_WIDTH + MLA_Q_LORA) // XQ_WIDTH
    ckv_blk = (BRANCH_WIDTH + MLA_Q_LORA + XQ_WIDTH) // MLA_KV_LORA
    kr_blk = (BRANCH_WIDTH + MLA_Q_LORA + XQ_WIDTH + MLA_KV_LORA) // LANES
    qt, kn, kr, vt = _mla_qkv(proj, cq_blk, ckv_blk, kr_blk, positions,
                              (q_lora_norm, kv_lora_norm, q_nope_norm, k_nope_norm, q_rope_norm, k_rope_norm),
                              w_uq, w_ukv, j, tm=512)
    attn = _flash(qt, kn, kr, vt, tq=512, hp=12, ahead_full=2, ahead_diagonal=4)
    return _merge(x, attn, proj, xq_blk, gate_blk, *mem_kv, xq_norm, w_out, layer, tm=512)


def kernel(x, mem, positions, ln_gain, w_out, mem_norm, w_mem_kv, xq_norm, xk_norm,
           s5_w_in, s5_lambda_re, s5_lambda_im, s5_log_step, s5_b_re, s5_b_im, s5_c_re, s5_c_im,
           s5_d, s5_w_glu, mla_w_in, mla_q_lora_norm, mla_kv_lora_norm, mla_w_uq, mla_w_ukv,
           mla_q_nope_norm, mla_k_nope_norm, mla_q_rope_norm, mla_k_rope_norm):
    depth = ln_gain.shape[0]
    mem_kv = _mem_kv(mem, mem_norm, w_mem_kv, xk_norm)
    for i in range(depth):
        j = i // 2
        if i % 2 == 0:
            x = _s5_layer(x, ln_gain[i], s5_w_in, s5_lambda_re[j], s5_lambda_im[j], s5_log_step[j],
                          s5_b_re[j], s5_b_im[j], s5_c_re[j], s5_c_im[j], s5_d[j], s5_w_glu,
                          w_out, mem_kv, xq_norm, i, j)
        else:
            x = _mla_layer(x, positions, ln_gain[i], mla_w_in, mla_q_lora_norm[j], mla_kv_lora_norm[j],
                           mla_w_uq, mla_w_ukv, mla_q_nope_norm[j], mla_k_nope_norm[j],
                           mla_q_rope_norm[j], mla_k_rope_norm[j],
                           w_out, mem_kv, xq_norm, i, j)
    return x
```

```python
import functools
import math

import jax
import jax.numpy as jnp
from jax import lax
from jax.experimental import pallas as pl
from jax.experimental.pallas import tpu as pltpu

D_MODEL = 1024
BRANCH_WIDTH = 2 * D_MODEL
XQ_WIDTH = BRANCH_WIDTH // 4
PRIMARY_WIDTH = BRANCH_WIDTH - XQ_WIDTH
X_HEADS = 4
X_HEAD_DIM = XQ_WIDTH // X_HEADS
S5_GROUP_CH = 16
S5_GROUPS = PRIMARY_WIDTH // S5_GROUP_CH
S5_STATE = 64
MLA_NOPE = 128
MLA_ROPE = 64
MLA_V = 128
MLA_HEADS = PRIMARY_WIDTH // MLA_V
MLA_Q_LORA = D_MODEL // 2
MLA_KV_LORA = D_MODEL // 4
ROPE_THETA = 10000.0
EPS = 1e-6

LANES = 128
MLA_QK_PAD = 2 * LANES
F32_SUBLANES = 8
BF16_SUBLANES = 16
MLA_VL = MLA_V + BF16_SUBLANES
S5_CHUNK = 2 * LANES // S5_GROUP_CH
S5_SCAN_EXPONENTS = [S5_CHUNK * 2 ** i for i in range(int(math.log2(LANES)))]
VMEM_LIMIT = 56 * 1024 * 1024

F32 = jnp.float32
BF16 = jnp.bfloat16


def _cparams(sem):
    return pltpu.CompilerParams(dimension_semantics=sem, vmem_limit_bytes=VMEM_LIMIT)


def _rms(x, g):
    return x * lax.rsqrt(jnp.mean(x * x, axis=-1, keepdims=True) + EPS) * g


def _layer_block(w, j, block=None, index=None):
    block = tuple(w.shape[1:]) if block is None else block
    index = (0,) * len(block) if index is None else index
    return (None,) + block, lambda *_: (j,) + index


def _mla_in_proj_kernel(x_ref, g_ref, w_ref, o_ref, wp_ref, *, segments, col_chunk):
    @pl.when(pl.program_id(0) == 0)
    def _():
        at = 0
        for start, width in segments:
            wp_ref[:, at:at + width] = w_ref[:, start:start + width].astype(BF16)
            at += width
        wp_ref[:, at:] = jnp.zeros((wp_ref.shape[0], wp_ref.shape[1] - at), BF16)

    xn = _rms(x_ref[...], g_ref[...]).astype(BF16)
    for c in range(o_ref.shape[1] // col_chunk):
        sl = slice(c * col_chunk, (c + 1) * col_chunk)
        o_ref[:, sl] = jnp.dot(xn, wp_ref[:, sl], preferred_element_type=F32).astype(o_ref.dtype)


def _mla_in_proj(x, g, w, j, segments, wout, *, tm, col_chunk):
    n, d = x.shape
    return pl.pallas_call(
        functools.partial(_mla_in_proj_kernel, segments=segments, col_chunk=col_chunk),
        grid=(n // tm,),
        in_specs=[pl.BlockSpec((tm, d), lambda i: (i, 0)),
                  pl.BlockSpec((1, d), lambda i: (0, 0)),
                  pl.BlockSpec(*_layer_block(w, j), pipeline_mode=pl.Buffered(1))],
        out_specs=pl.BlockSpec((tm, wout), lambda i: (i, 0)),
        out_shape=jax.ShapeDtypeStruct((n, wout), BF16),
        scratch_shapes=[pltpu.VMEM((d, wout), BF16)],
        compiler_params=_cparams(("arbitrary",)),
        name="mla_in_proj",
    )(x, g.reshape(1, d), w)


def _to_phase_order(a):
    n, d = a.shape
    return jnp.swapaxes(a.reshape(n // S5_CHUNK, S5_CHUNK, d), 0, 1).reshape(n, d)


def _from_phase_order(a):
    n, d = a.shape
    return jnp.swapaxes(a.reshape(S5_CHUNK, n // S5_CHUNK, d), 0, 1).reshape(n, d)


def _s5_in_proj_kernel(x_ref, g_ref, wg_ref, wx_ref, o_ref, xn_ref, *, tm, col_chunk):
    d = x_ref.shape[2]
    nc = tm // S5_CHUNK
    chunks = [slice(c * col_chunk, (c + 1) * col_chunk) for c in range(BRANCH_WIDTH // col_chunk)]
    weights = [wg_ref[:, sl].astype(BF16) for sl in chunks] + [wx_ref[...].astype(BF16)]
    chunks.append(slice(BRANCH_WIDTH, BRANCH_WIDTH + XQ_WIDTH))

    def normalise(k):
        xn = _rms(_to_phase_order(x_ref[0, k * tm:(k + 1) * tm, :]), g_ref[...]).astype(BF16)
        for s in range(S5_CHUNK):
            xn_ref[0, k * nc:(k + 1) * nc, s * d:(s + 1) * d] = xn[s * nc:(s + 1) * nc]
        return xn

    def project(k, xn):
        for sl, w in zip(chunks, weights):
            o_ref[0, k * tm:(k + 1) * tm, sl] = jnp.dot(xn, w, preferred_element_type=F32).astype(o_ref.dtype)

    n_sub = x_ref.shape[1] // tm
    prev = None
    for k in range(n_sub):
        if prev is not None:
            project(k - 1, prev)
        prev = normalise(k)
    project(n_sub - 1, prev)


def _s5_in_proj(x, g, w_in, j, *, tm, col_chunk):
    b, l, d = x.shape
    wout = BRANCH_WIDTH + XQ_WIDTH
    tg = 2 * tm
    nc = tg // S5_CHUNK
    return pl.pallas_call(
        functools.partial(_s5_in_proj_kernel, tm=tm, col_chunk=col_chunk),
        grid=(b, l // tg),
        in_specs=[pl.BlockSpec((1, tg, d), lambda i, j: (i, j, 0)),
                  pl.BlockSpec((1, d), lambda i, j: (0, 0)),
                  pl.BlockSpec(*_layer_block(w_in, j, (d, BRANCH_WIDTH),
                                             (0, (PRIMARY_WIDTH + XQ_WIDTH) // BRANCH_WIDTH)),
                               pipeline_mode=pl.Buffered(1)),
                  pl.BlockSpec(*_layer_block(w_in, j, (d, XQ_WIDTH), (0, PRIMARY_WIDTH // XQ_WIDTH)),
                               pipeline_mode=pl.Buffered(1))],
        out_specs=[pl.BlockSpec((1, tg, wout), lambda i, j: (i, j, 0)),
                   pl.BlockSpec((1, nc, S5_CHUNK * d), lambda i, j: (i, j, 0))],
        out_shape=[jax.ShapeDtypeStruct((b, l, wout), BF16),
                   jax.ShapeDtypeStruct((b, l // S5_CHUNK, S5_CHUNK * d), BF16)],
        compiler_params=_cparams(("parallel", "parallel")),
        name="s5_in_proj",
    )(x, g.reshape(1, d), w_in, w_in)


def _mem_kv_kernel(m_ref, g_ref, w_ref, kg_ref, k_ref, v_ref):
    b, m, d = m_ref.shape
    w = w_ref[0].astype(BF16)
    for i in range(b):
        mn = _rms(m_ref[i], g_ref[0]).astype(BF16)
        kv = jnp.dot(mn, w, preferred_element_type=F32)
        for h in range(X_HEADS):
            sl = slice(h * X_HEAD_DIM, (h + 1) * X_HEAD_DIM)
            k_ref[0, i, :, sl] = _rms(kv[:, sl], kg_ref[0]).astype(BF16)
        v_ref[0, i] = kv[:, XQ_WIDTH:].astype(BF16)


def _mem_kv(mem, mem_norm, w_mem_kv, xk_norm):
    b, m, d = mem.shape
    depth = w_mem_kv.shape[0]
    out = jax.ShapeDtypeStruct((depth, b, m, XQ_WIDTH), BF16)
    return pl.pallas_call(
        _mem_kv_kernel,
        grid=(depth,),
        in_specs=[pl.BlockSpec((b, m, d), lambda n: (0, 0, 0)),
                  pl.BlockSpec((1, 1, d), lambda n: (n, 0, 0)),
                  pl.BlockSpec((1, d, 2 * XQ_WIDTH), lambda n: (n, 0, 0)),
                  pl.BlockSpec((1, 1, X_HEAD_DIM), lambda n: (n, 0, 0))],
        out_specs=[pl.BlockSpec((1, b, m, XQ_WIDTH), lambda n: (n, 0, 0, 0)),
                   pl.BlockSpec((1, b, m, XQ_WIDTH), lambda n: (n, 0, 0, 0))],
        out_shape=[out, out],
        compiler_params=_cparams(("parallel",)),
        name="mem_kv",
    )(mem, mem_norm.reshape(depth, 1, d), w_mem_kv, xk_norm.reshape(depth, 1, X_HEAD_DIM))


def _s5_pow_kernel(lr_ref, li_ref, ls_ref, pr_ref, pi_ref):
    step = jnp.exp(ls_ref[...])
    zr, zi = lr_ref[...] * step, li_ref[...] * step
    for n in range(pr_ref.shape[0]):
        if n < len(S5_SCAN_EXPONENTS):
            e = S5_SCAN_EXPONENTS[n]
            mag = jnp.exp(zr * e)
            pr_ref[n] = mag * jnp.cos(zi * e)
            pi_ref[n] = mag * jnp.sin(zi * e)
        else:
            pr_ref[n] = jnp.zeros_like(zr)
            pi_ref[n] = jnp.zeros_like(zr)


def _s5_pow(lam_re, lam_im, log_step):
    g, p = lam_re.shape
    slots = -(-len(S5_SCAN_EXPONENTS) // F32_SUBLANES) * F32_SUBLANES
    out = jax.ShapeDtypeStruct((slots, g, p), F32)
    return pl.pallas_call(_s5_pow_kernel, out_shape=[out, out], name="s5_pow")(
        lam_re, lam_im, log_step.reshape(g, 1))


def _s5_asm_kernel(lr_ref, li_ref, ls_ref, cr_ref, ci_ref, btr_ref, bti_ref, br_ref, bi_ref, d_ref,
                   toep_ref, wout_ref, wst_ref, dcol_ref):
    def group(i, carry):
        _s5_asm_group(i, lr_ref, li_ref, ls_ref, cr_ref, ci_ref, btr_ref, bti_ref, br_ref, bi_ref,
                      toep_ref, wout_ref, wst_ref)
        dcol_ref[i] = jnp.concatenate([d_ref[i]] * S5_CHUNK, axis=0)
        return carry

    lax.fori_loop(0, cr_ref.shape[0], group, 0)


def _dot_3pass_tiled(a, b):
    c = b.shape[1]
    rep = ((lax.broadcasted_iota(jnp.int32, (c, S5_CHUNK * c), 1) & (c - 1))
           == lax.broadcasted_iota(jnp.int32, (c, S5_CHUNK * c), 0)).astype(BF16)
    dot = functools.partial(jnp.dot, preferred_element_type=F32)
    a_hi, b_hi = a.astype(BF16), b.astype(BF16)
    a_lo = (a - a_hi.astype(F32)).astype(BF16)
    b_lo = (b - b_hi.astype(F32)).astype(BF16)
    b_hi, b_lo = dot(b_hi, rep).astype(BF16), dot(b_lo, rep).astype(BF16)
    return dot(a_hi, b_hi) + (dot(a_hi, b_lo) + dot(a_lo, b_hi))


def _s5_asm_group(i, lr_ref, li_ref, ls_ref, cr_ref, ci_ref, btr_ref, bti_ref, br_ref, bi_ref,
                  toep_ref, wout_ref, wst_ref):
    t = S5_CHUNK
    lr, li = lr_ref[i], li_ref[i]
    step = jnp.exp(ls_ref[i])
    zr, zi = lr * step, li * step
    rows = -(-(t + 1) // F32_SUBLANES) * F32_SUBLANES
    e = lax.broadcasted_iota(jnp.int32, (rows, lr.shape[1]), 0).astype(F32)
    mag = jnp.exp(e * zr)
    pr, pi = mag * jnp.cos(e * zi), mag * jnp.sin(e * zi)
    den = lr * lr + li * li
    mr = ((pr[1:2] - 1.0) * lr + pi[1:2] * li) / den
    mi = (pi[1:2] * lr - (pr[1:2] - 1.0) * li) / den
    cr, ci = cr_ref[i], ci_ref[i]
    btr, bti = btr_ref[i], bti_ref[i]
    amr = pr[:t] * mr - pi[:t] * mi
    ami = pr[:t] * mi + pi[:t] * mr
    l_re, l_im, w_re, w_im, o_re, o_im = [], [], [], [], [], []
    for k in range(t):
        ar, ai = amr[k:k + 1], ami[k:k + 1]
        l_re.append(cr * ar - ci * ai)
        l_im.append(-(cr * ai + ci * ar))
        ar, ai = amr[t - 1 - k:t - k], ami[t - 1 - k:t - k]
        w_re.append(btr * ar - bti * ai)
        w_im.append(btr * ai + bti * ar)
        ar, ai = pr[k + 1:k + 2], pi[k + 1:k + 2]
        o_re.append(cr * ar - ci * ai)
        o_im.append(-(cr * ai + ci * ar))
    cat = lambda parts: jnp.concatenate(parts, axis=0)
    kt = _dot_3pass_tiled(jnp.concatenate([cat(l_re), cat(l_im)], axis=1),
                          jnp.concatenate([br_ref[i], bi_ref[i]], axis=0))
    n = kt.shape[0]
    blk = lax.shift_right_logical(lax.broadcasted_iota(jnp.int32, kt.shape, 1), int(math.log2(S5_GROUP_CH)))
    toep = jnp.where(blk == 0, kt, 0.0)
    for s in range(1, t):
        shifted = jnp.concatenate([jnp.zeros((s * S5_GROUP_CH, n), F32), kt[:n - s * S5_GROUP_CH]], axis=0)
        toep = jnp.where(blk == s, shifted, toep)
    toep_ref[i] = toep.astype(BF16)
    wout_ref[i] = jnp.concatenate([cat(o_re), cat(o_im)], axis=1).astype(BF16)
    wst_ref[i] = jnp.concatenate([cat(w_re), cat(w_im)], axis=1).T.astype(BF16)


def _s5_asm(lam_re, lam_im, log_step, c_re, c_im, b_re, b_im, d):
    g, c, p = c_re.shape
    tc = S5_CHUNK * c
    gs = 8
    blk = lambda a: pl.BlockSpec((gs,) + a.shape[1:], lambda i: (i, 0, 0))
    bt_re, bt_im = b_re.transpose(0, 2, 1), b_im.transpose(0, 2, 1)
    args = (lam_re.reshape(g, 1, p), lam_im.reshape(g, 1, p), log_step.reshape(g, 1, 1),
            c_re, c_im, bt_re, bt_im, b_re, b_im, d.reshape(g, c, 1))
    return pl.pallas_call(
        _s5_asm_kernel,
        grid=(g // gs,),
        in_specs=[blk(a) for a in args],
        out_specs=[pl.BlockSpec((gs, tc, tc), lambda i: (i, 0, 0)),
                   pl.BlockSpec((gs, tc, 2 * p), lambda i: (i, 0, 0)),
                   pl.BlockSpec((gs, 2 * p, tc), lambda i: (i, 0, 0)),
                   pl.BlockSpec((gs, tc, 1), lambda i: (i, 0, 0))],
        out_shape=[jax.ShapeDtypeStruct((g, tc, tc), BF16), jax.ShapeDtypeStruct((g, tc, 2 * p), BF16),
                   jax.ShapeDtypeStruct((g, 2 * p, tc), BF16), jax.ShapeDtypeStruct((g, tc, 1), F32)],
        compiler_params=_cparams(("parallel",)),
        name="s5_asm",
    )(*args)


def _s5_ut_kernel(xn_ref, w_ref, o_ref, wt_ref):
    @pl.when(pl.program_id(0) == 0)
    def _():
        wt_ref[...] = w_ref[...].T.astype(BF16)

    nb, nc, d = xn_ref.shape
    ut = lax.dot_general(wt_ref[...], xn_ref[...].reshape(nb * nc, d), (((1,), (1,)), ((), ())),
                         preferred_element_type=F32)
    o_ref[...] = ut.astype(BF16).reshape(o_ref.shape)


def _s5_ut(xn, w_in, j):
    b, nc, td = xn.shape
    d = td // S5_CHUNK
    return pl.pallas_call(
        _s5_ut_kernel,
        grid=(S5_CHUNK,),
        in_specs=[pl.BlockSpec((b, nc, d), lambda s: (0, 0, s)),
                  pl.BlockSpec(*_layer_block(w_in, j, (d, PRIMARY_WIDTH)))],
        out_specs=pl.BlockSpec((S5_GROUPS, S5_GROUP_CH, b * nc), lambda s: (0, s, 0)),
        out_shape=jax.ShapeDtypeStruct((S5_GROUPS, S5_CHUNK * S5_GROUP_CH, b * nc), BF16),
        scratch_shapes=[pltpu.VMEM((PRIMARY_WIDTH, d), BF16)],
        compiler_params=_cparams(("arbitrary",)),
        name="s5_ut",
    )(xn, w_in)


def _s5_mix_kernel(x_ref, toep_ref, wst_ref, wout_ref, sr_ref, si_ref, d_ref, o_ref, *, nb):
    p = S5_STATE
    gs = x_ref.shape[0]
    lane = lax.broadcasted_iota(jnp.int32, (p, LANES), 1)
    n_steps = int(math.log2(LANES))

    def scan(g, hloc):
        pw = []
        for i in range(n_steps):
            keep = lane >= (1 << i)
            pw.append((jnp.where(keep, jnp.broadcast_to(sr_ref[g, :, i:i + 1], (p, LANES)), 0.0),
                       jnp.where(keep, jnp.broadcast_to(si_ref[g, :, i:i + 1], (p, LANES)), 0.0)))
        h_re = [hloc[:p, b * LANES:(b + 1) * LANES] for b in range(nb)]
        h_im = [hloc[p:, b * LANES:(b + 1) * LANES] for b in range(nb)]
        for i in range(n_steps):
            ar, ai = pw[i]
            r_sh = [pltpu.roll(v, 1 << i, 1) for v in h_re]
            i_sh = [pltpu.roll(v, 1 << i, 1) for v in h_im]
            h_re = [h_re[b] + ar * r_sh[b] - ai * i_sh[b] for b in range(nb)]
            h_im = [h_im[b] + ar * i_sh[b] + ai * r_sh[b] for b in range(nb)]
        h_re = [jnp.where(lane >= 1, pltpu.roll(v, 1, 1), 0.0) for v in h_re]
        h_im = [jnp.where(lane >= 1, pltpu.roll(v, 1, 1), 0.0) for v in h_im]
        return jnp.concatenate([jnp.concatenate(h_re, axis=1), jnp.concatenate(h_im, axis=1)],
                               axis=0).astype(BF16)

    def outputs(g, h):
        x = x_ref[g]
        y = (jnp.dot(toep_ref[g], x, preferred_element_type=F32)
             + jnp.dot(wout_ref[g], h, preferred_element_type=F32)
             + d_ref[g] * x.astype(F32))
        o_ref[:, g * S5_GROUP_CH:(g + 1) * S5_GROUP_CH, :] = (
            jax.nn.gelu(y).astype(o_ref.dtype).reshape(S5_CHUNK, S5_GROUP_CH, y.shape[1]))

    h_prev = None
    for g in range(gs):
        hloc = jnp.dot(wst_ref[g], x_ref[g], preferred_element_type=F32)
        h = scan(g, hloc)
        if h_prev is not None:
            outputs(g - 1, h_prev)
        h_prev = h
    outputs(gs - 1, h_prev)


def _s5_mix(xg, toep, wst, wout, sc_re, sc_im, dcol, *, nb, gs):
    g, tc, cols = xg.shape
    assert cols == nb * LANES, "one batch's chunks must fill exactly one 128-lane block"
    blk = lambda a: pl.BlockSpec((gs,) + a.shape[1:], lambda i: (i, 0, 0))
    return pl.pallas_call(
        functools.partial(_s5_mix_kernel, nb=nb),
        grid=(g // gs,),
        in_specs=[blk(a) for a in (xg, toep, wst, wout, sc_re, sc_im, dcol)],
        out_specs=pl.BlockSpec((S5_CHUNK, gs * S5_GROUP_CH, cols), lambda i: (0, i, 0)),
        out_shape=jax.ShapeDtypeStruct((S5_CHUNK, g * S5_GROUP_CH, cols), BF16),
        compiler_params=_cparams(("parallel",)),
        name="s5_mix",
    )(xg, toep, wst, wout, sc_re, sc_im, dcol)


def _glu_kernel(y_ref, w_ref, o_ref, *, col_chunk):
    y = y_ref[0].T
    half = o_ref.shape[-1]
    for c in range(half // col_chunk):
        wa = w_ref[:, c * col_chunk:(c + 1) * col_chunk].astype(BF16)
        wg = w_ref[:, half + c * col_chunk:half + (c + 1) * col_chunk].astype(BF16)
        a = jnp.dot(y, wa, preferred_element_type=F32)
        g = jnp.dot(y, wg, preferred_element_type=F32)
        o_ref[:, :, c * col_chunk:(c + 1) * col_chunk] = (
            (a * jax.nn.sigmoid(g)).astype(o_ref.dtype).reshape(o_ref.shape[:2] + (col_chunk,)))


def _glu(yt, w, layer, *, nb, col_chunk):
    t, k, cols = yt.shape
    nc = cols // nb
    half = w.shape[2] // 2
    return pl.pallas_call(
        functools.partial(_glu_kernel, col_chunk=col_chunk),
        grid=(t,),
        in_specs=[pl.BlockSpec((1, k, cols), lambda j: (j, 0, 0)),
                  pl.BlockSpec(*_layer_block(w, layer), pipeline_mode=pl.Buffered(1))],
        out_specs=pl.BlockSpec((nb, nc, half), lambda j: (0, 0, j)),
        out_shape=jax.ShapeDtypeStruct((nb, nc, t * half), BF16),
        compiler_params=_cparams(("parallel",)),
        name="glu",
    )(yt, w)


def _merge_kernel(x_ref, mix_ref, xq_ref, gate_ref, k_ref, v_ref, qg_ref, w_ref, o_ref, *cat_refs, tm, phased):
    scale = X_HEAD_DIM ** -0.5
    nc = tm // S5_CHUNK
    w = w_ref[...].astype(BF16)

    def gather(k, cat_ref):
        r0 = k * tm
        gate = gate_ref[0, r0:r0 + tm, :]
        sg = gate * jax.nn.sigmoid(gate)
        if phased:
            for s in range(S5_CHUNK):
                rows = slice(s * nc, (s + 1) * nc)
                mix = mix_ref[0, k * nc:(k + 1) * nc, s * PRIMARY_WIDTH:(s + 1) * PRIMARY_WIDTH]
                cat_ref[rows, :PRIMARY_WIDTH] = mix * sg[rows, :PRIMARY_WIDTH]
        else:
            cat_ref[:, :PRIMARY_WIDTH] = mix_ref[0, r0:r0 + tm, :] * sg[:, :PRIMARY_WIDTH]
        for h in range(X_HEADS):
            sl = slice(h * X_HEAD_DIM, (h + 1) * X_HEAD_DIM)
            q = _rms(xq_ref[0, r0:r0 + tm, sl].astype(F32), qg_ref[...]).astype(BF16)
            s = lax.dot_general(q, k_ref[0, :, sl], (((1,), (1,)), ((), ())), preferred_element_type=F32) * scale
            p = jnp.exp(s - jnp.max(s, axis=-1, keepdims=True))
            p = (p / jnp.sum(p, axis=-1, keepdims=True)).astype(BF16)
            mo = jnp.dot(p, v_ref[0, :, sl], preferred_element_type=F32)
            osl = slice(PRIMARY_WIDTH + h * X_HEAD_DIM, PRIMARY_WIDTH + (h + 1) * X_HEAD_DIM)
            cat_ref[:, osl] = mo.astype(BF16) * sg[:, osl]

    def project(k, cat_ref):
        r0 = k * tm
        delta = jnp.dot(cat_ref[...], w, preferred_element_type=F32)
        o_ref[0, r0:r0 + tm, :] = x_ref[0, r0:r0 + tm, :] + (_from_phase_order(delta) if phased else delta)

    n_sub = len(cat_refs)
    for k in range(n_sub):
        gather(k, cat_refs[k])
        if k > 0:
            project(k - 1, cat_refs[k - 1])
    project(n_sub - 1, cat_refs[n_sub - 1])


def _merge(x, mix, proj, xq_blk, gate_blk, mk, mv, xq_norm, w_out, layer, *, tm, phased=False):
    b, l, d = x.shape
    m = mk.shape[2]
    n_sub = 2
    tg = n_sub * tm
    mix_spec = (pl.BlockSpec((1, tg // S5_CHUNK, S5_CHUNK * PRIMARY_WIDTH), lambda i, j: (i, j, 0)) if phased
                else pl.BlockSpec((1, tg, PRIMARY_WIDTH), lambda i, j: (i, j, 0)))
    return pl.pallas_call(
        functools.partial(_merge_kernel, tm=tm, phased=phased),
        grid=(b, l // tg),
        in_specs=[pl.BlockSpec((1, tg, d), lambda i, j: (i, j, 0)),
                  mix_spec,
                  pl.BlockSpec((1, tg, XQ_WIDTH), lambda i, j: (i, j, xq_blk)),
                  pl.BlockSpec((1, tg, BRANCH_WIDTH), lambda i, j: (i, j, gate_blk)),
                  pl.BlockSpec((None, 1, m, XQ_WIDTH), lambda i, j: (layer, i, 0, 0)),
                  pl.BlockSpec((None, 1, m, XQ_WIDTH), lambda i, j: (layer, i, 0, 0)),
                  pl.BlockSpec((None, 1, X_HEAD_DIM), lambda i, j: (layer, 0, 0)),
                  pl.BlockSpec((None, BRANCH_WIDTH, d), lambda i, j: (layer, 0, 0),
                               pipeline_mode=pl.Buffered(1))],
        out_specs=pl.BlockSpec((1, tg, d), lambda i, j: (i, j, 0)),
        out_shape=jax.ShapeDtypeStruct((b, l, d), F32),
        scratch_shapes=[pltpu.VMEM((tm, BRANCH_WIDTH), BF16)] * n_sub,
        compiler_params=_cparams(("parallel", "parallel")),
        name="merge",
    )(x, mix, proj, proj, mk, mv, xq_norm.reshape(-1, 1, X_HEAD_DIM), w_out)


def _mla_qkv_kernel(cq_ref, ckv_ref, kr_ref, posr_ref, invfc_ref, gq_ref, gkv_ref, gqn_ref,
                    gkn_ref, gqr_ref, gkr_ref, wuq_ref, wukv_ref, qt_ref, kn_ref, krope_ref, vt_ref,
                    wqt_ref, wk_ref, wvt_ref):
    half = MLA_ROPE // 2
    tm = cq_ref.shape[1]
    qk = MLA_NOPE + MLA_ROPE

    @pl.when((pl.program_id(0) == 0) & (pl.program_id(1) == 0))
    def _():
        wqt_ref[...] = wuq_ref[...].T.astype(BF16)
        for h in range(MLA_HEADS):
            c0 = h * (MLA_NOPE + MLA_V)
            wk_ref[:, h * MLA_NOPE:(h + 1) * MLA_NOPE] = wukv_ref[:, c0:c0 + MLA_NOPE].astype(BF16)
            wvt_ref[h * MLA_V:(h + 1) * MLA_V, :] = wukv_ref[:, c0 + MLA_NOPE:c0 + MLA_NOPE + MLA_V].T.astype(BF16)

    qscale = (MLA_NOPE + MLA_ROPE) ** -0.5 * math.log2(math.e)

    cq = _rms(cq_ref[0].astype(F32), gq_ref[...])
    ckv = _rms(ckv_ref[0].astype(F32), gkv_ref[...])
    cq_t = cq.T.astype(BF16)
    ckv_t = ckv.T.astype(BF16)
    ckv_b = ckv.astype(BF16)

    def project(h):
        dot = functools.partial(jnp.dot, preferred_element_type=F32)
        k_pair = dot(ckv_b, wk_ref[:, h * MLA_NOPE:(h + 2) * MLA_NOPE]) if h % 2 == 0 else None
        return (dot(wqt_ref[h * qk:(h + 1) * qk, :], cq_t),
                dot(wvt_ref[h * MLA_V:(h + 1) * MLA_V, :], ckv_t), k_pair)

    ang_t = invfc_ref[...] * posr_ref[0].astype(F32)
    cos_t, sin_t = jnp.cos(ang_t), jnp.sin(ang_t)
    g_nope = jnp.broadcast_to(gqn_ref[...], (MLA_NOPE, tm)) * qscale
    g_r1 = jnp.broadcast_to(gqr_ref[:half, :], (half, tm)) * qscale
    g_r2 = jnp.broadcast_to(gqr_ref[half:, :], (half, tm)) * qscale
    ahead = 2
    pending = [project(h) for h in range(ahead)]
    for h in range(MLA_HEADS):
        if h + ahead < MLA_HEADS:
            pending.append(project(h + ahead))
        q, v_t, _ = pending[h]
        k_n = pending[h - h % 2][2][:, (h % 2) * MLA_NOPE:(h % 2 + 1) * MLA_NOPE]
        nope = q[:MLA_NOPE]
        r = lax.rsqrt(jnp.mean(nope * nope, axis=0, keepdims=True) + EPS)
        qt_ref[0, h, :MLA_NOPE, :] = (nope * r * g_nope).astype(BF16)
        x1, x2 = q[MLA_NOPE:MLA_NOPE + half], q[MLA_NOPE + half:MLA_NOPE + MLA_ROPE]
        ss = jnp.sum(x1 * x1, axis=0, keepdims=True) + jnp.sum(x2 * x2, axis=0, keepdims=True)
        r = lax.rsqrt(ss * (1.0 / MLA_ROPE) + EPS)
        x1, x2 = x1 * r * g_r1, x2 * r * g_r2
        qt_ref[0, h, MLA_NOPE:MLA_NOPE + half, :] = (x1 * cos_t - x2 * sin_t).astype(BF16)
        qt_ref[0, h, MLA_NOPE + half:MLA_NOPE + MLA_ROPE, :] = (x1 * sin_t + x2 * cos_t).astype(BF16)
        qt_ref[0, h, MLA_NOPE + MLA_ROPE:, :] = jnp.zeros((MLA_QK_PAD - MLA_NOPE - MLA_ROPE, tm), BF16)
        kn_ref[0, h] = _rms(k_n, gkn_ref[...]).astype(BF16)
        vt_ref[0, h, :MLA_V, :] = v_t.astype(BF16)
        vt_ref[0, h, MLA_V:, :] = jnp.ones((MLA_VL - MLA_V, tm), BF16)

    kr_t = kr_ref[0].astype(F32).T
    x1, x2 = kr_t[:half], kr_t[half:MLA_ROPE]
    ss = jnp.sum(x1 * x1, axis=0, keepdims=True) + jnp.sum(x2 * x2, axis=0, keepdims=True)
    r = lax.rsqrt(ss * (1.0 / MLA_ROPE) + EPS)
    x1, x2 = x1 * r * gkr_ref[:half, :], x2 * r * gkr_ref[half:, :]
    rot = jnp.concatenate([x1 * cos_t - x2 * sin_t, x1 * sin_t + x2 * cos_t,
                           jnp.zeros((LANES - MLA_ROPE, tm), F32)], axis=0)
    krope_ref[0] = rot.T.astype(BF16)


def _mla_qkv(proj, cq_blk, ckv_blk, kr_blk, positions, gains, w_uq, w_ukv, layer, *, tm):
    b, l, _ = proj.shape
    hh = MLA_HEADS
    half = MLA_ROPE // 2
    inv_freq = ROPE_THETA ** (-jnp.arange(half, dtype=F32) / half)
    const = lambda a: pl.BlockSpec(a.shape, lambda i, j: (0,) * a.ndim)
    gq, gkv, gqn, gkn, gqr, gkr = gains
    consts = [inv_freq.reshape(half, 1), gq.reshape(1, -1), gkv.reshape(1, -1), gqn.reshape(-1, 1),
              gkn.reshape(1, -1), gqr.reshape(-1, 1), gkr.reshape(-1, 1)]
    weights = [w_uq, w_ukv]
    return pl.pallas_call(
        _mla_qkv_kernel,
        grid=(b, l // tm),
        in_specs=[pl.BlockSpec((1, tm, MLA_Q_LORA), lambda i, j: (i, j, cq_blk)),
                  pl.BlockSpec((1, tm, MLA_KV_LORA), lambda i, j: (i, j, ckv_blk)),
                  pl.BlockSpec((1, tm, LANES), lambda i, j: (i, j, kr_blk)),
                  pl.BlockSpec((1, 1, tm), lambda i, j: (i, 0, j))] + [const(a) for a in consts]
                 + [pl.BlockSpec(*_layer_block(w, layer)) for w in weights],
        out_specs=[pl.BlockSpec((1, hh, MLA_QK_PAD, tm), lambda i, j: (i, 0, 0, j)),
                   pl.BlockSpec((1, hh, tm, MLA_NOPE), lambda i, j: (i, 0, j, 0)),
                   pl.BlockSpec((1, tm, LANES), lambda i, j: (i, j, 0)),
                   pl.BlockSpec((1, hh, MLA_VL, tm), lambda i, j: (i, 0, 0, j))],
        out_shape=[jax.ShapeDtypeStruct((b, hh, MLA_QK_PAD, l), BF16),
                   jax.ShapeDtypeStruct((b, hh, l, MLA_NOPE), BF16),
                   jax.ShapeDtypeStruct((b, l, LANES), BF16),
                   jax.ShapeDtypeStruct((b, hh, MLA_VL, l), BF16)],
        scratch_shapes=[pltpu.VMEM((hh * (MLA_NOPE + MLA_ROPE), MLA_Q_LORA), BF16),
                        pltpu.VMEM((MLA_KV_LORA, hh * MLA_NOPE), BF16),
                        pltpu.VMEM((hh * MLA_V, MLA_KV_LORA), BF16)],
        compiler_params=_cparams(("arbitrary", "arbitrary")),
        name="mla_qkv",
    )(proj, proj, proj, positions.reshape(b, 1, l), *consts, *weights)


def _flash_kernel(qt_ref, kn_ref, kr_ref, vt_ref, o_ref, m_ref, acc_ref, *, tq, hp, ahead_full, ahead_diagonal):
    qi = pl.program_id(2)
    m_ref[...] = jnp.full(m_ref.shape, -jnp.inf, F32)
    acc_ref[...] = jnp.zeros(acc_ref.shape, F32)

    half = tq // 2
    lower = (lax.broadcasted_iota(jnp.int32, (half, half), 0)
             <= lax.broadcasted_iota(jnp.int32, (half, half), 1))

    def blocks(j, parts, diagonal):
        ahead = ahead_diagonal if diagonal else ahead_full
        base = pl.multiple_of(j * tq, tq)
        items = [(h, pl.ds(base + k0, nk), slice(q0, q0 + nq)) for k0, nk, q0, nq in parts for h in range(hp)]

        def scores(h, rows, cols):
            k = jnp.concatenate([kn_ref[0, h, rows, :], kr_ref[0, rows, :]], axis=-1)
            return jnp.dot(k, qt_ref[0, h, :, cols], preferred_element_type=F32)

        pending = [scores(*it) for it in items[:ahead]]
        for n, (h, rows, cols) in enumerate(items):
            if n + ahead < len(items):
                pending.append(scores(*items[n + ahead]))
            s = pending[n]
            if diagonal:
                square = jnp.where(lower, s[:, :half], jnp.finfo(F32).min)
                s = square if s.shape[1] == half else jnp.concatenate([square, s[:, half:]], axis=1)
            m = m_ref[h, :, cols]
            m_new = jnp.maximum(m, jnp.max(s, axis=0, keepdims=True))
            alpha = jnp.exp2(m - m_new)
            p = jnp.exp2(s - m_new)
            acc_ref[h, :, cols] = alpha * acc_ref[h, :, cols] + jnp.dot(
                vt_ref[0, h, :, rows], p.astype(BF16), preferred_element_type=F32)
            m_ref[h, :, cols] = m_new

    def body(j, carry):
        blocks(j, [(0, half, 0, tq), (half, half, 0, tq)], False)
        return carry

    lax.fori_loop(0, qi, body, 0)
    blocks(qi, [(0, half, 0, tq), (half, half, half, half)], True)
    for h in range(hp):
        o_ref[0, :, h * MLA_V:(h + 1) * MLA_V] = (
            acc_ref[h, :MLA_V, :] / acc_ref[h, MLA_V:MLA_V + 1, :]).T.astype(o_ref.dtype)


def _flash(qt, kn, kr, vt, *, tq, hp, ahead_full, ahead_diagonal):
    b, hh, _, l = qt.shape
    return pl.pallas_call(
        functools.partial(_flash_kernel, tq=tq, hp=hp, ahead_full=ahead_full, ahead_diagonal=ahead_diagonal),
        grid=(b, hh // hp, l // tq),
        in_specs=[pl.BlockSpec((1, hp, MLA_QK_PAD, tq), lambda i, h, j: (i, h, 0, j)),
                  pl.BlockSpec((1, hp, l, MLA_NOPE), lambda i, h, j: (i, h, 0, 0)),
                  pl.BlockSpec((1, l, LANES), lambda i, h, j: (i, 0, 0)),
                  pl.BlockSpec((1, hp, MLA_VL, l), lambda i, h, j: (i, h, 0, 0))],
        out_specs=pl.BlockSpec((1, tq, hp * MLA_V), lambda i, h, j: (i, j, h)),
        out_shape=jax.ShapeDtypeStruct((b, l, hh * MLA_V), BF16),
        scratch_shapes=[pltpu.VMEM((hp, 1, tq), F32), pltpu.VMEM((hp, MLA_VL, tq), F32)],
        compiler_params=_cparams(("parallel", "parallel", "parallel")),
        name="flash",
    )(qt, kn, kr, vt)


def _s5_layer(x, ln, w_in, lam_re, lam_im, log_step, b_re, b_im, c_re, c_im, d, w_glu,
              w_out, mem_kv, xq_norm, layer, j):
    b, l, dm = x.shape
    tm = 512
    proj, xn = _s5_in_proj(x, ln, w_in, j, tm=tm, col_chunk=512)
    xg = _s5_ut(xn, w_in, j)
    toep, wout, wst, dcol = _s5_asm(lam_re, lam_im, log_step, c_re, c_im, b_re, b_im, d)
    pw_re, pw_im = _s5_pow(lam_re, lam_im, log_step)
    col = lambda pw: pw.transpose(1, 2, 0)
    yt = _s5_mix(xg, toep, wst, wout, col(pw_re), col(pw_im), dcol, nb=b, gs=8)
    y = _glu(yt, w_glu, j, nb=b, col_chunk=256)
    return _merge(x, y, proj, BRANCH_WIDTH // XQ_WIDTH, 0, *mem_kv, xq_norm, w_out, layer, tm=tm, phased=True)


def _mla_layer(x, positions, ln, w_in, q_lora_norm, kv_lora_norm, w_uq, w_ukv, q_nope_norm, k_nope_norm,
               q_rope_norm, k_rope_norm, w_out, mem_kv, xq_norm, layer, j):
    b, l, dm = x.shape
    o1 = MLA_Q_LORA
    o2 = o1 + MLA_KV_LORA
    o3 = o2 + MLA_ROPE
    o4 = o3 + XQ_WIDTH
    segments = ((o4, BRANCH_WIDTH), (0, o1), (o3, XQ_WIDTH), (o1, MLA_KV_LORA), (o2, MLA_ROPE))
    wout = -(-(o4 + BRANCH_WIDTH) // 512) * 512
    proj = _mla_in_proj(x.reshape(b * l, dm), ln, w_in, j, segments, wout, tm=512, col_chunk=512)
    proj = proj.reshape(b, l, -1)
    gate_blk = 0
    cq_blk = BRANCH_WIDTH // MLA_Q_LORA
    xq_blk = (BRANCH_WIDTH + MLA_Q_LORA) // XQ_WIDTH
    ckv_blk = (BRANCH_WIDTH + MLA_Q_LORA + XQ_WIDTH) // MLA_KV_LORA
    kr_blk = (BRANCH_WIDTH + MLA_Q_LORA + XQ_WIDTH + MLA_KV_LORA) // LANES
    qt, kn, kr, vt = _mla_qkv(proj, cq_blk, ckv_blk, kr_blk, positions,
                              (q_lora_norm, kv_lora_norm, q_nope_norm, k_nope_norm, q_rope_norm, k_rope_norm),
                              w_uq, w_ukv, j, tm=512)
    attn = _flash(qt, kn, kr, vt, tq=512, hp=12, ahead_full=2, ahead_diagonal=4)
    return _merge(x, attn, proj, xq_blk, gate_blk, *mem_kv, xq_norm, w_out, layer, tm=512)


def kernel(x, mem, positions, ln_gain, w_out, mem_norm, w_mem_kv, xq_norm, xk_norm,
           s5_w_in, s5_lambda_re, s5_lambda_im, s5_log_step, s5_b_re, s5_b_im, s5_c_re, s5_c_im,
           s5_d, s5_w_glu, mla_w_in, mla_q_lora_norm, mla_kv_lora_norm, mla_w_uq, mla_w_ukv,
           mla_q_nope_norm, mla_k_nope_norm, mla_q_rope_norm, mla_k_rope_norm):
    depth = ln_gain.shape[0]
    mem_kv = _mem_kv(mem, mem_norm, w_mem_kv, xk_norm)
    for i in range(depth):
        j = i // 2
        if i % 2 == 0:
            x = _s5_layer(x, ln_gain[i], s5_w_in, s5_lambda_re[j], s5_lambda_im[j], s5_log_step[j],
                          s5_b_re[j], s5_b_im[j], s5_c_re[j], s5_c_im[j], s5_d[j], s5_w_glu,
                          w_out, mem_kv, xq_norm, i, j)
        else:
            x = _mla_layer(x, positions, ln_gain[i], mla_w_in, mla_q_lora_norm[j], mla_kv_lora_norm[j],
                           mla_w_uq, mla_w_ukv, mla_q_nope_norm[j], mla_k_nope_norm[j],
                           mla_q_rope_norm[j], mla_k_rope_norm[j],
                           w_out, mem_kv, xq_norm, i, j)
    return x
```

```python
import functools
import math

import jax
import jax.numpy as jnp
from jax import lax
from jax.experimental import pallas as pl
from jax.experimental.pallas import tpu as pltpu

D_MODEL = 1024
BRANCH_WIDTH = 2 * D_MODEL
XQ_WIDTH = BRANCH_WIDTH // 4
PRIMARY_WIDTH = BRANCH_WIDTH - XQ_WIDTH
X_HEADS = 4
X_HEAD_DIM = XQ_WIDTH // X_HEADS
S5_GROUP_CH = 16
S5_GROUPS = PRIMARY_WIDTH // S5_GROUP_CH
S5_STATE = 64
MLA_NOPE = 128
MLA_ROPE = 64
MLA_V = 128
MLA_HEADS = PRIMARY_WIDTH // MLA_V
MLA_Q_LORA = D_MODEL // 2
MLA_KV_LORA = D_MODEL // 4
ROPE_THETA = 10000.0
EPS = 1e-6

LANES = 128
MLA_QK_PAD = 2 * LANES
F32_SUBLANES = 8
BF16_SUBLANES = 16
MLA_VL = MLA_V + BF16_SUBLANES
S5_CHUNK = 2 * LANES // S5_GROUP_CH
S5_SCAN_EXPONENTS = [S5_CHUNK * 2 ** i for i in range(int(math.log2(LANES)))]
VMEM_LIMIT = 56 * 1024 * 1024

F32 = jnp.float32
BF16 = jnp.bfloat16


def _cparams(sem):
    return pltpu.CompilerParams(dimension_semantics=sem, vmem_limit_bytes=VMEM_LIMIT)


def _rms(x, g):
    return x * lax.rsqrt(jnp.mean(x * x, axis=-1, keepdims=True) + EPS) * g


def _layer_block(w, j, block=None, index=None):
    block = tuple(w.shape[1:]) if block is None else block
    index = (0,) * len(block) if index is None else index
    return (None,) + block, lambda *_: (j,) + index


def _mla_in_proj_kernel(x_ref, g_ref, w_ref, o_ref, wp_ref, *, segments, col_chunk):
    @pl.when(pl.program_id(0) == 0)
    def _():
        at = 0
        for start, width in segments:
            wp_ref[:, at:at + width] = w_ref[:, start:start + width].astype(BF16)
            at += width
        wp_ref[:, at:] = jnp.zeros((wp_ref.shape[0], wp_ref.shape[1] - at), BF16)

    xn = _rms(x_ref[...], g_ref[...]).astype(BF16)
    for c in range(o_ref.shape[1] // col_chunk):
        sl = slice(c * col_chunk, (c + 1) * col_chunk)
        o_ref[:, sl] = jnp.dot(xn, wp_ref[:, sl], preferred_element_type=F32).astype(o_ref.dtype)


def _mla_in_proj(x, g, w, j, segments, wout, *, tm, col_chunk):
    n, d = x.shape
    return pl.pallas_call(
        functools.partial(_mla_in_proj_kernel, segments=segments, col_chunk=col_chunk),
        grid=(n // tm,),
        in_specs=[pl.BlockSpec((tm, d), lambda i: (i, 0)),
                  pl.BlockSpec((1, d), lambda i: (0, 0)),
                  pl.BlockSpec(*_layer_block(w, j), pipeline_mode=pl.Buffered(1))],
        out_specs=pl.BlockSpec((tm, wout), lambda i: (i, 0)),
        out_shape=jax.ShapeDtypeStruct((n, wout), BF16),
        scratch_shapes=[pltpu.VMEM((d, wout), BF16)],
        compiler_params=_cparams(("arbitrary",)),
        name="mla_in_proj",
    )(x, g.reshape(1, d), w)


def _to_phase_order(a):
    n, d = a.shape
    return jnp.swapaxes(a.reshape(n // S5_CHUNK, S5_CHUNK, d), 0, 1).reshape(n, d)


def _from_phase_order(a):
    n, d = a.shape
    return jnp.swapaxes(a.reshape(S5_CHUNK, n // S5_CHUNK, d), 0, 1).reshape(n, d)


def _s5_in_proj_kernel(x_ref, g_ref, wg_ref, wx_ref, o_ref, xn_ref, wb_ref, *, tm, col_chunk):
    @pl.when((pl.program_id(0) == 0) & (pl.program_id(1) == 0))
    def _():
        wb_ref[:, :BRANCH_WIDTH] = wg_ref[...].astype(BF16)
        wb_ref[:, BRANCH_WIDTH:] = wx_ref[...].astype(BF16)

    d = x_ref.shape[2]
    nc = tm // S5_CHUNK
    chunks = [slice(c * col_chunk, (c + 1) * col_chunk) for c in range((BRANCH_WIDTH + XQ_WIDTH) // col_chunk)]
    weights = [wb_ref[:, sl] for sl in chunks]

    def normalise(k):
        xn = _rms(_to_phase_order(x_ref[0, k * tm:(k + 1) * tm, :]), g_ref[...]).astype(BF16)
        for s in range(S5_CHUNK):
            xn_ref[0, k * nc:(k + 1) * nc, s * d:(s + 1) * d] = xn[s * nc:(s + 1) * nc]
        return xn

    def project(k, xn):
        for sl, w in zip(chunks, weights):
            o_ref[0, k * tm:(k + 1) * tm, sl] = jnp.dot(xn, w, preferred_element_type=F32).astype(o_ref.dtype)

    n_sub = x_ref.shape[1] // tm
    prev = None
    for k in range(n_sub):
        if prev is not None:
            project(k - 1, prev)
        prev = normalise(k)
    project(n_sub - 1, prev)


def _s5_in_proj(x, g, w_in, j, *, tm, col_chunk):
    b, l, d = x.shape
    wout = BRANCH_WIDTH + XQ_WIDTH
    tg = 2 * tm
    nc = tg // S5_CHUNK
    return pl.pallas_call(
        functools.partial(_s5_in_proj_kernel, tm=tm, col_chunk=col_chunk),
        grid=(b, l // tg),
        in_specs=[pl.BlockSpec((1, tg, d), lambda i, j: (i, j, 0)),
                  pl.BlockSpec((1, d), lambda i, j: (0, 0)),
                  pl.BlockSpec(*_layer_block(w_in, j, (d, BRANCH_WIDTH),
                                             (0, (PRIMARY_WIDTH + XQ_WIDTH) // BRANCH_WIDTH)),
                               pipeline_mode=pl.Buffered(1)),
                  pl.BlockSpec(*_layer_block(w_in, j, (d, XQ_WIDTH), (0, PRIMARY_WIDTH // XQ_WIDTH)),
                               pipeline_mode=pl.Buffered(1))],
        out_specs=[pl.BlockSpec((1, tg, wout), lambda i, j: (i, j, 0)),
                   pl.BlockSpec((1, nc, S5_CHUNK * d), lambda i, j: (i, j, 0))],
        out_shape=[jax.ShapeDtypeStruct((b, l, wout), BF16),
                   jax.ShapeDtypeStruct((b, l // S5_CHUNK, S5_CHUNK * d), BF16)],
        scratch_shapes=[pltpu.VMEM((d, wout), BF16)],
        compiler_params=_cparams(("arbitrary", "arbitrary")),
        name="s5_in_proj",
    )(x, g.reshape(1, d), w_in, w_in)


def _mem_kv_kernel(m_ref, g_ref, w_ref, kg_ref, k_ref, v_ref):
    b, m, d = m_ref.shape
    w = w_ref[0].astype(BF16)
    for i in range(b):
        mn = _rms(m_ref[i], g_ref[0]).astype(BF16)
        kv = jnp.dot(mn, w, preferred_element_type=F32)
        for h in range(X_HEADS):
            sl = slice(h * X_HEAD_DIM, (h + 1) * X_HEAD_DIM)
            k_ref[0, i, :, sl] = _rms(kv[:, sl], kg_ref[0]).astype(BF16)
        v_ref[0, i] = kv[:, XQ_WIDTH:].astype(BF16)


def _mem_kv(mem, mem_norm, w_mem_kv, xk_norm):
    b, m, d = mem.shape
    depth = w_mem_kv.shape[0]
    out = jax.ShapeDtypeStruct((depth, b, m, XQ_WIDTH), BF16)
    return pl.pallas_call(
        _mem_kv_kernel,
        grid=(depth,),
        in_specs=[pl.BlockSpec((b, m, d), lambda n: (0, 0, 0)),
                  pl.BlockSpec((1, 1, d), lambda n: (n, 0, 0)),
                  pl.BlockSpec((1, d, 2 * XQ_WIDTH), lambda n: (n, 0, 0)),
                  pl.BlockSpec((1, 1, X_HEAD_DIM), lambda n: (n, 0, 0))],
        out_specs=[pl.BlockSpec((1, b, m, XQ_WIDTH), lambda n: (n, 0, 0, 0)),
                   pl.BlockSpec((1, b, m, XQ_WIDTH), lambda n: (n, 0, 0, 0))],
        out_shape=[out, out],
        compiler_params=_cparams(("parallel",)),
        name="mem_kv",
    )(mem, mem_norm.reshape(depth, 1, d), w_mem_kv, xk_norm.reshape(depth, 1, X_HEAD_DIM))


def _s5_pow_kernel(lr_ref, li_ref, ls_ref, pr_ref, pi_ref):
    step = jnp.exp(ls_ref[...])
    zr, zi = lr_ref[...] * step, li_ref[...] * step
    for n in range(pr_ref.shape[0]):
        if n < len(S5_SCAN_EXPONENTS):
            e = S5_SCAN_EXPONENTS[n]
            mag = jnp.exp(zr * e)
            pr_ref[n] = mag * jnp.cos(zi * e)
            pi_ref[n] = mag * jnp.sin(zi * e)
        else:
            pr_ref[n] = jnp.zeros_like(zr)
            pi_ref[n] = jnp.zeros_like(zr)


def _s5_pow(lam_re, lam_im, log_step):
    g, p = lam_re.shape
    slots = -(-len(S5_SCAN_EXPONENTS) // F32_SUBLANES) * F32_SUBLANES
    out = jax.ShapeDtypeStruct((slots, g, p), F32)
    return pl.pallas_call(_s5_pow_kernel, out_shape=[out, out], name="s5_pow")(
        lam_re, lam_im, log_step.reshape(g, 1))


def _s5_asm_kernel(lr_ref, li_ref, ls_ref, cr_ref, ci_ref, btr_ref, bti_ref, br_ref, bi_ref, d_ref,
                   toep_ref, wout_ref, wst_ref, dcol_ref):
    def group(i, carry):
        _s5_asm_group(i, lr_ref, li_ref, ls_ref, cr_ref, ci_ref, btr_ref, bti_ref, br_ref, bi_ref,
                      toep_ref, wout_ref, wst_ref)
        dcol_ref[i] = jnp.concatenate([d_ref[i]] * S5_CHUNK, axis=0)
        return carry

    lax.fori_loop(0, cr_ref.shape[0], group, 0)


def _dot_3pass_tiled(a, b):
    c = b.shape[1]
    rep = ((lax.broadcasted_iota(jnp.int32, (c, S5_CHUNK * c), 1) & (c - 1))
           == lax.broadcasted_iota(jnp.int32, (c, S5_CHUNK * c), 0)).astype(BF16)
    dot = functools.partial(jnp.dot, preferred_element_type=F32)
    a_hi, b_hi = a.astype(BF16), b.astype(BF16)
    a_lo = (a - a_hi.astype(F32)).astype(BF16)
    b_lo = (b - b_hi.astype(F32)).astype(BF16)
    b_hi, b_lo = dot(b_hi, rep).astype(BF16), dot(b_lo, rep).astype(BF16)
    return dot(a_hi, b_hi) + (dot(a_hi, b_lo) + dot(a_lo, b_hi))


def _s5_asm_group(i, lr_ref, li_ref, ls_ref, cr_ref, ci_ref, btr_ref, bti_ref, br_ref, bi_ref,
                  toep_ref, wout_ref, wst_ref):
    t = S5_CHUNK
    lr, li = lr_ref[i], li_ref[i]
    step = jnp.exp(ls_ref[i])
    zr, zi = lr * step, li * step
    rows = -(-(t + 1) // F32_SUBLANES) * F32_SUBLANES
    e = lax.broadcasted_iota(jnp.int32, (rows, lr.shape[1]), 0).astype(F32)
    mag = jnp.exp(e * zr)
    pr, pi = mag * jnp.cos(e * zi), mag * jnp.sin(e * zi)
    den = lr * lr + li * li
    mr = ((pr[1:2] - 1.0) * lr + pi[1:2] * li) / den
    mi = (pi[1:2] * lr - (pr[1:2] - 1.0) * li) / den
    cr, ci = cr_ref[i], ci_ref[i]
    btr, bti = btr_ref[i], bti_ref[i]
    amr = pr[:t] * mr - pi[:t] * mi
    ami = pr[:t] * mi + pi[:t] * mr
    l_re, l_im, w_re, w_im, o_re, o_im = [], [], [], [], [], []
    for k in range(t):
        ar, ai = amr[k:k + 1], ami[k:k + 1]
        l_re.append(cr * ar - ci * ai)
        l_im.append(-(cr * ai + ci * ar))
        ar, ai = amr[t - 1 - k:t - k], ami[t - 1 - k:t - k]
        w_re.append(btr * ar - bti * ai)
        w_im.append(btr * ai + bti * ar)
        ar, ai = pr[k + 1:k + 2], pi[k + 1:k + 2]
        o_re.append(cr * ar - ci * ai)
        o_im.append(-(cr * ai + ci * ar))
    cat = lambda parts: jnp.concatenate(parts, axis=0)
    kt = _dot_3pass_tiled(jnp.concatenate([cat(l_re), cat(l_im)], axis=1),
                          jnp.concatenate([br_ref[i], bi_ref[i]], axis=0))
    n = kt.shape[0]
    blk = lax.shift_right_logical(lax.broadcasted_iota(jnp.int32, kt.shape, 1), int(math.log2(S5_GROUP_CH)))
    toep = jnp.where(blk == 0, kt, 0.0)
    for s in range(1, t):
        shifted = jnp.concatenate([jnp.zeros((s * S5_GROUP_CH, n), F32), kt[:n - s * S5_GROUP_CH]], axis=0)
        toep = jnp.where(blk == s, shifted, toep)
    toep_ref[i] = toep.astype(BF16)
    wout_ref[i] = jnp.concatenate([cat(o_re), cat(o_im)], axis=1).astype(BF16)
    wst_ref[i] = jnp.concatenate([cat(w_re), cat(w_im)], axis=1).T.astype(BF16)


def _s5_asm(lam_re, lam_im, log_step, c_re, c_im, b_re, b_im, d):
    g, c, p = c_re.shape
    tc = S5_CHUNK * c
    gs = 8
    blk = lambda a: pl.BlockSpec((gs,) + a.shape[1:], lambda i: (i, 0, 0))
    bt_re, bt_im = b_re.transpose(0, 2, 1), b_im.transpose(0, 2, 1)
    args = (lam_re.reshape(g, 1, p), lam_im.reshape(g, 1, p), log_step.reshape(g, 1, 1),
            c_re, c_im, bt_re, bt_im, b_re, b_im, d.reshape(g, c, 1))
    return pl.pallas_call(
        _s5_asm_kernel,
        grid=(g // gs,),
        in_specs=[blk(a) for a in args],
        out_specs=[pl.BlockSpec((gs, tc, tc), lambda i: (i, 0, 0)),
                   pl.BlockSpec((gs, tc, 2 * p), lambda i: (i, 0, 0)),
                   pl.BlockSpec((gs, 2 * p, tc), lambda i: (i, 0, 0)),
                   pl.BlockSpec((gs, tc, 1), lambda i: (i, 0, 0))],
        out_shape=[jax.ShapeDtypeStruct((g, tc, tc), BF16), jax.ShapeDtypeStruct((g, tc, 2 * p), BF16),
                   jax.ShapeDtypeStruct((g, 2 * p, tc), BF16), jax.ShapeDtypeStruct((g, tc, 1), F32)],
        compiler_params=_cparams(("parallel",)),
        name="s5_asm",
    )(*args)


def _s5_ut_kernel(xn_ref, w_ref, o_ref, wt_ref):
    @pl.when(pl.program_id(0) == 0)
    def _():
        wt_ref[...] = w_ref[...].T.astype(BF16)

    nb, nc, d = xn_ref.shape
    ut = lax.dot_general(wt_ref[...], xn_ref[...].reshape(nb * nc, d), (((1,), (1,)), ((), ())),
                         preferred_element_type=F32)
    o_ref[...] = ut.astype(BF16).reshape(o_ref.shape)


def _s5_ut(xn, w_in, j):
    b, nc, td = xn.shape
    d = td // S5_CHUNK
    return pl.pallas_call(
        _s5_ut_kernel,
        grid=(S5_CHUNK,),
        in_specs=[pl.BlockSpec((b, nc, d), lambda s: (0, 0, s)),
                  pl.BlockSpec(*_layer_block(w_in, j, (d, PRIMARY_WIDTH)))],
        out_specs=pl.BlockSpec((S5_GROUPS, S5_GROUP_CH, b * nc), lambda s: (0, s, 0)),
        out_shape=jax.ShapeDtypeStruct((S5_GROUPS, S5_CHUNK * S5_GROUP_CH, b * nc), BF16),
        scratch_shapes=[pltpu.VMEM((PRIMARY_WIDTH, d), BF16)],
        compiler_params=_cparams(("arbitrary",)),
        name="s5_ut",
    )(xn, w_in)


def _s5_mix_kernel(x_ref, toep_ref, wst_ref, wout_ref, sr_ref, si_ref, d_ref, o_ref, *, nb):
    p = S5_STATE
    gs = x_ref.shape[0]
    lane = lax.broadcasted_iota(jnp.int32, (p, LANES), 1)
    n_steps = int(math.log2(LANES))

    def scan(g, hloc):
        pw = []
        for i in range(n_steps):
            keep = lane >= (1 << i)
            pw.append((jnp.where(keep, jnp.broadcast_to(sr_ref[g, :, i:i + 1], (p, LANES)), 0.0),
                       jnp.where(keep, jnp.broadcast_to(si_ref[g, :, i:i + 1], (p, LANES)), 0.0)))
        h_re = [hloc[:p, b * LANES:(b + 1) * LANES] for b in range(nb)]
        h_im = [hloc[p:, b * LANES:(b + 1) * LANES] for b in range(nb)]
        for i in range(n_steps):
            ar, ai = pw[i]
            r_sh = [pltpu.roll(v, 1 << i, 1) for v in h_re]
            i_sh = [pltpu.roll(v, 1 << i, 1) for v in h_im]
            h_re = [h_re[b] + ar * r_sh[b] - ai * i_sh[b] for b in range(nb)]
            h_im = [h_im[b] + ar * i_sh[b] + ai * r_sh[b] for b in range(nb)]
        h_re = [jnp.where(lane >= 1, pltpu.roll(v, 1, 1), 0.0) for v in h_re]
        h_im = [jnp.where(lane >= 1, pltpu.roll(v, 1, 1), 0.0) for v in h_im]
        return jnp.concatenate([jnp.concatenate(h_re, axis=1), jnp.concatenate(h_im, axis=1)],
                               axis=0).astype(BF16)

    def outputs(g, h):
        x = x_ref[g]
        y = (jnp.dot(toep_ref[g], x, preferred_element_type=F32)
             + jnp.dot(wout_ref[g], h, preferred_element_type=F32)
             + d_ref[g] * x.astype(F32))
        o_ref[:, g * S5_GROUP_CH:(g + 1) * S5_GROUP_CH, :] = (
            jax.nn.gelu(y).astype(o_ref.dtype).reshape(S5_CHUNK, S5_GROUP_CH, y.shape[1]))

    h_prev = None
    for g in range(gs):
        hloc = jnp.dot(wst_ref[g], x_ref[g], preferred_element_type=F32)
        h = scan(g, hloc)
        if h_prev is not None:
            outputs(g - 1, h_prev)
        h_prev = h
    outputs(gs - 1, h_prev)


def _s5_mix(xg, toep, wst, wout, sc_re, sc_im, dcol, *, nb, gs):
    g, tc, cols = xg.shape
    assert cols == nb * LANES, "one batch's chunks must fill exactly one 128-lane block"
    blk = lambda a: pl.BlockSpec((gs,) + a.shape[1:], lambda i: (i, 0, 0))
    return pl.pallas_call(
        functools.partial(_s5_mix_kernel, nb=nb),
        grid=(g // gs,),
        in_specs=[blk(a) for a in (xg, toep, wst, wout, sc_re, sc_im, dcol)],
        out_specs=pl.BlockSpec((S5_CHUNK, gs * S5_GROUP_CH, cols), lambda i: (0, i, 0)),
        out_shape=jax.ShapeDtypeStruct((S5_CHUNK, g * S5_GROUP_CH, cols), BF16),
        compiler_params=_cparams(("parallel",)),
        name="s5_mix",
    )(xg, toep, wst, wout, sc_re, sc_im, dcol)


def _glu_kernel(y_ref, w_ref, o_ref, *, col_chunk):
    y = y_ref[0].T
    half = o_ref.shape[-1]
    for c in range(half // col_chunk):
        wa = w_ref[:, c * col_chunk:(c + 1) * col_chunk].astype(BF16)
        wg = w_ref[:, half + c * col_chunk:half + (c + 1) * col_chunk].astype(BF16)
        a = jnp.dot(y, wa, preferred_element_type=F32)
        g = jnp.dot(y, wg, preferred_element_type=F32)
        o_ref[:, :, c * col_chunk:(c + 1) * col_chunk] = (
            (a * jax.nn.sigmoid(g)).astype(o_ref.dtype).reshape(o_ref.shape[:2] + (col_chunk,)))


def _glu(yt, w, layer, *, nb, col_chunk):
    t, k, cols = yt.shape
    nc = cols // nb
    half = w.shape[2] // 2
    return pl.pallas_call(
        functools.partial(_glu_kernel, col_chunk=col_chunk),
        grid=(t,),
        in_specs=[pl.BlockSpec((1, k, cols), lambda j: (j, 0, 0)),
                  pl.BlockSpec(*_layer_block(w, layer), pipeline_mode=pl.Buffered(1))],
        out_specs=pl.BlockSpec((nb, nc, half), lambda j: (0, 0, j)),
        out_shape=jax.ShapeDtypeStruct((nb, nc, t * half), BF16),
        compiler_params=_cparams(("parallel",)),
        name="glu",
    )(yt, w)


def _merge_kernel(x_ref, mix_ref, xq_ref, gate_ref, k_ref, v_ref, qg_ref, w_ref, o_ref, wb_ref, *cat_refs,
                  tm, phased):
    scale = X_HEAD_DIM ** -0.5
    nc = tm // S5_CHUNK
    @pl.when((pl.program_id(0) == 0) & (pl.program_id(1) == 0))
    def _():
        wb_ref[...] = w_ref[...].astype(BF16)

    def gather(k, cat_ref):
        r0 = k * tm
        gate = gate_ref[0, r0:r0 + tm, :]
        sg = gate * jax.nn.sigmoid(gate)
        if phased:
            for s in range(S5_CHUNK):
                rows = slice(s * nc, (s + 1) * nc)
                mix = mix_ref[0, k * nc:(k + 1) * nc, s * PRIMARY_WIDTH:(s + 1) * PRIMARY_WIDTH]
                cat_ref[rows, :PRIMARY_WIDTH] = mix * sg[rows, :PRIMARY_WIDTH]
        else:
            cat_ref[:, :PRIMARY_WIDTH] = mix_ref[0, r0:r0 + tm, :] * sg[:, :PRIMARY_WIDTH]
        for h in range(X_HEADS):
            sl = slice(h * X_HEAD_DIM, (h + 1) * X_HEAD_DIM)
            q = _rms(xq_ref[0, r0:r0 + tm, sl].astype(F32), qg_ref[...]).astype(BF16)
            s = lax.dot_general(q, k_ref[0, :, sl], (((1,), (1,)), ((), ())), preferred_element_type=F32) * scale
            p = jnp.exp(s - jnp.max(s, axis=-1, keepdims=True))
            p = (p / jnp.sum(p, axis=-1, keepdims=True)).astype(BF16)
            mo = jnp.dot(p, v_ref[0, :, sl], preferred_element_type=F32)
            osl = slice(PRIMARY_WIDTH + h * X_HEAD_DIM, PRIMARY_WIDTH + (h + 1) * X_HEAD_DIM)
            cat_ref[:, osl] = mo.astype(BF16) * sg[:, osl]

    def project(k, cat_ref):
        r0 = k * tm
        delta = jnp.dot(cat_ref[...], wb_ref[...], preferred_element_type=F32)
        o_ref[0, r0:r0 + tm, :] = x_ref[0, r0:r0 + tm, :] + (_from_phase_order(delta) if phased else delta)

    n_sub = len(cat_refs)
    for k in range(n_sub):
        gather(k, cat_refs[k])
        if k > 0:
            project(k - 1, cat_refs[k - 1])
    project(n_sub - 1, cat_refs[n_sub - 1])


def _merge(x, mix, proj, xq_blk, gate_blk, mk, mv, xq_norm, w_out, layer, *, tm, phased=False):
    b, l, d = x.shape
    m = mk.shape[2]
    n_sub = 2
    tg = n_sub * tm
    mix_spec = (pl.BlockSpec((1, tg // S5_CHUNK, S5_CHUNK * PRIMARY_WIDTH), lambda i, j: (i, j, 0)) if phased
                else pl.BlockSpec((1, tg, PRIMARY_WIDTH), lambda i, j: (i, j, 0)))
    return pl.pallas_call(
        functools.partial(_merge_kernel, tm=tm, phased=phased),
        grid=(b, l // tg),
        in_specs=[pl.BlockSpec((1, tg, d), lambda i, j: (i, j, 0)),
                  mix_spec,
                  pl.BlockSpec((1, tg, XQ_WIDTH), lambda i, j: (i, j, xq_blk)),
                  pl.BlockSpec((1, tg, BRANCH_WIDTH), lambda i, j: (i, j, gate_blk)),
                  pl.BlockSpec((None, 1, m, XQ_WIDTH), lambda i, j: (layer, i, 0, 0)),
                  pl.BlockSpec((None, 1, m, XQ_WIDTH), lambda i, j: (layer, i, 0, 0)),
                  pl.BlockSpec((None, 1, X_HEAD_DIM), lambda i, j: (layer, 0, 0)),
                  pl.BlockSpec((None, BRANCH_WIDTH, d), lambda i, j: (layer, 0, 0),
                               pipeline_mode=pl.Buffered(1))],
        out_specs=pl.BlockSpec((1, tg, d), lambda i, j: (i, j, 0)),
        out_shape=jax.ShapeDtypeStruct((b, l, d), F32),
        scratch_shapes=[pltpu.VMEM((BRANCH_WIDTH, d), BF16)] + [pltpu.VMEM((tm, BRANCH_WIDTH), BF16)] * n_sub,
        compiler_params=_cparams(("arbitrary", "arbitrary")),
        name="merge",
    )(x, mix, proj, proj, mk, mv, xq_norm.reshape(-1, 1, X_HEAD_DIM), w_out)


def _mla_qkv_kernel(cq_ref, ckv_ref, kr_ref, posr_ref, invfc_ref, gq_ref, gkv_ref, gqn_ref,
                    gkn_ref, gqr_ref, gkr_ref, wuq_ref, wukv_ref, qt_ref, kn_ref, krope_ref, vt_ref,
                    wqt_ref, wk_ref, wvt_ref):
    half = MLA_ROPE // 2
    tm = cq_ref.shape[1]
    qk = MLA_NOPE + MLA_ROPE

    @pl.when((pl.program_id(0) == 0) & (pl.program_id(1) == 0))
    def _():
        wqt_ref[...] = wuq_ref[...].T.astype(BF16)
        for h in range(MLA_HEADS):
            c0 = h * (MLA_NOPE + MLA_V)
            wk_ref[:, h * MLA_NOPE:(h + 1) * MLA_NOPE] = wukv_ref[:, c0:c0 + MLA_NOPE].astype(BF16)
            wvt_ref[h * MLA_V:(h + 1) * MLA_V, :] = wukv_ref[:, c0 + MLA_NOPE:c0 + MLA_NOPE + MLA_V].T.astype(BF16)

    qscale = (MLA_NOPE + MLA_ROPE) ** -0.5 * math.log2(math.e)

    cq = _rms(cq_ref[0].astype(F32), gq_ref[...])
    ckv = _rms(ckv_ref[0].astype(F32), gkv_ref[...])
    cq_t = cq.T.astype(BF16)
    ckv_t = ckv.T.astype(BF16)
    ckv_b = ckv.astype(BF16)

    def project(h):
        dot = functools.partial(jnp.dot, preferred_element_type=F32)
        k_pair = dot(ckv_b, wk_ref[:, h * MLA_NOPE:(h + 2) * MLA_NOPE]) if h % 2 == 0 else None
        return (dot(wqt_ref[h * qk:(h + 1) * qk, :], cq_t),
                dot(wvt_ref[h * MLA_V:(h + 1) * MLA_V, :], ckv_t), k_pair)

    ang_t = invfc_ref[...] * posr_ref[0].astype(F32)
    cos_t, sin_t = jnp.cos(ang_t), jnp.sin(ang_t)
    g_nope = jnp.broadcast_to(gqn_ref[...], (MLA_NOPE, tm)) * qscale
    g_r1 = jnp.broadcast_to(gqr_ref[:half, :], (half, tm)) * qscale
    g_r2 = jnp.broadcast_to(gqr_ref[half:, :], (half, tm)) * qscale
    ahead = 2
    pending = [project(h) for h in range(ahead)]
    for h in range(MLA_HEADS):
        if h + ahead < MLA_HEADS:
            pending.append(project(h + ahead))
        q, v_t, _ = pending[h]
        k_n = pending[h - h % 2][2][:, (h % 2) * MLA_NOPE:(h % 2 + 1) * MLA_NOPE]
        nope = q[:MLA_NOPE]
        r = lax.rsqrt(jnp.mean(nope * nope, axis=0, keepdims=True) + EPS)
        qt_ref[0, h, :MLA_NOPE, :] = (nope * r * g_nope).astype(BF16)
        x1, x2 = q[MLA_NOPE:MLA_NOPE + half], q[MLA_NOPE + half:MLA_NOPE + MLA_ROPE]
        ss = jnp.sum(x1 * x1, axis=0, keepdims=True) + jnp.sum(x2 * x2, axis=0, keepdims=True)
        r = lax.rsqrt(ss * (1.0 / MLA_ROPE) + EPS)
        x1, x2 = x1 * r * g_r1, x2 * r * g_r2
        qt_ref[0, h, MLA_NOPE:MLA_NOPE + half, :] = (x1 * cos_t - x2 * sin_t).astype(BF16)
        qt_ref[0, h, MLA_NOPE + half:MLA_NOPE + MLA_ROPE, :] = (x1 * sin_t + x2 * cos_t).astype(BF16)
        qt_ref[0, h, MLA_NOPE + MLA_ROPE:, :] = jnp.zeros((MLA_QK_PAD - MLA_NOPE - MLA_ROPE, tm), BF16)
        kn_ref[0, h] = _rms(k_n, gkn_ref[...]).astype(BF16)
        vt_ref[0, h, :MLA_V, :] = v_t.astype(BF16)
        vt_ref[0, h, MLA_V:, :] = jnp.ones((MLA_VL - MLA_V, tm), BF16)

    kr_t = kr_ref[0].astype(F32).T
    x1, x2 = kr_t[:half], kr_t[half:MLA_ROPE]
    ss = jnp.sum(x1 * x1, axis=0, keepdims=True) + jnp.sum(x2 * x2, axis=0, keepdims=True)
    r = lax.rsqrt(ss * (1.0 / MLA_ROPE) + EPS)
    x1, x2 = x1 * r * gkr_ref[:half, :], x2 * r * gkr_ref[half:, :]
    rot = jnp.concatenate([x1 * cos_t - x2 * sin_t, x1 * sin_t + x2 * cos_t,
                           jnp.zeros((LANES - MLA_ROPE, tm), F32)], axis=0)
    krope_ref[0] = rot.T.astype(BF16)


def _mla_qkv(proj, cq_blk, ckv_blk, kr_blk, positions, gains, w_uq, w_ukv, layer, *, tm):
    b, l, _ = proj.shape
    hh = MLA_HEADS
    half = MLA_ROPE // 2
    inv_freq = ROPE_THETA ** (-jnp.arange(half, dtype=F32) / half)
    const = lambda a: pl.BlockSpec(a.shape, lambda i, j: (0,) * a.ndim)
    gq, gkv, gqn, gkn, gqr, gkr = gains
    consts = [inv_freq.reshape(half, 1), gq.reshape(1, -1), gkv.reshape(1, -1), gqn.reshape(-1, 1),
              gkn.reshape(1, -1), gqr.reshape(-1, 1), gkr.reshape(-1, 1)]
    weights = [w_uq, w_ukv]
    return pl.pallas_call(
        _mla_qkv_kernel,
        grid=(b, l // tm),
        in_specs=[pl.BlockSpec((1, tm, MLA_Q_LORA), lambda i, j: (i, j, cq_blk)),
                  pl.BlockSpec((1, tm, MLA_KV_LORA), lambda i, j: (i, j, ckv_blk)),
                  pl.BlockSpec((1, tm, LANES), lambda i, j: (i, j, kr_blk)),
                  pl.BlockSpec((1, 1, tm), lambda i, j: (i, 0, j))] + [const(a) for a in consts]
                 + [pl.BlockSpec(*_layer_block(w, layer)) for w in weights],
        out_specs=[pl.BlockSpec((1, hh, MLA_QK_PAD, tm), lambda i, j: (i, 0, 0, j)),
                   pl.BlockSpec((1, hh, tm, MLA_NOPE), lambda i, j: (i, 0, j, 0)),
                   pl.BlockSpec((1, tm, LANES), lambda i, j: (i, j, 0)),
                   pl.BlockSpec((1, hh, MLA_VL, tm), lambda i, j: (i, 0, 0, j))],
        out_shape=[jax.ShapeDtypeStruct((b, hh, MLA_QK_PAD, l), BF16),
                   jax.ShapeDtypeStruct((b, hh, l, MLA_NOPE), BF16),
                   jax.ShapeDtypeStruct((b, l, LANES), BF16),
                   jax.ShapeDtypeStruct((b, hh, MLA_VL, l), BF16)],
        scratch_shapes=[pltpu.VMEM((hh * (MLA_NOPE + MLA_ROPE), MLA_Q_LORA), BF16),
                        pltpu.VMEM((MLA_KV_LORA, hh * MLA_NOPE), BF16),
                        pltpu.VMEM((hh * MLA_V, MLA_KV_LORA), BF16)],
        compiler_params=_cparams(("arbitrary", "arbitrary")),
        name="mla_qkv",
    )(proj, proj, proj, positions.reshape(b, 1, l), *consts, *weights)


def _flash_kernel(qt_ref, kn_ref, kr_ref, vt_ref, o_ref, m_ref, acc_ref, *, tq, hp, ahead_full, ahead_diagonal):
    qi = pl.program_id(2)
    m_ref[...] = jnp.full(m_ref.shape, -jnp.inf, F32)
    acc_ref[...] = jnp.zeros(acc_ref.shape, F32)

    half = tq // 2
    lower = (lax.broadcasted_iota(jnp.int32, (half, half), 0)
             <= lax.broadcasted_iota(jnp.int32, (half, half), 1))

    def blocks(j, parts, diagonal):
        ahead = ahead_diagonal if diagonal else ahead_full
        base = pl.multiple_of(j * tq, tq)
        items = [(h, pl.ds(base + k0, nk), slice(q0, q0 + nq)) for k0, nk, q0, nq in parts for h in range(hp)]

        def scores(h, rows, cols):
            k = jnp.concatenate([kn_ref[0, h, rows, :], kr_ref[0, rows, :]], axis=-1)
            return jnp.dot(k, qt_ref[0, h, :, cols], preferred_element_type=F32)

        pending = [scores(*it) for it in items[:ahead]]
        for n, (h, rows, cols) in enumerate(items):
            if n + ahead < len(items):
                pending.append(scores(*items[n + ahead]))
            s = pending[n]
            if diagonal:
                square = jnp.where(lower, s[:, :half], jnp.finfo(F32).min)
                s = square if s.shape[1] == half else jnp.concatenate([square, s[:, half:]], axis=1)
            m = m_ref[h, :, cols]
            m_new = jnp.maximum(m, jnp.max(s, axis=0, keepdims=True))
            alpha = jnp.exp2(m - m_new)
            p = jnp.exp2(s - m_new)
            acc_ref[h, :, cols] = alpha * acc_ref[h, :, cols] + jnp.dot(
                vt_ref[0, h, :, rows], p.astype(BF16), preferred_element_type=F32)
            m_ref[h, :, cols] = m_new

    def body(j, carry):
        blocks(j, [(0, half, 0, tq), (half, half, 0, tq)], False)
        return carry

    lax.fori_loop(0, qi, body, 0)
    blocks(qi, [(0, half, 0, tq), (half, half, half, half)], True)
    for h in range(hp):
        o_ref[0, :, h * MLA_V:(h + 1) * MLA_V] = (
            acc_ref[h, :MLA_V, :] / acc_ref[h, MLA_V:MLA_V + 1, :]).T.astype(o_ref.dtype)


def _flash(qt, kn, kr, vt, *, tq, hp, ahead_full, ahead_diagonal):
    b, hh, _, l = qt.shape
    return pl.pallas_call(
        functools.partial(_flash_kernel, tq=tq, hp=hp, ahead_full=ahead_full, ahead_diagonal=ahead_diagonal),
        grid=(b, hh // hp, l // tq),
        in_specs=[pl.BlockSpec((1, hp, MLA_QK_PAD, tq), lambda i, h, j: (i, h, 0, j)),
                  pl.BlockSpec((1, hp, l, MLA_NOPE), lambda i, h, j: (i, h, 0, 0)),
                  pl.BlockSpec((1, l, LANES), lambda i, h, j: (i, 0, 0)),
                  pl.BlockSpec((1, hp, MLA_VL, l), lambda i, h, j: (i, h, 0, 0))],
        out_specs=pl.BlockSpec((1, tq, hp * MLA_V), lambda i, h, j: (i, j, h)),
        out_shape=jax.ShapeDtypeStruct((b, l, hh * MLA_V), BF16),
        scratch_shapes=[pltpu.VMEM((hp, 1, tq), F32), pltpu.VMEM((hp, MLA_VL, tq), F32)],
        compiler_params=_cparams(("parallel", "parallel", "parallel")),
        name="flash",
    )(qt, kn, kr, vt)


def _s5_layer(x, ln, w_in, lam_re, lam_im, log_step, b_re, b_im, c_re, c_im, d, w_glu,
              w_out, mem_kv, xq_norm, layer, j):
    b, l, dm = x.shape
    tm = 512
    proj, xn = _s5_in_proj(x, ln, w_in, j, tm=tm, col_chunk=512)
    xg = _s5_ut(xn, w_in, j)
    toep, wout, wst, dcol = _s5_asm(lam_re, lam_im, log_step, c_re, c_im, b_re, b_im, d)
    pw_re, pw_im = _s5_pow(lam_re, lam_im, log_step)
    col = lambda pw: pw.transpose(1, 2, 0)
    yt = _s5_mix(xg, toep, wst, wout, col(pw_re), col(pw_im), dcol, nb=b, gs=8)
    y = _glu(yt, w_glu, j, nb=b, col_chunk=256)
    return _merge(x, y, proj, BRANCH_WIDTH // XQ_WIDTH, 0, *mem_kv, xq_norm, w_out, layer, tm=tm, phased=True)


def _mla_layer(x, positions, ln, w_in, q_lora_norm, kv_lora_norm, w_uq, w_ukv, q_nope_norm, k_nope_norm,
               q_rope_norm, k_rope_norm, w_out, mem_kv, xq_norm, layer, j):
    b, l, dm = x.shape
    o1 = MLA_Q_LORA
    o2 = o1 + MLA_KV_LORA
    o3 = o2 + MLA_ROPE
    o4 = o3 + XQ_WIDTH
    segments = ((o4, BRANCH_WIDTH), (0, o1), (o3, XQ_WIDTH), (o1, MLA_KV_LORA), (o2, MLA_ROPE))
    wout = -(-(o4 + BRANCH_WIDTH) // 512) * 512
    proj = _mla_in_proj(x.reshape(b * l, dm), ln, w_in, j, segments, wout, tm=512, col_chunk=512)
    proj = proj.reshape(b, l, -1)
    gate_blk = 0
    cq_blk = BRANCH_WIDTH // MLA_Q_LORA
    xq_blk = (BRANCH_WIDTH + MLA_Q_LORA) // XQ_WIDTH
    ckv_blk = (BRANCH_WIDTH + MLA_Q_LORA + XQ_WIDTH) // MLA_KV_LORA
    kr_blk = (BRANCH_WIDTH + MLA_Q_LORA + XQ_WIDTH + MLA_KV_LORA) // LANES
    qt, kn, kr, vt = _mla_qkv(proj, cq_blk, ckv_blk, kr_blk, positions,
                              (q_lora_norm, kv_lora_norm, q_nope_norm, k_nope_norm, q_rope_norm, k_rope_norm),
                              w_uq, w_ukv, j, tm=512)
    attn = _flash(qt, kn, kr, vt, tq=512, hp=12, ahead_full=2, ahead_diagonal=4)
    return _merge(x, attn, proj, xq_blk, gate_blk, *mem_kv, xq_norm, w_out, layer, tm=512)


def kernel(x, mem, positions, ln_gain, w_out, mem_norm, w_mem_kv, xq_norm, xk_norm,
           s5_w_in, s5_lambda_re, s5_lambda_im, s5_log_step, s5_b_re, s5_b_im, s5_c_re, s5_c_im,
           s5_d, s5_w_glu, mla_w_in, mla_q_lora_norm, mla_kv_lora_norm, mla_w_uq, mla_w_ukv,
           mla_q_nope_norm, mla_k_nope_norm, mla_q_rope_norm, mla_k_rope_norm):
    depth = ln_gain.shape[0]
    mem_kv = _mem_kv(mem, mem_norm, w_mem_kv, xk_norm)
    for i in range(depth):
        j = i // 2
        if i % 2 == 0:
            x = _s5_layer(x, ln_gain[i], s5_w_in, s5_lambda_re[j], s5_lambda_im[j], s5_log_step[j],
                          s5_b_re[j], s5_b_im[j], s5_c_re[j], s5_c_im[j], s5_d[j], s5_w_glu,
                          w_out, mem_kv, xq_norm, i, j)
        else:
            x = _mla_layer(x, positions, ln_gain[i], mla_w_in, mla_q_lora_norm[j], mla_kv_lora_norm[j],
                           mla_w_uq, mla_w_ukv, mla_q_nope_norm[j], mla_k_nope_norm[j],
                           mla_q_rope_norm[j], mla_k_rope_norm[j],
                           w_out, mem_kv, xq_norm, i, j)
    return x
```

```python
import functools
import math

import jax
import jax.numpy as jnp
from jax import lax
from jax.experimental import pallas as pl
from jax.experimental.pallas import tpu as pltpu

D_MODEL = 1024
BRANCH_WIDTH = 2 * D_MODEL
XQ_WIDTH = BRANCH_WIDTH // 4
PRIMARY_WIDTH = BRANCH_WIDTH - XQ_WIDTH
X_HEADS = 4
X_HEAD_DIM = XQ_WIDTH // X_HEADS
S5_GROUP_CH = 16
S5_GROUPS = PRIMARY_WIDTH // S5_GROUP_CH
S5_STATE = 64
MLA_NOPE = 128
MLA_ROPE = 64
MLA_V = 128
MLA_HEADS = PRIMARY_WIDTH // MLA_V
MLA_Q_LORA = D_MODEL // 2
MLA_KV_LORA = D_MODEL // 4
ROPE_THETA = 10000.0
EPS = 1e-6

LANES = 128
MLA_QK_PAD = MLA_NOPE + MLA_ROPE
F32_SUBLANES = 8
BF16_SUBLANES = 16
MLA_VL = MLA_V + BF16_SUBLANES
S5_CHUNK = 2 * LANES // S5_GROUP_CH
S5_SCAN_EXPONENTS = [S5_CHUNK * 2 ** i for i in range(int(math.log2(LANES)))]
VMEM_LIMIT = 56 * 1024 * 1024

F32 = jnp.float32
BF16 = jnp.bfloat16


def _cparams(sem):
    return pltpu.CompilerParams(dimension_semantics=sem, vmem_limit_bytes=VMEM_LIMIT)


def _rms(x, g):
    return x * lax.rsqrt(jnp.mean(x * x, axis=-1, keepdims=True) + EPS) * g


def _layer_block(w, j, block=None, index=None):
    block = tuple(w.shape[1:]) if block is None else block
    index = (0,) * len(block) if index is None else index
    return (None,) + block, lambda *_: (j,) + index


def _mla_in_proj_kernel(x_ref, g_ref, w_ref, o_ref, wp_ref, *, segments, col_chunk):
    @pl.when(pl.program_id(0) == 0)
    def _():
        at = 0
        for start, width in segments:
            wp_ref[:, at:at + width] = w_ref[:, start:start + width].astype(BF16)
            at += width
        wp_ref[:, at:] = jnp.zeros((wp_ref.shape[0], wp_ref.shape[1] - at), BF16)

    xn = _rms(x_ref[...], g_ref[...]).astype(BF16)
    for c in range(o_ref.shape[1] // col_chunk):
        sl = slice(c * col_chunk, (c + 1) * col_chunk)
        o_ref[:, sl] = jnp.dot(xn, wp_ref[:, sl], preferred_element_type=F32).astype(o_ref.dtype)


def _mla_in_proj(x, g, w, j, segments, wout, *, tm, col_chunk):
    n, d = x.shape
    return pl.pallas_call(
        functools.partial(_mla_in_proj_kernel, segments=segments, col_chunk=col_chunk),
        grid=(n // tm,),
        in_specs=[pl.BlockSpec((tm, d), lambda i: (i, 0)),
                  pl.BlockSpec((1, d), lambda i: (0, 0)),
                  pl.BlockSpec(*_layer_block(w, j), pipeline_mode=pl.Buffered(1))],
        out_specs=pl.BlockSpec((tm, wout), lambda i: (i, 0)),
        out_shape=jax.ShapeDtypeStruct((n, wout), BF16),
        scratch_shapes=[pltpu.VMEM((d, wout), BF16)],
        compiler_params=_cparams(("arbitrary",)),
        name="mla_in_proj",
    )(x, g.reshape(1, d), w)


def _to_phase_order(a):
    n, d = a.shape
    return jnp.swapaxes(a.reshape(n // S5_CHUNK, S5_CHUNK, d), 0, 1).reshape(n, d)


def _from_phase_order(a):
    n, d = a.shape
    return jnp.swapaxes(a.reshape(S5_CHUNK, n // S5_CHUNK, d), 0, 1).reshape(n, d)


def _s5_in_proj_kernel(x_ref, g_ref, wg_ref, wx_ref, o_ref, xn_ref, wb_ref, *, tm, col_chunk):
    @pl.when((pl.program_id(0) == 0) & (pl.program_id(1) == 0))
    def _():
        wb_ref[:, :BRANCH_WIDTH] = wg_ref[...].astype(BF16)
        wb_ref[:, BRANCH_WIDTH:] = wx_ref[...].astype(BF16)

    d = x_ref.shape[2]
    nc = tm // S5_CHUNK
    chunks = [slice(c * col_chunk, (c + 1) * col_chunk) for c in range((BRANCH_WIDTH + XQ_WIDTH) // col_chunk)]
    weights = [wb_ref[:, sl] for sl in chunks]

    def normalise(k):
        xn = _rms(_to_phase_order(x_ref[0, k * tm:(k + 1) * tm, :]), g_ref[...]).astype(BF16)
        for s in range(S5_CHUNK):
            xn_ref[0, k * nc:(k + 1) * nc, s * d:(s + 1) * d] = xn[s * nc:(s + 1) * nc]
        return xn

    def project(k, xn):
        for sl, w in zip(chunks, weights):
            o_ref[0, k * tm:(k + 1) * tm, sl] = jnp.dot(xn, w, preferred_element_type=F32).astype(o_ref.dtype)

    n_sub = x_ref.shape[1] // tm
    prev = None
    for k in range(n_sub):
        if prev is not None:
            project(k - 1, prev)
        prev = normalise(k)
    project(n_sub - 1, prev)


def _s5_in_proj(x, g, w_in, j, *, tm, col_chunk):
    b, l, d = x.shape
    wout = BRANCH_WIDTH + XQ_WIDTH
    tg = 2 * tm
    nc = tg // S5_CHUNK
    return pl.pallas_call(
        functools.partial(_s5_in_proj_kernel, tm=tm, col_chunk=col_chunk),
        grid=(b, l // tg),
        in_specs=[pl.BlockSpec((1, tg, d), lambda i, j: (i, j, 0)),
                  pl.BlockSpec((1, d), lambda i, j: (0, 0)),
                  pl.BlockSpec(*_layer_block(w_in, j, (d, BRANCH_WIDTH),
                                             (0, (PRIMARY_WIDTH + XQ_WIDTH) // BRANCH_WIDTH)),
                               pipeline_mode=pl.Buffered(1)),
                  pl.BlockSpec(*_layer_block(w_in, j, (d, XQ_WIDTH), (0, PRIMARY_WIDTH // XQ_WIDTH)),
                               pipeline_mode=pl.Buffered(1))],
        out_specs=[pl.BlockSpec((1, tg, wout), lambda i, j: (i, j, 0)),
                   pl.BlockSpec((1, nc, S5_CHUNK * d), lambda i, j: (i, j, 0))],
        out_shape=[jax.ShapeDtypeStruct((b, l, wout), BF16),
                   jax.ShapeDtypeStruct((b, l // S5_CHUNK, S5_CHUNK * d), BF16)],
        scratch_shapes=[pltpu.VMEM((d, wout), BF16)],
        compiler_params=_cparams(("arbitrary", "arbitrary")),
        name="s5_in_proj",
    )(x, g.reshape(1, d), w_in, w_in)


def _mem_kv_kernel(m_ref, g_ref, w_ref, kg_ref, k_ref, v_ref):
    b, m, d = m_ref.shape
    w = w_ref[0].astype(BF16)
    for i in range(b):
        mn = _rms(m_ref[i], g_ref[0]).astype(BF16)
        kv = jnp.dot(mn, w, preferred_element_type=F32)
        for h in range(X_HEADS):
            sl = slice(h * X_HEAD_DIM, (h + 1) * X_HEAD_DIM)
            k_ref[0, i, :, sl] = _rms(kv[:, sl], kg_ref[0]).astype(BF16)
        v_ref[0, i] = kv[:, XQ_WIDTH:].astype(BF16)


def _mem_kv(mem, mem_norm, w_mem_kv, xk_norm):
    b, m, d = mem.shape
    depth = w_mem_kv.shape[0]
    out = jax.ShapeDtypeStruct((depth, b, m, XQ_WIDTH), BF16)
    return pl.pallas_call(
        _mem_kv_kernel,
        grid=(depth,),
        in_specs=[pl.BlockSpec((b, m, d), lambda n: (0, 0, 0)),
                  pl.BlockSpec((1, 1, d), lambda n: (n, 0, 0)),
                  pl.BlockSpec((1, d, 2 * XQ_WIDTH), lambda n: (n, 0, 0)),
                  pl.BlockSpec((1, 1, X_HEAD_DIM), lambda n: (n, 0, 0))],
        out_specs=[pl.BlockSpec((1, b, m, XQ_WIDTH), lambda n: (n, 0, 0, 0)),
                   pl.BlockSpec((1, b, m, XQ_WIDTH), lambda n: (n, 0, 0, 0))],
        out_shape=[out, out],
        compiler_params=_cparams(("parallel",)),
        name="mem_kv",
    )(mem, mem_norm.reshape(depth, 1, d), w_mem_kv, xk_norm.reshape(depth, 1, X_HEAD_DIM))


def _s5_pow_kernel(lr_ref, li_ref, ls_ref, pr_ref, pi_ref):
    step = jnp.exp(ls_ref[...])
    zr, zi = lr_ref[...] * step, li_ref[...] * step
    for n in range(pr_ref.shape[0]):
        if n < len(S5_SCAN_EXPONENTS):
            e = S5_SCAN_EXPONENTS[n]
            mag = jnp.exp(zr * e)
            pr_ref[n] = mag * jnp.cos(zi * e)
            pi_ref[n] = mag * jnp.sin(zi * e)
        else:
            pr_ref[n] = jnp.zeros_like(zr)
            pi_ref[n] = jnp.zeros_like(zr)


def _s5_pow(lam_re, lam_im, log_step):
    g, p = lam_re.shape
    slots = -(-len(S5_SCAN_EXPONENTS) // F32_SUBLANES) * F32_SUBLANES
    out = jax.ShapeDtypeStruct((slots, g, p), F32)
    return pl.pallas_call(_s5_pow_kernel, out_shape=[out, out], name="s5_pow")(
        lam_re, lam_im, log_step.reshape(g, 1))


def _s5_asm_kernel(lr_ref, li_ref, ls_ref, cr_ref, ci_ref, btr_ref, bti_ref, br_ref, bi_ref, d_ref,
                   toep_ref, wout_ref, wst_ref, dcol_ref):
    def group(i, carry):
        _s5_asm_group(i, lr_ref, li_ref, ls_ref, cr_ref, ci_ref, btr_ref, bti_ref, br_ref, bi_ref,
                      toep_ref, wout_ref, wst_ref)
        dcol_ref[i] = jnp.concatenate([d_ref[i]] * S5_CHUNK, axis=0)
        return carry

    lax.fori_loop(0, cr_ref.shape[0], group, 0)


def _dot_3pass_tiled(a, b):
    c = b.shape[1]
    rep = ((lax.broadcasted_iota(jnp.int32, (c, S5_CHUNK * c), 1) & (c - 1))
           == lax.broadcasted_iota(jnp.int32, (c, S5_CHUNK * c), 0)).astype(BF16)
    dot = functools.partial(jnp.dot, preferred_element_type=F32)
    a_hi, b_hi = a.astype(BF16), b.astype(BF16)
    a_lo = (a - a_hi.astype(F32)).astype(BF16)
    b_lo = (b - b_hi.astype(F32)).astype(BF16)
    b_hi, b_lo = dot(b_hi, rep).astype(BF16), dot(b_lo, rep).astype(BF16)
    return dot(a_hi, b_hi) + (dot(a_hi, b_lo) + dot(a_lo, b_hi))


def _s5_asm_group(i, lr_ref, li_ref, ls_ref, cr_ref, ci_ref, btr_ref, bti_ref, br_ref, bi_ref,
                  toep_ref, wout_ref, wst_ref):
    t = S5_CHUNK
    lr, li = lr_ref[i], li_ref[i]
    step = jnp.exp(ls_ref[i])
    zr, zi = lr * step, li * step
    rows = -(-(t + 1) // F32_SUBLANES) * F32_SUBLANES
    e = lax.broadcasted_iota(jnp.int32, (rows, lr.shape[1]), 0).astype(F32)
    mag = jnp.exp(e * zr)
    pr, pi = mag * jnp.cos(e * zi), mag * jnp.sin(e * zi)
    den = lr * lr + li * li
    mr = ((pr[1:2] - 1.0) * lr + pi[1:2] * li) / den
    mi = (pi[1:2] * lr - (pr[1:2] - 1.0) * li) / den
    cr, ci = cr_ref[i], ci_ref[i]
    btr, bti = btr_ref[i], bti_ref[i]
    amr = pr[:t] * mr - pi[:t] * mi
    ami = pr[:t] * mi + pi[:t] * mr
    l_re, l_im, w_re, w_im, o_re, o_im = [], [], [], [], [], []
    for k in range(t):
        ar, ai = amr[k:k + 1], ami[k:k + 1]
        l_re.append(cr * ar - ci * ai)
        l_im.append(-(cr * ai + ci * ar))
        ar, ai = amr[t - 1 - k:t - k], ami[t - 1 - k:t - k]
        w_re.append(btr * ar - bti * ai)
        w_im.append(btr * ai + bti * ar)
        ar, ai = pr[k + 1:k + 2], pi[k + 1:k + 2]
        o_re.append(cr * ar - ci * ai)
        o_im.append(-(cr * ai + ci * ar))
    cat = lambda parts: jnp.concatenate(parts, axis=0)
    kt = _dot_3pass_tiled(jnp.concatenate([cat(l_re), cat(l_im)], axis=1),
                          jnp.concatenate([br_ref[i], bi_ref[i]], axis=0))
    n = kt.shape[0]
    blk = lax.shift_right_logical(lax.broadcasted_iota(jnp.int32, kt.shape, 1), int(math.log2(S5_GROUP_CH)))
    toep = jnp.where(blk == 0, kt, 0.0)
    for s in range(1, t):
        shifted = jnp.concatenate([jnp.zeros((s * S5_GROUP_CH, n), F32), kt[:n - s * S5_GROUP_CH]], axis=0)
        toep = jnp.where(blk == s, shifted, toep)
    toep_ref[i] = toep.astype(BF16)
    wout_ref[i] = jnp.concatenate([cat(o_re), cat(o_im)], axis=1).astype(BF16)
    wst_ref[i] = jnp.concatenate([cat(w_re), cat(w_im)], axis=1).T.astype(BF16)


def _s5_asm(lam_re, lam_im, log_step, c_re, c_im, b_re, b_im, d):
    g, c, p = c_re.shape
    tc = S5_CHUNK * c
    gs = 8
    blk = lambda a: pl.BlockSpec((gs,) + a.shape[1:], lambda i: (i, 0, 0))
    bt_re, bt_im = b_re.transpose(0, 2, 1), b_im.transpose(0, 2, 1)
    args = (lam_re.reshape(g, 1, p), lam_im.reshape(g, 1, p), log_step.reshape(g, 1, 1),
            c_re, c_im, bt_re, bt_im, b_re, b_im, d.reshape(g, c, 1))
    return pl.pallas_call(
        _s5_asm_kernel,
        grid=(g // gs,),
        in_specs=[blk(a) for a in args],
        out_specs=[pl.BlockSpec((gs, tc, tc), lambda i: (i, 0, 0)),
                   pl.BlockSpec((gs, tc, 2 * p), lambda i: (i, 0, 0)),
                   pl.BlockSpec((gs, 2 * p, tc), lambda i: (i, 0, 0)),
                   pl.BlockSpec((gs, tc, 1), lambda i: (i, 0, 0))],
        out_shape=[jax.ShapeDtypeStruct((g, tc, tc), BF16), jax.ShapeDtypeStruct((g, tc, 2 * p), BF16),
                   jax.ShapeDtypeStruct((g, 2 * p, tc), BF16), jax.ShapeDtypeStruct((g, tc, 1), F32)],
        compiler_params=_cparams(("parallel",)),
        name="s5_asm",
    )(*args)


def _s5_ut_kernel(xn_ref, w_ref, o_ref, wt_ref):
    @pl.when(pl.program_id(0) == 0)
    def _():
        wt_ref[...] = w_ref[...].T.astype(BF16)

    nb, nc, d = xn_ref.shape
    ut = lax.dot_general(wt_ref[...], xn_ref[...].reshape(nb * nc, d), (((1,), (1,)), ((), ())),
                         preferred_element_type=F32)
    o_ref[...] = ut.astype(BF16).reshape(o_ref.shape)


def _s5_ut(xn, w_in, j):
    b, nc, td = xn.shape
    d = td // S5_CHUNK
    return pl.pallas_call(
        _s5_ut_kernel,
        grid=(S5_CHUNK,),
        in_specs=[pl.BlockSpec((b, nc, d), lambda s: (0, 0, s)),
                  pl.BlockSpec(*_layer_block(w_in, j, (d, PRIMARY_WIDTH)))],
        out_specs=pl.BlockSpec((S5_GROUPS, S5_GROUP_CH, b * nc), lambda s: (0, s, 0)),
        out_shape=jax.ShapeDtypeStruct((S5_GROUPS, S5_CHUNK * S5_GROUP_CH, b * nc), BF16),
        scratch_shapes=[pltpu.VMEM((PRIMARY_WIDTH, d), BF16)],
        compiler_params=_cparams(("arbitrary",)),
        name="s5_ut",
    )(xn, w_in)


def _s5_mix_kernel(x_ref, toep_ref, wst_ref, wout_ref, sr_ref, si_ref, d_ref, o_ref, *, nb):
    p = S5_STATE
    gs = x_ref.shape[0]
    lane = lax.broadcasted_iota(jnp.int32, (p, LANES), 1)
    n_steps = int(math.log2(LANES))

    def scan(g, hloc):
        pw = []
        for i in range(n_steps):
            keep = lane >= (1 << i)
            pw.append((jnp.where(keep, jnp.broadcast_to(sr_ref[g, :, i:i + 1], (p, LANES)), 0.0),
                       jnp.where(keep, jnp.broadcast_to(si_ref[g, :, i:i + 1], (p, LANES)), 0.0)))
        h_re = [hloc[:p, b * LANES:(b + 1) * LANES] for b in range(nb)]
        h_im = [hloc[p:, b * LANES:(b + 1) * LANES] for b in range(nb)]
        for i in range(n_steps):
            ar, ai = pw[i]
            r_sh = [pltpu.roll(v, 1 << i, 1) for v in h_re]
            i_sh = [pltpu.roll(v, 1 << i, 1) for v in h_im]
            h_re = [h_re[b] + ar * r_sh[b] - ai * i_sh[b] for b in range(nb)]
            h_im = [h_im[b] + ar * i_sh[b] + ai * r_sh[b] for b in range(nb)]
        h_re = [jnp.where(lane >= 1, pltpu.roll(v, 1, 1), 0.0) for v in h_re]
        h_im = [jnp.where(lane >= 1, pltpu.roll(v, 1, 1), 0.0) for v in h_im]
        return jnp.concatenate([jnp.concatenate(h_re, axis=1), jnp.concatenate(h_im, axis=1)],
                               axis=0).astype(BF16)

    def outputs(g, h):
        x = x_ref[g]
        y = (jnp.dot(toep_ref[g], x, preferred_element_type=F32)
             + jnp.dot(wout_ref[g], h, preferred_element_type=F32)
             + d_ref[g] * x.astype(F32))
        o_ref[:, g * S5_GROUP_CH:(g + 1) * S5_GROUP_CH, :] = (
            jax.nn.gelu(y).astype(o_ref.dtype).reshape(S5_CHUNK, S5_GROUP_CH, y.shape[1]))

    h_prev = None
    for g in range(gs):
        hloc = jnp.dot(wst_ref[g], x_ref[g], preferred_element_type=F32)
        h = scan(g, hloc)
        if h_prev is not None:
            outputs(g - 1, h_prev)
        h_prev = h
    outputs(gs - 1, h_prev)


def _s5_mix(xg, toep, wst, wout, sc_re, sc_im, dcol, *, nb, gs):
    g, tc, cols = xg.shape
    assert cols == nb * LANES, "one batch's chunks must fill exactly one 128-lane block"
    blk = lambda a: pl.BlockSpec((gs,) + a.shape[1:], lambda i: (i, 0, 0))
    return pl.pallas_call(
        functools.partial(_s5_mix_kernel, nb=nb),
        grid=(g // gs,),
        in_specs=[blk(a) for a in (xg, toep, wst, wout, sc_re, sc_im, dcol)],
        out_specs=pl.BlockSpec((S5_CHUNK, gs * S5_GROUP_CH, cols), lambda i: (0, i, 0)),
        out_shape=jax.ShapeDtypeStruct((S5_CHUNK, g * S5_GROUP_CH, cols), BF16),
        compiler_params=_cparams(("parallel",)),
        name="s5_mix",
    )(xg, toep, wst, wout, sc_re, sc_im, dcol)


def _glu_kernel(y_ref, w_ref, o_ref, *, col_chunk):
    y = y_ref[0].T
    half = o_ref.shape[-1]
    for c in range(half // col_chunk):
        wa = w_ref[:, c * col_chunk:(c + 1) * col_chunk].astype(BF16)
        wg = w_ref[:, half + c * col_chunk:half + (c + 1) * col_chunk].astype(BF16)
        a = jnp.dot(y, wa, preferred_element_type=F32)
        g = jnp.dot(y, wg, preferred_element_type=F32)
        o_ref[:, :, c * col_chunk:(c + 1) * col_chunk] = (
            (a * jax.nn.sigmoid(g)).astype(o_ref.dtype).reshape(o_ref.shape[:2] + (col_chunk,)))


def _glu(yt, w, layer, *, nb, col_chunk):
    t, k, cols = yt.shape
    nc = cols // nb
    half = w.shape[2] // 2
    return pl.pallas_call(
        functools.partial(_glu_kernel, col_chunk=col_chunk),
        grid=(t,),
        in_specs=[pl.BlockSpec((1, k, cols), lambda j: (j, 0, 0)),
                  pl.BlockSpec(*_layer_block(w, layer), pipeline_mode=pl.Buffered(1))],
        out_specs=pl.BlockSpec((nb, nc, half), lambda j: (0, 0, j)),
        out_shape=jax.ShapeDtypeStruct((nb, nc, t * half), BF16),
        compiler_params=_cparams(("parallel",)),
        name="glu",
    )(yt, w)


def _merge_kernel(x_ref, mix_ref, xq_ref, gate_ref, k_ref, v_ref, qg_ref, w_ref, o_ref, wb_ref, *cat_refs,
                  tm, phased):
    scale = X_HEAD_DIM ** -0.5
    nc = tm // S5_CHUNK
    @pl.when((pl.program_id(0) == 0) & (pl.program_id(1) == 0))
    def _():
        wb_ref[...] = w_ref[...].astype(BF16)

    def gather(k, cat_ref):
        r0 = k * tm
        gate = gate_ref[0, r0:r0 + tm, :]
        sg = gate * jax.nn.sigmoid(gate)
        if phased:
            for s in range(S5_CHUNK):
                rows = slice(s * nc, (s + 1) * nc)
                mix = mix_ref[0, k * nc:(k + 1) * nc, s * PRIMARY_WIDTH:(s + 1) * PRIMARY_WIDTH]
                cat_ref[rows, :PRIMARY_WIDTH] = mix * sg[rows, :PRIMARY_WIDTH]
        else:
            cat_ref[:, :PRIMARY_WIDTH] = mix_ref[0, r0:r0 + tm, :] * sg[:, :PRIMARY_WIDTH]
        for h in range(X_HEADS):
            sl = slice(h * X_HEAD_DIM, (h + 1) * X_HEAD_DIM)
            q = _rms(xq_ref[0, r0:r0 + tm, sl].astype(F32), qg_ref[...]).astype(BF16)
            s = lax.dot_general(q, k_ref[0, :, sl], (((1,), (1,)), ((), ())), preferred_element_type=F32) * scale
            p = jnp.exp(s - jnp.max(s, axis=-1, keepdims=True))
            p = (p / jnp.sum(p, axis=-1, keepdims=True)).astype(BF16)
            mo = jnp.dot(p, v_ref[0, :, sl], preferred_element_type=F32)
            osl = slice(PRIMARY_WIDTH + h * X_HEAD_DIM, PRIMARY_WIDTH + (h + 1) * X_HEAD_DIM)
            cat_ref[:, osl] = mo.astype(BF16) * sg[:, osl]

    def project(k, cat_ref):
        r0 = k * tm
        delta = jnp.dot(cat_ref[...], wb_ref[...], preferred_element_type=F32)
        o_ref[0, r0:r0 + tm, :] = x_ref[0, r0:r0 + tm, :] + (_from_phase_order(delta) if phased else delta)

    n_sub = len(cat_refs)
    for k in range(n_sub):
        gather(k, cat_refs[k])
        if k > 0:
            project(k - 1, cat_refs[k - 1])
    project(n_sub - 1, cat_refs[n_sub - 1])


def _merge(x, mix, proj, xq_blk, gate_blk, mk, mv, xq_norm, w_out, layer, *, tm, phased=False):
    b, l, d = x.shape
    m = mk.shape[2]
    n_sub = 2
    tg = n_sub * tm
    mix_spec = (pl.BlockSpec((1, tg // S5_CHUNK, S5_CHUNK * PRIMARY_WIDTH), lambda i, j: (i, j, 0)) if phased
                else pl.BlockSpec((1, tg, PRIMARY_WIDTH), lambda i, j: (i, j, 0)))
    return pl.pallas_call(
        functools.partial(_merge_kernel, tm=tm, phased=phased),
        grid=(b, l // tg),
        in_specs=[pl.BlockSpec((1, tg, d), lambda i, j: (i, j, 0)),
                  mix_spec,
                  pl.BlockSpec((1, tg, XQ_WIDTH), lambda i, j: (i, j, xq_blk)),
                  pl.BlockSpec((1, tg, BRANCH_WIDTH), lambda i, j: (i, j, gate_blk)),
                  pl.BlockSpec((None, 1, m, XQ_WIDTH), lambda i, j: (layer, i, 0, 0)),
                  pl.BlockSpec((None, 1, m, XQ_WIDTH), lambda i, j: (layer, i, 0, 0)),
                  pl.BlockSpec((None, 1, X_HEAD_DIM), lambda i, j: (layer, 0, 0)),
                  pl.BlockSpec((None, BRANCH_WIDTH, d), lambda i, j: (layer, 0, 0),
                               pipeline_mode=pl.Buffered(1))],
        out_specs=pl.BlockSpec((1, tg, d), lambda i, j: (i, j, 0)),
        out_shape=jax.ShapeDtypeStruct((b, l, d), F32),
        scratch_shapes=[pltpu.VMEM((BRANCH_WIDTH, d), BF16)] + [pltpu.VMEM((tm, BRANCH_WIDTH), BF16)] * n_sub,
        compiler_params=_cparams(("arbitrary", "arbitrary")),
        name="merge",
    )(x, mix, proj, proj, mk, mv, xq_norm.reshape(-1, 1, X_HEAD_DIM), w_out)


def _mla_qkv_kernel(cq_ref, ckv_ref, kr_ref, posr_ref, invfc_ref, gq_ref, gkv_ref, gqn_ref,
                    gkn_ref, gqr_ref, gkr_ref, wuq_ref, wukv_ref, qt_ref, kn_ref, krope_ref, vt_ref,
                    wqt_ref, wk_ref, wvt_ref):
    half = MLA_ROPE // 2
    tm = cq_ref.shape[1]
    qk = MLA_NOPE + MLA_ROPE

    @pl.when((pl.program_id(0) == 0) & (pl.program_id(1) == 0))
    def _():
        wqt_ref[...] = wuq_ref[...].T.astype(BF16)
        for h in range(MLA_HEADS):
            c0 = h * (MLA_NOPE + MLA_V)
            wk_ref[:, h * MLA_NOPE:(h + 1) * MLA_NOPE] = wukv_ref[:, c0:c0 + MLA_NOPE].astype(BF16)
            wvt_ref[h * MLA_V:(h + 1) * MLA_V, :] = wukv_ref[:, c0 + MLA_NOPE:c0 + MLA_NOPE + MLA_V].T.astype(BF16)

    qscale = (MLA_NOPE + MLA_ROPE) ** -0.5 * math.log2(math.e)

    cq = _rms(cq_ref[0].astype(F32), gq_ref[...])
    ckv = _rms(ckv_ref[0].astype(F32), gkv_ref[...])
    cq_t = cq.T.astype(BF16)
    ckv_t = ckv.T.astype(BF16)
    ckv_b = ckv.astype(BF16)

    def project(h):
        dot = functools.partial(jnp.dot, preferred_element_type=F32)
        k_pair = dot(ckv_b, wk_ref[:, h * MLA_NOPE:(h + 2) * MLA_NOPE]) if h % 2 == 0 else None
        return (dot(wqt_ref[h * qk:(h + 1) * qk, :], cq_t),
                dot(wvt_ref[h * MLA_V:(h + 1) * MLA_V, :], ckv_t), k_pair)

    ang_t = invfc_ref[...] * posr_ref[0].astype(F32)
    cos_t, sin_t = jnp.cos(ang_t), jnp.sin(ang_t)
    g_nope = jnp.broadcast_to(gqn_ref[...], (MLA_NOPE, tm)) * qscale
    g_r1 = jnp.broadcast_to(gqr_ref[:half, :], (half, tm)) * qscale
    g_r2 = jnp.broadcast_to(gqr_ref[half:, :], (half, tm)) * qscale
    ahead = 2
    pending = [project(h) for h in range(ahead)]
    for h in range(MLA_HEADS):
        if h + ahead < MLA_HEADS:
            pending.append(project(h + ahead))
        q, v_t, _ = pending[h]
        k_n = pending[h - h % 2][2][:, (h % 2) * MLA_NOPE:(h % 2 + 1) * MLA_NOPE]
        nope = q[:MLA_NOPE]
        r = lax.rsqrt(jnp.mean(nope * nope, axis=0, keepdims=True) + EPS)
        qt_ref[0, h, :MLA_NOPE, :] = (nope * r * g_nope).astype(BF16)
        x1, x2 = q[MLA_NOPE:MLA_NOPE + half], q[MLA_NOPE + half:MLA_NOPE + MLA_ROPE]
        ss = jnp.sum(x1 * x1, axis=0, keepdims=True) + jnp.sum(x2 * x2, axis=0, keepdims=True)
        r = lax.rsqrt(ss * (1.0 / MLA_ROPE) + EPS)
        x1, x2 = x1 * r * g_r1, x2 * r * g_r2
        qt_ref[0, h, MLA_NOPE:MLA_NOPE + half, :] = (x1 * cos_t - x2 * sin_t).astype(BF16)
        qt_ref[0, h, MLA_NOPE + half:MLA_NOPE + MLA_ROPE, :] = (x1 * sin_t + x2 * cos_t).astype(BF16)
        kn_ref[0, h] = _rms(k_n, gkn_ref[...]).astype(BF16)
        vt_ref[0, h, :MLA_V, :] = v_t.astype(BF16)
        vt_ref[0, h, MLA_V:, :] = jnp.ones((MLA_VL - MLA_V, tm), BF16)

    kr_t = kr_ref[0].astype(F32).T
    x1, x2 = kr_t[:half], kr_t[half:MLA_ROPE]
    ss = jnp.sum(x1 * x1, axis=0, keepdims=True) + jnp.sum(x2 * x2, axis=0, keepdims=True)
    r = lax.rsqrt(ss * (1.0 / MLA_ROPE) + EPS)
    x1, x2 = x1 * r * gkr_ref[:half, :], x2 * r * gkr_ref[half:, :]
    rot = jnp.concatenate([x1 * cos_t - x2 * sin_t, x1 * sin_t + x2 * cos_t,
                           jnp.zeros((LANES - MLA_ROPE, tm), F32)], axis=0)
    krope_ref[0] = rot.T.astype(BF16)


def _mla_qkv(proj, cq_blk, ckv_blk, kr_blk, positions, gains, w_uq, w_ukv, layer, *, tm):
    b, l, _ = proj.shape
    hh = MLA_HEADS
    half = MLA_ROPE // 2
    inv_freq = ROPE_THETA ** (-jnp.arange(half, dtype=F32) / half)
    const = lambda a: pl.BlockSpec(a.shape, lambda i, j: (0,) * a.ndim)
    gq, gkv, gqn, gkn, gqr, gkr = gains
    consts = [inv_freq.reshape(half, 1), gq.reshape(1, -1), gkv.reshape(1, -1), gqn.reshape(-1, 1),
              gkn.reshape(1, -1), gqr.reshape(-1, 1), gkr.reshape(-1, 1)]
    weights = [w_uq, w_ukv]
    return pl.pallas_call(
        _mla_qkv_kernel,
        grid=(b, l // tm),
        in_specs=[pl.BlockSpec((1, tm, MLA_Q_LORA), lambda i, j: (i, j, cq_blk)),
                  pl.BlockSpec((1, tm, MLA_KV_LORA), lambda i, j: (i, j, ckv_blk)),
                  pl.BlockSpec((1, tm, LANES), lambda i, j: (i, j, kr_blk)),
                  pl.BlockSpec((1, 1, tm), lambda i, j: (i, 0, j))] + [const(a) for a in consts]
                 + [pl.BlockSpec(*_layer_block(w, layer)) for w in weights],
        out_specs=[pl.BlockSpec((1, hh, MLA_QK_PAD, tm), lambda i, j: (i, 0, 0, j)),
                   pl.BlockSpec((1, hh, tm, MLA_NOPE), lambda i, j: (i, 0, j, 0)),
                   pl.BlockSpec((1, tm, LANES), lambda i, j: (i, j, 0)),
                   pl.BlockSpec((1, hh, MLA_VL, tm), lambda i, j: (i, 0, 0, j))],
        out_shape=[jax.ShapeDtypeStruct((b, hh, MLA_QK_PAD, l), BF16),
                   jax.ShapeDtypeStruct((b, hh, l, MLA_NOPE), BF16),
                   jax.ShapeDtypeStruct((b, l, LANES), BF16),
                   jax.ShapeDtypeStruct((b, hh, MLA_VL, l), BF16)],
        scratch_shapes=[pltpu.VMEM((hh * (MLA_NOPE + MLA_ROPE), MLA_Q_LORA), BF16),
                        pltpu.VMEM((MLA_KV_LORA, hh * MLA_NOPE), BF16),
                        pltpu.VMEM((hh * MLA_V, MLA_KV_LORA), BF16)],
        compiler_params=_cparams(("arbitrary", "arbitrary")),
        name="mla_qkv",
    )(proj, proj, proj, positions.reshape(b, 1, l), *consts, *weights)


def _flash_kernel(qt_ref, kn_ref, kr_ref, vt_ref, o_ref, m_ref, acc_ref, *, tq, hp, ahead_full, ahead_diagonal):
    qi = pl.program_id(2)
    m_ref[...] = jnp.full(m_ref.shape, -jnp.inf, F32)
    acc_ref[...] = jnp.zeros(acc_ref.shape, F32)

    half = tq // 2
    lower = (lax.broadcasted_iota(jnp.int32, (half, half), 0)
             <= lax.broadcasted_iota(jnp.int32, (half, half), 1))

    def blocks(j, parts, diagonal):
        ahead = ahead_diagonal if diagonal else ahead_full
        base = pl.multiple_of(j * tq, tq)
        items = [(h, pl.ds(base + k0, nk), slice(q0, q0 + nq)) for k0, nk, q0, nq in parts for h in range(hp)]

        def scores(h, rows, cols):
            k = jnp.concatenate([kn_ref[0, h, rows, :], kr_ref[0, rows, :MLA_ROPE]], axis=-1)
            return jnp.dot(k, qt_ref[0, h, :, cols], preferred_element_type=F32)

        pending = [scores(*it) for it in items[:ahead]]
        for n, (h, rows, cols) in enumerate(items):
            if n + ahead < len(items):
                pending.append(scores(*items[n + ahead]))
            s = pending[n]
            if diagonal:
                square = jnp.where(lower, s[:, :half], jnp.finfo(F32).min)
                s = square if s.shape[1] == half else jnp.concatenate([square, s[:, half:]], axis=1)
            m = m_ref[h, :, cols]
            m_new = jnp.maximum(m, jnp.max(s, axis=0, keepdims=True))
            alpha = jnp.exp2(m - m_new)
            p = jnp.exp2(s - m_new)
            acc_ref[h, :, cols] = alpha * acc_ref[h, :, cols] + jnp.dot(
                vt_ref[0, h, :, rows], p.astype(BF16), preferred_element_type=F32)
            m_ref[h, :, cols] = m_new

    def body(j, carry):
        blocks(j, [(0, half, 0, tq), (half, half, 0, tq)], False)
        return carry

    lax.fori_loop(0, qi, body, 0)
    blocks(qi, [(0, half, 0, tq), (half, half, half, half)], True)
    for h in range(hp):
        o_ref[0, :, h * MLA_V:(h + 1) * MLA_V] = (
            acc_ref[h, :MLA_V, :] / acc_ref[h, MLA_V:MLA_V + 1, :]).T.astype(o_ref.dtype)


def _flash(qt, kn, kr, vt, *, tq, hp, ahead_full, ahead_diagonal):
    b, hh, _, l = qt.shape
    return pl.pallas_call(
        functools.partial(_flash_kernel, tq=tq, hp=hp, ahead_full=ahead_full, ahead_diagonal=ahead_diagonal),
        grid=(b, hh // hp, l // tq),
        in_specs=[pl.BlockSpec((1, hp, MLA_QK_PAD, tq), lambda i, h, j: (i, h, 0, j)),
                  pl.BlockSpec((1, hp, l, MLA_NOPE), lambda i, h, j: (i, h, 0, 0)),
                  pl.BlockSpec((1, l, LANES), lambda i, h, j: (i, 0, 0)),
                  pl.BlockSpec((1, hp, MLA_VL, l), lambda i, h, j: (i, h, 0, 0))],
        out_specs=pl.BlockSpec((1, tq, hp * MLA_V), lambda i, h, j: (i, j, h)),
        out_shape=jax.ShapeDtypeStruct((b, l, hh * MLA_V), BF16),
        scratch_shapes=[pltpu.VMEM((hp, 1, tq), F32), pltpu.VMEM((hp, MLA_VL, tq), F32)],
        compiler_params=_cparams(("parallel", "parallel", "parallel")),
        name="flash",
    )(qt, kn, kr, vt)


def _s5_layer(x, ln, w_in, lam_re, lam_im, log_step, b_re, b_im, c_re, c_im, d, w_glu,
              w_out, mem_kv, xq_norm, layer, j):
    b, l, dm = x.shape
    tm = 512
    proj, xn = _s5_in_proj(x, ln, w_in, j, tm=tm, col_chunk=512)
    xg = _s5_ut(xn, w_in, j)
    toep, wout, wst, dcol = _s5_asm(lam_re, lam_im, log_step, c_re, c_im, b_re, b_im, d)
    pw_re, pw_im = _s5_pow(lam_re, lam_im, log_step)
    col = lambda pw: pw.transpose(1, 2, 0)
    yt = _s5_mix(xg, toep, wst, wout, col(pw_re), col(pw_im), dcol, nb=b, gs=8)
    y = _glu(yt, w_glu, j, nb=b, col_chunk=256)
    return _merge(x, y, proj, BRANCH_WIDTH // XQ_WIDTH, 0, *mem_kv, xq_norm, w_out, layer, tm=tm, phased=True)


def _mla_layer(x, positions, ln, w_in, q_lora_norm, kv_lora_norm, w_uq, w_ukv, q_nope_norm, k_nope_norm,
               q_rope_norm, k_rope_norm, w_out, mem_kv, xq_norm, layer, j):
    b, l, dm = x.shape
    o1 = MLA_Q_LORA
    o2 = o1 + MLA_KV_LORA
    o3 = o2 + MLA_ROPE
    o4 = o3 + XQ_WIDTH
    segments = ((o4, BRANCH_WIDTH), (0, o1), (o3, XQ_WIDTH), (o1, MLA_KV_LORA), (o2, MLA_ROPE))
    wout = -(-(o4 + BRANCH_WIDTH) // 512) * 512
    proj = _mla_in_proj(x.reshape(b * l, dm), ln, w_in, j, segments, wout, tm=512, col_chunk=512)
    proj = proj.reshape(b, l, -1)
    gate_blk = 0
    cq_blk = BRANCH_WIDTH // MLA_Q_LORA
    xq_blk = (BRANCH_WIDTH + MLA_Q_LORA) // XQ_WIDTH
    ckv_blk = (BRANCH_WIDTH + MLA_Q_LORA + XQ_WIDTH) // MLA_KV_LORA
    kr_blk = (BRANCH_WIDTH + MLA_Q_LORA + XQ_WIDTH + MLA_KV_LORA) // LANES
    qt, kn, kr, vt = _mla_qkv(proj, cq_blk, ckv_blk, kr_blk, positions,
                              (q_lora_norm, kv_lora_norm, q_nope_norm, k_nope_norm, q_rope_norm, k_rope_norm),
                              w_uq, w_ukv, j, tm=512)
    attn = _flash(qt, kn, kr, vt, tq=512, hp=12, ahead_full=2, ahead_diagonal=4)
    return _merge(x, attn, proj, xq_blk, gate_blk, *mem_kv, xq_norm, w_out, layer, tm=512)


def kernel(x, mem, positions, ln_gain, w_out, mem_norm, w_mem_kv, xq_norm, xk_norm,
           s5_w_in, s5_lambda_re, s5_lambda_im, s5_log_step, s5_b_re, s5_b_im, s5_c_re, s5_c_im,
           s5_d, s5_w_glu, mla_w_in, mla_q_lora_norm, mla_kv_lora_norm, mla_w_uq, mla_w_ukv,
           mla_q_nope_norm, mla_k_nope_norm, mla_q_rope_norm, mla_k_rope_norm):
    depth = ln_gain.shape[0]
    mem_kv = _mem_kv(mem, mem_norm, w_mem_kv, xk_norm)
    for i in range(depth):
        j = i // 2
        if i % 2 == 0:
            x = _s5_layer(x, ln_gain[i], s5_w_in, s5_lambda_re[j], s5_lambda_im[j], s5_log_step[j],
                          s5_b_re[j], s5_b_im[j], s5_c_re[j], s5_c_im[j], s5_d[j], s5_w_glu,
                          w_out, mem_kv, xq_norm, i, j)
        else:
            x = _mla_layer(x, positions, ln_gain[i], mla_w_in, mla_q_lora_norm[j], mla_kv_lora_norm[j],
                           mla_w_uq, mla_w_ukv, mla_q_nope_norm[j], mla_k_nope_norm[j],
                           mla_q_rope_norm[j], mla_k_rope_norm[j],
                           w_out, mem_kv, xq_norm, i, j)
    return x
```

```python
import functools
import math

import jax
import jax.numpy as jnp
from jax import lax
from jax.experimental import pallas as pl
from jax.experimental.pallas import tpu as pltpu

D_MODEL = 1024
BRANCH_WIDTH = 2 * D_MODEL
XQ_WIDTH = BRANCH_WIDTH // 4
PRIMARY_WIDTH = BRANCH_WIDTH - XQ_WIDTH
X_HEADS = 4
X_HEAD_DIM = XQ_WIDTH // X_HEADS
S5_GROUP_CH = 16
S5_GROUPS = PRIMARY_WIDTH // S5_GROUP_CH
S5_STATE = 64
MLA_NOPE = 128
MLA_ROPE = 64
MLA_V = 128
MLA_HEADS = PRIMARY_WIDTH // MLA_V
MLA_Q_LORA = D_MODEL // 2
MLA_KV_LORA = D_MODEL // 4
ROPE_THETA = 10000.0
EPS = 1e-6

LANES = 128
MLA_QK_PAD = MLA_NOPE + MLA_ROPE
F32_SUBLANES = 8
BF16_SUBLANES = 16
MLA_VL = MLA_V + BF16_SUBLANES
S5_CHUNK = 2 * LANES // S5_GROUP_CH
S5_SCAN_EXPONENTS = [S5_CHUNK * 2 ** i for i in range(int(math.log2(LANES)))]
VMEM_LIMIT = 56 * 1024 * 1024

F32 = jnp.float32
BF16 = jnp.bfloat16


def _cparams(sem):
    return pltpu.CompilerParams(dimension_semantics=sem, vmem_limit_bytes=VMEM_LIMIT)


def _rms(x, g):
    return x * lax.rsqrt(jnp.mean(x * x, axis=-1, keepdims=True) + EPS) * g


def _layer_block(w, j, block=None, index=None):
    block = tuple(w.shape[1:]) if block is None else block
    index = (0,) * len(block) if index is None else index
    return (None,) + block, lambda *_: (j,) + index


def _mla_in_proj_kernel(x_ref, g_ref, w_ref, o_ref, wp_ref, *, segments, col_chunk):
    @pl.when(pl.program_id(0) == 0)
    def _():
        at = 0
        for start, width in segments:
            wp_ref[:, at:at + width] = w_ref[:, start:start + width].astype(BF16)
            at += width
        wp_ref[:, at:] = jnp.zeros((wp_ref.shape[0], wp_ref.shape[1] - at), BF16)

    xn = _rms(x_ref[...], g_ref[...]).astype(BF16)
    for c in range(o_ref.shape[1] // col_chunk):
        sl = slice(c * col_chunk, (c + 1) * col_chunk)
        o_ref[:, sl] = jnp.dot(xn, wp_ref[:, sl], preferred_element_type=F32).astype(o_ref.dtype)


def _mla_in_proj(x, g, w, j, segments, wout, *, tm, col_chunk):
    n, d = x.shape
    return pl.pallas_call(
        functools.partial(_mla_in_proj_kernel, segments=segments, col_chunk=col_chunk),
        grid=(n // tm,),
        in_specs=[pl.BlockSpec((tm, d), lambda i: (i, 0)),
                  pl.BlockSpec((1, d), lambda i: (0, 0)),
                  pl.BlockSpec(*_layer_block(w, j), pipeline_mode=pl.Buffered(1))],
        out_specs=pl.BlockSpec((tm, wout), lambda i: (i, 0)),
        out_shape=jax.ShapeDtypeStruct((n, wout), BF16),
        scratch_shapes=[pltpu.VMEM((d, wout), BF16)],
        compiler_params=_cparams(("arbitrary",)),
        name="mla_in_proj",
    )(x, g.reshape(1, d), w)


def _to_phase_order(a):
    n, d = a.shape
    return jnp.swapaxes(a.reshape(n // S5_CHUNK, S5_CHUNK, d), 0, 1).reshape(n, d)


def _from_phase_order(a):
    n, d = a.shape
    return jnp.swapaxes(a.reshape(S5_CHUNK, n // S5_CHUNK, d), 0, 1).reshape(n, d)


def _s5_in_proj_kernel(x_ref, g_ref, wg_ref, wx_ref, o_ref, xn_ref, wb_ref, *, tm, col_chunk):
    @pl.when((pl.program_id(0) == 0) & (pl.program_id(1) == 0))
    def _():
        wb_ref[:, :BRANCH_WIDTH] = wg_ref[...].astype(BF16)
        wb_ref[:, BRANCH_WIDTH:] = wx_ref[...].astype(BF16)

    d = x_ref.shape[2]
    nc = tm // S5_CHUNK
    chunks = [slice(c * col_chunk, (c + 1) * col_chunk) for c in range((BRANCH_WIDTH + XQ_WIDTH) // col_chunk)]
    weights = [wb_ref[:, sl] for sl in chunks]

    def normalise(k):
        xn = _rms(_to_phase_order(x_ref[0, k * tm:(k + 1) * tm, :]), g_ref[...]).astype(BF16)
        for s in range(S5_CHUNK):
            xn_ref[0, k * nc:(k + 1) * nc, s * d:(s + 1) * d] = xn[s * nc:(s + 1) * nc]
        return xn

    def project(k, xn):
        for sl, w in zip(chunks, weights):
            o_ref[0, k * tm:(k + 1) * tm, sl] = jnp.dot(xn, w, preferred_element_type=F32).astype(o_ref.dtype)

    n_sub = x_ref.shape[1] // tm
    prev = None
    for k in range(n_sub):
        if prev is not None:
            project(k - 1, prev)
        prev = normalise(k)
    project(n_sub - 1, prev)


def _s5_in_proj(x, g, w_in, j, *, tm, col_chunk):
    b, l, d = x.shape
    wout = BRANCH_WIDTH + XQ_WIDTH
    tg = 2 * tm
    nc = tg // S5_CHUNK
    return pl.pallas_call(
        functools.partial(_s5_in_proj_kernel, tm=tm, col_chunk=col_chunk),
        grid=(b, l // tg),
        in_specs=[pl.BlockSpec((1, tg, d), lambda i, j: (i, j, 0)),
                  pl.BlockSpec((1, d), lambda i, j: (0, 0)),
                  pl.BlockSpec(*_layer_block(w_in, j, (d, BRANCH_WIDTH),
                                             (0, (PRIMARY_WIDTH + XQ_WIDTH) // BRANCH_WIDTH)),
                               pipeline_mode=pl.Buffered(1)),
                  pl.BlockSpec(*_layer_block(w_in, j, (d, XQ_WIDTH), (0, PRIMARY_WIDTH // XQ_WIDTH)),
                               pipeline_mode=pl.Buffered(1))],
        out_specs=[pl.BlockSpec((1, tg, wout), lambda i, j: (i, j, 0)),
                   pl.BlockSpec((1, nc, S5_CHUNK * d), lambda i, j: (i, j, 0))],
        out_shape=[jax.ShapeDtypeStruct((b, l, wout), BF16),
                   jax.ShapeDtypeStruct((b, l // S5_CHUNK, S5_CHUNK * d), BF16)],
        scratch_shapes=[pltpu.VMEM((d, wout), BF16)],
        compiler_params=_cparams(("arbitrary", "arbitrary")),
        name="s5_in_proj",
    )(x, g.reshape(1, d), w_in, w_in)


def _mem_kv_kernel(m_ref, g_ref, w_ref, kg_ref, k_ref, v_ref):
    b, m, d = m_ref.shape
    w = w_ref[0].astype(BF16)
    for i in range(b):
        mn = _rms(m_ref[i], g_ref[0]).astype(BF16)
        kv = jnp.dot(mn, w, preferred_element_type=F32)
        for h in range(X_HEADS):
            sl = slice(h * X_HEAD_DIM, (h + 1) * X_HEAD_DIM)
            k_ref[0, i, :, sl] = _rms(kv[:, sl], kg_ref[0]).astype(BF16)
        v_ref[0, i] = kv[:, XQ_WIDTH:].astype(BF16)


def _mem_kv(mem, mem_norm, w_mem_kv, xk_norm):
    b, m, d = mem.shape
    depth = w_mem_kv.shape[0]
    out = jax.ShapeDtypeStruct((depth, b, m, XQ_WIDTH), BF16)
    return pl.pallas_call(
        _mem_kv_kernel,
        grid=(depth,),
        in_specs=[pl.BlockSpec((b, m, d), lambda n: (0, 0, 0)),
                  pl.BlockSpec((1, 1, d), lambda n: (n, 0, 0)),
                  pl.BlockSpec((1, d, 2 * XQ_WIDTH), lambda n: (n, 0, 0)),
                  pl.BlockSpec((1, 1, X_HEAD_DIM), lambda n: (n, 0, 0))],
        out_specs=[pl.BlockSpec((1, b, m, XQ_WIDTH), lambda n: (n, 0, 0, 0)),
                   pl.BlockSpec((1, b, m, XQ_WIDTH), lambda n: (n, 0, 0, 0))],
        out_shape=[out, out],
        compiler_params=_cparams(("parallel",)),
        name="mem_kv",
    )(mem, mem_norm.reshape(depth, 1, d), w_mem_kv, xk_norm.reshape(depth, 1, X_HEAD_DIM))


def _s5_pow_kernel(lr_ref, li_ref, ls_ref, pr_ref, pi_ref):
    step = jnp.exp(ls_ref[...])
    zr, zi = lr_ref[...] * step, li_ref[...] * step
    for n in range(pr_ref.shape[0]):
        if n < len(S5_SCAN_EXPONENTS):
            e = S5_SCAN_EXPONENTS[n]
            mag = jnp.exp(zr * e)
            pr_ref[n] = mag * jnp.cos(zi * e)
            pi_ref[n] = mag * jnp.sin(zi * e)
        else:
            pr_ref[n] = jnp.zeros_like(zr)
            pi_ref[n] = jnp.zeros_like(zr)


def _s5_pow(lam_re, lam_im, log_step):
    g, p = lam_re.shape
    slots = -(-len(S5_SCAN_EXPONENTS) // F32_SUBLANES) * F32_SUBLANES
    out = jax.ShapeDtypeStruct((slots, g, p), F32)
    return pl.pallas_call(_s5_pow_kernel, out_shape=[out, out], name="s5_pow")(
        lam_re, lam_im, log_step.reshape(g, 1))


def _s5_asm_kernel(lr_ref, li_ref, ls_ref, cr_ref, ci_ref, btr_ref, bti_ref, br_ref, bi_ref, d_ref,
                   toep_ref, wout_ref, wst_ref, dcol_ref):
    def group(i, carry):
        _s5_asm_group(i, lr_ref, li_ref, ls_ref, cr_ref, ci_ref, btr_ref, bti_ref, br_ref, bi_ref,
                      toep_ref, wout_ref, wst_ref)
        dcol_ref[i] = jnp.concatenate([d_ref[i]] * S5_CHUNK, axis=0)
        return carry

    lax.fori_loop(0, cr_ref.shape[0], group, 0)


def _dot_3pass_tiled(a, b):
    c = b.shape[1]
    rep = ((lax.broadcasted_iota(jnp.int32, (c, S5_CHUNK * c), 1) & (c - 1))
           == lax.broadcasted_iota(jnp.int32, (c, S5_CHUNK * c), 0)).astype(BF16)
    dot = functools.partial(jnp.dot, preferred_element_type=F32)
    a_hi, b_hi = a.astype(BF16), b.astype(BF16)
    a_lo = (a - a_hi.astype(F32)).astype(BF16)
    b_lo = (b - b_hi.astype(F32)).astype(BF16)
    b_hi, b_lo = dot(b_hi, rep).astype(BF16), dot(b_lo, rep).astype(BF16)
    return dot(a_hi, b_hi) + (dot(a_hi, b_lo) + dot(a_lo, b_hi))


def _s5_asm_group(i, lr_ref, li_ref, ls_ref, cr_ref, ci_ref, btr_ref, bti_ref, br_ref, bi_ref,
                  toep_ref, wout_ref, wst_ref):
    t = S5_CHUNK
    lr, li = lr_ref[i], li_ref[i]
    step = jnp.exp(ls_ref[i])
    zr, zi = lr * step, li * step
    rows = -(-(t + 1) // F32_SUBLANES) * F32_SUBLANES
    e = lax.broadcasted_iota(jnp.int32, (rows, lr.shape[1]), 0).astype(F32)
    mag = jnp.exp(e * zr)
    pr, pi = mag * jnp.cos(e * zi), mag * jnp.sin(e * zi)
    den = lr * lr + li * li
    mr = ((pr[1:2] - 1.0) * lr + pi[1:2] * li) / den
    mi = (pi[1:2] * lr - (pr[1:2] - 1.0) * li) / den
    cr, ci = cr_ref[i], ci_ref[i]
    btr, bti = btr_ref[i], bti_ref[i]
    amr = pr[:t] * mr - pi[:t] * mi
    ami = pr[:t] * mi + pi[:t] * mr
    l_re, l_im, w_re, w_im, o_re, o_im = [], [], [], [], [], []
    for k in range(t):
        ar, ai = amr[k:k + 1], ami[k:k + 1]
        l_re.append(cr * ar - ci * ai)
        l_im.append(-(cr * ai + ci * ar))
        ar, ai = amr[t - 1 - k:t - k], ami[t - 1 - k:t - k]
        w_re.append(btr * ar - bti * ai)
        w_im.append(btr * ai + bti * ar)
        ar, ai = pr[k + 1:k + 2], pi[k + 1:k + 2]
        o_re.append(cr * ar - ci * ai)
        o_im.append(-(cr * ai + ci * ar))
    cat = lambda parts: jnp.concatenate(parts, axis=0)
    kt = _dot_3pass_tiled(jnp.concatenate([cat(l_re), cat(l_im)], axis=1),
                          jnp.concatenate([br_ref[i], bi_ref[i]], axis=0))
    n = kt.shape[0]
    blk = lax.shift_right_logical(lax.broadcasted_iota(jnp.int32, kt.shape, 1), int(math.log2(S5_GROUP_CH)))
    toep = jnp.where(blk == 0, kt, 0.0)
    for s in range(1, t):
        shifted = jnp.concatenate([jnp.zeros((s * S5_GROUP_CH, n), F32), kt[:n - s * S5_GROUP_CH]], axis=0)
        toep = jnp.where(blk == s, shifted, toep)
    toep_ref[i] = toep.astype(BF16)
    wout_ref[i] = jnp.concatenate([cat(o_re), cat(o_im)], axis=1).astype(BF16)
    wst_ref[i] = jnp.concatenate([cat(w_re), cat(w_im)], axis=1).T.astype(BF16)


def _s5_asm(lam_re, lam_im, log_step, c_re, c_im, b_re, b_im, d):
    g, c, p = c_re.shape
    tc = S5_CHUNK * c
    gs = 8
    blk = lambda a: pl.BlockSpec((gs,) + a.shape[1:], lambda i: (i, 0, 0))
    bt_re, bt_im = b_re.transpose(0, 2, 1), b_im.transpose(0, 2, 1)
    args = (lam_re.reshape(g, 1, p), lam_im.reshape(g, 1, p), log_step.reshape(g, 1, 1),
            c_re, c_im, bt_re, bt_im, b_re, b_im, d.reshape(g, c, 1))
    return pl.pallas_call(
        _s5_asm_kernel,
        grid=(g // gs,),
        in_specs=[blk(a) for a in args],
        out_specs=[pl.BlockSpec((gs, tc, tc), lambda i: (i, 0, 0)),
                   pl.BlockSpec((gs, tc, 2 * p), lambda i: (i, 0, 0)),
                   pl.BlockSpec((gs, 2 * p, tc), lambda i: (i, 0, 0)),
                   pl.BlockSpec((gs, tc, 1), lambda i: (i, 0, 0))],
        out_shape=[jax.ShapeDtypeStruct((g, tc, tc), BF16), jax.ShapeDtypeStruct((g, tc, 2 * p), BF16),
                   jax.ShapeDtypeStruct((g, 2 * p, tc), BF16), jax.ShapeDtypeStruct((g, tc, 1), F32)],
        compiler_params=_cparams(("parallel",)),
        name="s5_asm",
    )(*args)


def _s5_ut_kernel(xn_ref, w_ref, o_ref, wt_ref):
    @pl.when(pl.program_id(0) == 0)
    def _():
        wt_ref[...] = w_ref[...].T.astype(BF16)

    nb, nc, d = xn_ref.shape
    ut = lax.dot_general(wt_ref[...], xn_ref[...].reshape(nb * nc, d), (((1,), (1,)), ((), ())),
                         preferred_element_type=F32)
    o_ref[...] = ut.astype(BF16).reshape(o_ref.shape)


def _s5_ut(xn, w_in, j):
    b, nc, td = xn.shape
    d = td // S5_CHUNK
    return pl.pallas_call(
        _s5_ut_kernel,
        grid=(S5_CHUNK,),
        in_specs=[pl.BlockSpec((b, nc, d), lambda s: (0, 0, s)),
                  pl.BlockSpec(*_layer_block(w_in, j, (d, PRIMARY_WIDTH)))],
        out_specs=pl.BlockSpec((S5_GROUPS, S5_GROUP_CH, b * nc), lambda s: (0, s, 0)),
        out_shape=jax.ShapeDtypeStruct((S5_GROUPS, S5_CHUNK * S5_GROUP_CH, b * nc), BF16),
        scratch_shapes=[pltpu.VMEM((PRIMARY_WIDTH, d), BF16)],
        compiler_params=_cparams(("arbitrary",)),
        name="s5_ut",
    )(xn, w_in)


def _s5_mix_kernel(x_ref, toep_ref, wst_ref, wout_ref, sr_ref, si_ref, d_ref, o_ref, *, nb):
    p = S5_STATE
    gs = x_ref.shape[0]
    lane = lax.broadcasted_iota(jnp.int32, (p, LANES), 1)
    n_steps = int(math.log2(LANES))

    def scan(g, hloc):
        pw = []
        for i in range(n_steps):
            keep = lane >= (1 << i)
            pw.append((jnp.where(keep, jnp.broadcast_to(sr_ref[g, :, i:i + 1], (p, LANES)), 0.0),
                       jnp.where(keep, jnp.broadcast_to(si_ref[g, :, i:i + 1], (p, LANES)), 0.0)))
        h_re = [hloc[:p, b * LANES:(b + 1) * LANES] for b in range(nb)]
        h_im = [hloc[p:, b * LANES:(b + 1) * LANES] for b in range(nb)]
        for i in range(n_steps):
            ar, ai = pw[i]
            r_sh = [pltpu.roll(v, 1 << i, 1) for v in h_re]
            i_sh = [pltpu.roll(v, 1 << i, 1) for v in h_im]
            h_re = [h_re[b] + ar * r_sh[b] - ai * i_sh[b] for b in range(nb)]
            h_im = [h_im[b] + ar * i_sh[b] + ai * r_sh[b] for b in range(nb)]
        h_re = [jnp.where(lane >= 1, pltpu.roll(v, 1, 1), 0.0) for v in h_re]
        h_im = [jnp.where(lane >= 1, pltpu.roll(v, 1, 1), 0.0) for v in h_im]
        return jnp.concatenate([jnp.concatenate(h_re, axis=1), jnp.concatenate(h_im, axis=1)],
                               axis=0).astype(BF16)

    def outputs(g, h):
        x = x_ref[g]
        y = (jnp.dot(toep_ref[g], x, preferred_element_type=F32)
             + jnp.dot(wout_ref[g], h, preferred_element_type=F32)
             + d_ref[g] * x.astype(F32))
        o_ref[:, g * S5_GROUP_CH:(g + 1) * S5_GROUP_CH, :] = (
            jax.nn.gelu(y).astype(o_ref.dtype).reshape(S5_CHUNK, S5_GROUP_CH, y.shape[1]))

    h_prev = None
    for g in range(gs):
        hloc = jnp.dot(wst_ref[g], x_ref[g], preferred_element_type=F32)
        h = scan(g, hloc)
        if h_prev is not None:
            outputs(g - 1, h_prev)
        h_prev = h
    outputs(gs - 1, h_prev)


def _s5_mix(xg, toep, wst, wout, sc_re, sc_im, dcol, *, nb, gs):
    g, tc, cols = xg.shape
    assert cols == nb * LANES, "one batch's chunks must fill exactly one 128-lane block"
    blk = lambda a: pl.BlockSpec((gs,) + a.shape[1:], lambda i: (i, 0, 0))
    return pl.pallas_call(
        functools.partial(_s5_mix_kernel, nb=nb),
        grid=(g // gs,),
        in_specs=[blk(a) for a in (xg, toep, wst, wout, sc_re, sc_im, dcol)],
        out_specs=pl.BlockSpec((S5_CHUNK, gs * S5_GROUP_CH, cols), lambda i: (0, i, 0)),
        out_shape=jax.ShapeDtypeStruct((S5_CHUNK, g * S5_GROUP_CH, cols), BF16),
        compiler_params=_cparams(("parallel",)),
        name="s5_mix",
    )(xg, toep, wst, wout, sc_re, sc_im, dcol)


def _glu_kernel(y_ref, w_ref, o_ref, *, col_chunk):
    y = y_ref[0].T
    half = o_ref.shape[-1]
    for c in range(half // col_chunk):
        wa = w_ref[:, c * col_chunk:(c + 1) * col_chunk].astype(BF16)
        wg = w_ref[:, half + c * col_chunk:half + (c + 1) * col_chunk].astype(BF16)
        a = jnp.dot(y, wa, preferred_element_type=F32)
        g = jnp.dot(y, wg, preferred_element_type=F32)
        o_ref[:, :, c * col_chunk:(c + 1) * col_chunk] = (
            (a * jax.nn.sigmoid(g)).astype(o_ref.dtype).reshape(o_ref.shape[:2] + (col_chunk,)))


def _glu(yt, w, layer, *, nb, col_chunk):
    t, k, cols = yt.shape
    nc = cols // nb
    half = w.shape[2] // 2
    return pl.pallas_call(
        functools.partial(_glu_kernel, col_chunk=col_chunk),
        grid=(t,),
        in_specs=[pl.BlockSpec((1, k, cols), lambda j: (j, 0, 0)),
                  pl.BlockSpec(*_layer_block(w, layer), pipeline_mode=pl.Buffered(1))],
        out_specs=pl.BlockSpec((nb, nc, half), lambda j: (0, 0, j)),
        out_shape=jax.ShapeDtypeStruct((nb, nc, t * half), BF16),
        compiler_params=_cparams(("parallel",)),
        name="glu",
    )(yt, w)


def _merge_kernel(x_ref, mix_ref, xq_ref, gate_ref, k_ref, v_ref, qg_ref, w_ref, o_ref, wb_ref, *cat_refs,
                  tm, phased):
    scale = X_HEAD_DIM ** -0.5
    nc = tm // S5_CHUNK
    @pl.when((pl.program_id(0) == 0) & (pl.program_id(1) == 0))
    def _():
        wb_ref[...] = w_ref[...].astype(BF16)

    def gather(k, cat_ref):
        r0 = k * tm
        gate = gate_ref[0, r0:r0 + tm, :]
        sg = gate * jax.nn.sigmoid(gate)
        if phased:
            for s in range(S5_CHUNK):
                rows = slice(s * nc, (s + 1) * nc)
                mix = mix_ref[0, k * nc:(k + 1) * nc, s * PRIMARY_WIDTH:(s + 1) * PRIMARY_WIDTH]
                cat_ref[rows, :PRIMARY_WIDTH] = mix * sg[rows, :PRIMARY_WIDTH]
        else:
            cat_ref[:, :PRIMARY_WIDTH] = mix_ref[0, r0:r0 + tm, :] * sg[:, :PRIMARY_WIDTH]
        for h in range(X_HEADS):
            sl = slice(h * X_HEAD_DIM, (h + 1) * X_HEAD_DIM)
            q = _rms(xq_ref[0, r0:r0 + tm, sl].astype(F32), qg_ref[...]).astype(BF16)
            s = lax.dot_general(q, k_ref[0, :, sl], (((1,), (1,)), ((), ())), preferred_element_type=F32) * scale
            p = jnp.exp(s - jnp.max(s, axis=-1, keepdims=True))
            p = (p / jnp.sum(p, axis=-1, keepdims=True)).astype(BF16)
            mo = jnp.dot(p, v_ref[0, :, sl], preferred_element_type=F32)
            osl = slice(PRIMARY_WIDTH + h * X_HEAD_DIM, PRIMARY_WIDTH + (h + 1) * X_HEAD_DIM)
            cat_ref[:, osl] = mo.astype(BF16) * sg[:, osl]

    def project(k, cat_ref):
        r0 = k * tm
        delta = (jnp.dot(cat_ref[:, :PRIMARY_WIDTH], wb_ref[:PRIMARY_WIDTH, :], preferred_element_type=F32)
                 + jnp.dot(cat_ref[:, PRIMARY_WIDTH:], wb_ref[PRIMARY_WIDTH:, :], preferred_element_type=F32))
        o_ref[0, r0:r0 + tm, :] = x_ref[0, r0:r0 + tm, :] + (_from_phase_order(delta) if phased else delta)

    n_sub = len(cat_refs)
    for k in range(n_sub):
        gather(k, cat_refs[k])
        if k > 0:
            project(k - 1, cat_refs[k - 1])
    project(n_sub - 1, cat_refs[n_sub - 1])


def _merge(x, mix, proj, xq_blk, gate_blk, mk, mv, xq_norm, w_out, layer, *, tm, phased=False):
    b, l, d = x.shape
    m = mk.shape[2]
    n_sub = 2
    tg = n_sub * tm
    mix_spec = (pl.BlockSpec((1, tg // S5_CHUNK, S5_CHUNK * PRIMARY_WIDTH), lambda i, j: (i, j, 0)) if phased
                else pl.BlockSpec((1, tg, PRIMARY_WIDTH), lambda i, j: (i, j, 0)))
    return pl.pallas_call(
        functools.partial(_merge_kernel, tm=tm, phased=phased),
        grid=(b, l // tg),
        in_specs=[pl.BlockSpec((1, tg, d), lambda i, j: (i, j, 0)),
                  mix_spec,
                  pl.BlockSpec((1, tg, XQ_WIDTH), lambda i, j: (i, j, xq_blk)),
                  pl.BlockSpec((1, tg, BRANCH_WIDTH), lambda i, j: (i, j, gate_blk)),
                  pl.BlockSpec((None, 1, m, XQ_WIDTH), lambda i, j: (layer, i, 0, 0)),
                  pl.BlockSpec((None, 1, m, XQ_WIDTH), lambda i, j: (layer, i, 0, 0)),
                  pl.BlockSpec((None, 1, X_HEAD_DIM), lambda i, j: (layer, 0, 0)),
                  pl.BlockSpec((None, BRANCH_WIDTH, d), lambda i, j: (layer, 0, 0),
                               pipeline_mode=pl.Buffered(1))],
        out_specs=pl.BlockSpec((1, tg, d), lambda i, j: (i, j, 0)),
        out_shape=jax.ShapeDtypeStruct((b, l, d), F32),
        scratch_shapes=[pltpu.VMEM((BRANCH_WIDTH, d), BF16)] + [pltpu.VMEM((tm, BRANCH_WIDTH), BF16)] * n_sub,
        compiler_params=_cparams(("arbitrary", "arbitrary")),
        name="merge",
    )(x, mix, proj, proj, mk, mv, xq_norm.reshape(-1, 1, X_HEAD_DIM), w_out)


def _mla_qkv_kernel(cq_ref, ckv_ref, kr_ref, posr_ref, invfc_ref, gq_ref, gkv_ref, gqn_ref,
                    gkn_ref, gqr_ref, gkr_ref, wuq_ref, wukv_ref, qt_ref, kn_ref, krope_ref, vt_ref,
                    wqt_ref, wk_ref, wvt_ref):
    half = MLA_ROPE // 2
    tm = cq_ref.shape[1]
    qk = MLA_NOPE + MLA_ROPE

    @pl.when((pl.program_id(0) == 0) & (pl.program_id(1) == 0))
    def _():
        wqt_ref[...] = wuq_ref[...].T.astype(BF16)
        for h in range(MLA_HEADS):
            c0 = h * (MLA_NOPE + MLA_V)
            wk_ref[:, h * MLA_NOPE:(h + 1) * MLA_NOPE] = wukv_ref[:, c0:c0 + MLA_NOPE].astype(BF16)
            wvt_ref[h * MLA_V:(h + 1) * MLA_V, :] = wukv_ref[:, c0 + MLA_NOPE:c0 + MLA_NOPE + MLA_V].T.astype(BF16)

    qscale = (MLA_NOPE + MLA_ROPE) ** -0.5 * math.log2(math.e)

    cq = _rms(cq_ref[0].astype(F32), gq_ref[...])
    ckv = _rms(ckv_ref[0].astype(F32), gkv_ref[...])
    cq_t = cq.T.astype(BF16)
    ckv_t = ckv.T.astype(BF16)
    ckv_b = ckv.astype(BF16)

    def project(h):
        dot = functools.partial(jnp.dot, preferred_element_type=F32)
        k_pair = dot(ckv_b, wk_ref[:, h * MLA_NOPE:(h + 2) * MLA_NOPE]) if h % 2 == 0 else None
        return (dot(wqt_ref[h * qk:(h + 1) * qk, :], cq_t),
                dot(wvt_ref[h * MLA_V:(h + 1) * MLA_V, :], ckv_t), k_pair)

    ang_t = invfc_ref[...] * posr_ref[0].astype(F32)
    cos_t, sin_t = jnp.cos(ang_t), jnp.sin(ang_t)
    g_nope = jnp.broadcast_to(gqn_ref[...], (MLA_NOPE, tm)) * qscale
    g_r1 = jnp.broadcast_to(gqr_ref[:half, :], (half, tm)) * qscale
    g_r2 = jnp.broadcast_to(gqr_ref[half:, :], (half, tm)) * qscale
    ahead = 2
    pending = [project(h) for h in range(ahead)]
    for h in range(MLA_HEADS):
        if h + ahead < MLA_HEADS:
            pending.append(project(h + ahead))
        q, v_t, _ = pending[h]
        k_n = pending[h - h % 2][2][:, (h % 2) * MLA_NOPE:(h % 2 + 1) * MLA_NOPE]
        nope = q[:MLA_NOPE]
        r = lax.rsqrt(jnp.mean(nope * nope, axis=0, keepdims=True) + EPS)
        qt_ref[0, h, :MLA_NOPE, :] = (nope * r * g_nope).astype(BF16)
        x1, x2 = q[MLA_NOPE:MLA_NOPE + half], q[MLA_NOPE + half:MLA_NOPE + MLA_ROPE]
        ss = jnp.sum(x1 * x1, axis=0, keepdims=True) + jnp.sum(x2 * x2, axis=0, keepdims=True)
        r = lax.rsqrt(ss * (1.0 / MLA_ROPE) + EPS)
        x1, x2 = x1 * r * g_r1, x2 * r * g_r2
        qt_ref[0, h, MLA_NOPE:MLA_NOPE + half, :] = (x1 * cos_t - x2 * sin_t).astype(BF16)
        qt_ref[0, h, MLA_NOPE + half:MLA_NOPE + MLA_ROPE, :] = (x1 * sin_t + x2 * cos_t).astype(BF16)
        kn_ref[0, h] = _rms(k_n, gkn_ref[...]).astype(BF16)
        vt_ref[0, h, :MLA_V, :] = v_t.astype(BF16)
        vt_ref[0, h, MLA_V:, :] = jnp.ones((MLA_VL - MLA_V, tm), BF16)

    kr_t = kr_ref[0].astype(F32).T
    x1, x2 = kr_t[:half], kr_t[half:MLA_ROPE]
    ss = jnp.sum(x1 * x1, axis=0, keepdims=True) + jnp.sum(x2 * x2, axis=0, keepdims=True)
    r = lax.rsqrt(ss * (1.0 / MLA_ROPE) + EPS)
    x1, x2 = x1 * r * gkr_ref[:half, :], x2 * r * gkr_ref[half:, :]
    rot = jnp.concatenate([x1 * cos_t - x2 * sin_t, x1 * sin_t + x2 * cos_t,
                           jnp.zeros((LANES - MLA_ROPE, tm), F32)], axis=0)
    krope_ref[0] = rot.T.astype(BF16)


def _mla_qkv(proj, cq_blk, ckv_blk, kr_blk, positions, gains, w_uq, w_ukv, layer, *, tm):
    b, l, _ = proj.shape
    hh = MLA_HEADS
    half = MLA_ROPE // 2
    inv_freq = ROPE_THETA ** (-jnp.arange(half, dtype=F32) / half)
    const = lambda a: pl.BlockSpec(a.shape, lambda i, j: (0,) * a.ndim)
    gq, gkv, gqn, gkn, gqr, gkr = gains
    consts = [inv_freq.reshape(half, 1), gq.reshape(1, -1), gkv.reshape(1, -1), gqn.reshape(-1, 1),
              gkn.reshape(1, -1), gqr.reshape(-1, 1), gkr.reshape(-1, 1)]
    weights = [w_uq, w_ukv]
    return pl.pallas_call(
        _mla_qkv_kernel,
        grid=(b, l // tm),
        in_specs=[pl.BlockSpec((1, tm, MLA_Q_LORA), lambda i, j: (i, j, cq_blk)),
                  pl.BlockSpec((1, tm, MLA_KV_LORA), lambda i, j: (i, j, ckv_blk)),
                  pl.BlockSpec((1, tm, LANES), lambda i, j: (i, j, kr_blk)),
                  pl.BlockSpec((1, 1, tm), lambda i, j: (i, 0, j))] + [const(a) for a in consts]
                 + [pl.BlockSpec(*_layer_block(w, layer)) for w in weights],
        out_specs=[pl.BlockSpec((1, hh, MLA_QK_PAD, tm), lambda i, j: (i, 0, 0, j)),
                   pl.BlockSpec((1, hh, tm, MLA_NOPE), lambda i, j: (i, 0, j, 0)),
                   pl.BlockSpec((1, tm, LANES), lambda i, j: (i, j, 0)),
                   pl.BlockSpec((1, hh, MLA_VL, tm), lambda i, j: (i, 0, 0, j))],
        out_shape=[jax.ShapeDtypeStruct((b, hh, MLA_QK_PAD, l), BF16),
                   jax.ShapeDtypeStruct((b, hh, l, MLA_NOPE), BF16),
                   jax.ShapeDtypeStruct((b, l, LANES), BF16),
                   jax.ShapeDtypeStruct((b, hh, MLA_VL, l), BF16)],
        scratch_shapes=[pltpu.VMEM((hh * (MLA_NOPE + MLA_ROPE), MLA_Q_LORA), BF16),
                        pltpu.VMEM((MLA_KV_LORA, hh * MLA_NOPE), BF16),
                        pltpu.VMEM((hh * MLA_V, MLA_KV_LORA), BF16)],
        compiler_params=_cparams(("arbitrary", "arbitrary")),
        name="mla_qkv",
    )(proj, proj, proj, positions.reshape(b, 1, l), *consts, *weights)


def _flash_kernel(qt_ref, kn_ref, kr_ref, vt_ref, o_ref, m_ref, acc_ref, *, tq, hp, ahead_full, ahead_diagonal):
    qi = pl.program_id(2)
    m_ref[...] = jnp.full(m_ref.shape, -jnp.inf, F32)
    acc_ref[...] = jnp.zeros(acc_ref.shape, F32)

    half = tq // 2
    lower = (lax.broadcasted_iota(jnp.int32, (half, half), 0)
             <= lax.broadcasted_iota(jnp.int32, (half, half), 1))

    def blocks(j, parts, diagonal):
        ahead = ahead_diagonal if diagonal else ahead_full
        base = pl.multiple_of(j * tq, tq)
        items = [(h, pl.ds(base + k0, nk), slice(q0, q0 + nq)) for k0, nk, q0, nq in parts for h in range(hp)]

        def scores(h, rows, cols):
            k = jnp.concatenate([kn_ref[0, h, rows, :], kr_ref[0, rows, :MLA_ROPE]], axis=-1)
            return jnp.dot(k, qt_ref[0, h, :, cols], preferred_element_type=F32)

        pending = [scores(*it) for it in items[:ahead]]
        for n, (h, rows, cols) in enumerate(items):
            if n + ahead < len(items):
                pending.append(scores(*items[n + ahead]))
            s = pending[n]
            if diagonal:
                square = jnp.where(lower, s[:, :half], jnp.finfo(F32).min)
                s = square if s.shape[1] == half else jnp.concatenate([square, s[:, half:]], axis=1)
            m = m_ref[h, :, cols]
            m_new = jnp.maximum(m, jnp.max(s, axis=0, keepdims=True))
            alpha = jnp.exp2(m - m_new)
            p = jnp.exp2(s - m_new)
            acc_ref[h, :, cols] = alpha * acc_ref[h, :, cols] + jnp.dot(
                vt_ref[0, h, :, rows], p.astype(BF16), preferred_element_type=F32)
            m_ref[h, :, cols] = m_new

    def body(j, carry):
        blocks(j, [(0, half, 0, tq), (half, half, 0, tq)], False)
        return carry

    lax.fori_loop(0, qi, body, 0)
    blocks(qi, [(0, half, 0, tq), (half, half, half, half)], True)
    for h in range(hp):
        o_ref[0, :, h * MLA_V:(h + 1) * MLA_V] = (
            acc_ref[h, :MLA_V, :] / acc_ref[h, MLA_V:MLA_V + 1, :]).T.astype(o_ref.dtype)


def _flash(qt, kn, kr, vt, *, tq, hp, ahead_full, ahead_diagonal):
    b, hh, _, l = qt.shape
    return pl.pallas_call(
        functools.partial(_flash_kernel, tq=tq, hp=hp, ahead_full=ahead_full, ahead_diagonal=ahead_diagonal),
        grid=(b, hh // hp, l // tq),
        in_specs=[pl.BlockSpec((1, hp, MLA_QK_PAD, tq), lambda i, h, j: (i, h, 0, j)),
                  pl.BlockSpec((1, hp, l, MLA_NOPE), lambda i, h, j: (i, h, 0, 0)),
                  pl.BlockSpec((1, l, LANES), lambda i, h, j: (i, 0, 0)),
                  pl.BlockSpec((1, hp, MLA_VL, l), lambda i, h, j: (i, h, 0, 0))],
        out_specs=pl.BlockSpec((1, tq, hp * MLA_V), lambda i, h, j: (i, j, h)),
        out_shape=jax.ShapeDtypeStruct((b, l, hh * MLA_V), BF16),
        scratch_shapes=[pltpu.VMEM((hp, 1, tq), F32), pltpu.VMEM((hp, MLA_VL, tq), F32)],
        compiler_params=_cparams(("parallel", "parallel", "parallel")),
        name="flash",
    )(qt, kn, kr, vt)


def _s5_layer(x, ln, w_in, lam_re, lam_im, log_step, b_re, b_im, c_re, c_im, d, w_glu,
              w_out, mem_kv, xq_norm, layer, j):
    b, l, dm = x.shape
    tm = 512
    proj, xn = _s5_in_proj(x, ln, w_in, j, tm=tm, col_chunk=512)
    xg = _s5_ut(xn, w_in, j)
    toep, wout, wst, dcol = _s5_asm(lam_re, lam_im, log_step, c_re, c_im, b_re, b_im, d)
    pw_re, pw_im = _s5_pow(lam_re, lam_im, log_step)
    col = lambda pw: pw.transpose(1, 2, 0)
    yt = _s5_mix(xg, toep, wst, wout, col(pw_re), col(pw_im), dcol, nb=b, gs=8)
    y = _glu(yt, w_glu, j, nb=b, col_chunk=256)
    return _merge(x, y, proj, BRANCH_WIDTH // XQ_WIDTH, 0, *mem_kv, xq_norm, w_out, layer, tm=tm, phased=True)


def _mla_layer(x, positions, ln, w_in, q_lora_norm, kv_lora_norm, w_uq, w_ukv, q_nope_norm, k_nope_norm,
               q_rope_norm, k_rope_norm, w_out, mem_kv, xq_norm, layer, j):
    b, l, dm = x.shape
    o1 = MLA_Q_LORA
    o2 = o1 + MLA_KV_LORA
    o3 = o2 + MLA_ROPE
    o4 = o3 + XQ_WIDTH
    segments = ((o4, BRANCH_WIDTH), (0, o1), (o3, XQ_WIDTH), (o1, MLA_KV_LORA), (o2, MLA_ROPE))
    wout = -(-(o4 + BRANCH_WIDTH) // 512) * 512
    proj = _mla_in_proj(x.reshape(b * l, dm), ln, w_in, j, segments, wout, tm=512, col_chunk=512)
    proj = proj.reshape(b, l, -1)
    gate_blk = 0
    cq_blk = BRANCH_WIDTH // MLA_Q_LORA
    xq_blk = (BRANCH_WIDTH + MLA_Q_LORA) // XQ_WIDTH
    ckv_blk = (BRANCH_WIDTH + MLA_Q_LORA + XQ_WIDTH) // MLA_KV_LORA
    kr_blk = (BRANCH_WIDTH + MLA_Q_LORA + XQ_WIDTH + MLA_KV_LORA) // LANES
    qt, kn, kr, vt = _mla_qkv(proj, cq_blk, ckv_blk, kr_blk, positions,
                              (q_lora_norm, kv_lora_norm, q_nope_norm, k_nope_norm, q_rope_norm, k_rope_norm),
                              w_uq, w_ukv, j, tm=512)
    attn = _flash(qt, kn, kr, vt, tq=512, hp=12, ahead_full=2, ahead_diagonal=4)
    return _merge(x, attn, proj, xq_blk, gate_blk, *mem_kv, xq_norm, w_out, layer, tm=512)


def kernel(x, mem, positions, ln_gain, w_out, mem_norm, w_mem_kv, xq_norm, xk_norm,
           s5_w_in, s5_lambda_re, s5_lambda_im, s5_log_step, s5_b_re, s5_b_im, s5_c_re, s5_c_im,
           s5_d, s5_w_glu, mla_w_in, mla_q_lora_norm, mla_kv_lora_norm, mla_w_uq, mla_w_ukv,
           mla_q_nope_norm, mla_k_nope_norm, mla_q_rope_norm, mla_k_rope_norm):
    depth = ln_gain.shape[0]
    mem_kv = _mem_kv(mem, mem_norm, w_mem_kv, xk_norm)
    for i in range(depth):
        j = i // 2
        if i % 2 == 0:
            x = _s5_layer(x, ln_gain[i], s5_w_in, s5_lambda_re[j], s5_lambda_im[j], s5_log_step[j],
                          s5_b_re[j], s5_b_im[j], s5_c_re[j], s5_c_im[j], s5_d[j], s5_w_glu,
                          w_out, mem_kv, xq_norm, i, j)
        else:
            x = _mla_layer(x, positions, ln_gain[i], mla_w_in, mla_q_lora_norm[j], mla_kv_lora_norm[j],
                           mla_w_uq, mla_w_ukv, mla_q_nope_norm[j], mla_k_nope_norm[j],
                           mla_q_rope_norm[j], mla_k_rope_norm[j],
                           w_out, mem_kv, xq_norm, i, j)
    return x
```

```python
import functools
import math

import jax
import jax.numpy as jnp
from jax import lax
from jax.experimental import pallas as pl
from jax.experimental.pallas import tpu as pltpu

D_MODEL = 1024
BRANCH_WIDTH = 2 * D_MODEL
XQ_WIDTH = BRANCH_WIDTH // 4
PRIMARY_WIDTH = BRANCH_WIDTH - XQ_WIDTH
X_HEADS = 4
X_HEAD_DIM = XQ_WIDTH // X_HEADS
S5_GROUP_CH = 16
S5_GROUPS = PRIMARY_WIDTH // S5_GROUP_CH
S5_STATE = 64
MLA_NOPE = 128
MLA_ROPE = 64
MLA_V = 128
MLA_HEADS = PRIMARY_WIDTH // MLA_V
MLA_Q_LORA = D_MODEL // 2
MLA_KV_LORA = D_MODEL // 4
ROPE_THETA = 10000.0
EPS = 1e-6

LANES = 128
MLA_QK_PAD = MLA_NOPE + MLA_ROPE
F32_SUBLANES = 8
BF16_SUBLANES = 16
MLA_VL = MLA_V + BF16_SUBLANES
S5_CHUNK = 2 * LANES // S5_GROUP_CH
S5_SCAN_EXPONENTS = [S5_CHUNK * 2 ** i for i in range(int(math.log2(LANES)))]
VMEM_LIMIT = 56 * 1024 * 1024

F32 = jnp.float32
BF16 = jnp.bfloat16


def _cparams(sem):
    return pltpu.CompilerParams(dimension_semantics=sem, vmem_limit_bytes=VMEM_LIMIT)


def _rms(x, g):
    return x * lax.rsqrt(jnp.mean(x * x, axis=-1, keepdims=True) + EPS) * g


def _layer_block(w, j, block=None, index=None):
    block = tuple(w.shape[1:]) if block is None else block
    index = (0,) * len(block) if index is None else index
    return (None,) + block, lambda *_: (j,) + index


def _mla_in_proj_kernel(x_ref, g_ref, w_ref, o_ref, wp_ref, *, segments, col_chunk):
    @pl.when(pl.program_id(0) == 0)
    def _():
        at = 0
        for start, width in segments:
            wp_ref[:, at:at + width] = w_ref[:, start:start + width].astype(BF16)
            at += width
        wp_ref[:, at:] = jnp.zeros((wp_ref.shape[0], wp_ref.shape[1] - at), BF16)

    xn = _rms(x_ref[...], g_ref[...]).astype(BF16)
    for c in range(o_ref.shape[1] // col_chunk):
        sl = slice(c * col_chunk, (c + 1) * col_chunk)
        o_ref[:, sl] = jnp.dot(xn, wp_ref[:, sl], preferred_element_type=F32).astype(o_ref.dtype)


def _mla_in_proj(x, g, w, j, segments, wout, *, tm, col_chunk):
    n, d = x.shape
    return pl.pallas_call(
        functools.partial(_mla_in_proj_kernel, segments=segments, col_chunk=col_chunk),
        grid=(n // tm,),
        in_specs=[pl.BlockSpec((tm, d), lambda i: (i, 0)),
                  pl.BlockSpec((1, d), lambda i: (0, 0)),
                  pl.BlockSpec(*_layer_block(w, j), pipeline_mode=pl.Buffered(1))],
        out_specs=pl.BlockSpec((tm, wout), lambda i: (i, 0)),
        out_shape=jax.ShapeDtypeStruct((n, wout), BF16),
        scratch_shapes=[pltpu.VMEM((d, wout), BF16)],
        compiler_params=_cparams(("arbitrary",)),
        name="mla_in_proj",
    )(x, g.reshape(1, d), w)


def _to_phase_order(a):
    n, d = a.shape
    return jnp.swapaxes(a.reshape(n // S5_CHUNK, S5_CHUNK, d), 0, 1).reshape(n, d)


def _from_phase_order(a):
    n, d = a.shape
    return jnp.swapaxes(a.reshape(S5_CHUNK, n // S5_CHUNK, d), 0, 1).reshape(n, d)


def _s5_in_proj_kernel(x_ref, g_ref, wg_ref, wx_ref, o_ref, xn_ref, wb_ref, *, tm, col_chunk):
    @pl.when((pl.program_id(0) == 0) & (pl.program_id(1) == 0))
    def _():
        wb_ref[:, :BRANCH_WIDTH] = wg_ref[...].astype(BF16)
        wb_ref[:, BRANCH_WIDTH:] = wx_ref[...].astype(BF16)

    d = x_ref.shape[2]
    nc = tm // S5_CHUNK
    chunks = [slice(c * col_chunk, (c + 1) * col_chunk) for c in range((BRANCH_WIDTH + XQ_WIDTH) // col_chunk)]
    weights = [wb_ref[:, sl] for sl in chunks]

    def normalise(k):
        xn = _rms(_to_phase_order(x_ref[0, k * tm:(k + 1) * tm, :]), g_ref[...]).astype(BF16)
        for s in range(S5_CHUNK):
            xn_ref[0, k * nc:(k + 1) * nc, s * d:(s + 1) * d] = xn[s * nc:(s + 1) * nc]
        return xn

    def project(k, xn):
        for sl, w in zip(chunks, weights):
            o_ref[0, k * tm:(k + 1) * tm, sl] = jnp.dot(xn, w, preferred_element_type=F32).astype(o_ref.dtype)

    n_sub = x_ref.shape[1] // tm
    prev = None
    for k in range(n_sub):
        if prev is not None:
            project(k - 1, prev)
        prev = normalise(k)
    project(n_sub - 1, prev)


def _s5_in_proj(x, g, w_in, j, *, tm, col_chunk):
    b, l, d = x.shape
    wout = BRANCH_WIDTH + XQ_WIDTH
    tg = 2 * tm
    nc = tg // S5_CHUNK
    return pl.pallas_call(
        functools.partial(_s5_in_proj_kernel, tm=tm, col_chunk=col_chunk),
        grid=(b, l // tg),
        in_specs=[pl.BlockSpec((1, tg, d), lambda i, j: (i, j, 0)),
                  pl.BlockSpec((1, d), lambda i, j: (0, 0)),
                  pl.BlockSpec(*_layer_block(w_in, j, (d, BRANCH_WIDTH),
                                             (0, (PRIMARY_WIDTH + XQ_WIDTH) // BRANCH_WIDTH)),
                               pipeline_mode=pl.Buffered(1)),
                  pl.BlockSpec(*_layer_block(w_in, j, (d, XQ_WIDTH), (0, PRIMARY_WIDTH // XQ_WIDTH)),
                               pipeline_mode=pl.Buffered(1))],
        out_specs=[pl.BlockSpec((1, tg, wout), lambda i, j: (i, j, 0)),
                   pl.BlockSpec((1, nc, S5_CHUNK * d), lambda i, j: (i, j, 0))],
        out_shape=[jax.ShapeDtypeStruct((b, l, wout), BF16),
                   jax.ShapeDtypeStruct((b, l // S5_CHUNK, S5_CHUNK * d), BF16)],
        scratch_shapes=[pltpu.VMEM((d, wout), BF16)],
        compiler_params=_cparams(("arbitrary", "arbitrary")),
        name="s5_in_proj",
    )(x, g.reshape(1, d), w_in, w_in)


def _mem_kv_kernel(m_ref, g_ref, w_ref, kg_ref, k_ref, v_ref):
    b, m, d = m_ref.shape
    w = w_ref[0].astype(BF16)
    for i in range(b):
        mn = _rms(m_ref[i], g_ref[0]).astype(BF16)
        kv = jnp.dot(mn, w, preferred_element_type=F32)
        for h in range(X_HEADS):
            sl = slice(h * X_HEAD_DIM, (h + 1) * X_HEAD_DIM)
            k_ref[0, i, :, sl] = _rms(kv[:, sl], kg_ref[0]).astype(BF16)
        v_ref[0, i] = kv[:, XQ_WIDTH:].astype(BF16)


def _mem_kv(mem, mem_norm, w_mem_kv, xk_norm):
    b, m, d = mem.shape
    depth = w_mem_kv.shape[0]
    out = jax.ShapeDtypeStruct((depth, b, m, XQ_WIDTH), BF16)
    return pl.pallas_call(
        _mem_kv_kernel,
        grid=(depth,),
        in_specs=[pl.BlockSpec((b, m, d), lambda n: (0, 0, 0)),
                  pl.BlockSpec((1, 1, d), lambda n: (n, 0, 0)),
                  pl.BlockSpec((1, d, 2 * XQ_WIDTH), lambda n: (n, 0, 0)),
                  pl.BlockSpec((1, 1, X_HEAD_DIM), lambda n: (n, 0, 0))],
        out_specs=[pl.BlockSpec((1, b, m, XQ_WIDTH), lambda n: (n, 0, 0, 0)),
                   pl.BlockSpec((1, b, m, XQ_WIDTH), lambda n: (n, 0, 0, 0))],
        out_shape=[out, out],
        compiler_params=_cparams(("parallel",)),
        name="mem_kv",
    )(mem, mem_norm.reshape(depth, 1, d), w_mem_kv, xk_norm.reshape(depth, 1, X_HEAD_DIM))


def _s5_pow_kernel(lr_ref, li_ref, ls_ref, pr_ref, pi_ref):
    step = jnp.exp(ls_ref[...])
    zr, zi = lr_ref[...] * step, li_ref[...] * step
    for n in range(pr_ref.shape[0]):
        if n < len(S5_SCAN_EXPONENTS):
            e = S5_SCAN_EXPONENTS[n]
            mag = jnp.exp(zr * e)
            pr_ref[n] = mag * jnp.cos(zi * e)
            pi_ref[n] = mag * jnp.sin(zi * e)
        else:
            pr_ref[n] = jnp.zeros_like(zr)
            pi_ref[n] = jnp.zeros_like(zr)


def _s5_pow(lam_re, lam_im, log_step):
    g, p = lam_re.shape
    slots = -(-len(S5_SCAN_EXPONENTS) // F32_SUBLANES) * F32_SUBLANES
    out = jax.ShapeDtypeStruct((slots, g, p), F32)
    return pl.pallas_call(_s5_pow_kernel, out_shape=[out, out], name="s5_pow")(
        lam_re, lam_im, log_step.reshape(g, 1))


def _s5_asm_kernel(lr_ref, li_ref, ls_ref, cr_ref, ci_ref, btr_ref, bti_ref, br_ref, bi_ref, d_ref,
                   toep_ref, wout_ref, wst_ref, dcol_ref):
    def group(i, carry):
        _s5_asm_group(i, lr_ref, li_ref, ls_ref, cr_ref, ci_ref, btr_ref, bti_ref, br_ref, bi_ref,
                      toep_ref, wout_ref, wst_ref)
        dcol_ref[i] = jnp.concatenate([d_ref[i]] * S5_CHUNK, axis=0)
        return carry

    lax.fori_loop(0, cr_ref.shape[0], group, 0)


def _dot_3pass_tiled(a, b):
    c = b.shape[1]
    rep = ((lax.broadcasted_iota(jnp.int32, (c, S5_CHUNK * c), 1) & (c - 1))
           == lax.broadcasted_iota(jnp.int32, (c, S5_CHUNK * c), 0)).astype(BF16)
    dot = functools.partial(jnp.dot, preferred_element_type=F32)
    a_hi, b_hi = a.astype(BF16), b.astype(BF16)
    a_lo = (a - a_hi.astype(F32)).astype(BF16)
    b_lo = (b - b_hi.astype(F32)).astype(BF16)
    b_hi, b_lo = dot(b_hi, rep).astype(BF16), dot(b_lo, rep).astype(BF16)
    return dot(a_hi, b_hi) + (dot(a_hi, b_lo) + dot(a_lo, b_hi))


def _s5_asm_group(i, lr_ref, li_ref, ls_ref, cr_ref, ci_ref, btr_ref, bti_ref, br_ref, bi_ref,
                  toep_ref, wout_ref, wst_ref):
    t = S5_CHUNK
    lr, li = lr_ref[i], li_ref[i]
    step = jnp.exp(ls_ref[i])
    zr, zi = lr * step, li * step
    rows = -(-(t + 1) // F32_SUBLANES) * F32_SUBLANES
    e = lax.broadcasted_iota(jnp.int32, (rows, lr.shape[1]), 0).astype(F32)
    mag = jnp.exp(e * zr)
    pr, pi = mag * jnp.cos(e * zi), mag * jnp.sin(e * zi)
    den = lr * lr + li * li
    mr = ((pr[1:2] - 1.0) * lr + pi[1:2] * li) / den
    mi = (pi[1:2] * lr - (pr[1:2] - 1.0) * li) / den
    cr, ci = cr_ref[i], ci_ref[i]
    btr, bti = btr_ref[i], bti_ref[i]
    amr = pr[:t] * mr - pi[:t] * mi
    ami = pr[:t] * mi + pi[:t] * mr
    l_re, l_im, w_re, w_im, o_re, o_im = [], [], [], [], [], []
    for k in range(t):
        ar, ai = amr[k:k + 1], ami[k:k + 1]
        l_re.append(cr * ar - ci * ai)
        l_im.append(-(cr * ai + ci * ar))
        ar, ai = amr[t - 1 - k:t - k], ami[t - 1 - k:t - k]
        w_re.append(btr * ar - bti * ai)
        w_im.append(btr * ai + bti * ar)
        ar, ai = pr[k + 1:k + 2], pi[k + 1:k + 2]
        o_re.append(cr * ar - ci * ai)
        o_im.append(-(cr * ai + ci * ar))
    cat = lambda parts: jnp.concatenate(parts, axis=0)
    kt = _dot_3pass_tiled(jnp.concatenate([cat(l_re), cat(l_im)], axis=1),
                          jnp.concatenate([br_ref[i], bi_ref[i]], axis=0))
    n = kt.shape[0]
    blk = lax.shift_right_logical(lax.broadcasted_iota(jnp.int32, kt.shape, 1), int(math.log2(S5_GROUP_CH)))
    toep = jnp.where(blk == 0, kt, 0.0)
    for s in range(1, t):
        shifted = jnp.concatenate([jnp.zeros((s * S5_GROUP_CH, n), F32), kt[:n - s * S5_GROUP_CH]], axis=0)
        toep = jnp.where(blk == s, shifted, toep)
    toep_ref[i] = toep.astype(BF16)
    wout_ref[i] = jnp.concatenate([cat(o_re), cat(o_im)], axis=1).astype(BF16)
    wst_ref[i] = jnp.concatenate([cat(w_re), cat(w_im)], axis=1).T.astype(BF16)


def _s5_asm(lam_re, lam_im, log_step, c_re, c_im, b_re, b_im, d):
    g, c, p = c_re.shape
    tc = S5_CHUNK * c
    gs = 8
    blk = lambda a: pl.BlockSpec((gs,) + a.shape[1:], lambda i: (i, 0, 0))
    bt_re, bt_im = b_re.transpose(0, 2, 1), b_im.transpose(0, 2, 1)
    args = (lam_re.reshape(g, 1, p), lam_im.reshape(g, 1, p), log_step.reshape(g, 1, 1),
            c_re, c_im, bt_re, bt_im, b_re, b_im, d.reshape(g, c, 1))
    return pl.pallas_call(
        _s5_asm_kernel,
        grid=(g // gs,),
        in_specs=[blk(a) for a in args],
        out_specs=[pl.BlockSpec((gs, tc, tc), lambda i: (i, 0, 0)),
                   pl.BlockSpec((gs, tc, 2 * p), lambda i: (i, 0, 0)),
                   pl.BlockSpec((gs, 2 * p, tc), lambda i: (i, 0, 0)),
                   pl.BlockSpec((gs, tc, 1), lambda i: (i, 0, 0))],
        out_shape=[jax.ShapeDtypeStruct((g, tc, tc), BF16), jax.ShapeDtypeStruct((g, tc, 2 * p), BF16),
                   jax.ShapeDtypeStruct((g, 2 * p, tc), BF16), jax.ShapeDtypeStruct((g, tc, 1), F32)],
        compiler_params=_cparams(("parallel",)),
        name="s5_asm",
    )(*args)


def _s5_ut_kernel(xn_ref, w_ref, o_ref, wt_ref):
    @pl.when(pl.program_id(0) == 0)
    def _():
        wt_ref[...] = w_ref[...].T.astype(BF16)

    nb, nc, d = xn_ref.shape
    ut = lax.dot_general(wt_ref[...], xn_ref[...].reshape(nb * nc, d), (((1,), (1,)), ((), ())),
                         preferred_element_type=F32)
    o_ref[...] = ut.astype(BF16).reshape(o_ref.shape)


def _s5_ut(xn, w_in, j):
    b, nc, td = xn.shape
    d = td // S5_CHUNK
    return pl.pallas_call(
        _s5_ut_kernel,
        grid=(S5_CHUNK,),
        in_specs=[pl.BlockSpec((b, nc, d), lambda s: (0, 0, s)),
                  pl.BlockSpec(*_layer_block(w_in, j, (d, PRIMARY_WIDTH)))],
        out_specs=pl.BlockSpec((S5_GROUPS, S5_GROUP_CH, b * nc), lambda s: (0, s, 0)),
        out_shape=jax.ShapeDtypeStruct((S5_GROUPS, S5_CHUNK * S5_GROUP_CH, b * nc), BF16),
        scratch_shapes=[pltpu.VMEM((PRIMARY_WIDTH, d), BF16)],
        compiler_params=_cparams(("arbitrary",)),
        name="s5_ut",
    )(xn, w_in)


def _s5_mix_kernel(x_ref, toep_ref, wst_ref, wout_ref, sr_ref, si_ref, d_ref, o_ref, *, nb):
    p = S5_STATE
    gs = x_ref.shape[0]
    lane = lax.broadcasted_iota(jnp.int32, (p, LANES), 1)
    n_steps = int(math.log2(LANES))

    def scan(g, hloc):
        pw = []
        for i in range(n_steps):
            keep = lane >= (1 << i)
            pw.append((jnp.where(keep, jnp.broadcast_to(sr_ref[g, :, i:i + 1], (p, LANES)), 0.0),
                       jnp.where(keep, jnp.broadcast_to(si_ref[g, :, i:i + 1], (p, LANES)), 0.0)))
        h_re = [hloc[:p, b * LANES:(b + 1) * LANES] for b in range(nb)]
        h_im = [hloc[p:, b * LANES:(b + 1) * LANES] for b in range(nb)]
        for i in range(n_steps):
            ar, ai = pw[i]
            r_sh = [pltpu.roll(v, 1 << i, 1) for v in h_re]
            i_sh = [pltpu.roll(v, 1 << i, 1) for v in h_im]
            h_re = [h_re[b] + ar * r_sh[b] - ai * i_sh[b] for b in range(nb)]
            h_im = [h_im[b] + ar * i_sh[b] + ai * r_sh[b] for b in range(nb)]
        h_re = [jnp.where(lane >= 1, pltpu.roll(v, 1, 1), 0.0) for v in h_re]
        h_im = [jnp.where(lane >= 1, pltpu.roll(v, 1, 1), 0.0) for v in h_im]
        return jnp.concatenate([jnp.concatenate(h_re, axis=1), jnp.concatenate(h_im, axis=1)],
                               axis=0).astype(BF16)

    def outputs(g, h, intra):
        y = intra + jnp.dot(wout_ref[g], h, preferred_element_type=F32)
        o_ref[:, g * S5_GROUP_CH:(g + 1) * S5_GROUP_CH, :] = (
            jax.nn.gelu(y).astype(o_ref.dtype).reshape(S5_CHUNK, S5_GROUP_CH, y.shape[1]))

    prev = None
    for g in range(gs):
        x = x_ref[g]
        hloc = jnp.dot(wst_ref[g], x, preferred_element_type=F32)
        intra = jnp.dot(toep_ref[g], x, preferred_element_type=F32) + d_ref[g] * x.astype(F32)
        h = scan(g, hloc)
        if prev is not None:
            outputs(g - 1, *prev)
        prev = (h, intra)
    outputs(gs - 1, *prev)


def _s5_mix(xg, toep, wst, wout, sc_re, sc_im, dcol, *, nb, gs):
    g, tc, cols = xg.shape
    assert cols == nb * LANES, "one batch's chunks must fill exactly one 128-lane block"
    blk = lambda a: pl.BlockSpec((gs,) + a.shape[1:], lambda i: (i, 0, 0))
    return pl.pallas_call(
        functools.partial(_s5_mix_kernel, nb=nb),
        grid=(g // gs,),
        in_specs=[blk(a) for a in (xg, toep, wst, wout, sc_re, sc_im, dcol)],
        out_specs=pl.BlockSpec((S5_CHUNK, gs * S5_GROUP_CH, cols), lambda i: (0, i, 0)),
        out_shape=jax.ShapeDtypeStruct((S5_CHUNK, g * S5_GROUP_CH, cols), BF16),
        compiler_params=_cparams(("parallel",)),
        name="s5_mix",
    )(xg, toep, wst, wout, sc_re, sc_im, dcol)


def _glu_kernel(y_ref, w_ref, o_ref, *, col_chunk):
    y = y_ref[0].T
    half = o_ref.shape[-1]
    for c in range(half // col_chunk):
        wa = w_ref[:, c * col_chunk:(c + 1) * col_chunk].astype(BF16)
        wg = w_ref[:, half + c * col_chunk:half + (c + 1) * col_chunk].astype(BF16)
        a = jnp.dot(y, wa, preferred_element_type=F32)
        g = jnp.dot(y, wg, preferred_element_type=F32)
        o_ref[:, :, c * col_chunk:(c + 1) * col_chunk] = (
            (a * jax.nn.sigmoid(g)).astype(o_ref.dtype).reshape(o_ref.shape[:2] + (col_chunk,)))


def _glu(yt, w, layer, *, nb, col_chunk):
    t, k, cols = yt.shape
    nc = cols // nb
    half = w.shape[2] // 2
    return pl.pallas_call(
        functools.partial(_glu_kernel, col_chunk=col_chunk),
        grid=(t,),
        in_specs=[pl.BlockSpec((1, k, cols), lambda j: (j, 0, 0)),
                  pl.BlockSpec(*_layer_block(w, layer), pipeline_mode=pl.Buffered(1))],
        out_specs=pl.BlockSpec((nb, nc, half), lambda j: (0, 0, j)),
        out_shape=jax.ShapeDtypeStruct((nb, nc, t * half), BF16),
        compiler_params=_cparams(("parallel",)),
        name="glu",
    )(yt, w)


def _merge_kernel(x_ref, mix_ref, xq_ref, gate_ref, k_ref, v_ref, qg_ref, w_ref, o_ref, wb_ref, *cat_refs,
                  tm, phased):
    scale = X_HEAD_DIM ** -0.5
    nc = tm // S5_CHUNK
    @pl.when((pl.program_id(0) == 0) & (pl.program_id(1) == 0))
    def _():
        wb_ref[...] = w_ref[...].astype(BF16)

    def gather(k, cat_ref):
        r0 = k * tm
        gate = gate_ref[0, r0:r0 + tm, :]
        sg = gate * jax.nn.sigmoid(gate)
        if phased:
            for s in range(S5_CHUNK):
                rows = slice(s * nc, (s + 1) * nc)
                mix = mix_ref[0, k * nc:(k + 1) * nc, s * PRIMARY_WIDTH:(s + 1) * PRIMARY_WIDTH]
                cat_ref[rows, :PRIMARY_WIDTH] = mix * sg[rows, :PRIMARY_WIDTH]
        else:
            cat_ref[:, :PRIMARY_WIDTH] = mix_ref[0, r0:r0 + tm, :] * sg[:, :PRIMARY_WIDTH]
        for h in range(X_HEADS):
            sl = slice(h * X_HEAD_DIM, (h + 1) * X_HEAD_DIM)
            q = _rms(xq_ref[0, r0:r0 + tm, sl].astype(F32), qg_ref[...]).astype(BF16)
            s = lax.dot_general(q, k_ref[0, :, sl], (((1,), (1,)), ((), ())), preferred_element_type=F32) * scale
            p = jnp.exp(s - jnp.max(s, axis=-1, keepdims=True))
            p = (p / jnp.sum(p, axis=-1, keepdims=True)).astype(BF16)
            mo = jnp.dot(p, v_ref[0, :, sl], preferred_element_type=F32)
            osl = slice(PRIMARY_WIDTH + h * X_HEAD_DIM, PRIMARY_WIDTH + (h + 1) * X_HEAD_DIM)
            cat_ref[:, osl] = mo.astype(BF16) * sg[:, osl]

    def project(k, cat_ref):
        r0 = k * tm
        delta = (jnp.dot(cat_ref[:, :PRIMARY_WIDTH], wb_ref[:PRIMARY_WIDTH, :], preferred_element_type=F32)
                 + jnp.dot(cat_ref[:, PRIMARY_WIDTH:], wb_ref[PRIMARY_WIDTH:, :], preferred_element_type=F32))
        o_ref[0, r0:r0 + tm, :] = x_ref[0, r0:r0 + tm, :] + (_from_phase_order(delta) if phased else delta)

    n_sub = len(cat_refs)
    for k in range(n_sub):
        gather(k, cat_refs[k])
        if k > 0:
            project(k - 1, cat_refs[k - 1])
    project(n_sub - 1, cat_refs[n_sub - 1])


def _merge(x, mix, proj, xq_blk, gate_blk, mk, mv, xq_norm, w_out, layer, *, tm, phased=False):
    b, l, d = x.shape
    m = mk.shape[2]
    n_sub = 2
    tg = n_sub * tm
    mix_spec = (pl.BlockSpec((1, tg // S5_CHUNK, S5_CHUNK * PRIMARY_WIDTH), lambda i, j: (i, j, 0)) if phased
                else pl.BlockSpec((1, tg, PRIMARY_WIDTH), lambda i, j: (i, j, 0)))
    return pl.pallas_call(
        functools.partial(_merge_kernel, tm=tm, phased=phased),
        grid=(b, l // tg),
        in_specs=[pl.BlockSpec((1, tg, d), lambda i, j: (i, j, 0)),
                  mix_spec,
                  pl.BlockSpec((1, tg, XQ_WIDTH), lambda i, j: (i, j, xq_blk)),
                  pl.BlockSpec((1, tg, BRANCH_WIDTH), lambda i, j: (i, j, gate_blk)),
                  pl.BlockSpec((None, 1, m, XQ_WIDTH), lambda i, j: (layer, i, 0, 0)),
                  pl.BlockSpec((None, 1, m, XQ_WIDTH), lambda i, j: (layer, i, 0, 0)),
                  pl.BlockSpec((None, 1, X_HEAD_DIM), lambda i, j: (layer, 0, 0)),
                  pl.BlockSpec((None, BRANCH_WIDTH, d), lambda i, j: (layer, 0, 0),
                               pipeline_mode=pl.Buffered(1))],
        out_specs=pl.BlockSpec((1, tg, d), lambda i, j: (i, j, 0)),
        out_shape=jax.ShapeDtypeStruct((b, l, d), F32),
        scratch_shapes=[pltpu.VMEM((BRANCH_WIDTH, d), BF16)] + [pltpu.VMEM((tm, BRANCH_WIDTH), BF16)] * n_sub,
        compiler_params=_cparams(("arbitrary", "arbitrary")),
        name="merge",
    )(x, mix, proj, proj, mk, mv, xq_norm.reshape(-1, 1, X_HEAD_DIM), w_out)


def _mla_qkv_kernel(cq_ref, ckv_ref, kr_ref, posr_ref, invfc_ref, gq_ref, gkv_ref, gqn_ref,
                    gkn_ref, gqr_ref, gkr_ref, wuq_ref, wukv_ref, qt_ref, kn_ref, krope_ref, vt_ref,
                    wqt_ref, wk_ref, wvt_ref):
    half = MLA_ROPE // 2
    tm = cq_ref.shape[1]
    qk = MLA_NOPE + MLA_ROPE

    @pl.when((pl.program_id(0) == 0) & (pl.program_id(1) == 0))
    def _():
        wqt_ref[...] = wuq_ref[...].T.astype(BF16)
        for h in range(MLA_HEADS):
            c0 = h * (MLA_NOPE + MLA_V)
            wk_ref[:, h * MLA_NOPE:(h + 1) * MLA_NOPE] = wukv_ref[:, c0:c0 + MLA_NOPE].astype(BF16)
            wvt_ref[h * MLA_V:(h + 1) * MLA_V, :] = wukv_ref[:, c0 + MLA_NOPE:c0 + MLA_NOPE + MLA_V].T.astype(BF16)

    qscale = (MLA_NOPE + MLA_ROPE) ** -0.5 * math.log2(math.e)

    cq = _rms(cq_ref[0].astype(F32), gq_ref[...])
    ckv = _rms(ckv_ref[0].astype(F32), gkv_ref[...])
    cq_t = cq.T.astype(BF16)
    ckv_t = ckv.T.astype(BF16)
    ckv_b = ckv.astype(BF16)

    def project(h):
        dot = functools.partial(jnp.dot, preferred_element_type=F32)
        k_pair = dot(ckv_b, wk_ref[:, h * MLA_NOPE:(h + 2) * MLA_NOPE]) if h % 2 == 0 else None
        return (dot(wqt_ref[h * qk:(h + 1) * qk, :], cq_t),
                dot(wvt_ref[h * MLA_V:(h + 1) * MLA_V, :], ckv_t), k_pair)

    ang_t = invfc_ref[...] * posr_ref[0].astype(F32)
    cos_t, sin_t = jnp.cos(ang_t), jnp.sin(ang_t)
    g_nope = jnp.broadcast_to(gqn_ref[...], (MLA_NOPE, tm)) * qscale
    g_r1 = jnp.broadcast_to(gqr_ref[:half, :], (half, tm)) * qscale
    g_r2 = jnp.broadcast_to(gqr_ref[half:, :], (half, tm)) * qscale
    ahead = 2
    pending = [project(h) for h in range(ahead)]
    for h in range(MLA_HEADS):
        if h + ahead < MLA_HEADS:
            pending.append(project(h + ahead))
        q, v_t, _ = pending[h]
        k_n = pending[h - h % 2][2][:, (h % 2) * MLA_NOPE:(h % 2 + 1) * MLA_NOPE]
        nope = q[:MLA_NOPE]
        r = lax.rsqrt(jnp.mean(nope * nope, axis=0, keepdims=True) + EPS)
        qt_ref[0, h, :MLA_NOPE, :] = (nope * r * g_nope).astype(BF16)
        x1, x2 = q[MLA_NOPE:MLA_NOPE + half], q[MLA_NOPE + half:MLA_NOPE + MLA_ROPE]
        ss = jnp.sum(x1 * x1, axis=0, keepdims=True) + jnp.sum(x2 * x2, axis=0, keepdims=True)
        r = lax.rsqrt(ss * (1.0 / MLA_ROPE) + EPS)
        x1, x2 = x1 * r * g_r1, x2 * r * g_r2
        qt_ref[0, h, MLA_NOPE:MLA_NOPE + half, :] = (x1 * cos_t - x2 * sin_t).astype(BF16)
        qt_ref[0, h, MLA_NOPE + half:MLA_NOPE + MLA_ROPE, :] = (x1 * sin_t + x2 * cos_t).astype(BF16)
        kn_ref[0, h] = _rms(k_n, gkn_ref[...]).astype(BF16)
        vt_ref[0, h, :MLA_V, :] = v_t.astype(BF16)
        vt_ref[0, h, MLA_V:, :] = jnp.ones((MLA_VL - MLA_V, tm), BF16)

    kr_t = kr_ref[0].astype(F32).T
    x1, x2 = kr_t[:half], kr_t[half:MLA_ROPE]
    ss = jnp.sum(x1 * x1, axis=0, keepdims=True) + jnp.sum(x2 * x2, axis=0, keepdims=True)
    r = lax.rsqrt(ss * (1.0 / MLA_ROPE) + EPS)
    x1, x2 = x1 * r * gkr_ref[:half, :], x2 * r * gkr_ref[half:, :]
    rot = jnp.concatenate([x1 * cos_t - x2 * sin_t, x1 * sin_t + x2 * cos_t,
                           jnp.zeros((LANES - MLA_ROPE, tm), F32)], axis=0)
    krope_ref[0] = rot.T.astype(BF16)


def _mla_qkv(proj, cq_blk, ckv_blk, kr_blk, positions, gains, w_uq, w_ukv, layer, *, tm):
    b, l, _ = proj.shape
    hh = MLA_HEADS
    half = MLA_ROPE // 2
    inv_freq = ROPE_THETA ** (-jnp.arange(half, dtype=F32) / half)
    const = lambda a: pl.BlockSpec(a.shape, lambda i, j: (0,) * a.ndim)
    gq, gkv, gqn, gkn, gqr, gkr = gains
    consts = [inv_freq.reshape(half, 1), gq.reshape(1, -1), gkv.reshape(1, -1), gqn.reshape(-1, 1),
              gkn.reshape(1, -1), gqr.reshape(-1, 1), gkr.reshape(-1, 1)]
    weights = [w_uq, w_ukv]
    return pl.pallas_call(
        _mla_qkv_kernel,
        grid=(b, l // tm),
        in_specs=[pl.BlockSpec((1, tm, MLA_Q_LORA), lambda i, j: (i, j, cq_blk)),
                  pl.BlockSpec((1, tm, MLA_KV_LORA), lambda i, j: (i, j, ckv_blk)),
                  pl.BlockSpec((1, tm, LANES), lambda i, j: (i, j, kr_blk)),
                  pl.BlockSpec((1, 1, tm), lambda i, j: (i, 0, j))] + [const(a) for a in consts]
                 + [pl.BlockSpec(*_layer_block(w, layer)) for w in weights],
        out_specs=[pl.BlockSpec((1, hh, MLA_QK_PAD, tm), lambda i, j: (i, 0, 0, j)),
                   pl.BlockSpec((1, hh, tm, MLA_NOPE), lambda i, j: (i, 0, j, 0)),
                   pl.BlockSpec((1, tm, LANES), lambda i, j: (i, j, 0)),
                   pl.BlockSpec((1, hh, MLA_VL, tm), lambda i, j: (i, 0, 0, j))],
        out_shape=[jax.ShapeDtypeStruct((b, hh, MLA_QK_PAD, l), BF16),
                   jax.ShapeDtypeStruct((b, hh, l, MLA_NOPE), BF16),
                   jax.ShapeDtypeStruct((b, l, LANES), BF16),
                   jax.ShapeDtypeStruct((b, hh, MLA_VL, l), BF16)],
        scratch_shapes=[pltpu.VMEM((hh * (MLA_NOPE + MLA_ROPE), MLA_Q_LORA), BF16),
                        pltpu.VMEM((MLA_KV_LORA, hh * MLA_NOPE), BF16),
                        pltpu.VMEM((hh * MLA_V, MLA_KV_LORA), BF16)],
        compiler_params=_cparams(("arbitrary", "arbitrary")),
        name="mla_qkv",
    )(proj, proj, proj, positions.reshape(b, 1, l), *consts, *weights)


def _flash_kernel(qt_ref, kn_ref, kr_ref, vt_ref, o_ref, m_ref, acc_ref, *, tq, hp, ahead_full, ahead_diagonal):
    qi = pl.program_id(2)
    m_ref[...] = jnp.full(m_ref.shape, -jnp.inf, F32)
    acc_ref[...] = jnp.zeros(acc_ref.shape, F32)

    half = tq // 2
    lower = (lax.broadcasted_iota(jnp.int32, (half, half), 0)
             <= lax.broadcasted_iota(jnp.int32, (half, half), 1))

    def blocks(j, parts, diagonal):
        ahead = ahead_diagonal if diagonal else ahead_full
        base = pl.multiple_of(j * tq, tq)
        items = [(h, pl.ds(base + k0, nk), slice(q0, q0 + nq)) for k0, nk, q0, nq in parts for h in range(hp)]

        def scores(h, rows, cols):
            k = jnp.concatenate([kn_ref[0, h, rows, :], kr_ref[0, rows, :MLA_ROPE]], axis=-1)
            return jnp.dot(k, qt_ref[0, h, :, cols], preferred_element_type=F32)

        pending = [scores(*it) for it in items[:ahead]]
        for n, (h, rows, cols) in enumerate(items):
            if n + ahead < len(items):
                pending.append(scores(*items[n + ahead]))
            s = pending[n]
            if diagonal:
                square = jnp.where(lower, s[:, :half], jnp.finfo(F32).min)
                s = square if s.shape[1] == half else jnp.concatenate([square, s[:, half:]], axis=1)
            m = m_ref[h, :, cols]
            m_new = jnp.maximum(m, jnp.max(s, axis=0, keepdims=True))
            alpha = jnp.exp2(m - m_new)
            p = jnp.exp2(s - m_new)
            acc_ref[h, :, cols] = alpha * acc_ref[h, :, cols] + jnp.dot(
                vt_ref[0, h, :, rows], p.astype(BF16), preferred_element_type=F32)
            m_ref[h, :, cols] = m_new

    def body(j, carry):
        blocks(j, [(0, half, 0, tq), (half, half, 0, tq)], False)
        return carry

    lax.fori_loop(0, qi, body, 0)
    blocks(qi, [(0, half, 0, tq), (half, half, half, half)], True)
    for h in range(hp):
        o_ref[0, :, h * MLA_V:(h + 1) * MLA_V] = (
            acc_ref[h, :MLA_V, :] / acc_ref[h, MLA_V:MLA_V + 1, :]).T.astype(o_ref.dtype)


def _flash(qt, kn, kr, vt, *, tq, hp, ahead_full, ahead_diagonal):
    b, hh, _, l = qt.shape
    return pl.pallas_call(
        functools.partial(_flash_kernel, tq=tq, hp=hp, ahead_full=ahead_full, ahead_diagonal=ahead_diagonal),
        grid=(b, hh // hp, l // tq),
        in_specs=[pl.BlockSpec((1, hp, MLA_QK_PAD, tq), lambda i, h, j: (i, h, 0, j)),
                  pl.BlockSpec((1, hp, l, MLA_NOPE), lambda i, h, j: (i, h, 0, 0)),
                  pl.BlockSpec((1, l, LANES), lambda i, h, j: (i, 0, 0)),
                  pl.BlockSpec((1, hp, MLA_VL, l), lambda i, h, j: (i, h, 0, 0))],
        out_specs=pl.BlockSpec((1, tq, hp * MLA_V), lambda i, h, j: (i, j, h)),
        out_shape=jax.ShapeDtypeStruct((b, l, hh * MLA_V), BF16),
        scratch_shapes=[pltpu.VMEM((hp, 1, tq), F32), pltpu.VMEM((hp, MLA_VL, tq), F32)],
        compiler_params=_cparams(("parallel", "parallel", "parallel")),
        name="flash",
    )(qt, kn, kr, vt)


def _s5_layer(x, ln, w_in, lam_re, lam_im, log_step, b_re, b_im, c_re, c_im, d, w_glu,
              w_out, mem_kv, xq_norm, layer, j):
    b, l, dm = x.shape
    tm = 512
    proj, xn = _s5_in_proj(x, ln, w_in, j, tm=tm, col_chunk=512)
    xg = _s5_ut(xn, w_in, j)
    toep, wout, wst, dcol = _s5_asm(lam_re, lam_im, log_step, c_re, c_im, b_re, b_im, d)
    pw_re, pw_im = _s5_pow(lam_re, lam_im, log_step)
    col = lambda pw: pw.transpose(1, 2, 0)
    yt = _s5_mix(xg, toep, wst, wout, col(pw_re), col(pw_im), dcol, nb=b, gs=8)
    y = _glu(yt, w_glu, j, nb=b, col_chunk=256)
    return _merge(x, y, proj, BRANCH_WIDTH // XQ_WIDTH, 0, *mem_kv, xq_norm, w_out, layer, tm=tm, phased=True)


def _mla_layer(x, positions, ln, w_in, q_lora_norm, kv_lora_norm, w_uq, w_ukv, q_nope_norm, k_nope_norm,
               q_rope_norm, k_rope_norm, w_out, mem_kv, xq_norm, layer, j):
    b, l, dm = x.shape
    o1 = MLA_Q_LORA
    o2 = o1 + MLA_KV_LORA
    o3 = o2 + MLA_ROPE
    o4 = o3 + XQ_WIDTH
    segments = ((o4, BRANCH_WIDTH), (0, o1), (o3, XQ_WIDTH), (o1, MLA_KV_LORA), (o2, MLA_ROPE))
    wout = -(-(o4 + BRANCH_WIDTH) // 512) * 512
    proj = _mla_in_proj(x.reshape(b * l, dm), ln, w_in, j, segments, wout, tm=512, col_chunk=512)
    proj = proj.reshape(b, l, -1)
    gate_blk = 0
    cq_blk = BRANCH_WIDTH // MLA_Q_LORA
    xq_blk = (BRANCH_WIDTH + MLA_Q_LORA) // XQ_WIDTH
    ckv_blk = (BRANCH_WIDTH + MLA_Q_LORA + XQ_WIDTH) // MLA_KV_LORA
    kr_blk = (BRANCH_WIDTH + MLA_Q_LORA + XQ_WIDTH + MLA_KV_LORA) // LANES
    qt, kn, kr, vt = _mla_qkv(proj, cq_blk, ckv_blk, kr_blk, positions,
                              (q_lora_norm, kv_lora_norm, q_nope_norm, k_nope_norm, q_rope_norm, k_rope_norm),
                              w_uq, w_ukv, j, tm=512)
    attn = _flash(qt, kn, kr, vt, tq=512, hp=12, ahead_full=2, ahead_diagonal=4)
    return _merge(x, attn, proj, xq_blk, gate_blk, *mem_kv, xq_norm, w_out, layer, tm=512)


def kernel(x, mem, positions, ln_gain, w_out, mem_norm, w_mem_kv, xq_norm, xk_norm,
           s5_w_in, s5_lambda_re, s5_lambda_im, s5_log_step, s5_b_re, s5_b_im, s5_c_re, s5_c_im,
           s5_d, s5_w_glu, mla_w_in, mla_q_lora_norm, mla_kv_lora_norm, mla_w_uq, mla_w_ukv,
           mla_q_nope_norm, mla_k_nope_norm, mla_q_rope_norm, mla_k_rope_norm):
    depth = ln_gain.shape[0]
    mem_kv = _mem_kv(mem, mem_norm, w_mem_kv, xk_norm)
    for i in range(depth):
        j = i // 2
        if i % 2 == 0:
            x = _s5_layer(x, ln_gain[i], s5_w_in, s5_lambda_re[j], s5_lambda_im[j], s5_log_step[j],
                          s5_b_re[j], s5_b_im[j], s5_c_re[j], s5_c_im[j], s5_d[j], s5_w_glu,
                          w_out, mem_kv, xq_norm, i, j)
        else:
            x = _mla_layer(x, positions, ln_gain[i], mla_w_in, mla_q_lora_norm[j], mla_kv_lora_norm[j],
                           mla_w_uq, mla_w_ukv, mla_q_nope_norm[j], mla_k_nope_norm[j],
                           mla_q_rope_norm[j], mla_k_rope_norm[j],
                           w_out, mem_kv, xq_norm, i, j)
    return x
```
